```python
import numpy as np
import jax
import jax.numpy as jnp
from jax import lax

D_MODEL = 1024
BATCH = 4
SEQ = 4096
DEPTH = 2

GRID_W = 64
CTX_LEN = 256
N_EVEN = (DEPTH + 1) // 2
N_ODD = DEPTH // 2
CHUNK = 64
CONV_K = 4
CONV_PAD_L = 2
CONV_PAD_R = 1
EPS = 1e-6
M_HEADS = 4
M_DK = D_MODEL // (2 * M_HEADS)
M_DV = D_MODEL // M_HEADS
M_QK = 2 * M_HEADS * M_DK
M_V = M_HEADS * M_DV
LRU_W = D_MODEL
LRU_BLOCKS = 8
LRU_BW = LRU_W // LRU_BLOCKS
LRU_C = 8.0
G_HEADS = 4
G_DK = D_MODEL // (2 * G_HEADS)
G_DV = D_MODEL // G_HEADS
G_QK = 2 * G_HEADS * G_DK
G_V = G_HEADS * G_DV
G_RANK = 16
G_TAU = 16.0
S_HEADS = 16
S_P = D_MODEL // S_HEADS
S_INNER = S_HEADS * S_P
S_N = 128
S_GROUPS = 2
S_HPG = S_HEADS // S_GROUPS
S_XBC = S_INNER + 2 * S_GROUPS * S_N
FF_DENSE = 2816
N_EXPERTS = 8
TOP_K = 2
FF_EXPERT = 3584
MOE_BLOCK = 256
AB_SIZES = [M_QK, M_V, M_V, 4 * M_HEADS, LRU_W, LRU_W]
AB_IN = sum(AB_SIZES)
CD_SIZES = [G_QK, G_V, G_V, 2 * G_RANK, S_INNER, S_XBC, 2 * S_HEADS]
CD_IN = sum(CD_SIZES)

kernel_name = 'hybrid_mlstm_rglru_gla_ssd_moe_trunk'


def split_cols(a, sizes):
    return jnp.split(a, np.cumsum(sizes)[:-1].tolist(), axis=-1)


def rmsnorm(u, g):
    uf = u.astype(jnp.float32)
    uf = uf * lax.rsqrt(jnp.mean(uf * uf, axis=-1, keepdims=True) + EPS)
    return (uf * g.astype(jnp.float32)).astype(u.dtype)


def headnorm(u, g, n_heads):
    shp = u.shape
    uh = u.astype(jnp.float32).reshape(*shp[:-1], n_heads, shp[-1] // n_heads)
    uh = uh * lax.rsqrt(jnp.mean(uh * uh, axis=-1, keepdims=True) + EPS)
    return uh.reshape(shp) * g.astype(jnp.float32)


def adaln(u, g, shift, scale):
    return rmsnorm(u, g) * (1 + scale) + shift


def dwconv(u, w, b):
    y = lax.conv_general_dilated(u, w[:, None, :].astype(u.dtype), window_strides=(1,),
                                 padding=[(CONV_PAD_L, CONV_PAD_R)],
                                 dimension_numbers=('NWC', 'WIO', 'NWC'),
                                 feature_group_count=u.shape[-1])
    return y + b.astype(u.dtype)


def stream_conv(u, w, b, n_ctx, rows):
    uc, ul = u[:, :n_ctx], u[:, n_ctx:]
    bsz, s, ch = ul.shape
    yl = dwconv(ul.reshape(bsz * rows, GRID_W, ch), w, b).reshape(bsz, s, ch)
    return jnp.concatenate([dwconv(uc, w, b), yl], axis=1)


def flip_streams(a, n_ctx):
    return jnp.concatenate([jnp.flip(a[:, :n_ctx], axis=1), jnp.flip(a[:, n_ctx:], axis=1)], axis=1)


def to_chunks(a):
    bsz, t = a.shape[:2]
    return jnp.moveaxis(a.reshape(bsz, t // CHUNK, CHUNK, *a.shape[2:]), 1, 0)


def from_chunks(a):
    a = jnp.moveaxis(a, 0, 1)
    return a.reshape(a.shape[0], a.shape[1] * a.shape[2], *a.shape[3:])


def chunk_mask(n_trailing):
    m = jnp.tril(jnp.ones((CHUNK, CHUNK), dtype=bool))
    return m.reshape((1, CHUNK, CHUNK) + (1,) * n_trailing)


def mlstm_scan(q, k, v, ig, lf):
    bsz, _, nh, dk = q.shape
    dv = v.shape[-1]
    mask = chunk_mask(1)

    def step(carry, inp):
        cmat, nvec, m = carry
        qc, kc, vc, ic, fc = inp
        b = jnp.cumsum(fc, axis=1)
        g = b + m[:, None]
        dlog = jnp.where(mask, b[:, :, None] - b[:, None] + ic[:, None], -jnp.inf)
        m_t = jnp.maximum(g, dlog.max(axis=2))
        w_inter = jnp.exp(g - m_t)
        sc = jnp.einsum('bthd,bshd->btsh', qc, kc) * jnp.exp(dlog - m_t[:, :, None])
        num = w_inter[..., None] * jnp.einsum('bhvd,bthd->bthv', cmat, qc) + jnp.einsum('btsh,bshv->bthv', sc, vc)
        den = w_inter * jnp.einsum('bhd,bthd->bth', nvec, qc) + sc.sum(axis=2)
        h = num / jnp.maximum(jnp.abs(den), jnp.exp(-m_t))[..., None]
        bl = b[:, -1]
        a_s = bl[:, None] - b + ic
        m_new = jnp.maximum(bl + m, a_s.max(axis=1))
        decay = jnp.exp(bl + m - m_new)
        ws = jnp.exp(a_s - m_new[:, None])
        cmat = decay[..., None, None] * cmat + jnp.einsum('bsh,bshv,bshd->bhvd', ws, vc, kc)
        nvec = decay[..., None] * nvec + jnp.einsum('bsh,bshd->bhd', ws, kc)
        return (cmat, nvec, m_new), h

    init = (jnp.zeros((bsz, nh, dv, dk), jnp.float32), jnp.zeros((bsz, nh, dk), jnp.float32),
            jnp.zeros((bsz, nh), jnp.float32))
    _, hs = lax.scan(step, init, (to_chunks(q), to_chunks(k), to_chunks(v), to_chunks(ig), to_chunks(lf)))
    return from_chunks(hs)


def _lin_combine(left, right):
    a1, b1 = left
    a2, b2 = right
    return a1 * a2, a2 * b1 + b2


def rglru(u, wa, ba, wx, bx, lam):
    bsz, t, wdt = u.shape
    ub = u.reshape(bsz, t, LRU_BLOCKS, LRU_BW)
    r = jax.nn.sigmoid(jnp.einsum('btnc,ncd->btnd', ub, wa.astype(jnp.float32)).reshape(bsz, t, wdt) + ba)
    i = jax.nn.sigmoid(jnp.einsum('btnc,ncd->btnd', ub, wx.astype(jnp.float32)).reshape(bsz, t, wdt) + bx)
    log_a = -LRU_C * r * jax.nn.softplus(-lam.astype(jnp.float32))
    a = jnp.exp(log_a)
    b = jnp.sqrt(-jnp.expm1(2.0 * log_a)) * (i * u)
    _, hs = lax.associative_scan(_lin_combine, (a, b), axis=1)
    return hs


def gla_scan(q, k, v, lg):
    bsz, _, nh, dk = q.shape
    dv = v.shape[-1]
    mask = chunk_mask(2)

    def step(state, inp):
        qc, kc, vc, gc = inp
        b = jnp.cumsum(gc, axis=1)
        inter = jnp.einsum('bthd,bhdv->bthv', qc * jnp.exp(b), state)
        decay = jnp.exp(jnp.where(mask, b[:, :, None] - b[:, None], -jnp.inf))
        att = jnp.einsum('bthd,bshd,btshd->btsh', qc, kc, decay)
        out = inter + jnp.einsum('btsh,bshv->bthv', att, vc)
        bl = b[:, -1]
        state = jnp.exp(bl)[..., None] * state + jnp.einsum('bshd,bshv->bhdv', kc * jnp.exp(bl[:, None] - b), vc)
        return state, out

    init = jnp.zeros((bsz, nh, dk, dv), jnp.float32)
    _, outs = lax.scan(step, init, (to_chunks(q), to_chunks(k), to_chunks(v), to_chunks(lg)))
    return from_chunks(outs)


def ssd_scan(x, dt, la, bm, cm):
    bsz = x.shape[0]
    mask = chunk_mask(2)

    def step(state, inp):
        xc, dtc, lac, bc, cc = inp
        b = jnp.cumsum(lac, axis=1)
        cb = jnp.einsum('btgn,bsgn->btsg', cc, bc)
        decay = jnp.exp(jnp.where(mask, b[:, :, None] - b[:, None], -jnp.inf))
        intra = jnp.einsum('btsg,btsge,bsgep->btgep', cb, decay, xc * dtc[..., None])
        inter = jnp.einsum('btgn,bgepn->btgep', cc, state) * jnp.exp(b)[..., None]
        bl = b[:, -1]
        w = jnp.exp(bl[:, None] - b) * dtc
        state = jnp.exp(bl)[..., None, None] * state + jnp.einsum('bsge,bsgep,bsgn->bgepn', w, xc, bc)
        return state, intra + inter

    init = jnp.zeros((bsz, S_GROUPS, S_HPG, S_P, S_N), jnp.float32)
    _, ys = lax.scan(step, init, (to_chunks(x), to_chunks(dt), to_chunks(la), to_chunks(bm), to_chunks(cm)))
    return from_chunks(ys)


def mix_ab(h, n_ctx, rows, in_w, m_conv_w, m_conv_b, m_gate_b, m_norm_w, l_conv_w, l_conv_b,
           l_wa, l_ba, l_wx, l_bx, l_lam, out_w):
    bsz, t, _ = h.shape
    fl = lambda a: flip_streams(a, n_ctx)
    qk, v, o, gt, lx, lg = split_cols(h @ in_w, AB_SIZES)
    qk = jax.nn.silu(stream_conv(qk, m_conv_w, m_conv_b, n_ctx, rows))
    q, k = jnp.split(qk, 2, axis=-1)
    q = q.reshape(bsz, t, M_HEADS, M_DK).astype(jnp.float32)
    k = k.reshape(bsz, t, M_HEADS, M_DK).astype(jnp.float32) * (M_DK ** -0.5)
    v = v.reshape(bsz, t, M_HEADS, M_DV).astype(jnp.float32)
    gt = gt.astype(jnp.float32).reshape(bsz, t, 4, M_HEADS) + m_gate_b.astype(jnp.float32)
    ig_f, ig_b = gt[:, :, 0], gt[:, :, 1]
    lf_f, lf_b = jax.nn.log_sigmoid(gt[:, :, 2]), jax.nn.log_sigmoid(gt[:, :, 3])
    hm = mlstm_scan(q, k, v, ig_f, lf_f) + fl(mlstm_scan(fl(q), fl(k), fl(v), fl(ig_b), fl(lf_b)))
    hm = headnorm(hm.reshape(bsz, t, M_V), m_norm_w, M_HEADS) * jax.nn.sigmoid(o.astype(jnp.float32))
    xl = stream_conv(lx, l_conv_w, l_conv_b, n_ctx, rows).astype(jnp.float32)
    hl = (rglru(xl, l_wa[0], l_ba[0], l_wx[0], l_bx[0], l_lam[0])
          + fl(rglru(fl(xl), l_wa[1], l_ba[1], l_wx[1], l_bx[1], l_lam[1])))
    hl = hl * jax.nn.gelu(lg.astype(jnp.float32))
    return jnp.concatenate([hm, hl], axis=-1).astype(h.dtype) @ out_w


def mix_cd(h, n_ctx, rows, in_w, g_alpha_w, g_alpha_b, g_norm_w, s_conv_w, s_conv_b,
           s_dt_bias, s_A_log, s_D, s_norm_w, out_w):
    bsz, t, _ = h.shape
    fl = lambda a: flip_streams(a, n_ctx)
    gqk, gv, gr, ga, z, xbc, dt = split_cols(h @ in_w, CD_SIZES)
    gq, gk = jnp.split(gqk, 2, axis=-1)
    gq = gq.reshape(bsz, t, G_HEADS, G_DK).astype(jnp.float32) * (G_DK ** -0.5)
    gk = gk.reshape(bsz, t, G_HEADS, G_DK).astype(jnp.float32)
    gv = gv.reshape(bsz, t, G_HEADS, G_DV).astype(jnp.float32)
    ga = ga.astype(jnp.float32).reshape(bsz, t, 2, G_RANK)

    def loggate(d):
        pre = ga[:, :, d] @ g_alpha_w[d].astype(jnp.float32) + g_alpha_b[d].astype(jnp.float32)
        return (jax.nn.log_sigmoid(pre) / G_TAU).reshape(bsz, t, G_HEADS, G_DK)

    og = gla_scan(gq, gk, gv, loggate(0)) + fl(gla_scan(fl(gq), fl(gk), fl(gv), fl(loggate(1))))
    og = headnorm(og.reshape(bsz, t, G_V), g_norm_w, G_HEADS) * jax.nn.silu(gr.astype(jnp.float32))
    xbc = jax.nn.silu(stream_conv(xbc, s_conv_w, s_conv_b, n_ctx, rows)).astype(jnp.float32)
    sx, sb, sc = split_cols(xbc, [S_INNER, S_GROUPS * S_N, S_GROUPS * S_N])
    sx = sx.reshape(bsz, t, S_GROUPS, S_HPG, S_P)
    sb = sb.reshape(bsz, t, S_GROUPS, S_N)
    sc = sc.reshape(bsz, t, S_GROUPS, S_N)
    dt = jax.nn.softplus(dt.astype(jnp.float32).reshape(bsz, t, 2, S_HEADS) + s_dt_bias.astype(jnp.float32))
    la = dt * (-jnp.exp(s_A_log.astype(jnp.float32)))
    grp = lambda a, d: a[:, :, d].reshape(bsz, t, S_GROUPS, S_HPG)
    ys = (ssd_scan(sx, grp(dt, 0), grp(la, 0), sb, sc)
          + fl(ssd_scan(fl(sx), fl(grp(dt, 1)), fl(grp(la, 1)), fl(sb), fl(sc))))
    ys = ys + s_D.astype(jnp.float32).reshape(S_GROUPS, S_HPG)[..., None] * sx
    ys = rmsnorm(ys.reshape(bsz, t, S_INNER) * jax.nn.silu(z.astype(jnp.float32)), s_norm_w)
    return jnp.concatenate([og, ys], axis=-1).astype(h.dtype) @ out_w


def swiglu(h, w1, w2):
    g, u = jnp.split(h @ w1, 2, axis=-1)
    return (jax.nn.silu(g) * u) @ w2


def moe_swiglu(h, router_w, router_b, w1, w2):
    bsz, t, dm = h.shape
    tok = h.reshape(-1, dm)
    n = tok.shape[0]
    logits = (tok @ router_w).astype(jnp.float32) + router_b.astype(jnp.float32)
    top_logit, top_idx = lax.top_k(logits, TOP_K)
    gates = jax.nn.softmax(top_logit, axis=-1).astype(h.dtype)
    flat_e = top_idx.reshape(-1)
    n_assign = n * TOP_K
    flat_tok = jnp.arange(n_assign, dtype=jnp.int32) // TOP_K
    order = jnp.argsort(flat_e)
    sorted_e = flat_e[order]
    counts = jnp.bincount(flat_e, length=N_EXPERTS)
    padded = (counts + MOE_BLOCK - 1) // MOE_BLOCK * MOE_BLOCK
    starts = jnp.cumsum(counts) - counts
    ends_p = jnp.cumsum(padded)
    pstarts = ends_p - padded
    slot_sorted = (pstarts[sorted_e] + jnp.arange(n_assign, dtype=jnp.int32) - starts[sorted_e]).astype(jnp.int32)
    slot = jnp.zeros((n_assign,), jnp.int32).at[order].set(slot_sorted)
    n_blocks = -(-n_assign // MOE_BLOCK) + N_EXPERTS
    slot_tok = jnp.full((n_blocks * MOE_BLOCK,), n, jnp.int32).at[slot].set(flat_tok)
    block_e = jnp.minimum(jnp.searchsorted(ends_p, jnp.arange(n_blocks, dtype=jnp.int32) * MOE_BLOCK, side='right'),
                          N_EXPERTS - 1)
    tok_pad = jnp.concatenate([tok, jnp.zeros((1, dm), tok.dtype)], axis=0)
    xs = tok_pad[slot_tok].reshape(n_blocks, MOE_BLOCK, dm)

    def expert_block(args):
        xb, e = args
        return swiglu(xb, w1[e], w2[e])

    ys = lax.map(expert_block, (xs, block_e)).reshape(-1, dm)
    y = (ys[slot].reshape(n, TOP_K, dm) * gates[..., None]).sum(axis=1)
    return y.reshape(bsz, t, dm)


def setup_inputs(seed: int = 0) -> dict:
    key = jax.random.key(seed)
    ks = iter(jax.random.split(key, 48))

    def nrm(shape, scale):
        return jax.random.normal(next(ks), shape, jnp.float32) * scale

    def unif(shape, lo, hi):
        return jax.random.uniform(next(ks), shape, jnp.float32, lo, hi)

    D = D_MODEL
    x = nrm((BATCH, SEQ, D), 1.0)
    c = nrm((BATCH, D), 1.0)
    ctx = nrm((BATCH, CTX_LEN, D), 1.0)
    c_ctx = nrm((D,), 1.0)
    mod_w = nrm((DEPTH, D, 6 * D), 0.5 * D ** -0.5)
    mod_b = nrm((DEPTH, 6 * D), 0.01)
    norm_g = 1.0 + nrm((DEPTH, 4, D), 0.02)
    ab_in_w = nrm((N_EVEN, D, AB_IN), D ** -0.5)
    m_conv_w = nrm((N_EVEN, CONV_K, M_QK), 0.5)
    m_conv_b = nrm((N_EVEN, M_QK), 0.01)
    m_gate_b = jnp.concatenate([nrm((N_EVEN, 2, M_HEADS), 0.1), unif((N_EVEN, 2, M_HEADS), 3.0, 6.0)], axis=1)
    m_norm_w = 1.0 + nrm((N_EVEN, M_V), 0.02)
    l_conv_w = nrm((N_EVEN, CONV_K, LRU_W), 0.5)
    l_conv_b = nrm((N_EVEN, LRU_W), 0.01)
    l_wa = nrm((N_EVEN, 2, LRU_BLOCKS, LRU_BW, LRU_BW), LRU_BW ** -0.5)
    l_ba = nrm((N_EVEN, 2, LRU_W), 0.01)
    l_wx = nrm((N_EVEN, 2, LRU_BLOCKS, LRU_BW, LRU_BW), LRU_BW ** -0.5)
    l_bx = nrm((N_EVEN, 2, LRU_W), 0.01)
    a_c = unif((N_EVEN, 2, LRU_W), 0.9, 0.999) ** (1.0 / LRU_C)
    l_lam = jnp.log(a_c) - jnp.log1p(-a_c)
    ab_out_w = nrm((N_EVEN, M_V + LRU_W, D), (M_V + LRU_W) ** -0.5)
    ffn_w1 = nrm((N_EVEN, D, 2 * FF_DENSE), D ** -0.5)
    ffn_w2 = nrm((N_EVEN, FF_DENSE, D), FF_DENSE ** -0.5)
    cd_in_w = nrm((N_ODD, D, CD_IN), D ** -0.5)
    g_alpha_w = nrm((N_ODD, 2, G_RANK, G_HEADS * G_DK), G_RANK ** -0.5)
    g_alpha_b = nrm((N_ODD, 2, G_HEADS * G_DK), 0.1)
    g_norm_w = 1.0 + nrm((N_ODD, G_V), 0.02)
    s_conv_w = nrm((N_ODD, CONV_K, S_XBC), 0.5)
    s_conv_b = nrm((N_ODD, S_XBC), 0.01)
    dt0 = jnp.exp(unif((N_ODD, 2, S_HEADS), float(np.log(1e-3)), float(np.log(1e-1))))
    s_dt_bias = dt0 + jnp.log(-jnp.expm1(-dt0))
    s_A_log = jnp.log(unif((N_ODD, 2, S_HEADS), 1.0, 16.0))
    s_D = 1.0 + nrm((N_ODD, S_HEADS), 0.1)
    s_norm_w = 1.0 + nrm((N_ODD, S_INNER), 0.02)
    cd_out_w = nrm((N_ODD, G_V + S_INNER, D), (G_V + S_INNER) ** -0.5)
    router_w = nrm((N_ODD, D, N_EXPERTS), D ** -0.5)
    router_b = nrm((N_ODD, N_EXPERTS), 0.01)
    moe_w1 = nrm((N_ODD, N_EXPERTS, D, 2 * FF_EXPERT), D ** -0.5)
    moe_w2 = nrm((N_ODD, N_EXPERTS, FF_EXPERT, D), FF_EXPERT ** -0.5)
    return {'x': x, 'c': c, 'ctx': ctx, 'c_ctx': c_ctx, 'mod_w': mod_w, 'mod_b': mod_b, 'norm_g': norm_g,
            'ab_in_w': ab_in_w, 'm_conv_w': m_conv_w, 'm_conv_b': m_conv_b, 'm_gate_b': m_gate_b,
            'm_norm_w': m_norm_w, 'l_conv_w': l_conv_w, 'l_conv_b': l_conv_b, 'l_wa': l_wa, 'l_ba': l_ba,
            'l_wx': l_wx, 'l_bx': l_bx, 'l_lam': l_lam, 'ab_out_w': ab_out_w, 'ffn_w1': ffn_w1, 'ffn_w2': ffn_w2,
            'cd_in_w': cd_in_w, 'g_alpha_w': g_alpha_w, 'g_alpha_b': g_alpha_b, 'g_norm_w': g_norm_w,
            's_conv_w': s_conv_w, 's_conv_b': s_conv_b, 's_dt_bias': s_dt_bias, 's_A_log': s_A_log, 's_D': s_D,
            's_norm_w': s_norm_w, 'cd_out_w': cd_out_w, 'router_w': router_w, 'router_b': router_b,
            'moe_w1': moe_w1, 'moe_w2': moe_w2}


def reference(x, c, ctx, c_ctx, mod_w, mod_b, norm_g,
              ab_in_w, m_conv_w, m_conv_b, m_gate_b, m_norm_w, l_conv_w, l_conv_b, l_wa, l_ba,
              l_wx, l_bx, l_lam, ab_out_w, ffn_w1, ffn_w2,
              cd_in_w, g_alpha_w, g_alpha_b, g_norm_w, s_conv_w, s_conv_b, s_dt_bias, s_A_log, s_D,
              s_norm_w, cd_out_w, router_w, router_b, moe_w1, moe_w2):
    bsz, seq, _ = x.shape
    n_ctx = ctx.shape[1]
    rows = seq // GRID_W
    xl, xc = x, ctx
    for layer in range(DEPTH):
        j = layer // 2
        last = layer == DEPTH - 1
        mlm = (jax.nn.silu(c) @ mod_w[layer] + mod_b[layer]).reshape(bsz, 6, 1, D_MODEL)
        ml = [mlm[:, i] for i in range(6)]
        mc = (jax.nn.silu(c_ctx) @ mod_w[layer] + mod_b[layer]).reshape(6, D_MODEL)
        g = norm_g[layer]
        h = jnp.concatenate([adaln(xc, g[0], mc[0], mc[1]), adaln(xl, g[0], ml[0], ml[1])], axis=1)
        if layer % 2 == 0:
            y = mix_ab(h, n_ctx, rows, ab_in_w[j], m_conv_w[j], m_conv_b[j], m_gate_b[j], m_norm_w[j],
                       l_conv_w[j], l_conv_b[j], l_wa[j], l_ba[j], l_wx[j], l_bx[j], l_lam[j], ab_out_w[j])
        else:
            y = mix_cd(h, n_ctx, rows, cd_in_w[j], g_alpha_w[j], g_alpha_b[j], g_norm_w[j], s_conv_w[j],
                       s_conv_b[j], s_dt_bias[j], s_A_log[j], s_D[j], s_norm_w[j], cd_out_w[j])
        y = rmsnorm(y, g[1])
        xl = xl + ml[2] * y[:, n_ctx:]
        hl = adaln(xl, g[2], ml[3], ml[4])
        if last:
            h2 = hl
        else:
            xc = xc + mc[2] * y[:, :n_ctx]
            h2 = jnp.concatenate([adaln(xc, g[2], mc[3], mc[4]), hl], axis=1)
        if layer % 2 == 0:
            f = swiglu(h2, ffn_w1[j], ffn_w2[j])
        else:
            f = moe_swiglu(h2, router_w[j], router_b[j], moe_w1[j], moe_w2[j])
        f = rmsnorm(f, g[3])
        xl = xl + ml[5] * f[:, f.shape[1] - seq:]
        if not last:
            xc = xc + mc[5] * f[:, :n_ctx]
    return xl
```

```python
import functools

import jax
import jax.numpy as jnp
from jax import lax
from jax.experimental import pallas as pl
from jax.experimental.pallas import tpu as pltpu

F32, BF16, I32 = jnp.float32, jnp.bfloat16, jnp.int32

D_MODEL = 1024
GRID_W = 64
EPS = 1e-6
M_HEADS, M_DK, M_DV = 4, 128, 256
LRU_BLOCKS, LRU_BW, LRU_C = 8, 128, 8.0
G_HEADS, G_DK, G_DV, G_RANK, G_TAU = 4, 128, 256, 16, 16.0
S_HEADS, S_P, S_N, S_GROUPS, S_HPG = 16, 64, 128, 2, 8
FF_DENSE = 2816
N_EXPERTS, TOP_K, FF_EXPERT = 8, 2, 3584

LANES = 128
VMEM_LIMIT = 56 * 1024 * 1024
ROW_TILE = 256
M_CHUNK = 256
G_CHUNK, G_SUB = 64, 16
S_CHUNK = 64
LRU_TILE = 256
FF_STEP = 256
MOE_ROWS = 512
MOE_FF = 512


def _cparams(*sem):
    return pltpu.CompilerParams(dimension_semantics=sem, vmem_limit_bytes=VMEM_LIMIT)


def _silu(x):
    return x * jax.nn.sigmoid(x)


def _softplus(x):
    return jnp.maximum(x, 0.0) + jnp.log1p(jnp.exp(-jnp.abs(x)))


def _log_sigmoid(x):
    return jnp.minimum(x, 0.0) - jnp.log1p(jnp.exp(-jnp.abs(x)))


def _dot(a, b):
    return jnp.dot(a, b, preferred_element_type=F32)


def _dot_nt(a, b):
    return lax.dot_general(a, b, (((1,), (1,)), ((), ())), preferred_element_type=F32)


def _dot_tn(a, b):
    return lax.dot_general(a, b, (((0,), (0,)), ((), ())), preferred_element_type=F32)


def _split3(f):
    f1 = f.astype(BF16)
    r = f - f1.astype(F32)
    f2 = r.astype(BF16)
    f3 = (r - f2.astype(F32)).astype(BF16)
    return f1, f2, f3


def _sel_cols(mask01, f):
    p1, p2, p3 = _split3(f)
    return _dot(mask01, p1) + _dot(mask01, p2) + _dot(mask01, p3)


def _sel_rows(f, mask01):
    p1, p2, p3 = _split3(f)
    return _dot_nt(p1, mask01) + _dot_nt(p2, mask01) + _dot_nt(p3, mask01)


def _expand(f, sel01):
    p1, p2, p3 = _split3(f)
    return _dot(p1, sel01) + _dot(p2, sel01) + _dot(p3, sel01)


def _rms(u, g):
    return u * lax.rsqrt(jnp.mean(u * u, axis=-1, keepdims=True) + EPS) * g


def _row_ids(tile_idx, tm):
    return tile_idx * tm + lax.broadcasted_iota(I32, (tm, 1), 0)


def _mod(modl_ref, modc_ref, idx, is_ctx):
    return jnp.where(is_ctx, modc_ref[idx:idx + 1, :], modl_ref[0, idx:idx + 1, :])


def _adaln(x, g, modl_ref, modc_ref, shift_idx, is_ctx):
    shift = _mod(modl_ref, modc_ref, shift_idx, is_ctx)
    scale = _mod(modl_ref, modc_ref, shift_idx + 1, is_ctx)
    return _rms(x, g) * (1.0 + scale) + shift


def _conv_masks(r, ctx_len):
    in_lat = r >= ctx_len
    pos = jnp.where(in_lat, jnp.bitwise_and(r - ctx_len, GRID_W - 1), r)
    seg = jnp.where(in_lat, GRID_W, ctx_len)
    return pos >= 2, pos >= 1, pos <= seg - 2


def _dwconv(y, cw, cb, masks, tm):
    m2, m1, p1 = masks
    ym2 = jnp.where(m2, pltpu.roll(y, 2, 0), 0.0)
    ym1 = jnp.where(m1, pltpu.roll(y, 1, 0), 0.0)
    yp1 = jnp.where(p1, pltpu.roll(y, tm - 1, 0), 0.0)
    return cb + cw[0:1, :] * ym2 + cw[1:2, :] * ym1 + cw[2:3, :] * y + cw[3:4, :] * yp1


def _mod_kernel(c_ref, w_ref, b_ref, o_ref):
    a = _silu(c_ref[...])
    o_ref[0] = jnp.dot(a, w_ref[0], preferred_element_type=F32, precision=lax.Precision.HIGHEST) + b_ref[0]


def _modulation(c_all, mod_w, mod_b):
    depth, d, n6 = mod_w.shape
    rows = c_all.shape[0]
    tn = 1024
    return pl.pallas_call(
        _mod_kernel,
        grid=(depth, n6 // tn),
        in_specs=[pl.BlockSpec((rows, d), lambda l, j: (0, 0)),
                  pl.BlockSpec((1, d, tn), lambda l, j: (l, 0, j)),
                  pl.BlockSpec((1, 1, tn), lambda l, j: (l, 0, j))],
        out_specs=pl.BlockSpec((1, rows, tn), lambda l, j: (l, 0, j)),
        out_shape=jax.ShapeDtypeStruct((depth, rows, n6), F32),
        compiler_params=_cparams("arbitrary", "arbitrary"),
        name="modulation",
    )(c_all, mod_w, mod_b.reshape(depth, 1, n6))


def _in_ab_kernel(tm, ctx_len, x_ref, modl_ref, modc_ref, g_ref, w_ref, mcw_ref, mcb_ref, lcw_ref, lcb_ref,
                  gb_ref, q_ref, k_ref, v_ref, og_ref, lx_ref, glg_ref, gates_ref):
    r = _row_ids(pl.program_id(1), tm)
    is_ctx = r < ctx_len
    h = _adaln(x_ref[...], g_ref[...], modl_ref, modc_ref, 0, is_ctx).astype(BF16)
    masks = _conv_masks(r, ctx_len)
    q = _dwconv(_dot(h, w_ref[:, 0:512]), mcw_ref[:, 0:512], mcb_ref[:, 0:512], masks, tm)
    q_ref[...] = _silu(q).astype(BF16)
    k = _dwconv(_dot(h, w_ref[:, 512:1024]), mcw_ref[:, 512:1024], mcb_ref[:, 512:1024], masks, tm)
    k_ref[...] = (_silu(k) * (M_DK ** -0.5)).astype(BF16)
    v_ref[...] = _dot(h, w_ref[:, 1024:2048]).astype(BF16)
    og_ref[...] = jax.nn.sigmoid(_dot(h, w_ref[:, 2048:3072])).astype(BF16)
    lx = _dwconv(_dot(h, w_ref[:, 3072:4096]), lcw_ref[...], lcb_ref[...], masks, tm)
    lx_ref[...] = lx
    glg_ref[...] = jax.nn.gelu(_dot(h, w_ref[:, 4096:5120])).astype(BF16)
    gt = _dot(h, w_ref[:, 5120:5248]) + gb_ref[...]
    lane = lax.broadcasted_iota(I32, gt.shape, 1)
    gates_ref[...] = jnp.where(lane >= 2 * M_HEADS, _log_sigmoid(gt), gt)


def _resident(shape):
    nd = len(shape)
    return pl.BlockSpec(shape, lambda *_: (0,) * nd)


def _in_ab(x2, modl, modc, g, w, mcw, mcb, lcw, lcb, gb, bsz, t, ctx_len):
    tm = ROW_TILE
    nt = t // tm
    n = bsz * t
    row = lambda c: pl.BlockSpec((tm, c), lambda b, i: (b * nt + i, 0))
    outs = [(512, BF16), (512, BF16), (1024, BF16), (1024, BF16), (1024, F32), (1024, BF16), (LANES, F32)]
    return pl.pallas_call(
        functools.partial(_in_ab_kernel, tm, ctx_len),
        grid=(bsz, nt),
        in_specs=[row(D_MODEL),
                  pl.BlockSpec((1, 8, D_MODEL), lambda b, i: (b, 0, 0)),
                  _resident((8, D_MODEL)), _resident((1, D_MODEL)), _resident(w.shape),
                  _resident(mcw.shape), _resident(mcb.shape), _resident(lcw.shape), _resident(lcb.shape),
                  _resident(gb.shape)],
        out_specs=[row(c) for c, _ in outs],
        out_shape=[jax.ShapeDtypeStruct((n, c), dt) for c, dt in outs],
        compiler_params=_cparams("arbitrary", "arbitrary"),
        name="in_proj_ab",
    )(x2, modl, modc, g, w, mcw, mcb, lcw, lcb, gb)


def _chunk_order(reverse, n_ctx_chunks, n_chunks):
    if not reverse:
        return lambda c: c
    return lambda c: jnp.where(c < n_ctx_chunks, n_ctx_chunks - 1 - c, n_chunks - 1 - (c - n_ctx_chunks))


def _tri_mask(L, reverse):
    row = lax.broadcasted_iota(I32, (L, L), 0)
    col = lax.broadcasted_iota(I32, (L, L), 1)
    return (col >= row) if reverse else (col <= row)


def _mlstm_kernel(L, reverse, final, *refs):
    if final:
        q_ref, k_ref, v_ref, gc_ref, gr_ref, hf_ref, og_ref, nw_ref, out_ref, c_scr, m_scr = refs
    else:
        q_ref, k_ref, v_ref, gc_ref, gr_ref, out_ref, c_scr, m_scr = refs
    d = 1 if reverse else 0

    @pl.when(pl.program_id(1) == 0)
    def _():
        c_scr[...] = jnp.zeros_like(c_scr)
        m_scr[...] = jnp.zeros_like(m_scr)

    mask = _tri_mask(L, reverse)
    mask01 = mask.astype(BF16)
    gc = gc_ref[...]
    gr = gr_ref[0]
    bcol = _sel_cols(mask01, gc)
    brow = _sel_rows(gr, mask01)
    last = 0 if reverse else L - 1
    ones_col = (lax.broadcasted_iota(I32, (L, LANES), 1) == 0).astype(BF16)
    for h in range(M_HEADS):
        fo, io = 2 * M_HEADS + M_HEADS * d + h, M_HEADS * d + h
        bc, br = bcol[:, fo:fo + 1], brow[fo:fo + 1, :]
        ic, ir = gc[:, io:io + 1], gr[io:io + 1, :]
        m = m_scr[h:h + 1, 0:1]
        g = bc + m
        dlog = jnp.where(mask, bc - br + ir, -jnp.inf)
        mt = jnp.maximum(g, jnp.max(dlog, axis=1, keepdims=True))
        w_inter = jnp.exp(g - mt)
        p = jnp.exp(dlog - mt)
        qh = q_ref[:, h * M_DK:(h + 1) * M_DK]
        kh = k_ref[:, h * M_DK:(h + 1) * M_DK]
        vx = jnp.concatenate([v_ref[:, h * M_DV:(h + 1) * M_DV], ones_col], axis=1)
        sc = (_dot_nt(qh, kh) * p).astype(BF16)
        cx = c_scr[h]
        numx = w_inter * _dot(qh, cx.astype(BF16)) + _dot(sc, vx)
        den = numx[:, M_DV:M_DV + 1]
        hh = numx[:, 0:M_DV] * (1.0 / jnp.maximum(jnp.abs(den), jnp.exp(-mt)))
        bl = bc[last:last + 1, :]
        a_s = bl - bc + ic
        m_new = jnp.maximum(bl + m, jnp.max(a_s, axis=0, keepdims=True))
        decay = jnp.exp(bl + m - m_new)
        ws = jnp.exp(a_s - m_new)
        kw = (kh.astype(F32) * ws).astype(BF16)
        c_scr[h] = decay * cx + _dot_tn(kw, vx)
        m_scr[h:h + 1, 0:1] = m_new
        sl = slice(h * M_DV, (h + 1) * M_DV)
        if final:
            hs = hh + hf_ref[:, sl]
            out_ref[:, sl] = (_rms(hs, nw_ref[:, sl]) * og_ref[:, sl].astype(F32)).astype(BF16)
        else:
            out_ref[:, sl] = hh


def _mlstm(q, k, v, gcol, grow, bsz, t, ctx_len, reverse, final_args=None):
    L = M_CHUNK
    nc, ncc = t // L, ctx_len // L
    order = _chunk_order(reverse, ncc, nc)
    n = bsz * t
    row = lambda c: pl.BlockSpec((L, c), lambda b, i: (b * nc + order(i), 0))
    in_specs = [row(512), row(512), row(1024), row(LANES),
                pl.BlockSpec((1, 16, L), lambda b, i: (b * nc + order(i), 0, 0))]
    args = [q, k, v, gcol, grow]
    final = final_args is not None
    if final:
        in_specs += [row(1024), row(1024), _resident((1, 1024))]
        args += list(final_args)
    return pl.pallas_call(
        functools.partial(_mlstm_kernel, L, reverse, final),
        grid=(bsz, nc),
        in_specs=in_specs,
        out_specs=row(1024),
        out_shape=jax.ShapeDtypeStruct((n, 1024), BF16 if final else F32),
        scratch_shapes=[pltpu.VMEM((M_HEADS, M_DK, M_DV + LANES), F32), pltpu.VMEM((8, LANES), F32)],
        compiler_params=_cparams("arbitrary", "arbitrary"),
        name="mlstm_bwd" if reverse else "mlstm_fwd",
    )(*args)


def _lru_kernel(L, reverse, final, *refs):
    if final:
        u_ref, wa_ref, wx_ref, ba_ref, bx_ref, lam_ref, hf_ref, glg_ref, out_ref, a_scr, b_scr, h_scr = refs
    else:
        u_ref, wa_ref, wx_ref, ba_ref, bx_ref, lam_ref, out_ref, a_scr, b_scr, h_scr = refs

    @pl.when(pl.program_id(1) == 0)
    def _():
        h_scr[...] = jnp.zeros_like(h_scr)

    u = u_ref[...]
    ub = u.astype(BF16)
    sp = _softplus(-lam_ref[...])
    for n in range(LRU_BLOCKS):
        sl = slice(n * LRU_BW, (n + 1) * LRU_BW)
        rg = jax.nn.sigmoid(_dot(ub[:, sl], wa_ref[n]) + ba_ref[:, sl])
        ig = jax.nn.sigmoid(_dot(ub[:, sl], wx_ref[n]) + bx_ref[:, sl])
        log_a = -LRU_C * rg * sp[:, sl]
        a = jnp.exp(log_a)
        a_scr[:, sl] = a
        b_scr[:, sl] = jnp.sqrt(-jnp.tanh(log_a) * (a * a + 1.0)) * (ig * u[:, sl])
    a = a_scr[...]
    b = b_scr[...]
    sub = jnp.bitwise_and(lax.broadcasted_iota(I32, (L, 1), 0), 7)
    for s in (1, 2, 4):
        if reverse:
            a_sh, b_sh, keep = pltpu.roll(a, L - s, 0), pltpu.roll(b, L - s, 0), sub < 8 - s
        else:
            a_sh, b_sh, keep = pltpu.roll(a, s, 0), pltpu.roll(b, s, 0), sub >= s
        b = jnp.where(keep, a * b_sh + b, b)
        a = jnp.where(keep, a * a_sh, a)
    a_scr[...] = a
    b_scr[...] = b
    ng = L // 8
    edge = 0 if reverse else 7

    def body(j, hc):
        gi = (ng - 1 - j) if reverse else j
        rows = pl.ds(pl.multiple_of(gi * 8, 8), 8)
        hg = b_scr[rows, :] + a_scr[rows, :] * hc
        b_scr[rows, :] = hg
        return hg[edge:edge + 1, :]

    h_scr[...] = lax.fori_loop(0, ng, body, h_scr[...])
    if final:
        out_ref[...] = ((b_scr[...] + hf_ref[...]) * glg_ref[...].astype(F32)).astype(BF16)
    else:
        out_ref[...] = b_scr[...]


def _lru(u, wa, wx, ba, bx, lam, bsz, t, ctx_len, reverse, final_args=None):
    L = LRU_TILE
    nc, ncc = t // L, ctx_len // L
    order = _chunk_order(reverse, ncc, nc)
    n = bsz * t
    row = lambda c: pl.BlockSpec((L, c), lambda b, i: (b * nc + order(i), 0))
    in_specs = [row(1024), _resident(wa.shape), _resident(wx.shape), _resident((1, 1024)),
                _resident((1, 1024)), _resident((1, 1024))]
    args = [u, wa, wx, ba, bx, lam]
    final = final_args is not None
    if final:
        in_specs += [row(1024), row(1024)]
        args += list(final_args)
    return pl.pallas_call(
        functools.partial(_lru_kernel, L, reverse, final),
        grid=(bsz, nc),
        in_specs=in_specs,
        out_specs=row(1024),
        out_shape=jax.ShapeDtypeStruct((n, 1024), BF16 if final else F32),
        scratch_shapes=[pltpu.VMEM((L, 1024), F32), pltpu.VMEM((L, 1024), F32), pltpu.VMEM((1, 1024), F32)],
        compiler_params=_cparams("arbitrary", "arbitrary"),
        name="rglru_bwd" if reverse else "rglru_fwd",
    )(*args)


def _out_kernel(tm, ctx_len, a1_ref, a2_ref, x_ref, modl_ref, modc_ref, g_ref, w_ref, xo_ref):
    is_ctx = _row_ids(pl.program_id(1), tm) < ctx_len
    half = w_ref.shape[0] // 2
    y = _dot(a1_ref[...], w_ref[0:half, :]) + _dot(a2_ref[...], w_ref[half:, :])
    xo_ref[...] = x_ref[...] + _mod(modl_ref, modc_ref, 2, is_ctx) * _rms(y, g_ref[...])


def _out_proj(a1, a2, x2, modl, modc, g, w, bsz, t, ctx_len):
    tm = ROW_TILE
    nt = t // tm
    row = lambda c: pl.BlockSpec((tm, c), lambda b, i: (b * nt + i, 0))
    return pl.pallas_call(
        functools.partial(_out_kernel, tm, ctx_len),
        grid=(bsz, nt),
        in_specs=[row(1024), row(1024), row(D_MODEL), pl.BlockSpec((1, 8, D_MODEL), lambda b, i: (b, 0, 0)),
                  _resident((8, D_MODEL)), _resident((1, D_MODEL)), _resident(w.shape)],
        out_specs=row(D_MODEL),
        out_shape=jax.ShapeDtypeStruct(x2.shape, F32),
        compiler_params=_cparams("arbitrary", "arbitrary"),
        name="out_proj",
    )(a1, a2, x2, modl, modc, g, w)


def _ffn_kernel(tm, ctx_len, x_ref, modl_ref, modc_ref, g2_ref, g3_ref, w1_ref, w2_ref, xo_ref):
    is_ctx = _row_ids(pl.program_id(1), tm) < ctx_len
    x = x_ref[...]
    h = _adaln(x, g2_ref[...], modl_ref, modc_ref, 3, is_ctx).astype(BF16)
    ff = w2_ref.shape[0]
    acc = jnp.zeros((tm, D_MODEL), F32)
    for j in range(ff // FF_STEP):
        lo = j * FF_STEP
        gj = _dot(h, w1_ref[:, lo:lo + FF_STEP])
        uj = _dot(h, w1_ref[:, ff + lo:ff + lo + FF_STEP])
        acc = acc + _dot((_silu(gj) * uj).astype(BF16), w2_ref[lo:lo + FF_STEP, :])
    xo_ref[...] = x + _mod(modl_ref, modc_ref, 5, is_ctx) * _rms(acc, g3_ref[...])


def _ffn(x2, modl, modc, g2, g3, w1, w2, bsz, t, ctx_len):
    tm = ROW_TILE
    nt = t // tm
    row = lambda c: pl.BlockSpec((tm, c), lambda b, i: (b * nt + i, 0))
    return pl.pallas_call(
        functools.partial(_ffn_kernel, tm, ctx_len),
        grid=(bsz, nt),
        in_specs=[row(D_MODEL), pl.BlockSpec((1, 8, D_MODEL), lambda b, i: (b, 0, 0)), _resident((8, D_MODEL)),
                  _resident((1, D_MODEL)), _resident((1, D_MODEL)), _resident(w1.shape), _resident(w2.shape)],
        out_specs=row(D_MODEL),
        out_shape=jax.ShapeDtypeStruct(x2.shape, F32),
        compiler_params=_cparams("arbitrary", "arbitrary"),
        name="dense_swiglu",
    )(x2, modl, modc, g2, g3, w1, w2)


def _in_cd_kernel(tm, ctx_len, x_ref, modl_ref, modc_ref, g_ref, w_ref, cw_ref, cb_ref, dtb_ref, aneg_ref,
                  gq_ref, gk_ref, gv_ref, gr_ref, z_ref, sx_ref, sb_ref, sc_ref, ga_ref, dts_ref):
    r = _row_ids(pl.program_id(1), tm)
    is_ctx = r < ctx_len
    h = _adaln(x_ref[...], g_ref[...], modl_ref, modc_ref, 0, is_ctx).astype(BF16)
    masks = _conv_masks(r, ctx_len)
    gq_ref[...] = (_dot(h, w_ref[:, 0:512]) * (G_DK ** -0.5)).astype(BF16)
    gk_ref[...] = _dot(h, w_ref[:, 512:1024]).astype(BF16)
    gv_ref[...] = _dot(h, w_ref[:, 1024:2048]).astype(BF16)
    gr_ref[...] = _silu(_dot(h, w_ref[:, 2048:3072])).astype(BF16)
    z_ref[...] = _silu(_dot(h, w_ref[:, 3072:4096])).astype(BF16)
    sx = _dwconv(_dot(h, w_ref[:, 4096:5120]), cw_ref[:, 0:1024], cb_ref[:, 0:1024], masks, tm)
    sx_ref[...] = _silu(sx).astype(BF16)
    sb = _dwconv(_dot(h, w_ref[:, 5120:5376]), cw_ref[:, 1024:1280], cb_ref[:, 1024:1280], masks, tm)
    sb_ref[...] = _silu(sb).astype(BF16)
    sc = _dwconv(_dot(h, w_ref[:, 5376:5632]), cw_ref[:, 1280:1536], cb_ref[:, 1280:1536], masks, tm)
    sc_ref[...] = _silu(sc).astype(BF16)
    ga_ref[...] = _dot(h, w_ref[:, 5632:5760])
    dt = _softplus(_dot(h, w_ref[:, 5760:5888]) + dtb_ref[...])
    lane = lax.broadcasted_iota(I32, dt.shape, 1)
    dts_ref[...] = jnp.where(lane < 2 * S_HEADS, dt, dt * aneg_ref[...])


def _in_cd(x2, modl, modc, g, w, cw, cb, dtb, aneg, bsz, t, ctx_len):
    tm = ROW_TILE
    nt = t // tm
    n = bsz * t
    row = lambda c: pl.BlockSpec((tm, c), lambda b, i: (b * nt + i, 0))
    outs = [(512, BF16), (512, BF16), (1024, BF16), (1024, BF16), (1024, BF16), (1024, BF16), (256, BF16),
            (256, BF16), (LANES, F32), (LANES, F32)]
    return pl.pallas_call(
        functools.partial(_in_cd_kernel, tm, ctx_len),
        grid=(bsz, nt),
        in_specs=[row(D_MODEL), pl.BlockSpec((1, 8, D_MODEL), lambda b, i: (b, 0, 0)), _resident((8, D_MODEL)),
                  _resident((1, D_MODEL)), _resident(w.shape), _resident(cw.shape), _resident(cb.shape),
                  _resident(dtb.shape), _resident(aneg.shape)],
        out_specs=[row(c) for c, _ in outs],
        out_shape=[jax.ShapeDtypeStruct((n, c), dt) for c, dt in outs],
        compiler_params=_cparams("arbitrary", "arbitrary"),
        name="in_proj_cd",
    )(x2, modl, modc, g, w, cw, cb, dtb, aneg)


def _gla_kernel(L, C, reverse, final, *refs):
    if final:
        q_ref, k_ref, v_ref, ga_ref, aw_ref, ab_ref, of_ref, gr_ref, nw_ref, out_ref, s_scr = refs
    else:
        q_ref, k_ref, v_ref, ga_ref, aw_ref, ab_ref, out_ref, s_scr = refs

    @pl.when(pl.program_id(1) == 0)
    def _():
        s_scr[...] = jnp.zeros_like(s_scr)

    mask = _tri_mask(L, reverse)
    mask01 = mask.astype(BF16)
    pre = jnp.dot(ga_ref[...], aw_ref[...], preferred_element_type=F32, precision=lax.Precision.HIGHEST)
    lg = _log_sigmoid(pre + ab_ref[...]) * (1.0 / G_TAU)
    ball = _sel_cols(mask01, lg)
    rows = lax.broadcasted_iota(I32, (L, 1), 0)
    nb = L // C
    last = 0 if reverse else L - 1
    for h in range(G_HEADS):
        ks = slice(h * G_DK, (h + 1) * G_DK)
        b = ball[:, ks]
        qh = q_ref[:, ks].astype(F32)
        kh = k_ref[:, ks].astype(F32)
        vh = v_ref[:, h * G_DV:(h + 1) * G_DV]
        st = s_scr[h]
        inter = _dot_nt((qh * jnp.exp(b)).astype(BF16), st.astype(BF16))
        blocks = []
        for i in range(nb):
            lo, hi = i * C, (i + 1) * C
            if reverse:
                bref = b[hi:hi + 1, :] if i < nb - 1 else jnp.zeros((1, G_DK), F32)
                visible = rows >= lo
            else:
                bref = b[lo - 1:lo, :] if i > 0 else jnp.zeros((1, G_DK), F32)
                visible = rows < hi
            qi = (qh[lo:hi, :] * jnp.exp(b[lo:hi, :] - bref)).astype(BF16)
            ki = (kh * jnp.exp(jnp.where(visible, bref - b, -jnp.inf))).astype(BF16)
            blocks.append(_dot_nt(qi, ki))
        att = jnp.where(mask, jnp.concatenate(blocks, axis=0), 0.0).astype(BF16)
        oh = inter + _dot(att, vh)
        bl = b[last:last + 1, :]
        kd = (kh * jnp.exp(bl - b)).astype(BF16)
        s_scr[h] = st * jnp.exp(bl) + _dot_tn(vh, kd)
        sl = slice(h * G_DV, (h + 1) * G_DV)
        if final:
            os_ = oh + of_ref[:, sl]
            out_ref[:, sl] = (_rms(os_, nw_ref[:, sl]) * gr_ref[:, sl].astype(F32)).astype(BF16)
        else:
            out_ref[:, sl] = oh


def _gla(q, k, v, ga, aw, ab, bsz, t, ctx_len, reverse, final_args=None):
    L, C = G_CHUNK, G_SUB
    nc, ncc = t // L, ctx_len // L
    order = _chunk_order(reverse, ncc, nc)
    n = bsz * t
    row = lambda c: pl.BlockSpec((L, c), lambda b, i: (b * nc + order(i), 0))
    in_specs = [row(512), row(512), row(1024), row(LANES), _resident(aw.shape), _resident(ab.shape)]
    args = [q, k, v, ga, aw, ab]
    final = final_args is not None
    if final:
        in_specs += [row(1024), row(1024), _resident((1, 1024))]
        args += list(final_args)
    return pl.pallas_call(
        functools.partial(_gla_kernel, L, C, reverse, final),
        grid=(bsz, nc),
        in_specs=in_specs,
        out_specs=row(1024),
        out_shape=jax.ShapeDtypeStruct((n, 1024), BF16 if final else F32),
        scratch_shapes=[pltpu.VMEM((G_HEADS, G_DV, G_DK), F32)],
        compiler_params=_cparams("arbitrary", "arbitrary"),
        name="gla_bwd" if reverse else "gla_fwd",
    )(*args)


def _ssd_kernel(L, reverse, final, *refs):
    if final:
        x_ref, b_ref, c_ref, gc_ref, gr_ref, yf_ref, z_ref, dsk_ref, nw_ref, out_ref, s_scr = refs
    else:
        x_ref, b_ref, c_ref, gc_ref, gr_ref, out_ref, s_scr = refs
    d = 1 if reverse else 0

    @pl.when(pl.program_id(1) == 0)
    def _():
        s_scr[...] = jnp.zeros_like(s_scr)

    mask = _tri_mask(L, reverse)
    mask01 = mask.astype(BF16)
    gc = gc_ref[...]
    gr = gr_ref[0]
    bcol = _sel_cols(mask01, gc)
    brow = _sel_rows(gr, mask01)
    last = 0 if reverse else L - 1
    dt_o, la_o = S_HEADS * d, 2 * S_HEADS + S_HEADS * d
    er = lax.broadcasted_iota(I32, (LANES, S_HEADS * S_P), 0)
    ec = jnp.right_shift(lax.broadcasted_iota(I32, (LANES, S_HEADS * S_P), 1), 6)
    sel_dt = (er == ec + dt_o).astype(BF16)
    sel_la = (er == ec + la_o).astype(BF16)
    x = x_ref[...].astype(F32)
    glane = lax.broadcasted_iota(I32, (L, LANES), 1)
    bla = jnp.where((glane >= la_o) & (glane < la_o + S_HEADS), bcol, 0.0)
    dtx = _expand(gc, sel_dt)
    ebx = _expand(jnp.exp(bla), sel_la)
    bl = bla[last:last + 1, :]
    wx = _expand(jnp.exp(bl - bla), sel_la) * dtx
    xdt = (x * dtx).astype(BF16)
    xw = (x * wx).astype(BF16)
    lane = lax.broadcasted_iota(I32, (L, 2 * S_P), 1)
    ys = []
    for g in range(S_GROUPS):
        ns = slice(g * S_N, (g + 1) * S_N)
        bg, cg = b_ref[:, ns], c_ref[:, ns]
        cb = _dot_nt(cg, bg)
        gs = slice(g * S_HPG * S_P, (g + 1) * S_HPG * S_P)
        st = s_scr[g]
        inter = _dot(cg, st.astype(BF16)) * ebx[:, gs]
        parts = []
        for pr in range(S_HPG // 2):
            e0 = g * S_HPG + 2 * pr
            xp = xdt[:, e0 * S_P:(e0 + 2) * S_P]
            res = []
            for e in (e0, e0 + 1):
                bc, br = bcol[:, la_o + e:la_o + e + 1], brow[la_o + e:la_o + e + 1, :]
                att = (cb * jnp.exp(jnp.where(mask, bc - br, -jnp.inf))).astype(BF16)
                res.append(_dot(att, xp))
            parts.append(jnp.where(lane < S_P, res[0], res[1]))
        y = jnp.concatenate(parts, axis=1) + inter
        s_scr[g] = st * ebx[last:last + 1, gs] + _dot_tn(bg, xw[:, gs])
        if final:
            ys.append(y + yf_ref[:, gs] + dsk_ref[:, gs] * x[:, gs])
        else:
            out_ref[:, gs] = y
    if final:
        yz = jnp.concatenate(ys, axis=1) * z_ref[...].astype(F32)
        out_ref[...] = _rms(yz, nw_ref[...]).astype(BF16)


def _ssd(x, bm, cm, gcol, grow, bsz, t, ctx_len, reverse, final_args=None):
    L = S_CHUNK
    nc, ncc = t // L, ctx_len // L
    order = _chunk_order(reverse, ncc, nc)
    n = bsz * t
    row = lambda c: pl.BlockSpec((L, c), lambda b, i: (b * nc + order(i), 0))
    in_specs = [row(1024), row(256), row(256), row(LANES),
                pl.BlockSpec((1, 4 * S_HEADS, L), lambda b, i: (b * nc + order(i), 0, 0))]
    args = [x, bm, cm, gcol, grow]
    final = final_args is not None
    if final:
        in_specs += [row(1024), row(1024), _resident((1, 1024)), _resident((1, 1024))]
        args += list(final_args)
    return pl.pallas_call(
        functools.partial(_ssd_kernel, L, reverse, final),
        grid=(bsz, nc),
        in_specs=in_specs,
        out_specs=row(1024),
        out_shape=jax.ShapeDtypeStruct((n, 1024), BF16 if final else F32),
        scratch_shapes=[pltpu.VMEM((S_GROUPS, S_N, S_HPG * S_P), F32)],
        compiler_params=_cparams("arbitrary", "arbitrary"),
        name="ssd_bwd" if reverse else "ssd_fwd",
    )(*args)


def _router_kernel(x_ref, modl_ref, g_ref, rw_ref, rb_ref, h_ref, idx_ref, gate_ref):
    h = _rms(x_ref[...], g_ref[...]) * (1.0 + modl_ref[0, 4:5, :]) + modl_ref[0, 3:4, :]
    h_ref[...] = h.astype(BF16)
    logits = jnp.dot(h, rw_ref[...], preferred_element_type=F32, precision=lax.Precision.HIGHEST) + rb_ref[...]
    lane = lax.broadcasted_iota(I32, logits.shape, 1)
    logits = jnp.where(lane < N_EXPERTS, logits, -jnp.inf)
    lanef = lane.astype(F32)
    m1 = jnp.max(logits, axis=1, keepdims=True)
    i1 = jnp.min(jnp.where(logits == m1, lanef, float(LANES)), axis=1, keepdims=True)
    rest = jnp.where(lanef == i1, -jnp.inf, logits)
    m2 = jnp.max(rest, axis=1, keepdims=True)
    i2 = jnp.min(jnp.where(rest == m2, lanef, float(LANES)), axis=1, keepdims=True)
    e = jnp.exp(m2 - m1)
    g1 = 1.0 / (1.0 + e)
    idx_ref[...] = jnp.where(lane == 0, i1, jnp.where(lane == 1, i2, 0.0)).astype(I32)
    gate_ref[...] = jnp.where(lane == 0, g1, jnp.where(lane == 1, e * g1, 0.0))


def _router(x2, modl, g, rw, rb, bsz, t, ctx_len):
    tm = ROW_TILE
    nt, nct = t // tm, ctx_len // tm
    nl = bsz * (t - ctx_len)
    nlt = nt - nct
    orow = lambda c: pl.BlockSpec((tm, c), lambda b, i: (b * nlt + i, 0))
    return pl.pallas_call(
        _router_kernel,
        grid=(bsz, nlt),
        in_specs=[pl.BlockSpec((tm, D_MODEL), lambda b, i: (b * nt + nct + i, 0)),
                  pl.BlockSpec((1, 8, D_MODEL), lambda b, i: (b, 0, 0)),
                  _resident((1, D_MODEL)), _resident(rw.shape), _resident(rb.shape)],
        out_specs=[orow(D_MODEL), orow(LANES), orow(LANES)],
        out_shape=[jax.ShapeDtypeStruct((nl, D_MODEL), BF16), jax.ShapeDtypeStruct((nl, LANES), I32),
                   jax.ShapeDtypeStruct((nl, LANES), F32)],
        compiler_params=_cparams("arbitrary", "arbitrary"),
        name="moe_router",
    )(x2, modl, g, rw, rb)


def _moe_kernel(be_ref, nu_ref, x_ref, w1g_ref, w1u_ref, w2_ref, y_ref, acc_ref):
    i, j = pl.program_id(0), pl.program_id(1)

    @pl.when(i < nu_ref[0])
    def _():
        @pl.when(j == 0)
        def _():
            acc_ref[...] = jnp.zeros_like(acc_ref)

        x = x_ref[...]
        a = (_silu(_dot(x, w1g_ref[0])) * _dot(x, w1u_ref[0])).astype(BF16)
        acc_ref[...] += _dot(a, w2_ref[0])

        @pl.when(j == pl.num_programs(1) - 1)
        def _():
            y_ref[...] = acc_ref[...]


def _moe_experts(xs, w1, w2, block_e, n_used, n_blocks):
    bm, fc = MOE_ROWS, MOE_FF
    nff = FF_EXPERT // fc
    used = lambda i, nu: jnp.minimum(i, nu[0] - 1)
    ffi = lambda i, j, nu: jnp.where(i < nu[0], j, nff - 1)
    grid_spec = pltpu.PrefetchScalarGridSpec(
        num_scalar_prefetch=2,
        grid=(n_blocks, nff),
        in_specs=[pl.BlockSpec((bm, D_MODEL), lambda i, j, be, nu: (used(i, nu), 0)),
                  pl.BlockSpec((1, D_MODEL, fc), lambda i, j, be, nu: (be[used(i, nu)], 0, ffi(i, j, nu))),
                  pl.BlockSpec((1, D_MODEL, fc), lambda i, j, be, nu: (be[used(i, nu)], 0, nff + ffi(i, j, nu))),
                  pl.BlockSpec((1, fc, D_MODEL), lambda i, j, be, nu: (be[used(i, nu)], ffi(i, j, nu), 0))],
        out_specs=pl.BlockSpec((bm, D_MODEL), lambda i, j, be, nu: (used(i, nu), 0)),
        scratch_shapes=[pltpu.VMEM((bm, D_MODEL), F32)],
    )
    return pl.pallas_call(
        _moe_kernel,
        grid_spec=grid_spec,
        out_shape=jax.ShapeDtypeStruct((n_blocks * bm, D_MODEL), F32),
        compiler_params=_cparams("arbitrary", "arbitrary"),
        name="moe_experts",
    )(block_e, n_used, xs, w1, w1, w2)


def _combine_kernel(y0_ref, y1_ref, gate_ref, x_ref, modl_ref, g_ref, o_ref):
    gt = gate_ref[...]
    y = gt[:, 0:1] * y0_ref[...] + gt[:, 1:2] * y1_ref[...]
    o_ref[...] = x_ref[...] + modl_ref[0, 5:6, :] * _rms(y, g_ref[...])


def _combine(y0, y1, gates, x2, modl, g, bsz, t, ctx_len):
    tm = ROW_TILE
    nt, nct = t // tm, ctx_len // tm
    nlt = nt - nct
    nl = bsz * (t - ctx_len)
    orow = lambda c: pl.BlockSpec((tm, c), lambda b, i: (b * nlt + i, 0))
    return pl.pallas_call(
        _combine_kernel,
        grid=(bsz, nlt),
        in_specs=[orow(D_MODEL), orow(D_MODEL), orow(LANES),
                  pl.BlockSpec((tm, D_MODEL), lambda b, i: (b * nt + nct + i, 0)),
                  pl.BlockSpec((1, 8, D_MODEL), lambda b, i: (b, 0, 0)), _resident((1, D_MODEL))],
        out_specs=orow(D_MODEL),
        out_shape=jax.ShapeDtypeStruct((nl, D_MODEL), F32),
        compiler_params=_cparams("arbitrary", "arbitrary"),
        name="moe_combine",
    )(y0, y1, gates, x2, modl, g)


def _routing_tables(top_idx, n_tok):
    bm = MOE_ROWS
    flat_e = top_idx.reshape(-1)
    n_assign = flat_e.shape[0]
    onehot = (flat_e[:, None] == jnp.arange(N_EXPERTS, dtype=I32)[None, :]).astype(I32)
    incl = jnp.cumsum(onehot, axis=0)
    counts = incl[-1]
    rank = jnp.take_along_axis(incl, flat_e[:, None], axis=1)[:, 0] - 1
    padded = (counts + bm - 1) // bm * bm
    ends_p = jnp.cumsum(padded)
    pstarts = ends_p - padded
    slot = (pstarts[flat_e] + rank).astype(I32)
    n_blocks = n_assign // bm + N_EXPERTS
    flat_tok = jnp.arange(n_assign, dtype=I32) // TOP_K
    slot_tok = jnp.full((n_blocks * bm,), n_tok, I32).at[slot].set(flat_tok)
    block_e = jnp.minimum(jnp.searchsorted(ends_p, jnp.arange(n_blocks, dtype=I32) * bm, side='right'),
                          N_EXPERTS - 1).astype(I32)
    n_used = (ends_p[-1] // bm).astype(I32).reshape(1)
    return slot, slot_tok, block_e, n_used, n_blocks


def _row_layout(gcol, ncols, L):
    n = gcol.shape[0]
    return gcol[:, :ncols].reshape(n // L, L, ncols).transpose(0, 2, 1)


def kernel(x, c, ctx, c_ctx, mod_w, mod_b, norm_g, ab_in_w, m_conv_w, m_conv_b, m_gate_b, m_norm_w, l_conv_w,
           l_conv_b, l_wa, l_ba, l_wx, l_bx, l_lam, ab_out_w, ffn_w1, ffn_w2, cd_in_w, g_alpha_w, g_alpha_b,
           g_norm_w, s_conv_w, s_conv_b, s_dt_bias, s_A_log, s_D, s_norm_w, cd_out_w, router_w, router_b,
           moe_w1, moe_w2):
    bsz, seq, dm = x.shape
    ctx_len = ctx.shape[1]
    t = ctx_len + seq
    n = bsz * t
    x2 = jnp.concatenate([ctx, x], axis=1).reshape(n, dm)

    c_all = jnp.concatenate([c, c_ctx[None, :], jnp.zeros((-(bsz + 1) % 8, dm), F32)], axis=0)
    mods = _modulation(c_all, mod_w, mod_b).reshape(mod_w.shape[0], c_all.shape[0], 6, dm)
    pad2 = jnp.zeros((bsz, 2, dm), F32)

    def layer_mods(layer):
        modl = jnp.concatenate([mods[layer, :bsz], pad2], axis=1)
        modc = jnp.concatenate([mods[layer, bsz], pad2[0]], axis=0)
        return modl, modc

    modl, modc = layer_mods(0)
    g = norm_g[0]
    w = ab_in_w[0]
    o_qk, o_v, o_o, o_gt, o_lx, o_lg = 0, 1024, 2048, 3072, 3088, 4112
    w_in = jnp.concatenate([w[:, o_qk:o_v], w[:, o_v:o_o], w[:, o_o:o_gt], w[:, o_lx:o_lg], w[:, o_lg:],
                            w[:, o_gt:o_lx], jnp.zeros((dm, LANES - 16), F32)], axis=1).astype(BF16)
    gb = jnp.concatenate([m_gate_b[0].reshape(1, 16), jnp.zeros((1, LANES - 16), F32)], axis=1)
    q, k, v, og, lx, glg, gates = _in_ab(x2, modl, modc, g[0:1], w_in, m_conv_w[0], m_conv_b[0][None],
                                         l_conv_w[0], l_conv_b[0][None], gb, bsz, t, ctx_len)
    grow = _row_layout(gates, 16, M_CHUNK)
    hm_f = _mlstm(q, k, v, gates, grow, bsz, t, ctx_len, False)
    hm = _mlstm(q, k, v, gates, grow, bsz, t, ctx_len, True, (hm_f, og, m_norm_w[0][None]))
    lru_args = lambda d: (l_wa[0, d].astype(BF16), l_wx[0, d].astype(BF16), l_ba[0, d][None], l_bx[0, d][None],
                          l_lam[0, d][None])
    hl_f = _lru(lx, *lru_args(0), bsz, t, ctx_len, False)
    hl = _lru(lx, *lru_args(1), bsz, t, ctx_len, True, (hl_f, glg))
    x2 = _out_proj(hm, hl, x2, modl, modc, g[1:2], ab_out_w[0].astype(BF16), bsz, t, ctx_len)
    x2 = _ffn(x2, modl, modc, g[2:3], g[3:4], ffn_w1[0].astype(BF16), ffn_w2[0].astype(BF16), bsz, t, ctx_len)

    modl, modc = layer_mods(1)
    g = norm_g[1]
    w = cd_in_w[0]
    o_gv, o_gr, o_ga, o_z, o_xbc, o_dt = 1024, 2048, 3072, 3104, 4128, 5664
    zpad = jnp.zeros((dm, LANES - 32), F32)
    w_in = jnp.concatenate([w[:, :o_gv], w[:, o_gv:o_gr], w[:, o_gr:o_ga], w[:, o_z:o_xbc], w[:, o_xbc:o_dt],
                            w[:, o_ga:o_z], zpad, w[:, o_dt:], w[:, o_dt:], jnp.zeros((dm, LANES - 64), F32)],
                           axis=1).astype(BF16)
    dtb = jnp.concatenate([s_dt_bias[0].reshape(1, 32), s_dt_bias[0].reshape(1, 32),
                           jnp.zeros((1, LANES - 64), F32)], axis=1)
    aneg = jnp.concatenate([jnp.zeros((1, 32), F32), -jnp.exp(s_A_log[0].reshape(1, 32)),
                            jnp.zeros((1, LANES - 64), F32)], axis=1)
    gq, gk, gv, grs, zs, sx, sb, sc, ga, dts = _in_cd(x2, modl, modc, g[0:1], w_in, s_conv_w[0], s_conv_b[0][None],
                                                      dtb, aneg, bsz, t, ctx_len)

    def alpha(d):
        aw = jnp.zeros((LANES, G_HEADS * G_DK), F32).at[d * G_RANK:(d + 1) * G_RANK].set(g_alpha_w[0, d])
        return aw, g_alpha_b[0, d][None]

    og_f = _gla(gq, gk, gv, ga, *alpha(0), bsz, t, ctx_len, False)
    og2 = _gla(gq, gk, gv, ga, *alpha(1), bsz, t, ctx_len, True, (og_f, grs, g_norm_w[0][None]))
    drow = _row_layout(dts, 4 * S_HEADS, S_CHUNK)
    ys_f = _ssd(sx, sb, sc, dts, drow, bsz, t, ctx_len, False)
    dskip = jnp.repeat(s_D[0], S_P)[None]
    ys = _ssd(sx, sb, sc, dts, drow, bsz, t, ctx_len, True, (ys_f, zs, dskip, s_norm_w[0][None]))
    x2 = _out_proj(og2, ys, x2, modl, modc, g[1:2], cd_out_w[0].astype(BF16), bsz, t, ctx_len)

    rw = jnp.concatenate([router_w[0], jnp.zeros((dm, LANES - N_EXPERTS), F32)], axis=1)
    rb = jnp.concatenate([router_b[0][None], jnp.zeros((1, LANES - N_EXPERTS), F32)], axis=1)
    h2, top_idx, top_gate = _router(x2, modl, g[2:3], rw, rb, bsz, t, ctx_len)
    n_tok = bsz * seq
    slot, slot_tok, block_e, n_used, n_blocks = _routing_tables(top_idx[:, :TOP_K], n_tok)
    h2_pad = jnp.concatenate([h2, jnp.zeros((1, dm), BF16)], axis=0)
    xs = jnp.take(h2_pad, slot_tok, axis=0)
    ys_e = _moe_experts(xs, moe_w1[0].astype(BF16), moe_w2[0].astype(BF16), block_e, n_used, n_blocks)
    slot2 = slot.reshape(n_tok, TOP_K)
    out = _combine(jnp.take(ys_e, slot2[:, 0], axis=0), jnp.take(ys_e, slot2[:, 1], axis=0), top_gate, x2, modl, g[3:4], bsz, t, ctx_len)
    return out.reshape(bsz, seq, dm)
```

```python
import functools

import jax
import jax.numpy as jnp
from jax import lax
from jax.experimental import pallas as pl
from jax.experimental.pallas import tpu as pltpu
from jax.experimental.pallas import tpu_sc as plsc

F32, BF16, I32 = jnp.float32, jnp.bfloat16, jnp.int32

D_MODEL = 1024
GRID_W = 64
EPS = 1e-6
M_HEADS, M_DK, M_DV = 4, 128, 256
LRU_BLOCKS, LRU_BW, LRU_C = 8, 128, 8.0
G_HEADS, G_DK, G_DV, G_RANK, G_TAU = 4, 128, 256, 16, 16.0
S_HEADS, S_P, S_N, S_GROUPS, S_HPG = 16, 64, 128, 2, 8
FF_DENSE = 2816
N_EXPERTS, TOP_K, FF_EXPERT = 8, 2, 3584

LANES = 128
VMEM_LIMIT = 56 * 1024 * 1024
ROW_TILE = 256
M_CHUNK = 256
G_CHUNK, G_SUB = 64, 16
S_CHUNK = 64
LRU_TILE = 256
FF_STEP = 256
MOE_ROWS = 512
MOE_FF = 512
SC_CORES, SC_SUBCORES = 2, 16
SC_GATHER_ROWS = 64


def _cparams(*sem):
    return pltpu.CompilerParams(dimension_semantics=sem, vmem_limit_bytes=VMEM_LIMIT)


def _silu(x):
    return x * jax.nn.sigmoid(x)


def _softplus(x):
    return jnp.maximum(x, 0.0) + jnp.log1p(jnp.exp(-jnp.abs(x)))


def _log_sigmoid(x):
    return jnp.minimum(x, 0.0) - jnp.log1p(jnp.exp(-jnp.abs(x)))


def _dot(a, b):
    return jnp.dot(a, b, preferred_element_type=F32)


def _dot_nt(a, b):
    return lax.dot_general(a, b, (((1,), (1,)), ((), ())), preferred_element_type=F32)


def _dot_tn(a, b):
    return lax.dot_general(a, b, (((0,), (0,)), ((), ())), preferred_element_type=F32)


def _split3(f):
    f1 = f.astype(BF16)
    r = f - f1.astype(F32)
    f2 = r.astype(BF16)
    f3 = (r - f2.astype(F32)).astype(BF16)
    return f1, f2, f3


def _sel_cols(mask01, f):
    p1, p2, p3 = _split3(f)
    return _dot(mask01, p1) + _dot(mask01, p2) + _dot(mask01, p3)


def _sel_rows(f, mask01):
    p1, p2, p3 = _split3(f)
    return _dot_nt(p1, mask01) + _dot_nt(p2, mask01) + _dot_nt(p3, mask01)


def _expand(f, sel01):
    p1, p2, p3 = _split3(f)
    return _dot(p1, sel01) + _dot(p2, sel01) + _dot(p3, sel01)


def _rms(u, g):
    return u * lax.rsqrt(jnp.mean(u * u, axis=-1, keepdims=True) + EPS) * g


def _row_ids(tile_idx, tm):
    return tile_idx * tm + lax.broadcasted_iota(I32, (tm, 1), 0)


def _mod(modl_ref, modc_ref, idx, is_ctx):
    return jnp.where(is_ctx, modc_ref[idx:idx + 1, :], modl_ref[0, idx:idx + 1, :])


def _adaln(x, g, modl_ref, modc_ref, shift_idx, is_ctx):
    shift = _mod(modl_ref, modc_ref, shift_idx, is_ctx)
    scale = _mod(modl_ref, modc_ref, shift_idx + 1, is_ctx)
    return _rms(x, g) * (1.0 + scale) + shift


def _conv_masks(r, ctx_len):
    in_lat = r >= ctx_len
    pos = jnp.where(in_lat, jnp.bitwise_and(r - ctx_len, GRID_W - 1), r)
    seg = jnp.where(in_lat, GRID_W, ctx_len)
    return pos >= 2, pos >= 1, pos <= seg - 2


def _dwconv(y, cw, cb, masks, tm):
    m2, m1, p1 = masks
    ym2 = jnp.where(m2, pltpu.roll(y, 2, 0), 0.0)
    ym1 = jnp.where(m1, pltpu.roll(y, 1, 0), 0.0)
    yp1 = jnp.where(p1, pltpu.roll(y, tm - 1, 0), 0.0)
    return cb + cw[0:1, :] * ym2 + cw[1:2, :] * ym1 + cw[2:3, :] * y + cw[3:4, :] * yp1


def _mod_kernel(c_ref, w_ref, b_ref, o_ref):
    a = _silu(c_ref[...])
    o_ref[0] = jnp.dot(a, w_ref[0], preferred_element_type=F32, precision=lax.Precision.HIGHEST) + b_ref[0]


def _modulation(c_all, mod_w, mod_b):
    depth, d, n6 = mod_w.shape
    rows = c_all.shape[0]
    tn = 1024
    return pl.pallas_call(
        _mod_kernel,
        grid=(depth, n6 // tn),
        in_specs=[pl.BlockSpec((rows, d), lambda l, j: (0, 0)),
                  pl.BlockSpec((1, d, tn), lambda l, j: (l, 0, j)),
                  pl.BlockSpec((1, 1, tn), lambda l, j: (l, 0, j))],
        out_specs=pl.BlockSpec((1, rows, tn), lambda l, j: (l, 0, j)),
        out_shape=jax.ShapeDtypeStruct((depth, rows, n6), F32),
        compiler_params=_cparams("arbitrary", "arbitrary"),
        name="modulation",
    )(c_all, mod_w, mod_b.reshape(depth, 1, n6))


def _in_ab_kernel(tm, ctx_len, x_ref, modl_ref, modc_ref, g_ref, w_ref, mcw_ref, mcb_ref, lcw_ref, lcb_ref,
                  gb_ref, q_ref, k_ref, v_ref, og_ref, lx_ref, glg_ref, gates_ref):
    r = _row_ids(pl.program_id(1), tm)
    is_ctx = r < ctx_len
    h = _adaln(x_ref[...], g_ref[...], modl_ref, modc_ref, 0, is_ctx).astype(BF16)
    masks = _conv_masks(r, ctx_len)
    q = _dwconv(_dot(h, w_ref[:, 0:512]), mcw_ref[:, 0:512], mcb_ref[:, 0:512], masks, tm)
    q_ref[...] = _silu(q).astype(BF16)
    k = _dwconv(_dot(h, w_ref[:, 512:1024]), mcw_ref[:, 512:1024], mcb_ref[:, 512:1024], masks, tm)
    k_ref[...] = (_silu(k) * (M_DK ** -0.5)).astype(BF16)
    v_ref[...] = _dot(h, w_ref[:, 1024:2048]).astype(BF16)
    og_ref[...] = jax.nn.sigmoid(_dot(h, w_ref[:, 2048:3072])).astype(BF16)
    lx = _dwconv(_dot(h, w_ref[:, 3072:4096]), lcw_ref[...], lcb_ref[...], masks, tm)
    lx_ref[...] = lx
    glg_ref[...] = jax.nn.gelu(_dot(h, w_ref[:, 4096:5120])).astype(BF16)
    gt = _dot(h, w_ref[:, 5120:5248]) + gb_ref[...]
    lane = lax.broadcasted_iota(I32, gt.shape, 1)
    gates_ref[...] = jnp.where(lane >= 2 * M_HEADS, _log_sigmoid(gt), gt)


def _resident(shape):
    nd = len(shape)
    return pl.BlockSpec(shape, lambda *_: (0,) * nd)


def _in_ab(x2, modl, modc, g, w, mcw, mcb, lcw, lcb, gb, bsz, t, ctx_len):
    tm = ROW_TILE
    nt = t // tm
    n = bsz * t
    row = lambda c: pl.BlockSpec((tm, c), lambda b, i: (b * nt + i, 0))
    outs = [(512, BF16), (512, BF16), (1024, BF16), (1024, BF16), (1024, F32), (1024, BF16), (LANES, F32)]
    return pl.pallas_call(
        functools.partial(_in_ab_kernel, tm, ctx_len),
        grid=(bsz, nt),
        in_specs=[row(D_MODEL),
                  pl.BlockSpec((1, 8, D_MODEL), lambda b, i: (b, 0, 0)),
                  _resident((8, D_MODEL)), _resident((1, D_MODEL)), _resident(w.shape),
                  _resident(mcw.shape), _resident(mcb.shape), _resident(lcw.shape), _resident(lcb.shape),
                  _resident(gb.shape)],
        out_specs=[row(c) for c, _ in outs],
        out_shape=[jax.ShapeDtypeStruct((n, c), dt) for c, dt in outs],
        compiler_params=_cparams("arbitrary", "arbitrary"),
        name="in_proj_ab",
    )(x2, modl, modc, g, w, mcw, mcb, lcw, lcb, gb)


def _chunk_order(reverse, n_ctx_chunks, n_chunks):
    if not reverse:
        return lambda c: c
    return lambda c: jnp.where(c < n_ctx_chunks, n_ctx_chunks - 1 - c, n_chunks - 1 - (c - n_ctx_chunks))


def _tri_mask(L, reverse):
    row = lax.broadcasted_iota(I32, (L, L), 0)
    col = lax.broadcasted_iota(I32, (L, L), 1)
    return (col >= row) if reverse else (col <= row)


def _mlstm_kernel(L, reverse, final, *refs):
    if final:
        q_ref, k_ref, v_ref, gc_ref, gr_ref, hf_ref, og_ref, nw_ref, out_ref, c_scr, m_scr = refs
    else:
        q_ref, k_ref, v_ref, gc_ref, gr_ref, out_ref, c_scr, m_scr = refs
    d = 1 if reverse else 0

    @pl.when(pl.program_id(1) == 0)
    def _():
        c_scr[...] = jnp.zeros_like(c_scr)
        m_scr[...] = jnp.zeros_like(m_scr)

    mask = _tri_mask(L, reverse)
    mask01 = mask.astype(BF16)
    gc = gc_ref[...]
    gr = gr_ref[0]
    bcol = _sel_cols(mask01, gc)
    brow = _sel_rows(gr, mask01)
    last = 0 if reverse else L - 1
    ones_col = (lax.broadcasted_iota(I32, (L, LANES), 1) == 0).astype(BF16)
    for h in range(M_HEADS):
        fo, io = 2 * M_HEADS + M_HEADS * d + h, M_HEADS * d + h
        bc, br = bcol[:, fo:fo + 1], brow[fo:fo + 1, :]
        ic, ir = gc[:, io:io + 1], gr[io:io + 1, :]
        m = m_scr[h:h + 1, 0:1]
        g = bc + m
        dlog = jnp.where(mask, bc - br + ir, -jnp.inf)
        mt = jnp.maximum(g, jnp.max(dlog, axis=1, keepdims=True))
        w_inter = jnp.exp(g - mt)
        p = jnp.exp(dlog - mt)
        qh = q_ref[:, h * M_DK:(h + 1) * M_DK]
        kh = k_ref[:, h * M_DK:(h + 1) * M_DK]
        vx = jnp.concatenate([v_ref[:, h * M_DV:(h + 1) * M_DV], ones_col], axis=1)
        sc = (_dot_nt(qh, kh) * p).astype(BF16)
        cx = c_scr[h]
        numx = w_inter * _dot(qh, cx.astype(BF16)) + _dot(sc, vx)
        den = numx[:, M_DV:M_DV + 1]
        hh = numx[:, 0:M_DV] * (1.0 / jnp.maximum(jnp.abs(den), jnp.exp(-mt)))
        bl = bc[last:last + 1, :]
        a_s = bl - bc + ic
        m_new = jnp.maximum(bl + m, jnp.max(a_s, axis=0, keepdims=True))
        decay = jnp.exp(bl + m - m_new)
        ws = jnp.exp(a_s - m_new)
        kw = (kh.astype(F32) * ws).astype(BF16)
        c_scr[h] = decay * cx + _dot_tn(kw, vx)
        m_scr[h:h + 1, 0:1] = m_new
        sl = slice(h * M_DV, (h + 1) * M_DV)
        if final:
            hs = hh + hf_ref[:, sl]
            out_ref[:, sl] = (_rms(hs, nw_ref[:, sl]) * og_ref[:, sl].astype(F32)).astype(BF16)
        else:
            out_ref[:, sl] = hh


def _mlstm(q, k, v, gcol, grow, bsz, t, ctx_len, reverse, final_args=None):
    L = M_CHUNK
    nc, ncc = t // L, ctx_len // L
    order = _chunk_order(reverse, ncc, nc)
    n = bsz * t
    row = lambda c: pl.BlockSpec((L, c), lambda b, i: (b * nc + order(i), 0))
    in_specs = [row(512), row(512), row(1024), row(LANES),
                pl.BlockSpec((1, 16, L), lambda b, i: (b * nc + order(i), 0, 0))]
    args = [q, k, v, gcol, grow]
    final = final_args is not None
    if final:
        in_specs += [row(1024), row(1024), _resident((1, 1024))]
        args += list(final_args)
    return pl.pallas_call(
        functools.partial(_mlstm_kernel, L, reverse, final),
        grid=(bsz, nc),
        in_specs=in_specs,
        out_specs=row(1024),
        out_shape=jax.ShapeDtypeStruct((n, 1024), BF16 if final else F32),
        scratch_shapes=[pltpu.VMEM((M_HEADS, M_DK, M_DV + LANES), F32), pltpu.VMEM((8, LANES), F32)],
        compiler_params=_cparams("arbitrary", "arbitrary"),
        name="mlstm_bwd" if reverse else "mlstm_fwd",
    )(*args)


def _lru_kernel(L, reverse, final, *refs):
    if final:
        u_ref, wa_ref, wx_ref, ba_ref, bx_ref, lam_ref, hf_ref, glg_ref, out_ref, a_scr, b_scr, h_scr = refs
    else:
        u_ref, wa_ref, wx_ref, ba_ref, bx_ref, lam_ref, out_ref, a_scr, b_scr, h_scr = refs

    @pl.when(pl.program_id(1) == 0)
    def _():
        h_scr[...] = jnp.zeros_like(h_scr)

    u = u_ref[...]
    ub = u.astype(BF16)
    sp = _softplus(-lam_ref[...])
    for n in range(LRU_BLOCKS):
        sl = slice(n * LRU_BW, (n + 1) * LRU_BW)
        rg = jax.nn.sigmoid(_dot(ub[:, sl], wa_ref[n]) + ba_ref[:, sl])
        ig = jax.nn.sigmoid(_dot(ub[:, sl], wx_ref[n]) + bx_ref[:, sl])
        log_a = -LRU_C * rg * sp[:, sl]
        a = jnp.exp(log_a)
        a_scr[:, sl] = a
        b_scr[:, sl] = jnp.sqrt(-jnp.tanh(log_a) * (a * a + 1.0)) * (ig * u[:, sl])
    a = a_scr[...]
    b = b_scr[...]
    sub = jnp.bitwise_and(lax.broadcasted_iota(I32, (L, 1), 0), 7)
    for s in (1, 2, 4):
        if reverse:
            a_sh, b_sh, keep = pltpu.roll(a, L - s, 0), pltpu.roll(b, L - s, 0), sub < 8 - s
        else:
            a_sh, b_sh, keep = pltpu.roll(a, s, 0), pltpu.roll(b, s, 0), sub >= s
        b = jnp.where(keep, a * b_sh + b, b)
        a = jnp.where(keep, a * a_sh, a)
    a_scr[...] = a
    b_scr[...] = b
    ng = L // 8
    edge = 0 if reverse else 7

    def body(j, hc):
        gi = (ng - 1 - j) if reverse else j
        rows = pl.ds(pl.multiple_of(gi * 8, 8), 8)
        hg = b_scr[rows, :] + a_scr[rows, :] * hc
        b_scr[rows, :] = hg
        return hg[edge:edge + 1, :]

    h_scr[...] = lax.fori_loop(0, ng, body, h_scr[...])
    if final:
        out_ref[...] = ((b_scr[...] + hf_ref[...]) * glg_ref[...].astype(F32)).astype(BF16)
    else:
        out_ref[...] = b_scr[...]


def _lru(u, wa, wx, ba, bx, lam, bsz, t, ctx_len, reverse, final_args=None):
    L = LRU_TILE
    nc, ncc = t // L, ctx_len // L
    order = _chunk_order(reverse, ncc, nc)
    n = bsz * t
    row = lambda c: pl.BlockSpec((L, c), lambda b, i: (b * nc + order(i), 0))
    in_specs = [row(1024), _resident(wa.shape), _resident(wx.shape), _resident((1, 1024)),
                _resident((1, 1024)), _resident((1, 1024))]
    args = [u, wa, wx, ba, bx, lam]
    final = final_args is not None
    if final:
        in_specs += [row(1024), row(1024)]
        args += list(final_args)
    return pl.pallas_call(
        functools.partial(_lru_kernel, L, reverse, final),
        grid=(bsz, nc),
        in_specs=in_specs,
        out_specs=row(1024),
        out_shape=jax.ShapeDtypeStruct((n, 1024), BF16 if final else F32),
        scratch_shapes=[pltpu.VMEM((L, 1024), F32), pltpu.VMEM((L, 1024), F32), pltpu.VMEM((1, 1024), F32)],
        compiler_params=_cparams("arbitrary", "arbitrary"),
        name="rglru_bwd" if reverse else "rglru_fwd",
    )(*args)


def _out_kernel(tm, ctx_len, a1_ref, a2_ref, x_ref, modl_ref, modc_ref, g_ref, w_ref, xo_ref):
    is_ctx = _row_ids(pl.program_id(1), tm) < ctx_len
    half = w_ref.shape[0] // 2
    y = _dot(a1_ref[...], w_ref[0:half, :]) + _dot(a2_ref[...], w_ref[half:, :])
    xo_ref[...] = x_ref[...] + _mod(modl_ref, modc_ref, 2, is_ctx) * _rms(y, g_ref[...])


def _out_proj(a1, a2, x2, modl, modc, g, w, bsz, t, ctx_len):
    tm = ROW_TILE
    nt = t // tm
    row = lambda c: pl.BlockSpec((tm, c), lambda b, i: (b * nt + i, 0))
    return pl.pallas_call(
        functools.partial(_out_kernel, tm, ctx_len),
        grid=(bsz, nt),
        in_specs=[row(1024), row(1024), row(D_MODEL), pl.BlockSpec((1, 8, D_MODEL), lambda b, i: (b, 0, 0)),
                  _resident((8, D_MODEL)), _resident((1, D_MODEL)), _resident(w.shape)],
        out_specs=row(D_MODEL),
        out_shape=jax.ShapeDtypeStruct(x2.shape, F32),
        compiler_params=_cparams("arbitrary", "arbitrary"),
        name="out_proj",
    )(a1, a2, x2, modl, modc, g, w)


def _ffn_kernel(tm, ctx_len, x_ref, modl_ref, modc_ref, g2_ref, g3_ref, w1_ref, w2_ref, xo_ref):
    is_ctx = _row_ids(pl.program_id(1), tm) < ctx_len
    x = x_ref[...]
    h = _adaln(x, g2_ref[...], modl_ref, modc_ref, 3, is_ctx).astype(BF16)
    ff = w2_ref.shape[0]
    acc = jnp.zeros((tm, D_MODEL), F32)
    for j in range(ff // FF_STEP):
        lo = j * FF_STEP
        gj = _dot(h, w1_ref[:, lo:lo + FF_STEP])
        uj = _dot(h, w1_ref[:, ff + lo:ff + lo + FF_STEP])
        acc = acc + _dot((_silu(gj) * uj).astype(BF16), w2_ref[lo:lo + FF_STEP, :])
    xo_ref[...] = x + _mod(modl_ref, modc_ref, 5, is_ctx) * _rms(acc, g3_ref[...])


def _ffn(x2, modl, modc, g2, g3, w1, w2, bsz, t, ctx_len):
    tm = ROW_TILE
    nt = t // tm
    row = lambda c: pl.BlockSpec((tm, c), lambda b, i: (b * nt + i, 0))
    return pl.pallas_call(
        functools.partial(_ffn_kernel, tm, ctx_len),
        grid=(bsz, nt),
        in_specs=[row(D_MODEL), pl.BlockSpec((1, 8, D_MODEL), lambda b, i: (b, 0, 0)), _resident((8, D_MODEL)),
                  _resident((1, D_MODEL)), _resident((1, D_MODEL)), _resident(w1.shape), _resident(w2.shape)],
        out_specs=row(D_MODEL),
        out_shape=jax.ShapeDtypeStruct(x2.shape, F32),
        compiler_params=_cparams("arbitrary", "arbitrary"),
        name="dense_swiglu",
    )(x2, modl, modc, g2, g3, w1, w2)


def _in_cd_kernel(tm, ctx_len, x_ref, modl_ref, modc_ref, g_ref, w_ref, cw_ref, cb_ref, dtb_ref, aneg_ref,
                  gq_ref, gk_ref, gv_ref, gr_ref, z_ref, sx_ref, sb_ref, sc_ref, ga_ref, dts_ref):
    r = _row_ids(pl.program_id(1), tm)
    is_ctx = r < ctx_len
    h = _adaln(x_ref[...], g_ref[...], modl_ref, modc_ref, 0, is_ctx).astype(BF16)
    masks = _conv_masks(r, ctx_len)
    gq_ref[...] = (_dot(h, w_ref[:, 0:512]) * (G_DK ** -0.5)).astype(BF16)
    gk_ref[...] = _dot(h, w_ref[:, 512:1024]).astype(BF16)
    gv_ref[...] = _dot(h, w_ref[:, 1024:2048]).astype(BF16)
    gr_ref[...] = _silu(_dot(h, w_ref[:, 2048:3072])).astype(BF16)
    z_ref[...] = _silu(_dot(h, w_ref[:, 3072:4096])).astype(BF16)
    sx = _dwconv(_dot(h, w_ref[:, 4096:5120]), cw_ref[:, 0:1024], cb_ref[:, 0:1024], masks, tm)
    sx_ref[...] = _silu(sx).astype(BF16)
    sb = _dwconv(_dot(h, w_ref[:, 5120:5376]), cw_ref[:, 1024:1280], cb_ref[:, 1024:1280], masks, tm)
    sb_ref[...] = _silu(sb).astype(BF16)
    sc = _dwconv(_dot(h, w_ref[:, 5376:5632]), cw_ref[:, 1280:1536], cb_ref[:, 1280:1536], masks, tm)
    sc_ref[...] = _silu(sc).astype(BF16)
    ga_ref[...] = _dot(h, w_ref[:, 5632:5760])
    dt = _softplus(_dot(h, w_ref[:, 5760:5888]) + dtb_ref[...])
    lane = lax.broadcasted_iota(I32, dt.shape, 1)
    dts_ref[...] = jnp.where(lane < 2 * S_HEADS, dt, dt * aneg_ref[...])


def _in_cd(x2, modl, modc, g, w, cw, cb, dtb, aneg, bsz, t, ctx_len):
    tm = ROW_TILE
    nt = t // tm
    n = bsz * t
    row = lambda c: pl.BlockSpec((tm, c), lambda b, i: (b * nt + i, 0))
    outs = [(512, BF16), (512, BF16), (1024, BF16), (1024, BF16), (1024, BF16), (1024, BF16), (256, BF16),
            (256, BF16), (LANES, F32), (LANES, F32)]
    return pl.pallas_call(
        functools.partial(_in_cd_kernel, tm, ctx_len),
        grid=(bsz, nt),
        in_specs=[row(D_MODEL), pl.BlockSpec((1, 8, D_MODEL), lambda b, i: (b, 0, 0)), _resident((8, D_MODEL)),
                  _resident((1, D_MODEL)), _resident(w.shape), _resident(cw.shape), _resident(cb.shape),
                  _resident(dtb.shape), _resident(aneg.shape)],
        out_specs=[row(c) for c, _ in outs],
        out_shape=[jax.ShapeDtypeStruct((n, c), dt) for c, dt in outs],
        compiler_params=_cparams("arbitrary", "arbitrary"),
        name="in_proj_cd",
    )(x2, modl, modc, g, w, cw, cb, dtb, aneg)


def _gla_kernel(L, C, reverse, final, *refs):
    if final:
        q_ref, k_ref, v_ref, ga_ref, aw_ref, ab_ref, of_ref, gr_ref, nw_ref, out_ref, s_scr = refs
    else:
        q_ref, k_ref, v_ref, ga_ref, aw_ref, ab_ref, out_ref, s_scr = refs

    @pl.when(pl.program_id(1) == 0)
    def _():
        s_scr[...] = jnp.zeros_like(s_scr)

    mask = _tri_mask(L, reverse)
    mask01 = mask.astype(BF16)
    pre = jnp.dot(ga_ref[...], aw_ref[...], preferred_element_type=F32, precision=lax.Precision.HIGHEST)
    lg = _log_sigmoid(pre + ab_ref[...]) * (1.0 / G_TAU)
    ball = _sel_cols(mask01, lg)
    rows = lax.broadcasted_iota(I32, (L, 1), 0)
    nb = L // C
    last = 0 if reverse else L - 1
    for h in range(G_HEADS):
        ks = slice(h * G_DK, (h + 1) * G_DK)
        b = ball[:, ks]
        qh = q_ref[:, ks].astype(F32)
        kh = k_ref[:, ks].astype(F32)
        vh = v_ref[:, h * G_DV:(h + 1) * G_DV]
        st = s_scr[h]
        inter = _dot_nt((qh * jnp.exp(b)).astype(BF16), st.astype(BF16))
        blocks = []
        for i in range(nb):
            lo, hi = i * C, (i + 1) * C
            if reverse:
                bref = b[hi:hi + 1, :] if i < nb - 1 else jnp.zeros((1, G_DK), F32)
                visible = rows >= lo
            else:
                bref = b[lo - 1:lo, :] if i > 0 else jnp.zeros((1, G_DK), F32)
                visible = rows < hi
            qi = (qh[lo:hi, :] * jnp.exp(b[lo:hi, :] - bref)).astype(BF16)
            ki = (kh * jnp.exp(jnp.where(visible, bref - b, -jnp.inf))).astype(BF16)
            blocks.append(_dot_nt(qi, ki))
        att = jnp.where(mask, jnp.concatenate(blocks, axis=0), 0.0).astype(BF16)
        oh = inter + _dot(att, vh)
        bl = b[last:last + 1, :]
        kd = (kh * jnp.exp(bl - b)).astype(BF16)
        s_scr[h] = st * jnp.exp(bl) + _dot_tn(vh, kd)
        sl = slice(h * G_DV, (h + 1) * G_DV)
        if final:
            os_ = oh + of_ref[:, sl]
            out_ref[:, sl] = (_rms(os_, nw_ref[:, sl]) * gr_ref[:, sl].astype(F32)).astype(BF16)
        else:
            out_ref[:, sl] = oh


def _gla(q, k, v, ga, aw, ab, bsz, t, ctx_len, reverse, final_args=None):
    L, C = G_CHUNK, G_SUB
    nc, ncc = t // L, ctx_len // L
    order = _chunk_order(reverse, ncc, nc)
    n = bsz * t
    row = lambda c: pl.BlockSpec((L, c), lambda b, i: (b * nc + order(i), 0))
    in_specs = [row(512), row(512), row(1024), row(LANES), _resident(aw.shape), _resident(ab.shape)]
    args = [q, k, v, ga, aw, ab]
    final = final_args is not None
    if final:
        in_specs += [row(1024), row(1024), _resident((1, 1024))]
        args += list(final_args)
    return pl.pallas_call(
        functools.partial(_gla_kernel, L, C, reverse, final),
        grid=(bsz, nc),
        in_specs=in_specs,
        out_specs=row(1024),
        out_shape=jax.ShapeDtypeStruct((n, 1024), BF16 if final else F32),
        scratch_shapes=[pltpu.VMEM((G_HEADS, G_DV, G_DK), F32)],
        compiler_params=_cparams("arbitrary", "arbitrary"),
        name="gla_bwd" if reverse else "gla_fwd",
    )(*args)


def _ssd_kernel(L, reverse, final, *refs):
    if final:
        x_ref, b_ref, c_ref, gc_ref, gr_ref, yf_ref, z_ref, dsk_ref, nw_ref, out_ref, s_scr = refs
    else:
        x_ref, b_ref, c_ref, gc_ref, gr_ref, out_ref, s_scr = refs
    d = 1 if reverse else 0

    @pl.when(pl.program_id(1) == 0)
    def _():
        s_scr[...] = jnp.zeros_like(s_scr)

    mask = _tri_mask(L, reverse)
    mask01 = mask.astype(BF16)
    gc = gc_ref[...]
    gr = gr_ref[0]
    bcol = _sel_cols(mask01, gc)
    brow = _sel_rows(gr, mask01)
    last = 0 if reverse else L - 1
    dt_o, la_o = S_HEADS * d, 2 * S_HEADS + S_HEADS * d
    er = lax.broadcasted_iota(I32, (LANES, S_HEADS * S_P), 0)
    ec = jnp.right_shift(lax.broadcasted_iota(I32, (LANES, S_HEADS * S_P), 1), 6)
    sel_dt = (er == ec + dt_o).astype(BF16)
    sel_la = (er == ec + la_o).astype(BF16)
    x = x_ref[...].astype(F32)
    glane = lax.broadcasted_iota(I32, (L, LANES), 1)
    bla = jnp.where((glane >= la_o) & (glane < la_o + S_HEADS), bcol, 0.0)
    dtx = _expand(gc, sel_dt)
    ebx = _expand(jnp.exp(bla), sel_la)
    bl = bla[last:last + 1, :]
    wx = _expand(jnp.exp(bl - bla), sel_la) * dtx
    xdt = (x * dtx).astype(BF16)
    xw = (x * wx).astype(BF16)
    lane = lax.broadcasted_iota(I32, (L, 2 * S_P), 1)
    ys = []
    for g in range(S_GROUPS):
        ns = slice(g * S_N, (g + 1) * S_N)
        bg, cg = b_ref[:, ns], c_ref[:, ns]
        cb = _dot_nt(cg, bg)
        gs = slice(g * S_HPG * S_P, (g + 1) * S_HPG * S_P)
        st = s_scr[g]
        inter = _dot(cg, st.astype(BF16)) * ebx[:, gs]
        parts = []
        for pr in range(S_HPG // 2):
            e0 = g * S_HPG + 2 * pr
            xp = xdt[:, e0 * S_P:(e0 + 2) * S_P]
            res = []
            for e in (e0, e0 + 1):
                bc, br = bcol[:, la_o + e:la_o + e + 1], brow[la_o + e:la_o + e + 1, :]
                att = (cb * jnp.exp(jnp.where(mask, bc - br, -jnp.inf))).astype(BF16)
                res.append(_dot(att, xp))
            parts.append(jnp.where(lane < S_P, res[0], res[1]))
        y = jnp.concatenate(parts, axis=1) + inter
        s_scr[g] = st * ebx[last:last + 1, gs] + _dot_tn(bg, xw[:, gs])
        if final:
            ys.append(y + yf_ref[:, gs] + dsk_ref[:, gs] * x[:, gs])
        else:
            out_ref[:, gs] = y
    if final:
        yz = jnp.concatenate(ys, axis=1) * z_ref[...].astype(F32)
        out_ref[...] = _rms(yz, nw_ref[...]).astype(BF16)


def _ssd(x, bm, cm, gcol, grow, bsz, t, ctx_len, reverse, final_args=None):
    L = S_CHUNK
    nc, ncc = t // L, ctx_len // L
    order = _chunk_order(reverse, ncc, nc)
    n = bsz * t
    row = lambda c: pl.BlockSpec((L, c), lambda b, i: (b * nc + order(i), 0))
    in_specs = [row(1024), row(256), row(256), row(LANES),
                pl.BlockSpec((1, 4 * S_HEADS, L), lambda b, i: (b * nc + order(i), 0, 0))]
    args = [x, bm, cm, gcol, grow]
    final = final_args is not None
    if final:
        in_specs += [row(1024), row(1024), _resident((1, 1024)), _resident((1, 1024))]
        args += list(final_args)
    return pl.pallas_call(
        functools.partial(_ssd_kernel, L, reverse, final),
        grid=(bsz, nc),
        in_specs=in_specs,
        out_specs=row(1024),
        out_shape=jax.ShapeDtypeStruct((n, 1024), BF16 if final else F32),
        scratch_shapes=[pltpu.VMEM((S_GROUPS, S_N, S_HPG * S_P), F32)],
        compiler_params=_cparams("arbitrary", "arbitrary"),
        name="ssd_bwd" if reverse else "ssd_fwd",
    )(*args)


def _router_kernel(x_ref, modl_ref, g_ref, rw_ref, rb_ref, h_ref, idx_ref, gate_ref):
    h = _rms(x_ref[...], g_ref[...]) * (1.0 + modl_ref[0, 4:5, :]) + modl_ref[0, 3:4, :]
    h_ref[...] = h
    logits = jnp.dot(h, rw_ref[...], preferred_element_type=F32, precision=lax.Precision.HIGHEST) + rb_ref[...]
    lane = lax.broadcasted_iota(I32, logits.shape, 1)
    logits = jnp.where(lane < N_EXPERTS, logits, -jnp.inf)
    lanef = lane.astype(F32)
    m1 = jnp.max(logits, axis=1, keepdims=True)
    i1 = jnp.min(jnp.where(logits == m1, lanef, float(LANES)), axis=1, keepdims=True)
    rest = jnp.where(lanef == i1, -jnp.inf, logits)
    m2 = jnp.max(rest, axis=1, keepdims=True)
    i2 = jnp.min(jnp.where(rest == m2, lanef, float(LANES)), axis=1, keepdims=True)
    e = jnp.exp(m2 - m1)
    g1 = 1.0 / (1.0 + e)
    idx_ref[...] = jnp.where(lane == 0, i1, jnp.where(lane == 1, i2, 0.0)).astype(I32)
    gate_ref[...] = jnp.where(lane == 0, g1, jnp.where(lane == 1, e * g1, 0.0))


def _router(x2, modl, g, rw, rb, bsz, t, ctx_len):
    tm = ROW_TILE
    nt, nct = t // tm, ctx_len // tm
    nl = bsz * (t - ctx_len)
    nlt = nt - nct
    orow = lambda c: pl.BlockSpec((tm, c), lambda b, i: (b * nlt + i, 0))
    return pl.pallas_call(
        _router_kernel,
        grid=(bsz, nlt),
        in_specs=[pl.BlockSpec((tm, D_MODEL), lambda b, i: (b * nt + nct + i, 0)),
                  pl.BlockSpec((1, 8, D_MODEL), lambda b, i: (b, 0, 0)),
                  _resident((1, D_MODEL)), _resident(rw.shape), _resident(rb.shape)],
        out_specs=[orow(D_MODEL), orow(LANES), orow(LANES)],
        out_shape=[jax.ShapeDtypeStruct((nl, D_MODEL), F32), jax.ShapeDtypeStruct((nl, LANES), I32),
                   jax.ShapeDtypeStruct((nl, LANES), F32)],
        compiler_params=_cparams("arbitrary", "arbitrary"),
        name="moe_router",
    )(x2, modl, g, rw, rb)


def _gather_rows(table, idx):
    n_rows, d = idx.shape[0], table.shape[1]
    n_workers = SC_CORES * SC_SUBCORES
    per_worker = n_rows // n_workers
    chunk = SC_GATHER_ROWS
    assert per_worker * n_workers == n_rows and per_worker % chunk == 0
    mesh = plsc.VectorSubcoreMesh(core_axis_name="c", subcore_axis_name="s")

    @functools.partial(
        pl.kernel, mesh=mesh, out_type=jax.ShapeDtypeStruct((n_rows, d), table.dtype),
        scratch_types=[pltpu.VMEM((chunk,), I32), pltpu.VMEM((chunk, d), table.dtype), pltpu.SemaphoreType.DMA],
        name="sc_gather_rows")
    def gather(table_hbm, idx_hbm, out_hbm, idx_v, rows_v, sem):
        base = (lax.axis_index("s") * SC_CORES + lax.axis_index("c")) * per_worker

        @pl.loop(0, per_worker // chunk)
        def _(j):
            off = pl.multiple_of(base + j * chunk, 8)
            pltpu.sync_copy(idx_hbm.at[pl.ds(off, chunk)], idx_v)
            pltpu.async_copy(table_hbm.at[idx_v], rows_v, sem).wait()
            pltpu.sync_copy(rows_v, out_hbm.at[pl.ds(off, chunk)])

    return gather(table, idx)


def _moe_kernel(be_ref, nu_ref, x_ref, w1g_ref, w1u_ref, w2_ref, y_ref, acc_ref):
    i, j = pl.program_id(0), pl.program_id(1)

    @pl.when(i < nu_ref[0])
    def _():
        @pl.when(j == 0)
        def _():
            acc_ref[...] = jnp.zeros_like(acc_ref)

        x = x_ref[...].astype(BF16)
        a = (_silu(_dot(x, w1g_ref[0])) * _dot(x, w1u_ref[0])).astype(BF16)
        acc_ref[...] += _dot(a, w2_ref[0])

        @pl.when(j == pl.num_programs(1) - 1)
        def _():
            y_ref[...] = acc_ref[...]


def _moe_experts(xs, w1, w2, block_e, n_used, n_blocks):
    bm, fc = MOE_ROWS, MOE_FF
    nff = FF_EXPERT // fc
    used = lambda i, nu: jnp.minimum(i, nu[0] - 1)
    ffi = lambda i, j, nu: jnp.where(i < nu[0], j, nff - 1)
    grid_spec = pltpu.PrefetchScalarGridSpec(
        num_scalar_prefetch=2,
        grid=(n_blocks, nff),
        in_specs=[pl.BlockSpec((bm, D_MODEL), lambda i, j, be, nu: (used(i, nu), 0)),
                  pl.BlockSpec((1, D_MODEL, fc), lambda i, j, be, nu: (be[used(i, nu)], 0, ffi(i, j, nu))),
                  pl.BlockSpec((1, D_MODEL, fc), lambda i, j, be, nu: (be[used(i, nu)], 0, nff + ffi(i, j, nu))),
                  pl.BlockSpec((1, fc, D_MODEL), lambda i, j, be, nu: (be[used(i, nu)], ffi(i, j, nu), 0))],
        out_specs=pl.BlockSpec((bm, D_MODEL), lambda i, j, be, nu: (used(i, nu), 0)),
        scratch_shapes=[pltpu.VMEM((bm, D_MODEL), F32)],
    )
    return pl.pallas_call(
        _moe_kernel,
        grid_spec=grid_spec,
        out_shape=jax.ShapeDtypeStruct((n_blocks * bm, D_MODEL), F32),
        compiler_params=_cparams("arbitrary", "arbitrary"),
        name="moe_experts",
    )(block_e, n_used, xs, w1, w1, w2)


def _combine_kernel(y0_ref, y1_ref, gate_ref, x_ref, modl_ref, g_ref, o_ref):
    gt = gate_ref[...]
    y = gt[:, 0:1] * y0_ref[...] + gt[:, 1:2] * y1_ref[...]
    o_ref[...] = x_ref[...] + modl_ref[0, 5:6, :] * _rms(y, g_ref[...])


def _combine(yk, gates, x2, modl, g, bsz, t, ctx_len):
    tm = ROW_TILE
    nt, nct = t // tm, ctx_len // tm
    nlt = nt - nct
    nl = bsz * (t - ctx_len)
    orow = lambda c: pl.BlockSpec((tm, c), lambda b, i: (b * nlt + i, 0))
    second = pl.BlockSpec((tm, D_MODEL), lambda b, i: (nl // tm + b * nlt + i, 0))
    return pl.pallas_call(
        _combine_kernel,
        grid=(bsz, nlt),
        in_specs=[orow(D_MODEL), second, orow(LANES),
                  pl.BlockSpec((tm, D_MODEL), lambda b, i: (b * nt + nct + i, 0)),
                  pl.BlockSpec((1, 8, D_MODEL), lambda b, i: (b, 0, 0)), _resident((1, D_MODEL))],
        out_specs=orow(D_MODEL),
        out_shape=jax.ShapeDtypeStruct((nl, D_MODEL), F32),
        compiler_params=_cparams("arbitrary", "arbitrary"),
        name="moe_combine",
    )(yk, yk, gates, x2, modl, g)


def _routing_tables(top_idx, n_tok):
    bm = MOE_ROWS
    flat_e = top_idx.reshape(-1)
    n_assign = flat_e.shape[0]
    onehot = (flat_e[:, None] == jnp.arange(N_EXPERTS, dtype=I32)[None, :]).astype(I32)
    incl = jnp.cumsum(onehot, axis=0)
    counts = incl[-1]
    rank = jnp.take_along_axis(incl, flat_e[:, None], axis=1)[:, 0] - 1
    padded = (counts + bm - 1) // bm * bm
    ends_p = jnp.cumsum(padded)
    pstarts = ends_p - padded
    slot = (pstarts[flat_e] + rank).astype(I32)
    n_blocks = n_assign // bm + N_EXPERTS
    flat_tok = jnp.arange(n_assign, dtype=I32) // TOP_K
    slot_tok = jnp.zeros((n_blocks * bm,), I32).at[slot].set(flat_tok)
    block_e = jnp.minimum(jnp.searchsorted(ends_p, jnp.arange(n_blocks, dtype=I32) * bm, side='right'),
                          N_EXPERTS - 1).astype(I32)
    n_used = (ends_p[-1] // bm).astype(I32).reshape(1)
    return slot, slot_tok, block_e, n_used, n_blocks


def _row_layout(gcol, ncols, L):
    n = gcol.shape[0]
    return gcol[:, :ncols].reshape(n // L, L, ncols).transpose(0, 2, 1)


def kernel(x, c, ctx, c_ctx, mod_w, mod_b, norm_g, ab_in_w, m_conv_w, m_conv_b, m_gate_b, m_norm_w, l_conv_w,
           l_conv_b, l_wa, l_ba, l_wx, l_bx, l_lam, ab_out_w, ffn_w1, ffn_w2, cd_in_w, g_alpha_w, g_alpha_b,
           g_norm_w, s_conv_w, s_conv_b, s_dt_bias, s_A_log, s_D, s_norm_w, cd_out_w, router_w, router_b,
           moe_w1, moe_w2):
    bsz, seq, dm = x.shape
    ctx_len = ctx.shape[1]
    t = ctx_len + seq
    n = bsz * t
    x2 = jnp.concatenate([ctx, x], axis=1).reshape(n, dm)

    c_all = jnp.concatenate([c, c_ctx[None, :], jnp.zeros((-(bsz + 1) % 8, dm), F32)], axis=0)
    mods = _modulation(c_all, mod_w, mod_b).reshape(mod_w.shape[0], c_all.shape[0], 6, dm)
    pad2 = jnp.zeros((bsz, 2, dm), F32)

    def layer_mods(layer):
        modl = jnp.concatenate([mods[layer, :bsz], pad2], axis=1)
        modc = jnp.concatenate([mods[layer, bsz], pad2[0]], axis=0)
        return modl, modc

    modl, modc = layer_mods(0)
    g = norm_g[0]
    w = ab_in_w[0]
    o_qk, o_v, o_o, o_gt, o_lx, o_lg = 0, 1024, 2048, 3072, 3088, 4112
    w_in = jnp.concatenate([w[:, o_qk:o_v], w[:, o_v:o_o], w[:, o_o:o_gt], w[:, o_lx:o_lg], w[:, o_lg:],
                            w[:, o_gt:o_lx], jnp.zeros((dm, LANES - 16), F32)], axis=1).astype(BF16)
    gb = jnp.concatenate([m_gate_b[0].reshape(1, 16), jnp.zeros((1, LANES - 16), F32)], axis=1)
    q, k, v, og, lx, glg, gates = _in_ab(x2, modl, modc, g[0:1], w_in, m_conv_w[0], m_conv_b[0][None],
                                         l_conv_w[0], l_conv_b[0][None], gb, bsz, t, ctx_len)
    grow = _row_layout(gates, 16, M_CHUNK)
    hm_f = _mlstm(q, k, v, gates, grow, bsz, t, ctx_len, False)
    hm = _mlstm(q, k, v, gates, grow, bsz, t, ctx_len, True, (hm_f, og, m_norm_w[0][None]))
    lru_args = lambda d: (l_wa[0, d].astype(BF16), l_wx[0, d].astype(BF16), l_ba[0, d][None], l_bx[0, d][None],
                          l_lam[0, d][None])
    hl_f = _lru(lx, *lru_args(0), bsz, t, ctx_len, False)
    hl = _lru(lx, *lru_args(1), bsz, t, ctx_len, True, (hl_f, glg))
    x2 = _out_proj(hm, hl, x2, modl, modc, g[1:2], ab_out_w[0].astype(BF16), bsz, t, ctx_len)
    x2 = _ffn(x2, modl, modc, g[2:3], g[3:4], ffn_w1[0].astype(BF16), ffn_w2[0].astype(BF16), bsz, t, ctx_len)

    modl, modc = layer_mods(1)
    g = norm_g[1]
    w = cd_in_w[0]
    o_gv, o_gr, o_ga, o_z, o_xbc, o_dt = 1024, 2048, 3072, 3104, 4128, 5664
    zpad = jnp.zeros((dm, LANES - 32), F32)
    w_in = jnp.concatenate([w[:, :o_gv], w[:, o_gv:o_gr], w[:, o_gr:o_ga], w[:, o_z:o_xbc], w[:, o_xbc:o_dt],
                            w[:, o_ga:o_z], zpad, w[:, o_dt:], w[:, o_dt:], jnp.zeros((dm, LANES - 64), F32)],
                           axis=1).astype(BF16)
    dtb = jnp.concatenate([s_dt_bias[0].reshape(1, 32), s_dt_bias[0].reshape(1, 32),
                           jnp.zeros((1, LANES - 64), F32)], axis=1)
    aneg = jnp.concatenate([jnp.zeros((1, 32), F32), -jnp.exp(s_A_log[0].reshape(1, 32)),
                            jnp.zeros((1, LANES - 64), F32)], axis=1)
    gq, gk, gv, grs, zs, sx, sb, sc, ga, dts = _in_cd(x2, modl, modc, g[0:1], w_in, s_conv_w[0], s_conv_b[0][None],
                                                      dtb, aneg, bsz, t, ctx_len)

    def alpha(d):
        aw = jnp.zeros((LANES, G_HEADS * G_DK), F32).at[d * G_RANK:(d + 1) * G_RANK].set(g_alpha_w[0, d])
        return aw, g_alpha_b[0, d][None]

    og_f = _gla(gq, gk, gv, ga, *alpha(0), bsz, t, ctx_len, False)
    og2 = _gla(gq, gk, gv, ga, *alpha(1), bsz, t, ctx_len, True, (og_f, grs, g_norm_w[0][None]))
    drow = _row_layout(dts, 4 * S_HEADS, S_CHUNK)
    ys_f = _ssd(sx, sb, sc, dts, drow, bsz, t, ctx_len, False)
    dskip = jnp.repeat(s_D[0], S_P)[None]
    ys = _ssd(sx, sb, sc, dts, drow, bsz, t, ctx_len, True, (ys_f, zs, dskip, s_norm_w[0][None]))
    x2 = _out_proj(og2, ys, x2, modl, modc, g[1:2], cd_out_w[0].astype(BF16), bsz, t, ctx_len)

    rw = jnp.concatenate([router_w[0], jnp.zeros((dm, LANES - N_EXPERTS), F32)], axis=1)
    rb = jnp.concatenate([router_b[0][None], jnp.zeros((1, LANES - N_EXPERTS), F32)], axis=1)
    h2, top_idx, top_gate = _router(x2, modl, g[2:3], rw, rb, bsz, t, ctx_len)
    n_tok = bsz * seq
    slot, slot_tok, block_e, n_used, n_blocks = _routing_tables(top_idx[:, :TOP_K], n_tok)
    xs = _gather_rows(h2, slot_tok)
    ys_e = _moe_experts(xs, moe_w1[0].astype(BF16), moe_w2[0].astype(BF16), block_e, n_used, n_blocks)
    slot2 = slot.reshape(n_tok, TOP_K)
    yk = _gather_rows(ys_e, jnp.concatenate([slot2[:, 0], slot2[:, 1]]))
    out = _combine(yk, top_gate, x2, modl, g[3:4], bsz, t, ctx_len)
    return out.reshape(bsz, seq, dm)
```

```python
import functools

import jax
import jax.numpy as jnp
from jax import lax
from jax.experimental import pallas as pl
from jax.experimental.pallas import tpu as pltpu
from jax.experimental.pallas import tpu_sc as plsc

F32, BF16, I32 = jnp.float32, jnp.bfloat16, jnp.int32

D_MODEL = 1024
GRID_W = 64
EPS = 1e-6
M_HEADS, M_DK, M_DV = 4, 128, 256
LRU_BLOCKS, LRU_BW, LRU_C = 8, 128, 8.0
G_HEADS, G_DK, G_DV, G_RANK, G_TAU = 4, 128, 256, 16, 16.0
S_HEADS, S_P, S_N, S_GROUPS, S_HPG = 16, 64, 128, 2, 8
FF_DENSE = 2816
N_EXPERTS, TOP_K, FF_EXPERT = 8, 2, 3584

LANES = 128
VMEM_LIMIT = 56 * 1024 * 1024
ROW_TILE = 256
M_CHUNK = 256
G_CHUNK, G_SUB = 64, 16
S_CHUNK = 64
LRU_TILE = 256
FF_STEP = 256
MOE_ROWS = 1024
MOE_FF = 512
SC_CORES, SC_SUBCORES = 2, 16
SC_GATHER_ROWS = 64


def _cparams(*sem):
    return pltpu.CompilerParams(dimension_semantics=sem, vmem_limit_bytes=VMEM_LIMIT)


def _silu(x):
    return x * jax.nn.sigmoid(x)


def _softplus(x):
    return jnp.maximum(x, 0.0) + jnp.log1p(jnp.exp(-jnp.abs(x)))


def _log_sigmoid(x):
    return jnp.minimum(x, 0.0) - jnp.log1p(jnp.exp(-jnp.abs(x)))


def _dot(a, b):
    return jnp.dot(a, b, preferred_element_type=F32)


def _dot_nt(a, b):
    return lax.dot_general(a, b, (((1,), (1,)), ((), ())), preferred_element_type=F32)


def _dot_tn(a, b):
    return lax.dot_general(a, b, (((0,), (0,)), ((), ())), preferred_element_type=F32)


def _split3(f):
    f1 = f.astype(BF16)
    r = f - f1.astype(F32)
    f2 = r.astype(BF16)
    f3 = (r - f2.astype(F32)).astype(BF16)
    return f1, f2, f3


def _sel_cols(mask01, f):
    p1, p2, p3 = _split3(f)
    return _dot(mask01, p1) + _dot(mask01, p2) + _dot(mask01, p3)


def _sel_rows(f, mask01):
    p1, p2, p3 = _split3(f)
    return _dot_nt(p1, mask01) + _dot_nt(p2, mask01) + _dot_nt(p3, mask01)


def _expand(f, sel01):
    p1, p2, p3 = _split3(f)
    return _dot(p1, sel01) + _dot(p2, sel01) + _dot(p3, sel01)


def _rms(u, g):
    return u * lax.rsqrt(jnp.mean(u * u, axis=-1, keepdims=True) + EPS) * g


def _head_rms(u, g, n_heads):
    w = u.shape[-1] // n_heads
    return jnp.concatenate([_rms(u[:, h * w:(h + 1) * w], g[:, h * w:(h + 1) * w]) for h in range(n_heads)], axis=1)


def _row_ids(tile_idx, tm):
    return tile_idx * tm + lax.broadcasted_iota(I32, (tm, 1), 0)


def _mod(modl_ref, modc_ref, idx, is_ctx):
    return jnp.where(is_ctx, modc_ref[idx:idx + 1, :], modl_ref[0, idx:idx + 1, :])


def _adaln(x, g, modl_ref, modc_ref, shift_idx, is_ctx):
    shift = _mod(modl_ref, modc_ref, shift_idx, is_ctx)
    scale = _mod(modl_ref, modc_ref, shift_idx + 1, is_ctx)
    return _rms(x, g) * (1.0 + scale) + shift


def _conv_masks(r, ctx_len):
    in_lat = r >= ctx_len
    pos = jnp.where(in_lat, jnp.bitwise_and(r - ctx_len, GRID_W - 1), r)
    seg = jnp.where(in_lat, GRID_W, ctx_len)
    return pos >= 2, pos >= 1, pos <= seg - 2


def _dwconv(y, cw, cb, masks, tm):
    m2, m1, p1 = masks
    ym2 = jnp.where(m2, pltpu.roll(y, 2, 0), 0.0)
    ym1 = jnp.where(m1, pltpu.roll(y, 1, 0), 0.0)
    yp1 = jnp.where(p1, pltpu.roll(y, tm - 1, 0), 0.0)
    return cb + cw[0:1, :] * ym2 + cw[1:2, :] * ym1 + cw[2:3, :] * y + cw[3:4, :] * yp1


def _resident(shape):
    nd = len(shape)
    return pl.BlockSpec(shape, lambda *_: (0,) * nd)


def _mod_kernel(c_ref, w_ref, b_ref, o_ref):
    a = _silu(c_ref[...])
    o_ref[0] = jnp.dot(a, w_ref[0], preferred_element_type=F32, precision=lax.Precision.HIGHEST) + b_ref[0]


def _modulation(c_all, mod_w, mod_b):
    depth, d, n6 = mod_w.shape
    rows = c_all.shape[0]
    tn = 1024
    return pl.pallas_call(
        _mod_kernel,
        grid=(depth, n6 // tn),
        in_specs=[pl.BlockSpec((rows, d), lambda l, j: (0, 0)),
                  pl.BlockSpec((1, d, tn), lambda l, j: (l, 0, j)),
                  pl.BlockSpec((1, 1, tn), lambda l, j: (l, 0, j))],
        out_specs=pl.BlockSpec((1, rows, tn), lambda l, j: (l, 0, j)),
        out_shape=jax.ShapeDtypeStruct((depth, rows, n6), F32),
        compiler_params=_cparams("arbitrary", "arbitrary"),
        name="modulation",
    )(c_all, mod_w, mod_b.reshape(depth, 1, n6))


def _in_ab_kernel(tm, ctx_len, x_ref, modl_ref, modc_ref, g_ref, w_ref, mcw_ref, mcb_ref, lcw_ref, lcb_ref,
                  gb_ref, q_ref, k_ref, v_ref, og_ref, lx_ref, glg_ref, gates_ref):
    r = _row_ids(pl.program_id(1), tm)
    is_ctx = r < ctx_len
    h = _adaln(x_ref[...], g_ref[...], modl_ref, modc_ref, 0, is_ctx).astype(BF16)
    masks = _conv_masks(r, ctx_len)
    q = _dwconv(_dot(h, w_ref[:, 0:512]), mcw_ref[:, 0:512], mcb_ref[:, 0:512], masks, tm)
    q_ref[...] = _silu(q).astype(BF16)
    k = _dwconv(_dot(h, w_ref[:, 512:1024]), mcw_ref[:, 512:1024], mcb_ref[:, 512:1024], masks, tm)
    k_ref[...] = (_silu(k) * (M_DK ** -0.5)).astype(BF16)
    v_ref[...] = _dot(h, w_ref[:, 1024:2048]).astype(BF16)
    og_ref[...] = jax.nn.sigmoid(_dot(h, w_ref[:, 2048:3072])).astype(BF16)
    lx = _dwconv(_dot(h, w_ref[:, 3072:4096]), lcw_ref[...], lcb_ref[...], masks, tm)
    lx_ref[...] = lx
    glg_ref[...] = jax.nn.gelu(_dot(h, w_ref[:, 4096:5120])).astype(BF16)
    gt = _dot(h, w_ref[:, 5120:5248]) + gb_ref[...]
    lane = lax.broadcasted_iota(I32, gt.shape, 1)
    gates_ref[...] = jnp.where(lane >= 2 * M_HEADS, _log_sigmoid(gt), gt)


def _in_ab(x2, modl, modc, g, w, mcw, mcb, lcw, lcb, gb, bsz, t, ctx_len):
    tm = ROW_TILE
    nt = t // tm
    n = bsz * t
    row = lambda c: pl.BlockSpec((tm, c), lambda b, i: (b * nt + i, 0))
    outs = [(512, BF16), (512, BF16), (1024, BF16), (1024, BF16), (1024, F32), (1024, BF16), (LANES, F32)]
    return pl.pallas_call(
        functools.partial(_in_ab_kernel, tm, ctx_len),
        grid=(bsz, nt),
        in_specs=[row(D_MODEL),
                  pl.BlockSpec((1, 8, D_MODEL), lambda b, i: (b, 0, 0)),
                  _resident((8, D_MODEL)), _resident((1, D_MODEL)), _resident(w.shape),
                  _resident(mcw.shape), _resident(mcb.shape), _resident(lcw.shape), _resident(lcb.shape),
                  _resident(gb.shape)],
        out_specs=[row(c) for c, _ in outs],
        out_shape=[jax.ShapeDtypeStruct((n, c), dt) for c, dt in outs],
        compiler_params=_cparams("arbitrary", "arbitrary"),
        name="in_proj_ab",
    )(x2, modl, modc, g, w, mcw, mcb, lcw, lcb, gb)


def _reverse_order(n_ctx_chunks, n_chunks):
    return lambda c: jnp.where(c < n_ctx_chunks, n_ctx_chunks - 1 - c, n_chunks - 1 - (c - n_ctx_chunks))


def _tri_mask(L, reverse):
    row = lax.broadcasted_iota(I32, (L, L), 0)
    col = lax.broadcasted_iota(I32, (L, L), 1)
    return (col >= row) if reverse else (col <= row)


def _bidir_scan(kern, L, row_ins, row3_ins, res_ins, out_cols, scratch, bsz, t, ctx_len, name):
    nc = t // L
    rev = _reverse_order(ctx_len // L, nc)
    fwd = lambda i: i
    n = bsz * t

    def row(c, order):
        return pl.BlockSpec((L, c), lambda b, i: (b * nc + order(i), 0))

    def row3(shape, order):
        return pl.BlockSpec((1,) + tuple(shape[1:]), lambda b, i: (b * nc + order(i), 0, 0))

    in_specs, args = [], []
    for order in (fwd, rev):
        in_specs += [row(a.shape[1], order) for a in row_ins] + [row3(a.shape, order) for a in row3_ins]
        args += list(row_ins) + list(row3_ins)
    in_specs += [_resident(a.shape) for a in res_ins]
    args += list(res_ins)
    return pl.pallas_call(
        kern,
        grid=(bsz, nc),
        in_specs=in_specs,
        out_specs=[row(c, fwd) for c in out_cols] + [row(c, rev) for c in out_cols],
        out_shape=[jax.ShapeDtypeStruct((n, c), BF16) for c in out_cols] * 2,
        scratch_shapes=scratch,
        compiler_params=_cparams("arbitrary", "arbitrary"),
        name=name,
    )(*args)


def _zero_at_start(*scratch):
    @pl.when(pl.program_id(1) == 0)
    def _():
        for s in scratch:
            s[...] = jnp.zeros_like(s)


def _mlstm_dir(L, d, q_ref, k_ref, v_ref, gc_ref, gr_ref, out_ref, c_scr, m_scr):
    reverse = d == 1
    mask = _tri_mask(L, reverse)
    mask01 = mask.astype(BF16)
    gc = gc_ref[...]
    gr = gr_ref[0]
    bcol = _sel_cols(mask01, gc)
    brow = _sel_rows(gr, mask01)
    last = 0 if reverse else L - 1
    ones_col = (lax.broadcasted_iota(I32, (L, LANES), 1) == 0).astype(BF16)
    for h in range(M_HEADS):
        fo, io = 2 * M_HEADS + M_HEADS * d + h, M_HEADS * d + h
        bc, br = bcol[:, fo:fo + 1], brow[fo:fo + 1, :]
        ic, ir = gc[:, io:io + 1], gr[io:io + 1, :]
        m = m_scr[d, h:h + 1, 0:1]
        g = bc + m
        dlog = jnp.where(mask, bc - br + ir, -jnp.inf)
        mt = jnp.maximum(g, jnp.max(dlog, axis=1, keepdims=True))
        w_inter = jnp.exp(g - mt)
        p = jnp.exp(dlog - mt)
        qh = q_ref[:, h * M_DK:(h + 1) * M_DK]
        kh = k_ref[:, h * M_DK:(h + 1) * M_DK]
        vx = jnp.concatenate([v_ref[:, h * M_DV:(h + 1) * M_DV], ones_col], axis=1)
        sc = (_dot_nt(qh, kh) * p).astype(BF16)
        cx = c_scr[d, h]
        numx = w_inter * _dot(qh, cx.astype(BF16)) + _dot(sc, vx)
        den = numx[:, M_DV:M_DV + 1]
        hh = numx[:, 0:M_DV] * (1.0 / jnp.maximum(jnp.abs(den), jnp.exp(-mt)))
        bl = bc[last:last + 1, :]
        a_s = bl - bc + ic
        m_new = jnp.maximum(bl + m, jnp.max(a_s, axis=0, keepdims=True))
        decay = jnp.exp(bl + m - m_new)
        ws = jnp.exp(a_s - m_new)
        kw = (kh.astype(F32) * ws).astype(BF16)
        c_scr[d, h] = decay * cx + _dot_tn(kw, vx)
        m_scr[d, h:h + 1, 0:1] = m_new
        out_ref[:, h * M_DV:(h + 1) * M_DV] = hh.astype(BF16)


def _mlstm_kernel(L, qf, kf, vf, gcf, grf, qb, kb, vb, gcb, grb, of, ob, c_scr, m_scr):
    _zero_at_start(c_scr, m_scr)
    _mlstm_dir(L, 0, qf, kf, vf, gcf, grf, of, c_scr, m_scr)
    _mlstm_dir(L, 1, qb, kb, vb, gcb, grb, ob, c_scr, m_scr)


def _mlstm(q, k, v, gcol, grow, bsz, t, ctx_len):
    L = M_CHUNK
    scratch = [pltpu.VMEM((2, M_HEADS, M_DK, M_DV + LANES), F32), pltpu.VMEM((2, 8, LANES), F32)]
    return _bidir_scan(functools.partial(_mlstm_kernel, L), L, [q, k, v, gcol], [grow], [], [1024], scratch,
                       bsz, t, ctx_len, "mlstm_scan")


def _lru_gates(L, d, u_ref, wa_ref, wx_ref, ba_ref, bx_ref, lam_ref, a_scr, b_scr):
    reverse = d == 1
    u = u_ref[...]
    ub = u.astype(BF16)
    sp = _softplus(-lam_ref[d])
    for n in range(LRU_BLOCKS):
        sl = slice(n * LRU_BW, (n + 1) * LRU_BW)
        rg = jax.nn.sigmoid(_dot(ub[:, sl], wa_ref[d, n]) + ba_ref[d, :, sl])
        ig = jax.nn.sigmoid(_dot(ub[:, sl], wx_ref[d, n]) + bx_ref[d, :, sl])
        log_a = -LRU_C * rg * sp[:, sl]
        a = jnp.exp(log_a)
        a_scr[d, :, sl] = a
        b_scr[d, :, sl] = jnp.sqrt(-jnp.tanh(log_a) * (a * a + 1.0)) * (ig * u[:, sl])
    a = a_scr[d]
    b = b_scr[d]
    sub = jnp.bitwise_and(lax.broadcasted_iota(I32, (L, 1), 0), 7)
    for s in (1, 2, 4):
        if reverse:
            a_sh, b_sh, keep = pltpu.roll(a, L - s, 0), pltpu.roll(b, L - s, 0), sub < 8 - s
        else:
            a_sh, b_sh, keep = pltpu.roll(a, s, 0), pltpu.roll(b, s, 0), sub >= s
        b = jnp.where(keep, a * b_sh + b, b)
        a = jnp.where(keep, a * a_sh, a)
    a_scr[d] = a
    b_scr[d] = b


def _lru_kernel(L, uf, ub, wa_ref, wx_ref, ba_ref, bx_ref, lam_ref, of, ob, a_scr, b_scr, h_scr):
    _zero_at_start(h_scr)
    _lru_gates(L, 0, uf, wa_ref, wx_ref, ba_ref, bx_ref, lam_ref, a_scr, b_scr)
    _lru_gates(L, 1, ub, wa_ref, wx_ref, ba_ref, bx_ref, lam_ref, a_scr, b_scr)
    ng = L // 8

    def body(j, carry):
        hf, hb = carry
        rf = pl.ds(pl.multiple_of(j * 8, 8), 8)
        rb = pl.ds(pl.multiple_of((ng - 1 - j) * 8, 8), 8)
        gf = b_scr[0, rf, :] + a_scr[0, rf, :] * hf
        gb = b_scr[1, rb, :] + a_scr[1, rb, :] * hb
        b_scr[0, rf, :] = gf
        b_scr[1, rb, :] = gb
        return gf[7:8, :], gb[0:1, :]

    hf, hb = lax.fori_loop(0, ng, body, (h_scr[0:1, :], h_scr[1:2, :]))
    h_scr[0:1, :] = hf
    h_scr[1:2, :] = hb
    of[...] = b_scr[0].astype(BF16)
    ob[...] = b_scr[1].astype(BF16)


def _lru(u, wa, wx, ba, bx, lam, bsz, t, ctx_len):
    L = LRU_TILE
    scratch = [pltpu.VMEM((2, L, 1024), F32), pltpu.VMEM((2, L, 1024), F32), pltpu.VMEM((8, 1024), F32)]
    return _bidir_scan(functools.partial(_lru_kernel, L), L, [u], [], [wa, wx, ba, bx, lam], [1024], scratch,
                       bsz, t, ctx_len, "rglru_scan")


def _out_tail(tm, ctx_len, a1, a2, x_ref, modl_ref, modc_ref, g_ref, w_ref, xo_ref):
    is_ctx = _row_ids(pl.program_id(1), tm) < ctx_len
    half = w_ref.shape[0] // 2
    y = _dot(a1.astype(BF16), w_ref[0:half, :]) + _dot(a2.astype(BF16), w_ref[half:, :])
    xo_ref[...] = x_ref[...] + _mod(modl_ref, modc_ref, 2, is_ctx) * _rms(y, g_ref[...])


def _out_ab_kernel(tm, ctx_len, hf, hb, og, lf, lb, glg, nw, x_ref, modl_ref, modc_ref, g_ref, w_ref, xo_ref):
    hm = _head_rms(hf[...].astype(F32) + hb[...].astype(F32), nw[...], M_HEADS) * og[...].astype(F32)
    hl = (lf[...].astype(F32) + lb[...].astype(F32)) * glg[...].astype(F32)
    _out_tail(tm, ctx_len, hm, hl, x_ref, modl_ref, modc_ref, g_ref, w_ref, xo_ref)


def _out_cd_kernel(tm, ctx_len, gf, gb, grs, yf, yb, sx, zs, gnw, dsk, snw, x_ref, modl_ref, modc_ref, g_ref,
                   w_ref, xo_ref):
    og = _head_rms(gf[...].astype(F32) + gb[...].astype(F32), gnw[...], G_HEADS) * grs[...].astype(F32)
    ys = yf[...].astype(F32) + yb[...].astype(F32) + dsk[...] * sx[...].astype(F32)
    ys = _rms(ys * zs[...].astype(F32), snw[...])
    _out_tail(tm, ctx_len, og, ys, x_ref, modl_ref, modc_ref, g_ref, w_ref, xo_ref)


def _out_proj(kern, name, acts, vecs, x2, modl, modc, g, w, bsz, t, ctx_len):
    tm = ROW_TILE
    nt = t // tm
    row = lambda c: pl.BlockSpec((tm, c), lambda b, i: (b * nt + i, 0))
    return pl.pallas_call(
        functools.partial(kern, tm, ctx_len),
        grid=(bsz, nt),
        in_specs=[row(1024)] * len(acts) + [_resident((1, 1024))] * len(vecs)
        + [row(D_MODEL), pl.BlockSpec((1, 8, D_MODEL), lambda b, i: (b, 0, 0)),
           _resident((8, D_MODEL)), _resident((1, D_MODEL)), _resident(w.shape)],
        out_specs=row(D_MODEL),
        out_shape=jax.ShapeDtypeStruct(x2.shape, F32),
        compiler_params=_cparams("arbitrary", "arbitrary"),
        name=name,
    )(*acts, *vecs, x2, modl, modc, g, w)


def _ffn_kernel(tm, ctx_len, x_ref, modl_ref, modc_ref, g2_ref, g3_ref, w1_ref, w2_ref, xo_ref):
    is_ctx = _row_ids(pl.program_id(1), tm) < ctx_len
    x = x_ref[...]
    h = _adaln(x, g2_ref[...], modl_ref, modc_ref, 3, is_ctx).astype(BF16)
    ff = w2_ref.shape[0]
    acc = jnp.zeros((tm, D_MODEL), F32)
    for j in range(ff // FF_STEP):
        lo = j * FF_STEP
        gj = _dot(h, w1_ref[:, lo:lo + FF_STEP])
        uj = _dot(h, w1_ref[:, ff + lo:ff + lo + FF_STEP])
        acc = acc + _dot((_silu(gj) * uj).astype(BF16), w2_ref[lo:lo + FF_STEP, :])
    xo_ref[...] = x + _mod(modl_ref, modc_ref, 5, is_ctx) * _rms(acc, g3_ref[...])


def _ffn(x2, modl, modc, g2, g3, w1, w2, bsz, t, ctx_len):
    tm = ROW_TILE
    nt = t // tm
    row = lambda c: pl.BlockSpec((tm, c), lambda b, i: (b * nt + i, 0))
    return pl.pallas_call(
        functools.partial(_ffn_kernel, tm, ctx_len),
        grid=(bsz, nt),
        in_specs=[row(D_MODEL), pl.BlockSpec((1, 8, D_MODEL), lambda b, i: (b, 0, 0)), _resident((8, D_MODEL)),
                  _resident((1, D_MODEL)), _resident((1, D_MODEL)), _resident(w1.shape), _resident(w2.shape)],
        out_specs=row(D_MODEL),
        out_shape=jax.ShapeDtypeStruct(x2.shape, F32),
        compiler_params=_cparams("arbitrary", "arbitrary"),
        name="dense_swiglu",
    )(x2, modl, modc, g2, g3, w1, w2)


def _in_cd_kernel(tm, ctx_len, x_ref, modl_ref, modc_ref, g_ref, w_ref, cw_ref, cb_ref, dtb_ref, aneg_ref,
                  gq_ref, gk_ref, gv_ref, gr_ref, z_ref, sx_ref, sb_ref, sc_ref, ga_ref, dts_ref):
    r = _row_ids(pl.program_id(1), tm)
    is_ctx = r < ctx_len
    h = _adaln(x_ref[...], g_ref[...], modl_ref, modc_ref, 0, is_ctx).astype(BF16)
    masks = _conv_masks(r, ctx_len)
    gq_ref[...] = (_dot(h, w_ref[:, 0:512]) * (G_DK ** -0.5)).astype(BF16)
    gk_ref[...] = _dot(h, w_ref[:, 512:1024]).astype(BF16)
    gv_ref[...] = _dot(h, w_ref[:, 1024:2048]).astype(BF16)
    gr_ref[...] = _silu(_dot(h, w_ref[:, 2048:3072])).astype(BF16)
    z_ref[...] = _silu(_dot(h, w_ref[:, 3072:4096])).astype(BF16)
    sx = _dwconv(_dot(h, w_ref[:, 4096:5120]), cw_ref[:, 0:1024], cb_ref[:, 0:1024], masks, tm)
    sx_ref[...] = _silu(sx).astype(BF16)
    sb = _dwconv(_dot(h, w_ref[:, 5120:5376]), cw_ref[:, 1024:1280], cb_ref[:, 1024:1280], masks, tm)
    sb_ref[...] = _silu(sb).astype(BF16)
    sc = _dwconv(_dot(h, w_ref[:, 5376:5632]), cw_ref[:, 1280:1536], cb_ref[:, 1280:1536], masks, tm)
    sc_ref[...] = _silu(sc).astype(BF16)
    ga_ref[...] = _dot(h, w_ref[:, 5632:5760])
    dt = _softplus(_dot(h, w_ref[:, 5760:5888]) + dtb_ref[...])
    lane = lax.broadcasted_iota(I32, dt.shape, 1)
    dts_ref[...] = jnp.where(lane < 2 * S_HEADS, dt, dt * aneg_ref[...])


def _in_cd(x2, modl, modc, g, w, cw, cb, dtb, aneg, bsz, t, ctx_len):
    tm = ROW_TILE
    nt = t // tm
    n = bsz * t
    row = lambda c: pl.BlockSpec((tm, c), lambda b, i: (b * nt + i, 0))
    outs = [(512, BF16), (512, BF16), (1024, BF16), (1024, BF16), (1024, BF16), (1024, BF16), (256, BF16),
            (256, BF16), (LANES, F32), (LANES, F32)]
    return pl.pallas_call(
        functools.partial(_in_cd_kernel, tm, ctx_len),
        grid=(bsz, nt),
        in_specs=[row(D_MODEL), pl.BlockSpec((1, 8, D_MODEL), lambda b, i: (b, 0, 0)), _resident((8, D_MODEL)),
                  _resident((1, D_MODEL)), _resident(w.shape), _resident(cw.shape), _resident(cb.shape),
                  _resident(dtb.shape), _resident(aneg.shape)],
        out_specs=[row(c) for c, _ in outs],
        out_shape=[jax.ShapeDtypeStruct((n, c), dt) for c, dt in outs],
        compiler_params=_cparams("arbitrary", "arbitrary"),
        name="in_proj_cd",
    )(x2, modl, modc, g, w, cw, cb, dtb, aneg)


def _gla_dir(L, C, d, q_ref, k_ref, v_ref, ga_ref, awh_ref, awl_ref, ab_ref, out_ref, s_scr):
    reverse = d == 1
    mask = _tri_mask(L, reverse)
    mask01 = mask.astype(BF16)
    g1, g2, _ = _split3(ga_ref[...])
    pre = _dot(g1, awh_ref[d]) + _dot(g2, awh_ref[d]) + _dot(g1, awl_ref[d])
    lg = _log_sigmoid(pre + ab_ref[d]) * (1.0 / G_TAU)
    ball = _sel_cols(mask01, lg)
    rows = lax.broadcasted_iota(I32, (L, 1), 0)
    nb = L // C
    last = 0 if reverse else L - 1
    for h in range(G_HEADS):
        ks = slice(h * G_DK, (h + 1) * G_DK)
        b = ball[:, ks]
        qh = q_ref[:, ks].astype(F32)
        kh = k_ref[:, ks].astype(F32)
        vh = v_ref[:, h * G_DV:(h + 1) * G_DV]
        st = s_scr[d, h]
        inter = _dot_nt((qh * jnp.exp(b)).astype(BF16), st.astype(BF16))
        blocks = []
        for i in range(nb):
            lo, hi = i * C, (i + 1) * C
            if reverse:
                bref = b[hi:hi + 1, :] if i < nb - 1 else jnp.zeros((1, G_DK), F32)
                visible = rows >= lo
            else:
                bref = b[lo - 1:lo, :] if i > 0 else jnp.zeros((1, G_DK), F32)
                visible = rows < hi
            qi = (qh[lo:hi, :] * jnp.exp(b[lo:hi, :] - bref)).astype(BF16)
            ki = (kh * jnp.exp(jnp.where(visible, bref - b, -jnp.inf))).astype(BF16)
            blocks.append(_dot_nt(qi, ki))
        att = jnp.where(mask, jnp.concatenate(blocks, axis=0), 0.0).astype(BF16)
        oh = inter + _dot(att, vh)
        bl = b[last:last + 1, :]
        kd = (kh * jnp.exp(bl - b)).astype(BF16)
        s_scr[d, h] = st * jnp.exp(bl) + _dot_tn(vh, kd)
        out_ref[:, h * G_DV:(h + 1) * G_DV] = oh.astype(BF16)


def _gla_kernel(L, C, qf, kf, vf, gaf, qb, kb, vb, gab, awh_ref, awl_ref, ab_ref, of, ob, s_scr):
    _zero_at_start(s_scr)
    _gla_dir(L, C, 0, qf, kf, vf, gaf, awh_ref, awl_ref, ab_ref, of, s_scr)
    _gla_dir(L, C, 1, qb, kb, vb, gab, awh_ref, awl_ref, ab_ref, ob, s_scr)


def _gla(q, k, v, ga, awh, awl, ab, bsz, t, ctx_len):
    L, C = G_CHUNK, G_SUB
    scratch = [pltpu.VMEM((2, G_HEADS, G_DV, G_DK), F32)]
    return _bidir_scan(functools.partial(_gla_kernel, L, C), L, [q, k, v, ga], [], [awh, awl, ab], [1024], scratch,
                       bsz, t, ctx_len, "gla_scan")


def _ssd_dir(L, d, x_ref, b_ref, c_ref, gc_ref, gr_ref, out_ref, s_scr):
    reverse = d == 1
    mask = _tri_mask(L, reverse)
    mask01 = mask.astype(BF16)
    gc = gc_ref[...]
    gr = gr_ref[0]
    bcol = _sel_cols(mask01, gc)
    brow = _sel_rows(gr, mask01)
    last = 0 if reverse else L - 1
    dt_o, la_o = S_HEADS * d, 2 * S_HEADS + S_HEADS * d
    er = lax.broadcasted_iota(I32, (LANES, S_HEADS * S_P), 0)
    ec = jnp.right_shift(lax.broadcasted_iota(I32, (LANES, S_HEADS * S_P), 1), 6)
    sel_dt = (er == ec + dt_o).astype(BF16)
    sel_la = (er == ec + la_o).astype(BF16)
    x = x_ref[...].astype(F32)
    glane = lax.broadcasted_iota(I32, (L, LANES), 1)
    bla = jnp.where((glane >= la_o) & (glane < la_o + S_HEADS), bcol, 0.0)
    dtx = _expand(gc, sel_dt)
    ebx = _expand(jnp.exp(bla), sel_la)
    bl = bla[last:last + 1, :]
    wx = _expand(jnp.exp(bl - bla), sel_la) * dtx
    xdt = (x * dtx).astype(BF16)
    xw = (x * wx).astype(BF16)
    lane = lax.broadcasted_iota(I32, (L, 2 * S_P), 1)
    for g in range(S_GROUPS):
        ns = slice(g * S_N, (g + 1) * S_N)
        bg, cg = b_ref[:, ns], c_ref[:, ns]
        cb = _dot_nt(cg, bg)
        gs = slice(g * S_HPG * S_P, (g + 1) * S_HPG * S_P)
        st = s_scr[d, g]
        inter = _dot(cg, st.astype(BF16)) * ebx[:, gs]
        parts = []
        for pr in range(S_HPG // 2):
            e0 = g * S_HPG + 2 * pr
            xp = xdt[:, e0 * S_P:(e0 + 2) * S_P]
            res = []
            for e in (e0, e0 + 1):
                bc, br = bcol[:, la_o + e:la_o + e + 1], brow[la_o + e:la_o + e + 1, :]
                att = (cb * jnp.exp(jnp.where(mask, bc - br, -jnp.inf))).astype(BF16)
                res.append(_dot(att, xp))
            parts.append(jnp.where(lane < S_P, res[0], res[1]))
        s_scr[d, g] = st * ebx[last:last + 1, gs] + _dot_tn(bg, xw[:, gs])
        out_ref[:, gs] = (jnp.concatenate(parts, axis=1) + inter).astype(BF16)


def _ssd_kernel(L, xf, bf, cf, gcf, grf, xb, bb, cb, gcb, grb, of, ob, s_scr):
    _zero_at_start(s_scr)
    _ssd_dir(L, 0, xf, bf, cf, gcf, grf, of, s_scr)
    _ssd_dir(L, 1, xb, bb, cb, gcb, grb, ob, s_scr)


def _ssd(x, bm, cm, gcol, grow, bsz, t, ctx_len):
    L = S_CHUNK
    scratch = [pltpu.VMEM((2, S_GROUPS, S_N, S_HPG * S_P), F32)]
    return _bidir_scan(functools.partial(_ssd_kernel, L), L, [x, bm, cm, gcol], [grow], [], [1024], scratch,
                       bsz, t, ctx_len, "ssd_scan")


def _router_kernel(x_ref, modl_ref, g_ref, rw_ref, rb_ref, h_ref, idx_ref, gate_ref):
    h = _rms(x_ref[...], g_ref[...]) * (1.0 + modl_ref[0, 4:5, :]) + modl_ref[0, 3:4, :]
    h_ref[...] = h
    logits = jnp.dot(h, rw_ref[...], preferred_element_type=F32, precision=lax.Precision.HIGHEST) + rb_ref[...]
    lane = lax.broadcasted_iota(I32, logits.shape, 1)
    logits = jnp.where(lane < N_EXPERTS, logits, -jnp.inf)
    lanef = lane.astype(F32)
    m1 = jnp.max(logits, axis=1, keepdims=True)
    i1 = jnp.min(jnp.where(logits == m1, lanef, float(LANES)), axis=1, keepdims=True)
    rest = jnp.where(lanef == i1, -jnp.inf, logits)
    m2 = jnp.max(rest, axis=1, keepdims=True)
    i2 = jnp.min(jnp.where(rest == m2, lanef, float(LANES)), axis=1, keepdims=True)
    e = jnp.exp(m2 - m1)
    g1 = 1.0 / (1.0 + e)
    idx_ref[...] = jnp.where(lane == 0, i1, jnp.where(lane == 1, i2, 0.0)).astype(I32)
    gate_ref[...] = jnp.where(lane == 0, g1, jnp.where(lane == 1, e * g1, 0.0))


def _router(x2, modl, g, rw, rb, bsz, t, ctx_len):
    tm = ROW_TILE
    nt, nct = t // tm, ctx_len // tm
    nl = bsz * (t - ctx_len)
    nlt = nt - nct
    orow = lambda c: pl.BlockSpec((tm, c), lambda b, i: (b * nlt + i, 0))
    return pl.pallas_call(
        _router_kernel,
        grid=(bsz, nlt),
        in_specs=[pl.BlockSpec((tm, D_MODEL), lambda b, i: (b * nt + nct + i, 0)),
                  pl.BlockSpec((1, 8, D_MODEL), lambda b, i: (b, 0, 0)),
                  _resident((1, D_MODEL)), _resident(rw.shape), _resident(rb.shape)],
        out_specs=[orow(D_MODEL), orow(LANES), orow(LANES)],
        out_shape=[jax.ShapeDtypeStruct((nl, D_MODEL), F32), jax.ShapeDtypeStruct((nl, LANES), I32),
                   jax.ShapeDtypeStruct((nl, LANES), F32)],
        compiler_params=_cparams("arbitrary", "arbitrary"),
        name="moe_router",
    )(x2, modl, g, rw, rb)


def _gather_rows(table, idx):
    n_rows, d = idx.shape[0], table.shape[1]
    n_workers = SC_CORES * SC_SUBCORES
    per_worker = n_rows // n_workers
    chunk = SC_GATHER_ROWS
    assert per_worker * n_workers == n_rows and per_worker % chunk == 0
    mesh = plsc.VectorSubcoreMesh(core_axis_name="c", subcore_axis_name="s")

    @functools.partial(
        pl.kernel, mesh=mesh, out_type=jax.ShapeDtypeStruct((n_rows, d), table.dtype),
        scratch_types=[pltpu.VMEM((chunk,), I32), pltpu.VMEM((chunk, d), table.dtype), pltpu.SemaphoreType.DMA],
        name="sc_gather_rows")
    def gather(table_hbm, idx_hbm, out_hbm, idx_v, rows_v, sem):
        base = (lax.axis_index("s") * SC_CORES + lax.axis_index("c")) * per_worker

        @pl.loop(0, per_worker // chunk)
        def _(j):
            off = pl.multiple_of(base + j * chunk, 8)
            pltpu.sync_copy(idx_hbm.at[pl.ds(off, chunk)], idx_v)
            pltpu.async_copy(table_hbm.at[idx_v], rows_v, sem).wait()
            pltpu.sync_copy(rows_v, out_hbm.at[pl.ds(off, chunk)])

    return gather(table, idx)


def _moe_kernel(be_ref, nu_ref, x_ref, w1g_ref, w1u_ref, w2_ref, y_ref, acc_ref):
    i, j = pl.program_id(0), pl.program_id(1)

    @pl.when(i < nu_ref[0])
    def _():
        @pl.when(j == 0)
        def _():
            acc_ref[...] = jnp.zeros_like(acc_ref)

        x = x_ref[...].astype(BF16)
        a = (_silu(_dot(x, w1g_ref[0].astype(BF16))) * _dot(x, w1u_ref[0].astype(BF16))).astype(BF16)
        acc_ref[...] += _dot(a, w2_ref[0].astype(BF16))

        @pl.when(j == pl.num_programs(1) - 1)
        def _():
            y_ref[...] = acc_ref[...]


def _moe_experts(xs, w1, w2, block_e, n_used, n_blocks):
    bm, fc = MOE_ROWS, MOE_FF
    nff = FF_EXPERT // fc
    used = lambda i, nu: jnp.minimum(i, nu[0] - 1)
    ffi = lambda i, j, nu: jnp.where(i < nu[0], j, nff - 1)
    grid_spec = pltpu.PrefetchScalarGridSpec(
        num_scalar_prefetch=2,
        grid=(n_blocks, nff),
        in_specs=[pl.BlockSpec((bm, D_MODEL), lambda i, j, be, nu: (used(i, nu), 0)),
                  pl.BlockSpec((1, D_MODEL, fc), lambda i, j, be, nu: (be[used(i, nu)], 0, ffi(i, j, nu))),
                  pl.BlockSpec((1, D_MODEL, fc), lambda i, j, be, nu: (be[used(i, nu)], 0, nff + ffi(i, j, nu))),
                  pl.BlockSpec((1, fc, D_MODEL), lambda i, j, be, nu: (be[used(i, nu)], ffi(i, j, nu), 0))],
        out_specs=pl.BlockSpec((bm, D_MODEL), lambda i, j, be, nu: (used(i, nu), 0)),
        scratch_shapes=[pltpu.VMEM((bm, D_MODEL), F32)],
    )
    return pl.pallas_call(
        _moe_kernel,
        grid_spec=grid_spec,
        out_shape=jax.ShapeDtypeStruct((n_blocks * bm, D_MODEL), F32),
        compiler_params=_cparams("arbitrary", "arbitrary"),
        name="moe_experts",
    )(block_e, n_used, xs, w1, w1, w2)


def _combine_kernel(y0_ref, y1_ref, gate_ref, x_ref, modl_ref, g_ref, o_ref):
    gt = gate_ref[...]
    y = gt[:, 0:1] * y0_ref[...] + gt[:, 1:2] * y1_ref[...]
    o_ref[...] = x_ref[...] + modl_ref[0, 5:6, :] * _rms(y, g_ref[...])


def _combine(yk, gates, x2, modl, g, bsz, t, ctx_len):
    tm = ROW_TILE
    nt, nct = t // tm, ctx_len // tm
    nlt = nt - nct
    nl = bsz * (t - ctx_len)
    orow = lambda c: pl.BlockSpec((tm, c), lambda b, i: (b * nlt + i, 0))
    second = pl.BlockSpec((tm, D_MODEL), lambda b, i: (nl // tm + b * nlt + i, 0))
    return pl.pallas_call(
        _combine_kernel,
        grid=(bsz, nlt),
        in_specs=[orow(D_MODEL), second, orow(LANES),
                  pl.BlockSpec((tm, D_MODEL), lambda b, i: (b * nt + nct + i, 0)),
                  pl.BlockSpec((1, 8, D_MODEL), lambda b, i: (b, 0, 0)), _resident((1, D_MODEL))],
        out_specs=orow(D_MODEL),
        out_shape=jax.ShapeDtypeStruct((nl, D_MODEL), F32),
        compiler_params=_cparams("arbitrary", "arbitrary"),
        name="moe_combine",
    )(yk, yk, gates, x2, modl, g)


def _routing_tables(top_idx, n_tok):
    bm = MOE_ROWS
    flat_e = top_idx.reshape(-1)
    n_assign = flat_e.shape[0]
    onehot = (flat_e[:, None] == jnp.arange(N_EXPERTS, dtype=I32)[None, :]).astype(I32)
    incl = jnp.cumsum(onehot, axis=0)
    counts = incl[-1]
    rank = jnp.take_along_axis(incl, flat_e[:, None], axis=1)[:, 0] - 1
    padded = (counts + bm - 1) // bm * bm
    ends_p = jnp.cumsum(padded)
    pstarts = ends_p - padded
    slot = (pstarts[flat_e] + rank).astype(I32)
    n_blocks = n_assign // bm + N_EXPERTS
    flat_tok = jnp.arange(n_assign, dtype=I32) // TOP_K
    slot_tok = jnp.zeros((n_blocks * bm,), I32).at[slot].set(flat_tok)
    block_e = jnp.minimum(jnp.searchsorted(ends_p, jnp.arange(n_blocks, dtype=I32) * bm, side='right'),
                          N_EXPERTS - 1).astype(I32)
    n_used = (ends_p[-1] // bm).astype(I32).reshape(1)
    return slot, slot_tok, block_e, n_used, n_blocks


def _row_layout(gcol, ncols, L):
    n = gcol.shape[0]
    return gcol[:, :ncols].reshape(n // L, L, ncols).transpose(0, 2, 1)


def kernel(x, c, ctx, c_ctx, mod_w, mod_b, norm_g, ab_in_w, m_conv_w, m_conv_b, m_gate_b, m_norm_w, l_conv_w,
           l_conv_b, l_wa, l_ba, l_wx, l_bx, l_lam, ab_out_w, ffn_w1, ffn_w2, cd_in_w, g_alpha_w, g_alpha_b,
           g_norm_w, s_conv_w, s_conv_b, s_dt_bias, s_A_log, s_D, s_norm_w, cd_out_w, router_w, router_b,
           moe_w1, moe_w2):
    bsz, seq, dm = x.shape
    ctx_len = ctx.shape[1]
    t = ctx_len + seq
    n = bsz * t
    x2 = jnp.concatenate([ctx, x], axis=1).reshape(n, dm)

    c_all = jnp.concatenate([c, c_ctx[None, :], jnp.zeros((-(bsz + 1) % 8, dm), F32)], axis=0)
    mods = _modulation(c_all, mod_w, mod_b).reshape(mod_w.shape[0], c_all.shape[0], 6, dm)
    pad2 = jnp.zeros((bsz, 2, dm), F32)

    def layer_mods(layer):
        modl = jnp.concatenate([mods[layer, :bsz], pad2], axis=1)
        modc = jnp.concatenate([mods[layer, bsz], pad2[0]], axis=0)
        return modl, modc

    modl, modc = layer_mods(0)
    g = norm_g[0]
    w = ab_in_w[0]
    o_qk, o_v, o_o, o_gt, o_lx, o_lg = 0, 1024, 2048, 3072, 3088, 4112
    w_in = jnp.concatenate([w[:, o_qk:o_v], w[:, o_v:o_o], w[:, o_o:o_gt], w[:, o_lx:o_lg], w[:, o_lg:],
                            w[:, o_gt:o_lx], jnp.zeros((dm, LANES - 16), F32)], axis=1).astype(BF16)
    gb = jnp.concatenate([m_gate_b[0].reshape(1, 16), jnp.zeros((1, LANES - 16), F32)], axis=1)
    q, k, v, og, lx, glg, gates = _in_ab(x2, modl, modc, g[0:1], w_in, m_conv_w[0], m_conv_b[0][None],
                                         l_conv_w[0], l_conv_b[0][None], gb, bsz, t, ctx_len)
    hm_f, hm_b = _mlstm(q, k, v, gates, _row_layout(gates, 16, M_CHUNK), bsz, t, ctx_len)
    hl_f, hl_b = _lru(lx, l_wa[0].astype(BF16), l_wx[0].astype(BF16), l_ba[0][:, None], l_bx[0][:, None],
                      l_lam[0][:, None], bsz, t, ctx_len)
    x2 = _out_proj(_out_ab_kernel, "out_proj_ab", [hm_f, hm_b, og, hl_f, hl_b, glg], [m_norm_w[0][None]],
                   x2, modl, modc, g[1:2], ab_out_w[0].astype(BF16), bsz, t, ctx_len)
    x2 = _ffn(x2, modl, modc, g[2:3], g[3:4], ffn_w1[0].astype(BF16), ffn_w2[0].astype(BF16), bsz, t, ctx_len)

    modl, modc = layer_mods(1)
    g = norm_g[1]
    w = cd_in_w[0]
    o_gv, o_gr, o_ga, o_z, o_xbc, o_dt = 1024, 2048, 3072, 3104, 4128, 5664
    zpad = jnp.zeros((dm, LANES - 32), F32)
    w_in = jnp.concatenate([w[:, :o_gv], w[:, o_gv:o_gr], w[:, o_gr:o_ga], w[:, o_z:o_xbc], w[:, o_xbc:o_dt],
                            w[:, o_ga:o_z], zpad, w[:, o_dt:], w[:, o_dt:], jnp.zeros((dm, LANES - 64), F32)],
                           axis=1).astype(BF16)
    dtb = jnp.concatenate([s_dt_bias[0].reshape(1, 32), s_dt_bias[0].reshape(1, 32),
                           jnp.zeros((1, LANES - 64), F32)], axis=1)
    aneg = jnp.concatenate([jnp.zeros((1, 32), F32), -jnp.exp(s_A_log[0].reshape(1, 32)),
                            jnp.zeros((1, LANES - 64), F32)], axis=1)
    gq, gk, gv, grs, zs, sx, sb, sc, ga, dts = _in_cd(x2, modl, modc, g[0:1], w_in, s_conv_w[0], s_conv_b[0][None],
                                                      dtb, aneg, bsz, t, ctx_len)
    aw = jnp.zeros((2, LANES, G_HEADS * G_DK), F32)
    aw = aw.at[0, 0:G_RANK].set(g_alpha_w[0, 0]).at[1, G_RANK:2 * G_RANK].set(g_alpha_w[0, 1])
    awh = aw.astype(BF16)
    awl = (aw - awh.astype(F32)).astype(BF16)
    og_f, og_b = _gla(gq, gk, gv, ga, awh, awl, g_alpha_b[0][:, None], bsz, t, ctx_len)
    ys_f, ys_b = _ssd(sx, sb, sc, dts, _row_layout(dts, 4 * S_HEADS, S_CHUNK), bsz, t, ctx_len)
    dskip = jnp.repeat(s_D[0], S_P)[None]
    x2 = _out_proj(_out_cd_kernel, "out_proj_cd", [og_f, og_b, grs, ys_f, ys_b, sx, zs],
                   [g_norm_w[0][None], dskip, s_norm_w[0][None]],
                   x2, modl, modc, g[1:2], cd_out_w[0].astype(BF16), bsz, t, ctx_len)

    rw = jnp.concatenate([router_w[0], jnp.zeros((dm, LANES - N_EXPERTS), F32)], axis=1)
    rb = jnp.concatenate([router_b[0][None], jnp.zeros((1, LANES - N_EXPERTS), F32)], axis=1)
    h2, top_idx, top_gate = _router(x2, modl, g[2:3], rw, rb, bsz, t, ctx_len)
    n_tok = bsz * seq
    slot, slot_tok, block_e, n_used, n_blocks = _routing_tables(top_idx[:, :TOP_K], n_tok)
    xs = _gather_rows(h2, slot_tok)
    ys_e = _moe_experts(xs, moe_w1[0], moe_w2[0], block_e, n_used, n_blocks)
    slot2 = slot.reshape(n_tok, TOP_K)
    yk = _gather_rows(ys_e, jnp.concatenate([slot2[:, 0], slot2[:, 1]]))
    out = _combine(yk, top_gate, x2, modl, g[3:4], bsz, t, ctx_len)
    return out.reshape(bsz, seq, dm)
```

```python
import functools

import jax
import jax.numpy as jnp
from jax import lax
from jax.experimental import pallas as pl
from jax.experimental.pallas import tpu as pltpu
from jax.experimental.pallas import tpu_sc as plsc

F32, BF16, I32 = jnp.float32, jnp.bfloat16, jnp.int32

D_MODEL = 1024
GRID_W = 64
EPS = 1e-6
LOG2E = 1.4426950408889634
M_HEADS, M_DK, M_DV = 4, 128, 256
LRU_BLOCKS, LRU_BW, LRU_C = 8, 128, 8.0
G_HEADS, G_DK, G_DV, G_RANK, G_TAU = 4, 128, 256, 16, 16.0
S_HEADS, S_P, S_N, S_GROUPS, S_HPG = 16, 64, 128, 2, 8
FF_DENSE = 2816
N_EXPERTS, TOP_K, FF_EXPERT = 8, 2, 3584

LANES = 128
VMEM_LIMIT = 56 * 1024 * 1024
ROW_TILE = 256
M_CHUNK = 256
G_CHUNK, G_SUB = 256, 16
S_CHUNK = 256
LRU_TILE = 256
FF_STEP = 256
MOE_ROWS = 1024
MOE_FF = 512
SC_CORES, SC_SUBCORES = 2, 16
SC_GATHER_ROWS = 64


def _cparams(*sem):
    return pltpu.CompilerParams(dimension_semantics=sem, vmem_limit_bytes=VMEM_LIMIT)


def _silu(x):
    return x * jax.nn.sigmoid(x)


def _softplus(x):
    return jnp.maximum(x, 0.0) + jnp.log1p(jnp.exp(-jnp.abs(x)))


def _log_sigmoid(x):
    return jnp.minimum(x, 0.0) - jnp.log1p(jnp.exp(-jnp.abs(x)))


def _dot(a, b):
    return jnp.dot(a, b, preferred_element_type=F32)


def _dot_nt(a, b):
    return lax.dot_general(a, b, (((1,), (1,)), ((), ())), preferred_element_type=F32)


def _dot_tn(a, b):
    return lax.dot_general(a, b, (((0,), (0,)), ((), ())), preferred_element_type=F32)


def _split3(f):
    f1 = f.astype(BF16)
    r = f - f1.astype(F32)
    f2 = r.astype(BF16)
    f3 = (r - f2.astype(F32)).astype(BF16)
    return f1, f2, f3


def _sel_cols(mask01, f):
    p1, p2, p3 = _split3(f)
    return _dot(mask01, p1) + _dot(mask01, p2) + _dot(mask01, p3)


def _sel_rows(f, mask01):
    p1, p2, p3 = _split3(f)
    return _dot_nt(p1, mask01) + _dot_nt(p2, mask01) + _dot_nt(p3, mask01)


def _expand(f, sel01):
    p1, p2, p3 = _split3(f)
    return _dot(p1, sel01) + _dot(p2, sel01) + _dot(p3, sel01)


def _rms(u, g):
    return u * lax.rsqrt(jnp.mean(u * u, axis=-1, keepdims=True) + EPS) * g


def _head_rms(u, g, n_heads):
    w = u.shape[-1] // n_heads
    return jnp.concatenate([_rms(u[:, h * w:(h + 1) * w], g[:, h * w:(h + 1) * w]) for h in range(n_heads)], axis=1)


def _row_ids(tile_idx, tm):
    return tile_idx * tm + lax.broadcasted_iota(I32, (tm, 1), 0)


def _mod(modl_ref, modc_ref, idx, is_ctx):
    return jnp.where(is_ctx, modc_ref[idx:idx + 1, :], modl_ref[0, idx:idx + 1, :])


def _adaln(x, g, modl_ref, modc_ref, shift_idx, is_ctx):
    shift = _mod(modl_ref, modc_ref, shift_idx, is_ctx)
    scale = _mod(modl_ref, modc_ref, shift_idx + 1, is_ctx)
    return _rms(x, g) * (1.0 + scale) + shift


def _conv_masks(r, ctx_len):
    in_lat = r >= ctx_len
    pos = jnp.where(in_lat, jnp.bitwise_and(r - ctx_len, GRID_W - 1), r)
    seg = jnp.where(in_lat, GRID_W, ctx_len)
    return pos >= 2, pos >= 1, pos <= seg - 2


def _dwconv(y, cw, cb, masks, tm):
    m2, m1, p1 = masks
    ym2 = jnp.where(m2, pltpu.roll(y, 2, 0), 0.0)
    ym1 = jnp.where(m1, pltpu.roll(y, 1, 0), 0.0)
    yp1 = jnp.where(p1, pltpu.roll(y, tm - 1, 0), 0.0)
    return cb + cw[0:1, :] * ym2 + cw[1:2, :] * ym1 + cw[2:3, :] * y + cw[3:4, :] * yp1


def _resident(shape):
    nd = len(shape)
    return pl.BlockSpec(shape, lambda *_: (0,) * nd)


def _mod_kernel(c_ref, w_ref, b_ref, o_ref):
    a = _silu(c_ref[...])
    o_ref[0] = jnp.dot(a, w_ref[0], preferred_element_type=F32, precision=lax.Precision.HIGHEST) + b_ref[0]


def _modulation(c_all, mod_w, mod_b):
    depth, d, n6 = mod_w.shape
    rows = c_all.shape[0]
    tn = 1024
    return pl.pallas_call(
        _mod_kernel,
        grid=(depth, n6 // tn),
        in_specs=[pl.BlockSpec((rows, d), lambda l, j: (0, 0)),
                  pl.BlockSpec((1, d, tn), lambda l, j: (l, 0, j)),
                  pl.BlockSpec((1, 1, tn), lambda l, j: (l, 0, j))],
        out_specs=pl.BlockSpec((1, rows, tn), lambda l, j: (l, 0, j)),
        out_shape=jax.ShapeDtypeStruct((depth, rows, n6), F32),
        compiler_params=_cparams("arbitrary", "arbitrary"),
        name="modulation",
    )(c_all, mod_w, mod_b.reshape(depth, 1, n6))


def _in_ab_kernel(tm, ctx_len, x_ref, modl_ref, modc_ref, g_ref, w_ref, mcw_ref, mcb_ref, lcw_ref, lcb_ref,
                  gb_ref, q_ref, k_ref, v_ref, og_ref, lx_ref, glg_ref, gates_ref):
    r = _row_ids(pl.program_id(1), tm)
    is_ctx = r < ctx_len
    h = _adaln(x_ref[...], g_ref[...], modl_ref, modc_ref, 0, is_ctx).astype(BF16)
    masks = _conv_masks(r, ctx_len)
    q = _dwconv(_dot(h, w_ref[:, 0:512]), mcw_ref[:, 0:512], mcb_ref[:, 0:512], masks, tm)
    q_ref[...] = _silu(q).astype(BF16)
    k = _dwconv(_dot(h, w_ref[:, 512:1024]), mcw_ref[:, 512:1024], mcb_ref[:, 512:1024], masks, tm)
    k_ref[...] = (_silu(k) * (M_DK ** -0.5)).astype(BF16)
    v_ref[...] = _dot(h, w_ref[:, 1024:2048]).astype(BF16)
    og_ref[...] = jax.nn.sigmoid(_dot(h, w_ref[:, 2048:3072])).astype(BF16)
    lx = _dwconv(_dot(h, w_ref[:, 3072:4096]), lcw_ref[...], lcb_ref[...], masks, tm)
    lx_ref[...] = lx
    glg_ref[...] = jax.nn.gelu(_dot(h, w_ref[:, 4096:5120])).astype(BF16)
    gt = _dot(h, w_ref[:, 5120:5248]) + gb_ref[...]
    lane = lax.broadcasted_iota(I32, gt.shape, 1)
    gates_ref[...] = jnp.where(lane >= 2 * M_HEADS, _log_sigmoid(gt), gt)


def _in_ab(x2, modl, modc, g, w, mcw, mcb, lcw, lcb, gb, bsz, t, ctx_len):
    tm = ROW_TILE
    nt = t // tm
    n = bsz * t
    row = lambda c: pl.BlockSpec((tm, c), lambda b, i: (b * nt + i, 0))
    outs = [(512, BF16), (512, BF16), (1024, BF16), (1024, BF16), (1024, F32), (1024, BF16), (LANES, F32)]
    return pl.pallas_call(
        functools.partial(_in_ab_kernel, tm, ctx_len),
        grid=(bsz, nt),
        in_specs=[row(D_MODEL),
                  pl.BlockSpec((1, 8, D_MODEL), lambda b, i: (b, 0, 0)),
                  _resident((8, D_MODEL)), _resident((1, D_MODEL)), _resident(w.shape),
                  _resident(mcw.shape), _resident(mcb.shape), _resident(lcw.shape), _resident(lcb.shape),
                  _resident(gb.shape)],
        out_specs=[row(c) for c, _ in outs],
        out_shape=[jax.ShapeDtypeStruct((n, c), dt) for c, dt in outs],
        compiler_params=_cparams("arbitrary", "arbitrary"),
        name="in_proj_ab",
    )(x2, modl, modc, g, w, mcw, mcb, lcw, lcb, gb)


def _reverse_order(n_ctx_chunks, n_chunks):
    return lambda c: jnp.where(c < n_ctx_chunks, n_ctx_chunks - 1 - c, n_chunks - 1 - (c - n_ctx_chunks))


def _tri_mask(L, reverse):
    row = lax.broadcasted_iota(I32, (L, L), 0)
    col = lax.broadcasted_iota(I32, (L, L), 1)
    return (col >= row) if reverse else (col <= row)


def _bidir_scan(kern, L, row_ins, row3_ins, res_ins, out_cols, scratch, bsz, t, ctx_len, name):
    nc = t // L
    rev = _reverse_order(ctx_len // L, nc)
    fwd = lambda i: i
    n = bsz * t

    def row(c, order):
        return pl.BlockSpec((L, c), lambda b, i: (b * nc + order(i), 0))

    def row3(shape, order):
        return pl.BlockSpec((1,) + tuple(shape[1:]), lambda b, i: (b * nc + order(i), 0, 0))

    in_specs, args = [], []
    for order in (fwd, rev):
        in_specs += [row(a.shape[1], order) for a in row_ins] + [row3(a.shape, order) for a in row3_ins]
        args += list(row_ins) + list(row3_ins)
    in_specs += [_resident(a.shape) for a in res_ins]
    args += list(res_ins)
    return pl.pallas_call(
        kern,
        grid=(bsz, nc),
        in_specs=in_specs,
        out_specs=[row(c, fwd) for c in out_cols] + [row(c, rev) for c in out_cols],
        out_shape=[jax.ShapeDtypeStruct((n, c), BF16) for c in out_cols] * 2,
        scratch_shapes=scratch,
        compiler_params=_cparams("arbitrary", "arbitrary"),
        name=name,
    )(*args)


def _zero_at_start(*scratch):
    @pl.when(pl.program_id(1) == 0)
    def _():
        for s in scratch:
            s[...] = jnp.zeros_like(s)


def _mlstm_dir(L, d, q_ref, k_ref, v_ref, gc_ref, gr_ref, out_ref, c_scr, m_scr):
    reverse = d == 1
    mask = _tri_mask(L, reverse)
    mask01 = mask.astype(BF16)
    gc = gc_ref[...]
    gr = gr_ref[0]
    bcol = _sel_cols(mask01, gc)
    brow = _sel_rows(gr, mask01)
    last = 0 if reverse else L - 1
    ones_col = (lax.broadcasted_iota(I32, (L, LANES), 1) == 0).astype(BF16)
    for h in range(M_HEADS):
        fo, io = 2 * M_HEADS + M_HEADS * d + h, M_HEADS * d + h
        bc, br = bcol[:, fo:fo + 1], brow[fo:fo + 1, :]
        ic, ir = gc[:, io:io + 1], gr[io:io + 1, :]
        m = m_scr[d, h:h + 1, 0:1]
        g = bc + m
        dlog = jnp.where(mask, bc - br + ir, -jnp.inf)
        mt = jnp.maximum(g, jnp.max(dlog, axis=1, keepdims=True))
        w_inter = jnp.exp(g - mt)
        p = jnp.exp(dlog - mt)
        qh = q_ref[:, h * M_DK:(h + 1) * M_DK]
        kh = k_ref[:, h * M_DK:(h + 1) * M_DK]
        vx = jnp.concatenate([v_ref[:, h * M_DV:(h + 1) * M_DV], ones_col], axis=1)
        sc = (_dot_nt(qh, kh) * p).astype(BF16)
        cx = c_scr[d, h]
        numx = w_inter * _dot(qh, cx.astype(BF16)) + _dot(sc, vx)
        den = numx[:, M_DV:M_DV + 1]
        hh = numx[:, 0:M_DV] * (1.0 / jnp.maximum(jnp.abs(den), jnp.exp(-mt)))
        bl = bc[last:last + 1, :]
        a_s = bl - bc + ic
        m_new = jnp.maximum(bl + m, jnp.max(a_s, axis=0, keepdims=True))
        decay = jnp.exp(bl + m - m_new)
        ws = jnp.exp(a_s - m_new)
        kw = (kh.astype(F32) * ws).astype(BF16)
        c_scr[d, h] = decay * cx + _dot_tn(kw, vx)
        m_scr[d, h:h + 1, 0:1] = m_new
        out_ref[:, h * M_DV:(h + 1) * M_DV] = hh.astype(BF16)


def _mlstm_kernel(L, qf, kf, vf, gcf, grf, qb, kb, vb, gcb, grb, of, ob, c_scr, m_scr):
    _zero_at_start(c_scr, m_scr)
    _mlstm_dir(L, 0, qf, kf, vf, gcf, grf, of, c_scr, m_scr)
    _mlstm_dir(L, 1, qb, kb, vb, gcb, grb, ob, c_scr, m_scr)


def _mlstm(q, k, v, gcol, grow, bsz, t, ctx_len):
    L = M_CHUNK
    scratch = [pltpu.VMEM((2, M_HEADS, M_DK, M_DV + LANES), F32), pltpu.VMEM((2, 8, LANES), F32)]
    return _bidir_scan(functools.partial(_mlstm_kernel, L), L, [q, k, v, gcol], [grow], [], [1024], scratch,
                       bsz, t, ctx_len, "mlstm_scan")


def _lru_gates(L, d, u_ref, wa_ref, wx_ref, ba_ref, bx_ref, lam_ref, a_scr, b_scr):
    reverse = d == 1
    u = u_ref[...]
    ub = u.astype(BF16)
    sp = _softplus(-lam_ref[d])
    for n in range(LRU_BLOCKS):
        sl = slice(n * LRU_BW, (n + 1) * LRU_BW)
        rg = jax.nn.sigmoid(_dot(ub[:, sl], wa_ref[d, n]) + ba_ref[d, :, sl])
        ig = jax.nn.sigmoid(_dot(ub[:, sl], wx_ref[d, n]) + bx_ref[d, :, sl])
        log_a = -LRU_C * rg * sp[:, sl]
        a = jnp.exp(log_a)
        a_scr[d, :, sl] = a
        b_scr[d, :, sl] = jnp.sqrt(-jnp.tanh(log_a) * (a * a + 1.0)) * (ig * u[:, sl])
    a = a_scr[d]
    b = b_scr[d]
    sub = jnp.bitwise_and(lax.broadcasted_iota(I32, (L, 1), 0), 7)
    for s in (1, 2, 4):
        if reverse:
            a_sh, b_sh, keep = pltpu.roll(a, L - s, 0), pltpu.roll(b, L - s, 0), sub < 8 - s
        else:
            a_sh, b_sh, keep = pltpu.roll(a, s, 0), pltpu.roll(b, s, 0), sub >= s
        b = jnp.where(keep, a * b_sh + b, b)
        a = jnp.where(keep, a * a_sh, a)
    a_scr[d] = a
    b_scr[d] = b


def _lru_kernel(L, uf, ub, wa_ref, wx_ref, ba_ref, bx_ref, lam_ref, of, ob, a_scr, b_scr, h_scr):
    _zero_at_start(h_scr)
    _lru_gates(L, 0, uf, wa_ref, wx_ref, ba_ref, bx_ref, lam_ref, a_scr, b_scr)
    _lru_gates(L, 1, ub, wa_ref, wx_ref, ba_ref, bx_ref, lam_ref, a_scr, b_scr)
    ng = L // 8

    def body(j, carry):
        hf, hb = carry
        rf = pl.ds(pl.multiple_of(j * 8, 8), 8)
        rb = pl.ds(pl.multiple_of((ng - 1 - j) * 8, 8), 8)
        gf = b_scr[0, rf, :] + a_scr[0, rf, :] * hf
        gb = b_scr[1, rb, :] + a_scr[1, rb, :] * hb
        b_scr[0, rf, :] = gf
        b_scr[1, rb, :] = gb
        return gf[7:8, :], gb[0:1, :]

    hf, hb = lax.fori_loop(0, ng, body, (h_scr[0:1, :], h_scr[1:2, :]))
    h_scr[0:1, :] = hf
    h_scr[1:2, :] = hb
    of[...] = b_scr[0].astype(BF16)
    ob[...] = b_scr[1].astype(BF16)


def _lru(u, wa, wx, ba, bx, lam, bsz, t, ctx_len):
    L = LRU_TILE
    scratch = [pltpu.VMEM((2, L, 1024), F32), pltpu.VMEM((2, L, 1024), F32), pltpu.VMEM((8, 1024), F32)]
    return _bidir_scan(functools.partial(_lru_kernel, L), L, [u], [], [wa, wx, ba, bx, lam], [1024], scratch,
                       bsz, t, ctx_len, "rglru_scan")


def _out_tail(tm, ctx_len, a1, a2, x_ref, modl_ref, modc_ref, g_ref, w_ref, xo_ref):
    is_ctx = _row_ids(pl.program_id(1), tm) < ctx_len
    half = w_ref.shape[0] // 2
    y = _dot(a1.astype(BF16), w_ref[0:half, :]) + _dot(a2.astype(BF16), w_ref[half:, :])
    xo_ref[...] = x_ref[...] + _mod(modl_ref, modc_ref, 2, is_ctx) * _rms(y, g_ref[...])


def _out_ab_kernel(tm, ctx_len, hf, hb, og, lf, lb, glg, nw, x_ref, modl_ref, modc_ref, g_ref, w_ref, xo_ref):
    hm = _head_rms(hf[...].astype(F32) + hb[...].astype(F32), nw[...], M_HEADS) * og[...].astype(F32)
    hl = (lf[...].astype(F32) + lb[...].astype(F32)) * glg[...].astype(F32)
    _out_tail(tm, ctx_len, hm, hl, x_ref, modl_ref, modc_ref, g_ref, w_ref, xo_ref)


def _out_cd_kernel(tm, ctx_len, gf, gb, grs, yf, yb, sx, zs, gnw, dsk, snw, x_ref, modl_ref, modc_ref, g_ref,
                   w_ref, xo_ref):
    og = _head_rms(gf[...].astype(F32) + gb[...].astype(F32), gnw[...], G_HEADS) * grs[...].astype(F32)
    ys = yf[...].astype(F32) + yb[...].astype(F32) + dsk[...] * sx[...].astype(F32)
    ys = _rms(ys * zs[...].astype(F32), snw[...])
    _out_tail(tm, ctx_len, og, ys, x_ref, modl_ref, modc_ref, g_ref, w_ref, xo_ref)


def _out_proj(kern, name, acts, vecs, x2, modl, modc, g, w, bsz, t, ctx_len):
    tm = ROW_TILE
    nt = t // tm
    row = lambda c: pl.BlockSpec((tm, c), lambda b, i: (b * nt + i, 0))
    return pl.pallas_call(
        functools.partial(kern, tm, ctx_len),
        grid=(bsz, nt),
        in_specs=[row(1024)] * len(acts) + [_resident((1, 1024))] * len(vecs)
        + [row(D_MODEL), pl.BlockSpec((1, 8, D_MODEL), lambda b, i: (b, 0, 0)),
           _resident((8, D_MODEL)), _resident((1, D_MODEL)), _resident(w.shape)],
        out_specs=row(D_MODEL),
        out_shape=jax.ShapeDtypeStruct(x2.shape, F32),
        compiler_params=_cparams("arbitrary", "arbitrary"),
        name=name,
    )(*acts, *vecs, x2, modl, modc, g, w)


def _ffn_kernel(tm, ctx_len, x_ref, modl_ref, modc_ref, g2_ref, g3_ref, w1_ref, w2_ref, xo_ref):
    is_ctx = _row_ids(pl.program_id(1), tm) < ctx_len
    x = x_ref[...]
    h = _adaln(x, g2_ref[...], modl_ref, modc_ref, 3, is_ctx).astype(BF16)
    ff = w2_ref.shape[0]
    acc = jnp.zeros((tm, D_MODEL), F32)
    for j in range(ff // FF_STEP):
        lo = j * FF_STEP
        gj = _dot(h, w1_ref[:, lo:lo + FF_STEP])
        uj = _dot(h, w1_ref[:, ff + lo:ff + lo + FF_STEP])
        acc = acc + _dot((_silu(gj) * uj).astype(BF16), w2_ref[lo:lo + FF_STEP, :])
    xo_ref[...] = x + _mod(modl_ref, modc_ref, 5, is_ctx) * _rms(acc, g3_ref[...])


def _ffn(x2, modl, modc, g2, g3, w1, w2, bsz, t, ctx_len):
    tm = ROW_TILE
    nt = t // tm
    row = lambda c: pl.BlockSpec((tm, c), lambda b, i: (b * nt + i, 0))
    return pl.pallas_call(
        functools.partial(_ffn_kernel, tm, ctx_len),
        grid=(bsz, nt),
        in_specs=[row(D_MODEL), pl.BlockSpec((1, 8, D_MODEL), lambda b, i: (b, 0, 0)), _resident((8, D_MODEL)),
                  _resident((1, D_MODEL)), _resident((1, D_MODEL)), _resident(w1.shape), _resident(w2.shape)],
        out_specs=row(D_MODEL),
        out_shape=jax.ShapeDtypeStruct(x2.shape, F32),
        compiler_params=_cparams("arbitrary", "arbitrary"),
        name="dense_swiglu",
    )(x2, modl, modc, g2, g3, w1, w2)


def _in_cd_kernel(tm, ctx_len, x_ref, modl_ref, modc_ref, g_ref, w_ref, cw_ref, cb_ref, dtb_ref, aneg_ref,
                  gq_ref, gk_ref, gv_ref, gr_ref, z_ref, sx_ref, sb_ref, sc_ref, ga_ref, dts_ref):
    r = _row_ids(pl.program_id(1), tm)
    is_ctx = r < ctx_len
    h = _adaln(x_ref[...], g_ref[...], modl_ref, modc_ref, 0, is_ctx).astype(BF16)
    masks = _conv_masks(r, ctx_len)
    gq_ref[...] = (_dot(h, w_ref[:, 0:512]) * (G_DK ** -0.5)).astype(BF16)
    gk_ref[...] = _dot(h, w_ref[:, 512:1024]).astype(BF16)
    gv_ref[...] = _dot(h, w_ref[:, 1024:2048]).astype(BF16)
    gr_ref[...] = _silu(_dot(h, w_ref[:, 2048:3072])).astype(BF16)
    z_ref[...] = _silu(_dot(h, w_ref[:, 3072:4096])).astype(BF16)
    sx = _dwconv(_dot(h, w_ref[:, 4096:5120]), cw_ref[:, 0:1024], cb_ref[:, 0:1024], masks, tm)
    sx_ref[...] = _silu(sx).astype(BF16)
    sb = _dwconv(_dot(h, w_ref[:, 5120:5376]), cw_ref[:, 1024:1280], cb_ref[:, 1024:1280], masks, tm)
    sb_ref[...] = _silu(sb).astype(BF16)
    sc = _dwconv(_dot(h, w_ref[:, 5376:5632]), cw_ref[:, 1280:1536], cb_ref[:, 1280:1536], masks, tm)
    sc_ref[...] = _silu(sc).astype(BF16)
    ga_ref[...] = _dot(h, w_ref[:, 5632:5760])
    dt = _softplus(_dot(h, w_ref[:, 5760:5888]) + dtb_ref[...])
    lane = lax.broadcasted_iota(I32, dt.shape, 1)
    dts_ref[...] = jnp.where(lane < 2 * S_HEADS, dt, dt * aneg_ref[...])


def _in_cd(x2, modl, modc, g, w, cw, cb, dtb, aneg, bsz, t, ctx_len):
    tm = ROW_TILE
    nt = t // tm
    n = bsz * t
    row = lambda c: pl.BlockSpec((tm, c), lambda b, i: (b * nt + i, 0))
    outs = [(512, BF16), (512, BF16), (1024, BF16), (1024, BF16), (1024, BF16), (1024, BF16), (256, BF16),
            (256, BF16), (LANES, F32), (LANES, F32)]
    return pl.pallas_call(
        functools.partial(_in_cd_kernel, tm, ctx_len),
        grid=(bsz, nt),
        in_specs=[row(D_MODEL), pl.BlockSpec((1, 8, D_MODEL), lambda b, i: (b, 0, 0)), _resident((8, D_MODEL)),
                  _resident((1, D_MODEL)), _resident(w.shape), _resident(cw.shape), _resident(cb.shape),
                  _resident(dtb.shape), _resident(aneg.shape)],
        out_specs=[row(c) for c, _ in outs],
        out_shape=[jax.ShapeDtypeStruct((n, c), dt) for c, dt in outs],
        compiler_params=_cparams("arbitrary", "arbitrary"),
        name="in_proj_cd",
    )(x2, modl, modc, g, w, cw, cb, dtb, aneg)


def _gla_dir(L, C, d, q_ref, k_ref, v_ref, ga_ref, awh_ref, awl_ref, ab_ref, out_ref, s_scr):
    reverse = d == 1
    mask = _tri_mask(L, reverse)
    mask01 = mask.astype(BF16)
    g1, g2, _ = _split3(ga_ref[...])
    pre = _dot(g1, awh_ref[d]) + _dot(g2, awh_ref[d]) + _dot(g1, awl_ref[d])
    lg = _log_sigmoid(pre + ab_ref[d]) * (1.0 / G_TAU)
    ball = _sel_cols(mask01, lg)
    nb = L // C
    last = 0 if reverse else L - 1
    for h in range(G_HEADS):
        ks = slice(h * G_DK, (h + 1) * G_DK)
        b = ball[:, ks] * LOG2E
        qh = q_ref[:, ks].astype(F32)
        kh = k_ref[:, ks].astype(F32)
        vh = v_ref[:, h * G_DV:(h + 1) * G_DV]
        st = s_scr[d, h]
        inter = _dot_nt((qh * jnp.exp2(b)).astype(BF16), st.astype(BF16))
        blocks = []
        for i in range(nb):
            lo, hi = i * C, (i + 1) * C
            if reverse:
                bref = b[hi:hi + 1, :] if i < nb - 1 else jnp.zeros((1, G_DK), F32)
                v0, v1 = lo // LANES * LANES, L
            else:
                bref = b[lo - 1:lo, :] if i > 0 else jnp.zeros((1, G_DK), F32)
                v0, v1 = 0, min(L, -(-hi // LANES) * LANES)
            qi = (qh[lo:hi, :] * jnp.exp2(b[lo:hi, :] - bref)).astype(BF16)
            ki = (kh[v0:v1] * jnp.exp2(bref - b[v0:v1])).astype(BF16)
            pieces = [jnp.zeros((C, v0), F32)] if v0 else []
            pieces.append(_dot_nt(qi, ki))
            if v1 < L:
                pieces.append(jnp.zeros((C, L - v1), F32))
            blocks.append(jnp.concatenate(pieces, axis=1) if len(pieces) > 1 else pieces[0])
        att = jnp.where(mask, jnp.concatenate(blocks, axis=0), 0.0).astype(BF16)
        oh = inter + _dot(att, vh)
        bl = b[last:last + 1, :]
        kd = (kh * jnp.exp2(bl - b)).astype(BF16)
        s_scr[d, h] = st * jnp.exp2(bl) + _dot_tn(vh, kd)
        out_ref[:, h * G_DV:(h + 1) * G_DV] = oh.astype(BF16)


def _gla_kernel(L, C, qf, kf, vf, gaf, qb, kb, vb, gab, awh_ref, awl_ref, ab_ref, of, ob, s_scr):
    _zero_at_start(s_scr)
    _gla_dir(L, C, 0, qf, kf, vf, gaf, awh_ref, awl_ref, ab_ref, of, s_scr)
    _gla_dir(L, C, 1, qb, kb, vb, gab, awh_ref, awl_ref, ab_ref, ob, s_scr)


def _gla(q, k, v, ga, awh, awl, ab, bsz, t, ctx_len):
    L, C = G_CHUNK, G_SUB
    scratch = [pltpu.VMEM((2, G_HEADS, G_DV, G_DK), F32)]
    return _bidir_scan(functools.partial(_gla_kernel, L, C), L, [q, k, v, ga], [], [awh, awl, ab], [1024], scratch,
                       bsz, t, ctx_len, "gla_scan")


def _ssd_dir(L, d, x_ref, b_ref, c_ref, gc_ref, gr_ref, out_ref, s_scr):
    reverse = d == 1
    mask = _tri_mask(L, reverse)
    mask01 = mask.astype(BF16)
    gc = gc_ref[...]
    gr = gr_ref[0]
    bcol = _sel_cols(mask01, gc)
    brow = _sel_rows(gr, mask01)
    last = 0 if reverse else L - 1
    dt_o, la_o = S_HEADS * d, 2 * S_HEADS + S_HEADS * d
    er = lax.broadcasted_iota(I32, (LANES, S_HEADS * S_P), 0)
    ec = jnp.right_shift(lax.broadcasted_iota(I32, (LANES, S_HEADS * S_P), 1), 6)
    sel_la = (er == ec + la_o).astype(BF16)
    x = x_ref[...]
    glane = lax.broadcasted_iota(I32, (L, LANES), 1)
    bla = jnp.where((glane >= la_o) & (glane < la_o + S_HEADS), bcol, 0.0)
    dt_at_la = pltpu.roll(gc, 2 * S_HEADS, 1)
    ebx = _expand(jnp.exp(bla), sel_la)
    bl = bla[last:last + 1, :]
    wx = _expand(jnp.exp(bl - bla) * dt_at_la, sel_la)
    xw = (x.astype(F32) * wx).astype(BF16)
    bcol2, brow2 = bcol * LOG2E, brow * LOG2E
    lane = lax.broadcasted_iota(I32, (L, 2 * S_P), 1)
    for g in range(S_GROUPS):
        ns = slice(g * S_N, (g + 1) * S_N)
        bg, cg = b_ref[:, ns], c_ref[:, ns]
        cb = _dot_nt(cg, bg)
        gs = slice(g * S_HPG * S_P, (g + 1) * S_HPG * S_P)
        st = s_scr[d, g]
        inter = _dot(cg, st.astype(BF16)) * ebx[:, gs]
        parts = []
        for pr in range(S_HPG // 2):
            e0 = g * S_HPG + 2 * pr
            xp = x[:, e0 * S_P:(e0 + 2) * S_P]
            res = []
            for e in (e0, e0 + 1):
                bc, br = bcol2[:, la_o + e:la_o + e + 1], brow2[la_o + e:la_o + e + 1, :]
                dtr = gr[dt_o + e:dt_o + e + 1, :]
                att = (cb * dtr * jnp.exp2(jnp.where(mask, bc - br, -jnp.inf))).astype(BF16)
                res.append(_dot(att, xp))
            parts.append(jnp.where(lane < S_P, res[0], res[1]))
        s_scr[d, g] = st * ebx[last:last + 1, gs] + _dot_tn(bg, xw[:, gs])
        out_ref[:, gs] = (jnp.concatenate(parts, axis=1) + inter).astype(BF16)


def _ssd_kernel(L, xf, bf, cf, gcf, grf, xb, bb, cb, gcb, grb, of, ob, s_scr):
    _zero_at_start(s_scr)
    _ssd_dir(L, 0, xf, bf, cf, gcf, grf, of, s_scr)
    _ssd_dir(L, 1, xb, bb, cb, gcb, grb, ob, s_scr)


def _ssd(x, bm, cm, gcol, grow, bsz, t, ctx_len):
    L = S_CHUNK
    scratch = [pltpu.VMEM((2, S_GROUPS, S_N, S_HPG * S_P), F32)]
    return _bidir_scan(functools.partial(_ssd_kernel, L), L, [x, bm, cm, gcol], [grow], [], [1024], scratch,
                       bsz, t, ctx_len, "ssd_scan")


def _router_kernel(x_ref, modl_ref, g_ref, rw_ref, rb_ref, h_ref, idx_ref, gate_ref):
    h = _rms(x_ref[...], g_ref[...]) * (1.0 + modl_ref[0, 4:5, :]) + modl_ref[0, 3:4, :]
    h_ref[...] = h
    logits = jnp.dot(h, rw_ref[...], preferred_element_type=F32, precision=lax.Precision.HIGHEST) + rb_ref[...]
    lane = lax.broadcasted_iota(I32, logits.shape, 1)
    logits = jnp.where(lane < N_EXPERTS, logits, -jnp.inf)
    lanef = lane.astype(F32)
    m1 = jnp.max(logits, axis=1, keepdims=True)
    i1 = jnp.min(jnp.where(logits == m1, lanef, float(LANES)), axis=1, keepdims=True)
    rest = jnp.where(lanef == i1, -jnp.inf, logits)
    m2 = jnp.max(rest, axis=1, keepdims=True)
    i2 = jnp.min(jnp.where(rest == m2, lanef, float(LANES)), axis=1, keepdims=True)
    e = jnp.exp(m2 - m1)
    g1 = 1.0 / (1.0 + e)
    idx_ref[...] = jnp.where(lane == 0, i1, jnp.where(lane == 1, i2, 0.0)).astype(I32)
    gate_ref[...] = jnp.where(lane == 0, g1, jnp.where(lane == 1, e * g1, 0.0))


def _router(x2, modl, g, rw, rb, bsz, t, ctx_len):
    tm = ROW_TILE
    nt, nct = t // tm, ctx_len // tm
    nl = bsz * (t - ctx_len)
    nlt = nt - nct
    orow = lambda c: pl.BlockSpec((tm, c), lambda b, i: (b * nlt + i, 0))
    return pl.pallas_call(
        _router_kernel,
        grid=(bsz, nlt),
        in_specs=[pl.BlockSpec((tm, D_MODEL), lambda b, i: (b * nt + nct + i, 0)),
                  pl.BlockSpec((1, 8, D_MODEL), lambda b, i: (b, 0, 0)),
                  _resident((1, D_MODEL)), _resident(rw.shape), _resident(rb.shape)],
        out_specs=[orow(D_MODEL), orow(LANES), orow(LANES)],
        out_shape=[jax.ShapeDtypeStruct((nl, D_MODEL), F32), jax.ShapeDtypeStruct((nl, LANES), I32),
                   jax.ShapeDtypeStruct((nl, LANES), F32)],
        compiler_params=_cparams("arbitrary", "arbitrary"),
        name="moe_router",
    )(x2, modl, g, rw, rb)


def _gather_rows(table, idx):
    n_rows, d = idx.shape[0], table.shape[1]
    n_workers = SC_CORES * SC_SUBCORES
    per_worker = n_rows // n_workers
    chunk = SC_GATHER_ROWS
    assert per_worker * n_workers == n_rows and per_worker % chunk == 0
    mesh = plsc.VectorSubcoreMesh(core_axis_name="c", subcore_axis_name="s")

    @functools.partial(
        pl.kernel, mesh=mesh, out_type=jax.ShapeDtypeStruct((n_rows, d), table.dtype),
        scratch_types=[pltpu.VMEM((chunk,), I32), pltpu.VMEM((chunk, d), table.dtype), pltpu.SemaphoreType.DMA],
        name="sc_gather_rows")
    def gather(table_hbm, idx_hbm, out_hbm, idx_v, rows_v, sem):
        base = (lax.axis_index("s") * SC_CORES + lax.axis_index("c")) * per_worker

        @pl.loop(0, per_worker // chunk)
        def _(j):
            off = pl.multiple_of(base + j * chunk, 8)
            pltpu.sync_copy(idx_hbm.at[pl.ds(off, chunk)], idx_v)
            pltpu.async_copy(table_hbm.at[idx_v], rows_v, sem).wait()
            pltpu.sync_copy(rows_v, out_hbm.at[pl.ds(off, chunk)])

    return gather(table, idx)


def _moe_kernel(be_ref, nu_ref, x_ref, w1g_ref, w1u_ref, w2_ref, y_ref, acc_ref):
    i, j = pl.program_id(0), pl.program_id(1)

    @pl.when(i < nu_ref[0])
    def _():
        @pl.when(j == 0)
        def _():
            acc_ref[...] = jnp.zeros_like(acc_ref)

        x = x_ref[...].astype(BF16)
        a = (_silu(_dot(x, w1g_ref[0].astype(BF16))) * _dot(x, w1u_ref[0].astype(BF16))).astype(BF16)
        acc_ref[...] += _dot(a, w2_ref[0].astype(BF16))

        @pl.when(j == pl.num_programs(1) - 1)
        def _():
            y_ref[...] = acc_ref[...]


def _moe_experts(xs, w1, w2, block_e, n_used, n_blocks):
    bm, fc = MOE_ROWS, MOE_FF
    nff = FF_EXPERT // fc
    used = lambda i, nu: jnp.minimum(i, nu[0] - 1)
    ffi = lambda i, j, nu: jnp.where(i < nu[0], j, nff - 1)
    grid_spec = pltpu.PrefetchScalarGridSpec(
        num_scalar_prefetch=2,
        grid=(n_blocks, nff),
        in_specs=[pl.BlockSpec((bm, D_MODEL), lambda i, j, be, nu: (used(i, nu), 0)),
                  pl.BlockSpec((1, D_MODEL, fc), lambda i, j, be, nu: (be[used(i, nu)], 0, ffi(i, j, nu))),
                  pl.BlockSpec((1, D_MODEL, fc), lambda i, j, be, nu: (be[used(i, nu)], 0, nff + ffi(i, j, nu))),
                  pl.BlockSpec((1, fc, D_MODEL), lambda i, j, be, nu: (be[used(i, nu)], ffi(i, j, nu), 0))],
        out_specs=pl.BlockSpec((bm, D_MODEL), lambda i, j, be, nu: (used(i, nu), 0)),
        scratch_shapes=[pltpu.VMEM((bm, D_MODEL), F32)],
    )
    return pl.pallas_call(
        _moe_kernel,
        grid_spec=grid_spec,
        out_shape=jax.ShapeDtypeStruct((n_blocks * bm, D_MODEL), F32),
        compiler_params=_cparams("arbitrary", "arbitrary"),
        name="moe_experts",
    )(block_e, n_used, xs, w1, w1, w2)


def _combine_kernel(y0_ref, y1_ref, gate_ref, x_ref, modl_ref, g_ref, o_ref):
    gt = gate_ref[...]
    y = gt[:, 0:1] * y0_ref[...] + gt[:, 1:2] * y1_ref[...]
    o_ref[...] = x_ref[...] + modl_ref[0, 5:6, :] * _rms(y, g_ref[...])


def _combine(yk, gates, x2, modl, g, bsz, t, ctx_len):
    tm = ROW_TILE
    nt, nct = t // tm, ctx_len // tm
    nlt = nt - nct
    nl = bsz * (t - ctx_len)
    orow = lambda c: pl.BlockSpec((tm, c), lambda b, i: (b * nlt + i, 0))
    second = pl.BlockSpec((tm, D_MODEL), lambda b, i: (nl // tm + b * nlt + i, 0))
    return pl.pallas_call(
        _combine_kernel,
        grid=(bsz, nlt),
        in_specs=[orow(D_MODEL), second, orow(LANES),
                  pl.BlockSpec((tm, D_MODEL), lambda b, i: (b * nt + nct + i, 0)),
                  pl.BlockSpec((1, 8, D_MODEL), lambda b, i: (b, 0, 0)), _resident((1, D_MODEL))],
        out_specs=orow(D_MODEL),
        out_shape=jax.ShapeDtypeStruct((nl, D_MODEL), F32),
        compiler_params=_cparams("arbitrary", "arbitrary"),
        name="moe_combine",
    )(yk, yk, gates, x2, modl, g)


def _routing_tables(top_idx, n_tok):
    bm = MOE_ROWS
    flat_e = top_idx.reshape(-1)
    n_assign = flat_e.shape[0]
    onehot = (flat_e[:, None] == jnp.arange(N_EXPERTS, dtype=I32)[None, :]).astype(I32)
    incl = jnp.cumsum(onehot, axis=0)
    counts = incl[-1]
    rank = jnp.take_along_axis(incl, flat_e[:, None], axis=1)[:, 0] - 1
    padded = (counts + bm - 1) // bm * bm
    ends_p = jnp.cumsum(padded)
    pstarts = ends_p - padded
    slot = (pstarts[flat_e] + rank).astype(I32)
    n_blocks = n_assign // bm + N_EXPERTS
    flat_tok = jnp.arange(n_assign, dtype=I32) // TOP_K
    slot_tok = (jnp.arange(n_blocks * bm, dtype=I32) % n_tok).at[slot].set(flat_tok)
    block_e = jnp.minimum(jnp.searchsorted(ends_p, jnp.arange(n_blocks, dtype=I32) * bm, side='right'),
                          N_EXPERTS - 1).astype(I32)
    n_used = (ends_p[-1] // bm).astype(I32).reshape(1)
    return slot, slot_tok, block_e, n_used, n_blocks


def _row_layout(gcol, ncols, L):
    n = gcol.shape[0]
    return gcol[:, :ncols].reshape(n // L, L, ncols).transpose(0, 2, 1)


def kernel(x, c, ctx, c_ctx, mod_w, mod_b, norm_g, ab_in_w, m_conv_w, m_conv_b, m_gate_b, m_norm_w, l_conv_w,
           l_conv_b, l_wa, l_ba, l_wx, l_bx, l_lam, ab_out_w, ffn_w1, ffn_w2, cd_in_w, g_alpha_w, g_alpha_b,
           g_norm_w, s_conv_w, s_conv_b, s_dt_bias, s_A_log, s_D, s_norm_w, cd_out_w, router_w, router_b,
           moe_w1, moe_w2):
    bsz, seq, dm = x.shape
    ctx_len = ctx.shape[1]
    t = ctx_len + seq
    n = bsz * t
    x2 = jnp.concatenate([ctx, x], axis=1).reshape(n, dm)

    c_all = jnp.concatenate([c, c_ctx[None, :], jnp.zeros((-(bsz + 1) % 8, dm), F32)], axis=0)
    mods = _modulation(c_all, mod_w, mod_b).reshape(mod_w.shape[0], c_all.shape[0], 6, dm)
    pad2 = jnp.zeros((bsz, 2, dm), F32)

    def layer_mods(layer):
        modl = jnp.concatenate([mods[layer, :bsz], pad2], axis=1)
        modc = jnp.concatenate([mods[layer, bsz], pad2[0]], axis=0)
        return modl, modc

    modl, modc = layer_mods(0)
    g = norm_g[0]
    w = ab_in_w[0]
    o_qk, o_v, o_o, o_gt, o_lx, o_lg = 0, 1024, 2048, 3072, 3088, 4112
    w_in = jnp.concatenate([w[:, o_qk:o_v], w[:, o_v:o_o], w[:, o_o:o_gt], w[:, o_lx:o_lg], w[:, o_lg:],
                            w[:, o_gt:o_lx], jnp.zeros((dm, LANES - 16), F32)], axis=1).astype(BF16)
    gb = jnp.concatenate([m_gate_b[0].reshape(1, 16), jnp.zeros((1, LANES - 16), F32)], axis=1)
    q, k, v, og, lx, glg, gates = _in_ab(x2, modl, modc, g[0:1], w_in, m_conv_w[0], m_conv_b[0][None],
                                         l_conv_w[0], l_conv_b[0][None], gb, bsz, t, ctx_len)
    hm_f, hm_b = _mlstm(q, k, v, gates, _row_layout(gates, 16, M_CHUNK), bsz, t, ctx_len)
    hl_f, hl_b = _lru(lx, l_wa[0].astype(BF16), l_wx[0].astype(BF16), l_ba[0][:, None], l_bx[0][:, None],
                      l_lam[0][:, None], bsz, t, ctx_len)
    x2 = _out_proj(_out_ab_kernel, "out_proj_ab", [hm_f, hm_b, og, hl_f, hl_b, glg], [m_norm_w[0][None]],
                   x2, modl, modc, g[1:2], ab_out_w[0].astype(BF16), bsz, t, ctx_len)
    x2 = _ffn(x2, modl, modc, g[2:3], g[3:4], ffn_w1[0].astype(BF16), ffn_w2[0].astype(BF16), bsz, t, ctx_len)

    modl, modc = layer_mods(1)
    g = norm_g[1]
    w = cd_in_w[0]
    o_gv, o_gr, o_ga, o_z, o_xbc, o_dt = 1024, 2048, 3072, 3104, 4128, 5664
    zpad = jnp.zeros((dm, LANES - 32), F32)
    w_in = jnp.concatenate([w[:, :o_gv], w[:, o_gv:o_gr], w[:, o_gr:o_ga], w[:, o_z:o_xbc], w[:, o_xbc:o_dt],
                            w[:, o_ga:o_z], zpad, w[:, o_dt:], w[:, o_dt:], jnp.zeros((dm, LANES - 64), F32)],
                           axis=1).astype(BF16)
    dtb = jnp.concatenate([s_dt_bias[0].reshape(1, 32), s_dt_bias[0].reshape(1, 32),
                           jnp.zeros((1, LANES - 64), F32)], axis=1)
    aneg = jnp.concatenate([jnp.zeros((1, 32), F32), -jnp.exp(s_A_log[0].reshape(1, 32)),
                            jnp.zeros((1, LANES - 64), F32)], axis=1)
    gq, gk, gv, grs, zs, sx, sb, sc, ga, dts = _in_cd(x2, modl, modc, g[0:1], w_in, s_conv_w[0], s_conv_b[0][None],
                                                      dtb, aneg, bsz, t, ctx_len)
    aw = jnp.zeros((2, LANES, G_HEADS * G_DK), F32)
    aw = aw.at[0, 0:G_RANK].set(g_alpha_w[0, 0]).at[1, G_RANK:2 * G_RANK].set(g_alpha_w[0, 1])
    awh = aw.astype(BF16)
    awl = (aw - awh.astype(F32)).astype(BF16)
    og_f, og_b = _gla(gq, gk, gv, ga, awh, awl, g_alpha_b[0][:, None], bsz, t, ctx_len)
    ys_f, ys_b = _ssd(sx, sb, sc, dts, _row_layout(dts, 4 * S_HEADS, S_CHUNK), bsz, t, ctx_len)
    dskip = jnp.repeat(s_D[0], S_P)[None]
    x2 = _out_proj(_out_cd_kernel, "out_proj_cd", [og_f, og_b, grs, ys_f, ys_b, sx, zs],
                   [g_norm_w[0][None], dskip, s_norm_w[0][None]],
                   x2, modl, modc, g[1:2], cd_out_w[0].astype(BF16), bsz, t, ctx_len)

    rw = jnp.concatenate([router_w[0], jnp.zeros((dm, LANES - N_EXPERTS), F32)], axis=1)
    rb = jnp.concatenate([router_b[0][None], jnp.zeros((1, LANES - N_EXPERTS), F32)], axis=1)
    h2, top_idx, top_gate = _router(x2, modl, g[2:3], rw, rb, bsz, t, ctx_len)
    n_tok = bsz * seq
    slot, slot_tok, block_e, n_used, n_blocks = _routing_tables(top_idx[:, :TOP_K], n_tok)
    xs = _gather_rows(h2, slot_tok)
    ys_e = _moe_experts(xs, moe_w1[0], moe_w2[0], block_e, n_used, n_blocks)
    slot2 = slot.reshape(n_tok, TOP_K)
    yk = _gather_rows(ys_e, jnp.concatenate([slot2[:, 0], slot2[:, 1]]))
    out = _combine(yk, top_gate, x2, modl, g[3:4], bsz, t, ctx_len)
    return out.reshape(bsz, seq, dm)
```

```python
import functools

import jax
import jax.numpy as jnp
from jax import lax
from jax.experimental import pallas as pl
from jax.experimental.pallas import tpu as pltpu
from jax.experimental.pallas import tpu_sc as plsc

F32, BF16, I32 = jnp.float32, jnp.bfloat16, jnp.int32

D_MODEL = 1024
GRID_W = 64
EPS = 1e-6
LOG2E = 1.4426950408889634
M_HEADS, M_DK, M_DV = 4, 128, 256
LRU_BLOCKS, LRU_BW, LRU_C = 8, 128, 8.0
G_HEADS, G_DK, G_DV, G_RANK, G_TAU = 4, 128, 256, 16, 16.0
S_HEADS, S_P, S_N, S_GROUPS, S_HPG = 16, 64, 128, 2, 8
FF_DENSE = 2816
N_EXPERTS, TOP_K, FF_EXPERT = 8, 2, 3584

LANES = 128
VMEM_LIMIT = 56 * 1024 * 1024
ROW_TILE = 256
BIG_ROW_TILE_MAX = 640
PROJ_COLS = 256
M_CHUNK = 256
G_CHUNK, G_SUB = 256, 16
S_CHUNK = 256
LRU_TILE = 256
FF_STEP = 256
MOE_ROWS = 1024
MOE_FF = 512
SC_CORES, SC_SUBCORES = 2, 16
SC_GATHER_ROWS = 64


def _cparams(*sem):
    return pltpu.CompilerParams(dimension_semantics=sem, vmem_limit_bytes=VMEM_LIMIT)


def _silu(x):
    return x * jax.nn.sigmoid(x)


def _softplus(x):
    return jnp.maximum(x, 0.0) + jnp.log1p(jnp.exp(-jnp.abs(x)))


def _log_sigmoid(x):
    return jnp.minimum(x, 0.0) - jnp.log1p(jnp.exp(-jnp.abs(x)))


def _dot(a, b):
    return jnp.dot(a, b, preferred_element_type=F32)


def _dot_nt(a, b):
    return lax.dot_general(a, b, (((1,), (1,)), ((), ())), preferred_element_type=F32)


def _dot_tn(a, b):
    return lax.dot_general(a, b, (((0,), (0,)), ((), ())), preferred_element_type=F32)


def _split3(f):
    f1 = f.astype(BF16)
    r = f - f1.astype(F32)
    f2 = r.astype(BF16)
    f3 = (r - f2.astype(F32)).astype(BF16)
    return f1, f2, f3


def _sel_cols(mask01, f):
    p1, p2, p3 = _split3(f)
    return _dot(mask01, p1) + _dot(mask01, p2) + _dot(mask01, p3)


def _sel_rows(f, mask01):
    p1, p2, p3 = _split3(f)
    return _dot_nt(p1, mask01) + _dot_nt(p2, mask01) + _dot_nt(p3, mask01)


def _expand(f, sel01):
    p1, p2, p3 = _split3(f)
    return _dot(p1, sel01) + _dot(p2, sel01) + _dot(p3, sel01)


def _rms(u, g):
    return u * lax.rsqrt(jnp.mean(u * u, axis=-1, keepdims=True) + EPS) * g


def _head_rms(u, g, n_heads):
    w = u.shape[-1] // n_heads
    return jnp.concatenate([_rms(u[:, h * w:(h + 1) * w], g[:, h * w:(h + 1) * w]) for h in range(n_heads)], axis=1)


def _row_ids(tile_idx, tm):
    return tile_idx * tm + lax.broadcasted_iota(I32, (tm, 1), 0)


def _mod(modl_ref, modc_ref, idx, is_ctx):
    return jnp.where(is_ctx, modc_ref[idx:idx + 1, :], modl_ref[0, idx:idx + 1, :])


def _adaln(x, g, modl_ref, modc_ref, shift_idx, is_ctx):
    shift = _mod(modl_ref, modc_ref, shift_idx, is_ctx)
    scale = _mod(modl_ref, modc_ref, shift_idx + 1, is_ctx)
    return _rms(x, g) * (1.0 + scale) + shift


def _neighbour_tiles(r, step, is_lat):
    n_tiles, per_seg = r.shape[0], GRID_W // 8
    zero = jnp.zeros((1,) + r.shape[1:], r.dtype)

    def across(src):
        return jnp.where(is_lat, 0.0, r[src:src + 1]) if 0 <= src < n_tiles else zero

    pieces = []
    for s0 in range(0, n_tiles, per_seg):
        if step > 0:
            pieces += [across(s0 - 1), r[s0:s0 + per_seg - 1]]
        else:
            pieces += [r[s0 + 1:s0 + per_seg], across(s0 + per_seg)]
    return jnp.concatenate(pieces, axis=0)


def _dwconv(y, cw, cb, is_lat):
    tm, n = y.shape
    y3 = y.reshape(tm // 8, 8, n)
    sub = lax.broadcasted_iota(I32, (1, 8, 1), 1)
    w = lambda j: cw[j:j + 1, :].reshape(1, 1, n)
    r1, r2, r7 = pltpu.roll(y3, 1, 1), pltpu.roll(y3, 2, 1), pltpu.roll(y3, 7, 1)
    ym1 = jnp.where(sub >= 1, r1, _neighbour_tiles(r1, 1, is_lat))
    ym2 = jnp.where(sub >= 2, r2, _neighbour_tiles(r2, 1, is_lat))
    yp1 = jnp.where(sub <= 6, r7, _neighbour_tiles(r7, -1, is_lat))
    out = cb.reshape(1, 1, n) + w(0) * ym2 + w(1) * ym1 + w(2) * y3 + w(3) * yp1
    return out.reshape(tm, n)


def _resident(shape):
    nd = len(shape)
    return pl.BlockSpec(shape, lambda *_: (0,) * nd, pipeline_mode=pl.Buffered(1))


def _big_row_tile(t):
    return max(tm for tm in range(8, BIG_ROW_TILE_MAX + 1, 8) if t % tm == 0)


def _mod_kernel(c_ref, w_ref, b_ref, o_ref):
    a = _silu(c_ref[...])
    o_ref[0] = jnp.dot(a, w_ref[0], preferred_element_type=F32, precision=lax.Precision.HIGHEST) + b_ref[0]


def _modulation(c_all, mod_w, mod_b):
    depth, d, n6 = mod_w.shape
    rows = c_all.shape[0]
    tn = 1024
    return pl.pallas_call(
        _mod_kernel,
        grid=(depth, n6 // tn),
        in_specs=[pl.BlockSpec((rows, d), lambda l, j: (0, 0)),
                  pl.BlockSpec((1, d, tn), lambda l, j: (l, 0, j)),
                  pl.BlockSpec((1, 1, tn), lambda l, j: (l, 0, j))],
        out_specs=pl.BlockSpec((1, rows, tn), lambda l, j: (l, 0, j)),
        out_shape=jax.ShapeDtypeStruct((depth, rows, n6), F32),
        compiler_params=_cparams("arbitrary", "arbitrary"),
        name="modulation",
    )(c_all, mod_w, mod_b.reshape(depth, 1, n6))


def _project(h_ref, w_ref, w_col, out_ref, epilogue):
    for c in range(0, out_ref.shape[1], PROJ_COLS):
        y = _dot(h_ref[...], w_ref[:, w_col + c:w_col + c + PROJ_COLS])
        out_ref[:, c:c + PROJ_COLS] = epilogue(y, c).astype(out_ref.dtype)


def _in_ab_kernel(tm, ctx_len, x_ref, modl_ref, modc_ref, g_ref, w_ref, mcw_ref, mcb_ref, lcw_ref, lcb_ref,
                  gb_ref, q_ref, k_ref, v_ref, og_ref, lx_ref, glg_ref, gates_ref, h_scr):
    r = _row_ids(pl.program_id(1), tm)
    is_ctx = r < ctx_len
    h_scr[...] = _adaln(x_ref[...], g_ref[...], modl_ref, modc_ref, 0, is_ctx).astype(BF16)
    is_lat = pl.program_id(1) * tm >= ctx_len

    def conv(cw_ref, cb_ref):
        return lambda y, c: _dwconv(y, cw_ref[:, c:c + PROJ_COLS], cb_ref[:, c:c + PROJ_COLS], is_lat)

    qc, kc, lc = conv(mcw_ref, mcb_ref), conv(mcw_ref, mcb_ref), conv(lcw_ref, lcb_ref)
    _project(h_scr, w_ref, 0, q_ref, lambda y, c: _silu(qc(y, c)))
    _project(h_scr, w_ref, 512, k_ref, lambda y, c: _silu(kc(y, 512 + c)) * (M_DK ** -0.5))
    _project(h_scr, w_ref, 1024, v_ref, lambda y, c: y)
    _project(h_scr, w_ref, 2048, og_ref, lambda y, c: jax.nn.sigmoid(y))
    _project(h_scr, w_ref, 3072, lx_ref, lc)
    _project(h_scr, w_ref, 4096, glg_ref, lambda y, c: jax.nn.gelu(y))
    gt = _dot(h_scr[...], w_ref[:, 5120:5248]) + gb_ref[...]
    lane = lax.broadcasted_iota(I32, gt.shape, 1)
    gates_ref[...] = jnp.where(lane >= 2 * M_HEADS, _log_sigmoid(gt), gt)


def _in_ab(x2, modl, modc, g, w, mcw, mcb, lcw, lcb, gb, bsz, t, ctx_len):
    tm = ROW_TILE
    nt = t // tm
    n = bsz * t
    row = lambda c: pl.BlockSpec((tm, c), lambda b, i: (b * nt + i, 0))
    outs = [(512, BF16), (512, BF16), (1024, BF16), (1024, BF16), (1024, F32), (1024, BF16), (LANES, F32)]
    return pl.pallas_call(
        functools.partial(_in_ab_kernel, tm, ctx_len),
        grid=(bsz, nt),
        in_specs=[row(D_MODEL),
                  pl.BlockSpec((1, 8, D_MODEL), lambda b, i: (b, 0, 0)),
                  _resident((8, D_MODEL)), _resident((1, D_MODEL)), _resident(w.shape),
                  _resident(mcw.shape), _resident(mcb.shape), _resident(lcw.shape), _resident(lcb.shape),
                  _resident(gb.shape)],
        out_specs=[row(c) for c, _ in outs],
        out_shape=[jax.ShapeDtypeStruct((n, c), dt) for c, dt in outs],
        scratch_shapes=[pltpu.VMEM((tm, D_MODEL), BF16)],
        compiler_params=_cparams("arbitrary", "arbitrary"),
        name="in_proj_ab",
    )(x2, modl, modc, g, w, mcw, mcb, lcw, lcb, gb)


def _reverse_order(n_ctx_chunks, n_chunks):
    return lambda c: jnp.where(c < n_ctx_chunks, n_ctx_chunks - 1 - c, n_chunks - 1 - (c - n_ctx_chunks))


def _tri_mask(L, reverse):
    row = lax.broadcasted_iota(I32, (L, L), 0)
    col = lax.broadcasted_iota(I32, (L, L), 1)
    return (col >= row) if reverse else (col <= row)


def _bidir_scan(kern, L, row_ins, row3_ins, res_ins, out_cols, scratch, bsz, t, ctx_len, name):
    nc = t // L
    rev = _reverse_order(ctx_len // L, nc)
    fwd = lambda i: i
    n = bsz * t

    def row(c, order):
        return pl.BlockSpec((L, c), lambda b, i: (b * nc + order(i), 0))

    def row3(shape, order):
        return pl.BlockSpec((1,) + tuple(shape[1:]), lambda b, i: (b * nc + order(i), 0, 0))

    in_specs, args = [], []
    for order in (fwd, rev):
        in_specs += [row(a.shape[1], order) for a in row_ins] + [row3(a.shape, order) for a in row3_ins]
        args += list(row_ins) + list(row3_ins)
    in_specs += [_resident(a.shape) for a in res_ins]
    args += list(res_ins)
    return pl.pallas_call(
        kern,
        grid=(bsz, nc),
        in_specs=in_specs,
        out_specs=[row(c, fwd) for c in out_cols] + [row(c, rev) for c in out_cols],
        out_shape=[jax.ShapeDtypeStruct((n, c), BF16) for c in out_cols] * 2,
        scratch_shapes=scratch,
        compiler_params=_cparams("arbitrary", "arbitrary"),
        name=name,
    )(*args)


def _zero_at_start(*scratch):
    @pl.when(pl.program_id(1) == 0)
    def _():
        for s in scratch:
            s[...] = jnp.zeros_like(s)


def _mlstm_dir(L, d, q_ref, k_ref, v_ref, gc_ref, gr_ref, out_ref, c_scr, m_scr):
    reverse = d == 1
    mask = _tri_mask(L, reverse)
    mask01 = mask.astype(BF16)
    gc = gc_ref[...]
    gr = gr_ref[0]
    bcol = _sel_cols(mask01, gc)
    brow = _sel_rows(gr, mask01)
    last = 0 if reverse else L - 1
    ones_col = (lax.broadcasted_iota(I32, (L, LANES), 1) == 0).astype(BF16)
    for h in range(M_HEADS):
        fo, io = 2 * M_HEADS + M_HEADS * d + h, M_HEADS * d + h
        bc, br = bcol[:, fo:fo + 1], brow[fo:fo + 1, :]
        ic, ir = gc[:, io:io + 1], gr[io:io + 1, :]
        m = m_scr[d, h:h + 1, 0:1]
        g = bc + m
        dlog = jnp.where(mask, bc - br + ir, -jnp.inf)
        mt = jnp.maximum(g, jnp.max(dlog, axis=1, keepdims=True))
        w_inter = jnp.exp(g - mt)
        p = jnp.exp(dlog - mt)
        qh = q_ref[:, h * M_DK:(h + 1) * M_DK]
        kh = k_ref[:, h * M_DK:(h + 1) * M_DK]
        vx = jnp.concatenate([v_ref[:, h * M_DV:(h + 1) * M_DV], ones_col], axis=1)
        sc = (_dot_nt(qh, kh) * p).astype(BF16)
        cx = c_scr[d, h]
        numx = w_inter * _dot(qh, cx.astype(BF16)) + _dot(sc, vx)
        den = numx[:, M_DV:M_DV + 1]
        hh = numx[:, 0:M_DV] * (1.0 / jnp.maximum(jnp.abs(den), jnp.exp(-mt)))
        bl = bc[last:last + 1, :]
        a_s = bl - bc + ic
        m_new = jnp.maximum(bl + m, jnp.max(a_s, axis=0, keepdims=True))
        decay = jnp.exp(bl + m - m_new)
        ws = jnp.exp(a_s - m_new)
        kw = (kh.astype(F32) * ws).astype(BF16)
        c_scr[d, h] = decay * cx + _dot_tn(kw, vx)
        m_scr[d, h:h + 1, 0:1] = m_new
        out_ref[:, h * M_DV:(h + 1) * M_DV] = hh.astype(BF16)


def _mlstm_kernel(L, qf, kf, vf, gcf, grf, qb, kb, vb, gcb, grb, of, ob, c_scr, m_scr):
    _zero_at_start(c_scr, m_scr)
    _mlstm_dir(L, 0, qf, kf, vf, gcf, grf, of, c_scr, m_scr)
    _mlstm_dir(L, 1, qb, kb, vb, gcb, grb, ob, c_scr, m_scr)


def _mlstm(q, k, v, gcol, grow, bsz, t, ctx_len):
    L = M_CHUNK
    scratch = [pltpu.VMEM((2, M_HEADS, M_DK, M_DV + LANES), F32), pltpu.VMEM((2, 8, LANES), F32)]
    return _bidir_scan(functools.partial(_mlstm_kernel, L), L, [q, k, v, gcol], [grow], [], [1024], scratch,
                       bsz, t, ctx_len, "mlstm_scan")


def _lru_gates(L, d, u_ref, wa_ref, wx_ref, ba_ref, bx_ref, lam_ref, a_scr, b_scr):
    reverse = d == 1
    u = u_ref[...]
    ub = u.astype(BF16)
    sp = _softplus(-lam_ref[d])
    for n in range(LRU_BLOCKS):
        sl = slice(n * LRU_BW, (n + 1) * LRU_BW)
        rg = jax.nn.sigmoid(_dot(ub[:, sl], wa_ref[d, n]) + ba_ref[d, :, sl])
        ig = jax.nn.sigmoid(_dot(ub[:, sl], wx_ref[d, n]) + bx_ref[d, :, sl])
        log_a = -LRU_C * rg * sp[:, sl]
        a = jnp.exp(log_a)
        a_scr[d, :, sl] = a
        b_scr[d, :, sl] = jnp.sqrt(-jnp.tanh(log_a) * (a * a + 1.0)) * (ig * u[:, sl])
    a = a_scr[d].reshape(L // 8, 8, LRU_BLOCKS * LRU_BW)
    b = b_scr[d].reshape(L // 8, 8, LRU_BLOCKS * LRU_BW)
    sub = lax.broadcasted_iota(I32, (1, 8, 1), 1)
    for s in (1, 2, 4):
        if reverse:
            a_sh, b_sh, keep = pltpu.roll(a, 8 - s, 1), pltpu.roll(b, 8 - s, 1), sub < 8 - s
        else:
            a_sh, b_sh, keep = pltpu.roll(a, s, 1), pltpu.roll(b, s, 1), sub >= s
        b = jnp.where(keep, a * b_sh + b, b)
        a = jnp.where(keep, a * a_sh, a)
    a_scr[d] = a.reshape(L, LRU_BLOCKS * LRU_BW)
    b_scr[d] = b.reshape(L, LRU_BLOCKS * LRU_BW)


def _lru_kernel(L, uf, ub, wa_ref, wx_ref, ba_ref, bx_ref, lam_ref, of, ob, a_scr, b_scr, h_scr):
    _zero_at_start(h_scr)
    _lru_gates(L, 0, uf, wa_ref, wx_ref, ba_ref, bx_ref, lam_ref, a_scr, b_scr)
    _lru_gates(L, 1, ub, wa_ref, wx_ref, ba_ref, bx_ref, lam_ref, a_scr, b_scr)
    ng = L // 8

    def body(j, carry):
        hf, hb = carry
        rf = pl.ds(pl.multiple_of(j * 8, 8), 8)
        rb = pl.ds(pl.multiple_of((ng - 1 - j) * 8, 8), 8)
        gf = b_scr[0, rf, :] + a_scr[0, rf, :] * hf
        gb = b_scr[1, rb, :] + a_scr[1, rb, :] * hb
        b_scr[0, rf, :] = gf
        b_scr[1, rb, :] = gb
        return gf[7:8, :], gb[0:1, :]

    hf, hb = lax.fori_loop(0, ng, body, (h_scr[0:1, :], h_scr[1:2, :]))
    h_scr[0:1, :] = hf
    h_scr[1:2, :] = hb
    of[...] = b_scr[0].astype(BF16)
    ob[...] = b_scr[1].astype(BF16)


def _lru(u, wa, wx, ba, bx, lam, bsz, t, ctx_len):
    L = LRU_TILE
    scratch = [pltpu.VMEM((2, L, 1024), F32), pltpu.VMEM((2, L, 1024), F32), pltpu.VMEM((8, 1024), F32)]
    return _bidir_scan(functools.partial(_lru_kernel, L), L, [u], [], [wa, wx, ba, bx, lam], [1024], scratch,
                       bsz, t, ctx_len, "rglru_scan")


def _out_tail(tm, ctx_len, a1, a2, x_ref, modl_ref, modc_ref, g_ref, w_ref, xo_ref):
    is_ctx = _row_ids(pl.program_id(1), tm) < ctx_len
    half = w_ref.shape[0] // 2
    y = _dot(a1.astype(BF16), w_ref[0:half, :]) + _dot(a2.astype(BF16), w_ref[half:, :])
    xo_ref[...] = x_ref[...] + _mod(modl_ref, modc_ref, 2, is_ctx) * _rms(y, g_ref[...])


def _out_ab_kernel(tm, ctx_len, hf, hb, og, lf, lb, glg, nw, x_ref, modl_ref, modc_ref, g_ref, w_ref, xo_ref):
    hm = _head_rms(hf[...].astype(F32) + hb[...].astype(F32), nw[...], M_HEADS) * og[...].astype(F32)
    hl = (lf[...].astype(F32) + lb[...].astype(F32)) * glg[...].astype(F32)
    _out_tail(tm, ctx_len, hm, hl, x_ref, modl_ref, modc_ref, g_ref, w_ref, xo_ref)


def _out_cd_kernel(tm, ctx_len, gf, gb, grs, yf, yb, sx, zs, gnw, dsk, snw, x_ref, modl_ref, modc_ref, g_ref,
                   w_ref, xo_ref):
    og = _head_rms(gf[...].astype(F32) + gb[...].astype(F32), gnw[...], G_HEADS) * grs[...].astype(F32)
    ys = yf[...].astype(F32) + yb[...].astype(F32) + dsk[...] * sx[...].astype(F32)
    ys = _rms(ys * zs[...].astype(F32), snw[...])
    _out_tail(tm, ctx_len, og, ys, x_ref, modl_ref, modc_ref, g_ref, w_ref, xo_ref)


def _out_proj(kern, name, acts, vecs, x2, modl, modc, g, w, bsz, t, ctx_len):
    tm = _big_row_tile(t)
    nt = t // tm
    row = lambda c: pl.BlockSpec((tm, c), lambda b, i: (b * nt + i, 0))
    return pl.pallas_call(
        functools.partial(kern, tm, ctx_len),
        grid=(bsz, nt),
        in_specs=[row(1024)] * len(acts) + [_resident((1, 1024))] * len(vecs)
        + [row(D_MODEL), pl.BlockSpec((1, 8, D_MODEL), lambda b, i: (b, 0, 0)),
           _resident((8, D_MODEL)), _resident((1, D_MODEL)), _resident(w.shape)],
        out_specs=row(D_MODEL),
        out_shape=jax.ShapeDtypeStruct(x2.shape, F32),
        compiler_params=_cparams("arbitrary", "arbitrary"),
        name=name,
    )(*acts, *vecs, x2, modl, modc, g, w)


def _ffn_kernel(tm, ctx_len, x_ref, modl_ref, modc_ref, g2_ref, g3_ref, w1_ref, w2_ref, xo_ref):
    is_ctx = _row_ids(pl.program_id(1), tm) < ctx_len
    x = x_ref[...]
    h = _adaln(x, g2_ref[...], modl_ref, modc_ref, 3, is_ctx).astype(BF16)
    ff = w2_ref.shape[0]
    acc = jnp.zeros((tm, D_MODEL), F32)
    for j in range(ff // FF_STEP):
        lo = j * FF_STEP
        gj = _dot(h, w1_ref[:, lo:lo + FF_STEP])
        uj = _dot(h, w1_ref[:, ff + lo:ff + lo + FF_STEP])
        acc = acc + _dot((_silu(gj) * uj).astype(BF16), w2_ref[lo:lo + FF_STEP, :])
    xo_ref[...] = x + _mod(modl_ref, modc_ref, 5, is_ctx) * _rms(acc, g3_ref[...])


def _ffn(x2, modl, modc, g2, g3, w1, w2, bsz, t, ctx_len):
    tm = _big_row_tile(t)
    nt = t // tm
    row = lambda c: pl.BlockSpec((tm, c), lambda b, i: (b * nt + i, 0))
    return pl.pallas_call(
        functools.partial(_ffn_kernel, tm, ctx_len),
        grid=(bsz, nt),
        in_specs=[row(D_MODEL), pl.BlockSpec((1, 8, D_MODEL), lambda b, i: (b, 0, 0)), _resident((8, D_MODEL)),
                  _resident((1, D_MODEL)), _resident((1, D_MODEL)), _resident(w1.shape), _resident(w2.shape)],
        out_specs=row(D_MODEL),
        out_shape=jax.ShapeDtypeStruct(x2.shape, F32),
        compiler_params=_cparams("arbitrary", "arbitrary"),
        name="dense_swiglu",
    )(x2, modl, modc, g2, g3, w1, w2)


def _in_cd_kernel(tm, ctx_len, x_ref, modl_ref, modc_ref, g_ref, w_ref, cw_ref, cb_ref, dtb_ref, aneg_ref,
                  gq_ref, gk_ref, gv_ref, gr_ref, z_ref, sx_ref, sb_ref, sc_ref, ga_ref, dts_ref, h_scr):
    r = _row_ids(pl.program_id(1), tm)
    is_ctx = r < ctx_len
    h_scr[...] = _adaln(x_ref[...], g_ref[...], modl_ref, modc_ref, 0, is_ctx).astype(BF16)
    is_lat = pl.program_id(1) * tm >= ctx_len

    def conv_silu(col0):
        return lambda y, c: _silu(_dwconv(y, cw_ref[:, col0 + c:col0 + c + PROJ_COLS],
                                          cb_ref[:, col0 + c:col0 + c + PROJ_COLS], is_lat))

    _project(h_scr, w_ref, 0, gq_ref, lambda y, c: y * (G_DK ** -0.5))
    _project(h_scr, w_ref, 512, gk_ref, lambda y, c: y)
    _project(h_scr, w_ref, 1024, gv_ref, lambda y, c: y)
    _project(h_scr, w_ref, 2048, gr_ref, lambda y, c: _silu(y))
    _project(h_scr, w_ref, 3072, z_ref, lambda y, c: _silu(y))
    _project(h_scr, w_ref, 4096, sx_ref, conv_silu(0))
    _project(h_scr, w_ref, 5120, sb_ref, conv_silu(1024))
    _project(h_scr, w_ref, 5376, sc_ref, conv_silu(1280))
    ga_ref[...] = _dot(h_scr[...], w_ref[:, 5632:5760])
    dt = _softplus(_dot(h_scr[...], w_ref[:, 5760:5888]) + dtb_ref[...])
    lane = lax.broadcasted_iota(I32, dt.shape, 1)
    dts_ref[...] = jnp.where(lane < 2 * S_HEADS, dt, dt * aneg_ref[...])


def _in_cd(x2, modl, modc, g, w, cw, cb, dtb, aneg, bsz, t, ctx_len):
    tm = ROW_TILE
    nt = t // tm
    n = bsz * t
    row = lambda c: pl.BlockSpec((tm, c), lambda b, i: (b * nt + i, 0))
    outs = [(512, BF16), (512, BF16), (1024, BF16), (1024, BF16), (1024, BF16), (1024, BF16), (256, BF16),
            (256, BF16), (LANES, F32), (LANES, F32)]
    return pl.pallas_call(
        functools.partial(_in_cd_kernel, tm, ctx_len),
        grid=(bsz, nt),
        in_specs=[row(D_MODEL), pl.BlockSpec((1, 8, D_MODEL), lambda b, i: (b, 0, 0)), _resident((8, D_MODEL)),
                  _resident((1, D_MODEL)), _resident(w.shape), _resident(cw.shape), _resident(cb.shape),
                  _resident(dtb.shape), _resident(aneg.shape)],
        out_specs=[row(c) for c, _ in outs],
        out_shape=[jax.ShapeDtypeStruct((n, c), dt) for c, dt in outs],
        scratch_shapes=[pltpu.VMEM((tm, D_MODEL), BF16)],
        compiler_params=_cparams("arbitrary", "arbitrary"),
        name="in_proj_cd",
    )(x2, modl, modc, g, w, cw, cb, dtb, aneg)


def _gla_dir(L, C, d, q_ref, k_ref, v_ref, ga_ref, awh_ref, awl_ref, ab_ref, out_ref, s_scr):
    reverse = d == 1
    mask = _tri_mask(L, reverse)
    mask01 = mask.astype(BF16)
    g1, g2, _ = _split3(ga_ref[...])
    pre = _dot(g1, awh_ref[d]) + _dot(g2, awh_ref[d]) + _dot(g1, awl_ref[d])
    lg = _log_sigmoid(pre + ab_ref[d]) * (1.0 / G_TAU)
    ball = _sel_cols(mask01, lg)
    nb = L // C
    last = 0 if reverse else L - 1
    for h in range(G_HEADS):
        ks = slice(h * G_DK, (h + 1) * G_DK)
        b = ball[:, ks] * LOG2E
        qh = q_ref[:, ks].astype(F32)
        kh = k_ref[:, ks].astype(F32)
        vh = v_ref[:, h * G_DV:(h + 1) * G_DV]
        st = s_scr[d, h]
        inter = _dot_nt((qh * jnp.exp2(b)).astype(BF16), st.astype(BF16))
        blocks = []
        for i in range(nb):
            lo, hi = i * C, (i + 1) * C
            if reverse:
                bref = b[hi:hi + 1, :] if i < nb - 1 else jnp.zeros((1, G_DK), F32)
                v0, v1 = lo // LANES * LANES, L
            else:
                bref = b[lo - 1:lo, :] if i > 0 else jnp.zeros((1, G_DK), F32)
                v0, v1 = 0, min(L, -(-hi // LANES) * LANES)
            qi = (qh[lo:hi, :] * jnp.exp2(b[lo:hi, :] - bref)).astype(BF16)
            ki = (kh[v0:v1] * jnp.exp2(bref - b[v0:v1])).astype(BF16)
            pieces = [jnp.zeros((C, v0), F32)] if v0 else []
            pieces.append(_dot_nt(qi, ki))
            if v1 < L:
                pieces.append(jnp.zeros((C, L - v1), F32))
            blocks.append(jnp.concatenate(pieces, axis=1) if len(pieces) > 1 else pieces[0])
        att = jnp.where(mask, jnp.concatenate(blocks, axis=0), 0.0).astype(BF16)
        oh = inter + _dot(att, vh)
        bl = b[last:last + 1, :]
        kd = (kh * jnp.exp2(bl - b)).astype(BF16)
        s_scr[d, h] = st * jnp.exp2(bl) + _dot_tn(vh, kd)
        out_ref[:, h * G_DV:(h + 1) * G_DV] = oh.astype(BF16)


def _gla_kernel(L, C, qf, kf, vf, gaf, qb, kb, vb, gab, awh_ref, awl_ref, ab_ref, of, ob, s_scr):
    _zero_at_start(s_scr)
    _gla_dir(L, C, 0, qf, kf, vf, gaf, awh_ref, awl_ref, ab_ref, of, s_scr)
    _gla_dir(L, C, 1, qb, kb, vb, gab, awh_ref, awl_ref, ab_ref, ob, s_scr)


def _gla(q, k, v, ga, awh, awl, ab, bsz, t, ctx_len):
    L, C = G_CHUNK, G_SUB
    scratch = [pltpu.VMEM((2, G_HEADS, G_DV, G_DK), F32)]
    return _bidir_scan(functools.partial(_gla_kernel, L, C), L, [q, k, v, ga], [], [awh, awl, ab], [1024], scratch,
                       bsz, t, ctx_len, "gla_scan")


def _ssd_dir(L, d, x_ref, b_ref, c_ref, gc_ref, gr_ref, out_ref, s_scr):
    reverse = d == 1
    mask = _tri_mask(L, reverse)
    mask01 = mask.astype(BF16)
    gc = gc_ref[...]
    gr = gr_ref[0]
    bcol = _sel_cols(mask01, gc)
    brow = _sel_rows(gr, mask01)
    last = 0 if reverse else L - 1
    dt_o, la_o = S_HEADS * d, 2 * S_HEADS + S_HEADS * d
    er = lax.broadcasted_iota(I32, (LANES, S_HEADS * S_P), 0)
    ec = jnp.right_shift(lax.broadcasted_iota(I32, (LANES, S_HEADS * S_P), 1), 6)
    sel_la = (er == ec + la_o).astype(BF16)
    x = x_ref[...]
    glane = lax.broadcasted_iota(I32, (L, LANES), 1)
    bla = jnp.where((glane >= la_o) & (glane < la_o + S_HEADS), bcol, 0.0)
    dt_at_la = pltpu.roll(gc, 2 * S_HEADS, 1)
    ebx = _expand(jnp.exp(bla), sel_la)
    bl = bla[last:last + 1, :]
    wx = _expand(jnp.exp(bl - bla) * dt_at_la, sel_la)
    xw = (x.astype(F32) * wx).astype(BF16)
    bcol2, brow2 = bcol * LOG2E, brow * LOG2E
    lane = lax.broadcasted_iota(I32, (L, 2 * S_P), 1)
    for g in range(S_GROUPS):
        ns = slice(g * S_N, (g + 1) * S_N)
        bg, cg = b_ref[:, ns], c_ref[:, ns]
        cb = _dot_nt(cg, bg)
        gs = slice(g * S_HPG * S_P, (g + 1) * S_HPG * S_P)
        st = s_scr[d, g]
        inter = _dot(cg, st.astype(BF16)) * ebx[:, gs]
        parts = []
        for pr in range(S_HPG // 2):
            e0 = g * S_HPG + 2 * pr
            xp = x[:, e0 * S_P:(e0 + 2) * S_P]
            res = []
            for e in (e0, e0 + 1):
                bc, br = bcol2[:, la_o + e:la_o + e + 1], brow2[la_o + e:la_o + e + 1, :]
                dtr = gr[dt_o + e:dt_o + e + 1, :]
                att = (cb * dtr * jnp.exp2(jnp.where(mask, bc - br, -jnp.inf))).astype(BF16)
                res.append(_dot(att, xp))
            parts.append(jnp.where(lane < S_P, res[0], res[1]))
        s_scr[d, g] = st * ebx[last:last + 1, gs] + _dot_tn(bg, xw[:, gs])
        out_ref[:, gs] = (jnp.concatenate(parts, axis=1) + inter).astype(BF16)


def _ssd_kernel(L, xf, bf, cf, gcf, grf, xb, bb, cb, gcb, grb, of, ob, s_scr):
    _zero_at_start(s_scr)
    _ssd_dir(L, 0, xf, bf, cf, gcf, grf, of, s_scr)
    _ssd_dir(L, 1, xb, bb, cb, gcb, grb, ob, s_scr)


def _ssd(x, bm, cm, gcol, grow, bsz, t, ctx_len):
    L = S_CHUNK
    scratch = [pltpu.VMEM((2, S_GROUPS, S_N, S_HPG * S_P), F32)]
    return _bidir_scan(functools.partial(_ssd_kernel, L), L, [x, bm, cm, gcol], [grow], [], [1024], scratch,
                       bsz, t, ctx_len, "ssd_scan")


def _router_kernel(x_ref, modl_ref, g_ref, rw_ref, rb_ref, h_ref, idx_ref, gate_ref):
    h = _rms(x_ref[...], g_ref[...]) * (1.0 + modl_ref[0, 4:5, :]) + modl_ref[0, 3:4, :]
    h_ref[...] = h
    logits = jnp.dot(h, rw_ref[...], preferred_element_type=F32, precision=lax.Precision.HIGHEST) + rb_ref[...]
    lane = lax.broadcasted_iota(I32, logits.shape, 1)
    logits = jnp.where(lane < N_EXPERTS, logits, -jnp.inf)
    lanef = lane.astype(F32)
    m1 = jnp.max(logits, axis=1, keepdims=True)
    i1 = jnp.min(jnp.where(logits == m1, lanef, float(LANES)), axis=1, keepdims=True)
    rest = jnp.where(lanef == i1, -jnp.inf, logits)
    m2 = jnp.max(rest, axis=1, keepdims=True)
    i2 = jnp.min(jnp.where(rest == m2, lanef, float(LANES)), axis=1, keepdims=True)
    e = jnp.exp(m2 - m1)
    g1 = 1.0 / (1.0 + e)
    idx_ref[...] = jnp.where(lane == 0, i1, jnp.where(lane == 1, i2, 0.0)).astype(I32)
    gate_ref[...] = jnp.where(lane == 0, g1, jnp.where(lane == 1, e * g1, 0.0))


def _router(x2, modl, g, rw, rb, bsz, t, ctx_len):
    tm = ROW_TILE
    nt, nct = t // tm, ctx_len // tm
    nl = bsz * (t - ctx_len)
    nlt = nt - nct
    orow = lambda c: pl.BlockSpec((tm, c), lambda b, i: (b * nlt + i, 0))
    return pl.pallas_call(
        _router_kernel,
        grid=(bsz, nlt),
        in_specs=[pl.BlockSpec((tm, D_MODEL), lambda b, i: (b * nt + nct + i, 0)),
                  pl.BlockSpec((1, 8, D_MODEL), lambda b, i: (b, 0, 0)),
                  _resident((1, D_MODEL)), _resident(rw.shape), _resident(rb.shape)],
        out_specs=[orow(D_MODEL), orow(LANES), orow(LANES)],
        out_shape=[jax.ShapeDtypeStruct((nl, D_MODEL), F32), jax.ShapeDtypeStruct((nl, LANES), I32),
                   jax.ShapeDtypeStruct((nl, LANES), F32)],
        compiler_params=_cparams("arbitrary", "arbitrary"),
        name="moe_router",
    )(x2, modl, g, rw, rb)


def _gather_rows(table, idx):
    n_rows, d = idx.shape[0], table.shape[1]
    n_workers = SC_CORES * SC_SUBCORES
    per_worker = n_rows // n_workers
    chunk = SC_GATHER_ROWS
    assert per_worker * n_workers == n_rows and per_worker % chunk == 0
    mesh = plsc.VectorSubcoreMesh(core_axis_name="c", subcore_axis_name="s")

    @functools.partial(
        pl.kernel, mesh=mesh, out_type=jax.ShapeDtypeStruct((n_rows, d), table.dtype),
        scratch_types=[pltpu.VMEM((chunk,), I32), pltpu.VMEM((chunk, d), table.dtype), pltpu.SemaphoreType.DMA],
        name="sc_gather_rows")
    def gather(table_hbm, idx_hbm, out_hbm, idx_v, rows_v, sem):
        base = (lax.axis_index("s") * SC_CORES + lax.axis_index("c")) * per_worker

        @pl.loop(0, per_worker // chunk)
        def _(j):
            off = pl.multiple_of(base + j * chunk, 8)
            pltpu.sync_copy(idx_hbm.at[pl.ds(off, chunk)], idx_v)
            pltpu.async_copy(table_hbm.at[idx_v], rows_v, sem).wait()
            pltpu.sync_copy(rows_v, out_hbm.at[pl.ds(off, chunk)])

    return gather(table, idx)


def _moe_kernel(be_ref, nu_ref, x_ref, w1g_ref, w1u_ref, w2_ref, y_ref, acc_ref):
    i, j = pl.program_id(0), pl.program_id(1)

    @pl.when(i < nu_ref[0])
    def _():
        @pl.when(j == 0)
        def _():
            acc_ref[...] = jnp.zeros_like(acc_ref)

        x = x_ref[...].astype(BF16)
        a =(_silu(_dot(x, w1g_ref[0].astype(BF16))) * _dot(x, w1u_ref[0].astype(BF16))).astype(BF16)
        acc_ref[...] += _dot(a, w2_ref[0].astype(BF16))

        @pl.when(j == pl.num_programs(1) - 1)
        def _():
            y_ref[...] = acc_ref[...]


def _moe_experts(xs, w1, w2, block_e, n_used, n_blocks):
    bm, fc = MOE_ROWS, MOE_FF
    nff = FF_EXPERT // fc
    used = lambda i, nu: jnp.minimum(i, nu[0] - 1)
    ffi = lambda i, j, nu: jnp.where(i < nu[0], j, nff - 1)
    grid_spec = pltpu.PrefetchScalarGridSpec(
        num_scalar_prefetch=2,
        grid=(n_blocks, nff),
        in_specs=[pl.BlockSpec((bm, D_MODEL), lambda i, j, be, nu: (used(i, nu), 0)),
                  pl.BlockSpec((1, D_MODEL, fc), lambda i, j, be, nu: (be[used(i, nu)], 0, ffi(i, j, nu))),
                  pl.BlockSpec((1, D_MODEL, fc), lambda i, j, be, nu: (be[used(i, nu)], 0, nff + ffi(i, j, nu))),
                  pl.BlockSpec((1, fc, D_MODEL), lambda i, j, be, nu: (be[used(i, nu)], ffi(i, j, nu), 0))],
        out_specs=pl.BlockSpec((bm, D_MODEL), lambda i, j, be, nu: (used(i, nu), 0)),
        scratch_shapes=[pltpu.VMEM((bm, D_MODEL), F32)],
    )
    return pl.pallas_call(
        _moe_kernel,
        grid_spec=grid_spec,
        out_shape=jax.ShapeDtypeStruct((n_blocks * bm, D_MODEL), F32),
        compiler_params=_cparams("arbitrary", "arbitrary"),
        name="moe_experts",
    )(block_e, n_used, xs, w1, w1, w2)


def _combine_kernel(y0_ref, y1_ref, gate_ref, x_ref, modl_ref, g_ref, o_ref):
    gt = gate_ref[...]
    y = gt[:, 0:1] * y0_ref[...] + gt[:, 1:2] * y1_ref[...]
    o_ref[...] = x_ref[...] + modl_ref[0, 5:6, :] * _rms(y, g_ref[...])


def _combine(yk, gates, x2, modl, g, bsz, t, ctx_len):
    tm = ROW_TILE
    nt, nct = t // tm, ctx_len // tm
    nlt = nt - nct
    nl = bsz * (t - ctx_len)
    orow = lambda c: pl.BlockSpec((tm, c), lambda b, i: (b * nlt + i, 0))
    second = pl.BlockSpec((tm, D_MODEL), lambda b, i: (nl // tm + b * nlt + i, 0))
    return pl.pallas_call(
        _combine_kernel,
        grid=(bsz, nlt),
        in_specs=[orow(D_MODEL), second, orow(LANES),
                  pl.BlockSpec((tm, D_MODEL), lambda b, i: (b * nt + nct + i, 0)),
                  pl.BlockSpec((1, 8, D_MODEL), lambda b, i: (b, 0, 0)), _resident((1, D_MODEL))],
        out_specs=orow(D_MODEL),
        out_shape=jax.ShapeDtypeStruct((nl, D_MODEL), F32),
        compiler_params=_cparams("arbitrary", "arbitrary"),
        name="moe_combine",
    )(yk, yk, gates, x2, modl, g)


def _routing_tables(top_idx, n_tok):
    bm = MOE_ROWS
    flat_e = top_idx.reshape(-1)
    n_assign = flat_e.shape[0]
    onehot = (flat_e[:, None] == jnp.arange(N_EXPERTS, dtype=I32)[None, :]).astype(I32)
    incl = jnp.cumsum(onehot, axis=0)
    counts = incl[-1]
    rank = jnp.take_along_axis(incl, flat_e[:, None], axis=1)[:, 0] - 1
    padded = (counts + bm - 1) // bm * bm
    ends_p = jnp.cumsum(padded)
    pstarts = ends_p - padded
    slot = (pstarts[flat_e] + rank).astype(I32)
    n_blocks = n_assign // bm + N_EXPERTS
    flat_tok = jnp.arange(n_assign, dtype=I32) // TOP_K
    slot_tok = (jnp.arange(n_blocks * bm, dtype=I32) % n_tok).at[slot].set(flat_tok)
    block_e = jnp.minimum(jnp.searchsorted(ends_p, jnp.arange(n_blocks, dtype=I32) * bm, side='right'),
                          N_EXPERTS - 1).astype(I32)
    n_used = (ends_p[-1] // bm).astype(I32).reshape(1)
    return slot, slot_tok, block_e, n_used, n_blocks


def _row_layout(gcol, ncols, L):
    n = gcol.shape[0]
    return gcol[:, :ncols].reshape(n // L, L, ncols).transpose(0, 2, 1)


def kernel(x, c, ctx, c_ctx, mod_w, mod_b, norm_g, ab_in_w, m_conv_w, m_conv_b, m_gate_b, m_norm_w, l_conv_w,
           l_conv_b, l_wa, l_ba, l_wx, l_bx, l_lam, ab_out_w, ffn_w1, ffn_w2, cd_in_w, g_alpha_w, g_alpha_b,
           g_norm_w, s_conv_w, s_conv_b, s_dt_bias, s_A_log, s_D, s_norm_w, cd_out_w, router_w, router_b,
           moe_w1, moe_w2):
    bsz, seq, dm = x.shape
    ctx_len = ctx.shape[1]
    t = ctx_len + seq
    n = bsz * t
    assert ctx_len == ROW_TILE and seq % ROW_TILE == 0 and ROW_TILE % GRID_W == 0
    x2 = jnp.concatenate([ctx, x], axis=1).reshape(n, dm)

    c_all = jnp.concatenate([c, c_ctx[None, :], jnp.zeros((-(bsz + 1) % 8, dm), F32)], axis=0)
    mods = _modulation(c_all, mod_w, mod_b).reshape(mod_w.shape[0], c_all.shape[0], 6, dm)
    pad2 = jnp.zeros((bsz, 2, dm), F32)

    def layer_mods(layer):
        modl = jnp.concatenate([mods[layer, :bsz], pad2], axis=1)
        modc = jnp.concatenate([mods[layer, bsz], pad2[0]], axis=0)
        return modl, modc

    modl, modc = layer_mods(0)
    g = norm_g[0]
    w = ab_in_w[0]
    o_qk, o_v, o_o, o_gt, o_lx, o_lg = 0, 1024, 2048, 3072, 3088, 4112
    w_in = jnp.concatenate([w[:, o_qk:o_v], w[:, o_v:o_o], w[:, o_o:o_gt], w[:, o_lx:o_lg], w[:, o_lg:],
                            w[:, o_gt:o_lx], jnp.zeros((dm, LANES - 16), F32)], axis=1).astype(BF16)
    gb = jnp.concatenate([m_gate_b[0].reshape(1, 16), jnp.zeros((1, LANES - 16), F32)], axis=1)
    q, k, v, og, lx, glg, gates = _in_ab(x2, modl, modc, g[0:1], w_in, m_conv_w[0], m_conv_b[0][None],
                                         l_conv_w[0], l_conv_b[0][None], gb, bsz, t, ctx_len)
    hm_f, hm_b = _mlstm(q, k, v, gates, _row_layout(gates, 16, M_CHUNK), bsz, t, ctx_len)
    hl_f, hl_b = _lru(lx, l_wa[0].astype(BF16), l_wx[0].astype(BF16), l_ba[0][:, None], l_bx[0][:, None],
                      l_lam[0][:, None], bsz, t, ctx_len)
    x2 = _out_proj(_out_ab_kernel, "out_proj_ab", [hm_f, hm_b, og, hl_f, hl_b, glg], [m_norm_w[0][None]],
                   x2, modl, modc, g[1:2], ab_out_w[0].astype(BF16), bsz, t, ctx_len)
    x2 = _ffn(x2, modl, modc, g[2:3], g[3:4], ffn_w1[0].astype(BF16), ffn_w2[0].astype(BF16), bsz, t, ctx_len)

    modl, modc = layer_mods(1)
    g = norm_g[1]
    w = cd_in_w[0]
    o_gv, o_gr, o_ga, o_z, o_xbc, o_dt = 1024, 2048, 3072, 3104, 4128, 5664
    zpad = jnp.zeros((dm, LANES - 32), F32)
    w_in = jnp.concatenate([w[:, :o_gv], w[:, o_gv:o_gr], w[:, o_gr:o_ga], w[:, o_z:o_xbc], w[:, o_xbc:o_dt],
                            w[:, o_ga:o_z], zpad, w[:, o_dt:], w[:, o_dt:], jnp.zeros((dm, LANES - 64), F32)],
                           axis=1).astype(BF16)
    dtb = jnp.concatenate([s_dt_bias[0].reshape(1, 32), s_dt_bias[0].reshape(1, 32),
                           jnp.zeros((1, LANES - 64), F32)], axis=1)
    aneg = jnp.concatenate([jnp.zeros((1, 32), F32), -jnp.exp(s_A_log[0].reshape(1, 32)),
                            jnp.zeros((1, LANES - 64), F32)], axis=1)
    gq, gk, gv, grs, zs, sx, sb, sc, ga, dts = _in_cd(x2, modl, modc, g[0:1], w_in, s_conv_w[0], s_conv_b[0][None],
                                                      dtb, aneg, bsz, t, ctx_len)
    aw = jnp.zeros((2, LANES, G_HEADS * G_DK), F32)
    aw = aw.at[0, 0:G_RANK].set(g_alpha_w[0, 0]).at[1, G_RANK:2 * G_RANK].set(g_alpha_w[0, 1])
    awh = aw.astype(BF16)
    awl = (aw - awh.astype(F32)).astype(BF16)
    og_f, og_b = _gla(gq, gk, gv, ga, awh, awl, g_alpha_b[0][:, None], bsz, t, ctx_len)
    ys_f, ys_b = _ssd(sx, sb, sc, dts, _row_layout(dts, 4 * S_HEADS, S_CHUNK), bsz, t, ctx_len)
    dskip = jnp.repeat(s_D[0], S_P)[None]
    x2 = _out_proj(_out_cd_kernel, "out_proj_cd", [og_f, og_b, grs, ys_f, ys_b, sx, zs],
                   [g_norm_w[0][None], dskip, s_norm_w[0][None]],
                   x2, modl, modc, g[1:2], cd_out_w[0].astype(BF16), bsz, t, ctx_len)

    rw = jnp.concatenate([router_w[0], jnp.zeros((dm, LANES - N_EXPERTS), F32)], axis=1)
    rb = jnp.concatenate([router_b[0][None], jnp.zeros((1, LANES - N_EXPERTS), F32)], axis=1)
    h2, top_idx, top_gate = _router(x2, modl, g[2:3], rw, rb, bsz, t, ctx_len)
    n_tok = bsz * seq
    slot, slot_tok, block_e, n_used, n_blocks = _routing_tables(top_idx[:, :TOP_K], n_tok)
    xs = _gather_rows(h2, slot_tok)
    ys_e = _moe_experts(xs, moe_w1[0], moe_w2[0], block_e, n_used, n_blocks)
    slot2 = slot.reshape(n_tok, TOP_K)
    yk = _gather_rows(ys_e, jnp.concatenate([slot2[:, 0], slot2[:, 1]]))
    out = _combine(yk, top_gate, x2, modl, g[3:4], bsz, t, ctx_len)
    return out.reshape(bsz, seq, dm)
```

```python
import functools

import jax
import jax.numpy as jnp
from jax import lax
from jax.experimental import pallas as pl
from jax.experimental.pallas import tpu as pltpu
from jax.experimental.pallas import tpu_sc as plsc

F32, BF16, I32 = jnp.float32, jnp.bfloat16, jnp.int32

D_MODEL = 1024
GRID_W = 64
EPS = 1e-6
LOG2E = 1.4426950408889634
M_HEADS, M_DK, M_DV = 4, 128, 256
LRU_BLOCKS, LRU_BW, LRU_C = 8, 128, 8.0
G_HEADS, G_DK, G_DV, G_RANK, G_TAU = 4, 128, 256, 16, 16.0
S_HEADS, S_P, S_N, S_GROUPS, S_HPG = 16, 64, 128, 2, 8
FF_DENSE = 2816
N_EXPERTS, TOP_K, FF_EXPERT = 8, 2, 3584

LANES = 128
VMEM_LIMIT = 56 * 1024 * 1024
ROW_TILE = 256
BIG_ROW_TILE_MAX = 640
PROJ_COLS = 256
M_CHUNK = 256
G_CHUNK, G_SUB = 256, 16
S_CHUNK = 256
LRU_TILE = 256
FF_STEP = 256
MOE_ROWS = 1024
MOE_FF = 512
SC_CORES, SC_SUBCORES = 2, 16
SC_LANES = 16
SC_GATHER_ROWS = 64


def _cparams(*sem):
    return pltpu.CompilerParams(dimension_semantics=sem, vmem_limit_bytes=VMEM_LIMIT)


def _silu(x):
    return x * jax.nn.sigmoid(x)


def _softplus(x):
    return jnp.maximum(x, 0.0) + jnp.log1p(jnp.exp(-jnp.abs(x)))


def _log_sigmoid(x):
    return jnp.minimum(x, 0.0) - jnp.log1p(jnp.exp(-jnp.abs(x)))


def _dot(a, b):
    return jnp.dot(a, b, preferred_element_type=F32)


def _dot_nt(a, b):
    return lax.dot_general(a, b, (((1,), (1,)), ((), ())), preferred_element_type=F32)


def _dot_tn(a, b):
    return lax.dot_general(a, b, (((0,), (0,)), ((), ())), preferred_element_type=F32)


def _split3(f):
    f1 = f.astype(BF16)
    r = f - f1.astype(F32)
    f2 = r.astype(BF16)
    f3 = (r - f2.astype(F32)).astype(BF16)
    return f1, f2, f3


def _sel_cols(mask01, f):
    p1, p2, p3 = _split3(f)
    return _dot(mask01, p1) + _dot(mask01, p2) + _dot(mask01, p3)


def _sel_rows(f, mask01):
    p1, p2, p3 = _split3(f)
    return _dot_nt(p1, mask01) + _dot_nt(p2, mask01) + _dot_nt(p3, mask01)


def _expand(f, sel01):
    p1, p2, p3 = _split3(f)
    return _dot(p1, sel01) + _dot(p2, sel01) + _dot(p3, sel01)


def _rms(u, g):
    return u * lax.rsqrt(jnp.mean(u * u, axis=-1, keepdims=True) + EPS) * g


def _head_rms(u, g, n_heads):
    w = u.shape[-1] // n_heads
    return jnp.concatenate([_rms(u[:, h * w:(h + 1) * w], g[:, h * w:(h + 1) * w]) for h in range(n_heads)], axis=1)


def _row_ids(tile_idx, tm):
    return tile_idx * tm + lax.broadcasted_iota(I32, (tm, 1), 0)


def _mod(modl_ref, modc_ref, idx, is_ctx):
    return jnp.where(is_ctx, modc_ref[idx:idx + 1, :], modl_ref[0, idx:idx + 1, :])


def _adaln(x, g, modl_ref, modc_ref, shift_idx, is_ctx):
    shift = _mod(modl_ref, modc_ref, shift_idx, is_ctx)
    scale = _mod(modl_ref, modc_ref, shift_idx + 1, is_ctx)
    return _rms(x, g) * (1.0 + scale) + shift


def _neighbour_tiles(r, step, is_lat):
    n_tiles, per_seg = r.shape[0], GRID_W // 8
    zero = jnp.zeros((1,) + r.shape[1:], r.dtype)

    def across(src):
        return jnp.where(is_lat, 0.0, r[src:src + 1]) if 0 <= src < n_tiles else zero

    pieces = []
    for s0 in range(0, n_tiles, per_seg):
        if step > 0:
            pieces += [across(s0 - 1), r[s0:s0 + per_seg - 1]]
        else:
            pieces += [r[s0 + 1:s0 + per_seg], across(s0 + per_seg)]
    return jnp.concatenate(pieces, axis=0)


def _dwconv(y, cw, cb, is_lat):
    tm, n = y.shape
    y3 = y.reshape(tm // 8, 8, n)
    sub = lax.broadcasted_iota(I32, (1, 8, 1), 1)
    w = lambda j: cw[j:j + 1, :].reshape(1, 1, n)
    r1, r2, r7 = pltpu.roll(y3, 1, 1), pltpu.roll(y3, 2, 1), pltpu.roll(y3, 7, 1)
    ym1 = jnp.where(sub >= 1, r1, _neighbour_tiles(r1, 1, is_lat))
    ym2 = jnp.where(sub >= 2, r2, _neighbour_tiles(r2, 1, is_lat))
    yp1 = jnp.where(sub <= 6, r7, _neighbour_tiles(r7, -1, is_lat))
    out = cb.reshape(1, 1, n) + w(0) * ym2 + w(1) * ym1 + w(2) * y3 + w(3) * yp1
    return out.reshape(tm, n)


def _resident(shape):
    nd = len(shape)
    return pl.BlockSpec(shape, lambda *_: (0,) * nd, pipeline_mode=pl.Buffered(1))


def _big_row_tile(t):
    return max(tm for tm in range(8, BIG_ROW_TILE_MAX + 1, 8) if t % tm == 0)


def _mod_kernel(c_ref, w_ref, b_ref, o_ref):
    a = _silu(c_ref[...])
    o_ref[0] = jnp.dot(a, w_ref[0], preferred_element_type=F32, precision=lax.Precision.HIGHEST) + b_ref[0]


def _modulation(c_all, mod_w, mod_b):
    depth, d, n6 = mod_w.shape
    rows = c_all.shape[0]
    tn = 1024
    return pl.pallas_call(
        _mod_kernel,
        grid=(depth, n6 // tn),
        in_specs=[pl.BlockSpec((rows, d), lambda l, j: (0, 0)),
                  pl.BlockSpec((1, d, tn), lambda l, j: (l, 0, j)),
                  pl.BlockSpec((1, 1, tn), lambda l, j: (l, 0, j))],
        out_specs=pl.BlockSpec((1, rows, tn), lambda l, j: (l, 0, j)),
        out_shape=jax.ShapeDtypeStruct((depth, rows, n6), F32),
        compiler_params=_cparams("arbitrary", "arbitrary"),
        name="modulation",
    )(c_all, mod_w, mod_b.reshape(depth, 1, n6))


def _project(h_ref, w_ref, w_col, out_ref, epilogue):
    for c in range(0, out_ref.shape[1], PROJ_COLS):
        y = _dot(h_ref[...], w_ref[:, w_col + c:w_col + c + PROJ_COLS])
        out_ref[:, c:c + PROJ_COLS] = epilogue(y, c).astype(out_ref.dtype)


def _in_ab_kernel(tm, ctx_len, x_ref, modl_ref, modc_ref, g_ref, w_ref, mcw_ref, mcb_ref, lcw_ref, lcb_ref,
                  gb_ref, q_ref, k_ref, v_ref, og_ref, lx_ref, glg_ref, gates_ref, h_scr):
    r = _row_ids(pl.program_id(1), tm)
    is_ctx = r < ctx_len
    h_scr[...] = _adaln(x_ref[...], g_ref[...], modl_ref, modc_ref, 0, is_ctx).astype(BF16)
    is_lat = pl.program_id(1) * tm >= ctx_len

    def conv(cw_ref, cb_ref):
        return lambda y, c: _dwconv(y, cw_ref[:, c:c + PROJ_COLS], cb_ref[:, c:c + PROJ_COLS], is_lat)

    qc, kc, lc = conv(mcw_ref, mcb_ref), conv(mcw_ref, mcb_ref), conv(lcw_ref, lcb_ref)
    _project(h_scr, w_ref, 0, q_ref, lambda y, c: _silu(qc(y, c)))
    _project(h_scr, w_ref, 512, k_ref, lambda y, c: _silu(kc(y, 512 + c)) * (M_DK ** -0.5))
    _project(h_scr, w_ref, 1024, v_ref, lambda y, c: y)
    _project(h_scr, w_ref, 2048, og_ref, lambda y, c: jax.nn.sigmoid(y))
    _project(h_scr, w_ref, 3072, lx_ref, lc)
    _project(h_scr, w_ref, 4096, glg_ref, lambda y, c: jax.nn.gelu(y))
    gt = _dot(h_scr[...], w_ref[:, 5120:5248]) + gb_ref[...]
    lane = lax.broadcasted_iota(I32, gt.shape, 1)
    gates_ref[...] = jnp.where(lane >= 2 * M_HEADS, _log_sigmoid(gt), gt)


def _in_ab(x2, modl, modc, g, w, mcw, mcb, lcw, lcb, gb, bsz, t, ctx_len):
    tm = ROW_TILE
    nt = t // tm
    n = bsz * t
    row = lambda c: pl.BlockSpec((tm, c), lambda b, i: (b * nt + i, 0))
    outs = [(512, BF16), (512, BF16), (1024, BF16), (1024, BF16), (1024, F32), (1024, BF16), (LANES, F32)]
    return pl.pallas_call(
        functools.partial(_in_ab_kernel, tm, ctx_len),
        grid=(bsz, nt),
        in_specs=[row(D_MODEL),
                  pl.BlockSpec((1, 8, D_MODEL), lambda b, i: (b, 0, 0)),
                  _resident((8, D_MODEL)), _resident((1, D_MODEL)), _resident(w.shape),
                  _resident(mcw.shape), _resident(mcb.shape), _resident(lcw.shape), _resident(lcb.shape),
                  _resident(gb.shape)],
        out_specs=[row(c) for c, _ in outs],
        out_shape=[jax.ShapeDtypeStruct((n, c), dt) for c, dt in outs],
        scratch_shapes=[pltpu.VMEM((tm, D_MODEL), BF16)],
        compiler_params=_cparams("arbitrary", "arbitrary"),
        name="in_proj_ab",
    )(x2, modl, modc, g, w, mcw, mcb, lcw, lcb, gb)


def _reverse_order(n_ctx_chunks, n_chunks):
    return lambda c: jnp.where(c < n_ctx_chunks, n_ctx_chunks - 1 - c, n_chunks - 1 - (c - n_ctx_chunks))


def _tri_mask(L, reverse):
    row = lax.broadcasted_iota(I32, (L, L), 0)
    col = lax.broadcasted_iota(I32, (L, L), 1)
    return (col >= row) if reverse else (col <= row)


def _bidir_scan(kern, L, row_ins, row3_ins, res_ins, out_cols, scratch, bsz, t, ctx_len, name):
    nc = t // L
    rev = _reverse_order(ctx_len // L, nc)
    fwd = lambda i: i
    n = bsz * t

    def row(c, order):
        return pl.BlockSpec((L, c), lambda b, i: (b * nc + order(i), 0))

    def row3(shape, order):
        return pl.BlockSpec((1,) + tuple(shape[1:]), lambda b, i: (b * nc + order(i), 0, 0))

    in_specs, args = [], []
    for order in (fwd, rev):
        in_specs += [row(a.shape[1], order) for a in row_ins] + [row3(a.shape, order) for a in row3_ins]
        args += list(row_ins) + list(row3_ins)
    in_specs += [_resident(a.shape) for a in res_ins]
    args += list(res_ins)
    return pl.pallas_call(
        kern,
        grid=(bsz, nc),
        in_specs=in_specs,
        out_specs=[row(c, fwd) for c in out_cols] + [row(c, rev) for c in out_cols],
        out_shape=[jax.ShapeDtypeStruct((n, c), BF16) for c in out_cols] * 2,
        scratch_shapes=scratch,
        compiler_params=_cparams("arbitrary", "arbitrary"),
        name=name,
    )(*args)


def _zero_at_start(*scratch):
    @pl.when(pl.program_id(1) == 0)
    def _():
        for s in scratch:
            s[...] = jnp.zeros_like(s)


def _mlstm_kernel(L, qf, kf, vf, gcf, grf, qb, kb, vb, gcb, grb, of, ob, c_scr, m_scr):
    _zero_at_start(c_scr, m_scr)
    ones_col = (lax.broadcasted_iota(I32, (L, LANES), 1) == 0).astype(BF16)
    chains = []
    for d, (q_ref, k_ref, v_ref, gc_ref, gr_ref, out_ref) in enumerate(((qf, kf, vf, gcf, grf, of),
                                                                         (qb, kb, vb, gcb, grb, ob))):
        reverse = d == 1
        mask = _tri_mask(L, reverse)
        mask01 = mask.astype(BF16)
        gc = gc_ref[...]
        gr = gr_ref[0]
        bcol = _sel_cols(mask01, gc)
        brow = _sel_rows(gr, mask01)
        last = 0 if reverse else L - 1
        for h in range(M_HEADS):
            fo, io = 2 * M_HEADS + M_HEADS * d + h, M_HEADS * d + h
            bc, br = bcol[:, fo:fo + 1], brow[fo:fo + 1, :]
            ic, ir = gc[:, io:io + 1], gr[io:io + 1, :]
            m = m_scr[d, h:h + 1, 0:1]
            g = bc + m
            dlog = jnp.where(mask, bc - br + ir, -jnp.inf)
            mt = jnp.maximum(g, jnp.max(dlog, axis=1, keepdims=True))
            bl = bc[last:last + 1, :]
            a_s = bl - bc + ic
            m_new = jnp.maximum(bl + m, jnp.max(a_s, axis=0, keepdims=True))
            chains.append(dict(
                d=d, h=h, out_ref=out_ref, mt=mt, w_inter=jnp.exp(g - mt), p=jnp.exp(dlog - mt),
                m_new=m_new, decay=jnp.exp(bl + m - m_new), ws=jnp.exp(a_s - m_new),
                qh=q_ref[:, h * M_DK:(h + 1) * M_DK], kh=k_ref[:, h * M_DK:(h + 1) * M_DK],
                vx=jnp.concatenate([v_ref[:, h * M_DV:(h + 1) * M_DV], ones_col], axis=1)))
    for ch in chains:
        ch["sc"] = (_dot_nt(ch["qh"], ch["kh"]) * ch["p"]).astype(BF16)
    for ch in chains:
        cx = c_scr[ch["d"], ch["h"]]
        numx = ch["w_inter"] * _dot(ch["qh"], cx.astype(BF16)) + _dot(ch["sc"], ch["vx"])
        den = numx[:, M_DV:M_DV + 1]
        hh = numx[:, 0:M_DV] * (1.0 / jnp.maximum(jnp.abs(den), jnp.exp(-ch["mt"])))
        ch["out_ref"][:, ch["h"] * M_DV:(ch["h"] + 1) * M_DV] = hh.astype(BF16)
    for ch in chains:
        d, h = ch["d"], ch["h"]
        kw = (ch["kh"].astype(F32) * ch["ws"]).astype(BF16)
        c_scr[d, h] = ch["decay"] * c_scr[d, h] + _dot_tn(kw, ch["vx"])
        m_scr[d, h:h + 1, 0:1] = ch["m_new"]


def _mlstm(q, k, v, gcol, grow, bsz, t, ctx_len):
    L = M_CHUNK
    scratch = [pltpu.VMEM((2, M_HEADS, M_DK, M_DV + LANES), F32), pltpu.VMEM((2, 8, LANES), F32)]
    return _bidir_scan(functools.partial(_mlstm_kernel, L), L, [q, k, v, gcol], [grow], [], [1024], scratch,
                       bsz, t, ctx_len, "mlstm_scan")


def _lru_gates(L, d, u_ref, wa_ref, wx_ref, ba_ref, bx_ref, lam_ref, a_scr, b_scr):
    reverse = d == 1
    u = u_ref[...]
    ub = u.astype(BF16)
    sp = _softplus(-lam_ref[d])
    for n in range(LRU_BLOCKS):
        sl = slice(n * LRU_BW, (n + 1) * LRU_BW)
        rg = jax.nn.sigmoid(_dot(ub[:, sl], wa_ref[d, n]) + ba_ref[d, :, sl])
        ig = jax.nn.sigmoid(_dot(ub[:, sl], wx_ref[d, n]) + bx_ref[d, :, sl])
        log_a = -LRU_C * rg * sp[:, sl]
        a = jnp.exp(log_a)
        a_scr[d, :, sl] = a
        b_scr[d, :, sl] = jnp.sqrt(-jnp.tanh(log_a) * (a * a + 1.0)) * (ig * u[:, sl])
    a = a_scr[d].reshape(L // 8, 8, LRU_BLOCKS * LRU_BW)
    b = b_scr[d].reshape(L // 8, 8, LRU_BLOCKS * LRU_BW)
    sub = lax.broadcasted_iota(I32, (1, 8, 1), 1)
    for s in (1, 2, 4):
        if reverse:
            a_sh, b_sh, keep = pltpu.roll(a, 8 - s, 1), pltpu.roll(b, 8 - s, 1), sub < 8 - s
        else:
            a_sh, b_sh, keep = pltpu.roll(a, s, 1), pltpu.roll(b, s, 1), sub >= s
        b = jnp.where(keep, a * b_sh + b, b)
        a = jnp.where(keep, a * a_sh, a)
    a_scr[d] = a.reshape(L, LRU_BLOCKS * LRU_BW)
    b_scr[d] = b.reshape(L, LRU_BLOCKS * LRU_BW)


def _lru_kernel(L, uf, ub, wa_ref, wx_ref, ba_ref, bx_ref, lam_ref, of, ob, a_scr, b_scr, h_scr):
    _zero_at_start(h_scr)
    _lru_gates(L, 0, uf, wa_ref, wx_ref, ba_ref, bx_ref, lam_ref, a_scr, b_scr)
    _lru_gates(L, 1, ub, wa_ref, wx_ref, ba_ref, bx_ref, lam_ref, a_scr, b_scr)
    ng = L // 8

    def body(j, carry):
        hf, hb = carry
        rf = pl.ds(pl.multiple_of(j * 8, 8), 8)
        rb = pl.ds(pl.multiple_of((ng - 1 - j) * 8, 8), 8)
        gf = b_scr[0, rf, :] + a_scr[0, rf, :] * hf
        gb = b_scr[1, rb, :] + a_scr[1, rb, :] * hb
        b_scr[0, rf, :] = gf
        b_scr[1, rb, :] = gb
        return gf[7:8, :], gb[0:1, :]

    hf, hb = lax.fori_loop(0, ng, body, (h_scr[0:1, :], h_scr[1:2, :]))
    h_scr[0:1, :] = hf
    h_scr[1:2, :] = hb
    of[...] = b_scr[0].astype(BF16)
    ob[...] = b_scr[1].astype(BF16)


def _lru(u, wa, wx, ba, bx, lam, bsz, t, ctx_len):
    L = LRU_TILE
    scratch = [pltpu.VMEM((2, L, 1024), F32), pltpu.VMEM((2, L, 1024), F32), pltpu.VMEM((8, 1024), F32)]
    return _bidir_scan(functools.partial(_lru_kernel, L), L, [u], [], [wa, wx, ba, bx, lam], [1024], scratch,
                       bsz, t, ctx_len, "rglru_scan")


def _out_tail(tm, ctx_len, a1, a2, x_ref, modl_ref, modc_ref, g_ref, w_ref, xo_ref):
    is_ctx = _row_ids(pl.program_id(1), tm) < ctx_len
    half = w_ref.shape[0] // 2
    y = _dot(a1.astype(BF16), w_ref[0:half, :]) + _dot(a2.astype(BF16), w_ref[half:, :])
    xo_ref[...] = x_ref[...] + _mod(modl_ref, modc_ref, 2, is_ctx) * _rms(y, g_ref[...])


def _out_ab_kernel(tm, ctx_len, hf, hb, og, lf, lb, glg, nw, x_ref, modl_ref, modc_ref, g_ref, w_ref, xo_ref):
    hm = _head_rms(hf[...].astype(F32) + hb[...].astype(F32), nw[...], M_HEADS) * og[...].astype(F32)
    hl = (lf[...].astype(F32) + lb[...].astype(F32)) * glg[...].astype(F32)
    _out_tail(tm, ctx_len, hm, hl, x_ref, modl_ref, modc_ref, g_ref, w_ref, xo_ref)


def _out_cd_kernel(tm, ctx_len, gf, gb, grs, yf, yb, sx, zs, gnw, dsk, snw, x_ref, modl_ref, modc_ref, g_ref,
                   w_ref, xo_ref):
    og = _head_rms(gf[...].astype(F32) + gb[...].astype(F32), gnw[...], G_HEADS) * grs[...].astype(F32)
    ys = yf[...].astype(F32) + yb[...].astype(F32) + dsk[...] * sx[...].astype(F32)
    ys = _rms(ys * zs[...].astype(F32), snw[...])
    _out_tail(tm, ctx_len, og, ys, x_ref, modl_ref, modc_ref, g_ref, w_ref, xo_ref)


def _out_proj(kern, name, acts, vecs, x2, modl, modc, g, w, bsz, t, ctx_len):
    tm = _big_row_tile(t)
    nt = t // tm
    row = lambda c: pl.BlockSpec((tm, c), lambda b, i: (b * nt + i, 0))
    return pl.pallas_call(
        functools.partial(kern, tm, ctx_len),
        grid=(bsz, nt),
        in_specs=[row(1024)] * len(acts) + [_resident((1, 1024))] * len(vecs)
        + [row(D_MODEL), pl.BlockSpec((1, 8, D_MODEL), lambda b, i: (b, 0, 0)),
           _resident((8, D_MODEL)), _resident((1, D_MODEL)), _resident(w.shape)],
        out_specs=row(D_MODEL),
        out_shape=jax.ShapeDtypeStruct(x2.shape, F32),
        compiler_params=_cparams("arbitrary", "arbitrary"),
        name=name,
    )(*acts, *vecs, x2, modl, modc, g, w)


def _ffn_kernel(tm, ctx_len, x_ref, modl_ref, modc_ref, g2_ref, g3_ref, w1_ref, w2_ref, xo_ref):
    is_ctx = _row_ids(pl.program_id(1), tm) < ctx_len
    x = x_ref[...]
    h = _adaln(x, g2_ref[...], modl_ref, modc_ref, 3, is_ctx).astype(BF16)
    ff = w2_ref.shape[0]
    acc = jnp.zeros((tm, D_MODEL), F32)
    for j in range(ff // FF_STEP):
        lo = j * FF_STEP
        gj = _dot(h, w1_ref[:, lo:lo + FF_STEP])
        uj = _dot(h, w1_ref[:, ff + lo:ff + lo + FF_STEP])
        acc = acc + _dot((_silu(gj) * uj).astype(BF16), w2_ref[lo:lo + FF_STEP, :])
    xo_ref[...] = x + _mod(modl_ref, modc_ref, 5, is_ctx) * _rms(acc, g3_ref[...])


def _ffn(x2, modl, modc, g2, g3, w1, w2, bsz, t, ctx_len):
    tm = _big_row_tile(t)
    nt = t // tm
    row = lambda c: pl.BlockSpec((tm, c), lambda b, i: (b * nt + i, 0))
    return pl.pallas_call(
        functools.partial(_ffn_kernel, tm, ctx_len),
        grid=(bsz, nt),
        in_specs=[row(D_MODEL), pl.BlockSpec((1, 8, D_MODEL), lambda b, i: (b, 0, 0)), _resident((8, D_MODEL)),
                  _resident((1, D_MODEL)), _resident((1, D_MODEL)), _resident(w1.shape), _resident(w2.shape)],
        out_specs=row(D_MODEL),
        out_shape=jax.ShapeDtypeStruct(x2.shape, F32),
        compiler_params=_cparams("arbitrary", "arbitrary"),
        name="dense_swiglu",
    )(x2, modl, modc, g2, g3, w1, w2)


def _in_cd_kernel(tm, ctx_len, x_ref, modl_ref, modc_ref, g_ref, w_ref, cw_ref, cb_ref, dtb_ref, aneg_ref,
                  gq_ref, gk_ref, gv_ref, gr_ref, z_ref, sx_ref, sb_ref, sc_ref, ga_ref, dts_ref, h_scr):
    r = _row_ids(pl.program_id(1), tm)
    is_ctx = r < ctx_len
    h_scr[...] = _adaln(x_ref[...], g_ref[...], modl_ref, modc_ref, 0, is_ctx).astype(BF16)
    is_lat = pl.program_id(1) * tm >= ctx_len

    def conv_silu(col0):
        return lambda y, c: _silu(_dwconv(y, cw_ref[:, col0 + c:col0 + c + PROJ_COLS],
                                          cb_ref[:, col0 + c:col0 + c + PROJ_COLS], is_lat))

    _project(h_scr, w_ref, 0, gq_ref, lambda y, c: y * (G_DK ** -0.5))
    _project(h_scr, w_ref, 512, gk_ref, lambda y, c: y)
    _project(h_scr, w_ref, 1024, gv_ref, lambda y, c: y)
    _project(h_scr, w_ref, 2048, gr_ref, lambda y, c: _silu(y))
    _project(h_scr, w_ref, 3072, z_ref, lambda y, c: _silu(y))
    _project(h_scr, w_ref, 4096, sx_ref, conv_silu(0))
    _project(h_scr, w_ref, 5120, sb_ref, conv_silu(1024))
    _project(h_scr, w_ref, 5376, sc_ref, conv_silu(1280))
    ga_ref[...] = _dot(h_scr[...], w_ref[:, 5632:5760])
    dt = _softplus(_dot(h_scr[...], w_ref[:, 5760:5888]) + dtb_ref[...])
    lane = lax.broadcasted_iota(I32, dt.shape, 1)
    dts_ref[...] = jnp.where(lane < 2 * S_HEADS, dt, dt * aneg_ref[...])


def _in_cd(x2, modl, modc, g, w, cw, cb, dtb, aneg, bsz, t, ctx_len):
    tm = ROW_TILE
    nt = t // tm
    n = bsz * t
    row = lambda c: pl.BlockSpec((tm, c), lambda b, i: (b * nt + i, 0))
    outs = [(512, BF16), (512, BF16), (1024, BF16), (1024, BF16), (1024, BF16), (1024, BF16), (256, BF16),
            (256, BF16), (LANES, F32), (LANES, F32)]
    return pl.pallas_call(
        functools.partial(_in_cd_kernel, tm, ctx_len),
        grid=(bsz, nt),
        in_specs=[row(D_MODEL), pl.BlockSpec((1, 8, D_MODEL), lambda b, i: (b, 0, 0)), _resident((8, D_MODEL)),
                  _resident((1, D_MODEL)), _resident(w.shape), _resident(cw.shape), _resident(cb.shape),
                  _resident(dtb.shape), _resident(aneg.shape)],
        out_specs=[row(c) for c, _ in outs],
        out_shape=[jax.ShapeDtypeStruct((n, c), dt) for c, dt in outs],
        scratch_shapes=[pltpu.VMEM((tm, D_MODEL), BF16)],
        compiler_params=_cparams("arbitrary", "arbitrary"),
        name="in_proj_cd",
    )(x2, modl, modc, g, w, cw, cb, dtb, aneg)


def _gla_dir(L, C, d, q_ref, k_ref, v_ref, ga_ref, awh_ref, awl_ref, ab_ref, out_ref, s_scr):
    reverse = d == 1
    mask = _tri_mask(L, reverse)
    mask01 = mask.astype(BF16)
    g1, g2, _ = _split3(ga_ref[...])
    pre = _dot(g1, awh_ref[d]) + _dot(g2, awh_ref[d]) + _dot(g1, awl_ref[d])
    lg = _log_sigmoid(pre + ab_ref[d]) * (1.0 / G_TAU)
    ball = _sel_cols(mask01, lg)
    nb = L // C
    last = 0 if reverse else L - 1
    for h in range(G_HEADS):
        ks = slice(h * G_DK, (h + 1) * G_DK)
        b = ball[:, ks] * LOG2E
        qh = q_ref[:, ks].astype(F32)
        kh = k_ref[:, ks].astype(F32)
        vh = v_ref[:, h * G_DV:(h + 1) * G_DV]
        st = s_scr[d, h]
        inter = _dot_nt((qh * jnp.exp2(b)).astype(BF16), st.astype(BF16))
        blocks = []
        for i in range(nb):
            lo, hi = i * C, (i + 1) * C
            if reverse:
                bref = b[hi:hi + 1, :] if i < nb - 1 else jnp.zeros((1, G_DK), F32)
                v0, v1 = lo // LANES * LANES, L
            else:
                bref = b[lo - 1:lo, :] if i > 0 else jnp.zeros((1, G_DK), F32)
                v0, v1 = 0, min(L, -(-hi // LANES) * LANES)
            qi = (qh[lo:hi, :] * jnp.exp2(b[lo:hi, :] - bref)).astype(BF16)
            ki = (kh[v0:v1] * jnp.exp2(bref - b[v0:v1])).astype(BF16)
            pieces = [jnp.zeros((C, v0), F32)] if v0 else []
            pieces.append(_dot_nt(qi, ki))
            if v1 < L:
                pieces.append(jnp.zeros((C, L - v1), F32))
            blocks.append(jnp.concatenate(pieces, axis=1) if len(pieces) > 1 else pieces[0])
        att = jnp.where(mask, jnp.concatenate(blocks, axis=0), 0.0).astype(BF16)
        oh = inter + _dot(att, vh)
        bl = b[last:last + 1, :]
        kd = (kh * jnp.exp2(bl - b)).astype(BF16)
        s_scr[d, h] = st * jnp.exp2(bl) + _dot_tn(vh, kd)
        out_ref[:, h * G_DV:(h + 1) * G_DV] = oh.astype(BF16)


def _gla_kernel(L, C, qf, kf, vf, gaf, qb, kb, vb, gab, awh_ref, awl_ref, ab_ref, of, ob, s_scr):
    _zero_at_start(s_scr)
    _gla_dir(L, C, 0, qf, kf, vf, gaf, awh_ref, awl_ref, ab_ref, of, s_scr)
    _gla_dir(L, C, 1, qb, kb, vb, gab, awh_ref, awl_ref, ab_ref, ob, s_scr)


def _gla(q, k, v, ga, awh, awl, ab, bsz, t, ctx_len):
    L, C = G_CHUNK, G_SUB
    scratch = [pltpu.VMEM((2, G_HEADS, G_DV, G_DK), F32)]
    return _bidir_scan(functools.partial(_gla_kernel, L, C), L, [q, k, v, ga], [], [awh, awl, ab], [1024], scratch,
                       bsz, t, ctx_len, "gla_scan")


def _ssd_dir(L, d, x_ref, b_ref, c_ref, gc_ref, gr_ref, out_ref, s_scr):
    reverse = d == 1
    mask = _tri_mask(L, reverse)
    mask01 = mask.astype(BF16)
    gc = gc_ref[...]
    gr = gr_ref[0]
    bcol = _sel_cols(mask01, gc)
    brow = _sel_rows(gr, mask01)
    last = 0 if reverse else L - 1
    dt_o, la_o = S_HEADS * d, 2 * S_HEADS + S_HEADS * d
    er = lax.broadcasted_iota(I32, (LANES, S_HEADS * S_P), 0)
    ec = jnp.right_shift(lax.broadcasted_iota(I32, (LANES, S_HEADS * S_P), 1), 6)
    sel_la = (er == ec + la_o).astype(BF16)
    x = x_ref[...]
    glane = lax.broadcasted_iota(I32, (L, LANES), 1)
    bla = jnp.where((glane >= la_o) & (glane < la_o + S_HEADS), bcol, 0.0)
    dt_at_la = pltpu.roll(gc, 2 * S_HEADS, 1)
    ebx = _expand(jnp.exp(bla), sel_la)
    bl = bla[last:last + 1, :]
    wx = _expand(jnp.exp(bl - bla) * dt_at_la, sel_la)
    xw = (x.astype(F32) * wx).astype(BF16)
    bcol2, brow2 = bcol * LOG2E, brow * LOG2E
    lane = lax.broadcasted_iota(I32, (L, 2 * S_P), 1)
    for g in range(S_GROUPS):
        ns = slice(g * S_N, (g + 1) * S_N)
        bg, cg = b_ref[:, ns], c_ref[:, ns]
        cb = _dot_nt(cg, bg)
        gs = slice(g * S_HPG * S_P, (g + 1) * S_HPG * S_P)
        st = s_scr[d, g]
        inter = _dot(cg, st.astype(BF16)) * ebx[:, gs]
        parts = []
        for pr in range(S_HPG // 2):
            e0 = g * S_HPG + 2 * pr
            xp = x[:, e0 * S_P:(e0 + 2) * S_P]
            res = []
            for e in (e0, e0 + 1):
                bc, br = bcol2[:, la_o + e:la_o + e + 1], brow2[la_o + e:la_o + e + 1, :]
                dtr = gr[dt_o + e:dt_o + e + 1, :]
                att = (cb * dtr * jnp.exp2(jnp.where(mask, bc - br, -jnp.inf))).astype(BF16)
                res.append(_dot(att, xp))
            parts.append(jnp.where(lane < S_P, res[0], res[1]))
        s_scr[d, g] = st * ebx[last:last + 1, gs] + _dot_tn(bg, xw[:, gs])
        out_ref[:, gs] = (jnp.concatenate(parts, axis=1) + inter).astype(BF16)


def _ssd_kernel(L, xf, bf, cf, gcf, grf, xb, bb, cb, gcb, grb, of, ob, s_scr):
    _zero_at_start(s_scr)
    _ssd_dir(L, 0, xf, bf, cf, gcf, grf, of, s_scr)
    _ssd_dir(L, 1, xb, bb, cb, gcb, grb, ob, s_scr)


def _ssd(x, bm, cm, gcol, grow, bsz, t, ctx_len):
    L = S_CHUNK
    scratch = [pltpu.VMEM((2, S_GROUPS, S_N, S_HPG * S_P), F32)]
    return _bidir_scan(functools.partial(_ssd_kernel, L), L, [x, bm, cm, gcol], [grow], [], [1024], scratch,
                       bsz, t, ctx_len, "ssd_scan")


def _router_kernel(x_ref, modl_ref, g_ref, rw_ref, rb_ref, h_ref, idx_ref, gate_ref, cnt_ref, base_scr):
    @pl.when((pl.program_id(0) == 0) & (pl.program_id(1) == 0))
    def _():
        base_scr[...] = jnp.zeros_like(base_scr)

    h = _rms(x_ref[...], g_ref[...]) * (1.0 + modl_ref[0, 4:5, :]) + modl_ref[0, 3:4, :]
    h_ref[...] = h
    logits = jnp.dot(h, rw_ref[...], preferred_element_type=F32, precision=lax.Precision.HIGHEST) + rb_ref[...]
    lane = lax.broadcasted_iota(I32, logits.shape, 1)
    logits = jnp.where(lane < N_EXPERTS, logits, -jnp.inf)
    lanef = lane.astype(F32)
    m1 = jnp.max(logits, axis=1, keepdims=True)
    i1 = jnp.min(jnp.where(logits == m1, lanef, float(LANES)), axis=1, keepdims=True)
    rest = jnp.where(lanef == i1, -jnp.inf, logits)
    m2 = jnp.max(rest, axis=1, keepdims=True)
    i2 = jnp.min(jnp.where(rest == m2, lanef, float(LANES)), axis=1, keepdims=True)
    e = jnp.exp(m2 - m1)
    g1 = 1.0 / (1.0 + e)
    gate_ref[...] = jnp.where(lane == 0, g1, jnp.where(lane == 1, e * g1, 0.0))
    tm = logits.shape[0]
    oh1, oh2 = (lanef == i1).astype(F32), (lanef == i2).astype(F32)
    earlier = (lax.broadcasted_iota(I32, (tm, tm), 1) < lax.broadcasted_iota(I32, (tm, tm), 0)).astype(BF16)
    seen = base_scr[0:1, :] + _dot(earlier, (oh1 + oh2).astype(BF16))
    r1 = jnp.sum(oh1 * seen, axis=1, keepdims=True)
    r2 = jnp.sum(oh2 * seen, axis=1, keepdims=True)
    base_scr[0:1, :] = base_scr[0:1, :] + jnp.sum(oh1 + oh2, axis=0, keepdims=True)
    cnt_ref[...] = base_scr[...]
    idx_ref[...] = jnp.where(lane == 0, i1, jnp.where(lane == 1, i2, jnp.where(lane == 2, r1, jnp.where(
        lane == 3, r2, 0.0)))).astype(I32)


def _router(x2, modl, g, rw, rb, bsz, t, ctx_len):
    tm = ROW_TILE
    nt, nct = t // tm, ctx_len // tm
    nl = bsz * (t - ctx_len)
    nlt = nt - nct
    orow = lambda c: pl.BlockSpec((tm, c), lambda b, i: (b * nlt + i, 0))
    return pl.pallas_call(
        _router_kernel,
        grid=(bsz, nlt),
        in_specs=[pl.BlockSpec((tm, D_MODEL), lambda b, i: (b * nt + nct + i, 0)),
                  pl.BlockSpec((1, 8, D_MODEL), lambda b, i: (b, 0, 0)),
                  _resident((1, D_MODEL)), _resident(rw.shape), _resident(rb.shape)],
        out_specs=[orow(D_MODEL), orow(LANES), orow(LANES), pl.BlockSpec((8, LANES), lambda b, i: (0, 0))],
        out_shape=[jax.ShapeDtypeStruct((nl, D_MODEL), F32), jax.ShapeDtypeStruct((nl, LANES), I32),
                   jax.ShapeDtypeStruct((nl, LANES), F32), jax.ShapeDtypeStruct((8, LANES), F32)],
        scratch_shapes=[pltpu.VMEM((8, LANES), F32)],
        compiler_params=_cparams("arbitrary", "arbitrary"),
        name="moe_router",
    )(x2, modl, g, rw, rb)


def _gather_rows(table, idx):
    n_rows, d = idx.shape[0], table.shape[1]
    n_workers = SC_CORES * SC_SUBCORES
    per_worker = n_rows // n_workers
    chunk = SC_GATHER_ROWS
    assert per_worker * n_workers == n_rows and per_worker % chunk == 0
    mesh = plsc.VectorSubcoreMesh(core_axis_name="c", subcore_axis_name="s")

    @functools.partial(
        pl.kernel, mesh=mesh, out_type=jax.ShapeDtypeStruct((n_rows, d), table.dtype),
        scratch_types=[pltpu.VMEM((chunk,), I32), pltpu.VMEM((chunk, d), table.dtype), pltpu.SemaphoreType.DMA],
        name="sc_gather_rows")
    def gather(table_hbm, idx_hbm, out_hbm, idx_v, rows_v, sem):
        base = (lax.axis_index("s") * SC_CORES + lax.axis_index("c")) * per_worker

        @pl.loop(0, per_worker // chunk)
        def _(j):
            off = pl.multiple_of(base + j * chunk, 8)
            pltpu.sync_copy(idx_hbm.at[pl.ds(off, chunk)], idx_v)
            pltpu.async_copy(table_hbm.at[idx_v], rows_v, sem).wait()
            pltpu.sync_copy(rows_v, out_hbm.at[pl.ds(off, chunk)])

    return gather(table, idx)


def _dispatch_rows(table, slot, n_out):
    assert TOP_K == 2
    n_assign, (n_tok, d) = slot.shape[0], table.shape
    n_workers = SC_CORES * SC_SUBCORES
    per_worker = n_out // n_workers
    chunk, vec = SC_GATHER_ROWS, SC_LANES
    assert per_worker * n_workers == n_out and per_worker % chunk == 0 and n_assign % vec == 0
    assert n_out < 3 * n_tok
    mesh = plsc.VectorSubcoreMesh(core_axis_name="c", subcore_axis_name="s")

    @functools.partial(
        pl.kernel, mesh=mesh, out_type=jax.ShapeDtypeStruct((n_out, d), table.dtype),
        scratch_types=[pltpu.VMEM((n_assign,), I32), pltpu.VMEM((per_worker,), I32),
                       pltpu.VMEM((chunk, d), table.dtype), pltpu.SemaphoreType.DMA],
        compiler_params=pltpu.CompilerParams(needs_layout_passes=False),
        name="sc_dispatch_rows")
    def dispatch(table_hbm, slot_hbm, out_hbm, slots_v, tok_v, rows_v, sem):
        base = (lax.axis_index("s") * SC_CORES + lax.axis_index("c")) * per_worker
        pltpu.sync_copy(slot_hbm, slots_v)
        lane = lax.broadcasted_iota(I32, (vec,), 0)

        @pl.loop(0, per_worker // vec)
        def _(j):
            p = base + j * vec + lane
            p = jnp.where(p >= n_tok, p - n_tok, p)
            tok_v[pl.ds(j * vec, vec)] = jnp.where(p >= n_tok, p - n_tok, p)

        @pl.loop(0, n_assign // vec)
        def _(j):
            s = slots_v[pl.ds(j * vec, vec)] - base
            mine = (s >= 0) & (s < per_worker)
            tok = lax.shift_right_logical(j * vec + lane, 1)
            plsc.store_scatter(tok_v, [jnp.where(mine, s, 0)], tok, mask=mine)

        @pl.loop(0, per_worker // chunk)
        def _(j):
            off = pl.multiple_of(j * chunk, 8)
            pltpu.async_copy(table_hbm.at[tok_v.at[pl.ds(off, chunk)]], rows_v, sem).wait()
            pltpu.sync_copy(rows_v, out_hbm.at[pl.ds(base + off, chunk)])

    return dispatch(table, slot)


def _moe_kernel(be_ref, nu_ref, x_ref, w1g_ref, w1u_ref, w2_ref, y_ref, acc_ref):
    i, j = pl.program_id(0), pl.program_id(1)

    @pl.when(i < nu_ref[0])
    def _():
        @pl.when(j == 0)
        def _():
            acc_ref[...] = jnp.zeros_like(acc_ref)

        x = x_ref[...].astype(BF16)
        a =(_silu(_dot(x, w1g_ref[0].astype(BF16))) * _dot(x, w1u_ref[0].astype(BF16))).astype(BF16)
        acc_ref[...] += _dot(a, w2_ref[0].astype(BF16))

        @pl.when(j == pl.num_programs(1) - 1)
        def _():
            y_ref[...] = acc_ref[...]


def _moe_experts(xs, w1, w2, block_e, n_used, n_blocks):
    bm, fc = MOE_ROWS, MOE_FF
    nff = FF_EXPERT // fc
    used = lambda i, nu: jnp.minimum(i, nu[0] - 1)
    ffi = lambda i, j, nu: jnp.where(i < nu[0], j, nff - 1)
    grid_spec = pltpu.PrefetchScalarGridSpec(
        num_scalar_prefetch=2,
        grid=(n_blocks, nff),
        in_specs=[pl.BlockSpec((bm, D_MODEL), lambda i, j, be, nu: (used(i, nu), 0)),
                  pl.BlockSpec((1, D_MODEL, fc), lambda i, j, be, nu: (be[used(i, nu)], 0, ffi(i, j, nu))),
                  pl.BlockSpec((1, D_MODEL, fc), lambda i, j, be, nu: (be[used(i, nu)], 0, nff + ffi(i, j, nu))),
                  pl.BlockSpec((1, fc, D_MODEL), lambda i, j, be, nu: (be[used(i, nu)], ffi(i, j, nu), 0))],
        out_specs=pl.BlockSpec((bm, D_MODEL), lambda i, j, be, nu: (used(i, nu), 0)),
        scratch_shapes=[pltpu.VMEM((bm, D_MODEL), F32)],
    )
    return pl.pallas_call(
        _moe_kernel,
        grid_spec=grid_spec,
        out_shape=jax.ShapeDtypeStruct((n_blocks * bm, D_MODEL), F32),
        compiler_params=_cparams("arbitrary", "arbitrary"),
        name="moe_experts",
    )(block_e, n_used, xs, w1, w1, w2)


def _combine_kernel(y0_ref, y1_ref, gate_ref, x_ref, modl_ref, g_ref, o_ref):
    gt = gate_ref[...]
    y = gt[:, 0:1] * y0_ref[...] + gt[:, 1:2] * y1_ref[...]
    o_ref[...] = x_ref[...] + modl_ref[0, 5:6, :] * _rms(y, g_ref[...])


def _combine(yk, gates, x2, modl, g, bsz, t, ctx_len):
    tm = ROW_TILE
    nt, nct = t // tm, ctx_len // tm
    nlt = nt - nct
    nl = bsz * (t - ctx_len)
    orow = lambda c: pl.BlockSpec((tm, c), lambda b, i: (b * nlt + i, 0))
    second = pl.BlockSpec((tm, D_MODEL), lambda b, i: (nl // tm + b * nlt + i, 0))
    return pl.pallas_call(
        _combine_kernel,
        grid=(bsz, nlt),
        in_specs=[orow(D_MODEL), second, orow(LANES),
                  pl.BlockSpec((tm, D_MODEL), lambda b, i: (b * nt + nct + i, 0)),
                  pl.BlockSpec((1, 8, D_MODEL), lambda b, i: (b, 0, 0)), _resident((1, D_MODEL))],
        out_specs=orow(D_MODEL),
        out_shape=jax.ShapeDtypeStruct((nl, D_MODEL), F32),
        compiler_params=_cparams("arbitrary", "arbitrary"),
        name="moe_combine",
    )(yk, yk, gates, x2, modl, g)


def _routing_tables(experts, ranks, counts):
    bm = MOE_ROWS
    n_assign = experts.size
    padded = (counts + bm - 1) // bm * bm
    ends_p = jnp.cumsum(padded)
    pstarts = ends_p - padded
    onehot = experts[..., None] == jnp.arange(N_EXPERTS, dtype=I32)
    slot = (jnp.sum(jnp.where(onehot, pstarts, 0), axis=-1) + ranks).astype(I32)
    n_blocks = n_assign // bm + N_EXPERTS
    block_e = jnp.minimum(jnp.sum(ends_p[None, :] <= (jnp.arange(n_blocks, dtype=I32) * bm)[:, None], axis=1),
                          N_EXPERTS - 1).astype(I32)
    n_used = (ends_p[-1] // bm).astype(I32).reshape(1)
    return slot, block_e, n_used, n_blocks


def _row_layout(gcol, ncols, L):
    n = gcol.shape[0]
    return gcol[:, :ncols].reshape(n // L, L, ncols).transpose(0, 2, 1)


def kernel(x, c, ctx, c_ctx, mod_w, mod_b, norm_g, ab_in_w, m_conv_w, m_conv_b, m_gate_b, m_norm_w, l_conv_w,
           l_conv_b, l_wa, l_ba, l_wx, l_bx, l_lam, ab_out_w, ffn_w1, ffn_w2, cd_in_w, g_alpha_w, g_alpha_b,
           g_norm_w, s_conv_w, s_conv_b, s_dt_bias, s_A_log, s_D, s_norm_w, cd_out_w, router_w, router_b,
           moe_w1, moe_w2):
    bsz, seq, dm = x.shape
    ctx_len = ctx.shape[1]
    t = ctx_len + seq
    n = bsz * t
    assert ctx_len == ROW_TILE and seq % ROW_TILE == 0 and ROW_TILE % GRID_W == 0
    x2 = jnp.concatenate([ctx, x], axis=1).reshape(n, dm)

    c_all = jnp.concatenate([c, c_ctx[None, :], jnp.zeros((-(bsz + 1) % 8, dm), F32)], axis=0)
    mods = _modulation(c_all, mod_w, mod_b).reshape(mod_w.shape[0], c_all.shape[0], 6, dm)
    pad2 = jnp.zeros((bsz, 2, dm), F32)

    def layer_mods(layer):
        modl = jnp.concatenate([mods[layer, :bsz], pad2], axis=1)
        modc = jnp.concatenate([mods[layer, bsz], pad2[0]], axis=0)
        return modl, modc

    modl, modc = layer_mods(0)
    g = norm_g[0]
    w = ab_in_w[0]
    o_qk, o_v, o_o, o_gt, o_lx, o_lg = 0, 1024, 2048, 3072, 3088, 4112
    w_in = jnp.concatenate([w[:, o_qk:o_v], w[:, o_v:o_o], w[:, o_o:o_gt], w[:, o_lx:o_lg], w[:, o_lg:],
                            w[:, o_gt:o_lx], jnp.zeros((dm, LANES - 16), F32)], axis=1).astype(BF16)
    gb = jnp.concatenate([m_gate_b[0].reshape(1, 16), jnp.zeros((1, LANES - 16), F32)], axis=1)
    q, k, v, og, lx, glg, gates = _in_ab(x2, modl, modc, g[0:1], w_in, m_conv_w[0], m_conv_b[0][None],
                                         l_conv_w[0], l_conv_b[0][None], gb, bsz, t, ctx_len)
    hm_f, hm_b = _mlstm(q, k, v, gates, _row_layout(gates, 16, M_CHUNK), bsz, t, ctx_len)
    hl_f, hl_b = _lru(lx, l_wa[0].astype(BF16), l_wx[0].astype(BF16), l_ba[0][:, None], l_bx[0][:, None],
                      l_lam[0][:, None], bsz, t, ctx_len)
    x2 = _out_proj(_out_ab_kernel, "out_proj_ab", [hm_f, hm_b, og, hl_f, hl_b, glg], [m_norm_w[0][None]],
                   x2, modl, modc, g[1:2], ab_out_w[0].astype(BF16), bsz, t, ctx_len)
    x2 = _ffn(x2, modl, modc, g[2:3], g[3:4], ffn_w1[0].astype(BF16), ffn_w2[0].astype(BF16), bsz, t, ctx_len)

    modl, modc = layer_mods(1)
    g = norm_g[1]
    w = cd_in_w[0]
    o_gv, o_gr, o_ga, o_z, o_xbc, o_dt = 1024, 2048, 3072, 3104, 4128, 5664
    zpad = jnp.zeros((dm, LANES - 32), F32)
    w_in = jnp.concatenate([w[:, :o_gv], w[:, o_gv:o_gr], w[:, o_gr:o_ga], w[:, o_z:o_xbc], w[:, o_xbc:o_dt],
                            w[:, o_ga:o_z], zpad, w[:, o_dt:], w[:, o_dt:], jnp.zeros((dm, LANES - 64), F32)],
                           axis=1).astype(BF16)
    dtb = jnp.concatenate([s_dt_bias[0].reshape(1, 32), s_dt_bias[0].reshape(1, 32),
                           jnp.zeros((1, LANES - 64), F32)], axis=1)
    aneg = jnp.concatenate([jnp.zeros((1, 32), F32), -jnp.exp(s_A_log[0].reshape(1, 32)),
                            jnp.zeros((1, LANES - 64), F32)], axis=1)
    gq, gk, gv, grs, zs, sx, sb, sc, ga, dts = _in_cd(x2, modl, modc, g[0:1], w_in, s_conv_w[0], s_conv_b[0][None],
                                                      dtb, aneg, bsz, t, ctx_len)
    aw = jnp.zeros((2, LANES, G_HEADS * G_DK), F32)
    aw = aw.at[0, 0:G_RANK].set(g_alpha_w[0, 0]).at[1, G_RANK:2 * G_RANK].set(g_alpha_w[0, 1])
    awh = aw.astype(BF16)
    awl = (aw - awh.astype(F32)).astype(BF16)
    og_f, og_b = _gla(gq, gk, gv, ga, awh, awl, g_alpha_b[0][:, None], bsz, t, ctx_len)
    ys_f, ys_b = _ssd(sx, sb, sc, dts, _row_layout(dts, 4 * S_HEADS, S_CHUNK), bsz, t, ctx_len)
    dskip = jnp.repeat(s_D[0], S_P)[None]
    x2 = _out_proj(_out_cd_kernel, "out_proj_cd", [og_f, og_b, grs, ys_f, ys_b, sx, zs],
                   [g_norm_w[0][None], dskip, s_norm_w[0][None]],
                   x2, modl, modc, g[1:2], cd_out_w[0].astype(BF16), bsz, t, ctx_len)

    rw = jnp.concatenate([router_w[0], jnp.zeros((dm, LANES - N_EXPERTS), F32)], axis=1)
    rb = jnp.concatenate([router_b[0][None], jnp.zeros((1, LANES - N_EXPERTS), F32)], axis=1)
    h2, top_idx, top_gate, counts = _router(x2, modl, g[2:3], rw, rb, bsz, t, ctx_len)
    slot, block_e, n_used, n_blocks = _routing_tables(top_idx[:, 0:TOP_K], top_idx[:, TOP_K:2 * TOP_K],
                                                      counts[0, :N_EXPERTS].astype(I32))
    xs = _dispatch_rows(h2, slot.reshape(-1), n_blocks * MOE_ROWS)
    ys_e = _moe_experts(xs, moe_w1[0], moe_w2[0], block_e, n_used, n_blocks)
    yk = _gather_rows(ys_e, jnp.concatenate([slot[:, 0], slot[:, 1]]))
    out = _combine(yk, top_gate, x2, modl, g[3:4], bsz, t, ctx_len)
    return out.reshape(bsz, seq, dm)
```

```python
import functools

import jax
import jax.numpy as jnp
from jax import lax
from jax.experimental import pallas as pl
from jax.experimental.pallas import tpu as pltpu
from jax.experimental.pallas import tpu_sc as plsc

F32, BF16, I32 = jnp.float32, jnp.bfloat16, jnp.int32

D_MODEL = 1024
GRID_W = 64
EPS = 1e-6
LOG2E = 1.4426950408889634
M_HEADS, M_DK, M_DV = 4, 128, 256
LRU_BLOCKS, LRU_BW, LRU_C = 8, 128, 8.0
G_HEADS, G_DK, G_DV, G_RANK, G_TAU = 4, 128, 256, 16, 16.0
S_HEADS, S_P, S_N, S_GROUPS, S_HPG = 16, 64, 128, 2, 8
FF_DENSE = 2816
N_EXPERTS, TOP_K, FF_EXPERT = 8, 2, 3584

LANES = 128
VMEM_LIMIT = 56 * 1024 * 1024
ROW_TILE = 256
BIG_ROW_TILE_MAX = 640
PROJ_COLS = 256
M_CHUNK = 256
G_CHUNK, G_SUB = 256, 16
S_CHUNK = 256
LRU_TILE = 256
FF_STEP = 256
MOE_ROWS = 1024
MOE_FF = 512
SC_CORES, SC_SUBCORES = 2, 16
SC_LANES = 16
SC_GATHER_ROWS = 64


def _cparams(*sem):
    return pltpu.CompilerParams(dimension_semantics=sem, vmem_limit_bytes=VMEM_LIMIT)


def _silu(x):
    return x * jax.nn.sigmoid(x)


def _softplus(x):
    return jnp.maximum(x, 0.0) + jnp.log1p(jnp.exp(-jnp.abs(x)))


def _log_sigmoid(x):
    return jnp.minimum(x, 0.0) - jnp.log1p(jnp.exp(-jnp.abs(x)))


def _dot(a, b):
    return jnp.dot(a, b, preferred_element_type=F32)


def _dot_nt(a, b):
    return lax.dot_general(a, b, (((1,), (1,)), ((), ())), preferred_element_type=F32)


def _dot_tn(a, b):
    return lax.dot_general(a, b, (((0,), (0,)), ((), ())), preferred_element_type=F32)


def _split3(f):
    f1 = f.astype(BF16)
    r = f - f1.astype(F32)
    f2 = r.astype(BF16)
    f3 = (r - f2.astype(F32)).astype(BF16)
    return f1, f2, f3


def _sel_cols(mask01, f):
    p1, p2, p3 = _split3(f)
    return _dot(mask01, p1) + _dot(mask01, p2) + _dot(mask01, p3)


def _sel_rows(f, mask01):
    p1, p2, p3 = _split3(f)
    return _dot_nt(p1, mask01) + _dot_nt(p2, mask01) + _dot_nt(p3, mask01)


def _expand(f, sel01):
    p1, p2, p3 = _split3(f)
    return _dot(p1, sel01) + _dot(p2, sel01) + _dot(p3, sel01)


def _rms(u, g):
    return u * lax.rsqrt(jnp.mean(u * u, axis=-1, keepdims=True) + EPS) * g


def _head_rms(u, g, n_heads):
    w = u.shape[-1] // n_heads
    return jnp.concatenate([_rms(u[:, h * w:(h + 1) * w], g[:, h * w:(h + 1) * w]) for h in range(n_heads)], axis=1)


def _row_ids(tile_idx, tm):
    return tile_idx * tm + lax.broadcasted_iota(I32, (tm, 1), 0)


def _mod(modl_ref, modc_ref, idx, is_ctx):
    return jnp.where(is_ctx, modc_ref[idx:idx + 1, :], modl_ref[0, idx:idx + 1, :])


def _adaln(x, g, modl_ref, modc_ref, shift_idx, is_ctx):
    shift = _mod(modl_ref, modc_ref, shift_idx, is_ctx)
    scale = _mod(modl_ref, modc_ref, shift_idx + 1, is_ctx)
    return _rms(x, g) * (1.0 + scale) + shift


def _neighbour_tiles(r, step, is_lat):
    n_tiles, per_seg = r.shape[0], GRID_W // 8
    zero = jnp.zeros((1,) + r.shape[1:], r.dtype)

    def across(src):
        return jnp.where(is_lat, 0.0, r[src:src + 1]) if 0 <= src < n_tiles else zero

    pieces = []
    for s0 in range(0, n_tiles, per_seg):
        if step > 0:
            pieces += [across(s0 - 1), r[s0:s0 + per_seg - 1]]
        else:
            pieces += [r[s0 + 1:s0 + per_seg], across(s0 + per_seg)]
    return jnp.concatenate(pieces, axis=0)


def _dwconv(y, cw, cb, is_lat):
    tm, n = y.shape
    y3 = y.reshape(tm // 8, 8, n)
    sub = lax.broadcasted_iota(I32, (1, 8, 1), 1)
    w = lambda j: cw[j:j + 1, :].reshape(1, 1, n)
    r1, r2, r7 = pltpu.roll(y3, 1, 1), pltpu.roll(y3, 2, 1), pltpu.roll(y3, 7, 1)
    ym1 = jnp.where(sub >= 1, r1, _neighbour_tiles(r1, 1, is_lat))
    ym2 = jnp.where(sub >= 2, r2, _neighbour_tiles(r2, 1, is_lat))
    yp1 = jnp.where(sub <= 6, r7, _neighbour_tiles(r7, -1, is_lat))
    out = cb.reshape(1, 1, n) + w(0) * ym2 + w(1) * ym1 + w(2) * y3 + w(3) * yp1
    return out.reshape(tm, n)


def _resident(shape):
    nd = len(shape)
    return pl.BlockSpec(shape, lambda *_: (0,) * nd, pipeline_mode=pl.Buffered(1))


def _big_row_tile(t):
    return max(tm for tm in range(8, BIG_ROW_TILE_MAX + 1, 8) if t % tm == 0)


def _mod_kernel(c_ref, w_ref, b_ref, o_ref):
    a = _silu(c_ref[...])
    o_ref[0] = jnp.dot(a, w_ref[0], preferred_element_type=F32, precision=lax.Precision.HIGHEST) + b_ref[0]


def _modulation(c_all, mod_w, mod_b):
    depth, d, n6 = mod_w.shape
    rows = c_all.shape[0]
    tn = 1024
    return pl.pallas_call(
        _mod_kernel,
        grid=(depth, n6 // tn),
        in_specs=[pl.BlockSpec((rows, d), lambda l, j: (0, 0)),
                  pl.BlockSpec((1, d, tn), lambda l, j: (l, 0, j)),
                  pl.BlockSpec((1, 1, tn), lambda l, j: (l, 0, j))],
        out_specs=pl.BlockSpec((1, rows, tn), lambda l, j: (l, 0, j)),
        out_shape=jax.ShapeDtypeStruct((depth, rows, n6), F32),
        compiler_params=_cparams("arbitrary", "arbitrary"),
        name="modulation",
    )(c_all, mod_w, mod_b.reshape(depth, 1, n6))


def _project(h_ref, w_ref, w_col, out_ref, epilogue):
    for c in range(0, out_ref.shape[1], PROJ_COLS):
        y = _dot(h_ref[...], w_ref[:, w_col + c:w_col + c + PROJ_COLS])
        out_ref[:, c:c + PROJ_COLS] = epilogue(y, c).astype(out_ref.dtype)


def _in_ab_kernel(tm, ctx_len, x_ref, modl_ref, modc_ref, g_ref, w_ref, mcw_ref, mcb_ref, lcw_ref, lcb_ref,
                  gb_ref, q_ref, k_ref, v_ref, og_ref, lx_ref, glg_ref, gates_ref, h_scr):
    r = _row_ids(pl.program_id(1), tm)
    is_ctx = r < ctx_len
    h_scr[...] = _adaln(x_ref[...], g_ref[...], modl_ref, modc_ref, 0, is_ctx).astype(BF16)
    is_lat = pl.program_id(1) * tm >= ctx_len

    def conv(cw_ref, cb_ref):
        return lambda y, c: _dwconv(y, cw_ref[:, c:c + PROJ_COLS], cb_ref[:, c:c + PROJ_COLS], is_lat)

    qc, kc, lc = conv(mcw_ref, mcb_ref), conv(mcw_ref, mcb_ref), conv(lcw_ref, lcb_ref)
    _project(h_scr, w_ref, 0, q_ref, lambda y, c: _silu(qc(y, c)))
    _project(h_scr, w_ref, 512, k_ref, lambda y, c: _silu(kc(y, 512 + c)) * (M_DK ** -0.5))
    _project(h_scr, w_ref, 1024, v_ref, lambda y, c: y)
    _project(h_scr, w_ref, 2048, og_ref, lambda y, c: jax.nn.sigmoid(y))
    _project(h_scr, w_ref, 3072, lx_ref, lc)
    _project(h_scr, w_ref, 4096, glg_ref, lambda y, c: jax.nn.gelu(y))
    gt = _dot(h_scr[...], w_ref[:, 5120:5248]) + gb_ref[...]
    lane = lax.broadcasted_iota(I32, gt.shape, 1)
    gates_ref[...] = jnp.where(lane >= 2 * M_HEADS, _log_sigmoid(gt), gt)


def _in_ab(x2, modl, modc, g, w, mcw, mcb, lcw, lcb, gb, bsz, t, ctx_len):
    tm = ROW_TILE
    nt = t // tm
    n = bsz * t
    row = lambda c: pl.BlockSpec((tm, c), lambda b, i: (b * nt + i, 0))
    outs = [(512, BF16), (512, BF16), (1024, BF16), (1024, BF16), (1024, F32), (1024, BF16), (LANES, F32)]
    return pl.pallas_call(
        functools.partial(_in_ab_kernel, tm, ctx_len),
        grid=(bsz, nt),
        in_specs=[row(D_MODEL),
                  pl.BlockSpec((1, 8, D_MODEL), lambda b, i: (b, 0, 0)),
                  _resident((8, D_MODEL)), _resident((1, D_MODEL)), _resident(w.shape),
                  _resident(mcw.shape), _resident(mcb.shape), _resident(lcw.shape), _resident(lcb.shape),
                  _resident(gb.shape)],
        out_specs=[row(c) for c, _ in outs],
        out_shape=[jax.ShapeDtypeStruct((n, c), dt) for c, dt in outs],
        scratch_shapes=[pltpu.VMEM((tm, D_MODEL), BF16)],
        compiler_params=_cparams("arbitrary", "arbitrary"),
        name="in_proj_ab",
    )(x2, modl, modc, g, w, mcw, mcb, lcw, lcb, gb)


def _reverse_order(n_ctx_chunks, n_chunks):
    return lambda c: jnp.where(c < n_ctx_chunks, n_ctx_chunks - 1 - c, n_chunks - 1 - (c - n_ctx_chunks))


def _tri_mask(L, reverse):
    row = lax.broadcasted_iota(I32, (L, L), 0)
    col = lax.broadcasted_iota(I32, (L, L), 1)
    return (col >= row) if reverse else (col <= row)


def _bidir_scan(kern, L, row_ins, row3_ins, res_ins, out_cols, scratch, bsz, t, ctx_len, name):
    nc = t // L
    rev = _reverse_order(ctx_len // L, nc)
    fwd = lambda i: i
    n = bsz * t

    def row(c, order):
        return pl.BlockSpec((L, c), lambda b, i: (b * nc + order(i), 0))

    def row3(shape, order):
        return pl.BlockSpec((1,) + tuple(shape[1:]), lambda b, i: (b * nc + order(i), 0, 0))

    in_specs, args = [], []
    for order in (fwd, rev):
        in_specs += [row(a.shape[1], order) for a in row_ins] + [row3(a.shape, order) for a in row3_ins]
        args += list(row_ins) + list(row3_ins)
    in_specs += [_resident(a.shape) for a in res_ins]
    args += list(res_ins)
    return pl.pallas_call(
        kern,
        grid=(bsz, nc),
        in_specs=in_specs,
        out_specs=[row(c, fwd) for c in out_cols] + [row(c, rev) for c in out_cols],
        out_shape=[jax.ShapeDtypeStruct((n, c), BF16) for c in out_cols] * 2,
        scratch_shapes=scratch,
        compiler_params=_cparams("arbitrary", "arbitrary"),
        name=name,
    )(*args)


def _zero_at_start(*scratch):
    @pl.when(pl.program_id(1) == 0)
    def _():
        for s in scratch:
            s[...] = jnp.zeros_like(s)


def _mlstm_kernel(L, qf, kf, vf, gcf, grf, qb, kb, vb, gcb, grb, of, ob, c_scr, m_scr):
    _zero_at_start(c_scr, m_scr)
    ones_col = (lax.broadcasted_iota(I32, (L, LANES), 1) == 0).astype(BF16)
    chains = []
    for d, (q_ref, k_ref, v_ref, gc_ref, gr_ref, out_ref) in enumerate(((qf, kf, vf, gcf, grf, of),
                                                                         (qb, kb, vb, gcb, grb, ob))):
        reverse = d == 1
        mask = _tri_mask(L, reverse)
        mask01 = mask.astype(BF16)
        gc = gc_ref[...]
        gr = gr_ref[0]
        bcol = _sel_cols(mask01, gc)
        brow = _sel_rows(gr, mask01)
        last = 0 if reverse else L - 1
        for h in range(M_HEADS):
            fo, io = 2 * M_HEADS + M_HEADS * d + h, M_HEADS * d + h
            bc, br = bcol[:, fo:fo + 1], brow[fo:fo + 1, :]
            ic, ir = gc[:, io:io + 1], gr[io:io + 1, :]
            m = m_scr[d, h:h + 1, 0:1]
            g = bc + m
            dlog = jnp.where(mask, bc - br + ir, -jnp.inf)
            mt = jnp.maximum(g, jnp.max(dlog, axis=1, keepdims=True))
            bl = bc[last:last + 1, :]
            a_s = bl - bc + ic
            m_new = jnp.maximum(bl + m, jnp.max(a_s, axis=0, keepdims=True))
            chains.append(dict(
                d=d, h=h, out_ref=out_ref, mt=mt, w_inter=jnp.exp(g - mt), p=jnp.exp(dlog - mt),
                m_new=m_new, decay=jnp.exp(bl + m - m_new), ws=jnp.exp(a_s - m_new),
                qh=q_ref[:, h * M_DK:(h + 1) * M_DK], kh=k_ref[:, h * M_DK:(h + 1) * M_DK],
                vx=jnp.concatenate([v_ref[:, h * M_DV:(h + 1) * M_DV], ones_col], axis=1)))
    for ch in chains:
        ch["sc"] = (_dot_nt(ch["qh"], ch["kh"]) * ch["p"]).astype(BF16)
    for ch in chains:
        cx = c_scr[ch["d"], ch["h"]]
        numx = ch["w_inter"] * _dot(ch["qh"], cx.astype(BF16)) + _dot(ch["sc"], ch["vx"])
        den = numx[:, M_DV:M_DV + 1]
        hh = numx[:, 0:M_DV] * (1.0 / jnp.maximum(jnp.abs(den), jnp.exp(-ch["mt"])))
        ch["out_ref"][:, ch["h"] * M_DV:(ch["h"] + 1) * M_DV] = hh.astype(BF16)
    for ch in chains:
        d, h = ch["d"], ch["h"]
        kw = (ch["kh"].astype(F32) * ch["ws"]).astype(BF16)
        c_scr[d, h] = ch["decay"] * c_scr[d, h] + _dot_tn(kw, ch["vx"])
        m_scr[d, h:h + 1, 0:1] = ch["m_new"]


def _mlstm(q, k, v, gcol, grow, bsz, t, ctx_len):
    L = M_CHUNK
    scratch = [pltpu.VMEM((2, M_HEADS, M_DK, M_DV + LANES), F32), pltpu.VMEM((2, 8, LANES), F32)]
    return _bidir_scan(functools.partial(_mlstm_kernel, L), L, [q, k, v, gcol], [grow], [], [1024], scratch,
                       bsz, t, ctx_len, "mlstm_scan")


def _lru_gates(L, d, u_ref, wa_ref, wx_ref, ba_ref, bx_ref, lam_ref, a_scr, b_scr):
    reverse = d == 1
    u = u_ref[...]
    ub = u.astype(BF16)
    sp = _softplus(-lam_ref[d])
    for n in range(LRU_BLOCKS):
        sl = slice(n * LRU_BW, (n + 1) * LRU_BW)
        rg = jax.nn.sigmoid(_dot(ub[:, sl], wa_ref[d, n]) + ba_ref[d, :, sl])
        ig = jax.nn.sigmoid(_dot(ub[:, sl], wx_ref[d, n]) + bx_ref[d, :, sl])
        log_a = -LRU_C * rg * sp[:, sl]
        a = jnp.exp(log_a)
        a_scr[d, :, sl] = a
        b_scr[d, :, sl] = jnp.sqrt(-jnp.tanh(log_a) * (a * a + 1.0)) * (ig * u[:, sl])
    a = a_scr[d].reshape(L // 8, 8, LRU_BLOCKS * LRU_BW)
    b = b_scr[d].reshape(L // 8, 8, LRU_BLOCKS * LRU_BW)
    sub = lax.broadcasted_iota(I32, (1, 8, 1), 1)
    for s in (1, 2, 4):
        if reverse:
            a_sh, b_sh, keep = pltpu.roll(a, 8 - s, 1), pltpu.roll(b, 8 - s, 1), sub < 8 - s
        else:
            a_sh, b_sh, keep = pltpu.roll(a, s, 1), pltpu.roll(b, s, 1), sub >= s
        b = jnp.where(keep, a * b_sh + b, b)
        a = jnp.where(keep, a * a_sh, a)
    a_scr[d] = a.reshape(L, LRU_BLOCKS * LRU_BW)
    b_scr[d] = b.reshape(L, LRU_BLOCKS * LRU_BW)


def _lru_kernel(L, uf, ub, wa_ref, wx_ref, ba_ref, bx_ref, lam_ref, of, ob, a_scr, b_scr, h_scr):
    _zero_at_start(h_scr)
    _lru_gates(L, 0, uf, wa_ref, wx_ref, ba_ref, bx_ref, lam_ref, a_scr, b_scr)
    _lru_gates(L, 1, ub, wa_ref, wx_ref, ba_ref, bx_ref, lam_ref, a_scr, b_scr)
    ng = L // 8

    def body(j, carry):
        hf, hb = carry
        rf = pl.ds(pl.multiple_of(j * 8, 8), 8)
        rb = pl.ds(pl.multiple_of((ng - 1 - j) * 8, 8), 8)
        gf = b_scr[0, rf, :] + a_scr[0, rf, :] * hf
        gb = b_scr[1, rb, :] + a_scr[1, rb, :] * hb
        b_scr[0, rf, :] = gf
        b_scr[1, rb, :] = gb
        return gf[7:8, :], gb[0:1, :]

    hf, hb = lax.fori_loop(0, ng, body, (h_scr[0:1, :], h_scr[1:2, :]))
    h_scr[0:1, :] = hf
    h_scr[1:2, :] = hb
    of[...] = b_scr[0].astype(BF16)
    ob[...] = b_scr[1].astype(BF16)


def _lru(u, wa, wx, ba, bx, lam, bsz, t, ctx_len):
    L = LRU_TILE
    scratch = [pltpu.VMEM((2, L, 1024), F32), pltpu.VMEM((2, L, 1024), F32), pltpu.VMEM((8, 1024), F32)]
    return _bidir_scan(functools.partial(_lru_kernel, L), L, [u], [], [wa, wx, ba, bx, lam], [1024], scratch,
                       bsz, t, ctx_len, "rglru_scan")


def _out_tail(tm, ctx_len, a1, a2, x_ref, modl_ref, modc_ref, g_ref, w_ref, xo_ref):
    is_ctx = _row_ids(pl.program_id(1), tm) < ctx_len
    half = w_ref.shape[0] // 2
    y = _dot(a1.astype(BF16), w_ref[0:half, :]) + _dot(a2.astype(BF16), w_ref[half:, :])
    xo_ref[...] = x_ref[...] + _mod(modl_ref, modc_ref, 2, is_ctx) * _rms(y, g_ref[...])


def _out_ab_kernel(tm, ctx_len, hf, hb, og, lf, lb, glg, nw, x_ref, modl_ref, modc_ref, g_ref, w_ref, xo_ref):
    hm = _head_rms(hf[...].astype(F32) + hb[...].astype(F32), nw[...], M_HEADS) * og[...].astype(F32)
    hl = (lf[...].astype(F32) + lb[...].astype(F32)) * glg[...].astype(F32)
    _out_tail(tm, ctx_len, hm, hl, x_ref, modl_ref, modc_ref, g_ref, w_ref, xo_ref)


def _out_cd_kernel(tm, ctx_len, gf, gb, grs, yf, yb, sx, zs, gnw, dsk, snw, x_ref, modl_ref, modc_ref, g_ref,
                   w_ref, xo_ref):
    og = _head_rms(gf[...].astype(F32) + gb[...].astype(F32), gnw[...], G_HEADS) * grs[...].astype(F32)
    ys = yf[...].astype(F32) + yb[...].astype(F32) + dsk[...] * sx[...].astype(F32)
    ys = _rms(ys * zs[...].astype(F32), snw[...])
    _out_tail(tm, ctx_len, og, ys, x_ref, modl_ref, modc_ref, g_ref, w_ref, xo_ref)


def _out_proj(kern, name, acts, vecs, x2, modl, modc, g, w, bsz, t, ctx_len):
    tm = _big_row_tile(t)
    nt = t // tm
    row = lambda c: pl.BlockSpec((tm, c), lambda b, i: (b * nt + i, 0))
    return pl.pallas_call(
        functools.partial(kern, tm, ctx_len),
        grid=(bsz, nt),
        in_specs=[row(1024)] * len(acts) + [_resident((1, 1024))] * len(vecs)
        + [row(D_MODEL), pl.BlockSpec((1, 8, D_MODEL), lambda b, i: (b, 0, 0)),
           _resident((8, D_MODEL)), _resident((1, D_MODEL)), _resident(w.shape)],
        out_specs=row(D_MODEL),
        out_shape=jax.ShapeDtypeStruct(x2.shape, F32),
        compiler_params=_cparams("arbitrary", "arbitrary"),
        name=name,
    )(*acts, *vecs, x2, modl, modc, g, w)


def _ffn_kernel(tm, ctx_len, x_ref, modl_ref, modc_ref, g2_ref, g3_ref, w1_ref, w2_ref, xo_ref):
    is_ctx = _row_ids(pl.program_id(1), tm) < ctx_len
    x = x_ref[...]
    h = _adaln(x, g2_ref[...], modl_ref, modc_ref, 3, is_ctx).astype(BF16)
    ff = w2_ref.shape[0]
    acc = jnp.zeros((tm, D_MODEL), F32)
    for j in range(ff // FF_STEP):
        lo = j * FF_STEP
        gj = _dot(h, w1_ref[:, lo:lo + FF_STEP])
        uj = _dot(h, w1_ref[:, ff + lo:ff + lo + FF_STEP])
        acc = acc + _dot((_silu(gj) * uj).astype(BF16), w2_ref[lo:lo + FF_STEP, :])
    xo_ref[...] = x + _mod(modl_ref, modc_ref, 5, is_ctx) * _rms(acc, g3_ref[...])


def _ffn(x2, modl, modc, g2, g3, w1, w2, bsz, t, ctx_len):
    tm = _big_row_tile(t)
    nt = t // tm
    row = lambda c: pl.BlockSpec((tm, c), lambda b, i: (b * nt + i, 0))
    return pl.pallas_call(
        functools.partial(_ffn_kernel, tm, ctx_len),
        grid=(bsz, nt),
        in_specs=[row(D_MODEL), pl.BlockSpec((1, 8, D_MODEL), lambda b, i: (b, 0, 0)), _resident((8, D_MODEL)),
                  _resident((1, D_MODEL)), _resident((1, D_MODEL)), _resident(w1.shape), _resident(w2.shape)],
        out_specs=row(D_MODEL),
        out_shape=jax.ShapeDtypeStruct(x2.shape, F32),
        compiler_params=_cparams("arbitrary", "arbitrary"),
        name="dense_swiglu",
    )(x2, modl, modc, g2, g3, w1, w2)


def _in_cd_kernel(tm, ctx_len, x_ref, modl_ref, modc_ref, g_ref, w_ref, cw_ref, cb_ref, dtb_ref, aneg_ref,
                  gq_ref, gk_ref, gv_ref, gr_ref, z_ref, sx_ref, sb_ref, sc_ref, ga_ref, dts_ref, h_scr):
    r = _row_ids(pl.program_id(1), tm)
    is_ctx = r < ctx_len
    h_scr[...] = _adaln(x_ref[...], g_ref[...], modl_ref, modc_ref, 0, is_ctx).astype(BF16)
    is_lat = pl.program_id(1) * tm >= ctx_len

    def conv_silu(col0):
        return lambda y, c: _silu(_dwconv(y, cw_ref[:, col0 + c:col0 + c + PROJ_COLS],
                                          cb_ref[:, col0 + c:col0 + c + PROJ_COLS], is_lat))

    _project(h_scr, w_ref, 0, gq_ref, lambda y, c: y * (G_DK ** -0.5))
    _project(h_scr, w_ref, 512, gk_ref, lambda y, c: y)
    _project(h_scr, w_ref, 1024, gv_ref, lambda y, c: y)
    _project(h_scr, w_ref, 2048, gr_ref, lambda y, c: _silu(y))
    _project(h_scr, w_ref, 3072, z_ref, lambda y, c: _silu(y))
    _project(h_scr, w_ref, 4096, sx_ref, conv_silu(0))
    _project(h_scr, w_ref, 5120, sb_ref, conv_silu(1024))
    _project(h_scr, w_ref, 5376, sc_ref, conv_silu(1280))
    ga_ref[...] = _dot(h_scr[...], w_ref[:, 5632:5760])
    dt = _softplus(_dot(h_scr[...], w_ref[:, 5760:5888]) + dtb_ref[...])
    lane = lax.broadcasted_iota(I32, dt.shape, 1)
    dts_ref[...] = jnp.where(lane < 2 * S_HEADS, dt, dt * aneg_ref[...])


def _in_cd(x2, modl, modc, g, w, cw, cb, dtb, aneg, bsz, t, ctx_len):
    tm = ROW_TILE
    nt = t // tm
    n = bsz * t
    row = lambda c: pl.BlockSpec((tm, c), lambda b, i: (b * nt + i, 0))
    outs = [(512, BF16), (512, BF16), (1024, BF16), (1024, BF16), (1024, BF16), (1024, BF16), (256, BF16),
            (256, BF16), (LANES, F32), (LANES, F32)]
    return pl.pallas_call(
        functools.partial(_in_cd_kernel, tm, ctx_len),
        grid=(bsz, nt),
        in_specs=[row(D_MODEL), pl.BlockSpec((1, 8, D_MODEL), lambda b, i: (b, 0, 0)), _resident((8, D_MODEL)),
                  _resident((1, D_MODEL)), _resident(w.shape), _resident(cw.shape), _resident(cb.shape),
                  _resident(dtb.shape), _resident(aneg.shape)],
        out_specs=[row(c) for c, _ in outs],
        out_shape=[jax.ShapeDtypeStruct((n, c), dt) for c, dt in outs],
        scratch_shapes=[pltpu.VMEM((tm, D_MODEL), BF16)],
        compiler_params=_cparams("arbitrary", "arbitrary"),
        name="in_proj_cd",
    )(x2, modl, modc, g, w, cw, cb, dtb, aneg)


def _gla_kernel(L, C, qf, kf, vf, gaf, qb, kb, vb, gab, awh_ref, awl_ref, ab_ref, of, ob, s_scr):
    _zero_at_start(s_scr)
    nb = L // C
    chains = []
    for d, (q_ref, k_ref, v_ref, ga_ref, out_ref) in enumerate(((qf, kf, vf, gaf, of), (qb, kb, vb, gab, ob))):
        reverse = d == 1
        mask = _tri_mask(L, reverse)
        g1, g2, _ = _split3(ga_ref[...])
        pre = _dot(g1, awh_ref[d]) + _dot(g2, awh_ref[d]) + _dot(g1, awl_ref[d])
        lg = _log_sigmoid(pre + ab_ref[d]) * (1.0 / G_TAU)
        ball = _sel_cols(mask.astype(BF16), lg)
        for h in range(G_HEADS):
            ks = slice(h * G_DK, (h + 1) * G_DK)
            b = ball[:, ks] * LOG2E
            qh = q_ref[:, ks].astype(F32)
            inter = _dot_nt((qh * jnp.exp2(b)).astype(BF16), s_scr[d, h].astype(BF16))
            chains.append(dict(d=d, h=h, reverse=reverse, mask=mask, b=b, qh=qh, kh=k_ref[:, ks].astype(F32),
                               vh=v_ref[:, h * G_DV:(h + 1) * G_DV], inter=inter, out_ref=out_ref, blocks=[]))
    for i in range(nb):
        lo, hi = i * C, (i + 1) * C
        for ch in chains:
            b = ch["b"]
            if ch["reverse"]:
                bref = b[hi:hi + 1, :] if i < nb - 1 else jnp.zeros((1, G_DK), F32)
                v0, v1 = lo // LANES * LANES, L
            else:
                bref = b[lo - 1:lo, :] if i > 0 else jnp.zeros((1, G_DK), F32)
                v0, v1 = 0, min(L, -(-hi // LANES) * LANES)
            qi = (ch["qh"][lo:hi, :] * jnp.exp2(b[lo:hi, :] - bref)).astype(BF16)
            ki = (ch["kh"][v0:v1] * jnp.exp2(bref - b[v0:v1])).astype(BF16)
            pieces = [jnp.zeros((C, v0), F32)] if v0 else []
            pieces.append(_dot_nt(qi, ki))
            if v1 < L:
                pieces.append(jnp.zeros((C, L - v1), F32))
            ch["blocks"].append(jnp.concatenate(pieces, axis=1) if len(pieces) > 1 else pieces[0])
    for ch in chains:
        att = jnp.where(ch["mask"], jnp.concatenate(ch["blocks"], axis=0), 0.0).astype(BF16)
        oh = ch["inter"] + _dot(att, ch["vh"])
        ch["out_ref"][:, ch["h"] * G_DV:(ch["h"] + 1) * G_DV] = oh.astype(BF16)
    for ch in chains:
        d, h, b = ch["d"], ch["h"], ch["b"]
        last = 0 if ch["reverse"] else L - 1
        bl = b[last:last + 1, :]
        kd = (ch["kh"] * jnp.exp2(bl - b)).astype(BF16)
        s_scr[d, h] = s_scr[d, h] * jnp.exp2(bl) + _dot_tn(ch["vh"], kd)


def _gla(q, k, v, ga, awh, awl, ab, bsz, t, ctx_len):
    L, C = G_CHUNK, G_SUB
    scratch = [pltpu.VMEM((2, G_HEADS, G_DV, G_DK), F32)]
    return _bidir_scan(functools.partial(_gla_kernel, L, C), L, [q, k, v, ga], [], [awh, awl, ab], [1024], scratch,
                       bsz, t, ctx_len, "gla_scan")


def _ssd_kernel(L, xf, bf, cf, gcf, grf, xb, bb, cb_, gcb, grb, of, ob, s_scr):
    _zero_at_start(s_scr)
    er = lax.broadcasted_iota(I32, (LANES, S_HEADS * S_P), 0)
    ec = jnp.right_shift(lax.broadcasted_iota(I32, (LANES, S_HEADS * S_P), 1), 6)
    glane = lax.broadcasted_iota(I32, (L, LANES), 1)
    lane = lax.broadcasted_iota(I32, (L, 2 * S_P), 1)
    dirs, chains = [], []
    for d, (x_ref, b_ref, c_ref, gc_ref, gr_ref, out_ref) in enumerate(((xf, bf, cf, gcf, grf, of),
                                                                         (xb, bb, cb_, gcb, grb, ob))):
        reverse = d == 1
        mask = _tri_mask(L, reverse)
        mask01 = mask.astype(BF16)
        gc = gc_ref[...]
        gr = gr_ref[0]
        bcol = _sel_cols(mask01, gc)
        brow = _sel_rows(gr, mask01)
        last = 0 if reverse else L - 1
        dt_o, la_o = S_HEADS * d, 2 * S_HEADS + S_HEADS * d
        sel_la = (er == ec + la_o).astype(BF16)
        x = x_ref[...]
        bla = jnp.where((glane >= la_o) & (glane < la_o + S_HEADS), bcol, 0.0)
        dt_at_la = pltpu.roll(gc, 2 * S_HEADS, 1)
        bl = bla[last:last + 1, :]
        ebx = _dot(jnp.exp(bla).astype(BF16), sel_la)
        wx = _dot((jnp.exp(bl - bla) * dt_at_la).astype(BF16), sel_la)
        dirs.append(dict(d=d, mask=mask, gr=gr, x=x, ebx=ebx, last=last, dt_o=dt_o, la_o=la_o, out_ref=out_ref,
                         xw=(x.astype(F32) * wx).astype(BF16),
                         bcol2=bcol * LOG2E, brow2=brow * LOG2E))
        for g in range(S_GROUPS):
            ns = slice(g * S_N, (g + 1) * S_N)
            chains.append(dict(dr=dirs[-1], g=g, bg=b_ref[:, ns], cg=c_ref[:, ns],
                               gs=slice(g * S_HPG * S_P, (g + 1) * S_HPG * S_P)))
    for ch in chains:
        dr = ch["dr"]
        ch["cb"] = _dot_nt(ch["cg"], ch["bg"])
        ch["inter"] = _dot(ch["cg"], s_scr[dr["d"], ch["g"]].astype(BF16)) * dr["ebx"][:, ch["gs"]]
    for ch in chains:
        dr, g = ch["dr"], ch["g"]
        parts = []
        for pr in range(S_HPG // 2):
            e0 = g * S_HPG + 2 * pr
            xp = dr["x"][:, e0 * S_P:(e0 + 2) * S_P]
            res = []
            for e in (e0, e0 + 1):
                col = dr["la_o"] + e
                bc, br = dr["bcol2"][:, col:col + 1], dr["brow2"][col:col + 1, :]
                dtr = dr["gr"][dr["dt_o"] + e:dr["dt_o"] + e + 1, :]
                att = (ch["cb"] * dtr * jnp.exp2(jnp.where(dr["mask"], bc - br, -jnp.inf))).astype(BF16)
                res.append(_dot(att, xp))
            parts.append(jnp.where(lane < S_P, res[0], res[1]))
        dr["out_ref"][:, ch["gs"]] = (jnp.concatenate(parts, axis=1) + ch["inter"]).astype(BF16)
    for ch in chains:
        dr, g, gs = ch["dr"], ch["g"], ch["gs"]
        s_scr[dr["d"], g] = (s_scr[dr["d"], g] * dr["ebx"][dr["last"]:dr["last"] + 1, gs]
                             + _dot_tn(ch["bg"], dr["xw"][:, gs]))


def _ssd(x, bm, cm, gcol, grow, bsz, t, ctx_len):
    L = S_CHUNK
    scratch = [pltpu.VMEM((2, S_GROUPS, S_N, S_HPG * S_P), F32)]
    return _bidir_scan(functools.partial(_ssd_kernel, L), L, [x, bm, cm, gcol], [grow], [], [1024], scratch,
                       bsz, t, ctx_len, "ssd_scan")


def _router_kernel(x_ref, modl_ref, g_ref, rw_ref, rb_ref, h_ref, idx_ref, gate_ref, cnt_ref, base_scr):
    @pl.when((pl.program_id(0) == 0) & (pl.program_id(1) == 0))
    def _():
        base_scr[...] = jnp.zeros_like(base_scr)

    h = _rms(x_ref[...], g_ref[...]) * (1.0 + modl_ref[0, 4:5, :]) + modl_ref[0, 3:4, :]
    h_ref[...] = h
    logits = jnp.dot(h, rw_ref[...], preferred_element_type=F32, precision=lax.Precision.HIGHEST) + rb_ref[...]
    lane = lax.broadcasted_iota(I32, logits.shape, 1)
    logits = jnp.where(lane < N_EXPERTS, logits, -jnp.inf)
    lanef = lane.astype(F32)
    m1 = jnp.max(logits, axis=1, keepdims=True)
    i1 = jnp.min(jnp.where(logits == m1, lanef, float(LANES)), axis=1, keepdims=True)
    rest = jnp.where(lanef == i1, -jnp.inf, logits)
    m2 = jnp.max(rest, axis=1, keepdims=True)
    i2 = jnp.min(jnp.where(rest == m2, lanef, float(LANES)), axis=1, keepdims=True)
    e = jnp.exp(m2 - m1)
    g1 = 1.0 / (1.0 + e)
    gate_ref[...] = jnp.where(lane == 0, g1, jnp.where(lane == 1, e * g1, 0.0))
    tm = logits.shape[0]
    oh1, oh2 = (lanef == i1).astype(F32), (lanef == i2).astype(F32)
    earlier = (lax.broadcasted_iota(I32, (tm, tm), 1) < lax.broadcasted_iota(I32, (tm, tm), 0)).astype(BF16)
    seen = base_scr[0:1, :] + _dot(earlier, (oh1 + oh2).astype(BF16))
    r1 = jnp.sum(oh1 * seen, axis=1, keepdims=True)
    r2 = jnp.sum(oh2 * seen, axis=1, keepdims=True)
    base_scr[0:1, :] = base_scr[0:1, :] + jnp.sum(oh1 + oh2, axis=0, keepdims=True)
    cnt_ref[...] = base_scr[...]
    idx_ref[...] = jnp.where(lane == 0, i1, jnp.where(lane == 1, i2, jnp.where(lane == 2, r1, jnp.where(
        lane == 3, r2, 0.0)))).astype(I32)


def _router(x2, modl, g, rw, rb, bsz, t, ctx_len):
    tm = ROW_TILE
    nt, nct = t // tm, ctx_len // tm
    nl = bsz * (t - ctx_len)
    nlt = nt - nct
    orow = lambda c: pl.BlockSpec((tm, c), lambda b, i: (b * nlt + i, 0))
    return pl.pallas_call(
        _router_kernel,
        grid=(bsz, nlt),
        in_specs=[pl.BlockSpec((tm, D_MODEL), lambda b, i: (b * nt + nct + i, 0)),
                  pl.BlockSpec((1, 8, D_MODEL), lambda b, i: (b, 0, 0)),
                  _resident((1, D_MODEL)), _resident(rw.shape), _resident(rb.shape)],
        out_specs=[orow(D_MODEL), orow(LANES), orow(LANES), pl.BlockSpec((8, LANES), lambda b, i: (0, 0))],
        out_shape=[jax.ShapeDtypeStruct((nl, D_MODEL), F32), jax.ShapeDtypeStruct((nl, LANES), I32),
                   jax.ShapeDtypeStruct((nl, LANES), F32), jax.ShapeDtypeStruct((8, LANES), F32)],
        scratch_shapes=[pltpu.VMEM((8, LANES), F32)],
        compiler_params=_cparams("arbitrary", "arbitrary"),
        name="moe_router",
    )(x2, modl, g, rw, rb)


def _gather_rows(table, idx):
    n_rows, d = idx.shape[0], table.shape[1]
    n_workers = SC_CORES * SC_SUBCORES
    per_worker = n_rows // n_workers
    chunk = SC_GATHER_ROWS
    assert per_worker * n_workers == n_rows and per_worker % chunk == 0
    mesh = plsc.VectorSubcoreMesh(core_axis_name="c", subcore_axis_name="s")

    @functools.partial(
        pl.kernel, mesh=mesh, out_type=jax.ShapeDtypeStruct((n_rows, d), table.dtype),
        scratch_types=[pltpu.VMEM((chunk,), I32), pltpu.VMEM((chunk, d), table.dtype), pltpu.SemaphoreType.DMA],
        name="sc_gather_rows")
    def gather(table_hbm, idx_hbm, out_hbm, idx_v, rows_v, sem):
        base = (lax.axis_index("s") * SC_CORES + lax.axis_index("c")) * per_worker

        @pl.loop(0, per_worker // chunk)
        def _(j):
            off = pl.multiple_of(base + j * chunk, 8)
            pltpu.sync_copy(idx_hbm.at[pl.ds(off, chunk)], idx_v)
            pltpu.async_copy(table_hbm.at[idx_v], rows_v, sem).wait()
            pltpu.sync_copy(rows_v, out_hbm.at[pl.ds(off, chunk)])

    return gather(table, idx)


def _dispatch_rows(table, slot, n_out):
    assert TOP_K == 2
    n_assign, (n_tok, d) = slot.shape[0], table.shape
    n_workers = SC_CORES * SC_SUBCORES
    per_worker = n_out // n_workers
    chunk, vec = SC_GATHER_ROWS, SC_LANES
    assert per_worker * n_workers == n_out and per_worker % chunk == 0 and n_assign % vec == 0
    assert n_out < 3 * n_tok
    mesh = plsc.VectorSubcoreMesh(core_axis_name="c", subcore_axis_name="s")

    @functools.partial(
        pl.kernel, mesh=mesh, out_type=jax.ShapeDtypeStruct((n_out, d), table.dtype),
        scratch_types=[pltpu.VMEM((n_assign,), I32), pltpu.VMEM((per_worker,), I32),
                       pltpu.VMEM((chunk, d), table.dtype), pltpu.SemaphoreType.DMA],
        compiler_params=pltpu.CompilerParams(needs_layout_passes=False),
        name="sc_dispatch_rows")
    def dispatch(table_hbm, slot_hbm, out_hbm, slots_v, tok_v, rows_v, sem):
        base = (lax.axis_index("s") * SC_CORES + lax.axis_index("c")) * per_worker
        pltpu.sync_copy(slot_hbm, slots_v)
        lane = lax.broadcasted_iota(I32, (vec,), 0)

        @pl.loop(0, per_worker // vec)
        def _(j):
            p = base + j * vec + lane
            p = jnp.where(p >= n_tok, p - n_tok, p)
            tok_v[pl.ds(j * vec, vec)] = jnp.where(p >= n_tok, p - n_tok, p)

        @pl.loop(0, n_assign // vec)
        def _(j):
            s = slots_v[pl.ds(j * vec, vec)] - base
            mine = (s >= 0) & (s < per_worker)
            tok = lax.shift_right_logical(j * vec + lane, 1)
            plsc.store_scatter(tok_v, [jnp.where(mine, s, 0)], tok, mask=mine)

        @pl.loop(0, per_worker // chunk)
        def _(j):
            off = pl.multiple_of(j * chunk, 8)
            pltpu.async_copy(table_hbm.at[tok_v.at[pl.ds(off, chunk)]], rows_v, sem).wait()
            pltpu.sync_copy(rows_v, out_hbm.at[pl.ds(base + off, chunk)])

    return dispatch(table, slot)


def _moe_kernel(be_ref, nu_ref, x_ref, w1g_ref, w1u_ref, w2_ref, y_ref, acc_ref):
    i, j = pl.program_id(0), pl.program_id(1)

    @pl.when(i < nu_ref[0])
    def _():
        @pl.when(j == 0)
        def _():
            acc_ref[...] = jnp.zeros_like(acc_ref)

        x = x_ref[...].astype(BF16)
        a =(_silu(_dot(x, w1g_ref[0].astype(BF16))) * _dot(x, w1u_ref[0].astype(BF16))).astype(BF16)
        acc_ref[...] += _dot(a, w2_ref[0].astype(BF16))

        @pl.when(j == pl.num_programs(1) - 1)
        def _():
            y_ref[...] = acc_ref[...]


def _moe_experts(xs, w1, w2, block_e, n_used, n_blocks):
    bm, fc = MOE_ROWS, MOE_FF
    nff = FF_EXPERT // fc
    used = lambda i, nu: jnp.minimum(i, nu[0] - 1)
    ffi = lambda i, j, nu: jnp.where(i < nu[0], j, nff - 1)
    grid_spec = pltpu.PrefetchScalarGridSpec(
        num_scalar_prefetch=2,
        grid=(n_blocks, nff),
        in_specs=[pl.BlockSpec((bm, D_MODEL), lambda i, j, be, nu: (used(i, nu), 0)),
                  pl.BlockSpec((1, D_MODEL, fc), lambda i, j, be, nu: (be[used(i, nu)], 0, ffi(i, j, nu))),
                  pl.BlockSpec((1, D_MODEL, fc), lambda i, j, be, nu: (be[used(i, nu)], 0, nff + ffi(i, j, nu))),
                  pl.BlockSpec((1, fc, D_MODEL), lambda i, j, be, nu: (be[used(i, nu)], ffi(i, j, nu), 0))],
        out_specs=pl.BlockSpec((bm, D_MODEL), lambda i, j, be, nu: (used(i, nu), 0)),
        scratch_shapes=[pltpu.VMEM((bm, D_MODEL), F32)],
    )
    return pl.pallas_call(
        _moe_kernel,
        grid_spec=grid_spec,
        out_shape=jax.ShapeDtypeStruct((n_blocks * bm, D_MODEL), F32),
        compiler_params=_cparams("arbitrary", "arbitrary"),
        name="moe_experts",
    )(block_e, n_used, xs, w1, w1, w2)


def _combine_kernel(y0_ref, y1_ref, gate_ref, x_ref, modl_ref, g_ref, o_ref):
    gt = gate_ref[...]
    y = gt[:, 0:1] * y0_ref[...] + gt[:, 1:2] * y1_ref[...]
    o_ref[...] = x_ref[...] + modl_ref[0, 5:6, :] * _rms(y, g_ref[...])


def _combine(yk, gates, x2, modl, g, bsz, t, ctx_len):
    tm = ROW_TILE
    nt, nct = t // tm, ctx_len // tm
    nlt = nt - nct
    nl = bsz * (t - ctx_len)
    orow = lambda c: pl.BlockSpec((tm, c), lambda b, i: (b * nlt + i, 0))
    second = pl.BlockSpec((tm, D_MODEL), lambda b, i: (nl // tm + b * nlt + i, 0))
    return pl.pallas_call(
        _combine_kernel,
        grid=(bsz, nlt),
        in_specs=[orow(D_MODEL), second, orow(LANES),
                  pl.BlockSpec((tm, D_MODEL), lambda b, i: (b * nt + nct + i, 0)),
                  pl.BlockSpec((1, 8, D_MODEL), lambda b, i: (b, 0, 0)), _resident((1, D_MODEL))],
        out_specs=orow(D_MODEL),
        out_shape=jax.ShapeDtypeStruct((nl, D_MODEL), F32),
        compiler_params=_cparams("arbitrary", "arbitrary"),
        name="moe_combine",
    )(yk, yk, gates, x2, modl, g)


def _routing_tables(experts, ranks, counts):
    bm = MOE_ROWS
    n_assign = experts.size
    padded = (counts + bm - 1) // bm * bm
    ends_p = jnp.cumsum(padded)
    pstarts = ends_p - padded
    onehot = experts[..., None] == jnp.arange(N_EXPERTS, dtype=I32)
    slot = (jnp.sum(jnp.where(onehot, pstarts, 0), axis=-1) + ranks).astype(I32)
    n_blocks = n_assign // bm + N_EXPERTS
    block_e = jnp.minimum(jnp.sum(ends_p[None, :] <= (jnp.arange(n_blocks, dtype=I32) * bm)[:, None], axis=1),
                          N_EXPERTS - 1).astype(I32)
    n_used = (ends_p[-1] // bm).astype(I32).reshape(1)
    return slot, block_e, n_used, n_blocks


def _row_layout(gcol, ncols, L):
    n = gcol.shape[0]
    return gcol[:, :ncols].reshape(n // L, L, ncols).transpose(0, 2, 1)


def kernel(x, c, ctx, c_ctx, mod_w, mod_b, norm_g, ab_in_w, m_conv_w, m_conv_b, m_gate_b, m_norm_w, l_conv_w,
           l_conv_b, l_wa, l_ba, l_wx, l_bx, l_lam, ab_out_w, ffn_w1, ffn_w2, cd_in_w, g_alpha_w, g_alpha_b,
           g_norm_w, s_conv_w, s_conv_b, s_dt_bias, s_A_log, s_D, s_norm_w, cd_out_w, router_w, router_b,
           moe_w1, moe_w2):
    bsz, seq, dm = x.shape
    ctx_len = ctx.shape[1]
    t = ctx_len + seq
    n = bsz * t
    assert ctx_len == ROW_TILE and seq % ROW_TILE == 0 and ROW_TILE % GRID_W == 0
    x2 = jnp.concatenate([ctx, x], axis=1).reshape(n, dm)

    c_all = jnp.concatenate([c, c_ctx[None, :], jnp.zeros((-(bsz + 1) % 8, dm), F32)], axis=0)
    mods = _modulation(c_all, mod_w, mod_b).reshape(mod_w.shape[0], c_all.shape[0], 6, dm)
    pad2 = jnp.zeros((bsz, 2, dm), F32)

    def layer_mods(layer):
        modl = jnp.concatenate([mods[layer, :bsz], pad2], axis=1)
        modc = jnp.concatenate([mods[layer, bsz], pad2[0]], axis=0)
        return modl, modc

    modl, modc = layer_mods(0)
    g = norm_g[0]
    w = ab_in_w[0]
    o_qk, o_v, o_o, o_gt, o_lx, o_lg = 0, 1024, 2048, 3072, 3088, 4112
    w_in = jnp.concatenate([w[:, o_qk:o_v], w[:, o_v:o_o], w[:, o_o:o_gt], w[:, o_lx:o_lg], w[:, o_lg:],
                            w[:, o_gt:o_lx], jnp.zeros((dm, LANES - 16), F32)], axis=1).astype(BF16)
    gb = jnp.concatenate([m_gate_b[0].reshape(1, 16), jnp.zeros((1, LANES - 16), F32)], axis=1)
    q, k, v, og, lx, glg, gates = _in_ab(x2, modl, modc, g[0:1], w_in, m_conv_w[0], m_conv_b[0][None],
                                         l_conv_w[0], l_conv_b[0][None], gb, bsz, t, ctx_len)
    hm_f, hm_b = _mlstm(q, k, v, gates, _row_layout(gates, 16, M_CHUNK), bsz, t, ctx_len)
    hl_f, hl_b = _lru(lx, l_wa[0].astype(BF16), l_wx[0].astype(BF16), l_ba[0][:, None], l_bx[0][:, None],
                      l_lam[0][:, None], bsz, t, ctx_len)
    x2 = _out_proj(_out_ab_kernel, "out_proj_ab", [hm_f, hm_b, og, hl_f, hl_b, glg], [m_norm_w[0][None]],
                   x2, modl, modc, g[1:2], ab_out_w[0].astype(BF16), bsz, t, ctx_len)
    x2 = _ffn(x2, modl, modc, g[2:3], g[3:4], ffn_w1[0].astype(BF16), ffn_w2[0].astype(BF16), bsz, t, ctx_len)

    modl, modc = layer_mods(1)
    g = norm_g[1]
    w = cd_in_w[0]
    o_gv, o_gr, o_ga, o_z, o_xbc, o_dt = 1024, 2048, 3072, 3104, 4128, 5664
    zpad = jnp.zeros((dm, LANES - 32), F32)
    w_in = jnp.concatenate([w[:, :o_gv], w[:, o_gv:o_gr], w[:, o_gr:o_ga], w[:, o_z:o_xbc], w[:, o_xbc:o_dt],
                            w[:, o_ga:o_z], zpad, w[:, o_dt:], w[:, o_dt:], jnp.zeros((dm, LANES - 64), F32)],
                           axis=1).astype(BF16)
    dtb = jnp.concatenate([s_dt_bias[0].reshape(1, 32), s_dt_bias[0].reshape(1, 32),
                           jnp.zeros((1, LANES - 64), F32)], axis=1)
    aneg = jnp.concatenate([jnp.zeros((1, 32), F32), -jnp.exp(s_A_log[0].reshape(1, 32)),
                            jnp.zeros((1, LANES - 64), F32)], axis=1)
    gq, gk, gv, grs, zs, sx, sb, sc, ga, dts = _in_cd(x2, modl, modc, g[0:1], w_in, s_conv_w[0], s_conv_b[0][None],
                                                      dtb, aneg, bsz, t, ctx_len)
    aw = jnp.zeros((2, LANES, G_HEADS * G_DK), F32)
    aw = aw.at[0, 0:G_RANK].set(g_alpha_w[0, 0]).at[1, G_RANK:2 * G_RANK].set(g_alpha_w[0, 1])
    awh = aw.astype(BF16)
    awl = (aw - awh.astype(F32)).astype(BF16)
    og_f, og_b = _gla(gq, gk, gv, ga, awh, awl, g_alpha_b[0][:, None], bsz, t, ctx_len)
    ys_f, ys_b = _ssd(sx, sb, sc, dts, _row_layout(dts, 4 * S_HEADS, S_CHUNK), bsz, t, ctx_len)
    dskip = jnp.repeat(s_D[0], S_P)[None]
    x2 = _out_proj(_out_cd_kernel, "out_proj_cd", [og_f, og_b, grs, ys_f, ys_b, sx, zs],
                   [g_norm_w[0][None], dskip, s_norm_w[0][None]],
                   x2, modl, modc, g[1:2], cd_out_w[0].astype(BF16), bsz, t, ctx_len)

    rw = jnp.concatenate([router_w[0], jnp.zeros((dm, LANES - N_EXPERTS), F32)], axis=1)
    rb = jnp.concatenate([router_b[0][None], jnp.zeros((1, LANES - N_EXPERTS), F32)], axis=1)
    h2, top_idx, top_gate, counts = _router(x2, modl, g[2:3], rw, rb, bsz, t, ctx_len)
    slot, block_e, n_used, n_blocks = _routing_tables(top_idx[:, 0:TOP_K], top_idx[:, TOP_K:2 * TOP_K],
                                                      counts[0, :N_EXPERTS].astype(I32))
    xs = _dispatch_rows(h2, slot.reshape(-1), n_blocks * MOE_ROWS)
    ys_e = _moe_experts(xs, moe_w1[0], moe_w2[0], block_e, n_used, n_blocks)
    yk = _gather_rows(ys_e, jnp.concatenate([slot[:, 0], slot[:, 1]]))
    out = _combine(yk, top_gate, x2, modl, g[3:4], bsz, t, ctx_len)
    return out.reshape(bsz, seq, dm)
```

```python
import functools

import jax
import jax.numpy as jnp
from jax import lax
from jax.experimental import pallas as pl
from jax.experimental.pallas import tpu as pltpu
from jax.experimental.pallas import tpu_sc as plsc

F32, BF16, I32 = jnp.float32, jnp.bfloat16, jnp.int32

D_MODEL = 1024
GRID_W = 64
EPS = 1e-6
LOG2E = 1.4426950408889634
M_HEADS, M_DK, M_DV = 4, 128, 256
LRU_BLOCKS, LRU_BW, LRU_C = 8, 128, 8.0
G_HEADS, G_DK, G_DV, G_RANK, G_TAU = 4, 128, 256, 16, 16.0
S_HEADS, S_P, S_N, S_GROUPS, S_HPG = 16, 64, 128, 2, 8
FF_DENSE = 2816
N_EXPERTS, TOP_K, FF_EXPERT = 8, 2, 3584

LANES = 128
VMEM_LIMIT = 56 * 1024 * 1024
ROW_TILE = 256
BIG_ROW_TILE_MAX = 640
PROJ_COLS = 256
M_CHUNK = 256
G_CHUNK, G_SUB = 256, 16
S_CHUNK = 256
LRU_TILE = 256
FF_STEP = 256
MOE_ROWS = 1024
MOE_FF = 512
SC_CORES, SC_SUBCORES = 2, 16
SC_LANES = 16
SC_GATHER_ROWS = 128


def _cparams(*sem):
    return pltpu.CompilerParams(dimension_semantics=sem, vmem_limit_bytes=VMEM_LIMIT)


def _silu(x):
    return x * jax.nn.sigmoid(x)


def _softplus(x):
    return jnp.maximum(x, 0.0) + jnp.log1p(jnp.exp(-jnp.abs(x)))


def _log_sigmoid(x):
    return jnp.minimum(x, 0.0) - jnp.log1p(jnp.exp(-jnp.abs(x)))


def _dot(a, b):
    return jnp.dot(a, b, preferred_element_type=F32)


def _dot_nt(a, b):
    return lax.dot_general(a, b, (((1,), (1,)), ((), ())), preferred_element_type=F32)


def _dot_tn(a, b):
    return lax.dot_general(a, b, (((0,), (0,)), ((), ())), preferred_element_type=F32)


def _split3(f):
    f1 = f.astype(BF16)
    r = f - f1.astype(F32)
    f2 = r.astype(BF16)
    f3 = (r - f2.astype(F32)).astype(BF16)
    return f1, f2, f3


def _sel_cols(mask01, f):
    p1, p2, p3 = _split3(f)
    return _dot(mask01, p1) + _dot(mask01, p2) + _dot(mask01, p3)


def _sel_rows(f, mask01):
    p1, p2, p3 = _split3(f)
    return _dot_nt(p1, mask01) + _dot_nt(p2, mask01) + _dot_nt(p3, mask01)


def _expand(f, sel01):
    p1, p2, p3 = _split3(f)
    return _dot(p1, sel01) + _dot(p2, sel01) + _dot(p3, sel01)


def _pack_rows(u):
    n = u.shape[1] // 2
    bits = lax.bitcast_convert_type(u.astype(BF16).astype(F32), jnp.uint32)
    return jnp.bitwise_or(bits[:, :n], jnp.right_shift(bits[:, n:], jnp.uint32(16)))


def _unpack_rows(p):
    hi = lax.bitcast_convert_type(jnp.bitwise_and(p, jnp.uint32(0xFFFF0000)), F32)
    lo = lax.bitcast_convert_type(jnp.left_shift(p, jnp.uint32(16)), F32)
    return jnp.concatenate([hi, lo], axis=1)


def _rms(u, g):
    return u * lax.rsqrt(jnp.mean(u * u, axis=-1, keepdims=True) + EPS) * g


def _head_rms(u, g, n_heads):
    w = u.shape[-1] // n_heads
    return jnp.concatenate([_rms(u[:, h * w:(h + 1) * w], g[:, h * w:(h + 1) * w]) for h in range(n_heads)], axis=1)


def _row_ids(tile_idx, tm):
    return tile_idx * tm + lax.broadcasted_iota(I32, (tm, 1), 0)


def _mod(modl_ref, modc_ref, idx, is_ctx):
    return jnp.where(is_ctx, modc_ref[idx:idx + 1, :], modl_ref[0, idx:idx + 1, :])


def _adaln(x, g, modl_ref, modc_ref, shift_idx, is_ctx):
    shift = _mod(modl_ref, modc_ref, shift_idx, is_ctx)
    scale = _mod(modl_ref, modc_ref, shift_idx + 1, is_ctx)
    return _rms(x, g) * (1.0 + scale) + shift


def _neighbour_tiles(r, step, is_lat):
    n_tiles, per_seg = r.shape[0], GRID_W // 8
    zero = jnp.zeros((1,) + r.shape[1:], r.dtype)

    def across(src):
        return jnp.where(is_lat, 0.0, r[src:src + 1]) if 0 <= src < n_tiles else zero

    pieces = []
    for s0 in range(0, n_tiles, per_seg):
        if step > 0:
            pieces += [across(s0 - 1), r[s0:s0 + per_seg - 1]]
        else:
            pieces += [r[s0 + 1:s0 + per_seg], across(s0 + per_seg)]
    return jnp.concatenate(pieces, axis=0)


def _dwconv(y, cw, cb, is_lat):
    tm, n = y.shape
    y3 = y.reshape(tm // 8, 8, n)
    sub = lax.broadcasted_iota(I32, (1, 8, 1), 1)
    w = lambda j: cw[j:j + 1, :].reshape(1, 1, n)
    r1, r2, r7 = pltpu.roll(y3, 1, 1), pltpu.roll(y3, 2, 1), pltpu.roll(y3, 7, 1)
    ym1 = jnp.where(sub >= 1, r1, _neighbour_tiles(r1, 1, is_lat))
    ym2 = jnp.where(sub >= 2, r2, _neighbour_tiles(r2, 1, is_lat))
    yp1 = jnp.where(sub <= 6, r7, _neighbour_tiles(r7, -1, is_lat))
    out = cb.reshape(1, 1, n) + w(0) * ym2 + w(1) * ym1 + w(2) * y3 + w(3) * yp1
    return out.reshape(tm, n)


def _resident(shape):
    nd = len(shape)
    return pl.BlockSpec(shape, lambda *_: (0,) * nd, pipeline_mode=pl.Buffered(1))


def _big_row_tile(t):
    return max(tm for tm in range(8, BIG_ROW_TILE_MAX + 1, 8) if t % tm == 0)


def _mod_kernel(c_ref, w_ref, b_ref, o_ref):
    a = _silu(c_ref[...])
    o_ref[0] = jnp.dot(a, w_ref[0], preferred_element_type=F32, precision=lax.Precision.HIGHEST) + b_ref[0]


def _modulation(c_all, mod_w, mod_b):
    depth, d, n6 = mod_w.shape
    rows = c_all.shape[0]
    tn = 1024
    return pl.pallas_call(
        _mod_kernel,
        grid=(depth, n6 // tn),
        in_specs=[pl.BlockSpec((rows, d), lambda l, j: (0, 0)),
                  pl.BlockSpec((1, d, tn), lambda l, j: (l, 0, j)),
                  pl.BlockSpec((1, 1, tn), lambda l, j: (l, 0, j))],
        out_specs=pl.BlockSpec((1, rows, tn), lambda l, j: (l, 0, j)),
        out_shape=jax.ShapeDtypeStruct((depth, rows, n6), F32),
        compiler_params=_cparams("arbitrary", "arbitrary"),
        name="modulation",
    )(c_all, mod_w, mod_b.reshape(depth, 1, n6))


def _project(h_ref, w_ref, w_col, out_ref, epilogue):
    for c in range(0, out_ref.shape[1], PROJ_COLS):
        y = _dot(h_ref[...], w_ref[:, w_col + c:w_col + c + PROJ_COLS])
        out_ref[:, c:c + PROJ_COLS] = epilogue(y, c).astype(out_ref.dtype)


def _in_ab_kernel(tm, ctx_len, x_ref, modl_ref, modc_ref, g_ref, w_ref, mcw_ref, mcb_ref, lcw_ref, lcb_ref,
                  gb_ref, q_ref, k_ref, v_ref, og_ref, lx_ref, glg_ref, gates_ref, gates_t_ref, h_scr):
    r = _row_ids(pl.program_id(1), tm)
    is_ctx = r < ctx_len
    h_scr[...] = _adaln(x_ref[...], g_ref[...], modl_ref, modc_ref, 0, is_ctx).astype(BF16)
    is_lat = pl.program_id(1) * tm >= ctx_len

    def conv(cw_ref, cb_ref):
        return lambda y, c: _dwconv(y, cw_ref[:, c:c + PROJ_COLS], cb_ref[:, c:c + PROJ_COLS], is_lat)

    qc, kc, lc = conv(mcw_ref, mcb_ref), conv(mcw_ref, mcb_ref), conv(lcw_ref, lcb_ref)
    _project(h_scr, w_ref, 0, q_ref, lambda y, c: _silu(qc(y, c)))
    _project(h_scr, w_ref, 512, k_ref, lambda y, c: _silu(kc(y, 512 + c)) * (M_DK ** -0.5))
    _project(h_scr, w_ref, 1024, v_ref, lambda y, c: y)
    _project(h_scr, w_ref, 2048, og_ref, lambda y, c: jax.nn.sigmoid(y))
    _project(h_scr, w_ref, 3072, lx_ref, lc)
    _project(h_scr, w_ref, 4096, glg_ref, lambda y, c: jax.nn.gelu(y))
    gt = _dot(h_scr[...], w_ref[:, 5120:5248]) + gb_ref[...]
    lane = lax.broadcasted_iota(I32, gt.shape, 1)
    gates = jnp.where(lane >= 2 * M_HEADS, _log_sigmoid(gt), gt)
    gates_ref[...] = gates
    gates_t_ref[...] = gates.T


def _in_ab(x2, modl, modc, g, w, mcw, mcb, lcw, lcb, gb, bsz, t, ctx_len):
    tm = ROW_TILE
    nt = t // tm
    n = bsz * t
    row = lambda c: pl.BlockSpec((tm, c), lambda b, i: (b * nt + i, 0))
    outs = [(512, BF16), (512, BF16), (1024, BF16), (1024, BF16), (1024, F32), (1024, BF16), (LANES, F32)]
    return pl.pallas_call(
        functools.partial(_in_ab_kernel, tm, ctx_len),
        grid=(bsz, nt),
        in_specs=[row(D_MODEL),
                  pl.BlockSpec((1, 8, D_MODEL), lambda b, i: (b, 0, 0)),
                  _resident((8, D_MODEL)), _resident((1, D_MODEL)), _resident(w.shape),
                  _resident(mcw.shape), _resident(mcb.shape), _resident(lcw.shape), _resident(lcb.shape),
                  _resident(gb.shape)],
        out_specs=[row(c) for c, _ in outs] + [pl.BlockSpec((LANES, tm), lambda b, i: (0, b * nt + i))],
        out_shape=[jax.ShapeDtypeStruct((n, c), dt) for c, dt in outs] + [jax.ShapeDtypeStruct((LANES, n), F32)],
        scratch_shapes=[pltpu.VMEM((tm, D_MODEL), BF16)],
        compiler_params=_cparams("arbitrary", "arbitrary"),
        name="in_proj_ab",
    )(x2, modl, modc, g, w, mcw, mcb, lcw, lcb, gb)


def _reverse_order(n_ctx_chunks, n_chunks):
    return lambda c: jnp.where(c < n_ctx_chunks, n_ctx_chunks - 1 - c, n_chunks - 1 - (c - n_ctx_chunks))


def _tri_mask(L, reverse):
    row = lax.broadcasted_iota(I32, (L, L), 0)
    col = lax.broadcasted_iota(I32, (L, L), 1)
    return (col >= row) if reverse else (col <= row)


def _bidir_scan(kern, L, row_ins, lane_ins, res_ins, out_cols, scratch, bsz, t, ctx_len, name):
    nc = t // L
    rev = _reverse_order(ctx_len // L, nc)
    fwd = lambda i: i
    n = bsz * t

    def row(c, order):
        return pl.BlockSpec((L, c), lambda b, i: (b * nc + order(i), 0))

    def lanes(r, order):
        return pl.BlockSpec((r, L), lambda b, i: (0, b * nc + order(i)))

    in_specs, args = [], []
    for order in (fwd, rev):
        in_specs += [row(a.shape[1], order) for a in row_ins] + [lanes(r, order) for _, r in lane_ins]
        args += list(row_ins) + [a for a, _ in lane_ins]
    in_specs += [_resident(a.shape) for a in res_ins]
    args += list(res_ins)
    return pl.pallas_call(
        kern,
        grid=(bsz, nc),
        in_specs=in_specs,
        out_specs=[row(c, fwd) for c in out_cols] + [row(c, rev) for c in out_cols],
        out_shape=[jax.ShapeDtypeStruct((n, c), BF16) for c in out_cols] * 2,
        scratch_shapes=scratch,
        compiler_params=_cparams("arbitrary", "arbitrary"),
        name=name,
    )(*args)


def _zero_at_start(*scratch):
    @pl.when(pl.program_id(1) == 0)
    def _():
        for s in scratch:
            s[...] = jnp.zeros_like(s)


def _mlstm_kernel(L, qf, kf, vf, gcf, grf, qb, kb, vb, gcb, grb, of, ob, c_scr, m_scr):
    _zero_at_start(c_scr, m_scr)
    ones_col = (lax.broadcasted_iota(I32, (L, LANES), 1) == 0).astype(BF16)
    chains = []
    for d, (q_ref, k_ref, v_ref, gc_ref, gr_ref, out_ref) in enumerate(((qf, kf, vf, gcf, grf, of),
                                                                         (qb, kb, vb, gcb, grb, ob))):
        reverse = d == 1
        mask = _tri_mask(L, reverse)
        mask01 = mask.astype(BF16)
        gc = gc_ref[...]
        gr = gr_ref[...]
        bcol = _sel_cols(mask01, gc)
        brow = _sel_rows(gr, mask01)
        last = 0 if reverse else L - 1
        for h in range(M_HEADS):
            fo, io = 2 * M_HEADS + M_HEADS * d + h, M_HEADS * d + h
            bc, br = bcol[:, fo:fo + 1], brow[fo:fo + 1, :]
            ic, ir = gc[:, io:io + 1], gr[io:io + 1, :]
            m = m_scr[d, h:h + 1, 0:1]
            g = bc + m
            dlog = jnp.where(mask, bc - br + ir, -jnp.inf)
            mt = jnp.maximum(g, jnp.max(dlog, axis=1, keepdims=True))
            bl = bc[last:last + 1, :]
            a_s = bl - bc + ic
            m_new = jnp.maximum(bl + m, jnp.max(a_s, axis=0, keepdims=True))
            chains.append(dict(
                d=d, h=h, out_ref=out_ref, mt=mt, w_inter=jnp.exp(g - mt), p=jnp.exp(dlog - mt),
                m_new=m_new, decay=jnp.exp(bl + m - m_new), ws=jnp.exp(a_s - m_new),
                qh=q_ref[:, h * M_DK:(h + 1) * M_DK], kh=k_ref[:, h * M_DK:(h + 1) * M_DK],
                vx=jnp.concatenate([v_ref[:, h * M_DV:(h + 1) * M_DV], ones_col], axis=1)))
    for ch in chains:
        ch["sc"] = (_dot_nt(ch["qh"], ch["kh"]) * ch["p"]).astype(BF16)
    for ch in chains:
        cx = c_scr[ch["d"], ch["h"]]
        numx = ch["w_inter"] * _dot(ch["qh"], cx.astype(BF16)) + _dot(ch["sc"], ch["vx"])
        den = numx[:, M_DV:M_DV + 1]
        hh = numx[:, 0:M_DV] * (1.0 / jnp.maximum(jnp.abs(den), jnp.exp(-ch["mt"])))
        ch["out_ref"][:, ch["h"] * M_DV:(ch["h"] + 1) * M_DV] = hh.astype(BF16)
    for ch in chains:
        d, h = ch["d"], ch["h"]
        kw = (ch["kh"].astype(F32) * ch["ws"]).astype(BF16)
        c_scr[d, h] = ch["decay"] * c_scr[d, h] + _dot_tn(kw, ch["vx"])
        m_scr[d, h:h + 1, 0:1] = ch["m_new"]


def _mlstm(q, k, v, gcol, grow, bsz, t, ctx_len):
    L = M_CHUNK
    scratch = [pltpu.VMEM((2, M_HEADS, M_DK, M_DV + LANES), F32), pltpu.VMEM((2, 8, LANES), F32)]
    return _bidir_scan(functools.partial(_mlstm_kernel, L), L, [q, k, v, gcol], [(grow, 16)], [], [1024], scratch,
                       bsz, t, ctx_len, "mlstm_scan")


def _lru_gates(L, d, u_ref, wa_ref, wx_ref, ba_ref, bx_ref, lam_ref, a_scr, b_scr):
    reverse = d == 1
    u = u_ref[...]
    ub = u.astype(BF16)
    sp = _softplus(-lam_ref[d])
    for n in range(LRU_BLOCKS):
        sl = slice(n * LRU_BW, (n + 1) * LRU_BW)
        rg = jax.nn.sigmoid(_dot(ub[:, sl], wa_ref[d, n]) + ba_ref[d, :, sl])
        ig = jax.nn.sigmoid(_dot(ub[:, sl], wx_ref[d, n]) + bx_ref[d, :, sl])
        log_a = -LRU_C * rg * sp[:, sl]
        a = jnp.exp(log_a)
        a_scr[d, :, sl] = a
        b_scr[d, :, sl] = jnp.sqrt(-jnp.tanh(log_a) * (a * a + 1.0)) * (ig * u[:, sl])
    a = a_scr[d].reshape(L // 8, 8, LRU_BLOCKS * LRU_BW)
    b = b_scr[d].reshape(L // 8, 8, LRU_BLOCKS * LRU_BW)
    sub = lax.broadcasted_iota(I32, (1, 8, 1), 1)
    for s in (1, 2, 4):
        if reverse:
            a_sh, b_sh, keep = pltpu.roll(a, 8 - s, 1), pltpu.roll(b, 8 - s, 1), sub < 8 - s
        else:
            a_sh, b_sh, keep = pltpu.roll(a, s, 1), pltpu.roll(b, s, 1), sub >= s
        b = jnp.where(keep, a * b_sh + b, b)
        a = jnp.where(keep, a * a_sh, a)
    a_scr[d] = a.reshape(L, LRU_BLOCKS * LRU_BW)
    b_scr[d] = b.reshape(L, LRU_BLOCKS * LRU_BW)


def _lru_kernel(L, uf, ub, wa_ref, wx_ref, ba_ref, bx_ref, lam_ref, of, ob, a_scr, b_scr, h_scr):
    _zero_at_start(h_scr)
    _lru_gates(L, 0, uf, wa_ref, wx_ref, ba_ref, bx_ref, lam_ref, a_scr, b_scr)
    _lru_gates(L, 1, ub, wa_ref, wx_ref, ba_ref, bx_ref, lam_ref, a_scr, b_scr)
    ng = L // 8

    def body(j, carry):
        hf, hb = carry
        rf = pl.ds(pl.multiple_of(j * 8, 8), 8)
        rb = pl.ds(pl.multiple_of((ng - 1 - j) * 8, 8), 8)
        gf = b_scr[0, rf, :] + a_scr[0, rf, :] * hf
        gb = b_scr[1, rb, :] + a_scr[1, rb, :] * hb
        b_scr[0, rf, :] = gf
        b_scr[1, rb, :] = gb
        return gf[7:8, :], gb[0:1, :]

    hf, hb = lax.fori_loop(0, ng, body, (h_scr[0:1, :], h_scr[1:2, :]))
    h_scr[0:1, :] = hf
    h_scr[1:2, :] = hb
    of[...] = b_scr[0].astype(BF16)
    ob[...] = b_scr[1].astype(BF16)


def _lru(u, wa, wx, ba, bx, lam, bsz, t, ctx_len):
    L = LRU_TILE
    scratch = [pltpu.VMEM((2, L, 1024), F32), pltpu.VMEM((2, L, 1024), F32), pltpu.VMEM((8, 1024), F32)]
    return _bidir_scan(functools.partial(_lru_kernel, L), L, [u], [], [wa, wx, ba, bx, lam], [1024], scratch,
                       bsz, t, ctx_len, "rglru_scan")


def _out_tail(tm, ctx_len, a1, a2, x_ref, modl_ref, modc_ref, g_ref, w_ref, xo_ref):
    is_ctx = _row_ids(pl.program_id(1), tm) < ctx_len
    half = w_ref.shape[0] // 2
    y = _dot(a1.astype(BF16), w_ref[0:half, :]) + _dot(a2.astype(BF16), w_ref[half:, :])
    xo_ref[...] = x_ref[...] + _mod(modl_ref, modc_ref, 2, is_ctx) * _rms(y, g_ref[...])


def _out_ab_kernel(tm, ctx_len, hf, hb, og, lf, lb, glg, nw, x_ref, modl_ref, modc_ref, g_ref, w_ref, xo_ref):
    hm = _head_rms(hf[...].astype(F32) + hb[...].astype(F32), nw[...], M_HEADS) * og[...].astype(F32)
    hl = (lf[...].astype(F32) + lb[...].astype(F32)) * glg[...].astype(F32)
    _out_tail(tm, ctx_len, hm, hl, x_ref, modl_ref, modc_ref, g_ref, w_ref, xo_ref)


def _out_cd_kernel(tm, ctx_len, gf, gb, grs, yf, yb, sx, zs, gnw, dsk, snw, x_ref, modl_ref, modc_ref, g_ref,
                   w_ref, xo_ref):
    og = _head_rms(gf[...].astype(F32) + gb[...].astype(F32), gnw[...], G_HEADS) * grs[...].astype(F32)
    ys = yf[...].astype(F32) + yb[...].astype(F32) + dsk[...] * sx[...].astype(F32)
    ys = _rms(ys * zs[...].astype(F32), snw[...])
    _out_tail(tm, ctx_len, og, ys, x_ref, modl_ref, modc_ref, g_ref, w_ref, xo_ref)


def _out_proj(kern, name, acts, vecs, x2, modl, modc, g, w, bsz, t, ctx_len):
    tm = _big_row_tile(t)
    nt = t // tm
    row = lambda c: pl.BlockSpec((tm, c), lambda b, i: (b * nt + i, 0))
    return pl.pallas_call(
        functools.partial(kern, tm, ctx_len),
        grid=(bsz, nt),
        in_specs=[row(1024)] * len(acts) + [_resident((1, 1024))] * len(vecs)
        + [row(D_MODEL), pl.BlockSpec((1, 8, D_MODEL), lambda b, i: (b, 0, 0)),
           _resident((8, D_MODEL)), _resident((1, D_MODEL)), _resident(w.shape)],
        out_specs=row(D_MODEL),
        out_shape=jax.ShapeDtypeStruct(x2.shape, F32),
        compiler_params=_cparams("arbitrary", "arbitrary"),
        name=name,
    )(*acts, *vecs, x2, modl, modc, g, w)


def _ffn_kernel(tm, ctx_len, x_ref, modl_ref, modc_ref, g2_ref, g3_ref, w1_ref, w2_ref, xo_ref):
    is_ctx = _row_ids(pl.program_id(1), tm) < ctx_len
    x = x_ref[...]
    h = _adaln(x, g2_ref[...], modl_ref, modc_ref, 3, is_ctx).astype(BF16)
    ff = w2_ref.shape[0]
    acc = jnp.zeros((tm, D_MODEL), F32)
    for j in range(ff // FF_STEP):
        lo = j * FF_STEP
        gj = _dot(h, w1_ref[:, lo:lo + FF_STEP])
        uj = _dot(h, w1_ref[:, ff + lo:ff + lo + FF_STEP])
        acc = acc + _dot((_silu(gj) * uj).astype(BF16), w2_ref[lo:lo + FF_STEP, :])
    xo_ref[...] = x + _mod(modl_ref, modc_ref, 5, is_ctx) * _rms(acc, g3_ref[...])


def _ffn(x2, modl, modc, g2, g3, w1, w2, bsz, t, ctx_len):
    tm = _big_row_tile(t)
    nt = t // tm
    row = lambda c: pl.BlockSpec((tm, c), lambda b, i: (b * nt + i, 0))
    return pl.pallas_call(
        functools.partial(_ffn_kernel, tm, ctx_len),
        grid=(bsz, nt),
        in_specs=[row(D_MODEL), pl.BlockSpec((1, 8, D_MODEL), lambda b, i: (b, 0, 0)), _resident((8, D_MODEL)),
                  _resident((1, D_MODEL)), _resident((1, D_MODEL)), _resident(w1.shape), _resident(w2.shape)],
        out_specs=row(D_MODEL),
        out_shape=jax.ShapeDtypeStruct(x2.shape, F32),
        compiler_params=_cparams("arbitrary", "arbitrary"),
        name="dense_swiglu",
    )(x2, modl, modc, g2, g3, w1, w2)


def _in_cd_kernel(tm, ctx_len, x_ref, modl_ref, modc_ref, g_ref, w_ref, cw_ref, cb_ref, dtb_ref, aneg_ref,
                  gq_ref, gk_ref, gv_ref, gr_ref, z_ref, sx_ref, sb_ref, sc_ref, ga_ref, dts_ref, dts_t_ref, h_scr):
    r = _row_ids(pl.program_id(1), tm)
    is_ctx = r < ctx_len
    h_scr[...] = _adaln(x_ref[...], g_ref[...], modl_ref, modc_ref, 0, is_ctx).astype(BF16)
    is_lat = pl.program_id(1) * tm >= ctx_len

    def conv_silu(col0):
        return lambda y, c: _silu(_dwconv(y, cw_ref[:, col0 + c:col0 + c + PROJ_COLS],
                                          cb_ref[:, col0 + c:col0 + c + PROJ_COLS], is_lat))

    _project(h_scr, w_ref, 0, gq_ref, lambda y, c: y * (G_DK ** -0.5))
    _project(h_scr, w_ref, 512, gk_ref, lambda y, c: y)
    _project(h_scr, w_ref, 1024, gv_ref, lambda y, c: y)
    _project(h_scr, w_ref, 2048, gr_ref, lambda y, c: _silu(y))
    _project(h_scr, w_ref, 3072, z_ref, lambda y, c: _silu(y))
    _project(h_scr, w_ref, 4096, sx_ref, conv_silu(0))
    _project(h_scr, w_ref, 5120, sb_ref, conv_silu(1024))
    _project(h_scr, w_ref, 5376, sc_ref, conv_silu(1280))
    ga_ref[...] = _dot(h_scr[...], w_ref[:, 5632:5760])
    dt = _softplus(_dot(h_scr[...], w_ref[:, 5760:5888]) + dtb_ref[...])
    lane = lax.broadcasted_iota(I32, dt.shape, 1)
    dts = jnp.where(lane < 2 * S_HEADS, dt, dt * aneg_ref[...])
    dts_ref[...] = dts
    dts_t_ref[...] = dts.T


def _in_cd(x2, modl, modc, g, w, cw, cb, dtb, aneg, bsz, t, ctx_len):
    tm = ROW_TILE
    nt = t // tm
    n = bsz * t
    row = lambda c: pl.BlockSpec((tm, c), lambda b, i: (b * nt + i, 0))
    outs = [(512, BF16), (512, BF16), (1024, BF16), (1024, BF16), (1024, BF16), (1024, BF16), (256, BF16),
            (256, BF16), (LANES, F32), (LANES, F32)]
    return pl.pallas_call(
        functools.partial(_in_cd_kernel, tm, ctx_len),
        grid=(bsz, nt),
        in_specs=[row(D_MODEL), pl.BlockSpec((1, 8, D_MODEL), lambda b, i: (b, 0, 0)), _resident((8, D_MODEL)),
                  _resident((1, D_MODEL)), _resident(w.shape), _resident(cw.shape), _resident(cb.shape),
                  _resident(dtb.shape), _resident(aneg.shape)],
        out_specs=[row(c) for c, _ in outs] + [pl.BlockSpec((LANES, tm), lambda b, i: (0, b * nt + i))],
        out_shape=[jax.ShapeDtypeStruct((n, c), dt) for c, dt in outs] + [jax.ShapeDtypeStruct((LANES, n), F32)],
        scratch_shapes=[pltpu.VMEM((tm, D_MODEL), BF16)],
        compiler_params=_cparams("arbitrary", "arbitrary"),
        name="in_proj_cd",
    )(x2, modl, modc, g, w, cw, cb, dtb, aneg)


def _gla_kernel(L, C, qf, kf, vf, gaf, qb, kb, vb, gab, awh_ref, awl_ref, ab_ref, of, ob, s_scr):
    _zero_at_start(s_scr)
    nb = L // C
    chains = []
    for d, (q_ref, k_ref, v_ref, ga_ref, out_ref) in enumerate(((qf, kf, vf, gaf, of), (qb, kb, vb, gab, ob))):
        reverse = d == 1
        mask = _tri_mask(L, reverse)
        g1, g2, _ = _split3(ga_ref[...])
        pre = _dot(g1, awh_ref[d]) + _dot(g2, awh_ref[d]) + _dot(g1, awl_ref[d])
        lg = _log_sigmoid(pre + ab_ref[d]) * (1.0 / G_TAU)
        ball = _sel_cols(mask.astype(BF16), lg)
        for h in range(G_HEADS):
            ks = slice(h * G_DK, (h + 1) * G_DK)
            b = ball[:, ks] * LOG2E
            qh = q_ref[:, ks].astype(F32)
            inter = _dot_nt((qh * jnp.exp2(b)).astype(BF16), s_scr[d, h].astype(BF16))
            chains.append(dict(d=d, h=h, reverse=reverse, mask=mask, b=b, qh=qh, kh=k_ref[:, ks].astype(F32),
                               vh=v_ref[:, h * G_DV:(h + 1) * G_DV], inter=inter, out_ref=out_ref, blocks=[]))
    for i in range(nb):
        lo, hi = i * C, (i + 1) * C
        for ch in chains:
            b = ch["b"]
            if ch["reverse"]:
                bref = b[hi:hi + 1, :] if i < nb - 1 else jnp.zeros((1, G_DK), F32)
                v0, v1 = lo // LANES * LANES, L
            else:
                bref = b[lo - 1:lo, :] if i > 0 else jnp.zeros((1, G_DK), F32)
                v0, v1 = 0, min(L, -(-hi // LANES) * LANES)
            qi = (ch["qh"][lo:hi, :] * jnp.exp2(b[lo:hi, :] - bref)).astype(BF16)
            ki = (ch["kh"][v0:v1] * jnp.exp2(bref - b[v0:v1])).astype(BF16)
            pieces = [jnp.zeros((C, v0), F32)] if v0 else []
            pieces.append(_dot_nt(qi, ki))
            if v1 < L:
                pieces.append(jnp.zeros((C, L - v1), F32))
            ch["blocks"].append(jnp.concatenate(pieces, axis=1) if len(pieces) > 1 else pieces[0])
    for ch in chains:
        att = jnp.where(ch["mask"], jnp.concatenate(ch["blocks"], axis=0), 0.0).astype(BF16)
        oh = ch["inter"] + _dot(att, ch["vh"])
        ch["out_ref"][:, ch["h"] * G_DV:(ch["h"] + 1) * G_DV] = oh.astype(BF16)
    for ch in chains:
        d, h, b = ch["d"], ch["h"], ch["b"]
        last = 0 if ch["reverse"] else L - 1
        bl = b[last:last + 1, :]
        kd = (ch["kh"] * jnp.exp2(bl - b)).astype(BF16)
        s_scr[d, h] = s_scr[d, h] * jnp.exp2(bl) + _dot_tn(ch["vh"], kd)


def _gla(q, k, v, ga, awh, awl, ab, bsz, t, ctx_len):
    L, C = G_CHUNK, G_SUB
    scratch = [pltpu.VMEM((2, G_HEADS, G_DV, G_DK), F32)]
    return _bidir_scan(functools.partial(_gla_kernel, L, C), L, [q, k, v, ga], [], [awh, awl, ab], [1024], scratch,
                       bsz, t, ctx_len, "gla_scan")


def _ssd_kernel(L, xf, bf, cf, gcf, grf, xb, bb, cb_, gcb, grb, of, ob, s_scr):
    _zero_at_start(s_scr)
    er = lax.broadcasted_iota(I32, (LANES, S_HEADS * S_P), 0)
    ec = jnp.right_shift(lax.broadcasted_iota(I32, (LANES, S_HEADS * S_P), 1), 6)
    glane = lax.broadcasted_iota(I32, (L, LANES), 1)
    lane = lax.broadcasted_iota(I32, (L, 2 * S_P), 1)
    dirs, chains = [], []
    for d, (x_ref, b_ref, c_ref, gc_ref, gr_ref, out_ref) in enumerate(((xf, bf, cf, gcf, grf, of),
                                                                         (xb, bb, cb_, gcb, grb, ob))):
        reverse = d == 1
        mask = _tri_mask(L, reverse)
        mask01 = mask.astype(BF16)
        gc = gc_ref[...]
        gr = gr_ref[...]
        bcol = _sel_cols(mask01, gc)
        brow = _sel_rows(gr, mask01)
        last = 0 if reverse else L - 1
        dt_o, la_o = S_HEADS * d, 2 * S_HEADS + S_HEADS * d
        sel_la = (er == ec + la_o).astype(BF16)
        x = x_ref[...]
        bla = jnp.where((glane >= la_o) & (glane < la_o + S_HEADS), bcol, 0.0)
        dt_at_la = pltpu.roll(gc, 2 * S_HEADS, 1)
        bl = bla[last:last + 1, :]
        ebx = _dot(jnp.exp(bla).astype(BF16), sel_la)
        wx = _dot((jnp.exp(bl - bla) * dt_at_la).astype(BF16), sel_la)
        dirs.append(dict(d=d, mask=mask, gr=gr, x=x, ebx=ebx, last=last, dt_o=dt_o, la_o=la_o, out_ref=out_ref,
                         xw=(x.astype(F32) * wx).astype(BF16),
                         bcol2=bcol * LOG2E, brow2=brow * LOG2E))
        for g in range(S_GROUPS):
            ns = slice(g * S_N, (g + 1) * S_N)
            chains.append(dict(dr=dirs[-1], g=g, bg=b_ref[:, ns], cg=c_ref[:, ns],
                               gs=slice(g * S_HPG * S_P, (g + 1) * S_HPG * S_P)))
    for ch in chains:
        dr = ch["dr"]
        ch["cb"] = _dot_nt(ch["cg"], ch["bg"])
        ch["inter"] = _dot(ch["cg"], s_scr[dr["d"], ch["g"]].astype(BF16)) * dr["ebx"][:, ch["gs"]]
    for ch in chains:
        dr, g = ch["dr"], ch["g"]
        parts = []
        for pr in range(S_HPG // 2):
            e0 = g * S_HPG + 2 * pr
            xp = dr["x"][:, e0 * S_P:(e0 + 2) * S_P]
            res = []
            for e in (e0, e0 + 1):
                col = dr["la_o"] + e
                bc, br = dr["bcol2"][:, col:col + 1], dr["brow2"][col:col + 1, :]
                dtr = dr["gr"][dr["dt_o"] + e:dr["dt_o"] + e + 1, :]
                att = (ch["cb"] * dtr * jnp.exp2(jnp.where(dr["mask"], bc - br, -jnp.inf))).astype(BF16)
                res.append(_dot(att, xp))
            parts.append(jnp.where(lane < S_P, res[0], res[1]))
        dr["out_ref"][:, ch["gs"]] = (jnp.concatenate(parts, axis=1) + ch["inter"]).astype(BF16)
    for ch in chains:
        dr, g, gs = ch["dr"], ch["g"], ch["gs"]
        s_scr[dr["d"], g] = (s_scr[dr["d"], g] * dr["ebx"][dr["last"]:dr["last"] + 1, gs]
                             + _dot_tn(ch["bg"], dr["xw"][:, gs]))


def _ssd(x, bm, cm, gcol, grow, bsz, t, ctx_len):
    L = S_CHUNK
    scratch = [pltpu.VMEM((2, S_GROUPS, S_N, S_HPG * S_P), F32)]
    return _bidir_scan(functools.partial(_ssd_kernel, L), L, [x, bm, cm, gcol], [(grow, 4 * S_HEADS)], [], [1024],
                       scratch,
                       bsz, t, ctx_len, "ssd_scan")


def _router_kernel(x_ref, modl_ref, g_ref, rw_ref, rb_ref, h_ref, idx_ref, gate_ref, cnt_ref, base_scr):
    @pl.when((pl.program_id(0) == 0) & (pl.program_id(1) == 0))
    def _():
        base_scr[...] = jnp.zeros_like(base_scr)

    h = _rms(x_ref[...], g_ref[...]) * (1.0 + modl_ref[0, 4:5, :]) + modl_ref[0, 3:4, :]
    h_ref[...] = _pack_rows(h)
    logits =jnp.dot(h, rw_ref[...], preferred_element_type=F32, precision=lax.Precision.HIGHEST) + rb_ref[...]
    lane = lax.broadcasted_iota(I32, logits.shape, 1)
    logits = jnp.where(lane < N_EXPERTS, logits, -jnp.inf)
    lanef = lane.astype(F32)
    m1 = jnp.max(logits, axis=1, keepdims=True)
    i1 = jnp.min(jnp.where(logits == m1, lanef, float(LANES)), axis=1, keepdims=True)
    rest = jnp.where(lanef == i1, -jnp.inf, logits)
    m2 = jnp.max(rest, axis=1, keepdims=True)
    i2 = jnp.min(jnp.where(rest == m2, lanef, float(LANES)), axis=1, keepdims=True)
    e = jnp.exp(m2 - m1)
    g1 = 1.0 / (1.0 + e)
    gate_ref[...] = jnp.where(lane == 0, g1, jnp.where(lane == 1, e * g1, 0.0))
    tm = logits.shape[0]
    oh1, oh2 = (lanef == i1).astype(F32), (lanef == i2).astype(F32)
    earlier = (lax.broadcasted_iota(I32, (tm, tm), 1) < lax.broadcasted_iota(I32, (tm, tm), 0)).astype(BF16)
    seen = base_scr[0:1, :] + _dot(earlier, (oh1 + oh2).astype(BF16))
    r1 = jnp.sum(oh1 * seen, axis=1, keepdims=True)
    r2 = jnp.sum(oh2 * seen, axis=1, keepdims=True)
    base_scr[0:1, :] = base_scr[0:1, :] + jnp.sum(oh1 + oh2, axis=0, keepdims=True)
    cnt_ref[...] = base_scr[...]
    idx_ref[...] = jnp.where(lane == 0, i1, jnp.where(lane == 1, i2, jnp.where(lane == 2, r1, jnp.where(
        lane == 3, r2, 0.0)))).astype(I32)


def _router(x2, modl, g, rw, rb, bsz, t, ctx_len):
    tm = ROW_TILE
    nt, nct = t // tm, ctx_len // tm
    nl = bsz * (t - ctx_len)
    nlt = nt - nct
    orow = lambda c: pl.BlockSpec((tm, c), lambda b, i: (b * nlt + i, 0))
    return pl.pallas_call(
        _router_kernel,
        grid=(bsz, nlt),
        in_specs=[pl.BlockSpec((tm, D_MODEL), lambda b, i: (b * nt + nct + i, 0)),
                  pl.BlockSpec((1, 8, D_MODEL), lambda b, i: (b, 0, 0)),
                  _resident((1, D_MODEL)), _resident(rw.shape), _resident(rb.shape)],
        out_specs=[orow(D_MODEL // 2), orow(LANES), orow(LANES), pl.BlockSpec((8, LANES), lambda b, i: (0, 0))],
        out_shape=[jax.ShapeDtypeStruct((nl, D_MODEL // 2), jnp.uint32), jax.ShapeDtypeStruct((nl, LANES), I32),
                   jax.ShapeDtypeStruct((nl, LANES), F32), jax.ShapeDtypeStruct((8, LANES), F32)],
        scratch_shapes=[pltpu.VMEM((8, LANES), F32)],
        compiler_params=_cparams("arbitrary", "arbitrary"),
        name="moe_router",
    )(x2, modl, g, rw, rb)


def _gather_rows(table, idx):
    n_rows, d = idx.shape[0], table.shape[1]
    n_workers = SC_CORES * SC_SUBCORES
    per_worker = n_rows // n_workers
    chunk = SC_GATHER_ROWS
    assert per_worker * n_workers == n_rows and per_worker % chunk == 0
    mesh = plsc.VectorSubcoreMesh(core_axis_name="c", subcore_axis_name="s")

    @functools.partial(
        pl.kernel, mesh=mesh, out_type=jax.ShapeDtypeStruct((n_rows, d), table.dtype),
        scratch_types=[pltpu.VMEM((chunk,), I32), pltpu.VMEM((chunk, d), table.dtype), pltpu.SemaphoreType.DMA],
        name="sc_gather_rows")
    def gather(table_hbm, idx_hbm, out_hbm, idx_v, rows_v, sem):
        base = (lax.axis_index("s") * SC_CORES + lax.axis_index("c")) * per_worker

        @pl.loop(0, per_worker // chunk)
        def _(j):
            off = pl.multiple_of(base + j * chunk, 8)
            pltpu.sync_copy(idx_hbm.at[pl.ds(off, chunk)], idx_v)
            pltpu.async_copy(table_hbm.at[idx_v], rows_v, sem).wait()
            pltpu.sync_copy(rows_v, out_hbm.at[pl.ds(off, chunk)])

    return gather(table, idx)


def _dispatch_rows(table, slot, n_out):
    assert TOP_K == 2
    n_assign, (n_tok, d) = slot.shape[0], table.shape
    n_workers = SC_CORES * SC_SUBCORES
    per_worker = n_out // n_workers
    chunk, vec = SC_GATHER_ROWS, SC_LANES
    assert per_worker * n_workers == n_out and per_worker % chunk == 0 and n_assign % vec == 0
    assert n_out < 3 * n_tok
    mesh = plsc.VectorSubcoreMesh(core_axis_name="c", subcore_axis_name="s")

    @functools.partial(
        pl.kernel, mesh=mesh, out_type=jax.ShapeDtypeStruct((n_out, d), table.dtype),
        scratch_types=[pltpu.VMEM((n_assign,), I32), pltpu.VMEM((per_worker,), I32),
                       pltpu.VMEM((chunk, d), table.dtype), pltpu.SemaphoreType.DMA],
        compiler_params=pltpu.CompilerParams(needs_layout_passes=False),
        name="sc_dispatch_rows")
    def dispatch(table_hbm, slot_hbm, out_hbm, slots_v, tok_v, rows_v, sem):
        base = (lax.axis_index("s") * SC_CORES + lax.axis_index("c")) * per_worker
        pltpu.sync_copy(slot_hbm, slots_v)
        lane = lax.broadcasted_iota(I32, (vec,), 0)

        @pl.loop(0, per_worker // vec)
        def _(j):
            p = base + j * vec + lane
            p = jnp.where(p >= n_tok, p - n_tok, p)
            tok_v[pl.ds(j * vec, vec)] = jnp.where(p >= n_tok, p - n_tok, p)

        @pl.loop(0, n_assign // vec)
        def _(j):
            s = slots_v[pl.ds(j * vec, vec)] - base
            mine = (s >= 0) & (s < per_worker)
            tok = lax.shift_right_logical(j * vec + lane, 1)
            plsc.store_scatter(tok_v, [jnp.where(mine, s, 0)], tok, mask=mine)

        @pl.loop(0, per_worker // chunk)
        def _(j):
            off = pl.multiple_of(j * chunk, 8)
            pltpu.async_copy(table_hbm.at[tok_v.at[pl.ds(off, chunk)]], rows_v, sem).wait()
            pltpu.sync_copy(rows_v, out_hbm.at[pl.ds(base + off, chunk)])

    return dispatch(table, slot)


def _moe_kernel(be_ref, nu_ref, x_ref, w1g_ref, w1u_ref, w2_ref, y_ref, acc_ref):
    i, j = pl.program_id(0), pl.program_id(1)

    @pl.when(i < nu_ref[0])
    def _():
        @pl.when(j == 0)
        def _():
            acc_ref[...] = jnp.zeros_like(acc_ref)

        x = _unpack_rows(x_ref[...]).astype(BF16)
        a = (_silu(_dot(x, w1g_ref[0].astype(BF16))) * _dot(x, w1u_ref[0].astype(BF16))).astype(BF16)
        acc_ref[...] += _dot(a, w2_ref[0].astype(BF16))

        @pl.when(j == pl.num_programs(1) - 1)
        def _():
            y_ref[...] = _pack_rows(acc_ref[...])


def _moe_experts(xs, w1, w2, block_e, n_used, n_blocks):
    bm, fc = MOE_ROWS, MOE_FF
    nff = FF_EXPERT // fc
    used = lambda i, nu: jnp.minimum(i, nu[0] - 1)
    ffi = lambda i, j, nu: jnp.where(i < nu[0], j, nff - 1)
    grid_spec = pltpu.PrefetchScalarGridSpec(
        num_scalar_prefetch=2,
        grid=(n_blocks, nff),
        in_specs=[pl.BlockSpec((bm, D_MODEL // 2), lambda i, j, be, nu: (used(i, nu), 0)),
                  pl.BlockSpec((1, D_MODEL, fc), lambda i, j, be, nu: (be[used(i, nu)], 0, ffi(i, j, nu))),
                  pl.BlockSpec((1, D_MODEL, fc), lambda i, j, be, nu: (be[used(i, nu)], 0, nff + ffi(i, j, nu))),
                  pl.BlockSpec((1, fc, D_MODEL), lambda i, j, be, nu: (be[used(i, nu)], ffi(i, j, nu), 0))],
        out_specs=pl.BlockSpec((bm, D_MODEL // 2), lambda i, j, be, nu: (used(i, nu), 0)),
        scratch_shapes=[pltpu.VMEM((bm, D_MODEL), F32)],
    )
    return pl.pallas_call(
        _moe_kernel,
        grid_spec=grid_spec,
        out_shape=jax.ShapeDtypeStruct((n_blocks * bm, D_MODEL // 2), jnp.uint32),
        compiler_params=_cparams("arbitrary", "arbitrary"),
        name="moe_experts",
    )(block_e, n_used, xs, w1, w1, w2)


def _combine_kernel(y0_ref, y1_ref, gate_ref, x_ref, modl_ref, g_ref, o_ref):
    gt = gate_ref[...]
    y = gt[:, 0:1] * _unpack_rows(y0_ref[...]) + gt[:, 1:2] * _unpack_rows(y1_ref[...])
    o_ref[...] = x_ref[...] + modl_ref[0, 5:6, :] * _rms(y, g_ref[...])


def _combine(yk, gates, x2, modl, g, bsz, t, ctx_len):
    tm = ROW_TILE
    nt, nct = t // tm, ctx_len // tm
    nlt = nt - nct
    nl = bsz * (t - ctx_len)
    orow = lambda c: pl.BlockSpec((tm, c), lambda b, i: (b * nlt + i, 0))
    second = pl.BlockSpec((tm, D_MODEL // 2), lambda b, i: (nl // tm + b * nlt + i, 0))
    return pl.pallas_call(
        _combine_kernel,
        grid=(bsz, nlt),
        in_specs=[orow(D_MODEL // 2), second, orow(LANES),
                  pl.BlockSpec((tm, D_MODEL), lambda b, i: (b * nt + nct + i, 0)),
                  pl.BlockSpec((1, 8, D_MODEL), lambda b, i: (b, 0, 0)), _resident((1, D_MODEL))],
        out_specs=orow(D_MODEL),
        out_shape=jax.ShapeDtypeStruct((nl, D_MODEL), F32),
        compiler_params=_cparams("arbitrary", "arbitrary"),
        name="moe_combine",
    )(yk, yk, gates, x2, modl, g)


def _routing_tables(experts, ranks, counts):
    bm = MOE_ROWS
    n_assign = experts.size
    padded = (counts + bm - 1) // bm * bm
    ends_p = jnp.cumsum(padded)
    pstarts = ends_p - padded
    onehot = experts[..., None] == jnp.arange(N_EXPERTS, dtype=I32)
    slot = (jnp.sum(jnp.where(onehot, pstarts, 0), axis=-1) + ranks).astype(I32)
    n_blocks = n_assign // bm + N_EXPERTS
    block_e = jnp.minimum(jnp.sum(ends_p[None, :] <= (jnp.arange(n_blocks, dtype=I32) * bm)[:, None], axis=1),
                          N_EXPERTS - 1).astype(I32)
    n_used = (ends_p[-1] // bm).astype(I32).reshape(1)
    return slot, block_e, n_used, n_blocks


def kernel(x, c, ctx, c_ctx, mod_w, mod_b, norm_g, ab_in_w, m_conv_w, m_conv_b, m_gate_b, m_norm_w, l_conv_w,
           l_conv_b, l_wa, l_ba, l_wx, l_bx, l_lam, ab_out_w, ffn_w1, ffn_w2, cd_in_w, g_alpha_w, g_alpha_b,
           g_norm_w, s_conv_w, s_conv_b, s_dt_bias, s_A_log, s_D, s_norm_w, cd_out_w, router_w, router_b,
           moe_w1, moe_w2):
    bsz, seq, dm = x.shape
    ctx_len = ctx.shape[1]
    t = ctx_len + seq
    n = bsz * t
    assert ctx_len == ROW_TILE and seq % ROW_TILE == 0 and ROW_TILE % GRID_W == 0
    x2 = jnp.concatenate([ctx, x], axis=1).reshape(n, dm)

    c_all = jnp.concatenate([c, c_ctx[None, :], jnp.zeros((-(bsz + 1) % 8, dm), F32)], axis=0)
    mods = _modulation(c_all, mod_w, mod_b).reshape(mod_w.shape[0], c_all.shape[0], 6, dm)
    pad2 = jnp.zeros((bsz, 2, dm), F32)

    def layer_mods(layer):
        modl = jnp.concatenate([mods[layer, :bsz], pad2], axis=1)
        modc = jnp.concatenate([mods[layer, bsz], pad2[0]], axis=0)
        return modl, modc

    modl, modc = layer_mods(0)
    g = norm_g[0]
    w = ab_in_w[0]
    o_qk, o_v, o_o, o_gt, o_lx, o_lg = 0, 1024, 2048, 3072, 3088, 4112
    w_in = jnp.concatenate([w[:, o_qk:o_v], w[:, o_v:o_o], w[:, o_o:o_gt], w[:, o_lx:o_lg], w[:, o_lg:],
                            w[:, o_gt:o_lx], jnp.zeros((dm, LANES - 16), F32)], axis=1).astype(BF16)
    gb = jnp.concatenate([m_gate_b[0].reshape(1, 16), jnp.zeros((1, LANES - 16), F32)], axis=1)
    q, k, v, og, lx, glg, gates, gates_t = _in_ab(x2, modl, modc, g[0:1], w_in, m_conv_w[0], m_conv_b[0][None],
                                                  l_conv_w[0], l_conv_b[0][None], gb, bsz, t, ctx_len)
    hm_f, hm_b = _mlstm(q, k, v, gates, gates_t, bsz, t, ctx_len)
    hl_f, hl_b = _lru(lx, l_wa[0].astype(BF16), l_wx[0].astype(BF16), l_ba[0][:, None], l_bx[0][:, None],
                      l_lam[0][:, None], bsz, t, ctx_len)
    x2 = _out_proj(_out_ab_kernel, "out_proj_ab", [hm_f, hm_b, og, hl_f, hl_b, glg], [m_norm_w[0][None]],
                   x2, modl, modc, g[1:2], ab_out_w[0].astype(BF16), bsz, t, ctx_len)
    x2 = _ffn(x2, modl, modc, g[2:3], g[3:4], ffn_w1[0].astype(BF16), ffn_w2[0].astype(BF16), bsz, t, ctx_len)

    modl, modc = layer_mods(1)
    g = norm_g[1]
    w = cd_in_w[0]
    o_gv, o_gr, o_ga, o_z, o_xbc, o_dt = 1024, 2048, 3072, 3104, 4128, 5664
    zpad = jnp.zeros((dm, LANES - 32), F32)
    w_in = jnp.concatenate([w[:, :o_gv], w[:, o_gv:o_gr], w[:, o_gr:o_ga], w[:, o_z:o_xbc], w[:, o_xbc:o_dt],
                            w[:, o_ga:o_z], zpad, w[:, o_dt:], w[:, o_dt:], jnp.zeros((dm, LANES - 64), F32)],
                           axis=1).astype(BF16)
    dtb = jnp.concatenate([s_dt_bias[0].reshape(1, 32), s_dt_bias[0].reshape(1, 32),
                           jnp.zeros((1, LANES - 64), F32)], axis=1)
    aneg = jnp.concatenate([jnp.zeros((1, 32), F32), -jnp.exp(s_A_log[0].reshape(1, 32)),
                            jnp.zeros((1, LANES - 64), F32)], axis=1)
    gq, gk, gv, grs, zs, sx, sb, sc, ga, dts, dts_t = _in_cd(x2, modl, modc, g[0:1], w_in, s_conv_w[0],
                                                             s_conv_b[0][None], dtb, aneg, bsz, t, ctx_len)
    aw = jnp.zeros((2, LANES, G_HEADS * G_DK), F32)
    aw = aw.at[0, 0:G_RANK].set(g_alpha_w[0, 0]).at[1, G_RANK:2 * G_RANK].set(g_alpha_w[0, 1])
    awh = aw.astype(BF16)
    awl = (aw - awh.astype(F32)).astype(BF16)
    og_f, og_b = _gla(gq, gk, gv, ga, awh, awl, g_alpha_b[0][:, None], bsz, t, ctx_len)
    ys_f, ys_b = _ssd(sx, sb, sc, dts, dts_t, bsz, t, ctx_len)
    dskip = jnp.repeat(s_D[0], S_P)[None]
    x2 = _out_proj(_out_cd_kernel, "out_proj_cd", [og_f, og_b, grs, ys_f, ys_b, sx, zs],
                   [g_norm_w[0][None], dskip, s_norm_w[0][None]],
                   x2, modl, modc, g[1:2], cd_out_w[0].astype(BF16), bsz, t, ctx_len)

    rw = jnp.concatenate([router_w[0], jnp.zeros((dm, LANES - N_EXPERTS), F32)], axis=1)
    rb = jnp.concatenate([router_b[0][None], jnp.zeros((1, LANES - N_EXPERTS), F32)], axis=1)
    h2, top_idx, top_gate, counts = _router(x2, modl, g[2:3], rw, rb, bsz, t, ctx_len)
    slot, block_e, n_used, n_blocks = _routing_tables(top_idx[:, 0:TOP_K], top_idx[:, TOP_K:2 * TOP_K],
                                                      counts[0, :N_EXPERTS].astype(I32))
    xs = _dispatch_rows(h2, slot.reshape(-1), n_blocks * MOE_ROWS)
    ys_e = _moe_experts(xs, moe_w1[0], moe_w2[0], block_e, n_used, n_blocks)
    yk = _gather_rows(ys_e, jnp.concatenate([slot[:, 0], slot[:, 1]]))
    out = _combine(yk, top_gate, x2, modl, g[3:4], bsz, t, ctx_len)
    return out.reshape(bsz, seq, dm)
```

```python
import functools

import jax
import jax.numpy as jnp
from jax import lax
from jax.experimental import pallas as pl
from jax.experimental.pallas import tpu as pltpu
from jax.experimental.pallas import tpu_sc as plsc

F32, BF16, I32 = jnp.float32, jnp.bfloat16, jnp.int32

D_MODEL = 1024
GRID_W = 64
EPS = 1e-6
LOG2E = 1.4426950408889634
M_HEADS, M_DK, M_DV = 4, 128, 256
LRU_BLOCKS, LRU_BW, LRU_C = 8, 128, 8.0
G_HEADS, G_DK, G_DV, G_RANK, G_TAU = 4, 128, 256, 16, 16.0
S_HEADS, S_P, S_N, S_GROUPS, S_HPG = 16, 64, 128, 2, 8
FF_DENSE = 2816
N_EXPERTS, TOP_K, FF_EXPERT = 8, 2, 3584

LANES = 128
VMEM_LIMIT = 56 * 1024 * 1024
ROW_TILE = 256
BIG_ROW_TILE_MAX = 640
PROJ_COLS = 256
M_CHUNK = 256
G_CHUNK, G_SUB = 256, 16
S_CHUNK = 256
LRU_TILE = 256
FF_STEP = 256
MOE_ROWS = 1024
MOE_FF = 896
SC_CORES, SC_SUBCORES = 2, 16
SC_LANES = 16
SC_GATHER_ROWS = 128


def _cparams(*sem):
    return pltpu.CompilerParams(dimension_semantics=sem, vmem_limit_bytes=VMEM_LIMIT)


def _silu(x):
    return x * jax.nn.sigmoid(x)


def _softplus(x):
    return jnp.maximum(x, 0.0) + jnp.log1p(jnp.exp(-jnp.abs(x)))


def _log_sigmoid(x):
    return jnp.minimum(x, 0.0) - jnp.log1p(jnp.exp(-jnp.abs(x)))


def _dot(a, b):
    return jnp.dot(a, b, preferred_element_type=F32)


def _dot_nt(a, b):
    return lax.dot_general(a, b, (((1,), (1,)), ((), ())), preferred_element_type=F32)


def _dot_tn(a, b):
    return lax.dot_general(a, b, (((0,), (0,)), ((), ())), preferred_element_type=F32)


def _split3(f):
    f1 = f.astype(BF16)
    r = f - f1.astype(F32)
    f2 = r.astype(BF16)
    f3 = (r - f2.astype(F32)).astype(BF16)
    return f1, f2, f3


def _sel_cols(mask01, f):
    p1, p2, p3 = _split3(f)
    return _dot(mask01, p1) + _dot(mask01, p2) + _dot(mask01, p3)


def _sel_rows(f, mask01):
    p1, p2, p3 = _split3(f)
    return _dot_nt(p1, mask01) + _dot_nt(p2, mask01) + _dot_nt(p3, mask01)


def _expand(f, sel01):
    p1, p2, p3 = _split3(f)
    return _dot(p1, sel01) + _dot(p2, sel01) + _dot(p3, sel01)


def _pack_rows(u):
    n = u.shape[1] // 2
    bits = lax.bitcast_convert_type(u.astype(BF16).astype(F32), jnp.uint32)
    return jnp.bitwise_or(bits[:, :n], jnp.right_shift(bits[:, n:], jnp.uint32(16)))


def _unpack_rows(p):
    hi = lax.bitcast_convert_type(jnp.bitwise_and(p, jnp.uint32(0xFFFF0000)), F32)
    lo = lax.bitcast_convert_type(jnp.left_shift(p, jnp.uint32(16)), F32)
    return jnp.concatenate([hi, lo], axis=1)


def _rms(u, g):
    return u * lax.rsqrt(jnp.mean(u * u, axis=-1, keepdims=True) + EPS) * g


def _head_rms(u, g, n_heads):
    w = u.shape[-1] // n_heads
    return jnp.concatenate([_rms(u[:, h * w:(h + 1) * w], g[:, h * w:(h + 1) * w]) for h in range(n_heads)], axis=1)


def _row_ids(tile_idx, tm):
    return tile_idx * tm + lax.broadcasted_iota(I32, (tm, 1), 0)


def _mod(modl_ref, modc_ref, idx, is_ctx):
    return jnp.where(is_ctx, modc_ref[idx:idx + 1, :], modl_ref[0, idx:idx + 1, :])


def _adaln(x, g, modl_ref, modc_ref, shift_idx, is_ctx):
    shift = _mod(modl_ref, modc_ref, shift_idx, is_ctx)
    scale = _mod(modl_ref, modc_ref, shift_idx + 1, is_ctx)
    return _rms(x, g) * (1.0 + scale) + shift


def _neighbour_tiles(r, step, is_lat):
    n_tiles, per_seg = r.shape[0], GRID_W // 8
    zero = jnp.zeros((1,) + r.shape[1:], r.dtype)

    def across(src):
        return jnp.where(is_lat, 0.0, r[src:src + 1]) if 0 <= src < n_tiles else zero

    pieces = []
    for s0 in range(0, n_tiles, per_seg):
        if step > 0:
            pieces += [across(s0 - 1), r[s0:s0 + per_seg - 1]]
        else:
            pieces += [r[s0 + 1:s0 + per_seg], across(s0 + per_seg)]
    return jnp.concatenate(pieces, axis=0)


def _dwconv(y, cw, cb, is_lat):
    tm, n = y.shape
    y3 = y.reshape(tm // 8, 8, n)
    sub = lax.broadcasted_iota(I32, (1, 8, 1), 1)
    w = lambda j: cw[j:j + 1, :].reshape(1, 1, n)
    r1, r2, r7 = pltpu.roll(y3, 1, 1), pltpu.roll(y3, 2, 1), pltpu.roll(y3, 7, 1)
    ym1 = jnp.where(sub >= 1, r1, _neighbour_tiles(r1, 1, is_lat))
    ym2 = jnp.where(sub >= 2, r2, _neighbour_tiles(r2, 1, is_lat))
    yp1 = jnp.where(sub <= 6, r7, _neighbour_tiles(r7, -1, is_lat))
    out = cb.reshape(1, 1, n) + w(0) * ym2 + w(1) * ym1 + w(2) * y3 + w(3) * yp1
    return out.reshape(tm, n)


def _resident(shape):
    nd = len(shape)
    return pl.BlockSpec(shape, lambda *_: (0,) * nd, pipeline_mode=pl.Buffered(1))


def _big_row_tile(t):
    return max(tm for tm in range(8, BIG_ROW_TILE_MAX + 1, 8) if t % tm == 0)


def _mod_kernel(c_ref, w_ref, b_ref, o_ref):
    a = _silu(c_ref[...])
    o_ref[0] = jnp.dot(a, w_ref[0], preferred_element_type=F32, precision=lax.Precision.HIGHEST) + b_ref[0]


def _modulation(c_all, mod_w, mod_b):
    depth, d, n6 = mod_w.shape
    rows = c_all.shape[0]
    tn = 1024
    return pl.pallas_call(
        _mod_kernel,
        grid=(depth, n6 // tn),
        in_specs=[pl.BlockSpec((rows, d), lambda l, j: (0, 0)),
                  pl.BlockSpec((1, d, tn), lambda l, j: (l, 0, j)),
                  pl.BlockSpec((1, 1, tn), lambda l, j: (l, 0, j))],
        out_specs=pl.BlockSpec((1, rows, tn), lambda l, j: (l, 0, j)),
        out_shape=jax.ShapeDtypeStruct((depth, rows, n6), F32),
        compiler_params=_cparams("arbitrary", "arbitrary"),
        name="modulation",
    )(c_all, mod_w, mod_b.reshape(depth, 1, n6))


def _project(h_ref, w_ref, w_col, out_ref, epilogue):
    for c in range(0, out_ref.shape[1], PROJ_COLS):
        y = _dot(h_ref[...], w_ref[:, w_col + c:w_col + c + PROJ_COLS])
        out_ref[:, c:c + PROJ_COLS] = epilogue(y, c).astype(out_ref.dtype)


def _in_ab_kernel(tm, ctx_len, xc_ref, xl_ref, modl_ref, modc_ref, g_ref, w_ref, mcw_ref, mcb_ref, lcw_ref,
                  lcb_ref, gb_ref, x_ref, q_ref, k_ref, v_ref, og_ref, lx_ref, glg_ref, gates_ref, gates_t_ref,
                  h_scr):
    r = _row_ids(pl.program_id(1), tm)
    is_ctx = r < ctx_len
    is_lat = pl.program_id(1) * tm >= ctx_len
    x_ref[...] = jnp.where(is_lat, xl_ref[...], xc_ref[...])
    h_scr[...] = _adaln(x_ref[...], g_ref[...], modl_ref, modc_ref, 0, is_ctx).astype(BF16)

    def conv(cw_ref, cb_ref):
        return lambda y, c: _dwconv(y, cw_ref[:, c:c + PROJ_COLS], cb_ref[:, c:c + PROJ_COLS], is_lat)

    qc, kc, lc = conv(mcw_ref, mcb_ref), conv(mcw_ref, mcb_ref), conv(lcw_ref, lcb_ref)
    _project(h_scr, w_ref, 0, q_ref, lambda y, c: _silu(qc(y, c)))
    _project(h_scr, w_ref, 512, k_ref, lambda y, c: _silu(kc(y, 512 + c)) * (M_DK ** -0.5))
    _project(h_scr, w_ref, 1024, v_ref, lambda y, c: y)
    _project(h_scr, w_ref, 2048, og_ref, lambda y, c: jax.nn.sigmoid(y))
    _project(h_scr, w_ref, 3072, lx_ref, lc)
    _project(h_scr, w_ref, 4096, glg_ref, lambda y, c: jax.nn.gelu(y))
    gt = _dot(h_scr[...], w_ref[:, 5120:5248]) + gb_ref[...]
    lane = lax.broadcasted_iota(I32, gt.shape, 1)
    gates = jnp.where(lane >= 2 * M_HEADS, _log_sigmoid(gt), gt)
    gates_ref[...] = gates
    gates_t_ref[...] = gates.T


def _in_ab(xc2, xl2, modl, modc, g, w, mcw, mcb, lcw, lcb, gb, bsz, t, ctx_len):
    tm = ROW_TILE
    nt, nct = t // tm, ctx_len // tm
    n = bsz * t
    row = lambda c: pl.BlockSpec((tm, c), lambda b, i: (b * nt + i, 0))
    outs = [(D_MODEL, F32), (512, BF16), (512, BF16), (1024, BF16), (1024, BF16), (1024, F32), (1024, BF16),
            (LANES, F32)]
    return pl.pallas_call(
        functools.partial(_in_ab_kernel, tm, ctx_len),
        grid=(bsz, nt),
        in_specs=[pl.BlockSpec((tm, D_MODEL), lambda b, i: (b * nct + jnp.minimum(i, nct - 1), 0)),
                  pl.BlockSpec((tm, D_MODEL), lambda b, i: (b * (nt - nct) + jnp.maximum(i - nct, 0), 0)),
                  pl.BlockSpec((1, 8, D_MODEL), lambda b, i: (b, 0, 0)),
                  _resident((8, D_MODEL)), _resident((1, D_MODEL)), _resident(w.shape),
                  _resident(mcw.shape), _resident(mcb.shape), _resident(lcw.shape), _resident(lcb.shape),
                  _resident(gb.shape)],
        out_specs=[row(c) for c, _ in outs] + [pl.BlockSpec((LANES, tm), lambda b, i: (0, b * nt + i))],
        out_shape=[jax.ShapeDtypeStruct((n, c), dt) for c, dt in outs] + [jax.ShapeDtypeStruct((LANES, n), F32)],
        scratch_shapes=[pltpu.VMEM((tm, D_MODEL), BF16)],
        compiler_params=_cparams("arbitrary", "arbitrary"),
        name="in_proj_ab",
    )(xc2, xl2, modl, modc, g, w, mcw, mcb, lcw, lcb, gb)


def _reverse_order(n_ctx_chunks, n_chunks):
    return lambda c: jnp.where(c < n_ctx_chunks, n_ctx_chunks - 1 - c, n_chunks - 1 - (c - n_ctx_chunks))


def _tri_mask(L, reverse):
    row = lax.broadcasted_iota(I32, (L, L), 0)
    col = lax.broadcasted_iota(I32, (L, L), 1)
    return (col >= row) if reverse else (col <= row)


def _bidir_scan(kern, L, row_ins, lane_ins, res_ins, out_cols, scratch, bsz, t, ctx_len, name):
    nc = t // L
    rev = _reverse_order(ctx_len // L, nc)
    fwd = lambda i: i
    n = bsz * t

    def row(c, order):
        return pl.BlockSpec((L, c), lambda b, i: (b * nc + order(i), 0))

    def lanes(r, order):
        return pl.BlockSpec((r, L), lambda b, i: (0, b * nc + order(i)))

    in_specs, args = [], []
    for order in (fwd, rev):
        in_specs += [row(a.shape[1], order) for a in row_ins] + [lanes(r, order) for _, r in lane_ins]
        args += list(row_ins) + [a for a, _ in lane_ins]
    in_specs += [_resident(a.shape) for a in res_ins]
    args += list(res_ins)
    return pl.pallas_call(
        kern,
        grid=(bsz, nc),
        in_specs=in_specs,
        out_specs=[row(c, fwd) for c in out_cols] + [row(c, rev) for c in out_cols],
        out_shape=[jax.ShapeDtypeStruct((n, c), BF16) for c in out_cols] * 2,
        scratch_shapes=scratch,
        compiler_params=_cparams("arbitrary", "arbitrary"),
        name=name,
    )(*args)


def _zero_at_start(*scratch):
    @pl.when(pl.program_id(1) == 0)
    def _():
        for s in scratch:
            s[...] = jnp.zeros_like(s)


def _mlstm_kernel(L, qf, kf, vf, gcf, grf, qb, kb, vb, gcb, grb, of, ob, c_scr, m_scr):
    _zero_at_start(c_scr, m_scr)
    ones_col = (lax.broadcasted_iota(I32, (L, LANES), 1) == 0).astype(BF16)
    chains = []
    for d, (q_ref, k_ref, v_ref, gc_ref, gr_ref, out_ref) in enumerate(((qf, kf, vf, gcf, grf, of),
                                                                         (qb, kb, vb, gcb, grb, ob))):
        reverse = d == 1
        mask = _tri_mask(L, reverse)
        mask01 = mask.astype(BF16)
        gc = gc_ref[...]
        gr = gr_ref[...]
        bcol = _sel_cols(mask01, gc)
        brow = _sel_rows(gr, mask01)
        last = 0 if reverse else L - 1
        for h in range(M_HEADS):
            fo, io = 2 * M_HEADS + M_HEADS * d + h, M_HEADS * d + h
            bc, br = bcol[:, fo:fo + 1], brow[fo:fo + 1, :]
            ic, ir = gc[:, io:io + 1], gr[io:io + 1, :]
            m = m_scr[d, h:h + 1, 0:1]
            g = bc + m
            dlog = jnp.where(mask, bc - br + ir, -jnp.inf)
            mt = jnp.maximum(g, jnp.max(dlog, axis=1, keepdims=True))
            bl = bc[last:last + 1, :]
            a_s = bl - bc + ic
            m_new = jnp.maximum(bl + m, jnp.max(a_s, axis=0, keepdims=True))
            chains.append(dict(
                d=d, h=h, out_ref=out_ref, mt=mt, w_inter=jnp.exp(g - mt), p=jnp.exp(dlog - mt),
                m_new=m_new, decay=jnp.exp(bl + m - m_new), ws=jnp.exp(a_s - m_new),
                qh=q_ref[:, h * M_DK:(h + 1) * M_DK], kh=k_ref[:, h * M_DK:(h + 1) * M_DK],
                vx=jnp.concatenate([v_ref[:, h * M_DV:(h + 1) * M_DV], ones_col], axis=1)))
    for ch in chains:
        ch["sc"] = (_dot_nt(ch["qh"], ch["kh"]) * ch["p"]).astype(BF16)
    for ch in chains:
        cx = c_scr[ch["d"], ch["h"]]
        numx = ch["w_inter"] * _dot(ch["qh"], cx.astype(BF16)) + _dot(ch["sc"], ch["vx"])
        den = numx[:, M_DV:M_DV + 1]
        hh = numx[:, 0:M_DV] * (1.0 / jnp.maximum(jnp.abs(den), jnp.exp(-ch["mt"])))
        ch["out_ref"][:, ch["h"] * M_DV:(ch["h"] + 1) * M_DV] = hh.astype(BF16)
    for ch in chains:
        d, h = ch["d"], ch["h"]
        kw = (ch["kh"].astype(F32) * ch["ws"]).astype(BF16)
        c_scr[d, h] = ch["decay"] * c_scr[d, h] + _dot_tn(kw, ch["vx"])
        m_scr[d, h:h + 1, 0:1] = ch["m_new"]


def _mlstm(q, k, v, gcol, grow, bsz, t, ctx_len):
    L = M_CHUNK
    scratch = [pltpu.VMEM((2, M_HEADS, M_DK, M_DV + LANES), F32), pltpu.VMEM((2, 8, LANES), F32)]
    return _bidir_scan(functools.partial(_mlstm_kernel, L), L, [q, k, v, gcol], [(grow, 16)], [], [1024], scratch,
                       bsz, t, ctx_len, "mlstm_scan")


def _lru_gates(L, d, u_ref, wa_ref, wx_ref, ba_ref, bx_ref, lam_ref, a_scr, b_scr):
    reverse = d == 1
    u = u_ref[...]
    ub = u.astype(BF16)
    sp = _softplus(-lam_ref[d])
    for n in range(LRU_BLOCKS):
        sl = slice(n * LRU_BW, (n + 1) * LRU_BW)
        rg = jax.nn.sigmoid(_dot(ub[:, sl], wa_ref[d, n]) + ba_ref[d, :, sl])
        ig = jax.nn.sigmoid(_dot(ub[:, sl], wx_ref[d, n]) + bx_ref[d, :, sl])
        log_a = -LRU_C * rg * sp[:, sl]
        a = jnp.exp(log_a)
        a_scr[d, :, sl] = a
        b_scr[d, :, sl] = jnp.sqrt(-jnp.tanh(log_a) * (a * a + 1.0)) * (ig * u[:, sl])
    a = a_scr[d].reshape(L // 8, 8, LRU_BLOCKS * LRU_BW)
    b = b_scr[d].reshape(L // 8, 8, LRU_BLOCKS * LRU_BW)
    sub = lax.broadcasted_iota(I32, (1, 8, 1), 1)
    for s in (1, 2, 4):
        if reverse:
            a_sh, b_sh, keep = pltpu.roll(a, 8 - s, 1), pltpu.roll(b, 8 - s, 1), sub < 8 - s
        else:
            a_sh, b_sh, keep = pltpu.roll(a, s, 1), pltpu.roll(b, s, 1), sub >= s
        b = jnp.where(keep, a * b_sh + b, b)
        a = jnp.where(keep, a * a_sh, a)
    a_scr[d] = a.reshape(L, LRU_BLOCKS * LRU_BW)
    b_scr[d] = b.reshape(L, LRU_BLOCKS * LRU_BW)


def _lru_kernel(L, uf, ub, wa_ref, wx_ref, ba_ref, bx_ref, lam_ref, of, ob, a_scr, b_scr, h_scr):
    _zero_at_start(h_scr)
    _lru_gates(L, 0, uf, wa_ref, wx_ref, ba_ref, bx_ref, lam_ref, a_scr, b_scr)
    _lru_gates(L, 1, ub, wa_ref, wx_ref, ba_ref, bx_ref, lam_ref, a_scr, b_scr)
    ng = L // 8

    def body(j, carry):
        hf, hb = carry
        rf = pl.ds(pl.multiple_of(j * 8, 8), 8)
        rb = pl.ds(pl.multiple_of((ng - 1 - j) * 8, 8), 8)
        gf = b_scr[0, rf, :] + a_scr[0, rf, :] * hf
        gb = b_scr[1, rb, :] + a_scr[1, rb, :] * hb
        b_scr[0, rf, :] = gf
        b_scr[1, rb, :] = gb
        return gf[7:8, :], gb[0:1, :]

    hf, hb = lax.fori_loop(0, ng, body, (h_scr[0:1, :], h_scr[1:2, :]))
    h_scr[0:1, :] = hf
    h_scr[1:2, :] = hb
    of[...] = b_scr[0].astype(BF16)
    ob[...] = b_scr[1].astype(BF16)


def _lru(u, wa, wx, ba, bx, lam, bsz, t, ctx_len):
    L = LRU_TILE
    scratch = [pltpu.VMEM((2, L, 1024), F32), pltpu.VMEM((2, L, 1024), F32), pltpu.VMEM((8, 1024), F32)]
    return _bidir_scan(functools.partial(_lru_kernel, L), L, [u], [], [wa, wx, ba, bx, lam], [1024], scratch,
                       bsz, t, ctx_len, "rglru_scan")


def _out_tail(tm, ctx_len, a1, a2, x_ref, modl_ref, modc_ref, g_ref, w_ref, xo_ref):
    is_ctx = _row_ids(pl.program_id(1), tm) < ctx_len
    half = w_ref.shape[0] // 2
    y = _dot(a1.astype(BF16), w_ref[0:half, :]) + _dot(a2.astype(BF16), w_ref[half:, :])
    xo_ref[...] = x_ref[...] + _mod(modl_ref, modc_ref, 2, is_ctx) * _rms(y, g_ref[...])


def _out_ab_kernel(tm, ctx_len, hf, hb, og, lf, lb, glg, nw, x_ref, modl_ref, modc_ref, g_ref, w_ref, xo_ref):
    hm = _head_rms(hf[...].astype(F32) + hb[...].astype(F32), nw[...], M_HEADS) * og[...].astype(F32)
    hl = (lf[...].astype(F32) + lb[...].astype(F32)) * glg[...].astype(F32)
    _out_tail(tm, ctx_len, hm, hl, x_ref, modl_ref, modc_ref, g_ref, w_ref, xo_ref)


def _out_cd_kernel(tm, ctx_len, gf, gb, grs, yf, yb, sx, zs, gnw, dsk, snw, x_ref, modl_ref, modc_ref, g_ref,
                   w_ref, xo_ref):
    og = _head_rms(gf[...].astype(F32) + gb[...].astype(F32), gnw[...], G_HEADS) * grs[...].astype(F32)
    ys = yf[...].astype(F32) + yb[...].astype(F32) + dsk[...] * sx[...].astype(F32)
    ys = _rms(ys * zs[...].astype(F32), snw[...])
    _out_tail(tm, ctx_len, og, ys, x_ref, modl_ref, modc_ref, g_ref, w_ref, xo_ref)


def _out_proj(kern, name, acts, vecs, x2, modl, modc, g, w, bsz, t, ctx_len):
    tm = _big_row_tile(t)
    nt = t // tm
    row = lambda c: pl.BlockSpec((tm, c), lambda b, i: (b * nt + i, 0))
    return pl.pallas_call(
        functools.partial(kern, tm, ctx_len),
        grid=(bsz, nt),
        in_specs=[row(1024)] * len(acts) + [_resident((1, 1024))] * len(vecs)
        + [row(D_MODEL), pl.BlockSpec((1, 8, D_MODEL), lambda b, i: (b, 0, 0)),
           _resident((8, D_MODEL)), _resident((1, D_MODEL)), _resident(w.shape)],
        out_specs=row(D_MODEL),
        out_shape=jax.ShapeDtypeStruct(x2.shape, F32),
        compiler_params=_cparams("arbitrary", "arbitrary"),
        name=name,
    )(*acts, *vecs, x2, modl, modc, g, w)


def _ffn_kernel(tm, ctx_len, x_ref, modl_ref, modc_ref, g2_ref, g3_ref, w1_ref, w2_ref, xo_ref):
    is_ctx = _row_ids(pl.program_id(1), tm) < ctx_len
    x = x_ref[...]
    h = _adaln(x, g2_ref[...], modl_ref, modc_ref, 3, is_ctx).astype(BF16)
    ff = w2_ref.shape[0]
    acc = jnp.zeros((tm, D_MODEL), F32)
    for j in range(ff // FF_STEP):
        lo = j * FF_STEP
        gj = _dot(h, w1_ref[:, lo:lo + FF_STEP])
        uj = _dot(h, w1_ref[:, ff + lo:ff + lo + FF_STEP])
        acc = acc + _dot((_silu(gj) * uj).astype(BF16), w2_ref[lo:lo + FF_STEP, :])
    xo_ref[...] = x + _mod(modl_ref, modc_ref, 5, is_ctx) * _rms(acc, g3_ref[...])


def _ffn(x2, modl, modc, g2, g3, w1, w2, bsz, t, ctx_len):
    tm = _big_row_tile(t)
    nt = t // tm
    row = lambda c: pl.BlockSpec((tm, c), lambda b, i: (b * nt + i, 0))
    return pl.pallas_call(
        functools.partial(_ffn_kernel, tm, ctx_len),
        grid=(bsz, nt),
        in_specs=[row(D_MODEL), pl.BlockSpec((1, 8, D_MODEL), lambda b, i: (b, 0, 0)), _resident((8, D_MODEL)),
                  _resident((1, D_MODEL)), _resident((1, D_MODEL)), _resident(w1.shape), _resident(w2.shape)],
        out_specs=row(D_MODEL),
        out_shape=jax.ShapeDtypeStruct(x2.shape, F32),
        compiler_params=_cparams("arbitrary", "arbitrary"),
        name="dense_swiglu",
    )(x2, modl, modc, g2, g3, w1, w2)


def _in_cd_kernel(tm, ctx_len, x_ref, modl_ref, modc_ref, g_ref, w_ref, cw_ref, cb_ref, dtb_ref, aneg_ref,
                  gq_ref, gk_ref, gv_ref, gr_ref, z_ref, sx_ref, sb_ref, sc_ref, ga_ref, dts_ref, dts_t_ref, h_scr):
    r = _row_ids(pl.program_id(1), tm)
    is_ctx = r < ctx_len
    h_scr[...] = _adaln(x_ref[...], g_ref[...], modl_ref, modc_ref, 0, is_ctx).astype(BF16)
    is_lat = pl.program_id(1) * tm >= ctx_len

    def conv_silu(col0):
        return lambda y, c: _silu(_dwconv(y, cw_ref[:, col0 + c:col0 + c + PROJ_COLS],
                                          cb_ref[:, col0 + c:col0 + c + PROJ_COLS], is_lat))

    _project(h_scr, w_ref, 0, gq_ref, lambda y, c: y * (G_DK ** -0.5))
    _project(h_scr, w_ref, 512, gk_ref, lambda y, c: y)
    _project(h_scr, w_ref, 1024, gv_ref, lambda y, c: y)
    _project(h_scr, w_ref, 2048, gr_ref, lambda y, c: _silu(y))
    _project(h_scr, w_ref, 3072, z_ref, lambda y, c: _silu(y))
    _project(h_scr, w_ref, 4096, sx_ref, conv_silu(0))
    _project(h_scr, w_ref, 5120, sb_ref, conv_silu(1024))
    _project(h_scr, w_ref, 5376, sc_ref, conv_silu(1280))
    ga_ref[...] = _dot(h_scr[...], w_ref[:, 5632:5760])
    dt = _softplus(_dot(h_scr[...], w_ref[:, 5760:5888]) + dtb_ref[...])
    lane = lax.broadcasted_iota(I32, dt.shape, 1)
    dts = jnp.where(lane < 2 * S_HEADS, dt, dt * aneg_ref[...])
    dts_ref[...] = dts
    dts_t_ref[...] = dts.T


def _in_cd(x2, modl, modc, g, w, cw, cb, dtb, aneg, bsz, t, ctx_len):
    tm = ROW_TILE
    nt = t // tm
    n = bsz * t
    row = lambda c: pl.BlockSpec((tm, c), lambda b, i: (b * nt + i, 0))
    outs = [(512, BF16), (512, BF16), (1024, BF16), (1024, BF16), (1024, BF16), (1024, BF16), (256, BF16),
            (256, BF16), (LANES, F32), (LANES, F32)]
    return pl.pallas_call(
        functools.partial(_in_cd_kernel, tm, ctx_len),
        grid=(bsz, nt),
        in_specs=[row(D_MODEL), pl.BlockSpec((1, 8, D_MODEL), lambda b, i: (b, 0, 0)), _resident((8, D_MODEL)),
                  _resident((1, D_MODEL)), _resident(w.shape), _resident(cw.shape), _resident(cb.shape),
                  _resident(dtb.shape), _resident(aneg.shape)],
        out_specs=[row(c) for c, _ in outs] + [pl.BlockSpec((LANES, tm), lambda b, i: (0, b * nt + i))],
        out_shape=[jax.ShapeDtypeStruct((n, c), dt) for c, dt in outs] + [jax.ShapeDtypeStruct((LANES, n), F32)],
        scratch_shapes=[pltpu.VMEM((tm, D_MODEL), BF16)],
        compiler_params=_cparams("arbitrary", "arbitrary"),
        name="in_proj_cd",
    )(x2, modl, modc, g, w, cw, cb, dtb, aneg)


def _gla_kernel(L, C, qf, kf, vf, gaf, qb, kb, vb, gab, awh_ref, awl_ref, ab_ref, of, ob, s_scr):
    _zero_at_start(s_scr)
    nb = L // C
    chains = []
    for d, (q_ref, k_ref, v_ref, ga_ref, out_ref) in enumerate(((qf, kf, vf, gaf, of), (qb, kb, vb, gab, ob))):
        reverse = d == 1
        mask = _tri_mask(L, reverse)
        g1, g2, _ = _split3(ga_ref[...])
        pre = _dot(g1, awh_ref[d]) + _dot(g2, awh_ref[d]) + _dot(g1, awl_ref[d])
        lg = _log_sigmoid(pre + ab_ref[d]) * (1.0 / G_TAU)
        ball = _sel_cols(mask.astype(BF16), lg)
        for h in range(G_HEADS):
            ks = slice(h * G_DK, (h + 1) * G_DK)
            b = ball[:, ks] * LOG2E
            qh = q_ref[:, ks].astype(F32)
            inter = _dot_nt((qh * jnp.exp2(b)).astype(BF16), s_scr[d, h].astype(BF16))
            chains.append(dict(d=d, h=h, reverse=reverse, mask=mask, b=b, qh=qh, kh=k_ref[:, ks].astype(F32),
                               vh=v_ref[:, h * G_DV:(h + 1) * G_DV], inter=inter, out_ref=out_ref, blocks=[]))
    for i in range(nb):
        lo, hi = i * C, (i + 1) * C
        for ch in chains:
            b = ch["b"]
            if ch["reverse"]:
                bref = b[hi:hi + 1, :] if i < nb - 1 else jnp.zeros((1, G_DK), F32)
                v0, v1 = lo // LANES * LANES, L
            else:
                bref = b[lo - 1:lo, :] if i > 0 else jnp.zeros((1, G_DK), F32)
                v0, v1 = 0, min(L, -(-hi // LANES) * LANES)
            qi = (ch["qh"][lo:hi, :] * jnp.exp2(b[lo:hi, :] - bref)).astype(BF16)
            ki = (ch["kh"][v0:v1] * jnp.exp2(bref - b[v0:v1])).astype(BF16)
            pieces = [jnp.zeros((C, v0), F32)] if v0 else []
            pieces.append(_dot_nt(qi, ki))
            if v1 < L:
                pieces.append(jnp.zeros((C, L - v1), F32))
            ch["blocks"].append(jnp.concatenate(pieces, axis=1) if len(pieces) > 1 else pieces[0])
    for ch in chains:
        att = jnp.where(ch["mask"], jnp.concatenate(ch["blocks"], axis=0), 0.0).astype(BF16)
        oh = ch["inter"] + _dot(att, ch["vh"])
        ch["out_ref"][:, ch["h"] * G_DV:(ch["h"] + 1) * G_DV] = oh.astype(BF16)
    for ch in chains:
        d, h, b = ch["d"], ch["h"], ch["b"]
        last = 0 if ch["reverse"] else L - 1
        bl = b[last:last + 1, :]
        kd = (ch["kh"] * jnp.exp2(bl - b)).astype(BF16)
        s_scr[d, h] = s_scr[d, h] * jnp.exp2(bl) + _dot_tn(ch["vh"], kd)


def _gla(q, k, v, ga, awh, awl, ab, bsz, t, ctx_len):
    L, C = G_CHUNK, G_SUB
    scratch = [pltpu.VMEM((2, G_HEADS, G_DV, G_DK), F32)]
    return _bidir_scan(functools.partial(_gla_kernel, L, C), L, [q, k, v, ga], [], [awh, awl, ab], [1024], scratch,
                       bsz, t, ctx_len, "gla_scan")


def _ssd_kernel(L, xf, bf, cf, gcf, grf, xb, bb, cb_, gcb, grb, of, ob, s_scr):
    _zero_at_start(s_scr)
    er = lax.broadcasted_iota(I32, (LANES, S_HEADS * S_P), 0)
    ec = jnp.right_shift(lax.broadcasted_iota(I32, (LANES, S_HEADS * S_P), 1), 6)
    glane = lax.broadcasted_iota(I32, (L, LANES), 1)
    lane = lax.broadcasted_iota(I32, (L, 2 * S_P), 1)
    dirs, chains = [], []
    for d, (x_ref, b_ref, c_ref, gc_ref, gr_ref, out_ref) in enumerate(((xf, bf, cf, gcf, grf, of),
                                                                         (xb, bb, cb_, gcb, grb, ob))):
        reverse = d == 1
        mask = _tri_mask(L, reverse)
        mask01 = mask.astype(BF16)
        gc = gc_ref[...]
        gr = gr_ref[...]
        bcol = _sel_cols(mask01, gc)
        brow = _sel_rows(gr, mask01)
        last = 0 if reverse else L - 1
        dt_o, la_o = S_HEADS * d, 2 * S_HEADS + S_HEADS * d
        sel_la = (er == ec + la_o).astype(BF16)
        x = x_ref[...]
        bla = jnp.where((glane >= la_o) & (glane < la_o + S_HEADS), bcol, 0.0)
        dt_at_la = pltpu.roll(gc, 2 * S_HEADS, 1)
        bl = bla[last:last + 1, :]
        ebx = _dot(jnp.exp(bla).astype(BF16), sel_la)
        wx = _dot((jnp.exp(bl - bla) * dt_at_la).astype(BF16), sel_la)
        dirs.append(dict(d=d, mask=mask, gr=gr, x=x, ebx=ebx, last=last, dt_o=dt_o, la_o=la_o, out_ref=out_ref,
                         xw=(x.astype(F32) * wx).astype(BF16),
                         bcol2=bcol * LOG2E, brow2=brow * LOG2E))
        for g in range(S_GROUPS):
            ns = slice(g * S_N, (g + 1) * S_N)
            chains.append(dict(dr=dirs[-1], g=g, bg=b_ref[:, ns], cg=c_ref[:, ns],
                               gs=slice(g * S_HPG * S_P, (g + 1) * S_HPG * S_P)))
    for ch in chains:
        dr = ch["dr"]
        ch["cb"] = _dot_nt(ch["cg"], ch["bg"])
        ch["inter"] = _dot(ch["cg"], s_scr[dr["d"], ch["g"]].astype(BF16)) * dr["ebx"][:, ch["gs"]]
    for ch in chains:
        dr, g = ch["dr"], ch["g"]
        parts = []
        for pr in range(S_HPG // 2):
            e0 = g * S_HPG + 2 * pr
            xp = dr["x"][:, e0 * S_P:(e0 + 2) * S_P]
            res = []
            for e in (e0, e0 + 1):
                col = dr["la_o"] + e
                bc, br = dr["bcol2"][:, col:col + 1], dr["brow2"][col:col + 1, :]
                dtr = dr["gr"][dr["dt_o"] + e:dr["dt_o"] + e + 1, :]
                att = (ch["cb"] * dtr * jnp.exp2(jnp.where(dr["mask"], bc - br, -jnp.inf))).astype(BF16)
                res.append(_dot(att, xp))
            parts.append(jnp.where(lane < S_P, res[0], res[1]))
        dr["out_ref"][:, ch["gs"]] = (jnp.concatenate(parts, axis=1) + ch["inter"]).astype(BF16)
    for ch in chains:
        dr, g, gs = ch["dr"], ch["g"], ch["gs"]
        s_scr[dr["d"], g] = (s_scr[dr["d"], g] * dr["ebx"][dr["last"]:dr["last"] + 1, gs]
                             + _dot_tn(ch["bg"], dr["xw"][:, gs]))


def _ssd(x, bm, cm, gcol, grow, bsz, t, ctx_len):
    L = S_CHUNK
    scratch = [pltpu.VMEM((2, S_GROUPS, S_N, S_HPG * S_P), F32)]
    return _bidir_scan(functools.partial(_ssd_kernel, L), L, [x, bm, cm, gcol], [(grow, 4 * S_HEADS)], [], [1024],
                       scratch,
                       bsz, t, ctx_len, "ssd_scan")


def _router_kernel(x_ref, modl_ref, g_ref, rw_ref, rb_ref, h_ref, idx_ref, gate_ref, cnt_ref, base_scr):
    @pl.when((pl.program_id(0) == 0) & (pl.program_id(1) == 0))
    def _():
        base_scr[...] = jnp.zeros_like(base_scr)

    h = _rms(x_ref[...], g_ref[...]) * (1.0 + modl_ref[0, 4:5, :]) + modl_ref[0, 3:4, :]
    h_ref[...] = _pack_rows(h)
    h1, h2, _ = _split3(h)
    logits = _dot(h1, rw_ref[0]) + _dot(h2, rw_ref[0]) + _dot(h1, rw_ref[1]) + rb_ref[...]
    lane = lax.broadcasted_iota(I32, logits.shape, 1)
    logits = jnp.where(lane < N_EXPERTS, logits, -jnp.inf)
    lanef = lane.astype(F32)
    m1 = jnp.max(logits, axis=1, keepdims=True)
    i1 = jnp.min(jnp.where(logits == m1, lanef, float(LANES)), axis=1, keepdims=True)
    rest = jnp.where(lanef == i1, -jnp.inf, logits)
    m2 = jnp.max(rest, axis=1, keepdims=True)
    i2 = jnp.min(jnp.where(rest == m2, lanef, float(LANES)), axis=1, keepdims=True)
    e = jnp.exp(m2 - m1)
    g1 = 1.0 / (1.0 + e)
    gate_ref[...] = jnp.where(lane == 0, g1, jnp.where(lane == 1, e * g1, 0.0))
    tm = logits.shape[0]
    oh1, oh2 = (lanef == i1).astype(F32), (lanef == i2).astype(F32)
    earlier = (lax.broadcasted_iota(I32, (tm, tm), 1) < lax.broadcasted_iota(I32, (tm, tm), 0)).astype(BF16)
    seen = base_scr[0:1, :] + _dot(earlier, (oh1 + oh2).astype(BF16))
    r1 = jnp.sum(oh1 * seen, axis=1, keepdims=True)
    r2 = jnp.sum(oh2 * seen, axis=1, keepdims=True)
    base_scr[0:1, :] = base_scr[0:1, :] + jnp.sum(oh1 + oh2, axis=0, keepdims=True)
    cnt_ref[...] = base_scr[...]
    idx_ref[...] = jnp.where(lane == 0, i1, jnp.where(lane == 1, i2, jnp.where(lane == 2, r1, jnp.where(
        lane == 3, r2, 0.0)))).astype(I32)


def _router(x2, modl, g, rw, rb, bsz, t, ctx_len):
    tm = ROW_TILE
    nt, nct = t // tm, ctx_len // tm
    nl = bsz * (t - ctx_len)
    nlt = nt - nct
    orow = lambda c: pl.BlockSpec((tm, c), lambda b, i: (b * nlt + i, 0))
    return pl.pallas_call(
        _router_kernel,
        grid=(bsz, nlt),
        in_specs=[pl.BlockSpec((tm, D_MODEL), lambda b, i: (b * nt + nct + i, 0)),
                  pl.BlockSpec((1, 8, D_MODEL), lambda b, i: (b, 0, 0)),
                  _resident((1, D_MODEL)), _resident(rw.shape), _resident(rb.shape)],
        out_specs=[orow(D_MODEL // 2), orow(LANES), orow(LANES), pl.BlockSpec((8, LANES), lambda b, i: (0, 0))],
        out_shape=[jax.ShapeDtypeStruct((nl, D_MODEL // 2), jnp.uint32), jax.ShapeDtypeStruct((nl, LANES), I32),
                   jax.ShapeDtypeStruct((nl, LANES), F32), jax.ShapeDtypeStruct((8, LANES), F32)],
        scratch_shapes=[pltpu.VMEM((8, LANES), F32)],
        compiler_params=_cparams("arbitrary", "arbitrary"),
        name="moe_router",
    )(x2, modl, g, rw, rb)


def _gather_rows(table, idx):
    n_rows, d = idx.shape[0], table.shape[1]
    n_workers = SC_CORES * SC_SUBCORES
    per_worker = n_rows // n_workers
    chunk = SC_GATHER_ROWS
    assert per_worker * n_workers == n_rows and per_worker % chunk == 0
    mesh = plsc.VectorSubcoreMesh(core_axis_name="c", subcore_axis_name="s")

    @functools.partial(
        pl.kernel, mesh=mesh, out_type=jax.ShapeDtypeStruct((n_rows, d), table.dtype),
        scratch_types=[pltpu.VMEM((chunk,), I32), pltpu.VMEM((chunk, d), table.dtype), pltpu.SemaphoreType.DMA],
        name="sc_gather_rows")
    def gather(table_hbm, idx_hbm, out_hbm, idx_v, rows_v, sem):
        base = (lax.axis_index("s") * SC_CORES + lax.axis_index("c")) * per_worker

        @pl.loop(0, per_worker // chunk)
        def _(j):
            off = pl.multiple_of(base + j * chunk, 8)
            pltpu.sync_copy(idx_hbm.at[pl.ds(off, chunk)], idx_v)
            pltpu.async_copy(table_hbm.at[idx_v], rows_v, sem).wait()
            pltpu.sync_copy(rows_v, out_hbm.at[pl.ds(off, chunk)])

    return gather(table, idx)


def _dispatch_rows(table, slot, n_out):
    assert TOP_K == 2
    n_assign, (n_tok, d) = slot.shape[0], table.shape
    n_workers = SC_CORES * SC_SUBCORES
    per_worker = n_out // n_workers
    chunk, vec = SC_GATHER_ROWS, SC_LANES
    assert per_worker * n_workers == n_out and per_worker % chunk == 0 and n_assign % vec == 0
    assert n_out < 3 * n_tok
    mesh = plsc.VectorSubcoreMesh(core_axis_name="c", subcore_axis_name="s")

    @functools.partial(
        pl.kernel, mesh=mesh, out_type=jax.ShapeDtypeStruct((n_out, d), table.dtype),
        scratch_types=[pltpu.VMEM((n_assign,), I32), pltpu.VMEM((per_worker,), I32),
                       pltpu.VMEM((chunk, d), table.dtype), pltpu.SemaphoreType.DMA],
        compiler_params=pltpu.CompilerParams(needs_layout_passes=False),
        name="sc_dispatch_rows")
    def dispatch(table_hbm, slot_hbm, out_hbm, slots_v, tok_v, rows_v, sem):
        base = (lax.axis_index("s") * SC_CORES + lax.axis_index("c")) * per_worker
        pltpu.sync_copy(slot_hbm, slots_v)
        lane = lax.broadcasted_iota(I32, (vec,), 0)

        @pl.loop(0, per_worker // vec)
        def _(j):
            p = base + j * vec + lane
            p = jnp.where(p >= n_tok, p - n_tok, p)
            tok_v[pl.ds(j * vec, vec)] = jnp.where(p >= n_tok, p - n_tok, p)

        @pl.loop(0, n_assign // vec)
        def _(j):
            s = slots_v[pl.ds(j * vec, vec)] - base
            mine = (s >= 0) & (s < per_worker)
            tok = lax.shift_right_logical(j * vec + lane, 1)
            plsc.store_scatter(tok_v, [jnp.where(mine, s, 0)], tok, mask=mine)

        @pl.loop(0, per_worker // chunk)
        def _(j):
            off = pl.multiple_of(j * chunk, 8)
            pltpu.async_copy(table_hbm.at[tok_v.at[pl.ds(off, chunk)]], rows_v, sem).wait()
            pltpu.sync_copy(rows_v, out_hbm.at[pl.ds(base + off, chunk)])

    return dispatch(table, slot)


def _moe_kernel(be_ref, nu_ref, nv_ref, x_ref, w1g_ref, w1u_ref, w2_ref, y_ref, acc_ref):
    i, j = pl.program_id(0), pl.program_id(1)
    bm = x_ref.shape[0]

    def swiglu_rows(rows):
        x = _unpack_rows(x_ref[0:rows, :]).astype(BF16)
        a = (_silu(_dot(x, w1g_ref[0].astype(BF16))) * _dot(x, w1u_ref[0].astype(BF16))).astype(BF16)
        acc_ref[0:rows, :] += _dot(a, w2_ref[0].astype(BF16))

    @pl.when(i < nu_ref[0])
    def _():
        @pl.when(j == 0)
        def _():
            acc_ref[...] = jnp.zeros_like(acc_ref)

        @pl.when(nv_ref[i] > bm // 2)
        def _():
            swiglu_rows(bm)

        @pl.when(nv_ref[i] <= bm // 2)
        def _():
            swiglu_rows(bm // 2)

        @pl.when(j == pl.num_programs(1) - 1)
        def _():
            y_ref[...] = _pack_rows(acc_ref[...])


def _moe_experts(xs, w1, w2, block_e, n_used, n_valid, n_blocks):
    bm, fc = MOE_ROWS, MOE_FF
    nff = FF_EXPERT // fc
    used = lambda i, nu: jnp.minimum(i, nu[0] - 1)
    ffi = lambda i, j, nu: jnp.where(i < nu[0], j, nff - 1)
    grid_spec = pltpu.PrefetchScalarGridSpec(
        num_scalar_prefetch=3,
        grid=(n_blocks, nff),
        in_specs=[pl.BlockSpec((bm, D_MODEL // 2), lambda i, j, be, nu, nv: (used(i, nu), 0)),
                  pl.BlockSpec((1, D_MODEL, fc), lambda i, j, be, nu, nv: (be[used(i, nu)], 0, ffi(i, j, nu))),
                  pl.BlockSpec((1, D_MODEL, fc),
                               lambda i, j, be, nu, nv: (be[used(i, nu)], 0, nff + ffi(i, j, nu))),
                  pl.BlockSpec((1, fc, D_MODEL), lambda i, j, be, nu, nv: (be[used(i, nu)], ffi(i, j, nu), 0))],
        out_specs=pl.BlockSpec((bm, D_MODEL // 2), lambda i, j, be, nu, nv: (used(i, nu), 0)),
        scratch_shapes=[pltpu.VMEM((bm, D_MODEL), F32)],
    )
    return pl.pallas_call(
        _moe_kernel,
        grid_spec=grid_spec,
        out_shape=jax.ShapeDtypeStruct((n_blocks * bm, D_MODEL // 2), jnp.uint32),
        compiler_params=_cparams("arbitrary", "arbitrary"),
        name="moe_experts",
    )(block_e, n_used, n_valid, xs, w1, w1, w2)


def _combine_kernel(y0_ref, y1_ref, gate_ref, x_ref, modl_ref, g_ref, o_ref):
    gt = gate_ref[...]
    y = gt[:, 0:1] * _unpack_rows(y0_ref[...]) + gt[:, 1:2] * _unpack_rows(y1_ref[...])
    o_ref[...] = x_ref[...] + modl_ref[0, 5:6, :] * _rms(y, g_ref[...])


def _combine(yk, gates, x2, modl, g, bsz, t, ctx_len):
    tm = ROW_TILE
    nt, nct = t // tm, ctx_len // tm
    nlt = nt - nct
    nl = bsz * (t - ctx_len)
    orow = lambda c: pl.BlockSpec((tm, c), lambda b, i: (b * nlt + i, 0))
    second = pl.BlockSpec((tm, D_MODEL // 2), lambda b, i: (nl // tm + b * nlt + i, 0))
    return pl.pallas_call(
        _combine_kernel,
        grid=(bsz, nlt),
        in_specs=[orow(D_MODEL // 2), second, orow(LANES),
                  pl.BlockSpec((tm, D_MODEL), lambda b, i: (b * nt + nct + i, 0)),
                  pl.BlockSpec((1, 8, D_MODEL), lambda b, i: (b, 0, 0)), _resident((1, D_MODEL))],
        out_specs=orow(D_MODEL),
        out_shape=jax.ShapeDtypeStruct((nl, D_MODEL), F32),
        compiler_params=_cparams("arbitrary", "arbitrary"),
        name="moe_combine",
    )(yk, yk, gates, x2, modl, g)


def _routing_tables(experts, ranks, counts):
    bm = MOE_ROWS
    n_assign = experts.size
    padded = (counts + bm - 1) // bm * bm
    ends_p = jnp.cumsum(padded)
    pstarts = ends_p - padded
    onehot = experts[..., None] == jnp.arange(N_EXPERTS, dtype=I32)
    slot = (jnp.sum(jnp.where(onehot, pstarts, 0), axis=-1) + ranks).astype(I32)
    n_blocks = n_assign // bm + N_EXPERTS
    block_e = jnp.minimum(jnp.sum(ends_p[None, :] <= (jnp.arange(n_blocks, dtype=I32) * bm)[:, None], axis=1),
                          N_EXPERTS - 1).astype(I32)
    n_used = (ends_p[-1] // bm).astype(I32).reshape(1)
    sel = block_e[:, None] == jnp.arange(N_EXPERTS, dtype=I32)
    rows_left = jnp.sum(jnp.where(sel, counts + pstarts, 0), axis=1) - jnp.arange(n_blocks, dtype=I32) * bm
    n_valid = jnp.clip(rows_left, 0, bm).astype(I32)
    return slot, block_e, n_used, n_valid, n_blocks


def kernel(x, c, ctx, c_ctx, mod_w, mod_b, norm_g, ab_in_w, m_conv_w, m_conv_b, m_gate_b, m_norm_w, l_conv_w,
           l_conv_b, l_wa, l_ba, l_wx, l_bx, l_lam, ab_out_w, ffn_w1, ffn_w2, cd_in_w, g_alpha_w, g_alpha_b,
           g_norm_w, s_conv_w, s_conv_b, s_dt_bias, s_A_log, s_D, s_norm_w, cd_out_w, router_w, router_b,
           moe_w1, moe_w2):
    bsz, seq, dm = x.shape
    ctx_len = ctx.shape[1]
    t = ctx_len + seq
    n = bsz * t
    assert ctx_len == ROW_TILE and seq % ROW_TILE == 0 and ROW_TILE % GRID_W == 0

    c_all = jnp.concatenate([c, c_ctx[None, :], jnp.zeros((-(bsz + 1) % 8, dm), F32)], axis=0)
    mods = _modulation(c_all, mod_w, mod_b).reshape(mod_w.shape[0], c_all.shape[0], 6, dm)
    pad2 = jnp.zeros((bsz, 2, dm), F32)

    def layer_mods(layer):
        modl = jnp.concatenate([mods[layer, :bsz], pad2], axis=1)
        modc = jnp.concatenate([mods[layer, bsz], pad2[0]], axis=0)
        return modl, modc

    modl, modc = layer_mods(0)
    g = norm_g[0]
    w = ab_in_w[0]
    o_qk, o_v, o_o, o_gt, o_lx, o_lg = 0, 1024, 2048, 3072, 3088, 4112
    w_in = jnp.concatenate([w[:, o_qk:o_v], w[:, o_v:o_o], w[:, o_o:o_gt], w[:, o_lx:o_lg], w[:, o_lg:],
                            w[:, o_gt:o_lx], jnp.zeros((dm, LANES - 16), F32)], axis=1).astype(BF16)
    gb = jnp.concatenate([m_gate_b[0].reshape(1, 16), jnp.zeros((1, LANES - 16), F32)], axis=1)
    x2, q, k, v, og, lx, glg, gates, gates_t = _in_ab(
        ctx.reshape(bsz * ctx_len, dm), x.reshape(bsz * seq, dm), modl, modc, g[0:1], w_in, m_conv_w[0],
        m_conv_b[0][None], l_conv_w[0], l_conv_b[0][None], gb, bsz, t, ctx_len)
    hm_f, hm_b = _mlstm(q, k, v, gates, gates_t, bsz, t, ctx_len)
    hl_f, hl_b = _lru(lx, l_wa[0].astype(BF16), l_wx[0].astype(BF16), l_ba[0][:, None], l_bx[0][:, None],
                      l_lam[0][:, None], bsz, t, ctx_len)
    x2 = _out_proj(_out_ab_kernel, "out_proj_ab", [hm_f, hm_b, og, hl_f, hl_b, glg], [m_norm_w[0][None]],
                   x2, modl, modc, g[1:2], ab_out_w[0].astype(BF16), bsz, t, ctx_len)
    x2 = _ffn(x2, modl, modc, g[2:3], g[3:4], ffn_w1[0].astype(BF16), ffn_w2[0].astype(BF16), bsz, t, ctx_len)

    modl, modc = layer_mods(1)
    g = norm_g[1]
    w = cd_in_w[0]
    o_gv, o_gr, o_ga, o_z, o_xbc, o_dt = 1024, 2048, 3072, 3104, 4128, 5664
    zpad = jnp.zeros((dm, LANES - 32), F32)
    w_in = jnp.concatenate([w[:, :o_gv], w[:, o_gv:o_gr], w[:, o_gr:o_ga], w[:, o_z:o_xbc], w[:, o_xbc:o_dt],
                            w[:, o_ga:o_z], zpad, w[:, o_dt:], w[:, o_dt:], jnp.zeros((dm, LANES - 64), F32)],
                           axis=1).astype(BF16)
    dtb = jnp.concatenate([s_dt_bias[0].reshape(1, 32), s_dt_bias[0].reshape(1, 32),
                           jnp.zeros((1, LANES - 64), F32)], axis=1)
    aneg = jnp.concatenate([jnp.zeros((1, 32), F32), -jnp.exp(s_A_log[0].reshape(1, 32)),
                            jnp.zeros((1, LANES - 64), F32)], axis=1)
    gq, gk, gv, grs, zs, sx, sb, sc, ga, dts, dts_t = _in_cd(x2, modl, modc, g[0:1], w_in, s_conv_w[0],
                                                             s_conv_b[0][None], dtb, aneg, bsz, t, ctx_len)
    aw = jnp.zeros((2, LANES, G_HEADS * G_DK), F32)
    aw = aw.at[0, 0:G_RANK].set(g_alpha_w[0, 0]).at[1, G_RANK:2 * G_RANK].set(g_alpha_w[0, 1])
    awh = aw.astype(BF16)
    awl = (aw - awh.astype(F32)).astype(BF16)
    og_f, og_b = _gla(gq, gk, gv, ga, awh, awl, g_alpha_b[0][:, None], bsz, t, ctx_len)
    ys_f, ys_b = _ssd(sx, sb, sc, dts, dts_t, bsz, t, ctx_len)
    dskip = jnp.repeat(s_D[0], S_P)[None]
    x2 = _out_proj(_out_cd_kernel, "out_proj_cd", [og_f, og_b, grs, ys_f, ys_b, sx, zs],
                   [g_norm_w[0][None], dskip, s_norm_w[0][None]],
                   x2, modl, modc, g[1:2], cd_out_w[0].astype(BF16), bsz, t, ctx_len)

    rw = jnp.concatenate([router_w[0], jnp.zeros((dm, LANES - N_EXPERTS), F32)], axis=1)
    rwh = rw.astype(BF16)
    rw = jnp.stack([rwh, (rw - rwh.astype(F32)).astype(BF16)])
    rb = jnp.concatenate([router_b[0][None], jnp.zeros((1, LANES - N_EXPERTS), F32)], axis=1)
    h2, top_idx, top_gate, counts = _router(x2, modl, g[2:3], rw, rb, bsz, t, ctx_len)
    slot, block_e, n_used, n_valid, n_blocks = _routing_tables(top_idx[:, 0:TOP_K], top_idx[:, TOP_K:2 * TOP_K],
                                                               counts[0, :N_EXPERTS].astype(I32))
    xs = _dispatch_rows(h2, slot.reshape(-1), n_blocks * MOE_ROWS)
    ys_e = _moe_experts(xs, moe_w1[0], moe_w2[0], block_e, n_used, n_valid, n_blocks)
    yk = _gather_rows(ys_e, jnp.concatenate([slot[:, 0], slot[:, 1]]))
    out = _combine(yk, top_gate, x2, modl, g[3:4], bsz, t, ctx_len)
    return out.reshape(bsz, seq, dm)
```

```python
import functools

import jax
import jax.numpy as jnp
from jax import lax
from jax.experimental import pallas as pl
from jax.experimental.pallas import tpu as pltpu
from jax.experimental.pallas import tpu_sc as plsc

F32, BF16, I32 = jnp.float32, jnp.bfloat16, jnp.int32

D_MODEL = 1024
GRID_W = 64
EPS = 1e-6
LOG2E = 1.4426950408889634
M_HEADS, M_DK, M_DV = 4, 128, 256
LRU_BLOCKS, LRU_BW, LRU_C = 8, 128, 8.0
G_HEADS, G_DK, G_DV, G_RANK, G_TAU = 4, 128, 256, 16, 16.0
S_HEADS, S_P, S_N, S_GROUPS, S_HPG = 16, 64, 128, 2, 8
FF_DENSE = 2816
N_EXPERTS, TOP_K, FF_EXPERT = 8, 2, 3584

LANES = 128
VMEM_LIMIT = 56 * 1024 * 1024
ROW_TILE = 256
BIG_ROW_TILE_MAX = 640
PROJ_COLS = 256
M_CHUNK = 256
G_CHUNK, G_SUB = 256, 16
S_CHUNK = 256
LRU_TILE = 256
FF_STEP = 256
MOE_ROWS = 1024
MOE_FF = 512
SC_CORES, SC_SUBCORES = 2, 16
SC_LANES = 16
SC_GATHER_ROWS = 128


def _cparams(*sem):
    return pltpu.CompilerParams(dimension_semantics=sem, vmem_limit_bytes=VMEM_LIMIT)


def _silu(x):
    return x * jax.nn.sigmoid(x)


def _softplus(x):
    return jnp.maximum(x, 0.0) + jnp.log1p(jnp.exp(-jnp.abs(x)))


def _log_sigmoid(x):
    return jnp.minimum(x, 0.0) - jnp.log1p(jnp.exp(-jnp.abs(x)))


def _dot(a, b):
    return jnp.dot(a, b, preferred_element_type=F32)


def _dot_nt(a, b):
    return lax.dot_general(a, b, (((1,), (1,)), ((), ())), preferred_element_type=F32)


def _dot_tn(a, b):
    return lax.dot_general(a, b, (((0,), (0,)), ((), ())), preferred_element_type=F32)


def _split3(f):
    f1 = f.astype(BF16)
    r = f - f1.astype(F32)
    f2 = r.astype(BF16)
    f3 = (r - f2.astype(F32)).astype(BF16)
    return f1, f2, f3


def _sel_cols(mask01, f):
    p1, p2, p3 = _split3(f)
    return _dot(mask01, p1) + _dot(mask01, p2) + _dot(mask01, p3)


def _sel_rows(f, mask01):
    p1, p2, p3 = _split3(f)
    return _dot_nt(p1, mask01) + _dot_nt(p2, mask01) + _dot_nt(p3, mask01)


def _expand(f, sel01):
    p1, p2, p3 = _split3(f)
    return _dot(p1, sel01) + _dot(p2, sel01) + _dot(p3, sel01)


def _pack_rows(u):
    n = u.shape[1] // 2
    bits = lax.bitcast_convert_type(u.astype(BF16).astype(F32), jnp.uint32)
    return jnp.bitwise_or(bits[:, :n], jnp.right_shift(bits[:, n:], jnp.uint32(16)))


def _unpack_rows(p):
    hi = lax.bitcast_convert_type(jnp.bitwise_and(p, jnp.uint32(0xFFFF0000)), F32)
    lo = lax.bitcast_convert_type(jnp.left_shift(p, jnp.uint32(16)), F32)
    return jnp.concatenate([hi, lo], axis=1)


def _rms(u, g):
    return u * lax.rsqrt(jnp.mean(u * u, axis=-1, keepdims=True) + EPS) * g


def _head_rms(u, g, n_heads):
    w = u.shape[-1] // n_heads
    return jnp.concatenate([_rms(u[:, h * w:(h + 1) * w], g[:, h * w:(h + 1) * w]) for h in range(n_heads)], axis=1)


def _row_ids(tile_idx, tm):
    return tile_idx * tm + lax.broadcasted_iota(I32, (tm, 1), 0)


def _mod(modl_ref, modc_ref, idx, is_ctx):
    return jnp.where(is_ctx, modc_ref[idx:idx + 1, :], modl_ref[0, idx:idx + 1, :])


def _adaln(x, g, modl_ref, modc_ref, shift_idx, is_ctx):
    shift = _mod(modl_ref, modc_ref, shift_idx, is_ctx)
    scale = _mod(modl_ref, modc_ref, shift_idx + 1, is_ctx)
    return _rms(x, g) * (1.0 + scale) + shift


def _neighbour_tiles(r, step, is_lat):
    n_tiles, per_seg = r.shape[0], GRID_W // 8
    zero = jnp.zeros((1,) + r.shape[1:], r.dtype)

    def across(src):
        return jnp.where(is_lat, 0.0, r[src:src + 1]) if 0 <= src < n_tiles else zero

    pieces = []
    for s0 in range(0, n_tiles, per_seg):
        if step > 0:
            pieces += [across(s0 - 1), r[s0:s0 + per_seg - 1]]
        else:
            pieces += [r[s0 + 1:s0 + per_seg], across(s0 + per_seg)]
    return jnp.concatenate(pieces, axis=0)


def _dwconv(y, cw, cb, is_lat):
    tm, n = y.shape
    y3 = y.reshape(tm // 8, 8, n)
    sub = lax.broadcasted_iota(I32, (1, 8, 1), 1)
    w = lambda j: cw[j:j + 1, :].reshape(1, 1, n)
    r1, r2, r7 = pltpu.roll(y3, 1, 1), pltpu.roll(y3, 2, 1), pltpu.roll(y3, 7, 1)
    ym1 = jnp.where(sub >= 1, r1, _neighbour_tiles(r1, 1, is_lat))
    ym2 = jnp.where(sub >= 2, r2, _neighbour_tiles(r2, 1, is_lat))
    yp1 = jnp.where(sub <= 6, r7, _neighbour_tiles(r7, -1, is_lat))
    out = cb.reshape(1, 1, n) + w(0) * ym2 + w(1) * ym1 + w(2) * y3 + w(3) * yp1
    return out.reshape(tm, n)


def _resident(shape):
    nd = len(shape)
    return pl.BlockSpec(shape, lambda *_: (0,) * nd, pipeline_mode=pl.Buffered(1))


def _big_row_tile(t):
    return max(tm for tm in range(8, BIG_ROW_TILE_MAX + 1, 8) if t % tm == 0)


def _mod_kernel(c_ref, w_ref, b_ref, o_ref):
    a = _silu(c_ref[...])
    o_ref[0] = jnp.dot(a, w_ref[0], preferred_element_type=F32, precision=lax.Precision.HIGHEST) + b_ref[0]


def _modulation(c_all, mod_w, mod_b):
    depth, d, n6 = mod_w.shape
    rows = c_all.shape[0]
    tn = 1024
    return pl.pallas_call(
        _mod_kernel,
        grid=(depth, n6 // tn),
        in_specs=[pl.BlockSpec((rows, d), lambda l, j: (0, 0)),
                  pl.BlockSpec((1, d, tn), lambda l, j: (l, 0, j)),
                  pl.BlockSpec((1, 1, tn), lambda l, j: (l, 0, j))],
        out_specs=pl.BlockSpec((1, rows, tn), lambda l, j: (l, 0, j)),
        out_shape=jax.ShapeDtypeStruct((depth, rows, n6), F32),
        compiler_params=_cparams("arbitrary", "arbitrary"),
        name="modulation",
    )(c_all, mod_w, mod_b.reshape(depth, 1, n6))


def _project(h_ref, w_ref, w_col, out_ref, epilogue):
    for c in range(0, out_ref.shape[1], PROJ_COLS):
        y = _dot(h_ref[...], w_ref[:, w_col + c:w_col + c + PROJ_COLS])
        out_ref[:, c:c + PROJ_COLS] = epilogue(y, c).astype(out_ref.dtype)


def _in_ab_kernel(tm, ctx_len, xc_ref, xl_ref, modl_ref, modc_ref, g_ref, w_ref, mcw_ref, mcb_ref, lcw_ref,
                  lcb_ref, gb_ref, x_ref, q_ref, k_ref, v_ref, og_ref, lx_ref, glg_ref, gates_ref, gates_t_ref,
                  h_scr):
    r = _row_ids(pl.program_id(1), tm)
    is_ctx = r < ctx_len
    is_lat = pl.program_id(1) * tm >= ctx_len
    x_ref[...] = jnp.where(is_lat, xl_ref[...], xc_ref[...])
    h_scr[...] = _adaln(x_ref[...], g_ref[...], modl_ref, modc_ref, 0, is_ctx).astype(BF16)

    def conv(cw_ref, cb_ref):
        return lambda y, c: _dwconv(y, cw_ref[:, c:c + PROJ_COLS], cb_ref[:, c:c + PROJ_COLS], is_lat)

    qc, kc, lc = conv(mcw_ref, mcb_ref), conv(mcw_ref, mcb_ref), conv(lcw_ref, lcb_ref)
    _project(h_scr, w_ref, 0, q_ref, lambda y, c: _silu(qc(y, c)))
    _project(h_scr, w_ref, 512, k_ref, lambda y, c: _silu(kc(y, 512 + c)) * (M_DK ** -0.5))
    _project(h_scr, w_ref, 1024, v_ref, lambda y, c: y)
    _project(h_scr, w_ref, 2048, og_ref, lambda y, c: jax.nn.sigmoid(y))
    _project(h_scr, w_ref, 3072, lx_ref, lc)
    _project(h_scr, w_ref, 4096, glg_ref, lambda y, c: jax.nn.gelu(y))
    gt = _dot(h_scr[...], w_ref[:, 5120:5248]) + gb_ref[...]
    lane = lax.broadcasted_iota(I32, gt.shape, 1)
    gates = jnp.where(lane >= 2 * M_HEADS, _log_sigmoid(gt), gt)
    gates_ref[...] = gates
    gates_t_ref[...] = gates.T


def _in_ab(xc2, xl2, modl, modc, g, w, mcw, mcb, lcw, lcb, gb, bsz, t, ctx_len):
    tm = ROW_TILE
    nt, nct = t // tm, ctx_len // tm
    n = bsz * t
    row = lambda c: pl.BlockSpec((tm, c), lambda b, i: (b * nt + i, 0))
    outs = [(D_MODEL, F32), (512, BF16), (512, BF16), (1024, BF16), (1024, BF16), (1024, BF16), (1024, BF16),
            (LANES, F32)]
    return pl.pallas_call(
        functools.partial(_in_ab_kernel, tm, ctx_len),
        grid=(bsz, nt),
        in_specs=[pl.BlockSpec((tm, D_MODEL), lambda b, i: (b * nct + jnp.minimum(i, nct - 1), 0)),
                  pl.BlockSpec((tm, D_MODEL), lambda b, i: (b * (nt - nct) + jnp.maximum(i - nct, 0), 0)),
                  pl.BlockSpec((1, 8, D_MODEL), lambda b, i: (b, 0, 0)),
                  _resident((8, D_MODEL)), _resident((1, D_MODEL)), _resident(w.shape),
                  _resident(mcw.shape), _resident(mcb.shape), _resident(lcw.shape), _resident(lcb.shape),
                  _resident(gb.shape)],
        out_specs=[row(c) for c, _ in outs] + [pl.BlockSpec((LANES, tm), lambda b, i: (0, b * nt + i))],
        out_shape=[jax.ShapeDtypeStruct((n, c), dt) for c, dt in outs] + [jax.ShapeDtypeStruct((LANES, n), F32)],
        scratch_shapes=[pltpu.VMEM((tm, D_MODEL), BF16)],
        compiler_params=_cparams("arbitrary", "arbitrary"),
        name="in_proj_ab",
    )(xc2, xl2, modl, modc, g, w, mcw, mcb, lcw, lcb, gb)


def _reverse_order(n_ctx_chunks, n_chunks):
    return lambda c: jnp.where(c < n_ctx_chunks, n_ctx_chunks - 1 - c, n_chunks - 1 - (c - n_ctx_chunks))


def _tri_mask(L, reverse):
    row = lax.broadcasted_iota(I32, (L, L), 0)
    col = lax.broadcasted_iota(I32, (L, L), 1)
    return (col >= row) if reverse else (col <= row)


def _bidir_scan(kern, L, row_ins, lane_ins, res_ins, out_cols, scratch, bsz, t, ctx_len, name):
    nc = t // L
    rev = _reverse_order(ctx_len // L, nc)
    fwd = lambda i: i
    n = bsz * t

    def row(c, order):
        return pl.BlockSpec((L, c), lambda b, i: (b * nc + order(i), 0))

    def lanes(r, order):
        return pl.BlockSpec((r, L), lambda b, i: (0, b * nc + order(i)))

    in_specs, args = [], []
    for order in (fwd, rev):
        in_specs += [row(a.shape[1], order) for a in row_ins] + [lanes(r, order) for _, r in lane_ins]
        args += list(row_ins) + [a for a, _ in lane_ins]
    in_specs += [_resident(a.shape) for a in res_ins]
    args += list(res_ins)
    return pl.pallas_call(
        kern,
        grid=(bsz, nc),
        in_specs=in_specs,
        out_specs=[row(c, fwd) for c in out_cols] + [row(c, rev) for c in out_cols],
        out_shape=[jax.ShapeDtypeStruct((n, c), BF16) for c in out_cols] * 2,
        scratch_shapes=scratch,
        compiler_params=_cparams("arbitrary", "arbitrary"),
        name=name,
    )(*args)


def _zero_at_start(*scratch):
    @pl.when(pl.program_id(1) == 0)
    def _():
        for s in scratch:
            s[...] = jnp.zeros_like(s)


def _mlstm_kernel(L, qf, kf, vf, gcf, grf, qb, kb, vb, gcb, grb, of, ob, c_scr, m_scr):
    _zero_at_start(c_scr, m_scr)
    ones = jnp.ones((L, LANES), BF16)
    wide = lambda c, k: jnp.concatenate([c] * k, axis=1)
    er = lax.broadcasted_iota(I32, (LANES, 2 * M_HEADS * LANES), 0)
    ej = jnp.right_shift(lax.broadcasted_iota(I32, (LANES, 2 * M_HEADS * LANES), 1), 7)
    chains = []
    for d, (q_ref, k_ref, v_ref, gc_ref, gr_ref, out_ref) in enumerate(((qf, kf, vf, gcf, grf, of),
                                                                         (qb, kb, vb, gcb, grb, ob))):
        reverse = d == 1
        mask = _tri_mask(L, reverse)
        mask01 = mask.astype(BF16)
        gc = gc_ref[...]
        gr = gr_ref[...]
        bcol = _sel_cols(mask01, gc)
        brow = _sel_rows(gr, mask01)
        glane = lax.broadcasted_iota(I32, (L, LANES), 1)
        src = jnp.where(ej < M_HEADS, M_HEADS * d + ej, M_HEADS + M_HEADS * d + ej)
        s1, s2, _ = _split3(jnp.where(glane < 2 * M_HEADS, gc, bcol))
        sel = (er == src).astype(BF16)
        rep = _dot(s1, sel) + _dot(s2, sel)
        last = 0 if reverse else L - 1
        for h in range(M_HEADS):
            fo, io = 2 * M_HEADS + M_HEADS * d + h, M_HEADS * d + h
            ic, bc = rep[:, h * LANES:(h + 1) * LANES], rep[:, (M_HEADS + h) * LANES:(M_HEADS + h + 1) * LANES]
            r1 = gr[io:io + 1, :] - brow[fo:fo + 1, :]
            m = m_scr[d, h:h + 1, :]
            g = bc + m
            mx = jnp.max(jnp.where(mask, r1, -jnp.inf), axis=1, keepdims=True)
            mt = jnp.maximum(g, bc + mx)
            p = jnp.exp(jnp.where(mask, wide(bc - mt, L // LANES) + r1, -jnp.inf))
            bl = bc[last:last + 1, :]
            a_s = bl - bc + ic
            m_new = jnp.maximum(bl + m, jnp.max(a_s, axis=0, keepdims=True))
            chains.append(dict(
                d=d, h=h, out_ref=out_ref, mt=mt, w_inter=jnp.exp(g - mt), p=p,
                m_new=m_new, decay=jnp.exp(bl + m - m_new), ws=jnp.exp(a_s - m_new),
                qh=q_ref[:, h * M_DK:(h + 1) * M_DK], kh=k_ref[:, h * M_DK:(h + 1) * M_DK],
                vx=jnp.concatenate([v_ref[:, h * M_DV:(h + 1) * M_DV], ones], axis=1)))
    for ch in chains:
        ch["sc"] = (_dot_nt(ch["qh"], ch["kh"]) * ch["p"]).astype(BF16)
    for ch in chains:
        cx = c_scr[ch["d"], ch["h"]]
        numx = wide(ch["w_inter"], 1 + M_DV // LANES) * _dot(ch["qh"], cx.astype(BF16)) + _dot(ch["sc"], ch["vx"])
        den = numx[:, M_DV:]
        rinv = 1.0 / jnp.maximum(jnp.abs(den), jnp.exp(-ch["mt"]))
        hh = numx[:, 0:M_DV] * wide(rinv, M_DV // LANES)
        ch["out_ref"][:, ch["h"] * M_DV:(ch["h"] + 1) * M_DV] = hh.astype(BF16)
    for ch in chains:
        d, h = ch["d"], ch["h"]
        kw = (ch["kh"].astype(F32) * ch["ws"]).astype(BF16)
        c_scr[d, h] = wide(ch["decay"], 1 + M_DV // LANES) * c_scr[d, h] + _dot_tn(kw, ch["vx"])
        m_scr[d, h:h + 1, :] = ch["m_new"]


def _mlstm(q, k, v, gcol, grow, bsz, t, ctx_len):
    L = M_CHUNK
    scratch = [pltpu.VMEM((2, M_HEADS, M_DK, M_DV + LANES), F32), pltpu.VMEM((2, 8, LANES), F32)]
    return _bidir_scan(functools.partial(_mlstm_kernel, L), L, [q, k, v, gcol], [(grow, 16)], [], [1024], scratch,
                       bsz, t, ctx_len, "mlstm_scan")


def _lru_gates(L, d, u_ref, wa_ref, wx_ref, ba_ref, bx_ref, lam_ref, a_scr, b_scr):
    reverse = d == 1
    ub = u_ref[...]
    u = ub.astype(F32)
    sp = _softplus(-lam_ref[d])
    for n in range(LRU_BLOCKS):
        sl = slice(n * LRU_BW, (n + 1) * LRU_BW)
        rg = jax.nn.sigmoid(_dot(ub[:, sl], wa_ref[d, n]) + ba_ref[d, :, sl])
        ig = jax.nn.sigmoid(_dot(ub[:, sl], wx_ref[d, n]) + bx_ref[d, :, sl])
        log_a = -LRU_C * rg * sp[:, sl]
        a = jnp.exp(log_a)
        a_scr[d, :, sl] = a
        b_scr[d, :, sl] = jnp.sqrt(-jnp.tanh(log_a) * (a * a + 1.0)) * (ig * u[:, sl])
    a = a_scr[d].reshape(L // 8, 8, LRU_BLOCKS * LRU_BW)
    b = b_scr[d].reshape(L // 8, 8, LRU_BLOCKS * LRU_BW)
    sub = lax.broadcasted_iota(I32, (1, 8, 1), 1)
    for s in (1, 2, 4):
        if reverse:
            a_sh, b_sh, keep = pltpu.roll(a, 8 - s, 1), pltpu.roll(b, 8 - s, 1), sub < 8 - s
        else:
            a_sh, b_sh, keep = pltpu.roll(a, s, 1), pltpu.roll(b, s, 1), sub >= s
        b = jnp.where(keep, a * b_sh + b, b)
        a = jnp.where(keep, a * a_sh, a)
    a_scr[d] = a.reshape(L, LRU_BLOCKS * LRU_BW)
    b_scr[d] = b.reshape(L, LRU_BLOCKS * LRU_BW)


def _lru_kernel(L, uf, ub, wa_ref, wx_ref, ba_ref, bx_ref, lam_ref, of, ob, a_scr, b_scr, h_scr):
    _zero_at_start(h_scr)
    _lru_gates(L, 0, uf, wa_ref, wx_ref, ba_ref, bx_ref, lam_ref, a_scr, b_scr)
    _lru_gates(L, 1, ub, wa_ref, wx_ref, ba_ref, bx_ref, lam_ref, a_scr, b_scr)
    ng = L // 8

    def body(j, carry):
        hf, hb = carry
        rf = pl.ds(pl.multiple_of(j * 8, 8), 8)
        rb = pl.ds(pl.multiple_of((ng - 1 - j) * 8, 8), 8)
        gf = b_scr[0, rf, :] + a_scr[0, rf, :] * hf
        gb = b_scr[1, rb, :] + a_scr[1, rb, :] * hb
        b_scr[0, rf, :] = gf
        b_scr[1, rb, :] = gb
        return gf[7:8, :], gb[0:1, :]

    hf, hb = lax.fori_loop(0, ng, body, (h_scr[0:1, :], h_scr[1:2, :]))
    h_scr[0:1, :] = hf
    h_scr[1:2, :] = hb
    of[...] = b_scr[0].astype(BF16)
    ob[...] = b_scr[1].astype(BF16)


def _lru(u, wa, wx, ba, bx, lam, bsz, t, ctx_len):
    L = LRU_TILE
    scratch = [pltpu.VMEM((2, L, 1024), F32), pltpu.VMEM((2, L, 1024), F32), pltpu.VMEM((8, 1024), F32)]
    return _bidir_scan(functools.partial(_lru_kernel, L), L, [u], [], [wa, wx, ba, bx, lam], [1024], scratch,
                       bsz, t, ctx_len, "rglru_scan")


def _out_tail(tm, ctx_len, a1, a2, x_ref, modl_ref, modc_ref, g_ref, w_ref, xo_ref):
    is_ctx = _row_ids(pl.program_id(1), tm) < ctx_len
    half = w_ref.shape[0] // 2
    y = _dot(a1.astype(BF16), w_ref[0:half, :]) + _dot(a2.astype(BF16), w_ref[half:, :])
    xo_ref[...] = x_ref[...] + _mod(modl_ref, modc_ref, 2, is_ctx) * _rms(y, g_ref[...])


def _out_ab_kernel(tm, ctx_len, hf, hb, og, lf, lb, glg, nw, x_ref, modl_ref, modc_ref, g_ref, w_ref, xo_ref):
    hm = _head_rms(hf[...].astype(F32) + hb[...].astype(F32), nw[...], M_HEADS) * og[...].astype(F32)
    hl = (lf[...].astype(F32) + lb[...].astype(F32)) * glg[...].astype(F32)
    _out_tail(tm, ctx_len, hm, hl, x_ref, modl_ref, modc_ref, g_ref, w_ref, xo_ref)


def _out_cd_kernel(tm, ctx_len, gf, gb, grs, yf, yb, sx, zs, gnw, dsk, snw, x_ref, modl_ref, modc_ref, g_ref,
                   w_ref, xo_ref):
    og = _head_rms(gf[...].astype(F32) + gb[...].astype(F32), gnw[...], G_HEADS) * grs[...].astype(F32)
    ys = yf[...].astype(F32) + yb[...].astype(F32) + dsk[...] * sx[...].astype(F32)
    ys = _rms(ys * zs[...].astype(F32), snw[...])
    _out_tail(tm, ctx_len, og, ys, x_ref, modl_ref, modc_ref, g_ref, w_ref, xo_ref)


def _out_proj(kern, name, acts, vecs, x2, modl, modc, g, w, bsz, t, ctx_len):
    tm = _big_row_tile(t)
    nt = t // tm
    row = lambda c: pl.BlockSpec((tm, c), lambda b, i: (b * nt + i, 0))
    return pl.pallas_call(
        functools.partial(kern, tm, ctx_len),
        grid=(bsz, nt),
        in_specs=[row(1024)] * len(acts) + [_resident((1, 1024))] * len(vecs)
        + [row(D_MODEL), pl.BlockSpec((1, 8, D_MODEL), lambda b, i: (b, 0, 0)),
           _resident((8, D_MODEL)), _resident((1, D_MODEL)), _resident(w.shape)],
        out_specs=row(D_MODEL),
        out_shape=jax.ShapeDtypeStruct(x2.shape, F32),
        compiler_params=_cparams("arbitrary", "arbitrary"),
        name=name,
    )(*acts, *vecs, x2, modl, modc, g, w)


def _ffn_kernel(tm, ctx_len, x_ref, modl_ref, modc_ref, g2_ref, g3_ref, w1_ref, w2_ref, xo_ref):
    is_ctx = _row_ids(pl.program_id(1), tm) < ctx_len
    x = x_ref[...]
    h = _adaln(x, g2_ref[...], modl_ref, modc_ref, 3, is_ctx).astype(BF16)
    ff = w2_ref.shape[0]
    acc = jnp.zeros((tm, D_MODEL), F32)
    for j in range(ff // FF_STEP):
        lo = j * FF_STEP
        gj = _dot(h, w1_ref[:, lo:lo + FF_STEP])
        uj = _dot(h, w1_ref[:, ff + lo:ff + lo + FF_STEP])
        acc = acc + _dot((_silu(gj) * uj).astype(BF16), w2_ref[lo:lo + FF_STEP, :])
    xo_ref[...] = x + _mod(modl_ref, modc_ref, 5, is_ctx) * _rms(acc, g3_ref[...])


def _ffn(x2, modl, modc, g2, g3, w1, w2, bsz, t, ctx_len):
    tm = _big_row_tile(t)
    nt = t // tm
    row = lambda c: pl.BlockSpec((tm, c), lambda b, i: (b * nt + i, 0))
    return pl.pallas_call(
        functools.partial(_ffn_kernel, tm, ctx_len),
        grid=(bsz, nt),
        in_specs=[row(D_MODEL), pl.BlockSpec((1, 8, D_MODEL), lambda b, i: (b, 0, 0)), _resident((8, D_MODEL)),
                  _resident((1, D_MODEL)), _resident((1, D_MODEL)), _resident(w1.shape), _resident(w2.shape)],
        out_specs=row(D_MODEL),
        out_shape=jax.ShapeDtypeStruct(x2.shape, F32),
        compiler_params=_cparams("arbitrary", "arbitrary"),
        name="dense_swiglu",
    )(x2, modl, modc, g2, g3, w1, w2)


def _in_cd_kernel(tm, ctx_len, x_ref, modl_ref, modc_ref, g_ref, w_ref, cw_ref, cb_ref, dtb_ref, aneg_ref,
                  gq_ref, gk_ref, gv_ref, gr_ref, z_ref, sx_ref, sb_ref, sc_ref, ga_ref, dts_ref, dts_t_ref, h_scr):
    r = _row_ids(pl.program_id(1), tm)
    is_ctx = r < ctx_len
    h_scr[...] = _adaln(x_ref[...], g_ref[...], modl_ref, modc_ref, 0, is_ctx).astype(BF16)
    is_lat = pl.program_id(1) * tm >= ctx_len

    def conv_silu(col0):
        return lambda y, c: _silu(_dwconv(y, cw_ref[:, col0 + c:col0 + c + PROJ_COLS],
                                          cb_ref[:, col0 + c:col0 + c + PROJ_COLS], is_lat))

    _project(h_scr, w_ref, 0, gq_ref, lambda y, c: y * (G_DK ** -0.5))
    _project(h_scr, w_ref, 512, gk_ref, lambda y, c: y)
    _project(h_scr, w_ref, 1024, gv_ref, lambda y, c: y)
    _project(h_scr, w_ref, 2048, gr_ref, lambda y, c: _silu(y))
    _project(h_scr, w_ref, 3072, z_ref, lambda y, c: _silu(y))
    _project(h_scr, w_ref, 4096, sx_ref, conv_silu(0))
    _project(h_scr, w_ref, 5120, sb_ref, conv_silu(1024))
    _project(h_scr, w_ref, 5376, sc_ref, conv_silu(1280))
    ga_ref[...] = _dot(h_scr[...], w_ref[:, 5632:5760])
    dt = _softplus(_dot(h_scr[...], w_ref[:, 5760:5888]) + dtb_ref[...])
    lane = lax.broadcasted_iota(I32, dt.shape, 1)
    dts = jnp.where(lane < 2 * S_HEADS, dt, dt * aneg_ref[...])
    dts_ref[...] = dts
    dts_t_ref[...] = dts.T


def _in_cd(x2, modl, modc, g, w, cw, cb, dtb, aneg, bsz, t, ctx_len):
    tm = ROW_TILE
    nt = t // tm
    n = bsz * t
    row = lambda c: pl.BlockSpec((tm, c), lambda b, i: (b * nt + i, 0))
    outs = [(512, BF16), (512, BF16), (1024, BF16), (1024, BF16), (1024, BF16), (1024, BF16), (256, BF16),
            (256, BF16), (LANES, F32), (LANES, F32)]
    return pl.pallas_call(
        functools.partial(_in_cd_kernel, tm, ctx_len),
        grid=(bsz, nt),
        in_specs=[row(D_MODEL), pl.BlockSpec((1, 8, D_MODEL), lambda b, i: (b, 0, 0)), _resident((8, D_MODEL)),
                  _resident((1, D_MODEL)), _resident(w.shape), _resident(cw.shape), _resident(cb.shape),
                  _resident(dtb.shape), _resident(aneg.shape)],
        out_specs=[row(c) for c, _ in outs] + [pl.BlockSpec((LANES, tm), lambda b, i: (0, b * nt + i))],
        out_shape=[jax.ShapeDtypeStruct((n, c), dt) for c, dt in outs] + [jax.ShapeDtypeStruct((LANES, n), F32)],
        scratch_shapes=[pltpu.VMEM((tm, D_MODEL), BF16)],
        compiler_params=_cparams("arbitrary", "arbitrary"),
        name="in_proj_cd",
    )(x2, modl, modc, g, w, cw, cb, dtb, aneg)


def _gla_kernel(L, C, qf, kf, vf, gaf, qb, kb, vb, gab, awh_ref, awl_ref, ab_ref, of, ob, s_scr):
    _zero_at_start(s_scr)
    nb = L // C
    chains = []
    for d, (q_ref, k_ref, v_ref, ga_ref, out_ref) in enumerate(((qf, kf, vf, gaf, of), (qb, kb, vb, gab, ob))):
        reverse = d == 1
        mask = _tri_mask(L, reverse)
        g1, g2, _ = _split3(ga_ref[...])
        pre = _dot(g1, awh_ref[d]) + _dot(g2, awh_ref[d]) + _dot(g1, awl_ref[d])
        lg = _log_sigmoid(pre + ab_ref[d]) * (1.0 / G_TAU)
        ball = _sel_cols(mask.astype(BF16), lg)
        for h in range(G_HEADS):
            ks = slice(h * G_DK, (h + 1) * G_DK)
            b = ball[:, ks] * LOG2E
            qh = q_ref[:, ks].astype(F32)
            inter = _dot_nt((qh * jnp.exp2(b)).astype(BF16), s_scr[d, h].astype(BF16))
            chains.append(dict(d=d, h=h, reverse=reverse, mask=mask, b=b, qh=qh, kh=k_ref[:, ks].astype(F32),
                               vh=v_ref[:, h * G_DV:(h + 1) * G_DV], inter=inter, out_ref=out_ref, blocks=[]))
    for i in range(nb):
        lo, hi = i * C, (i + 1) * C
        for ch in chains:
            b = ch["b"]
            if ch["reverse"]:
                bref = b[hi:hi + 1, :] if i < nb - 1 else jnp.zeros((1, G_DK), F32)
                v0, v1 = lo // LANES * LANES, L
            else:
                bref = b[lo - 1:lo, :] if i > 0 else jnp.zeros((1, G_DK), F32)
                v0, v1 = 0, min(L, -(-hi // LANES) * LANES)
            qi = (ch["qh"][lo:hi, :] * jnp.exp2(b[lo:hi, :] - bref)).astype(BF16)
            ki = (ch["kh"][v0:v1] * jnp.exp2(bref - b[v0:v1])).astype(BF16)
            pieces = [jnp.zeros((C, v0), F32)] if v0 else []
            pieces.append(_dot_nt(qi, ki))
            if v1 < L:
                pieces.append(jnp.zeros((C, L - v1), F32))
            ch["blocks"].append(jnp.concatenate(pieces, axis=1) if len(pieces) > 1 else pieces[0])
    for ch in chains:
        att = jnp.where(ch["mask"], jnp.concatenate(ch["blocks"], axis=0), 0.0).astype(BF16)
        oh = ch["inter"] + _dot(att, ch["vh"])
        ch["out_ref"][:, ch["h"] * G_DV:(ch["h"] + 1) * G_DV] = oh.astype(BF16)
    for ch in chains:
        d, h, b = ch["d"], ch["h"], ch["b"]
        last = 0 if ch["reverse"] else L - 1
        bl = b[last:last + 1, :]
        kd = (ch["kh"] * jnp.exp2(bl - b)).astype(BF16)
        s_scr[d, h] = s_scr[d, h] * jnp.exp2(bl) + _dot_tn(ch["vh"], kd)


def _gla(q, k, v, ga, awh, awl, ab, bsz, t, ctx_len):
    L, C = G_CHUNK, G_SUB
    scratch = [pltpu.VMEM((2, G_HEADS, G_DV, G_DK), F32)]
    return _bidir_scan(functools.partial(_gla_kernel, L, C), L, [q, k, v, ga], [], [awh, awl, ab], [1024], scratch,
                       bsz, t, ctx_len, "gla_scan")


def _ssd_kernel(L, xf, bf, cf, gcf, grf, xb, bb, cb_, gcb, grb, of, ob, s_scr):
    _zero_at_start(s_scr)
    er = lax.broadcasted_iota(I32, (LANES, S_HEADS * S_P), 0)
    ec = jnp.right_shift(lax.broadcasted_iota(I32, (LANES, S_HEADS * S_P), 1), 6)
    glane = lax.broadcasted_iota(I32, (L, LANES), 1)
    lane = lax.broadcasted_iota(I32, (L, 2 * S_P), 1)
    dirs, chains = [], []
    for d, (x_ref, b_ref, c_ref, gc_ref, gr_ref, out_ref) in enumerate(((xf, bf, cf, gcf, grf, of),
                                                                         (xb, bb, cb_, gcb, grb, ob))):
        reverse = d == 1
        mask = _tri_mask(L, reverse)
        mask01 = mask.astype(BF16)
        gc = gc_ref[...]
        gr = gr_ref[...]
        bcol = _sel_cols(mask01, gc)
        brow = _sel_rows(gr, mask01)
        last = 0 if reverse else L - 1
        dt_o, la_o = S_HEADS * d, 2 * S_HEADS + S_HEADS * d
        sel_la = (er == ec + la_o).astype(BF16)
        x = x_ref[...]
        bla = jnp.where((glane >= la_o) & (glane < la_o + S_HEADS), bcol, 0.0)
        dt_at_la = pltpu.roll(gc, 2 * S_HEADS, 1)
        bl = bla[last:last + 1, :]
        ebx = _dot(jnp.exp(bla).astype(BF16), sel_la)
        wx = _dot((jnp.exp(bl - bla) * dt_at_la).astype(BF16), sel_la)
        dirs.append(dict(d=d, mask=mask, gr=gr, x=x, ebx=ebx, last=last, dt_o=dt_o, la_o=la_o, out_ref=out_ref,
                         xw=(x.astype(F32) * wx).astype(BF16),
                         bcol2=bcol * LOG2E, brow2=brow * LOG2E))
        for g in range(S_GROUPS):
            ns = slice(g * S_N, (g + 1) * S_N)
            chains.append(dict(dr=dirs[-1], g=g, bg=b_ref[:, ns], cg=c_ref[:, ns],
                               gs=slice(g * S_HPG * S_P, (g + 1) * S_HPG * S_P)))
    for ch in chains:
        dr = ch["dr"]
        ch["cb"] = _dot_nt(ch["cg"], ch["bg"])
        ch["inter"] = _dot(ch["cg"], s_scr[dr["d"], ch["g"]].astype(BF16)) * dr["ebx"][:, ch["gs"]]
    for ch in chains:
        dr, g = ch["dr"], ch["g"]
        parts = []
        for pr in range(S_HPG // 2):
            e0 = g * S_HPG + 2 * pr
            xp = dr["x"][:, e0 * S_P:(e0 + 2) * S_P]
            res = []
            for e in (e0, e0 + 1):
                col = dr["la_o"] + e
                bc, br = dr["bcol2"][:, col:col + 1], dr["brow2"][col:col + 1, :]
                dtr = dr["gr"][dr["dt_o"] + e:dr["dt_o"] + e + 1, :]
                att = (ch["cb"] * dtr * jnp.exp2(jnp.where(dr["mask"], bc - br, -jnp.inf))).astype(BF16)
                res.append(_dot(att, xp))
            parts.append(jnp.where(lane < S_P, res[0], res[1]))
        dr["out_ref"][:, ch["gs"]] = (jnp.concatenate(parts, axis=1) + ch["inter"]).astype(BF16)
    for ch in chains:
        dr, g, gs = ch["dr"], ch["g"], ch["gs"]
        s_scr[dr["d"], g] = (s_scr[dr["d"], g] * dr["ebx"][dr["last"]:dr["last"] + 1, gs]
                             + _dot_tn(ch["bg"], dr["xw"][:, gs]))


def _ssd(x, bm, cm, gcol, grow, bsz, t, ctx_len):
    L = S_CHUNK
    scratch = [pltpu.VMEM((2, S_GROUPS, S_N, S_HPG * S_P), F32)]
    return _bidir_scan(functools.partial(_ssd_kernel, L), L, [x, bm, cm, gcol], [(grow, 4 * S_HEADS)], [], [1024],
                       scratch,
                       bsz, t, ctx_len, "ssd_scan")


def _router_kernel(x_ref, modl_ref, g_ref, rw_ref, rb_ref, h_ref, idx_ref, gate_ref, cnt_ref, base_scr):
    @pl.when((pl.program_id(0) == 0) & (pl.program_id(1) == 0))
    def _():
        base_scr[...] = jnp.zeros_like(base_scr)

    h = _rms(x_ref[...], g_ref[...]) * (1.0 + modl_ref[0, 4:5, :]) + modl_ref[0, 3:4, :]
    h_ref[...] = _pack_rows(h)
    h1, h2, _ = _split3(h)
    logits = _dot(h1, rw_ref[0]) + _dot(h2, rw_ref[0]) + _dot(h1, rw_ref[1]) + rb_ref[...]
    lane = lax.broadcasted_iota(I32, logits.shape, 1)
    logits = jnp.where(lane < N_EXPERTS, logits, -jnp.inf)
    lanef = lane.astype(F32)
    m1 = jnp.max(logits, axis=1, keepdims=True)
    i1 = jnp.min(jnp.where(logits == m1, lanef, float(LANES)), axis=1, keepdims=True)
    rest = jnp.where(lanef == i1, -jnp.inf, logits)
    m2 = jnp.max(rest, axis=1, keepdims=True)
    i2 = jnp.min(jnp.where(rest == m2, lanef, float(LANES)), axis=1, keepdims=True)
    e = jnp.exp(m2 - m1)
    g1 = 1.0 / (1.0 + e)
    gate_ref[...] = jnp.where(lane == 0, g1, jnp.where(lane == 1, e * g1, 0.0))
    tm = logits.shape[0]
    oh1, oh2 = (lanef == i1).astype(F32), (lanef == i2).astype(F32)
    earlier = (lax.broadcasted_iota(I32, (tm, tm), 1) < lax.broadcasted_iota(I32, (tm, tm), 0)).astype(BF16)
    seen = base_scr[0:1, :] + _dot(earlier, (oh1 + oh2).astype(BF16))
    r1 = jnp.sum(oh1 * seen, axis=1, keepdims=True)
    r2 = jnp.sum(oh2 * seen, axis=1, keepdims=True)
    base_scr[0:1, :] = base_scr[0:1, :] + jnp.sum(oh1 + oh2, axis=0, keepdims=True)
    cnt_ref[...] = base_scr[...]
    idx_ref[...] = jnp.where(lane == 0, i1, jnp.where(lane == 1, i2, jnp.where(lane == 2, r1, jnp.where(
        lane == 3, r2, 0.0)))).astype(I32)


def _router(x2, modl, g, rw, rb, bsz, t, ctx_len):
    tm = ROW_TILE
    nt, nct = t // tm, ctx_len // tm
    nl = bsz * (t - ctx_len)
    nlt = nt - nct
    orow = lambda c: pl.BlockSpec((tm, c), lambda b, i: (b * nlt + i, 0))
    return pl.pallas_call(
        _router_kernel,
        grid=(bsz, nlt),
        in_specs=[pl.BlockSpec((tm, D_MODEL), lambda b, i: (b * nt + nct + i, 0)),
                  pl.BlockSpec((1, 8, D_MODEL), lambda b, i: (b, 0, 0)),
                  _resident((1, D_MODEL)), _resident(rw.shape), _resident(rb.shape)],
        out_specs=[orow(D_MODEL // 2), orow(LANES), orow(LANES), pl.BlockSpec((8, LANES), lambda b, i: (0, 0))],
        out_shape=[jax.ShapeDtypeStruct((nl, D_MODEL // 2), jnp.uint32), jax.ShapeDtypeStruct((nl, LANES), I32),
                   jax.ShapeDtypeStruct((nl, LANES), F32), jax.ShapeDtypeStruct((8, LANES), F32)],
        scratch_shapes=[pltpu.VMEM((8, LANES), F32)],
        compiler_params=_cparams("arbitrary", "arbitrary"),
        name="moe_router",
    )(x2, modl, g, rw, rb)


def _gather_rows(table, idx):
    n_rows, d = idx.shape[0], table.shape[1]
    n_workers = SC_CORES * SC_SUBCORES
    per_worker = n_rows // n_workers
    chunk = SC_GATHER_ROWS
    assert per_worker * n_workers == n_rows and per_worker % chunk == 0
    mesh = plsc.VectorSubcoreMesh(core_axis_name="c", subcore_axis_name="s")

    @functools.partial(
        pl.kernel, mesh=mesh, out_type=jax.ShapeDtypeStruct((n_rows, d), table.dtype),
        scratch_types=[pltpu.VMEM((chunk,), I32), pltpu.VMEM((chunk, d), table.dtype), pltpu.SemaphoreType.DMA],
        name="sc_gather_rows")
    def gather(table_hbm, idx_hbm, out_hbm, idx_v, rows_v, sem):
        base = (lax.axis_index("s") * SC_CORES + lax.axis_index("c")) * per_worker

        @pl.loop(0, per_worker // chunk)
        def _(j):
            off = pl.multiple_of(base + j * chunk, 8)
            pltpu.sync_copy(idx_hbm.at[pl.ds(off, chunk)], idx_v)
            pltpu.async_copy(table_hbm.at[idx_v], rows_v, sem).wait()
            pltpu.sync_copy(rows_v, out_hbm.at[pl.ds(off, chunk)])

    return gather(table, idx)


def _dispatch_rows(table, slot, n_out):
    assert TOP_K == 2
    n_assign, (n_tok, d) = slot.shape[0], table.shape
    n_workers = SC_CORES * SC_SUBCORES
    per_worker = n_out // n_workers
    chunk, vec = SC_GATHER_ROWS, SC_LANES
    assert per_worker * n_workers == n_out and per_worker % chunk == 0 and n_assign % vec == 0
    assert n_out < 3 * n_tok
    mesh = plsc.VectorSubcoreMesh(core_axis_name="c", subcore_axis_name="s")

    @functools.partial(
        pl.kernel, mesh=mesh, out_type=jax.ShapeDtypeStruct((n_out, d), table.dtype),
        scratch_types=[pltpu.VMEM((n_assign,), I32), pltpu.VMEM((per_worker,), I32),
                       pltpu.VMEM((chunk, d), table.dtype), pltpu.SemaphoreType.DMA],
        compiler_params=pltpu.CompilerParams(needs_layout_passes=False),
        name="sc_dispatch_rows")
    def dispatch(table_hbm, slot_hbm, out_hbm, slots_v, tok_v, rows_v, sem):
        base = (lax.axis_index("s") * SC_CORES + lax.axis_index("c")) * per_worker
        pltpu.sync_copy(slot_hbm, slots_v)
        lane = lax.broadcasted_iota(I32, (vec,), 0)

        @pl.loop(0, per_worker // vec)
        def _(j):
            p = base + j * vec + lane
            p = jnp.where(p >= n_tok, p - n_tok, p)
            tok_v[pl.ds(j * vec, vec)] = jnp.where(p >= n_tok, p - n_tok, p)

        @pl.loop(0, n_assign // vec)
        def _(j):
            s = slots_v[pl.ds(j * vec, vec)] - base
            mine = (s >= 0) & (s < per_worker)
            tok = lax.shift_right_logical(j * vec + lane, 1)
            plsc.store_scatter(tok_v, [jnp.where(mine, s, 0)], tok, mask=mine)

        @pl.loop(0, per_worker // chunk)
        def _(j):
            off = pl.multiple_of(j * chunk, 8)
            pltpu.async_copy(table_hbm.at[tok_v.at[pl.ds(off, chunk)]], rows_v, sem).wait()
            pltpu.sync_copy(rows_v, out_hbm.at[pl.ds(base + off, chunk)])

    return dispatch(table, slot)


def _moe_kernel(be_ref, nu_ref, nv_ref, x_ref, w1g_ref, w1u_ref, w2_ref, y_ref, acc_ref):
    i, j = pl.program_id(0), pl.program_id(1)
    bm = x_ref.shape[0]

    def swiglu_rows(rows):
        x = _unpack_rows(x_ref[0:rows, :]).astype(BF16)
        a = (_silu(_dot(x, w1g_ref[0].astype(BF16))) * _dot(x, w1u_ref[0].astype(BF16))).astype(BF16)
        acc_ref[0:rows, :] += _dot(a, w2_ref[0].astype(BF16))

    @pl.when(i < nu_ref[0])
    def _():
        @pl.when(j == 0)
        def _():
            acc_ref[...] = jnp.zeros_like(acc_ref)

        @pl.when(nv_ref[i] > bm // 2)
        def _():
            swiglu_rows(bm)

        @pl.when(nv_ref[i] <= bm // 2)
        def _():
            swiglu_rows(bm // 2)

        @pl.when(j == pl.num_programs(1) - 1)
        def _():
            y_ref[...] = _pack_rows(acc_ref[...])


def _moe_experts(xs, w1, w2, block_e, n_used, n_valid, n_blocks):
    bm, fc = MOE_ROWS, MOE_FF
    nff = FF_EXPERT // fc
    used = lambda i, nu: jnp.minimum(i, nu[0] - 1)
    ffi = lambda i, j, nu: jnp.where(i < nu[0], j, nff - 1)
    grid_spec = pltpu.PrefetchScalarGridSpec(
        num_scalar_prefetch=3,
        grid=(n_blocks, nff),
        in_specs=[pl.BlockSpec((bm, D_MODEL // 2), lambda i, j, be, nu, nv: (used(i, nu), 0)),
                  pl.BlockSpec((1, D_MODEL, fc), lambda i, j, be, nu, nv: (be[used(i, nu)], 0, ffi(i, j, nu))),
                  pl.BlockSpec((1, D_MODEL, fc),
                               lambda i, j, be, nu, nv: (be[used(i, nu)], 0, nff + ffi(i, j, nu))),
                  pl.BlockSpec((1, fc, D_MODEL), lambda i, j, be, nu, nv: (be[used(i, nu)], ffi(i, j, nu), 0))],
        out_specs=pl.BlockSpec((bm, D_MODEL // 2), lambda i, j, be, nu, nv: (used(i, nu), 0)),
        scratch_shapes=[pltpu.VMEM((bm, D_MODEL), F32)],
    )
    return pl.pallas_call(
        _moe_kernel,
        grid_spec=grid_spec,
        out_shape=jax.ShapeDtypeStruct((n_blocks * bm, D_MODEL // 2), jnp.uint32),
        compiler_params=_cparams("arbitrary", "arbitrary"),
        name="moe_experts",
    )(block_e, n_used, n_valid, xs, w1, w1, w2)


def _combine_kernel(y0_ref, y1_ref, gate_ref, x_ref, modl_ref, g_ref, o_ref):
    gt = gate_ref[...]
    y = gt[:, 0:1] * _unpack_rows(y0_ref[...]) + gt[:, 1:2] * _unpack_rows(y1_ref[...])
    o_ref[...] = x_ref[...] + modl_ref[0, 5:6, :] * _rms(y, g_ref[...])


def _combine(yk, gates, x2, modl, g, bsz, t, ctx_len):
    tm = ROW_TILE
    nt, nct = t // tm, ctx_len // tm
    nlt = nt - nct
    nl = bsz * (t - ctx_len)
    orow = lambda c: pl.BlockSpec((tm, c), lambda b, i: (b * nlt + i, 0))
    second = pl.BlockSpec((tm, D_MODEL // 2), lambda b, i: (nl // tm + b * nlt + i, 0))
    return pl.pallas_call(
        _combine_kernel,
        grid=(bsz, nlt),
        in_specs=[orow(D_MODEL // 2), second, orow(LANES),
                  pl.BlockSpec((tm, D_MODEL), lambda b, i: (b * nt + nct + i, 0)),
                  pl.BlockSpec((1, 8, D_MODEL), lambda b, i: (b, 0, 0)), _resident((1, D_MODEL))],
        out_specs=orow(D_MODEL),
        out_shape=jax.ShapeDtypeStruct((nl, D_MODEL), F32),
        compiler_params=_cparams("arbitrary", "arbitrary"),
        name="moe_combine",
    )(yk, yk, gates, x2, modl, g)


def _routing_tables(experts, ranks, counts):
    bm = MOE_ROWS
    n_assign = experts.size
    padded = (counts + bm - 1) // bm * bm
    ends_p = jnp.cumsum(padded)
    pstarts = ends_p - padded
    onehot = experts[..., None] == jnp.arange(N_EXPERTS, dtype=I32)
    slot = (jnp.sum(jnp.where(onehot, pstarts, 0), axis=-1) + ranks).astype(I32)
    n_blocks = n_assign // bm + N_EXPERTS
    block_e = jnp.minimum(jnp.sum(ends_p[None, :] <= (jnp.arange(n_blocks, dtype=I32) * bm)[:, None], axis=1),
                          N_EXPERTS - 1).astype(I32)
    n_used = (ends_p[-1] // bm).astype(I32).reshape(1)
    sel = block_e[:, None] == jnp.arange(N_EXPERTS, dtype=I32)
    rows_left = jnp.sum(jnp.where(sel, counts + pstarts, 0), axis=1) - jnp.arange(n_blocks, dtype=I32) * bm
    n_valid = jnp.clip(rows_left, 0, bm).astype(I32)
    return slot, block_e, n_used, n_valid, n_blocks


def kernel(x, c, ctx, c_ctx, mod_w, mod_b, norm_g, ab_in_w, m_conv_w, m_conv_b, m_gate_b, m_norm_w, l_conv_w,
           l_conv_b, l_wa, l_ba, l_wx, l_bx, l_lam, ab_out_w, ffn_w1, ffn_w2, cd_in_w, g_alpha_w, g_alpha_b,
           g_norm_w, s_conv_w, s_conv_b, s_dt_bias, s_A_log, s_D, s_norm_w, cd_out_w, router_w, router_b,
           moe_w1, moe_w2):
    bsz, seq, dm = x.shape
    ctx_len = ctx.shape[1]
    t = ctx_len + seq
    n = bsz * t
    assert ctx_len == ROW_TILE and seq % ROW_TILE == 0 and ROW_TILE % GRID_W == 0

    c_all = jnp.concatenate([c, c_ctx[None, :], jnp.zeros((-(bsz + 1) % 8, dm), F32)], axis=0)
    mods = _modulation(c_all, mod_w, mod_b).reshape(mod_w.shape[0], c_all.shape[0], 6, dm)
    pad2 = jnp.zeros((bsz, 2, dm), F32)

    def layer_mods(layer):
        modl = jnp.concatenate([mods[layer, :bsz], pad2], axis=1)
        modc = jnp.concatenate([mods[layer, bsz], pad2[0]], axis=0)
        return modl, modc

    modl, modc = layer_mods(0)
    g = norm_g[0]
    w = ab_in_w[0]
    o_qk, o_v, o_o, o_gt, o_lx, o_lg = 0, 1024, 2048, 3072, 3088, 4112
    w_in = jnp.concatenate([w[:, o_qk:o_v], w[:, o_v:o_o], w[:, o_o:o_gt], w[:, o_lx:o_lg], w[:, o_lg:],
                            w[:, o_gt:o_lx], jnp.zeros((dm, LANES - 16), F32)], axis=1).astype(BF16)
    gb = jnp.concatenate([m_gate_b[0].reshape(1, 16), jnp.zeros((1, LANES - 16), F32)], axis=1)
    x2, q, k, v, og, lx, glg, gates, gates_t = _in_ab(
        ctx.reshape(bsz * ctx_len, dm), x.reshape(bsz * seq, dm), modl, modc, g[0:1], w_in, m_conv_w[0],
        m_conv_b[0][None], l_conv_w[0], l_conv_b[0][None], gb, bsz, t, ctx_len)
    hm_f, hm_b = _mlstm(q, k, v, gates, gates_t, bsz, t, ctx_len)
    hl_f, hl_b = _lru(lx, l_wa[0].astype(BF16), l_wx[0].astype(BF16), l_ba[0][:, None], l_bx[0][:, None],
                      l_lam[0][:, None], bsz, t, ctx_len)
    x2 = _out_proj(_out_ab_kernel, "out_proj_ab", [hm_f, hm_b, og, hl_f, hl_b, glg], [m_norm_w[0][None]],
                   x2, modl, modc, g[1:2], ab_out_w[0].astype(BF16), bsz, t, ctx_len)
    x2 = _ffn(x2, modl, modc, g[2:3], g[3:4], ffn_w1[0].astype(BF16), ffn_w2[0].astype(BF16), bsz, t, ctx_len)

    modl, modc = layer_mods(1)
    g = norm_g[1]
    w = cd_in_w[0]
    o_gv, o_gr, o_ga, o_z, o_xbc, o_dt = 1024, 2048, 3072, 3104, 4128, 5664
    zpad = jnp.zeros((dm, LANES - 32), F32)
    w_in = jnp.concatenate([w[:, :o_gv], w[:, o_gv:o_gr], w[:, o_gr:o_ga], w[:, o_z:o_xbc], w[:, o_xbc:o_dt],
                            w[:, o_ga:o_z], zpad, w[:, o_dt:], w[:, o_dt:], jnp.zeros((dm, LANES - 64), F32)],
                           axis=1).astype(BF16)
    dtb = jnp.concatenate([s_dt_bias[0].reshape(1, 32), s_dt_bias[0].reshape(1, 32),
                           jnp.zeros((1, LANES - 64), F32)], axis=1)
    aneg = jnp.concatenate([jnp.zeros((1, 32), F32), -jnp.exp(s_A_log[0].reshape(1, 32)),
                            jnp.zeros((1, LANES - 64), F32)], axis=1)
    gq, gk, gv, grs, zs, sx, sb, sc, ga, dts, dts_t = _in_cd(x2, modl, modc, g[0:1], w_in, s_conv_w[0],
                                                             s_conv_b[0][None], dtb, aneg, bsz, t, ctx_len)
    aw = jnp.zeros((2, LANES, G_HEADS * G_DK), F32)
    aw = aw.at[0, 0:G_RANK].set(g_alpha_w[0, 0]).at[1, G_RANK:2 * G_RANK].set(g_alpha_w[0, 1])
    awh = aw.astype(BF16)
    awl = (aw - awh.astype(F32)).astype(BF16)
    og_f, og_b = _gla(gq, gk, gv, ga, awh, awl, g_alpha_b[0][:, None], bsz, t, ctx_len)
    ys_f, ys_b = _ssd(sx, sb, sc, dts, dts_t, bsz, t, ctx_len)
    dskip = jnp.repeat(s_D[0], S_P)[None]
    x2 = _out_proj(_out_cd_kernel, "out_proj_cd", [og_f, og_b, grs, ys_f, ys_b, sx, zs],
                   [g_norm_w[0][None], dskip, s_norm_w[0][None]],
                   x2, modl, modc, g[1:2], cd_out_w[0].astype(BF16), bsz, t, ctx_len)

    rw = jnp.concatenate([router_w[0], jnp.zeros((dm, LANES - N_EXPERTS), F32)], axis=1)
    rwh = rw.astype(BF16)
    rw = jnp.stack([rwh, (rw - rwh.astype(F32)).astype(BF16)])
    rb = jnp.concatenate([router_b[0][None], jnp.zeros((1, LANES - N_EXPERTS), F32)], axis=1)
    h2, top_idx, top_gate, counts = _router(x2, modl, g[2:3], rw, rb, bsz, t, ctx_len)
    slot, block_e, n_used, n_valid, n_blocks = _routing_tables(top_idx[:, 0:TOP_K], top_idx[:, TOP_K:2 * TOP_K],
                                                               counts[0, :N_EXPERTS].astype(I32))
    xs = _dispatch_rows(h2, slot.reshape(-1), n_blocks * MOE_ROWS)
    ys_e = _moe_experts(xs, moe_w1[0], moe_w2[0], block_e, n_used, n_valid, n_blocks)
    yk = _gather_rows(ys_e, jnp.concatenate([slot[:, 0], slot[:, 1]]))
    out = _combine(yk, top_gate, x2, modl, g[3:4], bsz, t, ctx_len)
    return out.reshape(bsz, seq, dm)
```

```python
import functools

import jax
import jax.numpy as jnp
from jax import lax
from jax.experimental import pallas as pl
from jax.experimental.pallas import tpu as pltpu
from jax.experimental.pallas import tpu_sc as plsc

F32, BF16, I32 = jnp.float32, jnp.bfloat16, jnp.int32

D_MODEL = 1024
GRID_W = 64
EPS = 1e-6
LOG2E = 1.4426950408889634
M_HEADS, M_DK, M_DV = 4, 128, 256
LRU_BLOCKS, LRU_BW, LRU_C = 8, 128, 8.0
G_HEADS, G_DK, G_DV, G_RANK, G_TAU = 4, 128, 256, 16, 16.0
S_HEADS, S_P, S_N, S_GROUPS, S_HPG = 16, 64, 128, 2, 8
FF_DENSE = 2816
N_EXPERTS, TOP_K, FF_EXPERT = 8, 2, 3584

LANES = 128
VMEM_LIMIT = 56 * 1024 * 1024
ROW_TILE = 256
BIG_ROW_TILE_MAX = 640
PROJ_COLS = 256
M_CHUNK = 256
G_CHUNK, G_SUB = 256, 16
S_CHUNK = 256
LRU_TILE = 256
FF_STEP = 256
MOE_ROWS = 1024
MOE_FF = 512
SC_CORES, SC_SUBCORES = 2, 16
SC_LANES = 16
SC_GATHER_ROWS = 128


def _cparams(*sem):
    return pltpu.CompilerParams(dimension_semantics=sem, vmem_limit_bytes=VMEM_LIMIT)


def _sigmoid(x):
    return 0.5 + 0.5 * jnp.tanh(0.5 * x)


def _silu(x):
    return x * _sigmoid(x)


def _softplus(x):
    return jnp.maximum(x, 0.0) + jnp.log1p(jnp.exp(-jnp.abs(x)))


def _log_sigmoid(x):
    return jnp.minimum(x, 0.0) - jnp.log1p(jnp.exp(-jnp.abs(x)))


def _dot(a, b):
    return jnp.dot(a, b, preferred_element_type=F32)


def _dot_nt(a, b):
    return lax.dot_general(a, b, (((1,), (1,)), ((), ())), preferred_element_type=F32)


def _dot_tn(a, b):
    return lax.dot_general(a, b, (((0,), (0,)), ((), ())), preferred_element_type=F32)


def _split3(f):
    f1 = f.astype(BF16)
    r = f - f1.astype(F32)
    f2 = r.astype(BF16)
    f3 = (r - f2.astype(F32)).astype(BF16)
    return f1, f2, f3


def _sel_cols(mask01, f):
    p1, p2, p3 = _split3(f)
    return _dot(mask01, p1) + _dot(mask01, p2) + _dot(mask01, p3)


def _sel_rows(f, mask01):
    p1, p2, p3 = _split3(f)
    return _dot_nt(p1, mask01) + _dot_nt(p2, mask01) + _dot_nt(p3, mask01)


def _expand(f, sel01):
    p1, p2, p3 = _split3(f)
    return _dot(p1, sel01) + _dot(p2, sel01) + _dot(p3, sel01)


def _pack_rows(u):
    n = u.shape[1] // 2
    bits = lax.bitcast_convert_type(u.astype(BF16).astype(F32), jnp.uint32)
    return jnp.bitwise_or(bits[:, :n], jnp.right_shift(bits[:, n:], jnp.uint32(16)))


def _unpack_rows(p):
    hi = lax.bitcast_convert_type(jnp.bitwise_and(p, jnp.uint32(0xFFFF0000)), F32)
    lo = lax.bitcast_convert_type(jnp.left_shift(p, jnp.uint32(16)), F32)
    return jnp.concatenate([hi, lo], axis=1)


def _rms(u, g):
    return u * lax.rsqrt(jnp.mean(u * u, axis=-1, keepdims=True) + EPS) * g


def _head_rms(u, g, n_heads):
    w = u.shape[-1] // n_heads
    return jnp.concatenate([_rms(u[:, h * w:(h + 1) * w], g[:, h * w:(h + 1) * w]) for h in range(n_heads)], axis=1)


def _row_ids(tile_idx, tm):
    return tile_idx * tm + lax.broadcasted_iota(I32, (tm, 1), 0)


def _mod(modl_ref, modc_ref, idx, is_ctx):
    return jnp.where(is_ctx, modc_ref[idx:idx + 1, :], modl_ref[0, idx:idx + 1, :])


def _adaln(x, g, modl_ref, modc_ref, shift_idx, is_ctx):
    shift = _mod(modl_ref, modc_ref, shift_idx, is_ctx)
    scale = _mod(modl_ref, modc_ref, shift_idx + 1, is_ctx)
    return _rms(x, g) * (1.0 + scale) + shift


def _neighbour_tiles(r, step, is_lat):
    n_tiles, per_seg = r.shape[0], GRID_W // 8
    zero = jnp.zeros((1,) + r.shape[1:], r.dtype)

    def across(src):
        return jnp.where(is_lat, 0.0, r[src:src + 1]) if 0 <= src < n_tiles else zero

    pieces = []
    for s0 in range(0, n_tiles, per_seg):
        if step > 0:
            pieces += [across(s0 - 1), r[s0:s0 + per_seg - 1]]
        else:
            pieces += [r[s0 + 1:s0 + per_seg], across(s0 + per_seg)]
    return jnp.concatenate(pieces, axis=0)


def _dwconv(y, cw, cb, is_lat):
    tm, n = y.shape
    y3 = y.reshape(tm // 8, 8, n)
    sub = lax.broadcasted_iota(I32, (1, 8, 1), 1)
    w = lambda j: cw[j:j + 1, :].reshape(1, 1, n)
    r1, r2, r7 = pltpu.roll(y3, 1, 1), pltpu.roll(y3, 2, 1), pltpu.roll(y3, 7, 1)
    ym1 = jnp.where(sub >= 1, r1, _neighbour_tiles(r1, 1, is_lat))
    ym2 = jnp.where(sub >= 2, r2, _neighbour_tiles(r2, 1, is_lat))
    yp1 = jnp.where(sub <= 6, r7, _neighbour_tiles(r7, -1, is_lat))
    out = cb.reshape(1, 1, n) + w(0) * ym2 + w(1) * ym1 + w(2) * y3 + w(3) * yp1
    return out.reshape(tm, n)


def _resident(shape):
    nd = len(shape)
    return pl.BlockSpec(shape, lambda *_: (0,) * nd, pipeline_mode=pl.Buffered(1))


def _big_row_tile(t):
    return max(tm for tm in range(8, BIG_ROW_TILE_MAX + 1, 8) if t % tm == 0)


def _mod_kernel(c_ref, w_ref, b_ref, o_ref):
    a = _silu(c_ref[...])
    o_ref[0] = jnp.dot(a, w_ref[0], preferred_element_type=F32, precision=lax.Precision.HIGHEST) + b_ref[0]


def _modulation(c_all, mod_w, mod_b):
    depth, d, n6 = mod_w.shape
    rows = c_all.shape[0]
    tn = 1024
    return pl.pallas_call(
        _mod_kernel,
        grid=(depth, n6 // tn),
        in_specs=[pl.BlockSpec((rows, d), lambda l, j: (0, 0)),
                  pl.BlockSpec((1, d, tn), lambda l, j: (l, 0, j)),
                  pl.BlockSpec((1, 1, tn), lambda l, j: (l, 0, j))],
        out_specs=pl.BlockSpec((1, rows, tn), lambda l, j: (l, 0, j)),
        out_shape=jax.ShapeDtypeStruct((depth, rows, n6), F32),
        compiler_params=_cparams("arbitrary", "arbitrary"),
        name="modulation",
    )(c_all, mod_w, mod_b.reshape(depth, 1, n6))


def _project(h_ref, w_ref, w_col, out_ref, epilogue):
    for c in range(0, out_ref.shape[1], PROJ_COLS):
        y = _dot(h_ref[...], w_ref[:, w_col + c:w_col + c + PROJ_COLS])
        out_ref[:, c:c + PROJ_COLS] = epilogue(y, c).astype(out_ref.dtype)


def _in_ab_kernel(tm, ctx_len, xc_ref, xl_ref, modl_ref, modc_ref, g_ref, w_ref, mcw_ref, mcb_ref, lcw_ref,
                  lcb_ref, gb_ref, x_ref, q_ref, k_ref, v_ref, og_ref, lx_ref, glg_ref, gates_ref, gates_t_ref,
                  h_scr):
    r = _row_ids(pl.program_id(1), tm)
    is_ctx = r < ctx_len
    is_lat = pl.program_id(1) * tm >= ctx_len
    x_ref[...] = jnp.where(is_lat, xl_ref[...], xc_ref[...])
    h_scr[...] = _adaln(x_ref[...], g_ref[...], modl_ref, modc_ref, 0, is_ctx).astype(BF16)

    def conv(cw_ref, cb_ref):
        return lambda y, c: _dwconv(y, cw_ref[:, c:c + PROJ_COLS], cb_ref[:, c:c + PROJ_COLS], is_lat)

    qc, kc, lc = conv(mcw_ref, mcb_ref), conv(mcw_ref, mcb_ref), conv(lcw_ref, lcb_ref)
    _project(h_scr, w_ref, 0, q_ref, lambda y, c: _silu(qc(y, c)))
    _project(h_scr, w_ref, 512, k_ref, lambda y, c: _silu(kc(y, 512 + c)) * (M_DK ** -0.5))
    _project(h_scr, w_ref, 1024, v_ref, lambda y, c: y)
    _project(h_scr, w_ref, 2048, og_ref, lambda y, c: _sigmoid(y))
    _project(h_scr, w_ref, 3072, lx_ref, lc)
    _project(h_scr, w_ref, 4096, glg_ref, lambda y, c: jax.nn.gelu(y))
    gt = _dot(h_scr[...], w_ref[:, 5120:5248]) + gb_ref[...]
    lane = lax.broadcasted_iota(I32, gt.shape, 1)
    gates = jnp.where(lane >= 2 * M_HEADS, _log_sigmoid(gt), gt)
    gates_ref[...] = gates
    gates_t_ref[...] = gates.T


def _in_ab(xc2, xl2, modl, modc, g, w, mcw, mcb, lcw, lcb, gb, bsz, t, ctx_len):
    tm = ROW_TILE
    nt, nct = t // tm, ctx_len // tm
    n = bsz * t
    row = lambda c: pl.BlockSpec((tm, c), lambda b, i: (b * nt + i, 0))
    outs = [(D_MODEL, F32), (512, BF16), (512, BF16), (1024, BF16), (1024, BF16), (1024, BF16), (1024, BF16),
            (LANES, F32)]
    return pl.pallas_call(
        functools.partial(_in_ab_kernel, tm, ctx_len),
        grid=(bsz, nt),
        in_specs=[pl.BlockSpec((tm, D_MODEL), lambda b, i: (b * nct + jnp.minimum(i, nct - 1), 0)),
                  pl.BlockSpec((tm, D_MODEL), lambda b, i: (b * (nt - nct) + jnp.maximum(i - nct, 0), 0)),
                  pl.BlockSpec((1, 8, D_MODEL), lambda b, i: (b, 0, 0)),
                  _resident((8, D_MODEL)), _resident((1, D_MODEL)), _resident(w.shape),
                  _resident(mcw.shape), _resident(mcb.shape), _resident(lcw.shape), _resident(lcb.shape),
                  _resident(gb.shape)],
        out_specs=[row(c) for c, _ in outs] + [pl.BlockSpec((LANES, tm), lambda b, i: (0, b * nt + i))],
        out_shape=[jax.ShapeDtypeStruct((n, c), dt) for c, dt in outs] + [jax.ShapeDtypeStruct((LANES, n), F32)],
        scratch_shapes=[pltpu.VMEM((tm, D_MODEL), BF16)],
        compiler_params=_cparams("arbitrary", "arbitrary"),
        name="in_proj_ab",
    )(xc2, xl2, modl, modc, g, w, mcw, mcb, lcw, lcb, gb)


def _reverse_order(n_ctx_chunks, n_chunks):
    return lambda c: jnp.where(c < n_ctx_chunks, n_ctx_chunks - 1 - c, n_chunks - 1 - (c - n_ctx_chunks))


def _tri_mask(L, reverse):
    row = lax.broadcasted_iota(I32, (L, L), 0)
    col = lax.broadcasted_iota(I32, (L, L), 1)
    return (col >= row) if reverse else (col <= row)


def _bidir_scan(kern, L, row_ins, lane_ins, res_ins, out_cols, scratch, bsz, t, ctx_len, name):
    nc = t // L
    rev = _reverse_order(ctx_len // L, nc)
    fwd = lambda i: i
    n = bsz * t

    def row(c, order):
        return pl.BlockSpec((L, c), lambda b, i: (b * nc + order(i), 0))

    def lanes(r, order):
        return pl.BlockSpec((r, L), lambda b, i: (0, b * nc + order(i)))

    in_specs, args = [], []
    for order in (fwd, rev):
        in_specs += [row(a.shape[1], order) for a in row_ins] + [lanes(r, order) for _, r in lane_ins]
        args += list(row_ins) + [a for a, _ in lane_ins]
    in_specs += [_resident(a.shape) for a in res_ins]
    args += list(res_ins)
    return pl.pallas_call(
        kern,
        grid=(bsz, nc),
        in_specs=in_specs,
        out_specs=[row(c, fwd) for c in out_cols] + [row(c, rev) for c in out_cols],
        out_shape=[jax.ShapeDtypeStruct((n, c), BF16) for c in out_cols] * 2,
        scratch_shapes=scratch,
        compiler_params=_cparams("arbitrary", "arbitrary"),
        name=name,
    )(*args)


def _zero_at_start(*scratch):
    @pl.when(pl.program_id(1) == 0)
    def _():
        for s in scratch:
            s[...] = jnp.zeros_like(s)


def _mlstm_kernel(L, qf, kf, vf, gcf, grf, qb, kb, vb, gcb, grb, of, ob, c_scr, m_scr):
    _zero_at_start(c_scr, m_scr)
    ones = jnp.ones((L, LANES), BF16)
    wide = lambda c, k: jnp.concatenate([c] * k, axis=1)
    er = lax.broadcasted_iota(I32, (LANES, 2 * M_HEADS * LANES), 0)
    ej = jnp.right_shift(lax.broadcasted_iota(I32, (LANES, 2 * M_HEADS * LANES), 1), 7)
    chains = []
    for d, (q_ref, k_ref, v_ref, gc_ref, gr_ref, out_ref) in enumerate(((qf, kf, vf, gcf, grf, of),
                                                                         (qb, kb, vb, gcb, grb, ob))):
        reverse = d == 1
        mask = _tri_mask(L, reverse)
        mask01 = mask.astype(BF16)
        gc = gc_ref[...]
        gr = gr_ref[...]
        bcol = _sel_cols(mask01, gc)
        brow = _sel_rows(gr, mask01)
        glane = lax.broadcasted_iota(I32, (L, LANES), 1)
        src = jnp.where(ej < M_HEADS, M_HEADS * d + ej, M_HEADS + M_HEADS * d + ej)
        s1, s2, _ = _split3(jnp.where(glane < 2 * M_HEADS, gc, bcol))
        sel = (er == src).astype(BF16)
        rep = _dot(s1, sel) + _dot(s2, sel)
        last = 0 if reverse else L - 1
        for h in range(M_HEADS):
            fo, io = 2 * M_HEADS + M_HEADS * d + h, M_HEADS * d + h
            ic, bc = rep[:, h * LANES:(h + 1) * LANES], rep[:, (M_HEADS + h) * LANES:(M_HEADS + h + 1) * LANES]
            r1 = gr[io:io + 1, :] - brow[fo:fo + 1, :]
            m = m_scr[d, h:h + 1, :]
            g = bc + m
            mx = jnp.max(jnp.where(mask, r1, -jnp.inf), axis=1, keepdims=True)
            mt = jnp.maximum(g, bc + mx)
            p = jnp.exp(jnp.where(mask, wide(bc - mt, L // LANES) + r1, -jnp.inf))
            bl = bc[last:last + 1, :]
            a_s = bl - bc + ic
            m_new = jnp.maximum(bl + m, jnp.max(a_s, axis=0, keepdims=True))
            chains.append(dict(
                d=d, h=h, out_ref=out_ref, mt=mt, w_inter=jnp.exp(g - mt), p=p,
                m_new=m_new, decay=jnp.exp(bl + m - m_new), ws=jnp.exp(a_s - m_new),
                qh=q_ref[:, h * M_DK:(h + 1) * M_DK], kh=k_ref[:, h * M_DK:(h + 1) * M_DK],
                vx=jnp.concatenate([v_ref[:, h * M_DV:(h + 1) * M_DV], ones], axis=1)))
    for ch in chains:
        ch["sc"] = (_dot_nt(ch["qh"], ch["kh"]) * ch["p"]).astype(BF16)
    for ch in chains:
        cx = c_scr[ch["d"], ch["h"]]
        numx = wide(ch["w_inter"], 1 + M_DV // LANES) * _dot(ch["qh"], cx.astype(BF16)) + _dot(ch["sc"], ch["vx"])
        den = numx[:, M_DV:]
        rinv = 1.0 / jnp.maximum(jnp.abs(den), jnp.exp(-ch["mt"]))
        hh = numx[:, 0:M_DV] * wide(rinv, M_DV // LANES)
        ch["out_ref"][:, ch["h"] * M_DV:(ch["h"] + 1) * M_DV] = hh.astype(BF16)
    for ch in chains:
        d, h = ch["d"], ch["h"]
        kw = (ch["kh"].astype(F32) * ch["ws"]).astype(BF16)
        c_scr[d, h] = wide(ch["decay"], 1 + M_DV // LANES) * c_scr[d, h] + _dot_tn(kw, ch["vx"])
        m_scr[d, h:h + 1, :] = ch["m_new"]


def _mlstm(q, k, v, gcol, grow, bsz, t, ctx_len):
    L = M_CHUNK
    scratch = [pltpu.VMEM((2, M_HEADS, M_DK, M_DV + LANES), F32), pltpu.VMEM((2, 8, LANES), F32)]
    return _bidir_scan(functools.partial(_mlstm_kernel, L), L, [q, k, v, gcol], [(grow, 16)], [], [1024], scratch,
                       bsz, t, ctx_len, "mlstm_scan")


def _lru_kernel(L, uf, ub_, wa_ref, wx_ref, ba_ref, bx_ref, lam_ref, of, ob, a_scr, b_scr, h_scr):
    _zero_at_start(h_scr)
    R, C = L // 8, LRU_BLOCKS * LRU_BW
    i0, i1 = lax.broadcasted_iota(I32, (L, L), 0), lax.broadcasted_iota(I32, (L, L), 1)
    regroup = (i1 == jnp.bitwise_and(i0, 7) * R + jnp.right_shift(i0, 3)).astype(BF16)
    ungroup = (i0 == jnp.bitwise_and(i1, 7) * R + jnp.right_shift(i1, 3)).astype(BF16)
    sub = lax.broadcasted_iota(I32, (8, 1), 0)
    dirs = ((uf, of), (ub_, ob))
    for d, (u_ref, _) in enumerate(dirs):
        u = _dot(regroup, u_ref[...])
        ub = u.astype(BF16)
        sp = -LRU_C * _softplus(-lam_ref[d])
        for n in range(LRU_BLOCKS):
            sl = slice(n * LRU_BW, (n + 1) * LRU_BW)
            rg = _sigmoid(_dot(ub[:, sl], wa_ref[d, n]) + ba_ref[d, :, sl])
            ig = _sigmoid(_dot(ub[:, sl], wx_ref[d, n]) + bx_ref[d, :, sl])
            log_a = rg * sp[:, sl]
            a = jnp.exp(log_a)
            a_scr[d, :, sl] = a
            b_scr[d, :, sl] = jnp.sqrt(-jnp.tanh(log_a) * (a * a + 1.0)) * (ig * u[:, sl])
    hs = [jnp.zeros((8, C), F32), jnp.zeros((8, C), F32)]
    ps = [jnp.ones((8, C), F32), jnp.ones((8, C), F32)]
    for j in range(R):
        for d in range(2):
            rows = slice(8 * j, 8 * j + 8) if d == 0 else slice(8 * (R - 1 - j), 8 * (R - j))
            a = a_scr[d, rows, :]
            hs[d] = a * hs[d] + b_scr[d, rows, :]
            ps[d] = a * ps[d]
            b_scr[d, rows, :] = hs[d]
            a_scr[d, rows, :] = ps[d]
    for d, (_, out_ref) in enumerate(dirs):
        c = h_scr[d:d + 1, :]
        cin = jnp.zeros((8, C), F32)
        for s in (range(8) if d == 0 else reversed(range(8))):
            cin = jnp.where(sub == s, c, cin)
            c = ps[d][s:s + 1, :] * c + hs[d][s:s + 1, :]
        h_scr[d:d + 1, :] = c
        h = b_scr[d].reshape(R, 8, C) + a_scr[d].reshape(R, 8, C) * cin[None]
        out_ref[...] = _dot(ungroup, h.reshape(L, C).astype(BF16)).astype(BF16)


def _lru(u, wa, wx, ba, bx, lam, bsz, t, ctx_len):
    L = LRU_TILE
    scratch = [pltpu.VMEM((2, L, 1024), F32), pltpu.VMEM((2, L, 1024), F32), pltpu.VMEM((8, 1024), F32)]
    return _bidir_scan(functools.partial(_lru_kernel, L), L, [u], [], [wa, wx, ba, bx, lam], [1024], scratch,
                       bsz, t, ctx_len, "rglru_scan")


def _out_tail(tm, ctx_len, a1, a2, x_ref, modl_ref, modc_ref, g_ref, w_ref, xo_ref):
    is_ctx = _row_ids(pl.program_id(1), tm) < ctx_len
    half = w_ref.shape[0] // 2
    y = _dot(a1.astype(BF16), w_ref[0:half, :]) + _dot(a2.astype(BF16), w_ref[half:, :])
    xo_ref[...] = x_ref[...] + _mod(modl_ref, modc_ref, 2, is_ctx) * _rms(y, g_ref[...])


def _out_ab_kernel(tm, ctx_len, hf, hb, og, lf, lb, glg, nw, x_ref, modl_ref, modc_ref, g_ref, w_ref, xo_ref):
    hm = _head_rms(hf[...].astype(F32) + hb[...].astype(F32), nw[...], M_HEADS) * og[...].astype(F32)
    hl = (lf[...].astype(F32) + lb[...].astype(F32)) * glg[...].astype(F32)
    _out_tail(tm, ctx_len, hm, hl, x_ref, modl_ref, modc_ref, g_ref, w_ref, xo_ref)


def _out_cd_kernel(tm, ctx_len, gf, gb, grs, yf, yb, sx, zs, gnw, dsk, snw, x_ref, modl_ref, modc_ref, g_ref,
                   w_ref, xo_ref):
    og = _head_rms(gf[...].astype(F32) + gb[...].astype(F32), gnw[...], G_HEADS) * grs[...].astype(F32)
    ys = yf[...].astype(F32) + yb[...].astype(F32) + dsk[...] * sx[...].astype(F32)
    ys = _rms(ys * zs[...].astype(F32), snw[...])
    _out_tail(tm, ctx_len, og, ys, x_ref, modl_ref, modc_ref, g_ref, w_ref, xo_ref)


def _out_proj(kern, name, acts, vecs, x2, modl, modc, g, w, bsz, t, ctx_len):
    tm = _big_row_tile(t)
    nt = t // tm
    row = lambda c: pl.BlockSpec((tm, c), lambda b, i: (b * nt + i, 0))
    return pl.pallas_call(
        functools.partial(kern, tm, ctx_len),
        grid=(bsz, nt),
        in_specs=[row(1024)] * len(acts) + [_resident((1, 1024))] * len(vecs)
        + [row(D_MODEL), pl.BlockSpec((1, 8, D_MODEL), lambda b, i: (b, 0, 0)),
           _resident((8, D_MODEL)), _resident((1, D_MODEL)), _resident(w.shape)],
        out_specs=row(D_MODEL),
        out_shape=jax.ShapeDtypeStruct(x2.shape, F32),
        compiler_params=_cparams("arbitrary", "arbitrary"),
        name=name,
    )(*acts, *vecs, x2, modl, modc, g, w)


def _ffn_kernel(tm, ctx_len, x_ref, modl_ref, modc_ref, g2_ref, g3_ref, w1_ref, w2_ref, xo_ref):
    is_ctx = _row_ids(pl.program_id(1), tm) < ctx_len
    x = x_ref[...]
    h = _adaln(x, g2_ref[...], modl_ref, modc_ref, 3, is_ctx).astype(BF16)
    ff = w2_ref.shape[0]
    acc = jnp.zeros((tm, D_MODEL), F32)
    for j in range(ff // FF_STEP):
        lo = j * FF_STEP
        gj = _dot(h, w1_ref[:, lo:lo + FF_STEP])
        uj = _dot(h, w1_ref[:, ff + lo:ff + lo + FF_STEP])
        acc = acc + _dot((_silu(gj) * uj).astype(BF16), w2_ref[lo:lo + FF_STEP, :])
    xo_ref[...] = x + _mod(modl_ref, modc_ref, 5, is_ctx) * _rms(acc, g3_ref[...])


def _ffn(x2, modl, modc, g2, g3, w1, w2, bsz, t, ctx_len):
    tm = _big_row_tile(t)
    nt = t // tm
    row = lambda c: pl.BlockSpec((tm, c), lambda b, i: (b * nt + i, 0))
    return pl.pallas_call(
        functools.partial(_ffn_kernel, tm, ctx_len),
        grid=(bsz, nt),
        in_specs=[row(D_MODEL), pl.BlockSpec((1, 8, D_MODEL), lambda b, i: (b, 0, 0)), _resident((8, D_MODEL)),
                  _resident((1, D_MODEL)), _resident((1, D_MODEL)), _resident(w1.shape), _resident(w2.shape)],
        out_specs=row(D_MODEL),
        out_shape=jax.ShapeDtypeStruct(x2.shape, F32),
        compiler_params=_cparams("arbitrary", "arbitrary"),
        name="dense_swiglu",
    )(x2, modl, modc, g2, g3, w1, w2)


def _in_cd_kernel(tm, ctx_len, x_ref, modl_ref, modc_ref, g_ref, w_ref, cw_ref, cb_ref, dtb_ref, aneg_ref,
                  gq_ref, gk_ref, gv_ref, gr_ref, z_ref, sx_ref, sb_ref, sc_ref, ga_ref, dts_ref, dts_t_ref, h_scr):
    r = _row_ids(pl.program_id(1), tm)
    is_ctx = r < ctx_len
    h_scr[...] = _adaln(x_ref[...], g_ref[...], modl_ref, modc_ref, 0, is_ctx).astype(BF16)
    is_lat = pl.program_id(1) * tm >= ctx_len

    def conv_silu(col0):
        return lambda y, c: _silu(_dwconv(y, cw_ref[:, col0 + c:col0 + c + PROJ_COLS],
                                          cb_ref[:, col0 + c:col0 + c + PROJ_COLS], is_lat))

    _project(h_scr, w_ref, 0, gq_ref, lambda y, c: y * (G_DK ** -0.5))
    _project(h_scr, w_ref, 512, gk_ref, lambda y, c: y)
    _project(h_scr, w_ref, 1024, gv_ref, lambda y, c: y)
    _project(h_scr, w_ref, 2048, gr_ref, lambda y, c: _silu(y))
    _project(h_scr, w_ref, 3072, z_ref, lambda y, c: _silu(y))
    _project(h_scr, w_ref, 4096, sx_ref, conv_silu(0))
    _project(h_scr, w_ref, 5120, sb_ref, conv_silu(1024))
    _project(h_scr, w_ref, 5376, sc_ref, conv_silu(1280))
    ga_ref[...] = _dot(h_scr[...], w_ref[:, 5632:5760])
    dt = _softplus(_dot(h_scr[...], w_ref[:, 5760:5888]) + dtb_ref[...])
    lane = lax.broadcasted_iota(I32, dt.shape, 1)
    dts = jnp.where(lane < 2 * S_HEADS, dt, dt * aneg_ref[...])
    dts_ref[...] = dts
    dts_t_ref[...] = dts.T


def _in_cd(x2, modl, modc, g, w, cw, cb, dtb, aneg, bsz, t, ctx_len):
    tm = ROW_TILE
    nt = t // tm
    n = bsz * t
    row = lambda c: pl.BlockSpec((tm, c), lambda b, i: (b * nt + i, 0))
    outs = [(512, BF16), (512, BF16), (1024, BF16), (1024, BF16), (1024, BF16), (1024, BF16), (256, BF16),
            (256, BF16), (LANES, F32), (LANES, F32)]
    return pl.pallas_call(
        functools.partial(_in_cd_kernel, tm, ctx_len),
        grid=(bsz, nt),
        in_specs=[row(D_MODEL), pl.BlockSpec((1, 8, D_MODEL), lambda b, i: (b, 0, 0)), _resident((8, D_MODEL)),
                  _resident((1, D_MODEL)), _resident(w.shape), _resident(cw.shape), _resident(cb.shape),
                  _resident(dtb.shape), _resident(aneg.shape)],
        out_specs=[row(c) for c, _ in outs] + [pl.BlockSpec((LANES, tm), lambda b, i: (0, b * nt + i))],
        out_shape=[jax.ShapeDtypeStruct((n, c), dt) for c, dt in outs] + [jax.ShapeDtypeStruct((LANES, n), F32)],
        scratch_shapes=[pltpu.VMEM((tm, D_MODEL), BF16)],
        compiler_params=_cparams("arbitrary", "arbitrary"),
        name="in_proj_cd",
    )(x2, modl, modc, g, w, cw, cb, dtb, aneg)


def _gla_kernel(L, C, qf, kf, vf, gaf, qb, kb, vb, gab, awh_ref, awl_ref, ab_ref, of, ob, s_scr):
    _zero_at_start(s_scr)
    nb = L // C
    chains = []
    for d, (q_ref, k_ref, v_ref, ga_ref, out_ref) in enumerate(((qf, kf, vf, gaf, of), (qb, kb, vb, gab, ob))):
        reverse = d == 1
        mask = _tri_mask(L, reverse)
        g1, g2, _ = _split3(ga_ref[...])
        pre = _dot(g1, awh_ref[d]) + _dot(g2, awh_ref[d]) + _dot(g1, awl_ref[d])
        lg = _log_sigmoid(pre + ab_ref[d]) * (1.0 / G_TAU)
        ball = _sel_cols(mask.astype(BF16), lg)
        for h in range(G_HEADS):
            ks = slice(h * G_DK, (h + 1) * G_DK)
            b = ball[:, ks] * LOG2E
            qh = q_ref[:, ks].astype(F32)
            inter = _dot_nt((qh * jnp.exp2(b)).astype(BF16), s_scr[d, h].astype(BF16))
            chains.append(dict(d=d, h=h, reverse=reverse, mask=mask, b=b, qh=qh, kh=k_ref[:, ks].astype(F32),
                               vh=v_ref[:, h * G_DV:(h + 1) * G_DV], inter=inter, out_ref=out_ref, blocks=[]))
    for i in range(nb):
        lo, hi = i * C, (i + 1) * C
        for ch in chains:
            b = ch["b"]
            if ch["reverse"]:
                bref = b[hi:hi + 1, :] if i < nb - 1 else jnp.zeros((1, G_DK), F32)
                v0, v1 = lo // LANES * LANES, L
            else:
                bref = b[lo - 1:lo, :] if i > 0 else jnp.zeros((1, G_DK), F32)
                v0, v1 = 0, min(L, -(-hi // LANES) * LANES)
            qi = (ch["qh"][lo:hi, :] * jnp.exp2(b[lo:hi, :] - bref)).astype(BF16)
            ki = (ch["kh"][v0:v1] * jnp.exp2(bref - b[v0:v1])).astype(BF16)
            pieces = [jnp.zeros((C, v0), F32)] if v0 else []
            pieces.append(_dot_nt(qi, ki))
            if v1 < L:
                pieces.append(jnp.zeros((C, L - v1), F32))
            ch["blocks"].append(jnp.concatenate(pieces, axis=1) if len(pieces) > 1 else pieces[0])
    for ch in chains:
        att = jnp.where(ch["mask"], jnp.concatenate(ch["blocks"], axis=0), 0.0).astype(BF16)
        oh = ch["inter"] + _dot(att, ch["vh"])
        ch["out_ref"][:, ch["h"] * G_DV:(ch["h"] + 1) * G_DV] = oh.astype(BF16)
    for ch in chains:
        d, h, b = ch["d"], ch["h"], ch["b"]
        last = 0 if ch["reverse"] else L - 1
        bl = b[last:last + 1, :]
        kd = (ch["kh"] * jnp.exp2(bl - b)).astype(BF16)
        s_scr[d, h] = s_scr[d, h] * jnp.exp2(bl) + _dot_tn(ch["vh"], kd)


def _gla(q, k, v, ga, awh, awl, ab, bsz, t, ctx_len):
    L, C = G_CHUNK, G_SUB
    scratch = [pltpu.VMEM((2, G_HEADS, G_DV, G_DK), F32)]
    return _bidir_scan(functools.partial(_gla_kernel, L, C), L, [q, k, v, ga], [], [awh, awl, ab], [1024], scratch,
                       bsz, t, ctx_len, "gla_scan")


def _ssd_kernel(L, xf, bf, cf, gcf, grf, xb, bb, cb_, gcb, grb, of, ob, s_scr):
    _zero_at_start(s_scr)
    er = lax.broadcasted_iota(I32, (LANES, S_HEADS * S_P), 0)
    ec = jnp.right_shift(lax.broadcasted_iota(I32, (LANES, S_HEADS * S_P), 1), 6)
    glane = lax.broadcasted_iota(I32, (L, LANES), 1)
    lane = lax.broadcasted_iota(I32, (L, 2 * S_P), 1)
    dirs, chains = [], []
    for d, (x_ref, b_ref, c_ref, gc_ref, gr_ref, out_ref) in enumerate(((xf, bf, cf, gcf, grf, of),
                                                                         (xb, bb, cb_, gcb, grb, ob))):
        reverse = d == 1
        mask = _tri_mask(L, reverse)
        mask01 = mask.astype(BF16)
        gc = gc_ref[...]
        gr = gr_ref[...]
        bcol = _sel_cols(mask01, gc)
        brow = _sel_rows(gr, mask01)
        last = 0 if reverse else L - 1
        dt_o, la_o = S_HEADS * d, 2 * S_HEADS + S_HEADS * d
        sel_la = (er == ec + la_o).astype(BF16)
        x = x_ref[...]
        bla = jnp.where((glane >= la_o) & (glane < la_o + S_HEADS), bcol, 0.0)
        dt_at_la = pltpu.roll(gc, 2 * S_HEADS, 1)
        bl = bla[last:last + 1, :]
        ebx = _dot(jnp.exp(bla).astype(BF16), sel_la)
        wx = _dot((jnp.exp(bl - bla) * dt_at_la).astype(BF16), sel_la)
        dirs.append(dict(d=d, mask=mask, gr=gr, x=x, ebx=ebx, last=last, dt_o=dt_o, la_o=la_o, out_ref=out_ref,
                         xw=(x.astype(F32) * wx).astype(BF16),
                         bcol2=bcol * LOG2E, brow2=brow * LOG2E))
        for g in range(S_GROUPS):
            ns = slice(g * S_N, (g + 1) * S_N)
            chains.append(dict(dr=dirs[-1], g=g, bg=b_ref[:, ns], cg=c_ref[:, ns],
                               gs=slice(g * S_HPG * S_P, (g + 1) * S_HPG * S_P)))
    for ch in chains:
        dr = ch["dr"]
        ch["cb"] = _dot_nt(ch["cg"], ch["bg"])
        ch["inter"] = _dot(ch["cg"], s_scr[dr["d"], ch["g"]].astype(BF16)) * dr["ebx"][:, ch["gs"]]
    for ch in chains:
        dr, g = ch["dr"], ch["g"]
        parts = []
        for pr in range(S_HPG // 2):
            e0 = g * S_HPG + 2 * pr
            xp = dr["x"][:, e0 * S_P:(e0 + 2) * S_P]
            res = []
            for e in (e0, e0 + 1):
                col = dr["la_o"] + e
                bc, br = dr["bcol2"][:, col:col + 1], dr["brow2"][col:col + 1, :]
                dtr = dr["gr"][dr["dt_o"] + e:dr["dt_o"] + e + 1, :]
                att = (ch["cb"] * dtr * jnp.exp2(jnp.where(dr["mask"], bc - br, -jnp.inf))).astype(BF16)
                res.append(_dot(att, xp))
            parts.append(jnp.where(lane < S_P, res[0], res[1]))
        dr["out_ref"][:, ch["gs"]] = (jnp.concatenate(parts, axis=1) + ch["inter"]).astype(BF16)
    for ch in chains:
        dr, g, gs = ch["dr"], ch["g"], ch["gs"]
        s_scr[dr["d"], g] = (s_scr[dr["d"], g] * dr["ebx"][dr["last"]:dr["last"] + 1, gs]
                             + _dot_tn(ch["bg"], dr["xw"][:, gs]))


def _ssd(x, bm, cm, gcol, grow, bsz, t, ctx_len):
    L = S_CHUNK
    scratch = [pltpu.VMEM((2, S_GROUPS, S_N, S_HPG * S_P), F32)]
    return _bidir_scan(functools.partial(_ssd_kernel, L), L, [x, bm, cm, gcol], [(grow, 4 * S_HEADS)], [], [1024],
                       scratch,
                       bsz, t, ctx_len, "ssd_scan")


def _router_kernel(x_ref, modl_ref, g_ref, rw_ref, rb_ref, h_ref, idx_ref, gate_ref, cnt_ref, base_scr):
    @pl.when((pl.program_id(0) == 0) & (pl.program_id(1) == 0))
    def _():
        base_scr[...] = jnp.zeros_like(base_scr)

    h = _rms(x_ref[...], g_ref[...]) * (1.0 + modl_ref[0, 4:5, :]) + modl_ref[0, 3:4, :]
    h_ref[...] = _pack_rows(h)
    h1, h2, _ = _split3(h)
    logits = _dot(h1, rw_ref[0]) + _dot(h2, rw_ref[0]) + _dot(h1, rw_ref[1]) + rb_ref[...]
    lane = lax.broadcasted_iota(I32, logits.shape, 1)
    logits = jnp.where(lane < N_EXPERTS, logits, -jnp.inf)
    lanef = lane.astype(F32)
    m1 = jnp.max(logits, axis=1, keepdims=True)
    i1 = jnp.min(jnp.where(logits == m1, lanef, float(LANES)), axis=1, keepdims=True)
    rest = jnp.where(lanef == i1, -jnp.inf, logits)
    m2 = jnp.max(rest, axis=1, keepdims=True)
    i2 = jnp.min(jnp.where(rest == m2, lanef, float(LANES)), axis=1, keepdims=True)
    e = jnp.exp(m2 - m1)
    g1 = 1.0 / (1.0 + e)
    gate_ref[...] = jnp.where(lane == 0, g1, jnp.where(lane == 1, e * g1, 0.0))
    tm = logits.shape[0]
    oh1, oh2 = (lanef == i1).astype(F32), (lanef == i2).astype(F32)
    earlier = (lax.broadcasted_iota(I32, (tm, tm), 1) < lax.broadcasted_iota(I32, (tm, tm), 0)).astype(BF16)
    seen = base_scr[0:1, :] + _dot(earlier, (oh1 + oh2).astype(BF16))
    r1 = jnp.sum(oh1 * seen, axis=1, keepdims=True)
    r2 = jnp.sum(oh2 * seen, axis=1, keepdims=True)
    base_scr[0:1, :] = base_scr[0:1, :] + jnp.sum(oh1 + oh2, axis=0, keepdims=True)
    cnt_ref[...] = base_scr[...]
    idx_ref[...] = jnp.where(lane == 0, i1, jnp.where(lane == 1, i2, jnp.where(lane == 2, r1, jnp.where(
        lane == 3, r2, 0.0)))).astype(I32)


def _router(x2, modl, g, rw, rb, bsz, t, ctx_len):
    tm = ROW_TILE
    nt, nct = t // tm, ctx_len // tm
    nl = bsz * (t - ctx_len)
    nlt = nt - nct
    orow = lambda c: pl.BlockSpec((tm, c), lambda b, i: (b * nlt + i, 0))
    return pl.pallas_call(
        _router_kernel,
        grid=(bsz, nlt),
        in_specs=[pl.BlockSpec((tm, D_MODEL), lambda b, i: (b * nt + nct + i, 0)),
                  pl.BlockSpec((1, 8, D_MODEL), lambda b, i: (b, 0, 0)),
                  _resident((1, D_MODEL)), _resident(rw.shape), _resident(rb.shape)],
        out_specs=[orow(D_MODEL // 2), orow(LANES), orow(LANES), pl.BlockSpec((8, LANES), lambda b, i: (0, 0))],
        out_shape=[jax.ShapeDtypeStruct((nl, D_MODEL // 2), jnp.uint32), jax.ShapeDtypeStruct((nl, LANES), I32),
                   jax.ShapeDtypeStruct((nl, LANES), F32), jax.ShapeDtypeStruct((8, LANES), F32)],
        scratch_shapes=[pltpu.VMEM((8, LANES), F32)],
        compiler_params=_cparams("arbitrary", "arbitrary"),
        name="moe_router",
    )(x2, modl, g, rw, rb)


def _gather_rows(table, idx):
    n_rows, d = idx.shape[0], table.shape[1]
    n_workers = SC_CORES * SC_SUBCORES
    per_worker = n_rows // n_workers
    chunk = SC_GATHER_ROWS
    assert per_worker * n_workers == n_rows and per_worker % chunk == 0
    mesh = plsc.VectorSubcoreMesh(core_axis_name="c", subcore_axis_name="s")

    @functools.partial(
        pl.kernel, mesh=mesh, out_type=jax.ShapeDtypeStruct((n_rows, d), table.dtype),
        scratch_types=[pltpu.VMEM((chunk,), I32), pltpu.VMEM((chunk, d), table.dtype), pltpu.SemaphoreType.DMA],
        name="sc_gather_rows")
    def gather(table_hbm, idx_hbm, out_hbm, idx_v, rows_v, sem):
        base = (lax.axis_index("s") * SC_CORES + lax.axis_index("c")) * per_worker

        @pl.loop(0, per_worker // chunk)
        def _(j):
            off = pl.multiple_of(base + j * chunk, 8)
            pltpu.sync_copy(idx_hbm.at[pl.ds(off, chunk)], idx_v)
            pltpu.async_copy(table_hbm.at[idx_v], rows_v, sem).wait()
            pltpu.sync_copy(rows_v, out_hbm.at[pl.ds(off, chunk)])

    return gather(table, idx)


def _dispatch_rows(table, slot, n_out):
    assert TOP_K == 2
    n_assign, (n_tok, d) = slot.shape[0], table.shape
    n_workers = SC_CORES * SC_SUBCORES
    per_worker = n_out // n_workers
    chunk, vec = SC_GATHER_ROWS, SC_LANES
    assert per_worker * n_workers == n_out and per_worker % chunk == 0 and n_assign % vec == 0
    assert n_out < 3 * n_tok
    mesh = plsc.VectorSubcoreMesh(core_axis_name="c", subcore_axis_name="s")

    @functools.partial(
        pl.kernel, mesh=mesh, out_type=jax.ShapeDtypeStruct((n_out, d), table.dtype),
        scratch_types=[pltpu.VMEM((n_assign,), I32), pltpu.VMEM((per_worker,), I32),
                       pltpu.VMEM((chunk, d), table.dtype), pltpu.SemaphoreType.DMA],
        compiler_params=pltpu.CompilerParams(needs_layout_passes=False),
        name="sc_dispatch_rows")
    def dispatch(table_hbm, slot_hbm, out_hbm, slots_v, tok_v, rows_v, sem):
        base = (lax.axis_index("s") * SC_CORES + lax.axis_index("c")) * per_worker
        pltpu.sync_copy(slot_hbm, slots_v)
        lane = lax.broadcasted_iota(I32, (vec,), 0)

        @pl.loop(0, per_worker // vec)
        def _(j):
            p = base + j * vec + lane
            p = jnp.where(p >= n_tok, p - n_tok, p)
            tok_v[pl.ds(j * vec, vec)] = jnp.where(p >= n_tok, p - n_tok, p)

        @pl.loop(0, n_assign // vec)
        def _(j):
            s = slots_v[pl.ds(j * vec, vec)] - base
            mine = (s >= 0) & (s < per_worker)
            tok = lax.shift_right_logical(j * vec + lane, 1)
            plsc.store_scatter(tok_v, [jnp.where(mine, s, 0)], tok, mask=mine)

        @pl.loop(0, per_worker // chunk)
        def _(j):
            off = pl.multiple_of(j * chunk, 8)
            pltpu.async_copy(table_hbm.at[tok_v.at[pl.ds(off, chunk)]], rows_v, sem).wait()
            pltpu.sync_copy(rows_v, out_hbm.at[pl.ds(base + off, chunk)])

    return dispatch(table, slot)


def _moe_kernel(be_ref, nu_ref, nv_ref, x_ref, w1g_ref, w1u_ref, w2_ref, y_ref, xb_ref, acc_ref):
    i, j = pl.program_id(0), pl.program_id(1)
    bm = x_ref.shape[0]

    def swiglu_rows(rows):
        x = xb_ref[0:rows, :]
        a = (_silu(_dot(x, w1g_ref[0].astype(BF16))) * _dot(x, w1u_ref[0].astype(BF16))).astype(BF16)
        acc_ref[0:rows, :] += _dot(a, w2_ref[0].astype(BF16))

    @pl.when(i < nu_ref[0])
    def _():
        @pl.when(j == 0)
        def _():
            acc_ref[...] = jnp.zeros_like(acc_ref)
            xb_ref[...] = _unpack_rows(x_ref[...]).astype(BF16)

        @pl.when(nv_ref[i] > bm // 2)
        def _():
            swiglu_rows(bm)

        @pl.when(nv_ref[i] <= bm // 2)
        def _():
            swiglu_rows(bm // 2)

        @pl.when(j == pl.num_programs(1) - 1)
        def _():
            y_ref[...] = _pack_rows(acc_ref[...])


def _moe_experts(xs, w1, w2, block_e, n_used, n_valid, n_blocks):
    bm, fc = MOE_ROWS, MOE_FF
    nff = FF_EXPERT // fc
    used = lambda i, nu: jnp.minimum(i, nu[0] - 1)
    ffi = lambda i, j, nu: jnp.where(i < nu[0], j, nff - 1)
    grid_spec = pltpu.PrefetchScalarGridSpec(
        num_scalar_prefetch=3,
        grid=(n_blocks, nff),
        in_specs=[pl.BlockSpec((bm, D_MODEL // 2), lambda i, j, be, nu, nv: (used(i, nu), 0)),
                  pl.BlockSpec((1, D_MODEL, fc), lambda i, j, be, nu, nv: (be[used(i, nu)], 0, ffi(i, j, nu))),
                  pl.BlockSpec((1, D_MODEL, fc),
                               lambda i, j, be, nu, nv: (be[used(i, nu)], 0, nff + ffi(i, j, nu))),
                  pl.BlockSpec((1, fc, D_MODEL), lambda i, j, be, nu, nv: (be[used(i, nu)], ffi(i, j, nu), 0))],
        out_specs=pl.BlockSpec((bm, D_MODEL // 2), lambda i, j, be, nu, nv: (used(i, nu), 0)),
        scratch_shapes=[pltpu.VMEM((bm, D_MODEL), BF16), pltpu.VMEM((bm, D_MODEL), F32)],
    )
    return pl.pallas_call(
        _moe_kernel,
        grid_spec=grid_spec,
        out_shape=jax.ShapeDtypeStruct((n_blocks * bm, D_MODEL // 2), jnp.uint32),
        compiler_params=_cparams("arbitrary", "arbitrary"),
        name="moe_experts",
    )(block_e, n_used, n_valid, xs, w1, w1, w2)


def _combine_kernel(y0_ref, y1_ref, gate_ref, x_ref, modl_ref, g_ref, o_ref):
    gt = gate_ref[...]
    y = gt[:, 0:1] * _unpack_rows(y0_ref[...]) + gt[:, 1:2] * _unpack_rows(y1_ref[...])
    o_ref[...] = x_ref[...] + modl_ref[0, 5:6, :] * _rms(y, g_ref[...])


def _combine(yk, gates, x2, modl, g, bsz, t, ctx_len):
    tm = ROW_TILE
    nt, nct = t // tm, ctx_len // tm
    nlt = nt - nct
    nl = bsz * (t - ctx_len)
    orow = lambda c: pl.BlockSpec((tm, c), lambda b, i: (b * nlt + i, 0))
    second = pl.BlockSpec((tm, D_MODEL // 2), lambda b, i: (nl // tm + b * nlt + i, 0))
    return pl.pallas_call(
        _combine_kernel,
        grid=(bsz, nlt),
        in_specs=[orow(D_MODEL // 2), second, orow(LANES),
                  pl.BlockSpec((tm, D_MODEL), lambda b, i: (b * nt + nct + i, 0)),
                  pl.BlockSpec((1, 8, D_MODEL), lambda b, i: (b, 0, 0)), _resident((1, D_MODEL))],
        out_specs=orow(D_MODEL),
        out_shape=jax.ShapeDtypeStruct((nl, D_MODEL), F32),
        compiler_params=_cparams("arbitrary", "arbitrary"),
        name="moe_combine",
    )(yk, yk, gates, x2, modl, g)


def _routing_tables(experts, ranks, counts):
    bm = MOE_ROWS
    n_assign = experts.size
    padded = (counts + bm - 1) // bm * bm
    ends_p = jnp.cumsum(padded)
    pstarts = ends_p - padded
    onehot = experts[..., None] == jnp.arange(N_EXPERTS, dtype=I32)
    slot = (jnp.sum(jnp.where(onehot, pstarts, 0), axis=-1) + ranks).astype(I32)
    n_blocks = n_assign // bm + N_EXPERTS
    block_e = jnp.minimum(jnp.sum(ends_p[None, :] <= (jnp.arange(n_blocks, dtype=I32) * bm)[:, None], axis=1),
                          N_EXPERTS - 1).astype(I32)
    n_used = (ends_p[-1] // bm).astype(I32).reshape(1)
    sel = block_e[:, None] == jnp.arange(N_EXPERTS, dtype=I32)
    rows_left = jnp.sum(jnp.where(sel, counts + pstarts, 0), axis=1) - jnp.arange(n_blocks, dtype=I32) * bm
    n_valid = jnp.clip(rows_left, 0, bm).astype(I32)
    return slot, block_e, n_used, n_valid, n_blocks


def kernel(x, c, ctx, c_ctx, mod_w, mod_b, norm_g, ab_in_w, m_conv_w, m_conv_b, m_gate_b, m_norm_w, l_conv_w,
           l_conv_b, l_wa, l_ba, l_wx, l_bx, l_lam, ab_out_w, ffn_w1, ffn_w2, cd_in_w, g_alpha_w, g_alpha_b,
           g_norm_w, s_conv_w, s_conv_b, s_dt_bias, s_A_log, s_D, s_norm_w, cd_out_w, router_w, router_b,
           moe_w1, moe_w2):
    bsz, seq, dm = x.shape
    ctx_len = ctx.shape[1]
    t = ctx_len + seq
    n = bsz * t
    assert ctx_len == ROW_TILE and seq % ROW_TILE == 0 and ROW_TILE % GRID_W == 0

    c_all = jnp.concatenate([c, c_ctx[None, :], jnp.zeros((-(bsz + 1) % 8, dm), F32)], axis=0)
    mods = _modulation(c_all, mod_w, mod_b).reshape(mod_w.shape[0], c_all.shape[0], 6, dm)
    pad2 = jnp.zeros((bsz, 2, dm), F32)

    def layer_mods(layer):
        modl = jnp.concatenate([mods[layer, :bsz], pad2], axis=1)
        modc = jnp.concatenate([mods[layer, bsz], pad2[0]], axis=0)
        return modl, modc

    modl, modc = layer_mods(0)
    g = norm_g[0]
    w = ab_in_w[0]
    o_qk, o_v, o_o, o_gt, o_lx, o_lg = 0, 1024, 2048, 3072, 3088, 4112
    w_in = jnp.concatenate([w[:, o_qk:o_v], w[:, o_v:o_o], w[:, o_o:o_gt], w[:, o_lx:o_lg], w[:, o_lg:],
                            w[:, o_gt:o_lx], jnp.zeros((dm, LANES - 16), F32)], axis=1).astype(BF16)
    gb = jnp.concatenate([m_gate_b[0].reshape(1, 16), jnp.zeros((1, LANES - 16), F32)], axis=1)
    x2, q, k, v, og, lx, glg, gates, gates_t = _in_ab(
        ctx.reshape(bsz * ctx_len, dm), x.reshape(bsz * seq, dm), modl, modc, g[0:1], w_in, m_conv_w[0],
        m_conv_b[0][None], l_conv_w[0], l_conv_b[0][None], gb, bsz, t, ctx_len)
    hm_f, hm_b = _mlstm(q, k, v, gates, gates_t, bsz, t, ctx_len)
    hl_f, hl_b = _lru(lx, l_wa[0].astype(BF16), l_wx[0].astype(BF16), l_ba[0][:, None], l_bx[0][:, None],
                      l_lam[0][:, None], bsz, t, ctx_len)
    x2 = _out_proj(_out_ab_kernel, "out_proj_ab", [hm_f, hm_b, og, hl_f, hl_b, glg], [m_norm_w[0][None]],
                   x2, modl, modc, g[1:2], ab_out_w[0].astype(BF16), bsz, t, ctx_len)
    x2 = _ffn(x2, modl, modc, g[2:3], g[3:4], ffn_w1[0].astype(BF16), ffn_w2[0].astype(BF16), bsz, t, ctx_len)

    modl, modc = layer_mods(1)
    g = norm_g[1]
    w = cd_in_w[0]
    o_gv, o_gr, o_ga, o_z, o_xbc, o_dt = 1024, 2048, 3072, 3104, 4128, 5664
    zpad = jnp.zeros((dm, LANES - 32), F32)
    w_in = jnp.concatenate([w[:, :o_gv], w[:, o_gv:o_gr], w[:, o_gr:o_ga], w[:, o_z:o_xbc], w[:, o_xbc:o_dt],
                            w[:, o_ga:o_z], zpad, w[:, o_dt:], w[:, o_dt:], jnp.zeros((dm, LANES - 64), F32)],
                           axis=1).astype(BF16)
    dtb = jnp.concatenate([s_dt_bias[0].reshape(1, 32), s_dt_bias[0].reshape(1, 32),
                           jnp.zeros((1, LANES - 64), F32)], axis=1)
    aneg = jnp.concatenate([jnp.zeros((1, 32), F32), -jnp.exp(s_A_log[0].reshape(1, 32)),
                            jnp.zeros((1, LANES - 64), F32)], axis=1)
    gq, gk, gv, grs, zs, sx, sb, sc, ga, dts, dts_t = _in_cd(x2, modl, modc, g[0:1], w_in, s_conv_w[0],
                                                             s_conv_b[0][None], dtb, aneg, bsz, t, ctx_len)
    aw = jnp.zeros((2, LANES, G_HEADS * G_DK), F32)
    aw = aw.at[0, 0:G_RANK].set(g_alpha_w[0, 0]).at[1, G_RANK:2 * G_RANK].set(g_alpha_w[0, 1])
    awh = aw.astype(BF16)
    awl = (aw - awh.astype(F32)).astype(BF16)
    og_f, og_b = _gla(gq, gk, gv, ga, awh, awl, g_alpha_b[0][:, None], bsz, t, ctx_len)
    ys_f, ys_b = _ssd(sx, sb, sc, dts, dts_t, bsz, t, ctx_len)
    dskip = jnp.repeat(s_D[0], S_P)[None]
    x2 = _out_proj(_out_cd_kernel, "out_proj_cd", [og_f, og_b, grs, ys_f, ys_b, sx, zs],
                   [g_norm_w[0][None], dskip, s_norm_w[0][None]],
                   x2, modl, modc, g[1:2], cd_out_w[0].astype(BF16), bsz, t, ctx_len)

    rw = jnp.concatenate([router_w[0], jnp.zeros((dm, LANES - N_EXPERTS), F32)], axis=1)
    rwh = rw.astype(BF16)
    rw = jnp.stack([rwh, (rw - rwh.astype(F32)).astype(BF16)])
    rb = jnp.concatenate([router_b[0][None], jnp.zeros((1, LANES - N_EXPERTS), F32)], axis=1)
    h2, top_idx, top_gate, counts = _router(x2, modl, g[2:3], rw, rb, bsz, t, ctx_len)
    slot, block_e, n_used, n_valid, n_blocks = _routing_tables(top_idx[:, 0:TOP_K], top_idx[:, TOP_K:2 * TOP_K],
                                                               counts[0, :N_EXPERTS].astype(I32))
    xs = _dispatch_rows(h2, slot.reshape(-1), n_blocks * MOE_ROWS)
    ys_e = _moe_experts(xs, moe_w1[0], moe_w2[0], block_e, n_used, n_valid, n_blocks)
    yk = _gather_rows(ys_e, jnp.concatenate([slot[:, 0], slot[:, 1]]))
    out = _combine(yk, top_gate, x2, modl, g[3:4], bsz, t, ctx_len)
    return out.reshape(bsz, seq, dm)
```

```python
import functools

import jax
import jax.numpy as jnp
from jax import lax
from jax.experimental import pallas as pl
from jax.experimental.pallas import tpu as pltpu
from jax.experimental.pallas import tpu_sc as plsc

F32, BF16, I32 = jnp.float32, jnp.bfloat16, jnp.int32

D_MODEL = 1024
GRID_W = 64
EPS = 1e-6
LOG2E = 1.4426950408889634
M_HEADS, M_DK, M_DV = 4, 128, 256
LRU_BLOCKS, LRU_BW, LRU_C = 8, 128, 8.0
G_HEADS, G_DK, G_DV, G_RANK, G_TAU = 4, 128, 256, 16, 16.0
S_HEADS, S_P, S_N, S_GROUPS, S_HPG = 16, 64, 128, 2, 8
FF_DENSE = 2816
N_EXPERTS, TOP_K, FF_EXPERT = 8, 2, 3584

LANES = 128
VMEM_LIMIT = 56 * 1024 * 1024
ROW_TILE = 256
BIG_ROW_TILE_MAX = 640
PROJ_COLS = 256
M_CHUNK = 256
G_CHUNK, G_SUB = 256, 16
S_CHUNK = 256
LRU_TILE = 256
FF_STEP = 256
MOE_ROWS = 1024
MOE_FF = 512
SC_CORES, SC_SUBCORES = 2, 16
SC_LANES = 16
SC_GATHER_ROWS = 128


def _cparams(*sem):
    return pltpu.CompilerParams(dimension_semantics=sem, vmem_limit_bytes=VMEM_LIMIT)


def _sigmoid(x):
    return 0.5 + 0.5 * jnp.tanh(0.5 * x)


def _silu(x):
    return x * _sigmoid(x)


def _softplus(x):
    return jnp.maximum(x, 0.0) + jnp.log1p(jnp.exp(-jnp.abs(x)))


def _log_sigmoid(x):
    return jnp.minimum(x, 0.0) - jnp.log1p(jnp.exp(-jnp.abs(x)))


def _dot(a, b):
    return jnp.dot(a, b, preferred_element_type=F32)


def _dot_nt(a, b):
    return lax.dot_general(a, b, (((1,), (1,)), ((), ())), preferred_element_type=F32)


def _dot_tn(a, b):
    return lax.dot_general(a, b, (((0,), (0,)), ((), ())), preferred_element_type=F32)


def _split3(f):
    f1 = f.astype(BF16)
    r = f - f1.astype(F32)
    f2 = r.astype(BF16)
    f3 = (r - f2.astype(F32)).astype(BF16)
    return f1, f2, f3


def _sel_cols(mask01, f):
    p1, p2, p3 = _split3(f)
    return _dot(mask01, p1) + _dot(mask01, p2) + _dot(mask01, p3)


def _sel_rows(f, mask01):
    p1, p2, p3 = _split3(f)
    return _dot_nt(p1, mask01) + _dot_nt(p2, mask01) + _dot_nt(p3, mask01)


def _expand(f, sel01):
    p1, p2, p3 = _split3(f)
    return _dot(p1, sel01) + _dot(p2, sel01) + _dot(p3, sel01)


def _pack_rows(u):
    n = u.shape[1] // 2
    bits = lax.bitcast_convert_type(u.astype(BF16).astype(F32), jnp.uint32)
    return jnp.bitwise_or(bits[:, :n], jnp.right_shift(bits[:, n:], jnp.uint32(16)))


def _unpack_rows(p):
    hi = lax.bitcast_convert_type(jnp.bitwise_and(p, jnp.uint32(0xFFFF0000)), F32)
    lo = lax.bitcast_convert_type(jnp.left_shift(p, jnp.uint32(16)), F32)
    return jnp.concatenate([hi, lo], axis=1)


def _rms(u, g):
    return u * lax.rsqrt(jnp.mean(u * u, axis=-1, keepdims=True) + EPS) * g


def _head_rms(u, g, n_heads):
    w = u.shape[-1] // n_heads
    return jnp.concatenate([_rms(u[:, h * w:(h + 1) * w], g[:, h * w:(h + 1) * w]) for h in range(n_heads)], axis=1)


def _row_ids(tile_idx, tm):
    return tile_idx * tm + lax.broadcasted_iota(I32, (tm, 1), 0)


def _mod(modl_ref, modc_ref, idx, is_ctx):
    return jnp.where(is_ctx, modc_ref[idx:idx + 1, :], modl_ref[0, idx:idx + 1, :])


def _adaln(x, g, modl_ref, modc_ref, shift_idx, is_ctx):
    shift = _mod(modl_ref, modc_ref, shift_idx, is_ctx)
    scale = _mod(modl_ref, modc_ref, shift_idx + 1, is_ctx)
    return _rms(x, g) * (1.0 + scale) + shift


def _neighbour_tiles(r, step, is_lat):
    n_tiles, per_seg = r.shape[0], GRID_W // 8
    zero = jnp.zeros((1,) + r.shape[1:], r.dtype)

    def across(src):
        return jnp.where(is_lat, 0.0, r[src:src + 1]) if 0 <= src < n_tiles else zero

    pieces = []
    for s0 in range(0, n_tiles, per_seg):
        if step > 0:
            pieces += [across(s0 - 1), r[s0:s0 + per_seg - 1]]
        else:
            pieces += [r[s0 + 1:s0 + per_seg], across(s0 + per_seg)]
    return jnp.concatenate(pieces, axis=0)


def _dwconv(y, cw, cb, is_lat):
    tm, n = y.shape
    y3 = y.reshape(tm // 8, 8, n)
    sub = lax.broadcasted_iota(I32, (1, 8, 1), 1)
    w = lambda j: cw[j:j + 1, :].reshape(1, 1, n)
    r1, r2, r7 = pltpu.roll(y3, 1, 1), pltpu.roll(y3, 2, 1), pltpu.roll(y3, 7, 1)
    ym1 = jnp.where(sub >= 1, r1, _neighbour_tiles(r1, 1, is_lat))
    ym2 = jnp.where(sub >= 2, r2, _neighbour_tiles(r2, 1, is_lat))
    yp1 = jnp.where(sub <= 6, r7, _neighbour_tiles(r7, -1, is_lat))
    out = cb.reshape(1, 1, n) + w(0) * ym2 + w(1) * ym1 + w(2) * y3 + w(3) * yp1
    return out.reshape(tm, n)


def _resident(shape):
    nd = len(shape)
    return pl.BlockSpec(shape, lambda *_: (0,) * nd, pipeline_mode=pl.Buffered(1))


def _big_row_tile(t):
    return max(tm for tm in range(8, BIG_ROW_TILE_MAX + 1, 8) if t % tm == 0)


def _mod_kernel(c_ref, w_ref, b_ref, o_ref):
    a = _silu(c_ref[...])
    o_ref[0] = jnp.dot(a, w_ref[0], preferred_element_type=F32, precision=lax.Precision.HIGHEST) + b_ref[0]


def _modulation(c_all, mod_w, mod_b):
    depth, d, n6 = mod_w.shape
    rows = c_all.shape[0]
    tn = 1024
    return pl.pallas_call(
        _mod_kernel,
        grid=(depth, n6 // tn),
        in_specs=[pl.BlockSpec((rows, d), lambda l, j: (0, 0)),
                  pl.BlockSpec((1, d, tn), lambda l, j: (l, 0, j)),
                  pl.BlockSpec((1, 1, tn), lambda l, j: (l, 0, j))],
        out_specs=pl.BlockSpec((1, rows, tn), lambda l, j: (l, 0, j)),
        out_shape=jax.ShapeDtypeStruct((depth, rows, n6), F32),
        compiler_params=_cparams("arbitrary", "arbitrary"),
        name="modulation",
    )(c_all, mod_w, mod_b.reshape(depth, 1, n6))


def _project(h_ref, w_ref, w_col, out_ref, epilogue):
    for c in range(0, out_ref.shape[1], PROJ_COLS):
        y = _dot(h_ref[...], w_ref[:, w_col + c:w_col + c + PROJ_COLS])
        out_ref[:, c:c + PROJ_COLS] = epilogue(y, c).astype(out_ref.dtype)


def _in_ab_kernel(tm, ctx_len, xc_ref, xl_ref, modl_ref, modc_ref, g_ref, w_ref, mcw_ref, mcb_ref, lcw_ref,
                  lcb_ref, gb_ref, x_ref, q_ref, k_ref, v_ref, og_ref, lx_ref, glg_ref, gates_ref, gates_t_ref,
                  h_scr):
    r = _row_ids(pl.program_id(1), tm)
    is_ctx = r < ctx_len
    is_lat = pl.program_id(1) * tm >= ctx_len
    x_ref[...] = jnp.where(is_lat, xl_ref[...], xc_ref[...])
    h_scr[...] = _adaln(x_ref[...], g_ref[...], modl_ref, modc_ref, 0, is_ctx).astype(BF16)

    def conv(cw_ref, cb_ref):
        return lambda y, c: _dwconv(y, cw_ref[:, c:c + PROJ_COLS], cb_ref[:, c:c + PROJ_COLS], is_lat)

    qc, kc, lc = conv(mcw_ref, mcb_ref), conv(mcw_ref, mcb_ref), conv(lcw_ref, lcb_ref)
    _project(h_scr, w_ref, 0, q_ref, lambda y, c: _silu(qc(y, c)))
    _project(h_scr, w_ref, 512, k_ref, lambda y, c: _silu(kc(y, 512 + c)) * (M_DK ** -0.5))
    _project(h_scr, w_ref, 1024, v_ref, lambda y, c: y)
    _project(h_scr, w_ref, 2048, og_ref, lambda y, c: _sigmoid(y))
    _project(h_scr, w_ref, 3072, lx_ref, lc)
    _project(h_scr, w_ref, 4096, glg_ref, lambda y, c: jax.nn.gelu(y))
    gt = _dot(h_scr[...], w_ref[:, 5120:5248]) + gb_ref[...]
    lane = lax.broadcasted_iota(I32, gt.shape, 1)
    gates = jnp.where(lane >= 2 * M_HEADS, _log_sigmoid(gt), gt)
    gates_ref[...] = gates
    gates_t_ref[...] = gates.T


def _in_ab(xc2, xl2, modl, modc, g, w, mcw, mcb, lcw, lcb, gb, bsz, t, ctx_len):
    tm = ROW_TILE
    nt, nct = t // tm, ctx_len // tm
    n = bsz * t
    row = lambda c: pl.BlockSpec((tm, c), lambda b, i: (b * nt + i, 0))
    outs = [(D_MODEL, F32), (512, BF16), (512, BF16), (1024, BF16), (1024, BF16), (1024, BF16), (1024, BF16),
            (LANES, F32)]
    return pl.pallas_call(
        functools.partial(_in_ab_kernel, tm, ctx_len),
        grid=(bsz, nt),
        in_specs=[pl.BlockSpec((tm, D_MODEL), lambda b, i: (b * nct + jnp.minimum(i, nct - 1), 0)),
                  pl.BlockSpec((tm, D_MODEL), lambda b, i: (b * (nt - nct) + jnp.maximum(i - nct, 0), 0)),
                  pl.BlockSpec((1, 8, D_MODEL), lambda b, i: (b, 0, 0)),
                  _resident((8, D_MODEL)), _resident((1, D_MODEL)), _resident(w.shape),
                  _resident(mcw.shape), _resident(mcb.shape), _resident(lcw.shape), _resident(lcb.shape),
                  _resident(gb.shape)],
        out_specs=[row(c) for c, _ in outs] + [pl.BlockSpec((LANES, tm), lambda b, i: (0, b * nt + i))],
        out_shape=[jax.ShapeDtypeStruct((n, c), dt) for c, dt in outs] + [jax.ShapeDtypeStruct((LANES, n), F32)],
        scratch_shapes=[pltpu.VMEM((tm, D_MODEL), BF16)],
        compiler_params=_cparams("arbitrary", "arbitrary"),
        name="in_proj_ab",
    )(xc2, xl2, modl, modc, g, w, mcw, mcb, lcw, lcb, gb)


def _reverse_order(n_ctx_chunks, n_chunks):
    return lambda c: jnp.where(c < n_ctx_chunks, n_ctx_chunks - 1 - c, n_chunks - 1 - (c - n_ctx_chunks))


def _tri_mask(L, reverse):
    row = lax.broadcasted_iota(I32, (L, L), 0)
    col = lax.broadcasted_iota(I32, (L, L), 1)
    return (col >= row) if reverse else (col <= row)


def _bidir_scan(kern, L, row_ins, lane_ins, res_ins, out_cols, scratch, bsz, t, ctx_len, name):
    nc = t // L
    rev = _reverse_order(ctx_len // L, nc)
    fwd = lambda i: i
    n = bsz * t

    def row(c, order):
        return pl.BlockSpec((L, c), lambda b, i: (b * nc + order(i), 0))

    def lanes(r, order):
        return pl.BlockSpec((r, L), lambda b, i: (0, b * nc + order(i)))

    in_specs, args = [], []
    for order in (fwd, rev):
        in_specs += [row(a.shape[1], order) for a in row_ins] + [lanes(r, order) for _, r in lane_ins]
        args += list(row_ins) + [a for a, _ in lane_ins]
    in_specs += [_resident(a.shape) for a in res_ins]
    args += list(res_ins)
    return pl.pallas_call(
        kern,
        grid=(bsz, nc),
        in_specs=in_specs,
        out_specs=[row(c, fwd) for c in out_cols] + [row(c, rev) for c in out_cols],
        out_shape=[jax.ShapeDtypeStruct((n, c), BF16) for c in out_cols] * 2,
        scratch_shapes=scratch,
        compiler_params=_cparams("arbitrary", "arbitrary"),
        name=name,
    )(*args)


def _zero_at_start(*scratch):
    @pl.when(pl.program_id(1) == 0)
    def _():
        for s in scratch:
            s[...] = jnp.zeros_like(s)


def _mlstm_kernel(L, qf, kf, vf, gcf, grf, qb, kb, vb, gcb, grb, of, ob, c_scr, m_scr):
    _zero_at_start(c_scr, m_scr)
    ones = jnp.ones((L, LANES), BF16)
    wide = lambda c, k: jnp.concatenate([c] * k, axis=1)
    er = lax.broadcasted_iota(I32, (LANES, 2 * M_HEADS * LANES), 0)
    ej = jnp.right_shift(lax.broadcasted_iota(I32, (LANES, 2 * M_HEADS * LANES), 1), 7)
    chains = []
    for d, (q_ref, k_ref, v_ref, gc_ref, gr_ref, out_ref) in enumerate(((qf, kf, vf, gcf, grf, of),
                                                                         (qb, kb, vb, gcb, grb, ob))):
        reverse = d == 1
        mask = _tri_mask(L, reverse)
        mask01 = mask.astype(BF16)
        gc = gc_ref[...]
        gr = gr_ref[...]
        bcol = _sel_cols(mask01, gc)
        brow = _sel_rows(gr, mask01)
        glane = lax.broadcasted_iota(I32, (L, LANES), 1)
        src = jnp.where(ej < M_HEADS, M_HEADS * d + ej, M_HEADS + M_HEADS * d + ej)
        s1, s2, _ = _split3(jnp.where(glane < 2 * M_HEADS, gc, bcol))
        sel = (er == src).astype(BF16)
        rep = _dot(s1, sel) + _dot(s2, sel)
        last = 0 if reverse else L - 1
        for h in range(M_HEADS):
            fo, io = 2 * M_HEADS + M_HEADS * d + h, M_HEADS * d + h
            ic, bc = rep[:, h * LANES:(h + 1) * LANES], rep[:, (M_HEADS + h) * LANES:(M_HEADS + h + 1) * LANES]
            r1 = gr[io:io + 1, :] - brow[fo:fo + 1, :]
            m = m_scr[d, h:h + 1, :]
            g = bc + m
            mx = jnp.max(jnp.where(mask, r1, -jnp.inf), axis=1, keepdims=True)
            mt = jnp.maximum(g, bc + mx)
            p = jnp.exp(jnp.where(mask, wide(bc - mt, L // LANES) + r1, -jnp.inf))
            bl = bc[last:last + 1, :]
            a_s = bl - bc + ic
            m_new = jnp.maximum(bl + m, jnp.max(a_s, axis=0, keepdims=True))
            chains.append(dict(
                d=d, h=h, out_ref=out_ref, mt=mt, w_inter=jnp.exp(g - mt), p=p,
                m_new=m_new, decay=jnp.exp(bl + m - m_new), ws=jnp.exp(a_s - m_new),
                qh=q_ref[:, h * M_DK:(h + 1) * M_DK], kh=k_ref[:, h * M_DK:(h + 1) * M_DK],
                vx=jnp.concatenate([v_ref[:, h * M_DV:(h + 1) * M_DV], ones], axis=1)))
    for ch in chains:
        ch["sc"] = (_dot_nt(ch["qh"], ch["kh"]) * ch["p"]).astype(BF16)
    for ch in chains:
        cx = c_scr[ch["d"], ch["h"]]
        numx = wide(ch["w_inter"], 1 + M_DV // LANES) * _dot(ch["qh"], cx.astype(BF16)) + _dot(ch["sc"], ch["vx"])
        den = numx[:, M_DV:]
        rinv = 1.0 / jnp.maximum(jnp.abs(den), jnp.exp(-ch["mt"]))
        hh = numx[:, 0:M_DV] * wide(rinv, M_DV // LANES)
        ch["out_ref"][:, ch["h"] * M_DV:(ch["h"] + 1) * M_DV] = hh.astype(BF16)
    for ch in chains:
        d, h = ch["d"], ch["h"]
        kw = (ch["kh"].astype(F32) * ch["ws"]).astype(BF16)
        c_scr[d, h] = wide(ch["decay"], 1 + M_DV // LANES) * c_scr[d, h] + _dot_tn(kw, ch["vx"])
        m_scr[d, h:h + 1, :] = ch["m_new"]


def _mlstm(q, k, v, gcol, grow, bsz, t, ctx_len):
    L = M_CHUNK
    scratch = [pltpu.VMEM((2, M_HEADS, M_DK, M_DV + LANES), F32), pltpu.VMEM((2, 8, LANES), F32)]
    return _bidir_scan(functools.partial(_mlstm_kernel, L), L, [q, k, v, gcol], [(grow, 16)], [], [1024], scratch,
                       bsz, t, ctx_len, "mlstm_scan")


def _lru_kernel(L, uf, ub_, wa_ref, wx_ref, ba_ref, bx_ref, lam_ref, of, ob, a_scr, b_scr, h_scr):
    _zero_at_start(h_scr)
    R, C = L // 8, LRU_BLOCKS * LRU_BW
    i0, i1 = lax.broadcasted_iota(I32, (L, L), 0), lax.broadcasted_iota(I32, (L, L), 1)
    regroup = (i1 == jnp.bitwise_and(i0, 7) * R + jnp.right_shift(i0, 3)).astype(BF16)
    ungroup = (i0 == jnp.bitwise_and(i1, 7) * R + jnp.right_shift(i1, 3)).astype(BF16)
    sub = lax.broadcasted_iota(I32, (8, 1), 0)
    dirs = ((uf, of), (ub_, ob))
    for d, (u_ref, _) in enumerate(dirs):
        u = _dot(regroup, u_ref[...])
        ub = u.astype(BF16)
        sp = -LRU_C * _softplus(-lam_ref[d])
        for n in range(LRU_BLOCKS):
            sl = slice(n * LRU_BW, (n + 1) * LRU_BW)
            rg = _sigmoid(_dot(ub[:, sl], wa_ref[d, n]) + ba_ref[d, :, sl])
            ig = _sigmoid(_dot(ub[:, sl], wx_ref[d, n]) + bx_ref[d, :, sl])
            log_a = rg * sp[:, sl]
            a = jnp.exp(log_a)
            a_scr[d, :, sl] = a
            b_scr[d, :, sl] = jnp.sqrt(-jnp.tanh(log_a) * (a * a + 1.0)) * (ig * u[:, sl])
    hs = [jnp.zeros((8, C), F32), jnp.zeros((8, C), F32)]
    ps = [jnp.ones((8, C), F32), jnp.ones((8, C), F32)]
    for j in range(R):
        for d in range(2):
            rows = slice(8 * j, 8 * j + 8) if d == 0 else slice(8 * (R - 1 - j), 8 * (R - j))
            a = a_scr[d, rows, :]
            hs[d] = a * hs[d] + b_scr[d, rows, :]
            ps[d] = a * ps[d]
            b_scr[d, rows, :] = hs[d]
            a_scr[d, rows, :] = ps[d]
    for d, (_, out_ref) in enumerate(dirs):
        c = h_scr[d:d + 1, :]
        cin = jnp.zeros((8, C), F32)
        for s in (range(8) if d == 0 else reversed(range(8))):
            cin = jnp.where(sub == s, c, cin)
            c = ps[d][s:s + 1, :] * c + hs[d][s:s + 1, :]
        h_scr[d:d + 1, :] = c
        h = b_scr[d].reshape(R, 8, C) + a_scr[d].reshape(R, 8, C) * cin[None]
        out_ref[...] = _dot(ungroup, h.reshape(L, C).astype(BF16)).astype(BF16)


def _lru(u, wa, wx, ba, bx, lam, bsz, t, ctx_len):
    L = LRU_TILE
    scratch = [pltpu.VMEM((2, L, 1024), F32), pltpu.VMEM((2, L, 1024), F32), pltpu.VMEM((8, 1024), F32)]
    return _bidir_scan(functools.partial(_lru_kernel, L), L, [u], [], [wa, wx, ba, bx, lam], [1024], scratch,
                       bsz, t, ctx_len, "rglru_scan")


def _swiglu_rows(x, is_ctx, modl_ref, modc_ref, g2_ref, g3_ref, w1_ref, w2_ref):
    h = _adaln(x, g2_ref[...], modl_ref, modc_ref, 3, is_ctx).astype(BF16)
    ff = w2_ref.shape[0]
    acc = jnp.zeros(x.shape, F32)
    for j in range(ff // FF_STEP):
        lo = j * FF_STEP
        gj = _dot(h, w1_ref[:, lo:lo + FF_STEP])
        uj = _dot(h, w1_ref[:, ff + lo:ff + lo + FF_STEP])
        acc = acc + _dot((_silu(gj) * uj).astype(BF16), w2_ref[lo:lo + FF_STEP, :])
    return x + _mod(modl_ref, modc_ref, 5, is_ctx) * _rms(acc, g3_ref[...])


def _out_tail(tm, ctx_len, a1, a2, x_ref, modl_ref, modc_ref, g_ref, w_ref, xo_ref, ffn_refs=None):
    is_ctx = _row_ids(pl.program_id(1), tm) < ctx_len
    half = w_ref.shape[0] // 2
    y = _dot(a1.astype(BF16), w_ref[0:half, :]) + _dot(a2.astype(BF16), w_ref[half:, :])
    x = x_ref[...] + _mod(modl_ref, modc_ref, 2, is_ctx) * _rms(y, g_ref[...])
    if ffn_refs is not None:
        x = _swiglu_rows(x, is_ctx, modl_ref, modc_ref, *ffn_refs)
    xo_ref[...] = x


def _out_ab_kernel(tm, ctx_len, hf, hb, og, lf, lb, glg, nw, x_ref, modl_ref, modc_ref, g_ref, w_ref,
                   g2_ref, g3_ref, w1_ref, w2_ref, xo_ref):
    hm = _head_rms(hf[...].astype(F32) + hb[...].astype(F32), nw[...], M_HEADS) * og[...].astype(F32)
    hl = (lf[...].astype(F32) + lb[...].astype(F32)) * glg[...].astype(F32)
    _out_tail(tm, ctx_len, hm, hl, x_ref, modl_ref, modc_ref, g_ref, w_ref, xo_ref,
              (g2_ref, g3_ref, w1_ref, w2_ref))


def _out_cd_kernel(tm, ctx_len, gf, gb, grs, yf, yb, sx, zs, gnw, dsk, snw, x_ref, modl_ref, modc_ref, g_ref,
                   w_ref, xo_ref):
    og = _head_rms(gf[...].astype(F32) + gb[...].astype(F32), gnw[...], G_HEADS) * grs[...].astype(F32)
    ys = yf[...].astype(F32) + yb[...].astype(F32) + dsk[...] * sx[...].astype(F32)
    ys = _rms(ys * zs[...].astype(F32), snw[...])
    _out_tail(tm, ctx_len, og, ys, x_ref, modl_ref, modc_ref, g_ref, w_ref, xo_ref)


def _out_proj(kern, name, acts, vecs, x2, modl, modc, g, w, bsz, t, ctx_len, extra=()):
    tm = _big_row_tile(t)
    nt = t // tm
    row = lambda c: pl.BlockSpec((tm, c), lambda b, i: (b * nt + i, 0))
    return pl.pallas_call(
        functools.partial(kern, tm, ctx_len),
        grid=(bsz, nt),
        in_specs=[row(1024)] * len(acts) + [_resident((1, 1024))] * len(vecs)
        + [row(D_MODEL), pl.BlockSpec((1, 8, D_MODEL), lambda b, i: (b, 0, 0)),
           _resident((8, D_MODEL)), _resident((1, D_MODEL)), _resident(w.shape)]
        + [_resident(a.shape) for a in extra],
        out_specs=row(D_MODEL),
        out_shape=jax.ShapeDtypeStruct(x2.shape, F32),
        compiler_params=_cparams("arbitrary", "arbitrary"),
        name=name,
    )(*acts, *vecs, x2, modl, modc, g, w, *extra)


def _in_cd_kernel(tm, ctx_len, x_ref, modl_ref, modc_ref, g_ref, w_ref, cw_ref, cb_ref, dtb_ref, aneg_ref,
                  gq_ref, gk_ref, gv_ref, gr_ref, z_ref, sx_ref, sb_ref, sc_ref, ga_ref, dts_ref, dts_t_ref, h_scr):
    r = _row_ids(pl.program_id(1), tm)
    is_ctx = r < ctx_len
    h_scr[...] = _adaln(x_ref[...], g_ref[...], modl_ref, modc_ref, 0, is_ctx).astype(BF16)
    is_lat = pl.program_id(1) * tm >= ctx_len

    def conv_silu(col0):
        return lambda y, c: _silu(_dwconv(y, cw_ref[:, col0 + c:col0 + c + PROJ_COLS],
                                          cb_ref[:, col0 + c:col0 + c + PROJ_COLS], is_lat))

    _project(h_scr, w_ref, 0, gq_ref, lambda y, c: y * (G_DK ** -0.5))
    _project(h_scr, w_ref, 512, gk_ref, lambda y, c: y)
    _project(h_scr, w_ref, 1024, gv_ref, lambda y, c: y)
    _project(h_scr, w_ref, 2048, gr_ref, lambda y, c: _silu(y))
    _project(h_scr, w_ref, 3072, z_ref, lambda y, c: _silu(y))
    _project(h_scr, w_ref, 4096, sx_ref, conv_silu(0))
    _project(h_scr, w_ref, 5120, sb_ref, conv_silu(1024))
    _project(h_scr, w_ref, 5376, sc_ref, conv_silu(1280))
    ga_ref[...] = _dot(h_scr[...], w_ref[:, 5632:5760])
    dt = _softplus(_dot(h_scr[...], w_ref[:, 5760:5888]) + dtb_ref[...])
    lane = lax.broadcasted_iota(I32, dt.shape, 1)
    dts = jnp.where(lane < 2 * S_HEADS, dt, dt * aneg_ref[...])
    dts_ref[...] = dts
    dts_t_ref[...] = dts.T


def _in_cd(x2, modl, modc, g, w, cw, cb, dtb, aneg, bsz, t, ctx_len):
    tm = ROW_TILE
    nt = t // tm
    n = bsz * t
    row = lambda c: pl.BlockSpec((tm, c), lambda b, i: (b * nt + i, 0))
    outs = [(512, BF16), (512, BF16), (1024, BF16), (1024, BF16), (1024, BF16), (1024, BF16), (256, BF16),
            (256, BF16), (LANES, F32), (LANES, F32)]
    return pl.pallas_call(
        functools.partial(_in_cd_kernel, tm, ctx_len),
        grid=(bsz, nt),
        in_specs=[row(D_MODEL), pl.BlockSpec((1, 8, D_MODEL), lambda b, i: (b, 0, 0)), _resident((8, D_MODEL)),
                  _resident((1, D_MODEL)), _resident(w.shape), _resident(cw.shape), _resident(cb.shape),
                  _resident(dtb.shape), _resident(aneg.shape)],
        out_specs=[row(c) for c, _ in outs] + [pl.BlockSpec((LANES, tm), lambda b, i: (0, b * nt + i))],
        out_shape=[jax.ShapeDtypeStruct((n, c), dt) for c, dt in outs] + [jax.ShapeDtypeStruct((LANES, n), F32)],
        scratch_shapes=[pltpu.VMEM((tm, D_MODEL), BF16)],
        compiler_params=_cparams("arbitrary", "arbitrary"),
        name="in_proj_cd",
    )(x2, modl, modc, g, w, cw, cb, dtb, aneg)


def _gla_kernel(L, C, qf, kf, vf, gaf, qb, kb, vb, gab, awh_ref, awl_ref, ab_ref, of, ob, s_scr):
    _zero_at_start(s_scr)
    nb = L // C
    chains = []
    for d, (q_ref, k_ref, v_ref, ga_ref, out_ref) in enumerate(((qf, kf, vf, gaf, of), (qb, kb, vb, gab, ob))):
        reverse = d == 1
        mask = _tri_mask(L, reverse)
        g1, g2, _ = _split3(ga_ref[...])
        pre = _dot(g1, awh_ref[d]) + _dot(g2, awh_ref[d]) + _dot(g1, awl_ref[d])
        lg = _log_sigmoid(pre + ab_ref[d]) * (1.0 / G_TAU)
        ball = _sel_cols(mask.astype(BF16), lg)
        for h in range(G_HEADS):
            ks = slice(h * G_DK, (h + 1) * G_DK)
            b = ball[:, ks] * LOG2E
            qh = q_ref[:, ks].astype(F32)
            inter = _dot_nt((qh * jnp.exp2(b)).astype(BF16), s_scr[d, h].astype(BF16))
            chains.append(dict(d=d, h=h, reverse=reverse, mask=mask, b=b, qh=qh, kh=k_ref[:, ks].astype(F32),
                               vh=v_ref[:, h * G_DV:(h + 1) * G_DV], inter=inter, out_ref=out_ref, blocks=[]))
    for i in range(nb):
        lo, hi = i * C, (i + 1) * C
        for ch in chains:
            b = ch["b"]
            if ch["reverse"]:
                bref = b[hi:hi + 1, :] if i < nb - 1 else jnp.zeros((1, G_DK), F32)
                v0, v1 = lo // LANES * LANES, L
            else:
                bref = b[lo - 1:lo, :] if i > 0 else jnp.zeros((1, G_DK), F32)
                v0, v1 = 0, min(L, -(-hi // LANES) * LANES)
            qi = (ch["qh"][lo:hi, :] * jnp.exp2(b[lo:hi, :] - bref)).astype(BF16)
            ki = (ch["kh"][v0:v1] * jnp.exp2(bref - b[v0:v1])).astype(BF16)
            pieces = [jnp.zeros((C, v0), F32)] if v0 else []
            pieces.append(_dot_nt(qi, ki))
            if v1 < L:
                pieces.append(jnp.zeros((C, L - v1), F32))
            ch["blocks"].append(jnp.concatenate(pieces, axis=1) if len(pieces) > 1 else pieces[0])
    for ch in chains:
        att = jnp.where(ch["mask"], jnp.concatenate(ch["blocks"], axis=0), 0.0).astype(BF16)
        oh = ch["inter"] + _dot(att, ch["vh"])
        ch["out_ref"][:, ch["h"] * G_DV:(ch["h"] + 1) * G_DV] = oh.astype(BF16)
    for ch in chains:
        d, h, b = ch["d"], ch["h"], ch["b"]
        last = 0 if ch["reverse"] else L - 1
        bl = b[last:last + 1, :]
        kd = (ch["kh"] * jnp.exp2(bl - b)).astype(BF16)
        s_scr[d, h] = s_scr[d, h] * jnp.exp2(bl) + _dot_tn(ch["vh"], kd)


def _gla(q, k, v, ga, awh, awl, ab, bsz, t, ctx_len):
    L, C = G_CHUNK, G_SUB
    scratch = [pltpu.VMEM((2, G_HEADS, G_DV, G_DK), F32)]
    return _bidir_scan(functools.partial(_gla_kernel, L, C), L, [q, k, v, ga], [], [awh, awl, ab], [1024], scratch,
                       bsz, t, ctx_len, "gla_scan")


def _ssd_kernel(L, xf, bf, cf, gcf, grf, xb, bb, cb_, gcb, grb, of, ob, s_scr):
    _zero_at_start(s_scr)
    er = lax.broadcasted_iota(I32, (LANES, S_HEADS * S_P), 0)
    ec = jnp.right_shift(lax.broadcasted_iota(I32, (LANES, S_HEADS * S_P), 1), 6)
    glane = lax.broadcasted_iota(I32, (L, LANES), 1)
    lane = lax.broadcasted_iota(I32, (L, 2 * S_P), 1)
    dirs, chains = [], []
    for d, (x_ref, b_ref, c_ref, gc_ref, gr_ref, out_ref) in enumerate(((xf, bf, cf, gcf, grf, of),
                                                                         (xb, bb, cb_, gcb, grb, ob))):
        reverse = d == 1
        mask = _tri_mask(L, reverse)
        mask01 = mask.astype(BF16)
        gc = gc_ref[...]
        gr = gr_ref[...]
        bcol = _sel_cols(mask01, gc)
        brow = _sel_rows(gr, mask01)
        last = 0 if reverse else L - 1
        dt_o, la_o = S_HEADS * d, 2 * S_HEADS + S_HEADS * d
        sel_la = (er == ec + la_o).astype(BF16)
        x = x_ref[...]
        bla = jnp.where((glane >= la_o) & (glane < la_o + S_HEADS), bcol, 0.0)
        dt_at_la = pltpu.roll(gc, 2 * S_HEADS, 1)
        bl = bla[last:last + 1, :]
        ebx = _dot(jnp.exp(bla).astype(BF16), sel_la)
        wx = _dot((jnp.exp(bl - bla) * dt_at_la).astype(BF16), sel_la)
        dirs.append(dict(d=d, mask=mask, gr=gr, x=x, ebx=ebx, last=last, dt_o=dt_o, la_o=la_o, out_ref=out_ref,
                         xw=(x.astype(F32) * wx).astype(BF16),
                         bcol2=bcol * LOG2E, brow2=brow * LOG2E))
        for g in range(S_GROUPS):
            ns = slice(g * S_N, (g + 1) * S_N)
            chains.append(dict(dr=dirs[-1], g=g, bg=b_ref[:, ns], cg=c_ref[:, ns],
                               gs=slice(g * S_HPG * S_P, (g + 1) * S_HPG * S_P)))
    for ch in chains:
        dr = ch["dr"]
        ch["cb"] = _dot_nt(ch["cg"], ch["bg"])
        ch["inter"] = _dot(ch["cg"], s_scr[dr["d"], ch["g"]].astype(BF16)) * dr["ebx"][:, ch["gs"]]
    for ch in chains:
        dr, g = ch["dr"], ch["g"]
        parts = []
        for pr in range(S_HPG // 2):
            e0 = g * S_HPG + 2 * pr
            xp = dr["x"][:, e0 * S_P:(e0 + 2) * S_P]
            res = []
            for e in (e0, e0 + 1):
                col = dr["la_o"] + e
                bc, br = dr["bcol2"][:, col:col + 1], dr["brow2"][col:col + 1, :]
                dtr = dr["gr"][dr["dt_o"] + e:dr["dt_o"] + e + 1, :]
                att = (ch["cb"] * dtr * jnp.exp2(jnp.where(dr["mask"], bc - br, -jnp.inf))).astype(BF16)
                res.append(_dot(att, xp))
            parts.append(jnp.where(lane < S_P, res[0], res[1]))
        dr["out_ref"][:, ch["gs"]] = (jnp.concatenate(parts, axis=1) + ch["inter"]).astype(BF16)
    for ch in chains:
        dr, g, gs = ch["dr"], ch["g"], ch["gs"]
        s_scr[dr["d"], g] = (s_scr[dr["d"], g] * dr["ebx"][dr["last"]:dr["last"] + 1, gs]
                             + _dot_tn(ch["bg"], dr["xw"][:, gs]))


def _ssd(x, bm, cm, gcol, grow, bsz, t, ctx_len):
    L = S_CHUNK
    scratch = [pltpu.VMEM((2, S_GROUPS, S_N, S_HPG * S_P), F32)]
    return _bidir_scan(functools.partial(_ssd_kernel, L), L, [x, bm, cm, gcol], [(grow, 4 * S_HEADS)], [], [1024],
                       scratch,
                       bsz, t, ctx_len, "ssd_scan")


def _router_kernel(x_ref, modl_ref, g_ref, rw_ref, rb_ref, h_ref, idx_ref, gate_ref, cnt_ref, base_scr):
    @pl.when((pl.program_id(0) == 0) & (pl.program_id(1) == 0))
    def _():
        base_scr[...] = jnp.zeros_like(base_scr)

    h = _rms(x_ref[...], g_ref[...]) * (1.0 + modl_ref[0, 4:5, :]) + modl_ref[0, 3:4, :]
    h_ref[...] = _pack_rows(h)
    h1, h2, _ = _split3(h)
    logits = _dot(h1, rw_ref[0]) + _dot(h2, rw_ref[0]) + _dot(h1, rw_ref[1]) + rb_ref[...]
    lane = lax.broadcasted_iota(I32, logits.shape, 1)
    logits = jnp.where(lane < N_EXPERTS, logits, -jnp.inf)
    lanef = lane.astype(F32)
    m1 = jnp.max(logits, axis=1, keepdims=True)
    i1 = jnp.min(jnp.where(logits == m1, lanef, float(LANES)), axis=1, keepdims=True)
    rest = jnp.where(lanef == i1, -jnp.inf, logits)
    m2 = jnp.max(rest, axis=1, keepdims=True)
    i2 = jnp.min(jnp.where(rest == m2, lanef, float(LANES)), axis=1, keepdims=True)
    e = jnp.exp(m2 - m1)
    g1 = 1.0 / (1.0 + e)
    gate_ref[...] = jnp.where(lane == 0, g1, jnp.where(lane == 1, e * g1, 0.0))
    tm = logits.shape[0]
    oh1, oh2 = (lanef == i1).astype(F32), (lanef == i2).astype(F32)
    earlier = (lax.broadcasted_iota(I32, (tm, tm), 1) < lax.broadcasted_iota(I32, (tm, tm), 0)).astype(BF16)
    seen = base_scr[0:1, :] + _dot(earlier, (oh1 + oh2).astype(BF16))
    r1 = jnp.sum(oh1 * seen, axis=1, keepdims=True)
    r2 = jnp.sum(oh2 * seen, axis=1, keepdims=True)
    base_scr[0:1, :] = base_scr[0:1, :] + jnp.sum(oh1 + oh2, axis=0, keepdims=True)
    cnt_ref[...] = base_scr[...]
    idx_ref[...] = jnp.where(lane == 0, i1, jnp.where(lane == 1, i2, jnp.where(lane == 2, r1, jnp.where(
        lane == 3, r2, 0.0)))).astype(I32)


def _router(x2, modl, g, rw, rb, bsz, t, ctx_len):
    tm = ROW_TILE
    nt, nct = t // tm, ctx_len // tm
    nl = bsz * (t - ctx_len)
    nlt = nt - nct
    orow = lambda c: pl.BlockSpec((tm, c), lambda b, i: (b * nlt + i, 0))
    return pl.pallas_call(
        _router_kernel,
        grid=(bsz, nlt),
        in_specs=[pl.BlockSpec((tm, D_MODEL), lambda b, i: (b * nt + nct + i, 0)),
                  pl.BlockSpec((1, 8, D_MODEL), lambda b, i: (b, 0, 0)),
                  _resident((1, D_MODEL)), _resident(rw.shape), _resident(rb.shape)],
        out_specs=[orow(D_MODEL // 2), orow(LANES), orow(LANES), pl.BlockSpec((8, LANES), lambda b, i: (0, 0))],
        out_shape=[jax.ShapeDtypeStruct((nl, D_MODEL // 2), jnp.uint32), jax.ShapeDtypeStruct((nl, LANES), I32),
                   jax.ShapeDtypeStruct((nl, LANES), F32), jax.ShapeDtypeStruct((8, LANES), F32)],
        scratch_shapes=[pltpu.VMEM((8, LANES), F32)],
        compiler_params=_cparams("arbitrary", "arbitrary"),
        name="moe_router",
    )(x2, modl, g, rw, rb)


def _gather_rows(table, idx):
    n_rows, d = idx.shape[0], table.shape[1]
    n_workers = SC_CORES * SC_SUBCORES
    per_worker = n_rows // n_workers
    chunk = SC_GATHER_ROWS
    assert per_worker * n_workers == n_rows and per_worker % chunk == 0
    mesh = plsc.VectorSubcoreMesh(core_axis_name="c", subcore_axis_name="s")

    @functools.partial(
        pl.kernel, mesh=mesh, out_type=jax.ShapeDtypeStruct((n_rows, d), table.dtype),
        scratch_types=[pltpu.VMEM((chunk,), I32), pltpu.VMEM((chunk, d), table.dtype), pltpu.SemaphoreType.DMA],
        name="sc_gather_rows")
    def gather(table_hbm, idx_hbm, out_hbm, idx_v, rows_v, sem):
        base = (lax.axis_index("s") * SC_CORES + lax.axis_index("c")) * per_worker

        @pl.loop(0, per_worker // chunk)
        def _(j):
            off = pl.multiple_of(base + j * chunk, 8)
            pltpu.sync_copy(idx_hbm.at[pl.ds(off, chunk)], idx_v)
            pltpu.async_copy(table_hbm.at[idx_v], rows_v, sem).wait()
            pltpu.sync_copy(rows_v, out_hbm.at[pl.ds(off, chunk)])

    return gather(table, idx)


def _dispatch_rows(table, slot, n_out):
    assert TOP_K == 2
    n_assign, (n_tok, d) = slot.shape[0], table.shape
    n_workers = SC_CORES * SC_SUBCORES
    per_worker = n_out // n_workers
    chunk, vec = SC_GATHER_ROWS, SC_LANES
    assert per_worker * n_workers == n_out and per_worker % chunk == 0 and n_assign % vec == 0
    assert n_out < 3 * n_tok
    mesh = plsc.VectorSubcoreMesh(core_axis_name="c", subcore_axis_name="s")

    @functools.partial(
        pl.kernel, mesh=mesh, out_type=jax.ShapeDtypeStruct((n_out, d), table.dtype),
        scratch_types=[pltpu.VMEM((n_assign,), I32), pltpu.VMEM((per_worker,), I32),
                       pltpu.VMEM((chunk, d), table.dtype), pltpu.SemaphoreType.DMA],
        compiler_params=pltpu.CompilerParams(needs_layout_passes=False),
        name="sc_dispatch_rows")
    def dispatch(table_hbm, slot_hbm, out_hbm, slots_v, tok_v, rows_v, sem):
        base = (lax.axis_index("s") * SC_CORES + lax.axis_index("c")) * per_worker
        pltpu.sync_copy(slot_hbm, slots_v)
        lane = lax.broadcasted_iota(I32, (vec,), 0)

        @pl.loop(0, per_worker // vec)
        def _(j):
            p = base + j * vec + lane
            p = jnp.where(p >= n_tok, p - n_tok, p)
            tok_v[pl.ds(j * vec, vec)] = jnp.where(p >= n_tok, p - n_tok, p)

        @pl.loop(0, n_assign // vec)
        def _(j):
            s = slots_v[pl.ds(j * vec, vec)] - base
            mine = (s >= 0) & (s < per_worker)
            tok = lax.shift_right_logical(j * vec + lane, 1)
            plsc.store_scatter(tok_v, [jnp.where(mine, s, 0)], tok, mask=mine)

        @pl.loop(0, per_worker // chunk)
        def _(j):
            off = pl.multiple_of(j * chunk, 8)
            pltpu.async_copy(table_hbm.at[tok_v.at[pl.ds(off, chunk)]], rows_v, sem).wait()
            pltpu.sync_copy(rows_v, out_hbm.at[pl.ds(base + off, chunk)])

    return dispatch(table, slot)


def _moe_kernel(be_ref, nu_ref, nv_ref, x_ref, w1g_ref, w1u_ref, w2_ref, y_ref, xb_ref, acc_ref):
    i, j = pl.program_id(0), pl.program_id(1)
    bm = x_ref.shape[0]

    def swiglu_rows(rows):
        x = xb_ref[0:rows, :]
        a = (_silu(_dot(x, w1g_ref[0].astype(BF16))) * _dot(x, w1u_ref[0].astype(BF16))).astype(BF16)
        acc_ref[0:rows, :] += _dot(a, w2_ref[0].astype(BF16))

    @pl.when(i < nu_ref[0])
    def _():
        @pl.when(j == 0)
        def _():
            acc_ref[...] = jnp.zeros_like(acc_ref)
            xb_ref[...] = _unpack_rows(x_ref[...]).astype(BF16)

        @pl.when(nv_ref[i] > bm // 2)
        def _():
            swiglu_rows(bm)

        @pl.when(nv_ref[i] <= bm // 2)
        def _():
            swiglu_rows(bm // 2)

        @pl.when(j == pl.num_programs(1) - 1)
        def _():
            y_ref[...] = _pack_rows(acc_ref[...])


def _moe_experts(xs, w1, w2, block_e, n_used, n_valid, n_blocks):
    bm, fc = MOE_ROWS, MOE_FF
    nff = FF_EXPERT // fc
    used = lambda i, nu: jnp.minimum(i, nu[0] - 1)
    ffi = lambda i, j, nu: jnp.where(i < nu[0], j, nff - 1)
    grid_spec = pltpu.PrefetchScalarGridSpec(
        num_scalar_prefetch=3,
        grid=(n_blocks, nff),
        in_specs=[pl.BlockSpec((bm, D_MODEL // 2), lambda i, j, be, nu, nv: (used(i, nu), 0)),
                  pl.BlockSpec((1, D_MODEL, fc), lambda i, j, be, nu, nv: (be[used(i, nu)], 0, ffi(i, j, nu))),
                  pl.BlockSpec((1, D_MODEL, fc),
                               lambda i, j, be, nu, nv: (be[used(i, nu)], 0, nff + ffi(i, j, nu))),
                  pl.BlockSpec((1, fc, D_MODEL), lambda i, j, be, nu, nv: (be[used(i, nu)], ffi(i, j, nu), 0))],
        out_specs=pl.BlockSpec((bm, D_MODEL // 2), lambda i, j, be, nu, nv: (used(i, nu), 0)),
        scratch_shapes=[pltpu.VMEM((bm, D_MODEL), BF16), pltpu.VMEM((bm, D_MODEL), F32)],
    )
    return pl.pallas_call(
        _moe_kernel,
        grid_spec=grid_spec,
        out_shape=jax.ShapeDtypeStruct((n_blocks * bm, D_MODEL // 2), jnp.uint32),
        compiler_params=_cparams("arbitrary", "arbitrary"),
        name="moe_experts",
    )(block_e, n_used, n_valid, xs, w1, w1, w2)


def _combine_kernel(y0_ref, y1_ref, gate_ref, x_ref, modl_ref, g_ref, o_ref):
    gt = gate_ref[...]
    y = gt[:, 0:1] * _unpack_rows(y0_ref[...]) + gt[:, 1:2] * _unpack_rows(y1_ref[...])
    o_ref[...] = x_ref[...] + modl_ref[0, 5:6, :] * _rms(y, g_ref[...])


def _combine(yk, gates, x2, modl, g, bsz, t, ctx_len):
    tm = ROW_TILE
    nt, nct = t // tm, ctx_len // tm
    nlt = nt - nct
    nl = bsz * (t - ctx_len)
    orow = lambda c: pl.BlockSpec((tm, c), lambda b, i: (b * nlt + i, 0))
    second = pl.BlockSpec((tm, D_MODEL // 2), lambda b, i: (nl // tm + b * nlt + i, 0))
    return pl.pallas_call(
        _combine_kernel,
        grid=(bsz, nlt),
        in_specs=[orow(D_MODEL // 2), second, orow(LANES),
                  pl.BlockSpec((tm, D_MODEL), lambda b, i: (b * nt + nct + i, 0)),
                  pl.BlockSpec((1, 8, D_MODEL), lambda b, i: (b, 0, 0)), _resident((1, D_MODEL))],
        out_specs=orow(D_MODEL),
        out_shape=jax.ShapeDtypeStruct((nl, D_MODEL), F32),
        compiler_params=_cparams("arbitrary", "arbitrary"),
        name="moe_combine",
    )(yk, yk, gates, x2, modl, g)


def _routing_tables(experts, ranks, counts):
    bm = MOE_ROWS
    n_assign = experts.size
    padded = (counts + bm - 1) // bm * bm
    ends_p = jnp.cumsum(padded)
    pstarts = ends_p - padded
    onehot = experts[..., None] == jnp.arange(N_EXPERTS, dtype=I32)
    slot = (jnp.sum(jnp.where(onehot, pstarts, 0), axis=-1) + ranks).astype(I32)
    n_blocks = n_assign // bm + N_EXPERTS
    block_e = jnp.minimum(jnp.sum(ends_p[None, :] <= (jnp.arange(n_blocks, dtype=I32) * bm)[:, None], axis=1),
                          N_EXPERTS - 1).astype(I32)
    n_used = (ends_p[-1] // bm).astype(I32).reshape(1)
    sel = block_e[:, None] == jnp.arange(N_EXPERTS, dtype=I32)
    rows_left = jnp.sum(jnp.where(sel, counts + pstarts, 0), axis=1) - jnp.arange(n_blocks, dtype=I32) * bm
    n_valid = jnp.clip(rows_left, 0, bm).astype(I32)
    return slot, block_e, n_used, n_valid, n_blocks


def kernel(x, c, ctx, c_ctx, mod_w, mod_b, norm_g, ab_in_w, m_conv_w, m_conv_b, m_gate_b, m_norm_w, l_conv_w,
           l_conv_b, l_wa, l_ba, l_wx, l_bx, l_lam, ab_out_w, ffn_w1, ffn_w2, cd_in_w, g_alpha_w, g_alpha_b,
           g_norm_w, s_conv_w, s_conv_b, s_dt_bias, s_A_log, s_D, s_norm_w, cd_out_w, router_w, router_b,
           moe_w1, moe_w2):
    bsz, seq, dm = x.shape
    ctx_len = ctx.shape[1]
    t = ctx_len + seq
    n = bsz * t
    assert ctx_len == ROW_TILE and seq % ROW_TILE == 0 and ROW_TILE % GRID_W == 0

    c_all = jnp.concatenate([c, c_ctx[None, :], jnp.zeros((-(bsz + 1) % 8, dm), F32)], axis=0)
    mods = _modulation(c_all, mod_w, mod_b).reshape(mod_w.shape[0], c_all.shape[0], 6, dm)
    pad2 = jnp.zeros((bsz, 2, dm), F32)

    def layer_mods(layer):
        modl = jnp.concatenate([mods[layer, :bsz], pad2], axis=1)
        modc = jnp.concatenate([mods[layer, bsz], pad2[0]], axis=0)
        return modl, modc

    modl, modc = layer_mods(0)
    g = norm_g[0]
    w = ab_in_w[0]
    o_qk, o_v, o_o, o_gt, o_lx, o_lg = 0, 1024, 2048, 3072, 3088, 4112
    w_in = jnp.concatenate([w[:, o_qk:o_v], w[:, o_v:o_o], w[:, o_o:o_gt], w[:, o_lx:o_lg], w[:, o_lg:],
                            w[:, o_gt:o_lx], jnp.zeros((dm, LANES - 16), F32)], axis=1).astype(BF16)
    gb = jnp.concatenate([m_gate_b[0].reshape(1, 16), jnp.zeros((1, LANES - 16), F32)], axis=1)
    x2, q, k, v, og, lx, glg, gates, gates_t = _in_ab(
        ctx.reshape(bsz * ctx_len, dm), x.reshape(bsz * seq, dm), modl, modc, g[0:1], w_in, m_conv_w[0],
        m_conv_b[0][None], l_conv_w[0], l_conv_b[0][None], gb, bsz, t, ctx_len)
    hm_f, hm_b = _mlstm(q, k, v, gates, gates_t, bsz, t, ctx_len)
    hl_f, hl_b = _lru(lx, l_wa[0].astype(BF16), l_wx[0].astype(BF16), l_ba[0][:, None], l_bx[0][:, None],
                      l_lam[0][:, None], bsz, t, ctx_len)
    x2 = _out_proj(_out_ab_kernel, "out_proj_swiglu_ab", [hm_f, hm_b, og, hl_f, hl_b, glg], [m_norm_w[0][None]],
                   x2, modl, modc, g[1:2], ab_out_w[0].astype(BF16), bsz, t, ctx_len,
                   extra=(g[2:3], g[3:4], ffn_w1[0].astype(BF16), ffn_w2[0].astype(BF16)))

    modl, modc = layer_mods(1)
    g = norm_g[1]
    w = cd_in_w[0]
    o_gv, o_gr, o_ga, o_z, o_xbc, o_dt = 1024, 2048, 3072, 3104, 4128, 5664
    zpad = jnp.zeros((dm, LANES - 32), F32)
    w_in = jnp.concatenate([w[:, :o_gv], w[:, o_gv:o_gr], w[:, o_gr:o_ga], w[:, o_z:o_xbc], w[:, o_xbc:o_dt],
                            w[:, o_ga:o_z], zpad, w[:, o_dt:], w[:, o_dt:], jnp.zeros((dm, LANES - 64), F32)],
                           axis=1).astype(BF16)
    dtb = jnp.concatenate([s_dt_bias[0].reshape(1, 32), s_dt_bias[0].reshape(1, 32),
                           jnp.zeros((1, LANES - 64), F32)], axis=1)
    aneg = jnp.concatenate([jnp.zeros((1, 32), F32), -jnp.exp(s_A_log[0].reshape(1, 32)),
                            jnp.zeros((1, LANES - 64), F32)], axis=1)
    gq, gk, gv, grs, zs, sx, sb, sc, ga, dts, dts_t = _in_cd(x2, modl, modc, g[0:1], w_in, s_conv_w[0],
                                                             s_conv_b[0][None], dtb, aneg, bsz, t, ctx_len)
    aw = jnp.zeros((2, LANES, G_HEADS * G_DK), F32)
    aw = aw.at[0, 0:G_RANK].set(g_alpha_w[0, 0]).at[1, G_RANK:2 * G_RANK].set(g_alpha_w[0, 1])
    awh = aw.astype(BF16)
    awl = (aw - awh.astype(F32)).astype(BF16)
    og_f, og_b = _gla(gq, gk, gv, ga, awh, awl, g_alpha_b[0][:, None], bsz, t, ctx_len)
    ys_f, ys_b = _ssd(sx, sb, sc, dts, dts_t, bsz, t, ctx_len)
    dskip = jnp.repeat(s_D[0], S_P)[None]
    x2 = _out_proj(_out_cd_kernel, "out_proj_cd", [og_f, og_b, grs, ys_f, ys_b, sx, zs],
                   [g_norm_w[0][None], dskip, s_norm_w[0][None]],
                   x2, modl, modc, g[1:2], cd_out_w[0].astype(BF16), bsz, t, ctx_len)

    rw = jnp.concatenate([router_w[0], jnp.zeros((dm, LANES - N_EXPERTS), F32)], axis=1)
    rwh = rw.astype(BF16)
    rw = jnp.stack([rwh, (rw - rwh.astype(F32)).astype(BF16)])
    rb = jnp.concatenate([router_b[0][None], jnp.zeros((1, LANES - N_EXPERTS), F32)], axis=1)
    h2, top_idx, top_gate, counts = _router(x2, modl, g[2:3], rw, rb, bsz, t, ctx_len)
    slot, block_e, n_used, n_valid, n_blocks = _routing_tables(top_idx[:, 0:TOP_K], top_idx[:, TOP_K:2 * TOP_K],
                                                               counts[0, :N_EXPERTS].astype(I32))
    xs = _dispatch_rows(h2, slot.reshape(-1), n_blocks * MOE_ROWS)
    ys_e = _moe_experts(xs, moe_w1[0], moe_w2[0], block_e, n_used, n_valid, n_blocks)
    yk = _gather_rows(ys_e, jnp.concatenate([slot[:, 0], slot[:, 1]]))
    out = _combine(yk, top_gate, x2, modl, g[3:4], bsz, t, ctx_len)
    return out.reshape(bsz, seq, dm)
```

```python
import functools

import jax
import jax.numpy as jnp
from jax import lax
from jax.experimental import pallas as pl
from jax.experimental.pallas import tpu as pltpu
from jax.experimental.pallas import tpu_sc as plsc

F32, BF16, I32 = jnp.float32, jnp.bfloat16, jnp.int32

D_MODEL = 1024
GRID_W = 64
EPS = 1e-6
LOG2E = 1.4426950408889634
M_HEADS, M_DK, M_DV = 4, 128, 256
LRU_BLOCKS, LRU_BW, LRU_C = 8, 128, 8.0
G_HEADS, G_DK, G_DV, G_RANK, G_TAU = 4, 128, 256, 16, 16.0
S_HEADS, S_P, S_N, S_GROUPS, S_HPG = 16, 64, 128, 2, 8
FF_DENSE = 2816
N_EXPERTS, TOP_K, FF_EXPERT = 8, 2, 3584

LANES, SUBLANES = 128, 8
VMEM_LIMIT = 56 * 1024 * 1024
ROW_TILE = 256
BIG_ROW_TILE_MAX = 640
PROJ_COLS = 256
M_CHUNK = 256
G_CHUNK, G_SUB = 256, 16
G_GROUP = 64
S_CHUNK = 256
LRU_TILE = 256
FF_STEP = 256
MOE_ROWS = 1024
MOE_FF = 512
SC_CORES, SC_SUBCORES = 2, 16
SC_LANES = 16
SC_GATHER_ROWS = 128


def _cparams(*sem):
    return pltpu.CompilerParams(dimension_semantics=sem, vmem_limit_bytes=VMEM_LIMIT)


def _sigmoid(x):
    return 0.5 + 0.5 * jnp.tanh(0.5 * x)


def _silu(x):
    return x * _sigmoid(x)


def _softplus(x):
    return jnp.maximum(x, 0.0) + jnp.log1p(jnp.exp(-jnp.abs(x)))


def _log_sigmoid(x):
    return jnp.minimum(x, 0.0) - jnp.log1p(jnp.exp(-jnp.abs(x)))


def _dot(a, b):
    return jnp.dot(a, b, preferred_element_type=F32)


def _dot_nt(a, b):
    return lax.dot_general(a, b, (((1,), (1,)), ((), ())), preferred_element_type=F32)


def _dot_tn(a, b):
    return lax.dot_general(a, b, (((0,), (0,)), ((), ())), preferred_element_type=F32)


def _split3(f):
    f1 = f.astype(BF16)
    r = f - f1.astype(F32)
    f2 = r.astype(BF16)
    f3 = (r - f2.astype(F32)).astype(BF16)
    return f1, f2, f3


def _sel_cols(mask01, f):
    p1, p2, p3 = _split3(f)
    return _dot(mask01, p1) + _dot(mask01, p2) + _dot(mask01, p3)


def _sel_rows(f, mask01):
    p1, p2, p3 = _split3(f)
    return _dot_nt(p1, mask01) + _dot_nt(p2, mask01) + _dot_nt(p3, mask01)


def _pack_rows(u):
    n = u.shape[1] // 2
    bits = lax.bitcast_convert_type(u.astype(BF16).astype(F32), jnp.uint32)
    return jnp.bitwise_or(bits[:, :n], jnp.right_shift(bits[:, n:], jnp.uint32(16)))


def _unpack_rows(p):
    hi = lax.bitcast_convert_type(jnp.bitwise_and(p, jnp.uint32(0xFFFF0000)), F32)
    lo = lax.bitcast_convert_type(jnp.left_shift(p, jnp.uint32(16)), F32)
    return jnp.concatenate([hi, lo], axis=1)


def _rms(u, g):
    return u * lax.rsqrt(jnp.mean(u * u, axis=-1, keepdims=True) + EPS) * g


def _head_rms(u, g, n_heads):
    w = u.shape[-1] // n_heads
    return jnp.concatenate([_rms(u[:, h * w:(h + 1) * w], g[:, h * w:(h + 1) * w]) for h in range(n_heads)], axis=1)


def _row_ids(tile_idx, tm):
    return tile_idx * tm + lax.broadcasted_iota(I32, (tm, 1), 0)


def _mod(modl_ref, modc_ref, idx, is_ctx):
    return jnp.where(is_ctx, modc_ref[idx:idx + 1, :], modl_ref[0, idx:idx + 1, :])


def _adaln(x, g, modl_ref, modc_ref, shift_idx, is_ctx):
    shift = _mod(modl_ref, modc_ref, shift_idx, is_ctx)
    scale = _mod(modl_ref, modc_ref, shift_idx + 1, is_ctx)
    return _rms(x, g) * (1.0 + scale) + shift


def _neighbour_tiles(r, step, is_lat):
    n_tiles, per_seg = r.shape[0], GRID_W // SUBLANES
    zero = jnp.zeros((1,) + r.shape[1:], r.dtype)

    def across(src):
        return jnp.where(is_lat, 0.0, r[src:src + 1]) if 0 <= src < n_tiles else zero

    pieces = []
    for s0 in range(0, n_tiles, per_seg):
        if step > 0:
            pieces += [across(s0 - 1), r[s0:s0 + per_seg - 1]]
        else:
            pieces += [r[s0 + 1:s0 + per_seg], across(s0 + per_seg)]
    return jnp.concatenate(pieces, axis=0)


def _dwconv(y, cw, cb, is_lat):
    tm, n = y.shape
    y3 = y.reshape(tm // SUBLANES, SUBLANES, n)
    sub = lax.broadcasted_iota(I32, (1, SUBLANES, 1), 1)
    w = lambda j: cw[j:j + 1, :].reshape(1, 1, n)
    r1, r2, r7 = pltpu.roll(y3, 1, 1), pltpu.roll(y3, 2, 1), pltpu.roll(y3, SUBLANES - 1, 1)
    ym1 = jnp.where(sub >= 1, r1, _neighbour_tiles(r1, 1, is_lat))
    ym2 = jnp.where(sub >= 2, r2, _neighbour_tiles(r2, 1, is_lat))
    yp1 = jnp.where(sub <= SUBLANES - 2, r7, _neighbour_tiles(r7, -1, is_lat))
    out = cb.reshape(1, 1, n) + w(0) * ym2 + w(1) * ym1 + w(2) * y3 + w(3) * yp1
    return out.reshape(tm, n)


def _resident(shape):
    nd = len(shape)
    return pl.BlockSpec(shape, lambda *_: (0,) * nd, pipeline_mode=pl.Buffered(1))


def _big_row_tile(t):
    return max(tm for tm in range(8, BIG_ROW_TILE_MAX + 1, 8) if t % tm == 0)


def _mod_kernel(c_ref, w_ref, b_ref, o_ref):
    a = _silu(c_ref[...])
    o_ref[0] = jnp.dot(a, w_ref[0], preferred_element_type=F32, precision=lax.Precision.HIGHEST) + b_ref[0]


def _modulation(c_all, mod_w, mod_b):
    depth, d, n6 = mod_w.shape
    rows = c_all.shape[0]
    tn = 1024
    return pl.pallas_call(
        _mod_kernel,
        grid=(depth, n6 // tn),
        in_specs=[pl.BlockSpec((rows, d), lambda l, j: (0, 0)),
                  pl.BlockSpec((1, d, tn), lambda l, j: (l, 0, j)),
                  pl.BlockSpec((1, 1, tn), lambda l, j: (l, 0, j))],
        out_specs=pl.BlockSpec((1, rows, tn), lambda l, j: (l, 0, j)),
        out_shape=jax.ShapeDtypeStruct((depth, rows, n6), F32),
        compiler_params=_cparams("arbitrary", "arbitrary"),
        name="modulation",
    )(c_all, mod_w, mod_b.reshape(depth, 1, n6))


def _project(h_ref, w_ref, w_col, out_ref, epilogue):
    for c in range(0, out_ref.shape[1], PROJ_COLS):
        y = _dot(h_ref[...], w_ref[:, w_col + c:w_col + c + PROJ_COLS])
        out_ref[:, c:c + PROJ_COLS] = epilogue(y, c).astype(out_ref.dtype)


def _in_ab_kernel(tm, ctx_len, xc_ref, xl_ref, modl_ref, modc_ref, g_ref, w_ref, mcw_ref, mcb_ref, lcw_ref,
                  lcb_ref, gb_ref, x_ref, q_ref, k_ref, v_ref, og_ref, lx_ref, glg_ref, gates_ref, gates_t_ref,
                  h_scr):
    r = _row_ids(pl.program_id(1), tm)
    is_ctx = r < ctx_len
    is_lat = pl.program_id(1) * tm >= ctx_len
    x_ref[...] = jnp.where(is_lat, xl_ref[...], xc_ref[...])
    h_scr[...] = _adaln(x_ref[...], g_ref[...], modl_ref, modc_ref, 0, is_ctx).astype(BF16)

    def conv(cw_ref, cb_ref):
        return lambda y, c: _dwconv(y, cw_ref[:, c:c + PROJ_COLS], cb_ref[:, c:c + PROJ_COLS], is_lat)

    qc, kc, lc = conv(mcw_ref, mcb_ref), conv(mcw_ref, mcb_ref), conv(lcw_ref, lcb_ref)
    _project(h_scr, w_ref, 0, q_ref, lambda y, c: _silu(qc(y, c)))
    _project(h_scr, w_ref, 512, k_ref, lambda y, c: _silu(kc(y, 512 + c)) * (M_DK ** -0.5))
    _project(h_scr, w_ref, 1024, v_ref, lambda y, c: y)
    _project(h_scr, w_ref, 2048, og_ref, lambda y, c: _sigmoid(y))
    _project(h_scr, w_ref, 3072, lx_ref, lc)
    _project(h_scr, w_ref, 4096, glg_ref, lambda y, c: jax.nn.gelu(y))
    gt = _dot(h_scr[...], w_ref[:, 5120:5248]) + gb_ref[...]
    lane = lax.broadcasted_iota(I32, gt.shape, 1)
    gates = jnp.where(lane >= 2 * M_HEADS, _log_sigmoid(gt), gt)
    gates_ref[...] = gates
    gates_t_ref[...] = gates.T


def _in_ab(xc2, xl2, modl, modc, g, w, mcw, mcb, lcw, lcb, gb, bsz, t, ctx_len):
    tm = ROW_TILE
    nt, nct = t // tm, ctx_len // tm
    n = bsz * t
    row = lambda c: pl.BlockSpec((tm, c), lambda b, i: (b * nt + i, 0))
    outs = [(D_MODEL, F32), (512, BF16), (512, BF16), (1024, BF16), (1024, BF16), (1024, BF16), (1024, BF16),
            (LANES, F32)]
    return pl.pallas_call(
        functools.partial(_in_ab_kernel, tm, ctx_len),
        grid=(bsz, nt),
        in_specs=[pl.BlockSpec((tm, D_MODEL), lambda b, i: (b * nct + jnp.minimum(i, nct - 1), 0)),
                  pl.BlockSpec((tm, D_MODEL), lambda b, i: (b * (nt - nct) + jnp.maximum(i - nct, 0), 0)),
                  pl.BlockSpec((1, 8, D_MODEL), lambda b, i: (b, 0, 0)),
                  _resident((8, D_MODEL)), _resident((1, D_MODEL)), _resident(w.shape),
                  _resident(mcw.shape), _resident(mcb.shape), _resident(lcw.shape), _resident(lcb.shape),
                  _resident(gb.shape)],
        out_specs=[row(c) for c, _ in outs] + [pl.BlockSpec((LANES, tm), lambda b, i: (0, b * nt + i))],
        out_shape=[jax.ShapeDtypeStruct((n, c), dt) for c, dt in outs] + [jax.ShapeDtypeStruct((LANES, n), F32)],
        scratch_shapes=[pltpu.VMEM((tm, D_MODEL), BF16)],
        compiler_params=_cparams("arbitrary", "arbitrary"),
        name="in_proj_ab",
    )(xc2, xl2, modl, modc, g, w, mcw, mcb, lcw, lcb, gb)


def _reverse_order(n_ctx_chunks, n_chunks):
    return lambda c: jnp.where(c < n_ctx_chunks, n_ctx_chunks - 1 - c, n_chunks - 1 - (c - n_ctx_chunks))


def _tri_mask(L, reverse):
    row = lax.broadcasted_iota(I32, (L, L), 0)
    col = lax.broadcasted_iota(I32, (L, L), 1)
    return (col >= row) if reverse else (col <= row)


def _bidir_scan(kern, L, row_ins, lane_ins, res_ins, out_cols, scratch, bsz, t, ctx_len, name):
    nc = t // L
    rev = _reverse_order(ctx_len // L, nc)
    fwd = lambda i: i
    n = bsz * t

    def row(c, order):
        return pl.BlockSpec((L, c), lambda b, i: (b * nc + order(i), 0))

    def lanes(r, order):
        return pl.BlockSpec((r, L), lambda b, i: (0, b * nc + order(i)))

    in_specs, args = [], []
    for order in (fwd, rev):
        in_specs += [row(a.shape[1], order) for a in row_ins] + [lanes(r, order) for _, r in lane_ins]
        args += list(row_ins) + [a for a, _ in lane_ins]
    in_specs += [_resident(a.shape) for a in res_ins]
    args += list(res_ins)
    return pl.pallas_call(
        kern,
        grid=(bsz, nc),
        in_specs=in_specs,
        out_specs=[row(c, fwd) for c in out_cols] + [row(c, rev) for c in out_cols],
        out_shape=[jax.ShapeDtypeStruct((n, c), BF16) for c in out_cols] * 2,
        scratch_shapes=scratch,
        compiler_params=_cparams("arbitrary", "arbitrary"),
        name=name,
    )(*args)


def _zero_at_start(*scratch):
    @pl.when(pl.program_id(1) == 0)
    def _():
        for s in scratch:
            s[...] = jnp.zeros_like(s)


def _mlstm_kernel(L, qf, kf, vf, gcf, grf, qb, kb, vb, gcb, grb, of, ob, c_scr, m_scr):
    _zero_at_start(c_scr, m_scr)
    ones = jnp.ones((L, LANES), BF16)
    wide = lambda c, k: jnp.concatenate([c] * k, axis=1)
    er = lax.broadcasted_iota(I32, (LANES, 2 * M_HEADS * LANES), 0)
    ej = jnp.right_shift(lax.broadcasted_iota(I32, (LANES, 2 * M_HEADS * LANES), 1), LANES.bit_length() - 1)
    chains = []
    for d, (q_ref, k_ref, v_ref, gc_ref, gr_ref, out_ref) in enumerate(((qf, kf, vf, gcf, grf, of),
                                                                         (qb, kb, vb, gcb, grb, ob))):
        reverse = d == 1
        mask = _tri_mask(L, reverse)
        mask01 = mask.astype(BF16)
        gc = gc_ref[...]
        gr = gr_ref[...]
        bcol = _sel_cols(mask01, gc)
        brow = _sel_rows(gr, mask01)
        glane = lax.broadcasted_iota(I32, (L, LANES), 1)
        src = jnp.where(ej < M_HEADS, M_HEADS * d + ej, M_HEADS + M_HEADS * d + ej)
        s1, s2, _ = _split3(jnp.where(glane < 2 * M_HEADS, gc, bcol))
        sel = (er == src).astype(BF16)
        rep = _dot(s1, sel) + _dot(s2, sel)
        last = 0 if reverse else L - 1
        for h in range(M_HEADS):
            fo, io = 2 * M_HEADS + M_HEADS * d + h, M_HEADS * d + h
            ic, bc = rep[:, h * LANES:(h + 1) * LANES], rep[:, (M_HEADS + h) * LANES:(M_HEADS + h + 1) * LANES]
            r1 = gr[io:io + 1, :] - brow[fo:fo + 1, :]
            m = m_scr[d, h:h + 1, :]
            g = bc + m
            mx = jnp.max(jnp.where(mask, r1, -jnp.inf), axis=1, keepdims=True)
            mt = jnp.maximum(g, bc + mx)
            p = jnp.exp(jnp.where(mask, wide(bc - mt, L // LANES) + r1, -jnp.inf))
            bl = bc[last:last + 1, :]
            a_s = bl - bc + ic
            m_new = jnp.maximum(bl + m, jnp.max(a_s, axis=0, keepdims=True))
            chains.append(dict(
                d=d, h=h, out_ref=out_ref, mt=mt, w_inter=jnp.exp(g - mt), p=p,
                m_new=m_new, decay=jnp.exp(bl + m - m_new), ws=jnp.exp(a_s - m_new),
                qh=q_ref[:, h * M_DK:(h + 1) * M_DK], kh=k_ref[:, h * M_DK:(h + 1) * M_DK],
                vx=jnp.concatenate([v_ref[:, h * M_DV:(h + 1) * M_DV], ones], axis=1)))
    for ch in chains:
        ch["sc"] = (_dot_nt(ch["qh"], ch["kh"]) * ch["p"]).astype(BF16)
    for ch in chains:
        cx = c_scr[ch["d"], ch["h"]]
        numx = wide(ch["w_inter"], 1 + M_DV // LANES) * _dot(ch["qh"], cx.astype(BF16)) + _dot(ch["sc"], ch["vx"])
        den = numx[:, M_DV:]
        rinv = 1.0 / jnp.maximum(jnp.abs(den), jnp.exp(-ch["mt"]))
        hh = numx[:, 0:M_DV] * wide(rinv, M_DV // LANES)
        ch["out_ref"][:, ch["h"] * M_DV:(ch["h"] + 1) * M_DV] = hh.astype(BF16)
    for ch in chains:
        d, h = ch["d"], ch["h"]
        kw = (ch["kh"].astype(F32) * ch["ws"]).astype(BF16)
        c_scr[d, h] = wide(ch["decay"], 1 + M_DV // LANES) * c_scr[d, h] + _dot_tn(kw, ch["vx"])
        m_scr[d, h:h + 1, :] = ch["m_new"]


def _mlstm(q, k, v, gcol, grow, bsz, t, ctx_len):
    L = M_CHUNK
    scratch = [pltpu.VMEM((2, M_HEADS, M_DK, M_DV + LANES), F32), pltpu.VMEM((2, 8, LANES), F32)]
    return _bidir_scan(functools.partial(_mlstm_kernel, L), L, [q, k, v, gcol], [(grow, 16)], [], [1024], scratch,
                       bsz, t, ctx_len, "mlstm_scan")


def _lru_kernel(L, uf, ub_, wa_ref, wx_ref, ba_ref, bx_ref, lam_ref, of, ob, a_scr, b_scr, h_scr):
    _zero_at_start(h_scr)
    R, C = L // SUBLANES, LRU_BLOCKS * LRU_BW
    i0, i1 = lax.broadcasted_iota(I32, (L, L), 0), lax.broadcasted_iota(I32, (L, L), 1)
    when = lambda p: jnp.bitwise_and(p, SUBLANES - 1) * R + jnp.right_shift(p, SUBLANES.bit_length() - 1)
    regroup = (i1 == when(i0)).astype(BF16)
    ungroup = (i0 == when(i1)).astype(BF16)
    sub = lax.broadcasted_iota(I32, (SUBLANES, 1), 0)
    dirs = ((uf, of), (ub_, ob))
    for d, (u_ref, _) in enumerate(dirs):
        u = _dot(regroup, u_ref[...])
        ub = u.astype(BF16)
        sp = -LRU_C * _softplus(-lam_ref[d])
        for n in range(LRU_BLOCKS):
            sl = slice(n * LRU_BW, (n + 1) * LRU_BW)
            rg = _sigmoid(_dot(ub[:, sl], wa_ref[d, n]) + ba_ref[d, :, sl])
            ig = _sigmoid(_dot(ub[:, sl], wx_ref[d, n]) + bx_ref[d, :, sl])
            log_a = rg * sp[:, sl]
            a = jnp.exp(log_a)
            a_scr[d, :, sl] = a
            b_scr[d, :, sl] = jnp.sqrt(-jnp.tanh(log_a) * (a * a + 1.0)) * (ig * u[:, sl])
    S = SUBLANES
    hs = [jnp.zeros((S, C), F32), jnp.zeros((S, C), F32)]
    ps = [jnp.ones((S, C), F32), jnp.ones((S, C), F32)]
    for j in range(R):
        for d in range(2):
            rows = slice(S * j, S * j + S) if d == 0 else slice(S * (R - 1 - j), S * (R - j))
            a = a_scr[d, rows, :]
            hs[d] = a * hs[d] + b_scr[d, rows, :]
            ps[d] = a * ps[d]
            b_scr[d, rows, :] = hs[d]
            a_scr[d, rows, :] = ps[d]
    for d, (_, out_ref) in enumerate(dirs):
        c = h_scr[d:d + 1, :]
        cin = jnp.zeros((S, C), F32)
        for s in (range(S) if d == 0 else reversed(range(S))):
            cin = jnp.where(sub == s, c, cin)
            c = ps[d][s:s + 1, :] * c + hs[d][s:s + 1, :]
        h_scr[d:d + 1, :] = c
        h = b_scr[d].reshape(R, S, C) + a_scr[d].reshape(R, S, C) * cin[None]
        out_ref[...] = _dot(ungroup, h.reshape(L, C).astype(BF16)).astype(BF16)


def _lru(u, wa, wx, ba, bx, lam, bsz, t, ctx_len):
    L = LRU_TILE
    scratch = [pltpu.VMEM((2, L, 1024), F32), pltpu.VMEM((2, L, 1024), F32), pltpu.VMEM((8, 1024), F32)]
    return _bidir_scan(functools.partial(_lru_kernel, L), L, [u], [], [wa, wx, ba, bx, lam], [1024], scratch,
                       bsz, t, ctx_len, "rglru_scan")


def _swiglu_rows(x, is_ctx, modl_ref, modc_ref, g2_ref, g3_ref, w1_ref, w2_ref):
    h = _adaln(x, g2_ref[...], modl_ref, modc_ref, 3, is_ctx).astype(BF16)
    ff = w2_ref.shape[0]
    acc = jnp.zeros(x.shape, F32)
    for j in range(ff // FF_STEP):
        lo = j * FF_STEP
        gj = _dot(h, w1_ref[:, lo:lo + FF_STEP])
        uj = _dot(h, w1_ref[:, ff + lo:ff + lo + FF_STEP])
        acc = acc + _dot((_silu(gj) * uj).astype(BF16), w2_ref[lo:lo + FF_STEP, :])
    return x + _mod(modl_ref, modc_ref, 5, is_ctx) * _rms(acc, g3_ref[...])


def _out_tail(tm, ctx_len, a1, a2, x_ref, modl_ref, modc_ref, g_ref, w_ref, xo_ref, ffn_refs=None):
    is_ctx = _row_ids(pl.program_id(1), tm) < ctx_len
    half = w_ref.shape[0] // 2
    y = _dot(a1.astype(BF16), w_ref[0:half, :]) + _dot(a2.astype(BF16), w_ref[half:, :])
    x = x_ref[...] + _mod(modl_ref, modc_ref, 2, is_ctx) * _rms(y, g_ref[...])
    if ffn_refs is not None:
        x = _swiglu_rows(x, is_ctx, modl_ref, modc_ref, *ffn_refs)
    xo_ref[...] = x


def _out_ab_kernel(tm, ctx_len, hf, hb, og, lf, lb, glg, nw, x_ref, modl_ref, modc_ref, g_ref, w_ref,
                   g2_ref, g3_ref, w1_ref, w2_ref, xo_ref):
    hm = _head_rms(hf[...].astype(F32) + hb[...].astype(F32), nw[...], M_HEADS) * og[...].astype(F32)
    hl = (lf[...].astype(F32) + lb[...].astype(F32)) * glg[...].astype(F32)
    _out_tail(tm, ctx_len, hm, hl, x_ref, modl_ref, modc_ref, g_ref, w_ref, xo_ref,
              (g2_ref, g3_ref, w1_ref, w2_ref))


def _out_cd_kernel(tm, ctx_len, gf, gb, grs, yf, yb, sx, zs, gnw, dsk, snw, x_ref, modl_ref, modc_ref, g_ref,
                   w_ref, xo_ref):
    og = _head_rms(gf[...].astype(F32) + gb[...].astype(F32), gnw[...], G_HEADS) * grs[...].astype(F32)
    ys = yf[...].astype(F32) + yb[...].astype(F32) + dsk[...] * sx[...].astype(F32)
    ys = _rms(ys * zs[...].astype(F32), snw[...])
    _out_tail(tm, ctx_len, og, ys, x_ref, modl_ref, modc_ref, g_ref, w_ref, xo_ref)


def _out_proj(kern, name, acts, vecs, x2, modl, modc, g, w, bsz, t, ctx_len, extra=()):
    tm = _big_row_tile(t)
    nt = t // tm
    row = lambda c: pl.BlockSpec((tm, c), lambda b, i: (b * nt + i, 0))
    return pl.pallas_call(
        functools.partial(kern, tm, ctx_len),
        grid=(bsz, nt),
        in_specs=[row(1024)] * len(acts) + [_resident((1, 1024))] * len(vecs)
        + [row(D_MODEL), pl.BlockSpec((1, 8, D_MODEL), lambda b, i: (b, 0, 0)),
           _resident((8, D_MODEL)), _resident((1, D_MODEL)), _resident(w.shape)]
        + [_resident(a.shape) for a in extra],
        out_specs=row(D_MODEL),
        out_shape=jax.ShapeDtypeStruct(x2.shape, F32),
        compiler_params=_cparams("arbitrary", "arbitrary"),
        name=name,
    )(*acts, *vecs, x2, modl, modc, g, w, *extra)


def _in_cd_kernel(tm, ctx_len, x_ref, modl_ref, modc_ref, g_ref, w_ref, cw_ref, cb_ref, dtb_ref, aneg_ref,
                  gq_ref, gk_ref, gv_ref, gr_ref, z_ref, sx_ref, sb_ref, sc_ref, ga_ref, dts_ref, dts_t_ref, h_scr):
    r = _row_ids(pl.program_id(1), tm)
    is_ctx = r < ctx_len
    h_scr[...] = _adaln(x_ref[...], g_ref[...], modl_ref, modc_ref, 0, is_ctx).astype(BF16)
    is_lat = pl.program_id(1) * tm >= ctx_len

    def conv_silu(col0):
        return lambda y, c: _silu(_dwconv(y, cw_ref[:, col0 + c:col0 + c + PROJ_COLS],
                                          cb_ref[:, col0 + c:col0 + c + PROJ_COLS], is_lat))

    _project(h_scr, w_ref, 0, gq_ref, lambda y, c: y * (G_DK ** -0.5))
    _project(h_scr, w_ref, 512, gk_ref, lambda y, c: y)
    _project(h_scr, w_ref, 1024, gv_ref, lambda y, c: y)
    _project(h_scr, w_ref, 2048, gr_ref, lambda y, c: _silu(y))
    _project(h_scr, w_ref, 3072, z_ref, lambda y, c: _silu(y))
    _project(h_scr, w_ref, 4096, sx_ref, conv_silu(0))
    _project(h_scr, w_ref, 5120, sb_ref, conv_silu(1024))
    _project(h_scr, w_ref, 5376, sc_ref, conv_silu(1280))
    ga_ref[...] = _dot(h_scr[...], w_ref[:, 5632:5760])
    dt = _softplus(_dot(h_scr[...], w_ref[:, 5760:5888]) + dtb_ref[...])
    lane = lax.broadcasted_iota(I32, dt.shape, 1)
    dts = jnp.where(lane < 2 * S_HEADS, dt, dt * aneg_ref[...])
    dts_ref[...] = dts
    dts_t_ref[...] = dts.T


def _in_cd(x2, modl, modc, g, w, cw, cb, dtb, aneg, bsz, t, ctx_len):
    tm = ROW_TILE
    nt = t // tm
    n = bsz * t
    row = lambda c: pl.BlockSpec((tm, c), lambda b, i: (b * nt + i, 0))
    outs = [(512, BF16), (512, BF16), (1024, BF16), (1024, BF16), (1024, BF16), (1024, BF16), (256, BF16),
            (256, BF16), (LANES, F32), (LANES, F32)]
    return pl.pallas_call(
        functools.partial(_in_cd_kernel, tm, ctx_len),
        grid=(bsz, nt),
        in_specs=[row(D_MODEL), pl.BlockSpec((1, 8, D_MODEL), lambda b, i: (b, 0, 0)), _resident((8, D_MODEL)),
                  _resident((1, D_MODEL)), _resident(w.shape), _resident(cw.shape), _resident(cb.shape),
                  _resident(dtb.shape), _resident(aneg.shape)],
        out_specs=[row(c) for c, _ in outs] + [pl.BlockSpec((LANES, tm), lambda b, i: (0, b * nt + i))],
        out_shape=[jax.ShapeDtypeStruct((n, c), dt) for c, dt in outs] + [jax.ShapeDtypeStruct((LANES, n), F32)],
        scratch_shapes=[pltpu.VMEM((tm, D_MODEL), BF16)],
        compiler_params=_cparams("arbitrary", "arbitrary"),
        name="in_proj_cd",
    )(x2, modl, modc, g, w, cw, cb, dtb, aneg)


def _gla_kernel(L, C, qf, kf, vf, gaf, qb, kb, vb, gab, awh_ref, awl_ref, ab_ref, of, ob, s_scr, att_scr):
    _zero_at_start(s_scr, att_scr)
    nb = L // C
    chains = []
    for d, (q_ref, k_ref, v_ref, ga_ref, out_ref) in enumerate(((qf, kf, vf, gaf, of), (qb, kb, vb, gab, ob))):
        reverse = d == 1
        mask = _tri_mask(L, reverse)
        g1, g2, _ = _split3(ga_ref[...])
        pre = _dot(g1, awh_ref[d]) + _dot(g2, awh_ref[d]) + _dot(g1, awl_ref[d])
        lg = _log_sigmoid(pre + ab_ref[d]) * (1.0 / G_TAU)
        ball = _sel_cols(mask.astype(BF16), lg)
        for h in range(G_HEADS):
            ks = slice(h * G_DK, (h + 1) * G_DK)
            b = ball[:, ks] * LOG2E
            qh = q_ref[:, ks].astype(F32)
            inter = _dot_nt((qh * jnp.exp2(b)).astype(BF16), s_scr[d, h].astype(BF16))
            chains.append(dict(d=d, h=h, reverse=reverse, mask=mask, b=b, qh=qh, kh=k_ref[:, ks].astype(F32),
                               vh=v_ref[:, h * G_DV:(h + 1) * G_DV], inter=inter, out_ref=out_ref, blocks=[]))
    T = min(G_GROUP, L)
    for i in range(nb):
        lo, hi = i * C, (i + 1) * C
        t0 = lo // T * T
        for c, ch in enumerate(chains):
            b = ch["b"]
            if ch["reverse"]:
                bref = b[hi:hi + 1, :] if i < nb - 1 else jnp.zeros((1, G_DK), F32)
            else:
                bref = b[lo - 1:lo, :] if i > 0 else jnp.zeros((1, G_DK), F32)
            qi = (ch["qh"][lo:hi, :] * jnp.exp2(b[lo:hi, :] - bref)).astype(BF16)
            ki = (ch["kh"][t0:t0 + T] * jnp.exp2(bref - b[t0:t0 + T])).astype(BF16)
            att_scr[c, lo:hi, t0:t0 + T] = _dot_nt(qi, ki)
    for g in range(L // T):
        t0, t1 = g * T, (g + 1) * T
        for c, ch in enumerate(chains):
            b = ch["b"]
            e0, e1 = (t1, L) if ch["reverse"] else (0, t0)
            if e1 > e0:
                bref = b[t1:t1 + 1, :] if ch["reverse"] else b[t0 - 1:t0, :]
                qg = (ch["qh"][t0:t1] * jnp.exp2(b[t0:t1] - bref)).astype(BF16)
                kg = (ch["kh"][e0:e1] * jnp.exp2(bref - b[e0:e1])).astype(BF16)
                att_scr[c, t0:t1, e0:e1] = _dot_nt(qg, kg)
    for c, ch in enumerate(chains):
        att = jnp.where(ch["mask"], att_scr[c], 0.0).astype(BF16)
        oh = ch["inter"] + _dot(att, ch["vh"])
        ch["out_ref"][:, ch["h"] * G_DV:(ch["h"] + 1) * G_DV] = oh.astype(BF16)
    for ch in chains:
        d, h, b = ch["d"], ch["h"], ch["b"]
        last = 0 if ch["reverse"] else L - 1
        bl = b[last:last + 1, :]
        kd = (ch["kh"] * jnp.exp2(bl - b)).astype(BF16)
        s_scr[d, h] = s_scr[d, h] * jnp.exp2(bl) + _dot_tn(ch["vh"], kd)


def _gla(q, k, v, ga, awh, awl, ab, bsz, t, ctx_len):
    L, C = G_CHUNK, G_SUB
    scratch = [pltpu.VMEM((2, G_HEADS, G_DV, G_DK), F32), pltpu.VMEM((2 * G_HEADS, L, L), F32)]
    return _bidir_scan(functools.partial(_gla_kernel, L, C), L, [q, k, v, ga], [], [awh, awl, ab], [1024], scratch,
                       bsz, t, ctx_len, "gla_scan")


def _ssd_kernel(L, xf, bf, cf, gcf, grf, xb, bb, cb_, gcb, grb, of, ob, s_scr):
    _zero_at_start(s_scr)
    er = lax.broadcasted_iota(I32, (LANES, S_HEADS * S_P), 0)
    ec = jnp.right_shift(lax.broadcasted_iota(I32, (LANES, S_HEADS * S_P), 1), S_P.bit_length() - 1)
    glane = lax.broadcasted_iota(I32, (L, LANES), 1)
    lane = lax.broadcasted_iota(I32, (L, 2 * S_P), 1)
    dirs, chains = [], []
    for d, (x_ref, b_ref, c_ref, gc_ref, gr_ref, out_ref) in enumerate(((xf, bf, cf, gcf, grf, of),
                                                                         (xb, bb, cb_, gcb, grb, ob))):
        reverse = d == 1
        mask = _tri_mask(L, reverse)
        mask01 = mask.astype(BF16)
        gc = gc_ref[...]
        gr = gr_ref[...]
        bcol = _sel_cols(mask01, gc)
        brow = _sel_rows(gr, mask01)
        last = 0 if reverse else L - 1
        dt_o, la_o = S_HEADS * d, 2 * S_HEADS + S_HEADS * d
        sel_la = (er == ec + la_o).astype(BF16)
        x = x_ref[...]
        bla = jnp.where((glane >= la_o) & (glane < la_o + S_HEADS), bcol, 0.0)
        dt_at_la = pltpu.roll(gc, 2 * S_HEADS, 1)
        bl = bla[last:last + 1, :]
        ebx = _dot(jnp.exp(bla).astype(BF16), sel_la)
        wx = _dot((jnp.exp(bl - bla) * dt_at_la).astype(BF16), sel_la)
        dirs.append(dict(d=d, mask=mask, gr=gr, x=x, ebx=ebx, last=last, dt_o=dt_o, la_o=la_o, out_ref=out_ref,
                         xw=(x.astype(F32) * wx).astype(BF16),
                         bcol2=bcol * LOG2E, brow2=brow * LOG2E))
        for g in range(S_GROUPS):
            ns = slice(g * S_N, (g + 1) * S_N)
            chains.append(dict(dr=dirs[-1], g=g, bg=b_ref[:, ns], cg=c_ref[:, ns],
                               gs=slice(g * S_HPG * S_P, (g + 1) * S_HPG * S_P)))
    for ch in chains:
        dr = ch["dr"]
        ch["cb"] = _dot_nt(ch["cg"], ch["bg"])
        ch["inter"] = _dot(ch["cg"], s_scr[dr["d"], ch["g"]].astype(BF16)) * dr["ebx"][:, ch["gs"]]
    for ch in chains:
        dr, g = ch["dr"], ch["g"]
        parts = []
        for pr in range(S_HPG // 2):
            e0 = g * S_HPG + 2 * pr
            xp = dr["x"][:, e0 * S_P:(e0 + 2) * S_P]
            res = []
            for e in (e0, e0 + 1):
                col = dr["la_o"] + e
                bc, br = dr["bcol2"][:, col:col + 1], dr["brow2"][col:col + 1, :]
                dtr = dr["gr"][dr["dt_o"] + e:dr["dt_o"] + e + 1, :]
                att = (ch["cb"] * dtr * jnp.exp2(jnp.where(dr["mask"], bc - br, -jnp.inf))).astype(BF16)
                res.append(_dot(att, xp))
            parts.append(jnp.where(lane < S_P, res[0], res[1]))
        dr["out_ref"][:, ch["gs"]] = (jnp.concatenate(parts, axis=1) + ch["inter"]).astype(BF16)
    for ch in chains:
        dr, g, gs = ch["dr"], ch["g"], ch["gs"]
        s_scr[dr["d"], g] = (s_scr[dr["d"], g] * dr["ebx"][dr["last"]:dr["last"] + 1, gs]
                             + _dot_tn(ch["bg"], dr["xw"][:, gs]))


def _ssd(x, bm, cm, gcol, grow, bsz, t, ctx_len):
    L = S_CHUNK
    scratch = [pltpu.VMEM((2, S_GROUPS, S_N, S_HPG * S_P), F32)]
    return _bidir_scan(functools.partial(_ssd_kernel, L), L, [x, bm, cm, gcol], [(grow, 4 * S_HEADS)], [], [1024],
                       scratch,
                       bsz, t, ctx_len, "ssd_scan")


def _router_kernel(x_ref, modl_ref, g_ref, rw_ref, rb_ref, h_ref, idx_ref, gate_ref, cnt_ref, base_scr):
    @pl.when((pl.program_id(0) == 0) & (pl.program_id(1) == 0))
    def _():
        base_scr[...] = jnp.zeros_like(base_scr)

    h = _rms(x_ref[...], g_ref[...]) * (1.0 + modl_ref[0, 4:5, :]) + modl_ref[0, 3:4, :]
    h_ref[...] = _pack_rows(h)
    h1, h2, _ = _split3(h)
    logits = _dot(h1, rw_ref[0]) + _dot(h2, rw_ref[0]) + _dot(h1, rw_ref[1]) + rb_ref[...]
    lane = lax.broadcasted_iota(I32, logits.shape, 1)
    logits = jnp.where(lane < N_EXPERTS, logits, -jnp.inf)
    lanef = lane.astype(F32)
    m1 = jnp.max(logits, axis=1, keepdims=True)
    i1 = jnp.min(jnp.where(logits == m1, lanef, float(LANES)), axis=1, keepdims=True)
    rest = jnp.where(lanef == i1, -jnp.inf, logits)
    m2 = jnp.max(rest, axis=1, keepdims=True)
    i2 = jnp.min(jnp.where(rest == m2, lanef, float(LANES)), axis=1, keepdims=True)
    e = jnp.exp(m2 - m1)
    g1 = 1.0 / (1.0 + e)
    gate_ref[...] = jnp.where(lane == 0, g1, jnp.where(lane == 1, e * g1, 0.0))
    tm = logits.shape[0]
    oh1, oh2 = (lanef == i1).astype(F32), (lanef == i2).astype(F32)
    earlier = (lax.broadcasted_iota(I32, (tm, tm), 1) < lax.broadcasted_iota(I32, (tm, tm), 0)).astype(BF16)
    seen = base_scr[0:1, :] + _dot(earlier, (oh1 + oh2).astype(BF16))
    r1 = jnp.sum(oh1 * seen, axis=1, keepdims=True)
    r2 = jnp.sum(oh2 * seen, axis=1, keepdims=True)
    base_scr[0:1, :] = base_scr[0:1, :] + jnp.sum(oh1 + oh2, axis=0, keepdims=True)
    cnt_ref[...] = base_scr[...]
    idx_ref[...] = jnp.where(lane == 0, i1, jnp.where(lane == 1, i2, jnp.where(lane == 2, r1, jnp.where(
        lane == 3, r2, 0.0)))).astype(I32)


def _router(x2, modl, g, rw, rb, bsz, t, ctx_len):
    tm = ROW_TILE
    nt, nct = t // tm, ctx_len // tm
    nl = bsz * (t - ctx_len)
    nlt = nt - nct
    orow = lambda c: pl.BlockSpec((tm, c), lambda b, i: (b * nlt + i, 0))
    return pl.pallas_call(
        _router_kernel,
        grid=(bsz, nlt),
        in_specs=[pl.BlockSpec((tm, D_MODEL), lambda b, i: (b * nt + nct + i, 0)),
                  pl.BlockSpec((1, 8, D_MODEL), lambda b, i: (b, 0, 0)),
                  _resident((1, D_MODEL)), _resident(rw.shape), _resident(rb.shape)],
        out_specs=[orow(D_MODEL // 2), orow(LANES), orow(LANES), pl.BlockSpec((8, LANES), lambda b, i: (0, 0))],
        out_shape=[jax.ShapeDtypeStruct((nl, D_MODEL // 2), jnp.uint32), jax.ShapeDtypeStruct((nl, LANES), I32),
                   jax.ShapeDtypeStruct((nl, LANES), F32), jax.ShapeDtypeStruct((8, LANES), F32)],
        scratch_shapes=[pltpu.VMEM((8, LANES), F32)],
        compiler_params=_cparams("arbitrary", "arbitrary"),
        name="moe_router",
    )(x2, modl, g, rw, rb)


def _gather_rows(table, idx):
    n_rows, d = idx.shape[0], table.shape[1]
    n_workers = SC_CORES * SC_SUBCORES
    per_worker = n_rows // n_workers
    chunk = SC_GATHER_ROWS
    assert per_worker * n_workers == n_rows and per_worker % chunk == 0
    mesh = plsc.VectorSubcoreMesh(core_axis_name="c", subcore_axis_name="s")

    @functools.partial(
        pl.kernel, mesh=mesh, out_type=jax.ShapeDtypeStruct((n_rows, d), table.dtype),
        scratch_types=[pltpu.VMEM((chunk,), I32), pltpu.VMEM((chunk, d), table.dtype), pltpu.SemaphoreType.DMA],
        name="sc_gather_rows")
    def gather(table_hbm, idx_hbm, out_hbm, idx_v, rows_v, sem):
        base = (lax.axis_index("s") * SC_CORES + lax.axis_index("c")) * per_worker

        @pl.loop(0, per_worker // chunk)
        def _(j):
            off = pl.multiple_of(base + j * chunk, 8)
            pltpu.sync_copy(idx_hbm.at[pl.ds(off, chunk)], idx_v)
            pltpu.async_copy(table_hbm.at[idx_v], rows_v, sem).wait()
            pltpu.sync_copy(rows_v, out_hbm.at[pl.ds(off, chunk)])

    return gather(table, idx)


def _dispatch_rows(table, slot, n_out):
    assert TOP_K == 2
    n_assign, (n_tok, d) = slot.shape[0], table.shape
    n_workers = SC_CORES * SC_SUBCORES
    per_worker = n_out // n_workers
    chunk, vec = SC_GATHER_ROWS, SC_LANES
    assert per_worker * n_workers == n_out and per_worker % chunk == 0 and n_assign % vec == 0
    assert n_out < 3 * n_tok
    mesh = plsc.VectorSubcoreMesh(core_axis_name="c", subcore_axis_name="s")

    @functools.partial(
        pl.kernel, mesh=mesh, out_type=jax.ShapeDtypeStruct((n_out, d), table.dtype),
        scratch_types=[pltpu.VMEM((n_assign,), I32), pltpu.VMEM((per_worker,), I32),
                       pltpu.VMEM((chunk, d), table.dtype), pltpu.SemaphoreType.DMA],
        compiler_params=pltpu.CompilerParams(needs_layout_passes=False),
        name="sc_dispatch_rows")
    def dispatch(table_hbm, slot_hbm, out_hbm, slots_v, tok_v, rows_v, sem):
        base = (lax.axis_index("s") * SC_CORES + lax.axis_index("c")) * per_worker
        pltpu.sync_copy(slot_hbm, slots_v)
        lane = lax.broadcasted_iota(I32, (vec,), 0)

        @pl.loop(0, per_worker // vec)
        def _(j):
            p = base + j * vec + lane
            p = jnp.where(p >= n_tok, p - n_tok, p)
            tok_v[pl.ds(j * vec, vec)] = jnp.where(p >= n_tok, p - n_tok, p)

        @pl.loop(0, n_assign // vec)
        def _(j):
            s = slots_v[pl.ds(j * vec, vec)] - base
            mine = (s >= 0) & (s < per_worker)
            tok = lax.shift_right_logical(j * vec + lane, 1)
            plsc.store_scatter(tok_v, [jnp.where(mine, s, 0)], tok, mask=mine)

        @pl.loop(0, per_worker // chunk)
        def _(j):
            off = pl.multiple_of(j * chunk, 8)
            pltpu.async_copy(table_hbm.at[tok_v.at[pl.ds(off, chunk)]], rows_v, sem).wait()
            pltpu.sync_copy(rows_v, out_hbm.at[pl.ds(base + off, chunk)])

    return dispatch(table, slot)


def _moe_kernel(be_ref, nu_ref, nv_ref, x_ref, w1g_ref, w1u_ref, w2_ref, y_ref, xb_ref, acc_ref):
    i, j = pl.program_id(0), pl.program_id(1)
    bm = x_ref.shape[0]

    def swiglu_rows(rows):
        x = xb_ref[0:rows, :]
        a = (_silu(_dot(x, w1g_ref[0].astype(BF16))) * _dot(x, w1u_ref[0].astype(BF16))).astype(BF16)
        acc_ref[0:rows, :] += _dot(a, w2_ref[0].astype(BF16))

    @pl.when(i < nu_ref[0])
    def _():
        @pl.when(j == 0)
        def _():
            acc_ref[...] = jnp.zeros_like(acc_ref)
            xb_ref[...] = _unpack_rows(x_ref[...]).astype(BF16)

        @pl.when(nv_ref[i] > bm // 2)
        def _():
            swiglu_rows(bm)

        @pl.when(nv_ref[i] <= bm // 2)
        def _():
            swiglu_rows(bm // 2)

        @pl.when(j == pl.num_programs(1) - 1)
        def _():
            y_ref[...] = _pack_rows(acc_ref[...])


def _moe_experts(xs, w1, w2, block_e, n_used, n_valid, n_blocks):
    bm, fc = MOE_ROWS, MOE_FF
    nff = FF_EXPERT // fc
    used = lambda i, nu: jnp.minimum(i, nu[0] - 1)
    ffi = lambda i, j, nu: jnp.where(i < nu[0], j, nff - 1)
    grid_spec = pltpu.PrefetchScalarGridSpec(
        num_scalar_prefetch=3,
        grid=(n_blocks, nff),
        in_specs=[pl.BlockSpec((bm, D_MODEL // 2), lambda i, j, be, nu, nv: (used(i, nu), 0)),
                  pl.BlockSpec((1, D_MODEL, fc), lambda i, j, be, nu, nv: (be[used(i, nu)], 0, ffi(i, j, nu))),
                  pl.BlockSpec((1, D_MODEL, fc),
                               lambda i, j, be, nu, nv: (be[used(i, nu)], 0, nff + ffi(i, j, nu))),
                  pl.BlockSpec((1, fc, D_MODEL), lambda i, j, be, nu, nv: (be[used(i, nu)], ffi(i, j, nu), 0))],
        out_specs=pl.BlockSpec((bm, D_MODEL // 2), lambda i, j, be, nu, nv: (used(i, nu), 0)),
        scratch_shapes=[pltpu.VMEM((bm, D_MODEL), BF16), pltpu.VMEM((bm, D_MODEL), F32)],
    )
    return pl.pallas_call(
        _moe_kernel,
        grid_spec=grid_spec,
        out_shape=jax.ShapeDtypeStruct((n_blocks * bm, D_MODEL // 2), jnp.uint32),
        compiler_params=_cparams("arbitrary", "arbitrary"),
        name="moe_experts",
    )(block_e, n_used, n_valid, xs, w1, w1, w2)


def _combine_kernel(y0_ref, y1_ref, gate_ref, x_ref, modl_ref, g_ref, o_ref):
    gt = gate_ref[...]
    y = gt[:, 0:1] * _unpack_rows(y0_ref[...]) + gt[:, 1:2] * _unpack_rows(y1_ref[...])
    o_ref[...] = x_ref[...] + modl_ref[0, 5:6, :] * _rms(y, g_ref[...])


def _combine(yk, gates, x2, modl, g, bsz, t, ctx_len):
    tm = ROW_TILE
    nt, nct = t // tm, ctx_len // tm
    nlt = nt - nct
    nl = bsz * (t - ctx_len)
    orow = lambda c: pl.BlockSpec((tm, c), lambda b, i: (b * nlt + i, 0))
    second = pl.BlockSpec((tm, D_MODEL // 2), lambda b, i: (nl // tm + b * nlt + i, 0))
    return pl.pallas_call(
        _combine_kernel,
        grid=(bsz, nlt),
        in_specs=[orow(D_MODEL // 2), second, orow(LANES),
                  pl.BlockSpec((tm, D_MODEL), lambda b, i: (b * nt + nct + i, 0)),
                  pl.BlockSpec((1, 8, D_MODEL), lambda b, i: (b, 0, 0)), _resident((1, D_MODEL))],
        out_specs=orow(D_MODEL),
        out_shape=jax.ShapeDtypeStruct((nl, D_MODEL), F32),
        compiler_params=_cparams("arbitrary", "arbitrary"),
        name="moe_combine",
    )(yk, yk, gates, x2, modl, g)


def _routing_tables(experts, ranks, counts):
    bm = MOE_ROWS
    n_assign = experts.size
    padded = (counts + bm - 1) // bm * bm
    ends_p = jnp.cumsum(padded)
    pstarts = ends_p - padded
    onehot = experts[..., None] == jnp.arange(N_EXPERTS, dtype=I32)
    slot = (jnp.sum(jnp.where(onehot, pstarts, 0), axis=-1) + ranks).astype(I32)
    n_blocks = n_assign // bm + N_EXPERTS
    block_e = jnp.minimum(jnp.sum(ends_p[None, :] <= (jnp.arange(n_blocks, dtype=I32) * bm)[:, None], axis=1),
                          N_EXPERTS - 1).astype(I32)
    n_used = (ends_p[-1] // bm).astype(I32).reshape(1)
    sel = block_e[:, None] == jnp.arange(N_EXPERTS, dtype=I32)
    rows_left = jnp.sum(jnp.where(sel, counts + pstarts, 0), axis=1) - jnp.arange(n_blocks, dtype=I32) * bm
    n_valid = jnp.clip(rows_left, 0, bm).astype(I32)
    return slot, block_e, n_used, n_valid, n_blocks


def kernel(x, c, ctx, c_ctx, mod_w, mod_b, norm_g, ab_in_w, m_conv_w, m_conv_b, m_gate_b, m_norm_w, l_conv_w,
           l_conv_b, l_wa, l_ba, l_wx, l_bx, l_lam, ab_out_w, ffn_w1, ffn_w2, cd_in_w, g_alpha_w, g_alpha_b,
           g_norm_w, s_conv_w, s_conv_b, s_dt_bias, s_A_log, s_D, s_norm_w, cd_out_w, router_w, router_b,
           moe_w1, moe_w2):
    bsz, seq, dm = x.shape
    ctx_len = ctx.shape[1]
    t = ctx_len + seq
    n = bsz * t
    assert ctx_len == ROW_TILE and seq % ROW_TILE == 0 and ROW_TILE % GRID_W == 0

    c_all = jnp.concatenate([c, c_ctx[None, :], jnp.zeros((-(bsz + 1) % 8, dm), F32)], axis=0)
    mods = _modulation(c_all, mod_w, mod_b).reshape(mod_w.shape[0], c_all.shape[0], 6, dm)
    pad2 = jnp.zeros((bsz, 2, dm), F32)

    def layer_mods(layer):
        modl = jnp.concatenate([mods[layer, :bsz], pad2], axis=1)
        modc = jnp.concatenate([mods[layer, bsz], pad2[0]], axis=0)
        return modl, modc

    modl, modc = layer_mods(0)
    g = norm_g[0]
    w = ab_in_w[0]
    o_qk, o_v, o_o, o_gt, o_lx, o_lg = 0, 1024, 2048, 3072, 3088, 4112
    w_in = jnp.concatenate([w[:, o_qk:o_v], w[:, o_v:o_o], w[:, o_o:o_gt], w[:, o_lx:o_lg], w[:, o_lg:],
                            w[:, o_gt:o_lx], jnp.zeros((dm, LANES - 16), F32)], axis=1).astype(BF16)
    gb = jnp.concatenate([m_gate_b[0].reshape(1, 16), jnp.zeros((1, LANES - 16), F32)], axis=1)
    x2, q, k, v, og, lx, glg, gates, gates_t = _in_ab(
        ctx.reshape(bsz * ctx_len, dm), x.reshape(bsz * seq, dm), modl, modc, g[0:1], w_in, m_conv_w[0],
        m_conv_b[0][None], l_conv_w[0], l_conv_b[0][None], gb, bsz, t, ctx_len)
    hm_f, hm_b = _mlstm(q, k, v, gates, gates_t, bsz, t, ctx_len)
    hl_f, hl_b = _lru(lx, l_wa[0].astype(BF16), l_wx[0].astype(BF16), l_ba[0][:, None], l_bx[0][:, None],
                      l_lam[0][:, None], bsz, t, ctx_len)
    x2 = _out_proj(_out_ab_kernel, "out_proj_swiglu_ab", [hm_f, hm_b, og, hl_f, hl_b, glg], [m_norm_w[0][None]],
                   x2, modl, modc, g[1:2], ab_out_w[0].astype(BF16), bsz, t, ctx_len,
                   extra=(g[2:3], g[3:4], ffn_w1[0].astype(BF16), ffn_w2[0].astype(BF16)))

    modl, modc = layer_mods(1)
    g = norm_g[1]
    w = cd_in_w[0]
    o_gv, o_gr, o_ga, o_z, o_xbc, o_dt = 1024, 2048, 3072, 3104, 4128, 5664
    zpad = jnp.zeros((dm, LANES - 32), F32)
    w_in = jnp.concatenate([w[:, :o_gv], w[:, o_gv:o_gr], w[:, o_gr:o_ga], w[:, o_z:o_xbc], w[:, o_xbc:o_dt],
                            w[:, o_ga:o_z], zpad, w[:, o_dt:], w[:, o_dt:], jnp.zeros((dm, LANES - 64), F32)],
                           axis=1).astype(BF16)
    dtb = jnp.concatenate([s_dt_bias[0].reshape(1, 32), s_dt_bias[0].reshape(1, 32),
                           jnp.zeros((1, LANES - 64), F32)], axis=1)
    aneg = jnp.concatenate([jnp.zeros((1, 32), F32), -jnp.exp(s_A_log[0].reshape(1, 32)),
                            jnp.zeros((1, LANES - 64), F32)], axis=1)
    gq, gk, gv, grs, zs, sx, sb, sc, ga, dts, dts_t = _in_cd(x2, modl, modc, g[0:1], w_in, s_conv_w[0],
                                                             s_conv_b[0][None], dtb, aneg, bsz, t, ctx_len)
    aw = jnp.zeros((2, LANES, G_HEADS * G_DK), F32)
    aw = aw.at[0, 0:G_RANK].set(g_alpha_w[0, 0]).at[1, G_RANK:2 * G_RANK].set(g_alpha_w[0, 1])
    awh = aw.astype(BF16)
    awl = (aw - awh.astype(F32)).astype(BF16)
    og_f, og_b = _gla(gq, gk, gv, ga, awh, awl, g_alpha_b[0][:, None], bsz, t, ctx_len)
    ys_f, ys_b = _ssd(sx, sb, sc, dts, dts_t, bsz, t, ctx_len)
    dskip = jnp.repeat(s_D[0], S_P)[None]
    x2 = _out_proj(_out_cd_kernel, "out_proj_cd", [og_f, og_b, grs, ys_f, ys_b, sx, zs],
                   [g_norm_w[0][None], dskip, s_norm_w[0][None]],
                   x2, modl, modc, g[1:2], cd_out_w[0].astype(BF16), bsz, t, ctx_len)

    rw = jnp.concatenate([router_w[0], jnp.zeros((dm, LANES - N_EXPERTS), F32)], axis=1)
    rwh = rw.astype(BF16)
    rw = jnp.stack([rwh, (rw - rwh.astype(F32)).astype(BF16)])
    rb = jnp.concatenate([router_b[0][None], jnp.zeros((1, LANES - N_EXPERTS), F32)], axis=1)
    h2, top_idx, top_gate, counts = _router(x2, modl, g[2:3], rw, rb, bsz, t, ctx_len)
    slot, block_e, n_used, n_valid, n_blocks = _routing_tables(top_idx[:, 0:TOP_K], top_idx[:, TOP_K:2 * TOP_K],
                                                               counts[0, :N_EXPERTS].astype(I32))
    xs = _dispatch_rows(h2, slot.reshape(-1), n_blocks * MOE_ROWS)
    ys_e = _moe_experts(xs, moe_w1[0], moe_w2[0], block_e, n_used, n_valid, n_blocks)
    yk = _gather_rows(ys_e, jnp.concatenate([slot[:, 0], slot[:, 1]]))
    out = _combine(yk, top_gate, x2, modl, g[3:4], bsz, t, ctx_len)
    return out.reshape(bsz, seq, dm)
```

```python
import functools

import jax
import jax.numpy as jnp
from jax import lax
from jax.experimental import pallas as pl
from jax.experimental.pallas import tpu as pltpu
from jax.experimental.pallas import tpu_sc as plsc

F32, BF16, I32 = jnp.float32, jnp.bfloat16, jnp.int32

D_MODEL = 1024
GRID_W = 64
EPS = 1e-6
LOG2E = 1.4426950408889634
M_HEADS, M_DK, M_DV = 4, 128, 256
LRU_BLOCKS, LRU_BW, LRU_C = 8, 128, 8.0
G_HEADS, G_DK, G_DV, G_RANK, G_TAU = 4, 128, 256, 16, 16.0
S_HEADS, S_P, S_N, S_GROUPS, S_HPG = 16, 64, 128, 2, 8
FF_DENSE = 2816
N_EXPERTS, TOP_K, FF_EXPERT = 8, 2, 3584

LANES, SUBLANES = 128, 8
VMEM_LIMIT = 56 * 1024 * 1024
ROW_TILE = 256
BIG_ROW_TILE_MAX = 640
PROJ_COLS = 256
M_CHUNK = 256
G_CHUNK, G_SUB = 256, 16
G_GROUP = 64
S_CHUNK = 256
LRU_TILE = 256
FF_STEP = 256
MOE_ROWS = 1024
MOE_FF = 512
SC_CORES, SC_SUBCORES = 2, 16
SC_LANES = 16
SC_GATHER_ROWS = 128


def _cparams(*sem):
    return pltpu.CompilerParams(dimension_semantics=sem, vmem_limit_bytes=VMEM_LIMIT)


def _sigmoid(x):
    return 0.5 + 0.5 * jnp.tanh(0.5 * x)


def _silu(x):
    return x * _sigmoid(x)


def _softplus(x):
    return jnp.maximum(x, 0.0) + jnp.log1p(jnp.exp(-jnp.abs(x)))


def _log_sigmoid(x):
    return jnp.minimum(x, 0.0) - jnp.log1p(jnp.exp(-jnp.abs(x)))


def _dot(a, b):
    return jnp.dot(a, b, preferred_element_type=F32)


def _dot_nt(a, b):
    return lax.dot_general(a, b, (((1,), (1,)), ((), ())), preferred_element_type=F32)


def _dot_tn(a, b):
    return lax.dot_general(a, b, (((0,), (0,)), ((), ())), preferred_element_type=F32)


def _split3(f):
    f1 = f.astype(BF16)
    r = f - f1.astype(F32)
    f2 = r.astype(BF16)
    f3 = (r - f2.astype(F32)).astype(BF16)
    return f1, f2, f3


def _sel_cols(mask01, f):
    p1, p2, p3 = _split3(f)
    return _dot(mask01, p1) + _dot(mask01, p2) + _dot(mask01, p3)


def _sel_rows(f, mask01):
    p1, p2, p3 = _split3(f)
    return _dot_nt(p1, mask01) + _dot_nt(p2, mask01) + _dot_nt(p3, mask01)


def _pack_rows(u):
    n = u.shape[1] // 2
    bits = lax.bitcast_convert_type(u.astype(BF16).astype(F32), jnp.uint32)
    return jnp.bitwise_or(bits[:, :n], jnp.right_shift(bits[:, n:], jnp.uint32(16)))


def _unpack_rows(p):
    hi = lax.bitcast_convert_type(jnp.bitwise_and(p, jnp.uint32(0xFFFF0000)), F32)
    lo = lax.bitcast_convert_type(jnp.left_shift(p, jnp.uint32(16)), F32)
    return jnp.concatenate([hi, lo], axis=1)


def _rms(u, g):
    return u * lax.rsqrt(jnp.mean(u * u, axis=-1, keepdims=True) + EPS) * g


def _head_rms(u, g, n_heads):
    w = u.shape[-1] // n_heads
    return jnp.concatenate([_rms(u[:, h * w:(h + 1) * w], g[:, h * w:(h + 1) * w]) for h in range(n_heads)], axis=1)


def _row_ids(tile_idx, tm):
    return tile_idx * tm + lax.broadcasted_iota(I32, (tm, 1), 0)


def _mod(modl_ref, modc_ref, idx, is_ctx):
    return jnp.where(is_ctx, modc_ref[idx:idx + 1, :], modl_ref[0, idx:idx + 1, :])


def _adaln(x, g, modl_ref, modc_ref, shift_idx, is_ctx):
    shift = _mod(modl_ref, modc_ref, shift_idx, is_ctx)
    scale = _mod(modl_ref, modc_ref, shift_idx + 1, is_ctx)
    return _rms(x, g) * (1.0 + scale) + shift


def _neighbour_tiles(r, step, is_lat):
    n_tiles, per_seg = r.shape[0], GRID_W // SUBLANES
    zero = jnp.zeros((1,) + r.shape[1:], r.dtype)

    def across(src):
        return jnp.where(is_lat, 0.0, r[src:src + 1]) if 0 <= src < n_tiles else zero

    pieces = []
    for s0 in range(0, n_tiles, per_seg):
        if step > 0:
            pieces += [across(s0 - 1), r[s0:s0 + per_seg - 1]]
        else:
            pieces += [r[s0 + 1:s0 + per_seg], across(s0 + per_seg)]
    return jnp.concatenate(pieces, axis=0)


def _dwconv(y, cw, cb, is_lat):
    tm, n = y.shape
    y3 = y.reshape(tm // SUBLANES, SUBLANES, n)
    sub = lax.broadcasted_iota(I32, (1, SUBLANES, 1), 1)
    w = lambda j: cw[j:j + 1, :].reshape(1, 1, n)
    r1, r2, r7 = pltpu.roll(y3, 1, 1), pltpu.roll(y3, 2, 1), pltpu.roll(y3, SUBLANES - 1, 1)
    ym1 = jnp.where(sub >= 1, r1, _neighbour_tiles(r1, 1, is_lat))
    ym2 = jnp.where(sub >= 2, r2, _neighbour_tiles(r2, 1, is_lat))
    yp1 = jnp.where(sub <= SUBLANES - 2, r7, _neighbour_tiles(r7, -1, is_lat))
    out = cb.reshape(1, 1, n) + w(0) * ym2 + w(1) * ym1 + w(2) * y3 + w(3) * yp1
    return out.reshape(tm, n)


def _resident(shape):
    nd = len(shape)
    return pl.BlockSpec(shape, lambda *_: (0,) * nd, pipeline_mode=pl.Buffered(1))


def _big_row_tile(t):
    return max(tm for tm in range(8, BIG_ROW_TILE_MAX + 1, 8) if t % tm == 0)


def _mod_kernel(c_ref, w_ref, b_ref, o_ref):
    a = _silu(c_ref[...])
    o_ref[0] = jnp.dot(a, w_ref[0], preferred_element_type=F32, precision=lax.Precision.HIGHEST) + b_ref[0]


def _modulation(c_all, mod_w, mod_b):
    depth, d, n6 = mod_w.shape
    rows = c_all.shape[0]
    tn = 1024
    return pl.pallas_call(
        _mod_kernel,
        grid=(depth, n6 // tn),
        in_specs=[pl.BlockSpec((rows, d), lambda l, j: (0, 0)),
                  pl.BlockSpec((1, d, tn), lambda l, j: (l, 0, j)),
                  pl.BlockSpec((1, 1, tn), lambda l, j: (l, 0, j))],
        out_specs=pl.BlockSpec((1, rows, tn), lambda l, j: (l, 0, j)),
        out_shape=jax.ShapeDtypeStruct((depth, rows, n6), F32),
        compiler_params=_cparams("arbitrary", "arbitrary"),
        name="modulation",
    )(c_all, mod_w, mod_b.reshape(depth, 1, n6))


def _project(h_ref, w_ref, w_col, out_ref, epilogue):
    for c in range(0, out_ref.shape[1], PROJ_COLS):
        y = _dot(h_ref[...], w_ref[:, w_col + c:w_col + c + PROJ_COLS])
        out_ref[:, c:c + PROJ_COLS] = epilogue(y, c).astype(out_ref.dtype)


def _in_ab_kernel(tm, ctx_len, xc_ref, xl_ref, modl_ref, modc_ref, g_ref, w_ref, mcw_ref, mcb_ref, lcw_ref,
                  lcb_ref, gb_ref, x_ref, q_ref, k_ref, v_ref, og_ref, lx_ref, glg_ref, gates_ref, gates_t_ref,
                  h_scr):
    r = _row_ids(pl.program_id(1), tm)
    is_ctx = r < ctx_len
    is_lat = pl.program_id(1) * tm >= ctx_len
    x_ref[...] = jnp.where(is_lat, xl_ref[...], xc_ref[...])
    h_scr[...] = _adaln(x_ref[...], g_ref[...], modl_ref, modc_ref, 0, is_ctx).astype(BF16)

    def conv(cw_ref, cb_ref):
        return lambda y, c: _dwconv(y, cw_ref[:, c:c + y.shape[1]], cb_ref[:, c:c + y.shape[1]], is_lat)

    qc, kc, lc = conv(mcw_ref, mcb_ref), conv(mcw_ref, mcb_ref), conv(lcw_ref, lcb_ref)
    _project(h_scr, w_ref, 0, q_ref, lambda y, c: _silu(qc(y, c)))
    _project(h_scr, w_ref, 512, k_ref, lambda y, c: _silu(kc(y, 512 + c)) * (M_DK ** -0.5))
    _project(h_scr, w_ref, 1024, v_ref, lambda y, c: y)
    _project(h_scr, w_ref, 2048, og_ref, lambda y, c: _sigmoid(y))
    _project(h_scr, w_ref, 3072, lx_ref, lc)
    _project(h_scr, w_ref, 4096, glg_ref, lambda y, c: jax.nn.gelu(y))
    gt = _dot(h_scr[...], w_ref[:, 5120:5248]) + gb_ref[...]
    lane = lax.broadcasted_iota(I32, gt.shape, 1)
    gates = jnp.where(lane >= 2 * M_HEADS, _log_sigmoid(gt), gt)
    gates_ref[...] = gates
    gates_t_ref[...] = gates.T


def _in_ab(xc2, xl2, modl, modc, g, w, mcw, mcb, lcw, lcb, gb, bsz, t, ctx_len):
    tm = ROW_TILE
    nt, nct = t // tm, ctx_len // tm
    n = bsz * t
    row = lambda c: pl.BlockSpec((tm, c), lambda b, i: (b * nt + i, 0))
    outs = [(D_MODEL, F32), (512, BF16), (512, BF16), (1024, BF16), (1024, BF16), (1024, BF16), (1024, BF16),
            (LANES, F32)]
    return pl.pallas_call(
        functools.partial(_in_ab_kernel, tm, ctx_len),
        grid=(bsz, nt),
        in_specs=[pl.BlockSpec((tm, D_MODEL), lambda b, i: (b * nct + jnp.minimum(i, nct - 1), 0)),
                  pl.BlockSpec((tm, D_MODEL), lambda b, i: (b * (nt - nct) + jnp.maximum(i - nct, 0), 0)),
                  pl.BlockSpec((1, 8, D_MODEL), lambda b, i: (b, 0, 0)),
                  _resident((8, D_MODEL)), _resident((1, D_MODEL)), _resident(w.shape),
                  _resident(mcw.shape), _resident(mcb.shape), _resident(lcw.shape), _resident(lcb.shape),
                  _resident(gb.shape)],
        out_specs=[row(c) for c, _ in outs] + [pl.BlockSpec((LANES, tm), lambda b, i: (0, b * nt + i))],
        out_shape=[jax.ShapeDtypeStruct((n, c), dt) for c, dt in outs] + [jax.ShapeDtypeStruct((LANES, n), F32)],
        scratch_shapes=[pltpu.VMEM((tm, D_MODEL), BF16)],
        compiler_params=_cparams("arbitrary", "arbitrary"),
        name="in_proj_ab",
    )(xc2, xl2, modl, modc, g, w, mcw, mcb, lcw, lcb, gb)


def _reverse_order(n_ctx_chunks, n_chunks):
    return lambda c: jnp.where(c < n_ctx_chunks, n_ctx_chunks - 1 - c, n_chunks - 1 - (c - n_ctx_chunks))


def _tri_mask(L, reverse):
    row = lax.broadcasted_iota(I32, (L, L), 0)
    col = lax.broadcasted_iota(I32, (L, L), 1)
    return (col >= row) if reverse else (col <= row)


def _bidir_scan(kern, L, row_ins, lane_ins, res_ins, out_cols, scratch, bsz, t, ctx_len, name):
    nc = t // L
    rev = _reverse_order(ctx_len // L, nc)
    fwd = lambda i: i
    n = bsz * t

    def row(c, order):
        return pl.BlockSpec((L, c), lambda b, i: (b * nc + order(i), 0))

    def lanes(r, order):
        return pl.BlockSpec((r, L), lambda b, i: (0, b * nc + order(i)))

    in_specs, args = [], []
    for order in (fwd, rev):
        in_specs += [row(a.shape[1], order) for a in row_ins] + [lanes(r, order) for _, r in lane_ins]
        args += list(row_ins) + [a for a, _ in lane_ins]
    in_specs += [_resident(a.shape) for a in res_ins]
    args += list(res_ins)
    return pl.pallas_call(
        kern,
        grid=(bsz, nc),
        in_specs=in_specs,
        out_specs=[row(c, fwd) for c in out_cols] + [row(c, rev) for c in out_cols],
        out_shape=[jax.ShapeDtypeStruct((n, c), BF16) for c in out_cols] * 2,
        scratch_shapes=scratch,
        compiler_params=_cparams("arbitrary", "arbitrary"),
        name=name,
    )(*args)


def _zero_at_start(*scratch):
    @pl.when(pl.program_id(1) == 0)
    def _():
        for s in scratch:
            s[...] = jnp.zeros_like(s)


def _mlstm_kernel(L, qf, kf, vf, gcf, grf, qb, kb, vb, gcb, grb, of, ob, c_scr, m_scr):
    _zero_at_start(c_scr, m_scr)
    ones = jnp.ones((L, LANES), BF16)
    wide = lambda c, k: jnp.concatenate([c] * k, axis=1)
    er = lax.broadcasted_iota(I32, (LANES, 2 * M_HEADS * LANES), 0)
    ej = jnp.right_shift(lax.broadcasted_iota(I32, (LANES, 2 * M_HEADS * LANES), 1), LANES.bit_length() - 1)
    chains = []
    for d, (q_ref, k_ref, v_ref, gc_ref, gr_ref, out_ref) in enumerate(((qf, kf, vf, gcf, grf, of),
                                                                         (qb, kb, vb, gcb, grb, ob))):
        reverse = d == 1
        mask = _tri_mask(L, reverse)
        mask01 = mask.astype(BF16)
        gc = gc_ref[...]
        gr = gr_ref[...]
        bcol = _sel_cols(mask01, gc)
        brow = _sel_rows(gr, mask01)
        glane = lax.broadcasted_iota(I32, (L, LANES), 1)
        src = jnp.where(ej < M_HEADS, M_HEADS * d + ej, M_HEADS + M_HEADS * d + ej)
        s1, s2, _ = _split3(jnp.where(glane < 2 * M_HEADS, gc, bcol))
        sel = (er == src).astype(BF16)
        rep = _dot(s1, sel) + _dot(s2, sel)
        last = 0 if reverse else L - 1
        for h in range(M_HEADS):
            fo, io = 2 * M_HEADS + M_HEADS * d + h, M_HEADS * d + h
            ic, bc = rep[:, h * LANES:(h + 1) * LANES], rep[:, (M_HEADS + h) * LANES:(M_HEADS + h + 1) * LANES]
            r1 = gr[io:io + 1, :] - brow[fo:fo + 1, :]
            m = m_scr[d, h:h + 1, :]
            g = bc + m
            mx = jnp.max(jnp.where(mask, r1, -jnp.inf), axis=1, keepdims=True)
            mt = jnp.maximum(g, bc + mx)
            p = jnp.exp(jnp.where(mask, wide(bc - mt, L // LANES) + r1, -jnp.inf))
            bl = bc[last:last + 1, :]
            a_s = bl - bc + ic
            m_new = jnp.maximum(bl + m, jnp.max(a_s, axis=0, keepdims=True))
            chains.append(dict(
                d=d, h=h, out_ref=out_ref, mt=mt, w_inter=jnp.exp(g - mt), p=p,
                m_new=m_new, decay=jnp.exp(bl + m - m_new), ws=jnp.exp(a_s - m_new),
                qh=q_ref[:, h * M_DK:(h + 1) * M_DK], kh=k_ref[:, h * M_DK:(h + 1) * M_DK],
                vx=jnp.concatenate([v_ref[:, h * M_DV:(h + 1) * M_DV], ones], axis=1)))
    for ch in chains:
        ch["sc"] = (_dot_nt(ch["qh"], ch["kh"]) * ch["p"]).astype(BF16)
    for ch in chains:
        cx = c_scr[ch["d"], ch["h"]]
        numx = wide(ch["w_inter"], 1 + M_DV // LANES) * _dot(ch["qh"], cx.astype(BF16)) + _dot(ch["sc"], ch["vx"])
        den = numx[:, M_DV:]
        rinv = 1.0 / jnp.maximum(jnp.abs(den), jnp.exp(-ch["mt"]))
        hh = numx[:, 0:M_DV] * wide(rinv, M_DV // LANES)
        ch["out_ref"][:, ch["h"] * M_DV:(ch["h"] + 1) * M_DV] = hh.astype(BF16)
    for ch in chains:
        d, h = ch["d"], ch["h"]
        kw = (ch["kh"].astype(F32) * ch["ws"]).astype(BF16)
        c_scr[d, h] = wide(ch["decay"], 1 + M_DV // LANES) * c_scr[d, h] + _dot_tn(kw, ch["vx"])
        m_scr[d, h:h + 1, :] = ch["m_new"]


def _mlstm(q, k, v, gcol, grow, bsz, t, ctx_len):
    L = M_CHUNK
    scratch = [pltpu.VMEM((2, M_HEADS, M_DK, M_DV + LANES), F32), pltpu.VMEM((2, 8, LANES), F32)]
    return _bidir_scan(functools.partial(_mlstm_kernel, L), L, [q, k, v, gcol], [(grow, 16)], [], [1024], scratch,
                       bsz, t, ctx_len, "mlstm_scan")


def _lru_kernel(L, uf, ub_, wa_ref, wx_ref, ba_ref, bx_ref, lam_ref, of, ob, a_scr, b_scr, h_scr):
    _zero_at_start(h_scr)
    R, C = L // SUBLANES, LRU_BLOCKS * LRU_BW
    i0, i1 = lax.broadcasted_iota(I32, (L, L), 0), lax.broadcasted_iota(I32, (L, L), 1)
    when = lambda p: jnp.bitwise_and(p, SUBLANES - 1) * R + jnp.right_shift(p, SUBLANES.bit_length() - 1)
    regroup = (i1 == when(i0)).astype(BF16)
    ungroup = (i0 == when(i1)).astype(BF16)
    sub = lax.broadcasted_iota(I32, (SUBLANES, 1), 0)
    dirs = ((uf, of), (ub_, ob))
    for d, (u_ref, _) in enumerate(dirs):
        u = _dot(regroup, u_ref[...])
        ub = u.astype(BF16)
        sp = -LRU_C * _softplus(-lam_ref[d])
        for n in range(LRU_BLOCKS):
            sl = slice(n * LRU_BW, (n + 1) * LRU_BW)
            rg = _sigmoid(_dot(ub[:, sl], wa_ref[d, n]) + ba_ref[d, :, sl])
            ig = _sigmoid(_dot(ub[:, sl], wx_ref[d, n]) + bx_ref[d, :, sl])
            log_a = rg * sp[:, sl]
            a = jnp.exp(log_a)
            a_scr[d, :, sl] = a
            b_scr[d, :, sl] = jnp.sqrt(-jnp.tanh(log_a) * (a * a + 1.0)) * (ig * u[:, sl])
    S = SUBLANES
    hs = [jnp.zeros((S, C), F32), jnp.zeros((S, C), F32)]
    ps = [jnp.ones((S, C), F32), jnp.ones((S, C), F32)]
    for j in range(R):
        for d in range(2):
            rows = slice(S * j, S * j + S) if d == 0 else slice(S * (R - 1 - j), S * (R - j))
            a = a_scr[d, rows, :]
            hs[d] = a * hs[d] + b_scr[d, rows, :]
            ps[d] = a * ps[d]
            b_scr[d, rows, :] = hs[d]
            a_scr[d, rows, :] = ps[d]
    for d, (_, out_ref) in enumerate(dirs):
        c = h_scr[d:d + 1, :]
        cin = jnp.zeros((S, C), F32)
        for s in (range(S) if d == 0 else reversed(range(S))):
            cin = jnp.where(sub == s, c, cin)
            c = ps[d][s:s + 1, :] * c + hs[d][s:s + 1, :]
        h_scr[d:d + 1, :] = c
        h = b_scr[d].reshape(R, S, C) + a_scr[d].reshape(R, S, C) * cin[None]
        out_ref[...] = _dot(ungroup, h.reshape(L, C).astype(BF16)).astype(BF16)


def _lru(u, wa, wx, ba, bx, lam, bsz, t, ctx_len):
    L = LRU_TILE
    scratch = [pltpu.VMEM((2, L, 1024), F32), pltpu.VMEM((2, L, 1024), F32), pltpu.VMEM((8, 1024), F32)]
    return _bidir_scan(functools.partial(_lru_kernel, L), L, [u], [], [wa, wx, ba, bx, lam], [1024], scratch,
                       bsz, t, ctx_len, "rglru_scan")


def _swiglu_rows(x, is_ctx, modl_ref, modc_ref, g2_ref, g3_ref, w1_ref, w2_ref):
    h = _adaln(x, g2_ref[...], modl_ref, modc_ref, 3, is_ctx).astype(BF16)
    ff = w2_ref.shape[0]
    acc = jnp.zeros(x.shape, F32)
    for j in range(ff // FF_STEP):
        lo = j * FF_STEP
        gj = _dot(h, w1_ref[:, lo:lo + FF_STEP])
        uj = _dot(h, w1_ref[:, ff + lo:ff + lo + FF_STEP])
        acc = acc + _dot((_silu(gj) * uj).astype(BF16), w2_ref[lo:lo + FF_STEP, :])
    return x + _mod(modl_ref, modc_ref, 5, is_ctx) * _rms(acc, g3_ref[...])


def _out_tail(tm, ctx_len, a1, a2, x_ref, modl_ref, modc_ref, g_ref, w_ref, xo_ref, ffn_refs=None):
    is_ctx = _row_ids(pl.program_id(1), tm) < ctx_len
    half = w_ref.shape[0] // 2
    y = _dot(a1.astype(BF16), w_ref[0:half, :]) + _dot(a2.astype(BF16), w_ref[half:, :])
    x = x_ref[...] + _mod(modl_ref, modc_ref, 2, is_ctx) * _rms(y, g_ref[...])
    if ffn_refs is not None:
        x = _swiglu_rows(x, is_ctx, modl_ref, modc_ref, *ffn_refs)
    xo_ref[...] = x


def _out_ab_kernel(tm, ctx_len, hf, hb, og, lf, lb, glg, nw, x_ref, modl_ref, modc_ref, g_ref, w_ref,
                   g2_ref, g3_ref, w1_ref, w2_ref, xo_ref):
    hm = _head_rms(hf[...].astype(F32) + hb[...].astype(F32), nw[...], M_HEADS) * og[...].astype(F32)
    hl = (lf[...].astype(F32) + lb[...].astype(F32)) * glg[...].astype(F32)
    _out_tail(tm, ctx_len, hm, hl, x_ref, modl_ref, modc_ref, g_ref, w_ref, xo_ref,
              (g2_ref, g3_ref, w1_ref, w2_ref))


def _out_cd_kernel(tm, ctx_len, gf, gb, grs, yf, yb, sx, zs, gnw, dsk, snw, x_ref, modl_ref, modc_ref, g_ref,
                   w_ref, xo_ref):
    og = _head_rms(gf[...].astype(F32) + gb[...].astype(F32), gnw[...], G_HEADS) * grs[...].astype(F32)
    ys = yf[...].astype(F32) + yb[...].astype(F32) + dsk[...] * sx[...].astype(F32)
    ys = _rms(ys * zs[...].astype(F32), snw[...])
    _out_tail(tm, ctx_len, og, ys, x_ref, modl_ref, modc_ref, g_ref, w_ref, xo_ref)


def _out_proj(kern, name, acts, vecs, x2, modl, modc, g, w, bsz, t, ctx_len, extra=()):
    tm = _big_row_tile(t)
    nt = t // tm
    row = lambda c: pl.BlockSpec((tm, c), lambda b, i: (b * nt + i, 0))
    return pl.pallas_call(
        functools.partial(kern, tm, ctx_len),
        grid=(bsz, nt),
        in_specs=[row(1024)] * len(acts) + [_resident((1, 1024))] * len(vecs)
        + [row(D_MODEL), pl.BlockSpec((1, 8, D_MODEL), lambda b, i: (b, 0, 0)),
           _resident((8, D_MODEL)), _resident((1, D_MODEL)), _resident(w.shape)]
        + [_resident(a.shape) for a in extra],
        out_specs=row(D_MODEL),
        out_shape=jax.ShapeDtypeStruct(x2.shape, F32),
        compiler_params=_cparams("arbitrary", "arbitrary"),
        name=name,
    )(*acts, *vecs, x2, modl, modc, g, w, *extra)


def _in_cd_kernel(tm, ctx_len, x_ref, modl_ref, modc_ref, g_ref, w_ref, cw_ref, cb_ref, dtb_ref, aneg_ref,
                  gq_ref, gk_ref, gv_ref, gr_ref, z_ref, sx_ref, sb_ref, sc_ref, ga_ref, dts_ref, dts_t_ref, h_scr):
    r = _row_ids(pl.program_id(1), tm)
    is_ctx = r < ctx_len
    h_scr[...] = _adaln(x_ref[...], g_ref[...], modl_ref, modc_ref, 0, is_ctx).astype(BF16)
    is_lat = pl.program_id(1) * tm >= ctx_len

    def conv_silu(col0):
        return lambda y, c: _silu(_dwconv(y, cw_ref[:, col0 + c:col0 + c + y.shape[1]],
                                          cb_ref[:, col0 + c:col0 + c + y.shape[1]], is_lat))

    _project(h_scr, w_ref, 0, gq_ref, lambda y, c: y * (G_DK ** -0.5))
    _project(h_scr, w_ref, 512, gk_ref, lambda y, c: y)
    _project(h_scr, w_ref, 1024, gv_ref, lambda y, c: y)
    _project(h_scr, w_ref, 2048, gr_ref, lambda y, c: _silu(y))
    _project(h_scr, w_ref, 3072, z_ref, lambda y, c: _silu(y))
    _project(h_scr, w_ref, 4096, sx_ref, conv_silu(0))
    _project(h_scr, w_ref, 5120, sb_ref, conv_silu(1024))
    _project(h_scr, w_ref, 5376, sc_ref, conv_silu(1280))
    ga_ref[...] = _dot(h_scr[...], w_ref[:, 5632:5760])
    dt = _softplus(_dot(h_scr[...], w_ref[:, 5760:5888]) + dtb_ref[...])
    lane = lax.broadcasted_iota(I32, dt.shape, 1)
    dts = jnp.where(lane < 2 * S_HEADS, dt, dt * aneg_ref[...])
    dts_ref[...] = dts
    dts_t_ref[...] = dts.T


def _in_cd(x2, modl, modc, g, w, cw, cb, dtb, aneg, bsz, t, ctx_len):
    tm = ROW_TILE
    nt = t // tm
    n = bsz * t
    row = lambda c: pl.BlockSpec((tm, c), lambda b, i: (b * nt + i, 0))
    outs = [(512, BF16), (512, BF16), (1024, BF16), (1024, BF16), (1024, BF16), (1024, BF16), (256, BF16),
            (256, BF16), (LANES, F32), (LANES, F32)]
    return pl.pallas_call(
        functools.partial(_in_cd_kernel, tm, ctx_len),
        grid=(bsz, nt),
        in_specs=[row(D_MODEL), pl.BlockSpec((1, 8, D_MODEL), lambda b, i: (b, 0, 0)), _resident((8, D_MODEL)),
                  _resident((1, D_MODEL)), _resident(w.shape), _resident(cw.shape), _resident(cb.shape),
                  _resident(dtb.shape), _resident(aneg.shape)],
        out_specs=[row(c) for c, _ in outs] + [pl.BlockSpec((LANES, tm), lambda b, i: (0, b * nt + i))],
        out_shape=[jax.ShapeDtypeStruct((n, c), dt) for c, dt in outs] + [jax.ShapeDtypeStruct((LANES, n), F32)],
        scratch_shapes=[pltpu.VMEM((tm, D_MODEL), BF16)],
        compiler_params=_cparams("arbitrary", "arbitrary"),
        name="in_proj_cd",
    )(x2, modl, modc, g, w, cw, cb, dtb, aneg)


def _gla_kernel(L, C, qf, kf, vf, gaf, qb, kb, vb, gab, awh_ref, awl_ref, ab_ref, of, ob, s_scr, att_scr):
    _zero_at_start(s_scr, att_scr)
    nb = L // C
    chains = []
    for d, (q_ref, k_ref, v_ref, ga_ref, out_ref) in enumerate(((qf, kf, vf, gaf, of), (qb, kb, vb, gab, ob))):
        reverse = d == 1
        mask = _tri_mask(L, reverse)
        g1, g2, _ = _split3(ga_ref[...])
        pre = _dot(g1, awh_ref[d]) + _dot(g2, awh_ref[d]) + _dot(g1, awl_ref[d])
        lg = _log_sigmoid(pre + ab_ref[d]) * (1.0 / G_TAU)
        ball = _sel_cols(mask.astype(BF16), lg)
        for h in range(G_HEADS):
            ks = slice(h * G_DK, (h + 1) * G_DK)
            b = ball[:, ks] * LOG2E
            qh = q_ref[:, ks].astype(F32)
            inter = _dot_nt((qh * jnp.exp2(b)).astype(BF16), s_scr[d, h].astype(BF16))
            chains.append(dict(d=d, h=h, reverse=reverse, mask=mask, b=b, qh=qh, kh=k_ref[:, ks].astype(F32),
                               vh=v_ref[:, h * G_DV:(h + 1) * G_DV], inter=inter, out_ref=out_ref, blocks=[]))
    T = min(G_GROUP, L)
    for i in range(nb):
        lo, hi = i * C, (i + 1) * C
        t0 = lo // T * T
        for c, ch in enumerate(chains):
            b = ch["b"]
            if ch["reverse"]:
                bref = b[hi:hi + 1, :] if i < nb - 1 else jnp.zeros((1, G_DK), F32)
            else:
                bref = b[lo - 1:lo, :] if i > 0 else jnp.zeros((1, G_DK), F32)
            qi = (ch["qh"][lo:hi, :] * jnp.exp2(b[lo:hi, :] - bref)).astype(BF16)
            ki = (ch["kh"][t0:t0 + T] * jnp.exp2(bref - b[t0:t0 + T])).astype(BF16)
            att_scr[c, lo:hi, t0:t0 + T] = _dot_nt(qi, ki)
    for g in range(L // T):
        t0, t1 = g * T, (g + 1) * T
        for c, ch in enumerate(chains):
            b = ch["b"]
            e0, e1 = (t1, L) if ch["reverse"] else (0, t0)
            if e1 > e0:
                bref = b[t1:t1 + 1, :] if ch["reverse"] else b[t0 - 1:t0, :]
                qg = (ch["qh"][t0:t1] * jnp.exp2(b[t0:t1] - bref)).astype(BF16)
                kg = (ch["kh"][e0:e1] * jnp.exp2(bref - b[e0:e1])).astype(BF16)
                att_scr[c, t0:t1, e0:e1] = _dot_nt(qg, kg)
    for c, ch in enumerate(chains):
        att = jnp.where(ch["mask"], att_scr[c], 0.0).astype(BF16)
        oh = ch["inter"] + _dot(att, ch["vh"])
        ch["out_ref"][:, ch["h"] * G_DV:(ch["h"] + 1) * G_DV] = oh.astype(BF16)
    for ch in chains:
        d, h, b = ch["d"], ch["h"], ch["b"]
        last = 0 if ch["reverse"] else L - 1
        bl = b[last:last + 1, :]
        kd = (ch["kh"] * jnp.exp2(bl - b)).astype(BF16)
        s_scr[d, h] = s_scr[d, h] * jnp.exp2(bl) + _dot_tn(ch["vh"], kd)


def _gla(q, k, v, ga, awh, awl, ab, bsz, t, ctx_len):
    L, C = G_CHUNK, G_SUB
    scratch = [pltpu.VMEM((2, G_HEADS, G_DV, G_DK), F32), pltpu.VMEM((2 * G_HEADS, L, L), F32)]
    return _bidir_scan(functools.partial(_gla_kernel, L, C), L, [q, k, v, ga], [], [awh, awl, ab], [1024], scratch,
                       bsz, t, ctx_len, "gla_scan")


def _ssd_kernel(L, xf, bf, cf, gcf, grf, xb, bb, cb_, gcb, grb, of, ob, s_scr):
    _zero_at_start(s_scr)
    er = lax.broadcasted_iota(I32, (LANES, S_HEADS * S_P), 0)
    ec = jnp.right_shift(lax.broadcasted_iota(I32, (LANES, S_HEADS * S_P), 1), S_P.bit_length() - 1)
    glane = lax.broadcasted_iota(I32, (L, LANES), 1)
    lane = lax.broadcasted_iota(I32, (L, 2 * S_P), 1)
    dirs, chains = [], []
    for d, (x_ref, b_ref, c_ref, gc_ref, gr_ref, out_ref) in enumerate(((xf, bf, cf, gcf, grf, of),
                                                                         (xb, bb, cb_, gcb, grb, ob))):
        reverse = d == 1
        mask = _tri_mask(L, reverse)
        mask01 = mask.astype(BF16)
        gc = gc_ref[...]
        gr = gr_ref[...]
        bcol = _sel_cols(mask01, gc)
        brow = _sel_rows(gr, mask01)
        last = 0 if reverse else L - 1
        dt_o, la_o = S_HEADS * d, 2 * S_HEADS + S_HEADS * d
        sel_la = (er == ec + la_o).astype(BF16)
        x = x_ref[...]
        bla = jnp.where((glane >= la_o) & (glane < la_o + S_HEADS), bcol, 0.0)
        dt_at_la = pltpu.roll(gc, 2 * S_HEADS, 1)
        bl = bla[last:last + 1, :]
        ebx = _dot(jnp.exp(bla).astype(BF16), sel_la)
        wx = _dot((jnp.exp(bl - bla) * dt_at_la).astype(BF16), sel_la)
        dirs.append(dict(d=d, mask=mask, gr=gr, x=x, ebx=ebx, last=last, dt_o=dt_o, la_o=la_o, out_ref=out_ref,
                         xw=(x.astype(F32) * wx).astype(BF16),
                         bcol2=bcol * LOG2E, brow2=brow * LOG2E))
        for g in range(S_GROUPS):
            ns = slice(g * S_N, (g + 1) * S_N)
            chains.append(dict(dr=dirs[-1], g=g, bg=b_ref[:, ns], cg=c_ref[:, ns],
                               gs=slice(g * S_HPG * S_P, (g + 1) * S_HPG * S_P)))
    for ch in chains:
        dr = ch["dr"]
        ch["cb"] = _dot_nt(ch["cg"], ch["bg"])
        ch["inter"] = _dot(ch["cg"], s_scr[dr["d"], ch["g"]].astype(BF16)) * dr["ebx"][:, ch["gs"]]
    for ch in chains:
        dr, g = ch["dr"], ch["g"]
        parts = []
        for pr in range(S_HPG // 2):
            e0 = g * S_HPG + 2 * pr
            xp = dr["x"][:, e0 * S_P:(e0 + 2) * S_P]
            res = []
            for e in (e0, e0 + 1):
                col = dr["la_o"] + e
                bc, br = dr["bcol2"][:, col:col + 1], dr["brow2"][col:col + 1, :]
                dtr = dr["gr"][dr["dt_o"] + e:dr["dt_o"] + e + 1, :]
                att = (ch["cb"] * dtr * jnp.exp2(jnp.where(dr["mask"], bc - br, -jnp.inf))).astype(BF16)
                res.append(_dot(att, xp))
            parts.append(jnp.where(lane < S_P, res[0], res[1]))
        dr["out_ref"][:, ch["gs"]] = (jnp.concatenate(parts, axis=1) + ch["inter"]).astype(BF16)
    for ch in chains:
        dr, g, gs = ch["dr"], ch["g"], ch["gs"]
        s_scr[dr["d"], g] = (s_scr[dr["d"], g] * dr["ebx"][dr["last"]:dr["last"] + 1, gs]
                             + _dot_tn(ch["bg"], dr["xw"][:, gs]))


def _ssd(x, bm, cm, gcol, grow, bsz, t, ctx_len):
    L = S_CHUNK
    scratch = [pltpu.VMEM((2, S_GROUPS, S_N, S_HPG * S_P), F32)]
    return _bidir_scan(functools.partial(_ssd_kernel, L), L, [x, bm, cm, gcol], [(grow, 4 * S_HEADS)], [], [1024],
                       scratch,
                       bsz, t, ctx_len, "ssd_scan")


def _router_kernel(x_ref, modl_ref, g_ref, rw_ref, rb_ref, h_ref, idx_ref, gate_ref, cnt_ref, base_scr):
    @pl.when((pl.program_id(0) == 0) & (pl.program_id(1) == 0))
    def _():
        base_scr[...] = jnp.zeros_like(base_scr)

    h = _rms(x_ref[...], g_ref[...]) * (1.0 + modl_ref[0, 4:5, :]) + modl_ref[0, 3:4, :]
    h_ref[...] = _pack_rows(h)
    h1, h2, _ = _split3(h)
    logits = _dot(h1, rw_ref[0]) + _dot(h2, rw_ref[0]) + _dot(h1, rw_ref[1]) + rb_ref[...]
    lane = lax.broadcasted_iota(I32, logits.shape, 1)
    logits = jnp.where(lane < N_EXPERTS, logits, -jnp.inf)
    lanef = lane.astype(F32)
    m1 = jnp.max(logits, axis=1, keepdims=True)
    i1 = jnp.min(jnp.where(logits == m1, lanef, float(LANES)), axis=1, keepdims=True)
    rest = jnp.where(lanef == i1, -jnp.inf, logits)
    m2 = jnp.max(rest, axis=1, keepdims=True)
    i2 = jnp.min(jnp.where(rest == m2, lanef, float(LANES)), axis=1, keepdims=True)
    e = jnp.exp(m2 - m1)
    g1 = 1.0 / (1.0 + e)
    gate_ref[...] = jnp.where(lane == 0, g1, jnp.where(lane == 1, e * g1, 0.0))
    tm = logits.shape[0]
    oh1, oh2 = (lanef == i1).astype(F32), (lanef == i2).astype(F32)
    earlier = (lax.broadcasted_iota(I32, (tm, tm), 1) < lax.broadcasted_iota(I32, (tm, tm), 0)).astype(BF16)
    seen = base_scr[0:1, :] + _dot(earlier, (oh1 + oh2).astype(BF16))
    r1 = jnp.sum(oh1 * seen, axis=1, keepdims=True)
    r2 = jnp.sum(oh2 * seen, axis=1, keepdims=True)
    base_scr[0:1, :] = base_scr[0:1, :] + jnp.sum(oh1 + oh2, axis=0, keepdims=True)
    cnt_ref[...] = base_scr[...]
    idx_ref[...] = jnp.where(lane == 0, i1, jnp.where(lane == 1, i2, jnp.where(lane == 2, r1, jnp.where(
        lane == 3, r2, 0.0)))).astype(I32)


def _router(x2, modl, g, rw, rb, bsz, t, ctx_len):
    tm = ROW_TILE
    nt, nct = t // tm, ctx_len // tm
    nl = bsz * (t - ctx_len)
    nlt = nt - nct
    orow = lambda c: pl.BlockSpec((tm, c), lambda b, i: (b * nlt + i, 0))
    return pl.pallas_call(
        _router_kernel,
        grid=(bsz, nlt),
        in_specs=[pl.BlockSpec((tm, D_MODEL), lambda b, i: (b * nt + nct + i, 0)),
                  pl.BlockSpec((1, 8, D_MODEL), lambda b, i: (b, 0, 0)),
                  _resident((1, D_MODEL)), _resident(rw.shape), _resident(rb.shape)],
        out_specs=[orow(D_MODEL // 2), orow(LANES), orow(LANES), pl.BlockSpec((8, LANES), lambda b, i: (0, 0))],
        out_shape=[jax.ShapeDtypeStruct((nl, D_MODEL // 2), jnp.uint32), jax.ShapeDtypeStruct((nl, LANES), I32),
                   jax.ShapeDtypeStruct((nl, LANES), F32), jax.ShapeDtypeStruct((8, LANES), F32)],
        scratch_shapes=[pltpu.VMEM((8, LANES), F32)],
        compiler_params=_cparams("arbitrary", "arbitrary"),
        name="moe_router",
    )(x2, modl, g, rw, rb)


def _gather_rows(table, idx):
    n_rows, d = idx.shape[0], table.shape[1]
    n_workers = SC_CORES * SC_SUBCORES
    per_worker = n_rows // n_workers
    chunk = SC_GATHER_ROWS
    assert per_worker * n_workers == n_rows and per_worker % chunk == 0
    mesh = plsc.VectorSubcoreMesh(core_axis_name="c", subcore_axis_name="s")

    @functools.partial(
        pl.kernel, mesh=mesh, out_type=jax.ShapeDtypeStruct((n_rows, d), table.dtype),
        scratch_types=[pltpu.VMEM((chunk,), I32), pltpu.VMEM((chunk, d), table.dtype), pltpu.SemaphoreType.DMA],
        name="sc_gather_rows")
    def gather(table_hbm, idx_hbm, out_hbm, idx_v, rows_v, sem):
        base = (lax.axis_index("s") * SC_CORES + lax.axis_index("c")) * per_worker

        @pl.loop(0, per_worker // chunk)
        def _(j):
            off = pl.multiple_of(base + j * chunk, 8)
            pltpu.sync_copy(idx_hbm.at[pl.ds(off, chunk)], idx_v)
            pltpu.async_copy(table_hbm.at[idx_v], rows_v, sem).wait()
            pltpu.sync_copy(rows_v, out_hbm.at[pl.ds(off, chunk)])

    return gather(table, idx)


def _dispatch_rows(table, slot, n_out):
    assert TOP_K == 2
    n_assign, (n_tok, d) = slot.shape[0], table.shape
    n_workers = SC_CORES * SC_SUBCORES
    per_worker = n_out // n_workers
    chunk, vec = SC_GATHER_ROWS, SC_LANES
    assert per_worker * n_workers == n_out and per_worker % chunk == 0 and n_assign % vec == 0
    assert n_out < 3 * n_tok
    mesh = plsc.VectorSubcoreMesh(core_axis_name="c", subcore_axis_name="s")

    @functools.partial(
        pl.kernel, mesh=mesh, out_type=jax.ShapeDtypeStruct((n_out, d), table.dtype),
        scratch_types=[pltpu.VMEM((n_assign,), I32), pltpu.VMEM((per_worker,), I32),
                       pltpu.VMEM((chunk, d), table.dtype), pltpu.SemaphoreType.DMA],
        compiler_params=pltpu.CompilerParams(needs_layout_passes=False),
        name="sc_dispatch_rows")
    def dispatch(table_hbm, slot_hbm, out_hbm, slots_v, tok_v, rows_v, sem):
        base = (lax.axis_index("s") * SC_CORES + lax.axis_index("c")) * per_worker
        pltpu.sync_copy(slot_hbm, slots_v)
        lane = lax.broadcasted_iota(I32, (vec,), 0)

        @pl.loop(0, per_worker // vec)
        def _(j):
            p = base + j * vec + lane
            p = jnp.where(p >= n_tok, p - n_tok, p)
            tok_v[pl.ds(j * vec, vec)] = jnp.where(p >= n_tok, p - n_tok, p)

        @pl.loop(0, n_assign // vec)
        def _(j):
            s = slots_v[pl.ds(j * vec, vec)] - base
            mine = (s >= 0) & (s < per_worker)
            tok = lax.shift_right_logical(j * vec + lane, 1)
            plsc.store_scatter(tok_v, [jnp.where(mine, s, 0)], tok, mask=mine)

        @pl.loop(0, per_worker // chunk)
        def _(j):
            off = pl.multiple_of(j * chunk, 8)
            pltpu.async_copy(table_hbm.at[tok_v.at[pl.ds(off, chunk)]], rows_v, sem).wait()
            pltpu.sync_copy(rows_v, out_hbm.at[pl.ds(base + off, chunk)])

    return dispatch(table, slot)


def _moe_kernel(be_ref, nu_ref, nv_ref, fi_ref, x_ref, w1g_ref, w1u_ref, w2_ref, y_ref, xb_ref, acc_ref,
                wg_scr, wu_scr, w2_scr):
    i, j = pl.program_id(0), pl.program_id(1)
    bm = x_ref.shape[0]

    def swiglu_rows(rows):
        x = xb_ref[0:rows, :]
        a = (_silu(_dot(x, wg_scr[j])) * _dot(x, wu_scr[j])).astype(BF16)
        acc_ref[0:rows, :] += _dot(a, w2_scr[j])

    @pl.when(i < nu_ref[0])
    def _():
        @pl.when(fi_ref[i] == 1)
        def _():
            wg_scr[j] = w1g_ref[0].astype(BF16)
            wu_scr[j] = w1u_ref[0].astype(BF16)
            w2_scr[j] = w2_ref[0].astype(BF16)

        @pl.when(j == 0)
        def _():
            acc_ref[...] = jnp.zeros_like(acc_ref)
            xb_ref[...] = _unpack_rows(x_ref[...]).astype(BF16)

        @pl.when(nv_ref[i] > bm // 2)
        def _():
            swiglu_rows(bm)

        @pl.when(nv_ref[i] <= bm // 2)
        def _():
            swiglu_rows(bm // 2)

        @pl.when(j == pl.num_programs(1) - 1)
        def _():
            y_ref[...] = _pack_rows(acc_ref[...])


def _moe_experts(xs, w1, w2, block_e, n_used, n_valid, first, n_blocks):
    bm, fc = MOE_ROWS, MOE_FF
    nff = FF_EXPERT // fc
    used = lambda i, nu: jnp.minimum(i, nu[0] - 1)
    ffi = lambda i, j, nu, fi: jnp.where((i < nu[0]) & (fi[used(i, nu)] == 1), j, nff - 1)
    grid_spec = pltpu.PrefetchScalarGridSpec(
        num_scalar_prefetch=4,
        grid=(n_blocks, nff),
        in_specs=[pl.BlockSpec((bm, D_MODEL // 2), lambda i, j, be, nu, nv, fi: (used(i, nu), 0)),
                  pl.BlockSpec((1, D_MODEL, fc),
                               lambda i, j, be, nu, nv, fi: (be[used(i, nu)], 0, ffi(i, j, nu, fi))),
                  pl.BlockSpec((1, D_MODEL, fc),
                               lambda i, j, be, nu, nv, fi: (be[used(i, nu)], 0, nff + ffi(i, j, nu, fi))),
                  pl.BlockSpec((1, fc, D_MODEL),
                               lambda i, j, be, nu, nv, fi: (be[used(i, nu)], ffi(i, j, nu, fi), 0))],
        out_specs=pl.BlockSpec((bm, D_MODEL // 2), lambda i, j, be, nu, nv, fi: (used(i, nu), 0)),
        scratch_shapes=[pltpu.VMEM((bm, D_MODEL), BF16), pltpu.VMEM((bm, D_MODEL), F32),
                        pltpu.VMEM((nff, D_MODEL, fc), BF16), pltpu.VMEM((nff, D_MODEL, fc), BF16),
                        pltpu.VMEM((nff, fc, D_MODEL), BF16)],
    )
    return pl.pallas_call(
        _moe_kernel,
        grid_spec=grid_spec,
        out_shape=jax.ShapeDtypeStruct((n_blocks * bm, D_MODEL // 2), jnp.uint32),
        compiler_params=_cparams("arbitrary", "arbitrary"),
        name="moe_experts",
    )(block_e, n_used, n_valid, first, xs, w1, w1, w2)


def _combine_kernel(y0_ref, y1_ref, gate_ref, x_ref, modl_ref, g_ref, o_ref):
    gt = gate_ref[...]
    y = gt[:, 0:1] * _unpack_rows(y0_ref[...]) + gt[:, 1:2] * _unpack_rows(y1_ref[...])
    o_ref[...] = x_ref[...] + modl_ref[0, 5:6, :] * _rms(y, g_ref[...])


def _combine(yk, gates, x2, modl, g, bsz, t, ctx_len):
    tm = ROW_TILE
    nt, nct = t // tm, ctx_len // tm
    nlt = nt - nct
    nl = bsz * (t - ctx_len)
    orow = lambda c: pl.BlockSpec((tm, c), lambda b, i: (b * nlt + i, 0))
    second = pl.BlockSpec((tm, D_MODEL // 2), lambda b, i: (nl // tm + b * nlt + i, 0))
    return pl.pallas_call(
        _combine_kernel,
        grid=(bsz, nlt),
        in_specs=[orow(D_MODEL // 2), second, orow(LANES),
                  pl.BlockSpec((tm, D_MODEL), lambda b, i: (b * nt + nct + i, 0)),
                  pl.BlockSpec((1, 8, D_MODEL), lambda b, i: (b, 0, 0)), _resident((1, D_MODEL))],
        out_specs=orow(D_MODEL),
        out_shape=jax.ShapeDtypeStruct((nl, D_MODEL), F32),
        compiler_params=_cparams("arbitrary", "arbitrary"),
        name="moe_combine",
    )(yk, yk, gates, x2, modl, g)


def _routing_tables(experts, ranks, counts):
    bm = MOE_ROWS
    n_assign = experts.size
    padded = (counts + bm - 1) // bm * bm
    ends_p = jnp.cumsum(padded)
    pstarts = ends_p - padded
    onehot = experts[..., None] == jnp.arange(N_EXPERTS, dtype=I32)
    slot = (jnp.sum(jnp.where(onehot, pstarts, 0), axis=-1) + ranks).astype(I32)
    n_blocks = n_assign // bm + N_EXPERTS
    block_e = jnp.minimum(jnp.sum(ends_p[None, :] <= (jnp.arange(n_blocks, dtype=I32) * bm)[:, None], axis=1),
                          N_EXPERTS - 1).astype(I32)
    n_used = (ends_p[-1] // bm).astype(I32).reshape(1)
    sel = block_e[:, None] == jnp.arange(N_EXPERTS, dtype=I32)
    rows_left = jnp.sum(jnp.where(sel, counts + pstarts, 0), axis=1) - jnp.arange(n_blocks, dtype=I32) * bm
    n_valid = jnp.clip(rows_left, 0, bm).astype(I32)
    first = jnp.concatenate([jnp.ones((1,), I32), (block_e[1:] != block_e[:-1]).astype(I32)])
    return slot, block_e, n_used, n_valid, first, n_blocks


def kernel(x, c, ctx, c_ctx, mod_w, mod_b, norm_g, ab_in_w, m_conv_w, m_conv_b, m_gate_b, m_norm_w, l_conv_w,
           l_conv_b, l_wa, l_ba, l_wx, l_bx, l_lam, ab_out_w, ffn_w1, ffn_w2, cd_in_w, g_alpha_w, g_alpha_b,
           g_norm_w, s_conv_w, s_conv_b, s_dt_bias, s_A_log, s_D, s_norm_w, cd_out_w, router_w, router_b,
           moe_w1, moe_w2):
    bsz, seq, dm = x.shape
    ctx_len = ctx.shape[1]
    t = ctx_len + seq
    n = bsz * t
    assert ctx_len == ROW_TILE and seq % ROW_TILE == 0 and ROW_TILE % GRID_W == 0

    c_all = jnp.concatenate([c, c_ctx[None, :], jnp.zeros((-(bsz + 1) % 8, dm), F32)], axis=0)
    mods = _modulation(c_all, mod_w, mod_b).reshape(mod_w.shape[0], c_all.shape[0], 6, dm)
    pad2 = jnp.zeros((bsz, 2, dm), F32)

    def layer_mods(layer):
        modl = jnp.concatenate([mods[layer, :bsz], pad2], axis=1)
        modc = jnp.concatenate([mods[layer, bsz], pad2[0]], axis=0)
        return modl, modc

    modl, modc = layer_mods(0)
    g = norm_g[0]
    w = ab_in_w[0]
    o_qk, o_v, o_o, o_gt, o_lx, o_lg = 0, 1024, 2048, 3072, 3088, 4112
    w_in = jnp.concatenate([w[:, o_qk:o_v], w[:, o_v:o_o], w[:, o_o:o_gt], w[:, o_lx:o_lg], w[:, o_lg:],
                            w[:, o_gt:o_lx], jnp.zeros((dm, LANES - 16), F32)], axis=1).astype(BF16)
    gb = jnp.concatenate([m_gate_b[0].reshape(1, 16), jnp.zeros((1, LANES - 16), F32)], axis=1)
    x2, q, k, v, og, lx, glg, gates, gates_t = _in_ab(
        ctx.reshape(bsz * ctx_len, dm), x.reshape(bsz * seq, dm), modl, modc, g[0:1], w_in, m_conv_w[0],
        m_conv_b[0][None], l_conv_w[0], l_conv_b[0][None], gb, bsz, t, ctx_len)
    hm_f, hm_b = _mlstm(q, k, v, gates, gates_t, bsz, t, ctx_len)
    hl_f, hl_b = _lru(lx, l_wa[0].astype(BF16), l_wx[0].astype(BF16), l_ba[0][:, None], l_bx[0][:, None],
                      l_lam[0][:, None], bsz, t, ctx_len)
    x2 = _out_proj(_out_ab_kernel, "out_proj_swiglu_ab", [hm_f, hm_b, og, hl_f, hl_b, glg], [m_norm_w[0][None]],
                   x2, modl, modc, g[1:2], ab_out_w[0].astype(BF16), bsz, t, ctx_len,
                   extra=(g[2:3], g[3:4], ffn_w1[0].astype(BF16), ffn_w2[0].astype(BF16)))

    modl, modc = layer_mods(1)
    g = norm_g[1]
    w = cd_in_w[0]
    o_gv, o_gr, o_ga, o_z, o_xbc, o_dt = 1024, 2048, 3072, 3104, 4128, 5664
    zpad = jnp.zeros((dm, LANES - 32), F32)
    w_in = jnp.concatenate([w[:, :o_gv], w[:, o_gv:o_gr], w[:, o_gr:o_ga], w[:, o_z:o_xbc], w[:, o_xbc:o_dt],
                            w[:, o_ga:o_z], zpad, w[:, o_dt:], w[:, o_dt:], jnp.zeros((dm, LANES - 64), F32)],
                           axis=1).astype(BF16)
    dtb = jnp.concatenate([s_dt_bias[0].reshape(1, 32), s_dt_bias[0].reshape(1, 32),
                           jnp.zeros((1, LANES - 64), F32)], axis=1)
    aneg = jnp.concatenate([jnp.zeros((1, 32), F32), -jnp.exp(s_A_log[0].reshape(1, 32)),
                            jnp.zeros((1, LANES - 64), F32)], axis=1)
    gq, gk, gv, grs, zs, sx, sb, sc, ga, dts, dts_t = _in_cd(x2, modl, modc, g[0:1], w_in, s_conv_w[0],
                                                             s_conv_b[0][None], dtb, aneg, bsz, t, ctx_len)
    aw = jnp.zeros((2, LANES, G_HEADS * G_DK), F32)
    aw = aw.at[0, 0:G_RANK].set(g_alpha_w[0, 0]).at[1, G_RANK:2 * G_RANK].set(g_alpha_w[0, 1])
    awh = aw.astype(BF16)
    awl = (aw - awh.astype(F32)).astype(BF16)
    og_f, og_b = _gla(gq, gk, gv, ga, awh, awl, g_alpha_b[0][:, None], bsz, t, ctx_len)
    ys_f, ys_b = _ssd(sx, sb, sc, dts, dts_t, bsz, t, ctx_len)
    dskip = jnp.repeat(s_D[0], S_P)[None]
    x2 = _out_proj(_out_cd_kernel, "out_proj_cd", [og_f, og_b, grs, ys_f, ys_b, sx, zs],
                   [g_norm_w[0][None], dskip, s_norm_w[0][None]],
                   x2, modl, modc, g[1:2], cd_out_w[0].astype(BF16), bsz, t, ctx_len)

    rw = jnp.concatenate([router_w[0], jnp.zeros((dm, LANES - N_EXPERTS), F32)], axis=1)
    rwh = rw.astype(BF16)
    rw = jnp.stack([rwh, (rw - rwh.astype(F32)).astype(BF16)])
    rb = jnp.concatenate([router_b[0][None], jnp.zeros((1, LANES - N_EXPERTS), F32)], axis=1)
    h2, top_idx, top_gate, counts = _router(x2, modl, g[2:3], rw, rb, bsz, t, ctx_len)
    slot, block_e, n_used, n_valid, first, n_blocks = _routing_tables(
        top_idx[:, 0:TOP_K], top_idx[:, TOP_K:2 * TOP_K], counts[0, :N_EXPERTS].astype(I32))
    xs = _dispatch_rows(h2, slot.reshape(-1), n_blocks * MOE_ROWS)
    ys_e = _moe_experts(xs, moe_w1[0], moe_w2[0], block_e, n_used, n_valid, first, n_blocks)
    yk = _gather_rows(ys_e, jnp.concatenate([slot[:, 0], slot[:, 1]]))
    out = _combine(yk, top_gate, x2, modl, g[3:4], bsz, t, ctx_len)
    return out.reshape(bsz, seq, dm)
```

```python
import functools

import jax
import jax.numpy as jnp
from jax import lax
from jax.experimental import pallas as pl
from jax.experimental.pallas import tpu as pltpu
from jax.experimental.pallas import tpu_sc as plsc

F32, BF16, I32 = jnp.float32, jnp.bfloat16, jnp.int32

D_MODEL = 1024
GRID_W = 64
EPS = 1e-6
LOG2E = 1.4426950408889634
M_HEADS, M_DK, M_DV = 4, 128, 256
LRU_BLOCKS, LRU_BW, LRU_C = 8, 128, 8.0
G_HEADS, G_DK, G_DV, G_RANK, G_TAU = 4, 128, 256, 16, 16.0
S_HEADS, S_P, S_N, S_GROUPS, S_HPG = 16, 64, 128, 2, 8
FF_DENSE = 2816
N_EXPERTS, TOP_K, FF_EXPERT = 8, 2, 3584

LANES, SUBLANES = 128, 8
VMEM_LIMIT = 56 * 1024 * 1024
ROW_TILE = 256
BIG_ROW_TILE_MAX = 640
PROJ_COLS = 256
M_CHUNK = 256
G_CHUNK, G_SUB = 256, 16
G_GROUP = 64
S_CHUNK = 256
LRU_TILE = 256
FF_STEP = 256
MOE_ROWS = 1024
MOE_FF = 512
SC_CORES, SC_SUBCORES = 2, 16
SC_LANES = 16
SC_GATHER_ROWS = 128


def _cparams(*sem):
    return pltpu.CompilerParams(dimension_semantics=sem, vmem_limit_bytes=VMEM_LIMIT)


def _sigmoid(x):
    return 0.5 + 0.5 * jnp.tanh(0.5 * x)


def _silu(x):
    return x * _sigmoid(x)


def _softplus(x):
    return jnp.maximum(x, 0.0) + jnp.log1p(jnp.exp(-jnp.abs(x)))


def _log_sigmoid(x):
    return jnp.minimum(x, 0.0) - jnp.log1p(jnp.exp(-jnp.abs(x)))


def _dot(a, b):
    return jnp.dot(a, b, preferred_element_type=F32)


def _dot_nt(a, b):
    return lax.dot_general(a, b, (((1,), (1,)), ((), ())), preferred_element_type=F32)


def _dot_tn(a, b):
    return lax.dot_general(a, b, (((0,), (0,)), ((), ())), preferred_element_type=F32)


def _split3(f):
    f1 = f.astype(BF16)
    r = f - f1.astype(F32)
    f2 = r.astype(BF16)
    f3 = (r - f2.astype(F32)).astype(BF16)
    return f1, f2, f3


def _sel_cols(mask01, f):
    p1, p2, p3 = _split3(f)
    return _dot(mask01, p1) + _dot(mask01, p2) + _dot(mask01, p3)


def _sel_rows(f, mask01):
    p1, p2, p3 = _split3(f)
    return _dot_nt(p1, mask01) + _dot_nt(p2, mask01) + _dot_nt(p3, mask01)


def _pack_rows(u):
    n = u.shape[1] // 2
    bits = lax.bitcast_convert_type(u.astype(BF16).astype(F32), jnp.uint32)
    return jnp.bitwise_or(bits[:, :n], jnp.right_shift(bits[:, n:], jnp.uint32(16)))


def _unpack_rows(p):
    hi = lax.bitcast_convert_type(jnp.bitwise_and(p, jnp.uint32(0xFFFF0000)), F32)
    lo = lax.bitcast_convert_type(jnp.left_shift(p, jnp.uint32(16)), F32)
    return jnp.concatenate([hi, lo], axis=1)


def _rms(u, g):
    return u * lax.rsqrt(jnp.mean(u * u, axis=-1, keepdims=True) + EPS) * g


def _head_rms(u, g, n_heads):
    w = u.shape[-1] // n_heads
    return jnp.concatenate([_rms(u[:, h * w:(h + 1) * w], g[:, h * w:(h + 1) * w]) for h in range(n_heads)], axis=1)


def _row_ids(tile_idx, tm):
    return tile_idx * tm + lax.broadcasted_iota(I32, (tm, 1), 0)


def _mod(modl_ref, modc_ref, idx, is_ctx):
    return jnp.where(is_ctx, modc_ref[idx:idx + 1, :], modl_ref[0, idx:idx + 1, :])


def _adaln(x, g, modl_ref, modc_ref, shift_idx, is_ctx):
    shift = _mod(modl_ref, modc_ref, shift_idx, is_ctx)
    scale = _mod(modl_ref, modc_ref, shift_idx + 1, is_ctx)
    return _rms(x, g) * (1.0 + scale) + shift


def _neighbour_tiles(r, step, is_lat):
    n_tiles, per_seg = r.shape[0], GRID_W // SUBLANES
    zero = jnp.zeros((1,) + r.shape[1:], r.dtype)

    def across(src):
        return jnp.where(is_lat, 0.0, r[src:src + 1]) if 0 <= src < n_tiles else zero

    pieces = []
    for s0 in range(0, n_tiles, per_seg):
        if step > 0:
            pieces += [across(s0 - 1), r[s0:s0 + per_seg - 1]]
        else:
            pieces += [r[s0 + 1:s0 + per_seg], across(s0 + per_seg)]
    return jnp.concatenate(pieces, axis=0)


def _dwconv(y, cw, cb, is_lat):
    tm, n = y.shape
    y3 = y.reshape(tm // SUBLANES, SUBLANES, n)
    sub = lax.broadcasted_iota(I32, (1, SUBLANES, 1), 1)
    w = lambda j: cw[j:j + 1, :].reshape(1, 1, n)
    r1, r2, r7 = pltpu.roll(y3, 1, 1), pltpu.roll(y3, 2, 1), pltpu.roll(y3, SUBLANES - 1, 1)
    ym1 = jnp.where(sub >= 1, r1, _neighbour_tiles(r1, 1, is_lat))
    ym2 = jnp.where(sub >= 2, r2, _neighbour_tiles(r2, 1, is_lat))
    yp1 = jnp.where(sub <= SUBLANES - 2, r7, _neighbour_tiles(r7, -1, is_lat))
    out = cb.reshape(1, 1, n) + w(0) * ym2 + w(1) * ym1 + w(2) * y3 + w(3) * yp1
    return out.reshape(tm, n)


def _resident(shape):
    nd = len(shape)
    return pl.BlockSpec(shape, lambda *_: (0,) * nd, pipeline_mode=pl.Buffered(1))


def _big_row_tile(t):
    return max(tm for tm in range(8, BIG_ROW_TILE_MAX + 1, 8) if t % tm == 0)


def _mod_kernel(c_ref, w_ref, b_ref, o_ref):
    a1, a2, _ = _split3(_silu(c_ref[...]))
    w = w_ref[0]
    wh = w.astype(BF16)
    wl = (w - wh.astype(F32)).astype(BF16)
    o_ref[0] = _dot(a1, wh) + _dot(a2, wh) + _dot(a1, wl) + b_ref[0]


def _modulation(c_all, mod_w, mod_b):
    depth, d, n6 = mod_w.shape
    rows = c_all.shape[0]
    tn = 1024
    return pl.pallas_call(
        _mod_kernel,
        grid=(depth, n6 // tn),
        in_specs=[pl.BlockSpec((rows, d), lambda l, j: (0, 0)),
                  pl.BlockSpec((1, d, tn), lambda l, j: (l, 0, j)),
                  pl.BlockSpec((1, 1, tn), lambda l, j: (l, 0, j))],
        out_specs=pl.BlockSpec((1, rows, tn), lambda l, j: (l, 0, j)),
        out_shape=jax.ShapeDtypeStruct((depth, rows, n6), F32),
        compiler_params=_cparams("arbitrary", "arbitrary"),
        name="modulation",
    )(c_all, mod_w, mod_b.reshape(depth, 1, n6))


def _project(h_ref, w_ref, w_col, out_ref, epilogue):
    for c in range(0, out_ref.shape[1], PROJ_COLS):
        y = _dot(h_ref[...], w_ref[:, w_col + c:w_col + c + PROJ_COLS])
        out_ref[:, c:c + PROJ_COLS] = epilogue(y, c).astype(out_ref.dtype)


def _in_ab_kernel(tm, ctx_len, xc_ref, xl_ref, modl_ref, modc_ref, g_ref, w_ref, mcw_ref, mcb_ref, lcw_ref,
                  lcb_ref, gb_ref, x_ref, q_ref, k_ref, v_ref, og_ref, lx_ref, glg_ref, gates_ref, gates_t_ref,
                  h_scr):
    is_lat = pl.program_id(1) * tm >= ctx_len
    x_ref[...] = jnp.where(is_lat, xl_ref[...], xc_ref[...])
    h_scr[...] = _adaln(x_ref[...], g_ref[...], modl_ref, modc_ref, 0, jnp.logical_not(is_lat)).astype(BF16)

    def conv(cw_ref, cb_ref):
        return lambda y, c: _dwconv(y, cw_ref[:, c:c + PROJ_COLS], cb_ref[:, c:c + PROJ_COLS], is_lat)

    qc, kc, lc = conv(mcw_ref, mcb_ref), conv(mcw_ref, mcb_ref), conv(lcw_ref, lcb_ref)
    _project(h_scr, w_ref, 0, q_ref, lambda y, c: _silu(qc(y, c)))
    _project(h_scr, w_ref, 512, k_ref, lambda y, c: _silu(kc(y, 512 + c)) * (M_DK ** -0.5))
    _project(h_scr, w_ref, 1024, v_ref, lambda y, c: y)
    _project(h_scr, w_ref, 2048, og_ref, lambda y, c: _sigmoid(y))
    _project(h_scr, w_ref, 3072, lx_ref, lc)
    _project(h_scr, w_ref, 4096, glg_ref, lambda y, c: jax.nn.gelu(y))
    gt = _dot(h_scr[...], w_ref[:, 5120:5248]) + gb_ref[...]
    lane = lax.broadcasted_iota(I32, gt.shape, 1)
    gates = jnp.where(lane >= 2 * M_HEADS, _log_sigmoid(gt), gt)
    gates_ref[...] = gates
    gates_t_ref[...] = gates.T


def _in_ab(xc2, xl2, modl, modc, g, w, mcw, mcb, lcw, lcb, gb, bsz, t, ctx_len):
    tm = ROW_TILE
    nt, nct = t // tm, ctx_len // tm
    n = bsz * t
    row = lambda c: pl.BlockSpec((tm, c), lambda b, i: (b * nt + i, 0))
    outs = [(D_MODEL, F32), (512, BF16), (512, BF16), (1024, BF16), (1024, BF16), (1024, BF16), (1024, BF16),
            (LANES, F32)]
    return pl.pallas_call(
        functools.partial(_in_ab_kernel, tm, ctx_len),
        grid=(bsz, nt),
        in_specs=[pl.BlockSpec((tm, D_MODEL), lambda b, i: (b * nct + jnp.minimum(i, nct - 1), 0)),
                  pl.BlockSpec((tm, D_MODEL), lambda b, i: (b * (nt - nct) + jnp.maximum(i - nct, 0), 0)),
                  pl.BlockSpec((1, 8, D_MODEL), lambda b, i: (b, 0, 0)),
                  _resident((8, D_MODEL)), _resident((1, D_MODEL)), _resident(w.shape),
                  _resident(mcw.shape), _resident(mcb.shape), _resident(lcw.shape), _resident(lcb.shape),
                  _resident(gb.shape)],
        out_specs=[row(c) for c, _ in outs] + [pl.BlockSpec((LANES, tm), lambda b, i: (0, b * nt + i))],
        out_shape=[jax.ShapeDtypeStruct((n, c), dt) for c, dt in outs] + [jax.ShapeDtypeStruct((LANES, n), F32)],
        scratch_shapes=[pltpu.VMEM((tm, D_MODEL), BF16)],
        compiler_params=_cparams("arbitrary", "arbitrary"),
        name="in_proj_ab",
    )(xc2, xl2, modl, modc, g, w, mcw, mcb, lcw, lcb, gb)


def _reverse_order(n_ctx_chunks, n_chunks):
    return lambda c: jnp.where(c < n_ctx_chunks, n_ctx_chunks - 1 - c, n_chunks - 1 - (c - n_ctx_chunks))


def _tri_mask(L, reverse):
    row = lax.broadcasted_iota(I32, (L, L), 0)
    col = lax.broadcasted_iota(I32, (L, L), 1)
    return (col >= row) if reverse else (col <= row)


def _bidir_scan(kern, L, row_ins, lane_ins, res_ins, out_cols, scratch, bsz, t, ctx_len, name):
    nc = t // L
    rev = _reverse_order(ctx_len // L, nc)
    fwd = lambda i: i
    n = bsz * t

    def row(c, order):
        return pl.BlockSpec((L, c), lambda b, i: (b * nc + order(i), 0))

    def lanes(r, order):
        return pl.BlockSpec((r, L), lambda b, i: (0, b * nc + order(i)))

    in_specs, args = [], []
    for order in (fwd, rev):
        in_specs += [row(a.shape[1], order) for a in row_ins] + [lanes(r, order) for _, r in lane_ins]
        args += list(row_ins) + [a for a, _ in lane_ins]
    in_specs += [_resident(a.shape) for a in res_ins]
    args += list(res_ins)
    return pl.pallas_call(
        kern,
        grid=(bsz, nc),
        in_specs=in_specs,
        out_specs=[row(c, fwd) for c in out_cols] + [row(c, rev) for c in out_cols],
        out_shape=[jax.ShapeDtypeStruct((n, c), BF16) for c in out_cols] * 2,
        scratch_shapes=scratch,
        compiler_params=_cparams("arbitrary", "arbitrary"),
        name=name,
    )(*args)


def _zero_at_start(*scratch):
    @pl.when(pl.program_id(1) == 0)
    def _():
        for s in scratch:
            s[...] = jnp.zeros_like(s)


def _mlstm_kernel(L, qf, kf, vf, gcf, grf, qb, kb, vb, gcb, grb, of, ob, c_scr, m_scr):
    _zero_at_start(c_scr, m_scr)
    ones = jnp.ones((L, LANES), BF16)
    wide = lambda c, k: jnp.concatenate([c] * k, axis=1)
    er = lax.broadcasted_iota(I32, (LANES, 2 * M_HEADS * LANES), 0)
    ej = jnp.right_shift(lax.broadcasted_iota(I32, (LANES, 2 * M_HEADS * LANES), 1), LANES.bit_length() - 1)
    chains = []
    for d, (q_ref, k_ref, v_ref, gc_ref, gr_ref, out_ref) in enumerate(((qf, kf, vf, gcf, grf, of),
                                                                         (qb, kb, vb, gcb, grb, ob))):
        reverse = d == 1
        mask = _tri_mask(L, reverse)
        mask01 = mask.astype(BF16)
        gc = gc_ref[...]
        gr = gr_ref[...]
        bcol = _sel_cols(mask01, gc)
        brow = _sel_rows(gr, mask01)
        glane = lax.broadcasted_iota(I32, (L, LANES), 1)
        src = jnp.where(ej < M_HEADS, M_HEADS * d + ej, M_HEADS + M_HEADS * d + ej)
        s1, s2, _ = _split3(jnp.where(glane < 2 * M_HEADS, gc, bcol))
        sel = (er == src).astype(BF16)
        rep = _dot(s1, sel) + _dot(s2, sel)
        last = 0 if reverse else L - 1
        for h in range(M_HEADS):
            fo, io = 2 * M_HEADS + M_HEADS * d + h, M_HEADS * d + h
            ic, bc = rep[:, h * LANES:(h + 1) * LANES], rep[:, (M_HEADS + h) * LANES:(M_HEADS + h + 1) * LANES]
            r1 = gr[io:io + 1, :] - brow[fo:fo + 1, :]
            m = m_scr[d, h:h + 1, :]
            g = bc + m
            mx = jnp.max(jnp.where(mask, r1, -jnp.inf), axis=1, keepdims=True)
            mt = jnp.maximum(g, bc + mx)
            p = jnp.exp(jnp.where(mask, wide(bc - mt, L // LANES) + r1, -jnp.inf))
            bl = bc[last:last + 1, :]
            a_s = bl - bc + ic
            m_new = jnp.maximum(bl + m, jnp.max(a_s, axis=0, keepdims=True))
            chains.append(dict(
                d=d, h=h, out_ref=out_ref, mt=mt, w_inter=jnp.exp(g - mt), p=p,
                m_new=m_new, decay=jnp.exp(bl + m - m_new), ws=jnp.exp(a_s - m_new),
                qh=q_ref[:, h * M_DK:(h + 1) * M_DK], kh=k_ref[:, h * M_DK:(h + 1) * M_DK],
                vx=jnp.concatenate([v_ref[:, h * M_DV:(h + 1) * M_DV], ones], axis=1)))
    for ch in chains:
        ch["sc"] = (_dot_nt(ch["qh"], ch["kh"]) * ch["p"]).astype(BF16)
    for ch in chains:
        cx = c_scr[ch["d"], ch["h"]]
        numx = wide(ch["w_inter"], 1 + M_DV // LANES) * _dot(ch["qh"], cx.astype(BF16)) + _dot(ch["sc"], ch["vx"])
        den = numx[:, M_DV:]
        rinv = 1.0 / jnp.maximum(jnp.abs(den), jnp.exp(-ch["mt"]))
        hh = numx[:, 0:M_DV] * wide(rinv, M_DV // LANES)
        ch["out_ref"][:, ch["h"] * M_DV:(ch["h"] + 1) * M_DV] = hh.astype(BF16)
    for ch in chains:
        d, h = ch["d"], ch["h"]
        kw = (ch["kh"].astype(F32) * ch["ws"]).astype(BF16)
        c_scr[d, h] = wide(ch["decay"], 1 + M_DV // LANES) * c_scr[d, h] + _dot_tn(kw, ch["vx"])
        m_scr[d, h:h + 1, :] = ch["m_new"]


def _mlstm(q, k, v, gcol, grow, bsz, t, ctx_len):
    L = M_CHUNK
    scratch = [pltpu.VMEM((2, M_HEADS, M_DK, M_DV + LANES), F32), pltpu.VMEM((2, 8, LANES), F32)]
    return _bidir_scan(functools.partial(_mlstm_kernel, L), L, [q, k, v, gcol], [(grow, 16)], [], [1024], scratch,
                       bsz, t, ctx_len, "mlstm_scan")


def _lru_kernel(L, uf, ub_, wa_ref, wx_ref, ba_ref, bx_ref, lam_ref, of, ob, a_scr, b_scr, h_scr):
    _zero_at_start(h_scr)
    R, C = L // SUBLANES, LRU_BLOCKS * LRU_BW
    i0, i1 = lax.broadcasted_iota(I32, (L, L), 0), lax.broadcasted_iota(I32, (L, L), 1)
    when = lambda p: jnp.bitwise_and(p, SUBLANES - 1) * R + jnp.right_shift(p, SUBLANES.bit_length() - 1)
    regroup = (i1 == when(i0)).astype(BF16)
    ungroup = (i0 == when(i1)).astype(BF16)
    sub = lax.broadcasted_iota(I32, (SUBLANES, 1), 0)
    dirs = ((uf, of), (ub_, ob))
    for d, (u_ref, _) in enumerate(dirs):
        u = _dot(regroup, u_ref[...])
        ub = u.astype(BF16)
        sp = -LRU_C * _softplus(-lam_ref[d])
        for n in range(LRU_BLOCKS):
            sl = slice(n * LRU_BW, (n + 1) * LRU_BW)
            rg = _sigmoid(_dot(ub[:, sl], wa_ref[d, n]) + ba_ref[d, :, sl])
            ig = _sigmoid(_dot(ub[:, sl], wx_ref[d, n]) + bx_ref[d, :, sl])
            log_a = rg * sp[:, sl]
            a = jnp.exp(log_a)
            a_scr[d, :, sl] = a
            b_scr[d, :, sl] = jnp.sqrt(-jnp.tanh(log_a) * (a * a + 1.0)) * (ig * u[:, sl])
    S = SUBLANES
    hs = [jnp.zeros((S, C), F32), jnp.zeros((S, C), F32)]
    ps = [jnp.ones((S, C), F32), jnp.ones((S, C), F32)]
    for j in range(R):
        for d in range(2):
            rows = slice(S * j, S * j + S) if d == 0 else slice(S * (R - 1 - j), S * (R - j))
            a = a_scr[d, rows, :]
            hs[d] = a * hs[d] + b_scr[d, rows, :]
            ps[d] = a * ps[d]
            b_scr[d, rows, :] = hs[d]
            a_scr[d, rows, :] = ps[d]
    for d, (_, out_ref) in enumerate(dirs):
        c = h_scr[d:d + 1, :]
        cin = jnp.zeros((S, C), F32)
        for s in (range(S) if d == 0 else reversed(range(S))):
            cin = jnp.where(sub == s, c, cin)
            c = ps[d][s:s + 1, :] * c + hs[d][s:s + 1, :]
        h_scr[d:d + 1, :] = c
        h = b_scr[d].reshape(R, S, C) + a_scr[d].reshape(R, S, C) * cin[None]
        out_ref[...] = _dot(ungroup, h.reshape(L, C).astype(BF16)).astype(BF16)


def _lru(u, wa, wx, ba, bx, lam, bsz, t, ctx_len):
    L = LRU_TILE
    scratch = [pltpu.VMEM((2, L, 1024), F32), pltpu.VMEM((2, L, 1024), F32), pltpu.VMEM((8, 1024), F32)]
    return _bidir_scan(functools.partial(_lru_kernel, L), L, [u], [], [wa, wx, ba, bx, lam], [1024], scratch,
                       bsz, t, ctx_len, "rglru_scan")


def _swiglu_rows(x, is_ctx, modl_ref, modc_ref, g2_ref, g3_ref, w1_ref, w2_ref):
    h = _adaln(x, g2_ref[...], modl_ref, modc_ref, 3, is_ctx).astype(BF16)
    ff = w2_ref.shape[0]
    acc = jnp.zeros(x.shape, F32)
    for j in range(ff // FF_STEP):
        lo = j * FF_STEP
        gj = _dot(h, w1_ref[:, lo:lo + FF_STEP])
        uj = _dot(h, w1_ref[:, ff + lo:ff + lo + FF_STEP])
        acc = acc + _dot((_silu(gj) * uj).astype(BF16), w2_ref[lo:lo + FF_STEP, :])
    return x + _mod(modl_ref, modc_ref, 5, is_ctx) * _rms(acc, g3_ref[...])


def _out_tail(tm, ctx_len, a1, a2, x_ref, modl_ref, modc_ref, g_ref, w_ref, xo_ref, ffn_refs=None):
    is_ctx = _row_ids(pl.program_id(1), tm) < ctx_len
    half = w_ref.shape[0] // 2
    y = _dot(a1.astype(BF16), w_ref[0:half, :]) + _dot(a2.astype(BF16), w_ref[half:, :])
    x = x_ref[...] + _mod(modl_ref, modc_ref, 2, is_ctx) * _rms(y, g_ref[...])
    if ffn_refs is not None:
        x = _swiglu_rows(x, is_ctx, modl_ref, modc_ref, *ffn_refs)
    xo_ref[...] = x


def _out_ab_kernel(tm, ctx_len, hf, hb, og, lf, lb, glg, nw, x_ref, modl_ref, modc_ref, g_ref, w_ref,
                   g2_ref, g3_ref, w1_ref, w2_ref, xo_ref):
    hm = _head_rms(hf[...].astype(F32) + hb[...].astype(F32), nw[...], M_HEADS) * og[...].astype(F32)
    hl = (lf[...].astype(F32) + lb[...].astype(F32)) * glg[...].astype(F32)
    _out_tail(tm, ctx_len, hm, hl, x_ref, modl_ref, modc_ref, g_ref, w_ref, xo_ref,
              (g2_ref, g3_ref, w1_ref, w2_ref))


def _out_cd_kernel(tm, ctx_len, gf, gb, grs, yf, yb, sx, zs, gnw, dsk, snw, x_ref, modl_ref, modc_ref, g_ref,
                   w_ref, xo_ref):
    og = _head_rms(gf[...].astype(F32) + gb[...].astype(F32), gnw[...], G_HEADS) * grs[...].astype(F32)
    ys = yf[...].astype(F32) + yb[...].astype(F32) + dsk[...] * sx[...].astype(F32)
    ys = _rms(ys * zs[...].astype(F32), snw[...])
    _out_tail(tm, ctx_len, og, ys, x_ref, modl_ref, modc_ref, g_ref, w_ref, xo_ref)


def _out_proj(kern, name, acts, vecs, x2, modl, modc, g, w, bsz, t, ctx_len, extra=()):
    tm = _big_row_tile(t)
    nt = t // tm
    row = lambda c: pl.BlockSpec((tm, c), lambda b, i: (b * nt + i, 0))
    return pl.pallas_call(
        functools.partial(kern, tm, ctx_len),
        grid=(bsz, nt),
        in_specs=[row(1024)] * len(acts) + [_resident((1, 1024))] * len(vecs)
        + [row(D_MODEL), pl.BlockSpec((1, 8, D_MODEL), lambda b, i: (b, 0, 0)),
           _resident((8, D_MODEL)), _resident((1, D_MODEL)), _resident(w.shape)]
        + [_resident(a.shape) for a in extra],
        out_specs=row(D_MODEL),
        out_shape=jax.ShapeDtypeStruct(x2.shape, F32),
        compiler_params=_cparams("arbitrary", "arbitrary"),
        name=name,
    )(*acts, *vecs, x2, modl, modc, g, w, *extra)


def _in_cd_kernel(tm, ctx_len, x_ref, modl_ref, modc_ref, g_ref, w_ref, cw_ref, cb_ref, dtb_ref, aneg_ref,
                  gq_ref, gk_ref, gv_ref, gr_ref, z_ref, sx_ref, sb_ref, sc_ref, ga_ref, dts_ref, dts_t_ref, h_scr):
    is_lat = pl.program_id(1) * tm >= ctx_len
    h_scr[...] = _adaln(x_ref[...], g_ref[...], modl_ref, modc_ref, 0, jnp.logical_not(is_lat)).astype(BF16)

    def conv_silu(col0):
        return lambda y, c: _silu(_dwconv(y, cw_ref[:, col0 + c:col0 + c + PROJ_COLS],
                                          cb_ref[:, col0 + c:col0 + c + PROJ_COLS], is_lat))

    _project(h_scr, w_ref, 0, gq_ref, lambda y, c: y * (G_DK ** -0.5))
    _project(h_scr, w_ref, 512, gk_ref, lambda y, c: y)
    _project(h_scr, w_ref, 1024, gv_ref, lambda y, c: y)
    _project(h_scr, w_ref, 2048, gr_ref, lambda y, c: _silu(y))
    _project(h_scr, w_ref, 3072, z_ref, lambda y, c: _silu(y))
    _project(h_scr, w_ref, 4096, sx_ref, conv_silu(0))
    _project(h_scr, w_ref, 5120, sb_ref, conv_silu(1024))
    _project(h_scr, w_ref, 5376, sc_ref, conv_silu(1280))
    ga_ref[...] = _dot(h_scr[...], w_ref[:, 5632:5760])
    dt = _softplus(_dot(h_scr[...], w_ref[:, 5760:5888]) + dtb_ref[...])
    lane = lax.broadcasted_iota(I32, dt.shape, 1)
    dts = jnp.where(lane < 2 * S_HEADS, dt, dt * aneg_ref[...])
    dts_ref[...] = dts
    dts_t_ref[...] = dts.T


def _in_cd(x2, modl, modc, g, w, cw, cb, dtb, aneg, bsz, t, ctx_len):
    tm = ROW_TILE
    nt = t // tm
    n = bsz * t
    row = lambda c: pl.BlockSpec((tm, c), lambda b, i: (b * nt + i, 0))
    outs = [(512, BF16), (512, BF16), (1024, BF16), (1024, BF16), (1024, BF16), (1024, BF16), (256, BF16),
            (256, BF16), (LANES, F32), (LANES, F32)]
    return pl.pallas_call(
        functools.partial(_in_cd_kernel, tm, ctx_len),
        grid=(bsz, nt),
        in_specs=[row(D_MODEL), pl.BlockSpec((1, 8, D_MODEL), lambda b, i: (b, 0, 0)), _resident((8, D_MODEL)),
                  _resident((1, D_MODEL)), _resident(w.shape), _resident(cw.shape), _resident(cb.shape),
                  _resident(dtb.shape), _resident(aneg.shape)],
        out_specs=[row(c) for c, _ in outs] + [pl.BlockSpec((LANES, tm), lambda b, i: (0, b * nt + i))],
        out_shape=[jax.ShapeDtypeStruct((n, c), dt) for c, dt in outs] + [jax.ShapeDtypeStruct((LANES, n), F32)],
        scratch_shapes=[pltpu.VMEM((tm, D_MODEL), BF16)],
        compiler_params=_cparams("arbitrary", "arbitrary"),
        name="in_proj_cd",
    )(x2, modl, modc, g, w, cw, cb, dtb, aneg)


def _gla_kernel(L, C, qf, kf, vf, gaf, qb, kb, vb, gab, awh_ref, awl_ref, ab_ref, of, ob, s_scr, att_scr):
    _zero_at_start(s_scr, att_scr)
    nb = L // C
    chains = []
    for d, (q_ref, k_ref, v_ref, ga_ref, out_ref) in enumerate(((qf, kf, vf, gaf, of), (qb, kb, vb, gab, ob))):
        reverse = d == 1
        mask = _tri_mask(L, reverse)
        g1, g2, _ = _split3(ga_ref[...])
        pre = _dot(g1, awh_ref[d]) + _dot(g2, awh_ref[d]) + _dot(g1, awl_ref[d])
        lg = _log_sigmoid(pre + ab_ref[d]) * (1.0 / G_TAU)
        ball = _sel_cols(mask.astype(BF16), lg)
        for h in range(G_HEADS):
            ks = slice(h * G_DK, (h + 1) * G_DK)
            b = ball[:, ks] * LOG2E
            qh = q_ref[:, ks].astype(F32)
            inter = _dot_nt((qh * jnp.exp2(b)).astype(BF16), s_scr[d, h].astype(BF16))
            chains.append(dict(d=d, h=h, reverse=reverse, mask=mask, b=b, qh=qh, kh=k_ref[:, ks].astype(F32),
                               vh=v_ref[:, h * G_DV:(h + 1) * G_DV], inter=inter, out_ref=out_ref, blocks=[]))
    T = min(G_GROUP, L)
    for i in range(nb):
        lo, hi = i * C, (i + 1) * C
        t0 = lo // T * T
        for c, ch in enumerate(chains):
            b = ch["b"]
            if ch["reverse"]:
                bref = b[hi:hi + 1, :] if i < nb - 1 else jnp.zeros((1, G_DK), F32)
            else:
                bref = b[lo - 1:lo, :] if i > 0 else jnp.zeros((1, G_DK), F32)
            qi = (ch["qh"][lo:hi, :] * jnp.exp2(b[lo:hi, :] - bref)).astype(BF16)
            ki = (ch["kh"][t0:t0 + T] * jnp.exp2(bref - b[t0:t0 + T])).astype(BF16)
            att_scr[c, lo:hi, t0:t0 + T] = _dot_nt(qi, ki)
    for g in range(L // T):
        t0, t1 = g * T, (g + 1) * T
        for c, ch in enumerate(chains):
            b = ch["b"]
            e0, e1 = (t1, L) if ch["reverse"] else (0, t0)
            if e1 > e0:
                bref = b[t1:t1 + 1, :] if ch["reverse"] else b[t0 - 1:t0, :]
                qg = (ch["qh"][t0:t1] * jnp.exp2(b[t0:t1] - bref)).astype(BF16)
                kg = (ch["kh"][e0:e1] * jnp.exp2(bref - b[e0:e1])).astype(BF16)
                att_scr[c, t0:t1, e0:e1] = _dot_nt(qg, kg)
    for c, ch in enumerate(chains):
        att = jnp.where(ch["mask"], att_scr[c], 0.0).astype(BF16)
        oh = ch["inter"] + _dot(att, ch["vh"])
        ch["out_ref"][:, ch["h"] * G_DV:(ch["h"] + 1) * G_DV] = oh.astype(BF16)
    for ch in chains:
        d, h, b = ch["d"], ch["h"], ch["b"]
        last = 0 if ch["reverse"] else L - 1
        bl = b[last:last + 1, :]
        kd = (ch["kh"] * jnp.exp2(bl - b)).astype(BF16)
        s_scr[d, h] = s_scr[d, h] * jnp.exp2(bl) + _dot_tn(ch["vh"], kd)


def _gla(q, k, v, ga, awh, awl, ab, bsz, t, ctx_len):
    L, C = G_CHUNK, G_SUB
    scratch = [pltpu.VMEM((2, G_HEADS, G_DV, G_DK), F32), pltpu.VMEM((2 * G_HEADS, L, L), F32)]
    return _bidir_scan(functools.partial(_gla_kernel, L, C), L, [q, k, v, ga], [], [awh, awl, ab], [1024], scratch,
                       bsz, t, ctx_len, "gla_scan")


def _ssd_kernel(L, xf, bf, cf, gcf, grf, xb, bb, cb_, gcb, grb, of, ob, s_scr):
    _zero_at_start(s_scr)
    er = lax.broadcasted_iota(I32, (LANES, S_HEADS * S_P), 0)
    ec = jnp.right_shift(lax.broadcasted_iota(I32, (LANES, S_HEADS * S_P), 1), S_P.bit_length() - 1)
    glane = lax.broadcasted_iota(I32, (L, LANES), 1)
    lane = lax.broadcasted_iota(I32, (L, 2 * S_P), 1)
    dirs, chains = [], []
    for d, (x_ref, b_ref, c_ref, gc_ref, gr_ref, out_ref) in enumerate(((xf, bf, cf, gcf, grf, of),
                                                                         (xb, bb, cb_, gcb, grb, ob))):
        reverse = d == 1
        mask = _tri_mask(L, reverse)
        mask01 = mask.astype(BF16)
        gc = gc_ref[...]
        gr = gr_ref[...]
        bcol = _sel_cols(mask01, gc)
        brow = _sel_rows(gr, mask01)
        last = 0 if reverse else L - 1
        dt_o, la_o = S_HEADS * d, 2 * S_HEADS + S_HEADS * d
        sel_la = (er == ec + la_o).astype(BF16)
        x = x_ref[...]
        bla = jnp.where((glane >= la_o) & (glane < la_o + S_HEADS), bcol, 0.0)
        dt_at_la = pltpu.roll(gc, 2 * S_HEADS, 1)
        bl = bla[last:last + 1, :]
        ebx = _dot(jnp.exp(bla).astype(BF16), sel_la)
        wx = _dot((jnp.exp(bl - bla) * dt_at_la).astype(BF16), sel_la)
        dirs.append(dict(d=d, mask=mask, gr=gr, x=x, ebx=ebx, last=last, dt_o=dt_o, la_o=la_o, out_ref=out_ref,
                         xw=(x.astype(F32) * wx).astype(BF16),
                         bcol2=bcol * LOG2E, brow2=brow * LOG2E))
        for g in range(S_GROUPS):
            ns = slice(g * S_N, (g + 1) * S_N)
            chains.append(dict(dr=dirs[-1], g=g, bg=b_ref[:, ns], cg=c_ref[:, ns],
                               gs=slice(g * S_HPG * S_P, (g + 1) * S_HPG * S_P)))
    for ch in chains:
        dr = ch["dr"]
        ch["cb"] = _dot_nt(ch["cg"], ch["bg"])
        ch["inter"] = _dot(ch["cg"], s_scr[dr["d"], ch["g"]].astype(BF16)) * dr["ebx"][:, ch["gs"]]
    for ch in chains:
        dr, g = ch["dr"], ch["g"]
        parts = []
        for pr in range(S_HPG // 2):
            e0 = g * S_HPG + 2 * pr
            xp = dr["x"][:, e0 * S_P:(e0 + 2) * S_P]
            res = []
            for e in (e0, e0 + 1):
                col = dr["la_o"] + e
                bc, br = dr["bcol2"][:, col:col + 1], dr["brow2"][col:col + 1, :]
                dtr = dr["gr"][dr["dt_o"] + e:dr["dt_o"] + e + 1, :]
                att = (ch["cb"] * dtr * jnp.exp2(jnp.where(dr["mask"], bc - br, -jnp.inf))).astype(BF16)
                res.append(_dot(att, xp))
            parts.append(jnp.where(lane < S_P, res[0], res[1]))
        dr["out_ref"][:, ch["gs"]] = (jnp.concatenate(parts, axis=1) + ch["inter"]).astype(BF16)
    for ch in chains:
        dr, g, gs = ch["dr"], ch["g"], ch["gs"]
        s_scr[dr["d"], g] = (s_scr[dr["d"], g] * dr["ebx"][dr["last"]:dr["last"] + 1, gs]
                             + _dot_tn(ch["bg"], dr["xw"][:, gs]))


def _ssd(x, bm, cm, gcol, grow, bsz, t, ctx_len):
    L = S_CHUNK
    scratch = [pltpu.VMEM((2, S_GROUPS, S_N, S_HPG * S_P), F32)]
    return _bidir_scan(functools.partial(_ssd_kernel, L), L, [x, bm, cm, gcol], [(grow, 4 * S_HEADS)], [], [1024],
                       scratch,
                       bsz, t, ctx_len, "ssd_scan")


def _router_kernel(x_ref, modl_ref, g_ref, rw_ref, rb_ref, h_ref, idx_ref, gate_ref, cnt_ref, base_scr):
    @pl.when((pl.program_id(0) == 0) & (pl.program_id(1) == 0))
    def _():
        base_scr[...] = jnp.zeros_like(base_scr)

    h = _rms(x_ref[...], g_ref[...]) * (1.0 + modl_ref[0, 4:5, :]) + modl_ref[0, 3:4, :]
    h_ref[...] = _pack_rows(h)
    h1, h2, _ = _split3(h)
    both = _dot(h1, rw_ref[...])
    logits = both[:, :LANES] + both[:, LANES:] + _dot(h2, rw_ref[:, :LANES]) + rb_ref[...]
    lane = lax.broadcasted_iota(I32, logits.shape, 1)
    logits = jnp.where(lane < N_EXPERTS, logits, -jnp.inf)
    lanef = lane.astype(F32)
    m1 = jnp.max(logits, axis=1, keepdims=True)
    i1 = jnp.min(jnp.where(logits == m1, lanef, float(LANES)), axis=1, keepdims=True)
    rest = jnp.where(lanef == i1, -jnp.inf, logits)
    m2 = jnp.max(rest, axis=1, keepdims=True)
    i2 = jnp.min(jnp.where(rest == m2, lanef, float(LANES)), axis=1, keepdims=True)
    e = jnp.exp(m2 - m1)
    g1 = 1.0 / (1.0 + e)
    gate_ref[...] = jnp.where(lane == 0, g1, jnp.where(lane == 1, e * g1, 0.0))
    tm = logits.shape[0]
    oh1, oh2 = (lanef == i1).astype(F32), (lanef == i2).astype(F32)
    earlier = (lax.broadcasted_iota(I32, (tm, tm), 1) < lax.broadcasted_iota(I32, (tm, tm), 0)).astype(BF16)
    seen = base_scr[0:1, :] + _dot(earlier, (oh1 + oh2).astype(BF16))
    r1 = jnp.sum(oh1 * seen, axis=1, keepdims=True)
    r2 = jnp.sum(oh2 * seen, axis=1, keepdims=True)
    base_scr[0:1, :] = base_scr[0:1, :] + jnp.sum(oh1 + oh2, axis=0, keepdims=True)
    cnt_ref[...] = base_scr[...]
    idx_ref[...] = jnp.where(lane == 0, i1, jnp.where(lane == 1, i2, jnp.where(lane == 2, r1, jnp.where(
        lane == 3, r2, 0.0)))).astype(I32)


def _router(x2, modl, g, rw, rb, bsz, t, ctx_len):
    tm = ROW_TILE
    nt, nct = t // tm, ctx_len // tm
    nl = bsz * (t - ctx_len)
    nlt = nt - nct
    orow = lambda c: pl.BlockSpec((tm, c), lambda b, i: (b * nlt + i, 0))
    return pl.pallas_call(
        _router_kernel,
        grid=(bsz, nlt),
        in_specs=[pl.BlockSpec((tm, D_MODEL), lambda b, i: (b * nt + nct + i, 0)),
                  pl.BlockSpec((1, 8, D_MODEL), lambda b, i: (b, 0, 0)),
                  _resident((1, D_MODEL)), _resident(rw.shape), _resident(rb.shape)],
        out_specs=[orow(D_MODEL // 2), orow(LANES), orow(LANES), pl.BlockSpec((8, LANES), lambda b, i: (0, 0))],
        out_shape=[jax.ShapeDtypeStruct((nl, D_MODEL // 2), jnp.uint32), jax.ShapeDtypeStruct((nl, LANES), I32),
                   jax.ShapeDtypeStruct((nl, LANES), F32), jax.ShapeDtypeStruct((8, LANES), F32)],
        scratch_shapes=[pltpu.VMEM((8, LANES), F32)],
        compiler_params=_cparams("arbitrary", "arbitrary"),
        name="moe_router",
    )(x2, modl, g, rw, rb)


def _gather_rows(table, idx):
    n_rows, d = idx.shape[0], table.shape[1]
    n_workers = SC_CORES * SC_SUBCORES
    per_worker = n_rows // n_workers
    chunk = SC_GATHER_ROWS
    assert per_worker * n_workers == n_rows and per_worker % chunk == 0
    mesh = plsc.VectorSubcoreMesh(core_axis_name="c", subcore_axis_name="s")

    @functools.partial(
        pl.kernel, mesh=mesh, out_type=jax.ShapeDtypeStruct((n_rows, d), table.dtype),
        scratch_types=[pltpu.VMEM((chunk,), I32), pltpu.VMEM((chunk, d), table.dtype), pltpu.SemaphoreType.DMA],
        name="sc_gather_rows")
    def gather(table_hbm, idx_hbm, out_hbm, idx_v, rows_v, sem):
        base = (lax.axis_index("s") * SC_CORES + lax.axis_index("c")) * per_worker

        @pl.loop(0, per_worker // chunk)
        def _(j):
            off = pl.multiple_of(base + j * chunk, 8)
            pltpu.sync_copy(idx_hbm.at[pl.ds(off, chunk)], idx_v)
            pltpu.async_copy(table_hbm.at[idx_v], rows_v, sem).wait()
            pltpu.sync_copy(rows_v, out_hbm.at[pl.ds(off, chunk)])

    return gather(table, idx)


def _dispatch_rows(table, slot, n_out):
    assert TOP_K == 2
    n_assign, (n_tok, d) = slot.shape[0], table.shape
    n_workers = SC_CORES * SC_SUBCORES
    per_worker = n_out // n_workers
    chunk, vec = SC_GATHER_ROWS, SC_LANES
    assert per_worker * n_workers == n_out and per_worker % chunk == 0 and n_assign % vec == 0
    assert n_out < 3 * n_tok
    mesh = plsc.VectorSubcoreMesh(core_axis_name="c", subcore_axis_name="s")

    @functools.partial(
        pl.kernel, mesh=mesh, out_type=jax.ShapeDtypeStruct((n_out, d), table.dtype),
        scratch_types=[pltpu.VMEM((n_assign,), I32), pltpu.VMEM((per_worker,), I32),
                       pltpu.VMEM((chunk, d), table.dtype), pltpu.SemaphoreType.DMA],
        compiler_params=pltpu.CompilerParams(needs_layout_passes=False),
        name="sc_dispatch_rows")
    def dispatch(table_hbm, slot_hbm, out_hbm, slots_v, tok_v, rows_v, sem):
        base = (lax.axis_index("s") * SC_CORES + lax.axis_index("c")) * per_worker
        pltpu.sync_copy(slot_hbm, slots_v)
        lane = lax.broadcasted_iota(I32, (vec,), 0)

        @pl.loop(0, per_worker // vec)
        def _(j):
            p = base + j * vec + lane
            p = jnp.where(p >= n_tok, p - n_tok, p)
            tok_v[pl.ds(j * vec, vec)] = jnp.where(p >= n_tok, p - n_tok, p)

        @pl.loop(0, n_assign // vec)
        def _(j):
            s = slots_v[pl.ds(j * vec, vec)] - base
            mine = (s >= 0) & (s < per_worker)
            tok = lax.shift_right_logical(j * vec + lane, 1)
            plsc.store_scatter(tok_v, [jnp.where(mine, s, 0)], tok, mask=mine)

        @pl.loop(0, per_worker // chunk)
        def _(j):
            off = pl.multiple_of(j * chunk, 8)
            pltpu.async_copy(table_hbm.at[tok_v.at[pl.ds(off, chunk)]], rows_v, sem).wait()
            pltpu.sync_copy(rows_v, out_hbm.at[pl.ds(base + off, chunk)])

    return dispatch(table, slot)


def _moe_kernel(be_ref, nu_ref, nv_ref, x_ref, w1g_ref, w1u_ref, w2_ref, y_ref, xb_ref, acc_ref):
    i, j = pl.program_id(0), pl.program_id(1)
    bm = x_ref.shape[0]

    def swiglu_rows(rows):
        x = xb_ref[0:rows, :]
        a = (_silu(_dot(x, w1g_ref[0].astype(BF16))) * _dot(x, w1u_ref[0].astype(BF16))).astype(BF16)
        acc_ref[0:rows, :] += _dot(a, w2_ref[0].astype(BF16))

    @pl.when(i < nu_ref[0])
    def _():
        @pl.when(j == 0)
        def _():
            acc_ref[...] = jnp.zeros_like(acc_ref)
            xb_ref[...] = _unpack_rows(x_ref[...]).astype(BF16)

        @pl.when(nv_ref[i] > bm // 2)
        def _():
            swiglu_rows(bm)

        @pl.when(nv_ref[i] <= bm // 2)
        def _():
            swiglu_rows(bm // 2)

        @pl.when(j == pl.num_programs(1) - 1)
        def _():
            y_ref[...] = _pack_rows(acc_ref[...])


def _moe_experts(xs, w1, w2, block_e, n_used, n_valid, n_blocks):
    bm, fc = MOE_ROWS, MOE_FF
    nff = FF_EXPERT // fc
    used = lambda i, nu: jnp.minimum(i, nu[0] - 1)
    ffi = lambda i, j, nu: jnp.where(i < nu[0], j, nff - 1)
    grid_spec = pltpu.PrefetchScalarGridSpec(
        num_scalar_prefetch=3,
        grid=(n_blocks, nff),
        in_specs=[pl.BlockSpec((bm, D_MODEL // 2), lambda i, j, be, nu, nv: (used(i, nu), 0)),
                  pl.BlockSpec((1, D_MODEL, fc), lambda i, j, be, nu, nv: (be[used(i, nu)], 0, ffi(i, j, nu))),
                  pl.BlockSpec((1, D_MODEL, fc),
                               lambda i, j, be, nu, nv: (be[used(i, nu)], 0, nff + ffi(i, j, nu))),
                  pl.BlockSpec((1, fc, D_MODEL), lambda i, j, be, nu, nv: (be[used(i, nu)], ffi(i, j, nu), 0))],
        out_specs=pl.BlockSpec((bm, D_MODEL // 2), lambda i, j, be, nu, nv: (used(i, nu), 0)),
        scratch_shapes=[pltpu.VMEM((bm, D_MODEL), BF16), pltpu.VMEM((bm, D_MODEL), F32)],
    )
    return pl.pallas_call(
        _moe_kernel,
        grid_spec=grid_spec,
        out_shape=jax.ShapeDtypeStruct((n_blocks * bm, D_MODEL // 2), jnp.uint32),
        compiler_params=_cparams("arbitrary", "arbitrary"),
        name="moe_experts",
    )(block_e, n_used, n_valid, xs, w1, w1, w2)


def _combine_kernel(y0_ref, y1_ref, gate_ref, x_ref, modl_ref, g_ref, o_ref):
    gt = gate_ref[...]
    y = gt[:, 0:1] * _unpack_rows(y0_ref[...]) + gt[:, 1:2] * _unpack_rows(y1_ref[...])
    o_ref[...] = x_ref[...] + modl_ref[0, 5:6, :] * _rms(y, g_ref[...])


def _combine(yk, gates, x2, modl, g, bsz, t, ctx_len):
    tm = ROW_TILE
    nt, nct = t // tm, ctx_len // tm
    nlt = nt - nct
    nl = bsz * (t - ctx_len)
    orow = lambda c: pl.BlockSpec((tm, c), lambda b, i: (b * nlt + i, 0))
    second = pl.BlockSpec((tm, D_MODEL // 2), lambda b, i: (nl // tm + b * nlt + i, 0))
    return pl.pallas_call(
        _combine_kernel,
        grid=(bsz, nlt),
        in_specs=[orow(D_MODEL // 2), second, orow(LANES),
                  pl.BlockSpec((tm, D_MODEL), lambda b, i: (b * nt + nct + i, 0)),
                  pl.BlockSpec((1, 8, D_MODEL), lambda b, i: (b, 0, 0)), _resident((1, D_MODEL))],
        out_specs=orow(D_MODEL),
        out_shape=jax.ShapeDtypeStruct((nl, D_MODEL), F32),
        compiler_params=_cparams("arbitrary", "arbitrary"),
        name="moe_combine",
    )(yk, yk, gates, x2, modl, g)


def _routing_tables(experts, ranks, counts):
    bm = MOE_ROWS
    n_assign = experts.size
    padded = (counts + bm - 1) // bm * bm
    ends_p = jnp.cumsum(padded)
    pstarts = ends_p - padded
    onehot = experts[..., None] == jnp.arange(N_EXPERTS, dtype=I32)
    slot = (jnp.sum(jnp.where(onehot, pstarts, 0), axis=-1) + ranks).astype(I32)
    n_blocks = n_assign // bm + N_EXPERTS
    block_e = jnp.minimum(jnp.sum(ends_p[None, :] <= (jnp.arange(n_blocks, dtype=I32) * bm)[:, None], axis=1),
                          N_EXPERTS - 1).astype(I32)
    n_used = (ends_p[-1] // bm).astype(I32).reshape(1)
    sel = block_e[:, None] == jnp.arange(N_EXPERTS, dtype=I32)
    rows_left = jnp.sum(jnp.where(sel, counts + pstarts, 0), axis=1) - jnp.arange(n_blocks, dtype=I32) * bm
    n_valid = jnp.clip(rows_left, 0, bm).astype(I32)
    return slot, block_e, n_used, n_valid, n_blocks


def kernel(x, c, ctx, c_ctx, mod_w, mod_b, norm_g, ab_in_w, m_conv_w, m_conv_b, m_gate_b, m_norm_w, l_conv_w,
           l_conv_b, l_wa, l_ba, l_wx, l_bx, l_lam, ab_out_w, ffn_w1, ffn_w2, cd_in_w, g_alpha_w, g_alpha_b,
           g_norm_w, s_conv_w, s_conv_b, s_dt_bias, s_A_log, s_D, s_norm_w, cd_out_w, router_w, router_b,
           moe_w1, moe_w2):
    bsz, seq, dm = x.shape
    ctx_len = ctx.shape[1]
    t = ctx_len + seq
    n = bsz * t
    assert ctx_len == ROW_TILE and seq % ROW_TILE == 0 and ROW_TILE % GRID_W == 0

    c_all = jnp.concatenate([c, c_ctx[None, :], jnp.zeros((-(bsz + 1) % 8, dm), F32)], axis=0)
    mods = _modulation(c_all, mod_w, mod_b).reshape(mod_w.shape[0], c_all.shape[0], 6, dm)
    pad2 = jnp.zeros((bsz, 2, dm), F32)

    def layer_mods(layer):
        modl = jnp.concatenate([mods[layer, :bsz], pad2], axis=1)
        modc = jnp.concatenate([mods[layer, bsz], pad2[0]], axis=0)
        return modl, modc

    modl, modc = layer_mods(0)
    g = norm_g[0]
    w = ab_in_w[0]
    o_qk, o_v, o_o, o_gt, o_lx, o_lg = 0, 1024, 2048, 3072, 3088, 4112
    w_in = jnp.concatenate([w[:, o_qk:o_v], w[:, o_v:o_o], w[:, o_o:o_gt], w[:, o_lx:o_lg], w[:, o_lg:],
                            w[:, o_gt:o_lx], jnp.zeros((dm, LANES - 16), F32)], axis=1).astype(BF16)
    gb = jnp.concatenate([m_gate_b[0].reshape(1, 16), jnp.zeros((1, LANES - 16), F32)], axis=1)
    x2, q, k, v, og, lx, glg, gates, gates_t = _in_ab(
        ctx.reshape(bsz * ctx_len, dm), x.reshape(bsz * seq, dm), modl, modc, g[0:1], w_in, m_conv_w[0],
        m_conv_b[0][None], l_conv_w[0], l_conv_b[0][None], gb, bsz, t, ctx_len)
    hm_f, hm_b = _mlstm(q, k, v, gates, gates_t, bsz, t, ctx_len)
    hl_f, hl_b = _lru(lx, l_wa[0].astype(BF16), l_wx[0].astype(BF16), l_ba[0][:, None], l_bx[0][:, None],
                      l_lam[0][:, None], bsz, t, ctx_len)
    x2 = _out_proj(_out_ab_kernel, "out_proj_swiglu_ab", [hm_f, hm_b, og, hl_f, hl_b, glg], [m_norm_w[0][None]],
                   x2, modl, modc, g[1:2], ab_out_w[0].astype(BF16), bsz, t, ctx_len,
                   extra=(g[2:3], g[3:4], ffn_w1[0].astype(BF16), ffn_w2[0].astype(BF16)))

    modl, modc = layer_mods(1)
    g = norm_g[1]
    w = cd_in_w[0]
    o_gv, o_gr, o_ga, o_z, o_xbc, o_dt = 1024, 2048, 3072, 3104, 4128, 5664
    zpad = jnp.zeros((dm, LANES - 32), F32)
    w_in = jnp.concatenate([w[:, :o_gv], w[:, o_gv:o_gr], w[:, o_gr:o_ga], w[:, o_z:o_xbc], w[:, o_xbc:o_dt],
                            w[:, o_ga:o_z], zpad, w[:, o_dt:], w[:, o_dt:], jnp.zeros((dm, LANES - 64), F32)],
                           axis=1).astype(BF16)
    dtb = jnp.concatenate([s_dt_bias[0].reshape(1, 32), s_dt_bias[0].reshape(1, 32),
                           jnp.zeros((1, LANES - 64), F32)], axis=1)
    aneg = jnp.concatenate([jnp.zeros((1, 32), F32), -jnp.exp(s_A_log[0].reshape(1, 32)),
                            jnp.zeros((1, LANES - 64), F32)], axis=1)
    gq, gk, gv, grs, zs, sx, sb, sc, ga, dts, dts_t = _in_cd(x2, modl, modc, g[0:1], w_in, s_conv_w[0],
                                                             s_conv_b[0][None], dtb, aneg, bsz, t, ctx_len)
    aw = jnp.zeros((2, LANES, G_HEADS * G_DK), F32)
    aw = aw.at[0, 0:G_RANK].set(g_alpha_w[0, 0]).at[1, G_RANK:2 * G_RANK].set(g_alpha_w[0, 1])
    awh = aw.astype(BF16)
    awl = (aw - awh.astype(F32)).astype(BF16)
    og_f, og_b = _gla(gq, gk, gv, ga, awh, awl, g_alpha_b[0][:, None], bsz, t, ctx_len)
    ys_f, ys_b = _ssd(sx, sb, sc, dts, dts_t, bsz, t, ctx_len)
    dskip = jnp.repeat(s_D[0], S_P)[None]
    x2 = _out_proj(_out_cd_kernel, "out_proj_cd", [og_f, og_b, grs, ys_f, ys_b, sx, zs],
                   [g_norm_w[0][None], dskip, s_norm_w[0][None]],
                   x2, modl, modc, g[1:2], cd_out_w[0].astype(BF16), bsz, t, ctx_len)

    rw = jnp.concatenate([router_w[0], jnp.zeros((dm, LANES - N_EXPERTS), F32)], axis=1)
    rwh = rw.astype(BF16)
    rw = jnp.concatenate([rwh, (rw - rwh.astype(F32)).astype(BF16)], axis=1)
    rb = jnp.concatenate([router_b[0][None], jnp.zeros((1, LANES - N_EXPERTS), F32)], axis=1)
    h2, top_idx, top_gate, counts = _router(x2, modl, g[2:3], rw, rb, bsz, t, ctx_len)
    slot, block_e, n_used, n_valid, n_blocks = _routing_tables(top_idx[:, 0:TOP_K], top_idx[:, TOP_K:2 * TOP_K],
                                                               counts[0, :N_EXPERTS].astype(I32))
    xs = _dispatch_rows(h2, slot.reshape(-1), n_blocks * MOE_ROWS)
    ys_e = _moe_experts(xs, moe_w1[0], moe_w2[0], block_e, n_used, n_valid, n_blocks)
    yk = _gather_rows(ys_e, jnp.concatenate([slot[:, 0], slot[:, 1]]))
    out = _combine(yk, top_gate, x2, modl, g[3:4], bsz, t, ctx_len)
    return out.reshape(bsz, seq, dm)
```

```python
import functools

import jax
import jax.numpy as jnp
from jax import lax
from jax.experimental import pallas as pl
from jax.experimental.pallas import tpu as pltpu
from jax.experimental.pallas import tpu_sc as plsc

F32, BF16, I32 = jnp.float32, jnp.bfloat16, jnp.int32

D_MODEL = 1024
GRID_W = 64
EPS = 1e-6
LOG2E = 1.4426950408889634
M_HEADS, M_DK, M_DV = 4, 128, 256
LRU_BLOCKS, LRU_BW, LRU_C = 8, 128, 8.0
G_HEADS, G_DK, G_DV, G_RANK, G_TAU = 4, 128, 256, 16, 16.0
S_HEADS, S_P, S_N, S_GROUPS, S_HPG = 16, 64, 128, 2, 8
FF_DENSE = 2816
N_EXPERTS, TOP_K, FF_EXPERT = 8, 2, 3584

LANES, SUBLANES = 128, 8
VMEM_LIMIT = 56 * 1024 * 1024
ROW_TILE = 256
LATENT_TILE = 512
BIG_ROW_TILE_MAX = 640
PROJ_COLS = 256
M_CHUNK = 256
G_CHUNK, G_SUB = 256, 16
G_GROUP = 64
S_CHUNK = 256
LRU_TILE = 256
FF_STEP = 256
MOE_ROWS = 1024
MOE_FF = 512
SC_CORES, SC_SUBCORES = 2, 16
SC_LANES = 16
SC_GATHER_ROWS = 128


def _cparams(*sem):
    return pltpu.CompilerParams(dimension_semantics=sem, vmem_limit_bytes=VMEM_LIMIT)


def _sigmoid(x):
    return 0.5 + 0.5 * jnp.tanh(0.5 * x)


def _silu(x):
    return x * _sigmoid(x)


def _softplus(x):
    return jnp.maximum(x, 0.0) + jnp.log1p(jnp.exp(-jnp.abs(x)))


def _log_sigmoid(x):
    return jnp.minimum(x, 0.0) - jnp.log1p(jnp.exp(-jnp.abs(x)))


def _dot(a, b):
    return jnp.dot(a, b, preferred_element_type=F32)


def _dot_nt(a, b):
    return lax.dot_general(a, b, (((1,), (1,)), ((), ())), preferred_element_type=F32)


def _dot_tn(a, b):
    return lax.dot_general(a, b, (((0,), (0,)), ((), ())), preferred_element_type=F32)


def _split3(f):
    f1 = f.astype(BF16)
    r = f - f1.astype(F32)
    f2 = r.astype(BF16)
    f3 = (r - f2.astype(F32)).astype(BF16)
    return f1, f2, f3


def _sel_cols(mask01, f):
    p1, p2, p3 = _split3(f)
    return _dot(mask01, p1) + _dot(mask01, p2) + _dot(mask01, p3)


def _sel_rows(f, mask01):
    p1, p2, p3 = _split3(f)
    return _dot_nt(p1, mask01) + _dot_nt(p2, mask01) + _dot_nt(p3, mask01)


def _pack_rows(u):
    n = u.shape[1] // 2
    bits = lax.bitcast_convert_type(u.astype(BF16).astype(F32), jnp.uint32)
    return jnp.bitwise_or(bits[:, :n], jnp.right_shift(bits[:, n:], jnp.uint32(16)))


def _unpack_rows(p):
    hi = lax.bitcast_convert_type(jnp.bitwise_and(p, jnp.uint32(0xFFFF0000)), F32)
    lo = lax.bitcast_convert_type(jnp.left_shift(p, jnp.uint32(16)), F32)
    return jnp.concatenate([hi, lo], axis=1)


def _rms(u, g):
    return u * lax.rsqrt(jnp.mean(u * u, axis=-1, keepdims=True) + EPS) * g


def _head_rms(u, g, n_heads):
    w = u.shape[-1] // n_heads
    return jnp.concatenate([_rms(u[:, h * w:(h + 1) * w], g[:, h * w:(h + 1) * w]) for h in range(n_heads)], axis=1)


def _row_ids(tile_idx, tm):
    return tile_idx * tm + lax.broadcasted_iota(I32, (tm, 1), 0)


def _mod(modl_ref, modc_ref, idx, is_ctx):
    return jnp.where(is_ctx, modc_ref[idx:idx + 1, :], modl_ref[0, idx:idx + 1, :])


def _adaln(x, g, modl_ref, modc_ref, shift_idx, is_ctx):
    shift = _mod(modl_ref, modc_ref, shift_idx, is_ctx)
    scale = _mod(modl_ref, modc_ref, shift_idx + 1, is_ctx)
    return _rms(x, g) * (1.0 + scale) + shift


def _neighbour_tiles(r, step, is_lat):
    n_tiles, per_seg = r.shape[0], GRID_W // SUBLANES
    zero = jnp.zeros((1,) + r.shape[1:], r.dtype)

    def across(src):
        return jnp.where(is_lat, 0.0, r[src:src + 1]) if 0 <= src < n_tiles else zero

    pieces = []
    for s0 in range(0, n_tiles, per_seg):
        if step > 0:
            pieces += [across(s0 - 1), r[s0:s0 + per_seg - 1]]
        else:
            pieces += [r[s0 + 1:s0 + per_seg], across(s0 + per_seg)]
    return jnp.concatenate(pieces, axis=0)


def _dwconv(y, cw, cb, is_lat):
    tm, n = y.shape
    y3 = y.reshape(tm // SUBLANES, SUBLANES, n)
    sub = lax.broadcasted_iota(I32, (1, SUBLANES, 1), 1)
    w = lambda j: cw[j:j + 1, :].reshape(1, 1, n)
    r1, r2, r7 = pltpu.roll(y3, 1, 1), pltpu.roll(y3, 2, 1), pltpu.roll(y3, SUBLANES - 1, 1)
    ym1 = jnp.where(sub >= 1, r1, _neighbour_tiles(r1, 1, is_lat))
    ym2 = jnp.where(sub >= 2, r2, _neighbour_tiles(r2, 1, is_lat))
    yp1 = jnp.where(sub <= SUBLANES - 2, r7, _neighbour_tiles(r7, -1, is_lat))
    out = cb.reshape(1, 1, n) + w(0) * ym2 + w(1) * ym1 + w(2) * y3 + w(3) * yp1
    return out.reshape(tm, n)


def _resident(shape):
    nd = len(shape)
    return pl.BlockSpec(shape, lambda *_: (0,) * nd, pipeline_mode=pl.Buffered(1))


def _big_row_tile(t):
    return max(tm for tm in range(8, BIG_ROW_TILE_MAX + 1, 8) if t % tm == 0)


def _mod_kernel(c_ref, w_ref, b_ref, o_ref):
    a1, a2, _ = _split3(_silu(c_ref[...]))
    w = w_ref[0]
    wh = w.astype(BF16)
    wl = (w - wh.astype(F32)).astype(BF16)
    o_ref[0] = _dot(a1, wh) + _dot(a2, wh) + _dot(a1, wl) + b_ref[0]


def _modulation(c_all, mod_w, mod_b):
    depth, d, n6 = mod_w.shape
    rows = c_all.shape[0]
    tn = 1024
    return pl.pallas_call(
        _mod_kernel,
        grid=(depth, n6 // tn),
        in_specs=[pl.BlockSpec((rows, d), lambda l, j: (0, 0)),
                  pl.BlockSpec((1, d, tn), lambda l, j: (l, 0, j)),
                  pl.BlockSpec((1, 1, tn), lambda l, j: (l, 0, j))],
        out_specs=pl.BlockSpec((1, rows, tn), lambda l, j: (l, 0, j)),
        out_shape=jax.ShapeDtypeStruct((depth, rows, n6), F32),
        compiler_params=_cparams("arbitrary", "arbitrary"),
        name="modulation",
    )(c_all, mod_w, mod_b.reshape(depth, 1, n6))


def _project(h_ref, w_ref, w_col, out_ref, epilogue):
    for c in range(0, out_ref.shape[1], PROJ_COLS):
        y = _dot(h_ref[...], w_ref[:, w_col + c:w_col + c + PROJ_COLS])
        out_ref[:, c:c + PROJ_COLS] = epilogue(y, c).astype(out_ref.dtype)


def _in_ab_kernel(tm, ctx_len, xc_ref, xl_ref, modl_ref, modc_ref, g_ref, w_ref, mcw_ref, mcb_ref, lcw_ref,
                  lcb_ref, gb_ref, x_ref, q_ref, k_ref, v_ref, og_ref, lx_ref, glg_ref, gates_ref, gates_t_ref,
                  h_scr):
    is_lat = pl.program_id(1) * tm >= ctx_len
    x_ref[...] = jnp.where(is_lat, xl_ref[...], xc_ref[...])
    h_scr[...] = _adaln(x_ref[...], g_ref[...], modl_ref, modc_ref, 0, jnp.logical_not(is_lat)).astype(BF16)

    def conv(cw_ref, cb_ref):
        return lambda y, c: _dwconv(y, cw_ref[:, c:c + PROJ_COLS], cb_ref[:, c:c + PROJ_COLS], is_lat)

    qc, kc, lc = conv(mcw_ref, mcb_ref), conv(mcw_ref, mcb_ref), conv(lcw_ref, lcb_ref)
    _project(h_scr, w_ref, 0, q_ref, lambda y, c: _silu(qc(y, c)))
    _project(h_scr, w_ref, 512, k_ref, lambda y, c: _silu(kc(y, 512 + c)) * (M_DK ** -0.5))
    _project(h_scr, w_ref, 1024, v_ref, lambda y, c: y)
    _project(h_scr, w_ref, 2048, og_ref, lambda y, c: _sigmoid(y))
    _project(h_scr, w_ref, 3072, lx_ref, lc)
    _project(h_scr, w_ref, 4096, glg_ref, lambda y, c: jax.nn.gelu(y))
    gt = _dot(h_scr[...], w_ref[:, 5120:5248]) + gb_ref[...]
    lane = lax.broadcasted_iota(I32, gt.shape, 1)
    gates = jnp.where(lane >= 2 * M_HEADS, _log_sigmoid(gt), gt)
    gates_ref[...] = gates
    gates_t_ref[...] = gates.T


def _in_ab(xc2, xl2, modl, modc, g, w, mcw, mcb, lcw, lcb, gb, bsz, t, ctx_len):
    tm = ROW_TILE
    nt, nct = t // tm, ctx_len // tm
    n = bsz * t
    row = lambda c: pl.BlockSpec((tm, c), lambda b, i: (b * nt + i, 0))
    outs = [(D_MODEL, F32), (512, BF16), (512, BF16), (1024, BF16), (1024, BF16), (1024, BF16), (1024, BF16),
            (LANES, F32)]
    return pl.pallas_call(
        functools.partial(_in_ab_kernel, tm, ctx_len),
        grid=(bsz, nt),
        in_specs=[pl.BlockSpec((tm, D_MODEL), lambda b, i: (b * nct + jnp.minimum(i, nct - 1), 0)),
                  pl.BlockSpec((tm, D_MODEL), lambda b, i: (b * (nt - nct) + jnp.maximum(i - nct, 0), 0)),
                  pl.BlockSpec((1, 8, D_MODEL), lambda b, i: (b, 0, 0)),
                  _resident((8, D_MODEL)), _resident((1, D_MODEL)), _resident(w.shape),
                  _resident(mcw.shape), _resident(mcb.shape), _resident(lcw.shape), _resident(lcb.shape),
                  _resident(gb.shape)],
        out_specs=[row(c) for c, _ in outs] + [pl.BlockSpec((LANES, tm), lambda b, i: (0, b * nt + i))],
        out_shape=[jax.ShapeDtypeStruct((n, c), dt) for c, dt in outs] + [jax.ShapeDtypeStruct((LANES, n), F32)],
        scratch_shapes=[pltpu.VMEM((tm, D_MODEL), BF16)],
        compiler_params=_cparams("arbitrary", "arbitrary"),
        name="in_proj_ab",
    )(xc2, xl2, modl, modc, g, w, mcw, mcb, lcw, lcb, gb)


def _reverse_order(n_ctx_chunks, n_chunks):
    return lambda c: jnp.where(c < n_ctx_chunks, n_ctx_chunks - 1 - c, n_chunks - 1 - (c - n_ctx_chunks))


def _tri_mask(L, reverse):
    row = lax.broadcasted_iota(I32, (L, L), 0)
    col = lax.broadcasted_iota(I32, (L, L), 1)
    return (col >= row) if reverse else (col <= row)


def _bidir_scan(kern, L, row_ins, lane_ins, res_ins, out_cols, scratch, bsz, t, ctx_len, name):
    nc = t // L
    rev = _reverse_order(ctx_len // L, nc)
    fwd = lambda i: i
    n = bsz * t

    def row(c, order):
        return pl.BlockSpec((L, c), lambda b, i: (b * nc + order(i), 0))

    def lanes(r, order):
        return pl.BlockSpec((r, L), lambda b, i: (0, b * nc + order(i)))

    in_specs, args = [], []
    for order in (fwd, rev):
        in_specs += [row(a.shape[1], order) for a in row_ins] + [lanes(r, order) for _, r in lane_ins]
        args += list(row_ins) + [a for a, _ in lane_ins]
    in_specs += [_resident(a.shape) for a in res_ins]
    args += list(res_ins)
    return pl.pallas_call(
        kern,
        grid=(bsz, nc),
        in_specs=in_specs,
        out_specs=[row(c, fwd) for c in out_cols] + [row(c, rev) for c in out_cols],
        out_shape=[jax.ShapeDtypeStruct((n, c), BF16) for c in out_cols] * 2,
        scratch_shapes=scratch,
        compiler_params=_cparams("arbitrary", "arbitrary"),
        name=name,
    )(*args)


def _zero_at_start(*scratch):
    @pl.when(pl.program_id(1) == 0)
    def _():
        for s in scratch:
            s[...] = jnp.zeros_like(s)


def _mlstm_kernel(L, qf, kf, vf, gcf, grf, qb, kb, vb, gcb, grb, of, ob, c_scr, m_scr):
    _zero_at_start(c_scr, m_scr)
    ones = jnp.ones((L, LANES), BF16)
    wide = lambda c, k: jnp.concatenate([c] * k, axis=1)
    er = lax.broadcasted_iota(I32, (LANES, 2 * M_HEADS * LANES), 0)
    ej = jnp.right_shift(lax.broadcasted_iota(I32, (LANES, 2 * M_HEADS * LANES), 1), LANES.bit_length() - 1)
    chains = []
    for d, (q_ref, k_ref, v_ref, gc_ref, gr_ref, out_ref) in enumerate(((qf, kf, vf, gcf, grf, of),
                                                                         (qb, kb, vb, gcb, grb, ob))):
        reverse = d == 1
        mask = _tri_mask(L, reverse)
        mask01 = mask.astype(BF16)
        gc = gc_ref[...]
        gr = gr_ref[...]
        bcol = _sel_cols(mask01, gc)
        brow = _sel_rows(gr, mask01)
        glane = lax.broadcasted_iota(I32, (L, LANES), 1)
        src = jnp.where(ej < M_HEADS, M_HEADS * d + ej, M_HEADS + M_HEADS * d + ej)
        s1, s2, _ = _split3(jnp.where(glane < 2 * M_HEADS, gc, bcol))
        sel = (er == src).astype(BF16)
        rep = _dot(s1, sel) + _dot(s2, sel)
        last = 0 if reverse else L - 1
        for h in range(M_HEADS):
            fo, io = 2 * M_HEADS + M_HEADS * d + h, M_HEADS * d + h
            ic, bc = rep[:, h * LANES:(h + 1) * LANES], rep[:, (M_HEADS + h) * LANES:(M_HEADS + h + 1) * LANES]
            r1 = gr[io:io + 1, :] - brow[fo:fo + 1, :]
            m = m_scr[d, h:h + 1, :]
            g = bc + m
            mx = jnp.max(jnp.where(mask, r1, -jnp.inf), axis=1, keepdims=True)
            mt = jnp.maximum(g, bc + mx)
            p = jnp.exp(jnp.where(mask, wide(bc - mt, L // LANES) + r1, -jnp.inf))
            bl = bc[last:last + 1, :]
            a_s = bl - bc + ic
            m_new = jnp.maximum(bl + m, jnp.max(a_s, axis=0, keepdims=True))
            chains.append(dict(
                d=d, h=h, out_ref=out_ref, mt=mt, w_inter=jnp.exp(g - mt), p=p,
                m_new=m_new, decay=jnp.exp(bl + m - m_new), ws=jnp.exp(a_s - m_new),
                qh=q_ref[:, h * M_DK:(h + 1) * M_DK], kh=k_ref[:, h * M_DK:(h + 1) * M_DK],
                vx=jnp.concatenate([v_ref[:, h * M_DV:(h + 1) * M_DV], ones], axis=1)))
    for ch in chains:
        ch["sc"] = (_dot_nt(ch["qh"], ch["kh"]) * ch["p"]).astype(BF16)
    for ch in chains:
        cx = c_scr[ch["d"], ch["h"]]
        numx = wide(ch["w_inter"], 1 + M_DV // LANES) * _dot(ch["qh"], cx.astype(BF16)) + _dot(ch["sc"], ch["vx"])
        den = numx[:, M_DV:]
        rinv = 1.0 / jnp.maximum(jnp.abs(den), jnp.exp(-ch["mt"]))
        hh = numx[:, 0:M_DV] * wide(rinv, M_DV // LANES)
        ch["out_ref"][:, ch["h"] * M_DV:(ch["h"] + 1) * M_DV] = hh.astype(BF16)
    for ch in chains:
        d, h = ch["d"], ch["h"]
        kw = (ch["kh"].astype(F32) * ch["ws"]).astype(BF16)
        c_scr[d, h] = wide(ch["decay"], 1 + M_DV // LANES) * c_scr[d, h] + _dot_tn(kw, ch["vx"])
        m_scr[d, h:h + 1, :] = ch["m_new"]


def _mlstm(q, k, v, gcol, grow, bsz, t, ctx_len):
    L = M_CHUNK
    scratch = [pltpu.VMEM((2, M_HEADS, M_DK, M_DV + LANES), F32), pltpu.VMEM((2, 8, LANES), F32)]
    return _bidir_scan(functools.partial(_mlstm_kernel, L), L, [q, k, v, gcol], [(grow, 16)], [], [1024], scratch,
                       bsz, t, ctx_len, "mlstm_scan")


def _lru_kernel(L, uf, ub_, wa_ref, wx_ref, ba_ref, bx_ref, lam_ref, of, ob, a_scr, b_scr, h_scr):
    _zero_at_start(h_scr)
    R, C = L // SUBLANES, LRU_BLOCKS * LRU_BW
    i0, i1 = lax.broadcasted_iota(I32, (L, L), 0), lax.broadcasted_iota(I32, (L, L), 1)
    when = lambda p: jnp.bitwise_and(p, SUBLANES - 1) * R + jnp.right_shift(p, SUBLANES.bit_length() - 1)
    regroup = (i1 == when(i0)).astype(BF16)
    ungroup = (i0 == when(i1)).astype(BF16)
    sub = lax.broadcasted_iota(I32, (SUBLANES, 1), 0)
    dirs = ((uf, of), (ub_, ob))
    for d, (u_ref, _) in enumerate(dirs):
        u = _dot(regroup, u_ref[...])
        ub = u.astype(BF16)
        sp = -LRU_C * _softplus(-lam_ref[d])
        for n in range(LRU_BLOCKS):
            sl = slice(n * LRU_BW, (n + 1) * LRU_BW)
            rg = _sigmoid(_dot(ub[:, sl], wa_ref[d, n]) + ba_ref[d, :, sl])
            ig = _sigmoid(_dot(ub[:, sl], wx_ref[d, n]) + bx_ref[d, :, sl])
            log_a = rg * sp[:, sl]
            a = jnp.exp(log_a)
            a_scr[d, :, sl] = a
            b_scr[d, :, sl] = jnp.sqrt(-jnp.tanh(log_a) * (a * a + 1.0)) * (ig * u[:, sl])
    S = SUBLANES
    hs = [jnp.zeros((S, C), F32), jnp.zeros((S, C), F32)]
    ps = [jnp.ones((S, C), F32), jnp.ones((S, C), F32)]
    for j in range(R):
        for d in range(2):
            rows = slice(S * j, S * j + S) if d == 0 else slice(S * (R - 1 - j), S * (R - j))
            a = a_scr[d, rows, :]
            hs[d] = a * hs[d] + b_scr[d, rows, :]
            ps[d] = a * ps[d]
            b_scr[d, rows, :] = hs[d]
            a_scr[d, rows, :] = ps[d]
    for d, (_, out_ref) in enumerate(dirs):
        c = h_scr[d:d + 1, :]
        cin = jnp.zeros((S, C), F32)
        for s in (range(S) if d == 0 else reversed(range(S))):
            cin = jnp.where(sub == s, c, cin)
            c = ps[d][s:s + 1, :] * c + hs[d][s:s + 1, :]
        h_scr[d:d + 1, :] = c
        h = b_scr[d].reshape(R, S, C) + a_scr[d].reshape(R, S, C) * cin[None]
        out_ref[...] = _dot(ungroup, h.reshape(L, C).astype(BF16)).astype(BF16)


def _lru(u, wa, wx, ba, bx, lam, bsz, t, ctx_len):
    L = LRU_TILE
    scratch = [pltpu.VMEM((2, L, 1024), F32), pltpu.VMEM((2, L, 1024), F32), pltpu.VMEM((8, 1024), F32)]
    return _bidir_scan(functools.partial(_lru_kernel, L), L, [u], [], [wa, wx, ba, bx, lam], [1024], scratch,
                       bsz, t, ctx_len, "rglru_scan")


def _swiglu_rows(x, is_ctx, modl_ref, modc_ref, g2_ref, g3_ref, w1_ref, w2_ref):
    h = _adaln(x, g2_ref[...], modl_ref, modc_ref, 3, is_ctx).astype(BF16)
    ff = w2_ref.shape[0]
    acc = jnp.zeros(x.shape, F32)
    for j in range(ff // FF_STEP):
        lo = j * FF_STEP
        gj = _dot(h, w1_ref[:, lo:lo + FF_STEP])
        uj = _dot(h, w1_ref[:, ff + lo:ff + lo + FF_STEP])
        acc = acc + _dot((_silu(gj) * uj).astype(BF16), w2_ref[lo:lo + FF_STEP, :])
    return x + _mod(modl_ref, modc_ref, 5, is_ctx) * _rms(acc, g3_ref[...])


def _out_tail(tm, ctx_len, a1, a2, x_ref, modl_ref, modc_ref, g_ref, w_ref, xo_ref, ffn_refs=None):
    is_ctx = _row_ids(pl.program_id(1), tm) < ctx_len
    half = w_ref.shape[0] // 2
    y = _dot(a1.astype(BF16), w_ref[0:half, :]) + _dot(a2.astype(BF16), w_ref[half:, :])
    x = x_ref[...] + _mod(modl_ref, modc_ref, 2, is_ctx) * _rms(y, g_ref[...])
    if ffn_refs is not None:
        x = _swiglu_rows(x, is_ctx, modl_ref, modc_ref, *ffn_refs)
    xo_ref[...] = x


def _out_ab_kernel(tm, ctx_len, hf, hb, og, lf, lb, glg, nw, x_ref, modl_ref, modc_ref, g_ref, w_ref,
                   g2_ref, g3_ref, w1_ref, w2_ref, xo_ref):
    hm = _head_rms(hf[...].astype(F32) + hb[...].astype(F32), nw[...], M_HEADS) * og[...].astype(F32)
    hl = (lf[...].astype(F32) + lb[...].astype(F32)) * glg[...].astype(F32)
    _out_tail(tm, ctx_len, hm, hl, x_ref, modl_ref, modc_ref, g_ref, w_ref, xo_ref,
              (g2_ref, g3_ref, w1_ref, w2_ref))


def _out_cd_kernel(tm, ctx_len, gf, gb, grs, yf, yb, sx, zs, gnw, dsk, snw, x_ref, modl_ref, modc_ref, g_ref,
                   w_ref, xo_ref):
    og = _head_rms(gf[...].astype(F32) + gb[...].astype(F32), gnw[...], G_HEADS) * grs[...].astype(F32)
    ys = yf[...].astype(F32) + yb[...].astype(F32) + dsk[...] * sx[...].astype(F32)
    ys = _rms(ys * zs[...].astype(F32), snw[...])
    _out_tail(tm, ctx_len, og, ys, x_ref, modl_ref, modc_ref, g_ref, w_ref, xo_ref)


def _out_proj(kern, name, acts, vecs, x2, modl, modc, g, w, bsz, t, ctx_len, extra=()):
    tm = _big_row_tile(t)
    nt = t // tm
    row = lambda c: pl.BlockSpec((tm, c), lambda b, i: (b * nt + i, 0))
    return pl.pallas_call(
        functools.partial(kern, tm, ctx_len),
        grid=(bsz, nt),
        in_specs=[row(1024)] * len(acts) + [_resident((1, 1024))] * len(vecs)
        + [row(D_MODEL), pl.BlockSpec((1, 8, D_MODEL), lambda b, i: (b, 0, 0)),
           _resident((8, D_MODEL)), _resident((1, D_MODEL)), _resident(w.shape)]
        + [_resident(a.shape) for a in extra],
        out_specs=row(D_MODEL),
        out_shape=jax.ShapeDtypeStruct(x2.shape, F32),
        compiler_params=_cparams("arbitrary", "arbitrary"),
        name=name,
    )(*acts, *vecs, x2, modl, modc, g, w, *extra)


def _in_cd_kernel(tm, ctx_len, x_ref, modl_ref, modc_ref, g_ref, w_ref, cw_ref, cb_ref, dtb_ref, aneg_ref,
                  gq_ref, gk_ref, gv_ref, gr_ref, z_ref, sx_ref, sb_ref, sc_ref, ga_ref, dts_ref, dts_t_ref, h_scr):
    is_lat = pl.program_id(1) * tm >= ctx_len
    h_scr[...] = _adaln(x_ref[...], g_ref[...], modl_ref, modc_ref, 0, jnp.logical_not(is_lat)).astype(BF16)

    def conv_silu(col0):
        return lambda y, c: _silu(_dwconv(y, cw_ref[:, col0 + c:col0 + c + PROJ_COLS],
                                          cb_ref[:, col0 + c:col0 + c + PROJ_COLS], is_lat))

    _project(h_scr, w_ref, 0, gq_ref, lambda y, c: y * (G_DK ** -0.5))
    _project(h_scr, w_ref, 512, gk_ref, lambda y, c: y)
    _project(h_scr, w_ref, 1024, gv_ref, lambda y, c: y)
    _project(h_scr, w_ref, 2048, gr_ref, lambda y, c: _silu(y))
    _project(h_scr, w_ref, 3072, z_ref, lambda y, c: _silu(y))
    _project(h_scr, w_ref, 4096, sx_ref, conv_silu(0))
    _project(h_scr, w_ref, 5120, sb_ref, conv_silu(1024))
    _project(h_scr, w_ref, 5376, sc_ref, conv_silu(1280))
    ga_ref[...] = _dot(h_scr[...], w_ref[:, 5632:5760])
    dt = _softplus(_dot(h_scr[...], w_ref[:, 5760:5888]) + dtb_ref[...])
    lane = lax.broadcasted_iota(I32, dt.shape, 1)
    dts = jnp.where(lane < 2 * S_HEADS, dt, dt * aneg_ref[...])
    dts_ref[...] = dts
    dts_t_ref[...] = dts.T


def _in_cd(x2, modl, modc, g, w, cw, cb, dtb, aneg, bsz, t, ctx_len):
    tm = ROW_TILE
    nt = t // tm
    n = bsz * t
    row = lambda c: pl.BlockSpec((tm, c), lambda b, i: (b * nt + i, 0))
    outs = [(512, BF16), (512, BF16), (1024, BF16), (1024, BF16), (1024, BF16), (1024, BF16), (256, BF16),
            (256, BF16), (LANES, F32), (LANES, F32)]
    return pl.pallas_call(
        functools.partial(_in_cd_kernel, tm, ctx_len),
        grid=(bsz, nt),
        in_specs=[row(D_MODEL), pl.BlockSpec((1, 8, D_MODEL), lambda b, i: (b, 0, 0)), _resident((8, D_MODEL)),
                  _resident((1, D_MODEL)), _resident(w.shape), _resident(cw.shape), _resident(cb.shape),
                  _resident(dtb.shape), _resident(aneg.shape)],
        out_specs=[row(c) for c, _ in outs] + [pl.BlockSpec((LANES, tm), lambda b, i: (0, b * nt + i))],
        out_shape=[jax.ShapeDtypeStruct((n, c), dt) for c, dt in outs] + [jax.ShapeDtypeStruct((LANES, n), F32)],
        scratch_shapes=[pltpu.VMEM((tm, D_MODEL), BF16)],
        compiler_params=_cparams("arbitrary", "arbitrary"),
        name="in_proj_cd",
    )(x2, modl, modc, g, w, cw, cb, dtb, aneg)


def _gla_kernel(L, C, qf, kf, vf, gaf, qb, kb, vb, gab, awh_ref, awl_ref, ab_ref, of, ob, s_scr, att_scr):
    _zero_at_start(s_scr, att_scr)
    nb = L // C
    chains = []
    for d, (q_ref, k_ref, v_ref, ga_ref, out_ref) in enumerate(((qf, kf, vf, gaf, of), (qb, kb, vb, gab, ob))):
        reverse = d == 1
        mask = _tri_mask(L, reverse)
        g1, g2, _ = _split3(ga_ref[...])
        pre = _dot(g1, awh_ref[d]) + _dot(g2, awh_ref[d]) + _dot(g1, awl_ref[d])
        lg = _log_sigmoid(pre + ab_ref[d]) * (1.0 / G_TAU)
        ball = _sel_cols(mask.astype(BF16), lg)
        for h in range(G_HEADS):
            ks = slice(h * G_DK, (h + 1) * G_DK)
            b = ball[:, ks] * LOG2E
            qh = q_ref[:, ks].astype(F32)
            inter = _dot_nt((qh * jnp.exp2(b)).astype(BF16), s_scr[d, h].astype(BF16))
            chains.append(dict(d=d, h=h, reverse=reverse, mask=mask, b=b, qh=qh, kh=k_ref[:, ks].astype(F32),
                               vh=v_ref[:, h * G_DV:(h + 1) * G_DV], inter=inter, out_ref=out_ref, blocks=[]))
    T = min(G_GROUP, L)
    for i in range(nb):
        lo, hi = i * C, (i + 1) * C
        t0 = lo // T * T
        for c, ch in enumerate(chains):
            b = ch["b"]
            if ch["reverse"]:
                bref = b[hi:hi + 1, :] if i < nb - 1 else jnp.zeros((1, G_DK), F32)
            else:
                bref = b[lo - 1:lo, :] if i > 0 else jnp.zeros((1, G_DK), F32)
            qi = (ch["qh"][lo:hi, :] * jnp.exp2(b[lo:hi, :] - bref)).astype(BF16)
            ki = (ch["kh"][t0:t0 + T] * jnp.exp2(bref - b[t0:t0 + T])).astype(BF16)
            att_scr[c, lo:hi, t0:t0 + T] = _dot_nt(qi, ki)
    for g in range(L // T):
        t0, t1 = g * T, (g + 1) * T
        for c, ch in enumerate(chains):
            b = ch["b"]
            e0, e1 = (t1, L) if ch["reverse"] else (0, t0)
            if e1 > e0:
                bref = b[t1:t1 + 1, :] if ch["reverse"] else b[t0 - 1:t0, :]
                qg = (ch["qh"][t0:t1] * jnp.exp2(b[t0:t1] - bref)).astype(BF16)
                kg = (ch["kh"][e0:e1] * jnp.exp2(bref - b[e0:e1])).astype(BF16)
                att_scr[c, t0:t1, e0:e1] = _dot_nt(qg, kg)
    for c, ch in enumerate(chains):
        att = jnp.where(ch["mask"], att_scr[c], 0.0).astype(BF16)
        oh = ch["inter"] + _dot(att, ch["vh"])
        ch["out_ref"][:, ch["h"] * G_DV:(ch["h"] + 1) * G_DV] = oh.astype(BF16)
    for ch in chains:
        d, h, b = ch["d"], ch["h"], ch["b"]
        last = 0 if ch["reverse"] else L - 1
        bl = b[last:last + 1, :]
        kd = (ch["kh"] * jnp.exp2(bl - b)).astype(BF16)
        s_scr[d, h] = s_scr[d, h] * jnp.exp2(bl) + _dot_tn(ch["vh"], kd)


def _gla(q, k, v, ga, awh, awl, ab, bsz, t, ctx_len):
    L, C = G_CHUNK, G_SUB
    scratch = [pltpu.VMEM((2, G_HEADS, G_DV, G_DK), F32), pltpu.VMEM((2 * G_HEADS, L, L), F32)]
    return _bidir_scan(functools.partial(_gla_kernel, L, C), L, [q, k, v, ga], [], [awh, awl, ab], [1024], scratch,
                       bsz, t, ctx_len, "gla_scan")


def _ssd_kernel(L, xf, bf, cf, gcf, grf, xb, bb, cb_, gcb, grb, of, ob, s_scr):
    _zero_at_start(s_scr)
    er = lax.broadcasted_iota(I32, (LANES, S_HEADS * S_P), 0)
    ec = jnp.right_shift(lax.broadcasted_iota(I32, (LANES, S_HEADS * S_P), 1), S_P.bit_length() - 1)
    glane = lax.broadcasted_iota(I32, (L, LANES), 1)
    lane = lax.broadcasted_iota(I32, (L, 2 * S_P), 1)
    dirs, chains = [], []
    for d, (x_ref, b_ref, c_ref, gc_ref, gr_ref, out_ref) in enumerate(((xf, bf, cf, gcf, grf, of),
                                                                         (xb, bb, cb_, gcb, grb, ob))):
        reverse = d == 1
        mask = _tri_mask(L, reverse)
        mask01 = mask.astype(BF16)
        gc = gc_ref[...]
        gr = gr_ref[...]
        bcol = _sel_cols(mask01, gc)
        brow = _sel_rows(gr, mask01)
        last = 0 if reverse else L - 1
        dt_o, la_o = S_HEADS * d, 2 * S_HEADS + S_HEADS * d
        sel_la = (er == ec + la_o).astype(BF16)
        x = x_ref[...]
        bla = jnp.where((glane >= la_o) & (glane < la_o + S_HEADS), bcol, 0.0)
        dt_at_la = pltpu.roll(gc, 2 * S_HEADS, 1)
        bl = bla[last:last + 1, :]
        ebx = _dot(jnp.exp(bla).astype(BF16), sel_la)
        wx = _dot((jnp.exp(bl - bla) * dt_at_la).astype(BF16), sel_la)
        dirs.append(dict(d=d, mask=mask, gr=gr, x=x, ebx=ebx, last=last, dt_o=dt_o, la_o=la_o, out_ref=out_ref,
                         xw=(x.astype(F32) * wx).astype(BF16),
                         bcol2=bcol * LOG2E, brow2=brow * LOG2E))
        for g in range(S_GROUPS):
            ns = slice(g * S_N, (g + 1) * S_N)
            chains.append(dict(dr=dirs[-1], g=g, bg=b_ref[:, ns], cg=c_ref[:, ns],
                               gs=slice(g * S_HPG * S_P, (g + 1) * S_HPG * S_P)))
    for ch in chains:
        dr = ch["dr"]
        ch["cb"] = _dot_nt(ch["cg"], ch["bg"])
        ch["inter"] = _dot(ch["cg"], s_scr[dr["d"], ch["g"]].astype(BF16)) * dr["ebx"][:, ch["gs"]]
    for ch in chains:
        dr, g = ch["dr"], ch["g"]
        parts = []
        for pr in range(S_HPG // 2):
            e0 = g * S_HPG + 2 * pr
            xp = dr["x"][:, e0 * S_P:(e0 + 2) * S_P]
            res = []
            for e in (e0, e0 + 1):
                col = dr["la_o"] + e
                bc, br = dr["bcol2"][:, col:col + 1], dr["brow2"][col:col + 1, :]
                dtr = dr["gr"][dr["dt_o"] + e:dr["dt_o"] + e + 1, :]
                att = (ch["cb"] * dtr * jnp.exp2(jnp.where(dr["mask"], bc - br, -jnp.inf))).astype(BF16)
                res.append(_dot(att, xp))
            parts.append(jnp.where(lane < S_P, res[0], res[1]))
        dr["out_ref"][:, ch["gs"]] = (jnp.concatenate(parts, axis=1) + ch["inter"]).astype(BF16)
    for ch in chains:
        dr, g, gs = ch["dr"], ch["g"], ch["gs"]
        s_scr[dr["d"], g] = (s_scr[dr["d"], g] * dr["ebx"][dr["last"]:dr["last"] + 1, gs]
                             + _dot_tn(ch["bg"], dr["xw"][:, gs]))


def _ssd(x, bm, cm, gcol, grow, bsz, t, ctx_len):
    L = S_CHUNK
    scratch = [pltpu.VMEM((2, S_GROUPS, S_N, S_HPG * S_P), F32)]
    return _bidir_scan(functools.partial(_ssd_kernel, L), L, [x, bm, cm, gcol], [(grow, 4 * S_HEADS)], [], [1024],
                       scratch,
                       bsz, t, ctx_len, "ssd_scan")


def _router_kernel(x_ref, modl_ref, g_ref, rw_ref, rb_ref, h_ref, idx_ref, gate_ref, cnt_ref, base_scr):
    @pl.when((pl.program_id(0) == 0) & (pl.program_id(1) == 0))
    def _():
        base_scr[...] = jnp.zeros_like(base_scr)

    h = _rms(x_ref[...], g_ref[...]) * (1.0 + modl_ref[0, 4:5, :]) + modl_ref[0, 3:4, :]
    h_ref[...] = _pack_rows(h)
    h1, h2, _ = _split3(h)
    both = _dot(h1, rw_ref[...])
    logits = both[:, :LANES] + both[:, LANES:] + _dot(h2, rw_ref[:, :LANES]) + rb_ref[...]
    lane = lax.broadcasted_iota(I32, logits.shape, 1)
    logits = jnp.where(lane < N_EXPERTS, logits, -jnp.inf)
    lanef = lane.astype(F32)
    m1 = jnp.max(logits, axis=1, keepdims=True)
    i1 = jnp.min(jnp.where(logits == m1, lanef, float(LANES)), axis=1, keepdims=True)
    rest = jnp.where(lanef == i1, -jnp.inf, logits)
    m2 = jnp.max(rest, axis=1, keepdims=True)
    i2 = jnp.min(jnp.where(rest == m2, lanef, float(LANES)), axis=1, keepdims=True)
    e = jnp.exp(m2 - m1)
    g1 = 1.0 / (1.0 + e)
    gate_ref[...] = jnp.where(lane == 0, g1, jnp.where(lane == 1, e * g1, 0.0))
    tm = logits.shape[0]
    oh1, oh2 = (lanef == i1).astype(F32), (lanef == i2).astype(F32)
    earlier = (lax.broadcasted_iota(I32, (tm, tm), 1) < lax.broadcasted_iota(I32, (tm, tm), 0)).astype(BF16)
    seen = base_scr[0:1, :] + _dot(earlier, (oh1 + oh2).astype(BF16))
    r1 = jnp.sum(oh1 * seen, axis=1, keepdims=True)
    r2 = jnp.sum(oh2 * seen, axis=1, keepdims=True)
    base_scr[0:1, :] = base_scr[0:1, :] + jnp.sum(oh1 + oh2, axis=0, keepdims=True)
    cnt_ref[...] = base_scr[...]
    idx_ref[...] = jnp.where(lane == 0, i1, jnp.where(lane == 1, i2, jnp.where(lane == 2, r1, jnp.where(
        lane == 3, r2, 0.0)))).astype(I32)


def _latent_rows(tm, t, ctx_len):
    return pl.BlockSpec((pl.Element(tm), pl.Element(D_MODEL)),
                        lambda b, i: (pl.multiple_of(b * t + ctx_len + i * tm, SUBLANES), 0))


def _router(x2, modl, g, rw, rb, bsz, t, ctx_len):
    tm = LATENT_TILE
    nl = bsz * (t - ctx_len)
    nlt = (t - ctx_len) // tm
    orow = lambda c: pl.BlockSpec((tm, c), lambda b, i: (b * nlt + i, 0))
    return pl.pallas_call(
        _router_kernel,
        grid=(bsz, nlt),
        in_specs=[_latent_rows(tm, t, ctx_len),
                  pl.BlockSpec((1, 8, D_MODEL), lambda b, i: (b, 0, 0)),
                  _resident((1, D_MODEL)), _resident(rw.shape), _resident(rb.shape)],
        out_specs=[orow(D_MODEL // 2), orow(LANES), orow(LANES), pl.BlockSpec((8, LANES), lambda b, i: (0, 0))],
        out_shape=[jax.ShapeDtypeStruct((nl, D_MODEL // 2), jnp.uint32), jax.ShapeDtypeStruct((nl, LANES), I32),
                   jax.ShapeDtypeStruct((nl, LANES), F32), jax.ShapeDtypeStruct((8, LANES), F32)],
        scratch_shapes=[pltpu.VMEM((8, LANES), F32)],
        compiler_params=_cparams("arbitrary", "arbitrary"),
        name="moe_router",
    )(x2, modl, g, rw, rb)


def _gather_rows(table, idx):
    n_rows, d = idx.shape[0], table.shape[1]
    n_workers = SC_CORES * SC_SUBCORES
    per_worker = n_rows // n_workers
    chunk = SC_GATHER_ROWS
    assert per_worker * n_workers == n_rows and per_worker % chunk == 0
    mesh = plsc.VectorSubcoreMesh(core_axis_name="c", subcore_axis_name="s")

    @functools.partial(
        pl.kernel, mesh=mesh, out_type=jax.ShapeDtypeStruct((n_rows, d), table.dtype),
        scratch_types=[pltpu.VMEM((chunk,), I32), pltpu.VMEM((chunk, d), table.dtype), pltpu.SemaphoreType.DMA],
        name="sc_gather_rows")
    def gather(table_hbm, idx_hbm, out_hbm, idx_v, rows_v, sem):
        base = (lax.axis_index("s") * SC_CORES + lax.axis_index("c")) * per_worker

        @pl.loop(0, per_worker // chunk)
        def _(j):
            off = pl.multiple_of(base + j * chunk, 8)
            pltpu.sync_copy(idx_hbm.at[pl.ds(off, chunk)], idx_v)
            pltpu.async_copy(table_hbm.at[idx_v], rows_v, sem).wait()
            pltpu.sync_copy(rows_v, out_hbm.at[pl.ds(off, chunk)])

    return gather(table, idx)


def _dispatch_rows(table, slot, n_out):
    assert TOP_K == 2
    n_assign, (n_tok, d) = slot.shape[0], table.shape
    n_workers = SC_CORES * SC_SUBCORES
    per_worker = n_out // n_workers
    chunk, vec = SC_GATHER_ROWS, SC_LANES
    assert per_worker * n_workers == n_out and per_worker % chunk == 0 and n_assign % vec == 0
    assert n_out < 3 * n_tok
    mesh = plsc.VectorSubcoreMesh(core_axis_name="c", subcore_axis_name="s")

    @functools.partial(
        pl.kernel, mesh=mesh, out_type=jax.ShapeDtypeStruct((n_out, d), table.dtype),
        scratch_types=[pltpu.VMEM((n_assign,), I32), pltpu.VMEM((per_worker,), I32),
                       pltpu.VMEM((chunk, d), table.dtype), pltpu.SemaphoreType.DMA],
        compiler_params=pltpu.CompilerParams(needs_layout_passes=False),
        name="sc_dispatch_rows")
    def dispatch(table_hbm, slot_hbm, out_hbm, slots_v, tok_v, rows_v, sem):
        base = (lax.axis_index("s") * SC_CORES + lax.axis_index("c")) * per_worker
        pltpu.sync_copy(slot_hbm, slots_v)
        lane = lax.broadcasted_iota(I32, (vec,), 0)

        @pl.loop(0, per_worker // vec)
        def _(j):
            p = base + j * vec + lane
            p = jnp.where(p >= n_tok, p - n_tok, p)
            tok_v[pl.ds(j * vec, vec)] = jnp.where(p >= n_tok, p - n_tok, p)

        @pl.loop(0, n_assign // vec)
        def _(j):
            s = slots_v[pl.ds(j * vec, vec)] - base
            mine = (s >= 0) & (s < per_worker)
            tok = lax.shift_right_logical(j * vec + lane, 1)
            plsc.store_scatter(tok_v, [jnp.where(mine, s, 0)], tok, mask=mine)

        @pl.loop(0, per_worker // chunk)
        def _(j):
            off = pl.multiple_of(j * chunk, 8)
            pltpu.async_copy(table_hbm.at[tok_v.at[pl.ds(off, chunk)]], rows_v, sem).wait()
            pltpu.sync_copy(rows_v, out_hbm.at[pl.ds(base + off, chunk)])

    return dispatch(table, slot)


def _moe_kernel(be_ref, nu_ref, nv_ref, x_ref, w1g_ref, w1u_ref, w2_ref, y_ref, xb_ref, acc_ref):
    i, j = pl.program_id(0), pl.program_id(1)
    bm = x_ref.shape[0]

    def swiglu_rows(rows):
        x = xb_ref[0:rows, :]
        a = (_silu(_dot(x, w1g_ref[0].astype(BF16))) * _dot(x, w1u_ref[0].astype(BF16))).astype(BF16)
        acc_ref[0:rows, :] += _dot(a, w2_ref[0].astype(BF16))

    @pl.when(i < nu_ref[0])
    def _():
        @pl.when(j == 0)
        def _():
            acc_ref[...] = jnp.zeros_like(acc_ref)
            xb_ref[...] = _unpack_rows(x_ref[...]).astype(BF16)

        @pl.when(nv_ref[i] > bm // 2)
        def _():
            swiglu_rows(bm)

        @pl.when(nv_ref[i] <= bm // 2)
        def _():
            swiglu_rows(bm // 2)

        @pl.when(j == pl.num_programs(1) - 1)
        def _():
            y_ref[...] = _pack_rows(acc_ref[...])


def _moe_experts(xs, w1, w2, block_e, n_used, n_valid, n_blocks):
    bm, fc = MOE_ROWS, MOE_FF
    nff = FF_EXPERT // fc
    used = lambda i, nu: jnp.minimum(i, nu[0] - 1)
    ffi = lambda i, j, nu: jnp.where(i < nu[0], j, nff - 1)
    grid_spec = pltpu.PrefetchScalarGridSpec(
        num_scalar_prefetch=3,
        grid=(n_blocks, nff),
        in_specs=[pl.BlockSpec((bm, D_MODEL // 2), lambda i, j, be, nu, nv: (used(i, nu), 0)),
                  pl.BlockSpec((1, D_MODEL, fc), lambda i, j, be, nu, nv: (be[used(i, nu)], 0, ffi(i, j, nu))),
                  pl.BlockSpec((1, D_MODEL, fc),
                               lambda i, j, be, nu, nv: (be[used(i, nu)], 0, nff + ffi(i, j, nu))),
                  pl.BlockSpec((1, fc, D_MODEL), lambda i, j, be, nu, nv: (be[used(i, nu)], ffi(i, j, nu), 0))],
        out_specs=pl.BlockSpec((bm, D_MODEL // 2), lambda i, j, be, nu, nv: (used(i, nu), 0)),
        scratch_shapes=[pltpu.VMEM((bm, D_MODEL), BF16), pltpu.VMEM((bm, D_MODEL), F32)],
    )
    return pl.pallas_call(
        _moe_kernel,
        grid_spec=grid_spec,
        out_shape=jax.ShapeDtypeStruct((n_blocks * bm, D_MODEL // 2), jnp.uint32),
        compiler_params=_cparams("arbitrary", "arbitrary"),
        name="moe_experts",
    )(block_e, n_used, n_valid, xs, w1, w1, w2)


def _combine_kernel(y0_ref, y1_ref, gate_ref, x_ref, modl_ref, g_ref, o_ref):
    gt = gate_ref[...]
    y = gt[:, 0:1] * _unpack_rows(y0_ref[...]) + gt[:, 1:2] * _unpack_rows(y1_ref[...])
    o_ref[...] = x_ref[...] + modl_ref[0, 5:6, :] * _rms(y, g_ref[...])


def _combine(yk, gates, x2, modl, g, bsz, t, ctx_len):
    tm = LATENT_TILE
    nlt = (t - ctx_len) // tm
    nl = bsz * (t - ctx_len)
    orow = lambda c: pl.BlockSpec((tm, c), lambda b, i: (b * nlt + i, 0))
    second = pl.BlockSpec((tm, D_MODEL // 2), lambda b, i: (nl // tm + b * nlt + i, 0))
    return pl.pallas_call(
        _combine_kernel,
        grid=(bsz, nlt),
        in_specs=[orow(D_MODEL // 2), second, orow(LANES),
                  _latent_rows(tm, t, ctx_len),
                  pl.BlockSpec((1, 8, D_MODEL), lambda b, i: (b, 0, 0)), _resident((1, D_MODEL))],
        out_specs=orow(D_MODEL),
        out_shape=jax.ShapeDtypeStruct((nl, D_MODEL), F32),
        compiler_params=_cparams("arbitrary", "arbitrary"),
        name="moe_combine",
    )(yk, yk, gates, x2, modl, g)


def _routing_tables(experts, ranks, counts):
    bm = MOE_ROWS
    n_assign = experts.size
    padded = (counts + bm - 1) // bm * bm
    ends_p = jnp.cumsum(padded)
    pstarts = ends_p - padded
    onehot = experts[..., None] == jnp.arange(N_EXPERTS, dtype=I32)
    slot = (jnp.sum(jnp.where(onehot, pstarts, 0), axis=-1) + ranks).astype(I32)
    n_blocks = n_assign // bm + N_EXPERTS
    block_e = jnp.minimum(jnp.sum(ends_p[None, :] <= (jnp.arange(n_blocks, dtype=I32) * bm)[:, None], axis=1),
                          N_EXPERTS - 1).astype(I32)
    n_used = (ends_p[-1] // bm).astype(I32).reshape(1)
    sel = block_e[:, None] == jnp.arange(N_EXPERTS, dtype=I32)
    rows_left = jnp.sum(jnp.where(sel, counts + pstarts, 0), axis=1) - jnp.arange(n_blocks, dtype=I32) * bm
    n_valid = jnp.clip(rows_left, 0, bm).astype(I32)
    return slot, block_e, n_used, n_valid, n_blocks


def kernel(x, c, ctx, c_ctx, mod_w, mod_b, norm_g, ab_in_w, m_conv_w, m_conv_b, m_gate_b, m_norm_w, l_conv_w,
           l_conv_b, l_wa, l_ba, l_wx, l_bx, l_lam, ab_out_w, ffn_w1, ffn_w2, cd_in_w, g_alpha_w, g_alpha_b,
           g_norm_w, s_conv_w, s_conv_b, s_dt_bias, s_A_log, s_D, s_norm_w, cd_out_w, router_w, router_b,
           moe_w1, moe_w2):
    bsz, seq, dm = x.shape
    ctx_len = ctx.shape[1]
    t = ctx_len + seq
    n = bsz * t
    assert ctx_len == ROW_TILE and seq % ROW_TILE == 0 and ROW_TILE % GRID_W == 0

    c_all = jnp.concatenate([c, c_ctx[None, :], jnp.zeros((-(bsz + 1) % 8, dm), F32)], axis=0)
    mods = _modulation(c_all, mod_w, mod_b).reshape(mod_w.shape[0], c_all.shape[0], 6, dm)
    pad2 = jnp.zeros((bsz, 2, dm), F32)

    def layer_mods(layer):
        modl = jnp.concatenate([mods[layer, :bsz], pad2], axis=1)
        modc = jnp.concatenate([mods[layer, bsz], pad2[0]], axis=0)
        return modl, modc

    modl, modc = layer_mods(0)
    g = norm_g[0]
    w = ab_in_w[0]
    o_qk, o_v, o_o, o_gt, o_lx, o_lg = 0, 1024, 2048, 3072, 3088, 4112
    w_in = jnp.concatenate([w[:, o_qk:o_v], w[:, o_v:o_o], w[:, o_o:o_gt], w[:, o_lx:o_lg], w[:, o_lg:],
                            w[:, o_gt:o_lx], jnp.zeros((dm, LANES - 16), F32)], axis=1).astype(BF16)
    gb = jnp.concatenate([m_gate_b[0].reshape(1, 16), jnp.zeros((1, LANES - 16), F32)], axis=1)
    x2, q, k, v, og, lx, glg, gates, gates_t = _in_ab(
        ctx.reshape(bsz * ctx_len, dm), x.reshape(bsz * seq, dm), modl, modc, g[0:1], w_in, m_conv_w[0],
        m_conv_b[0][None], l_conv_w[0], l_conv_b[0][None], gb, bsz, t, ctx_len)
    hm_f, hm_b = _mlstm(q, k, v, gates, gates_t, bsz, t, ctx_len)
    hl_f, hl_b = _lru(lx, l_wa[0].astype(BF16), l_wx[0].astype(BF16), l_ba[0][:, None], l_bx[0][:, None],
                      l_lam[0][:, None], bsz, t, ctx_len)
    x2 = _out_proj(_out_ab_kernel, "out_proj_swiglu_ab", [hm_f, hm_b, og, hl_f, hl_b, glg], [m_norm_w[0][None]],
                   x2, modl, modc, g[1:2], ab_out_w[0].astype(BF16), bsz, t, ctx_len,
                   extra=(g[2:3], g[3:4], ffn_w1[0].astype(BF16), ffn_w2[0].astype(BF16)))

    modl, modc = layer_mods(1)
    g = norm_g[1]
    w = cd_in_w[0]
    o_gv, o_gr, o_ga, o_z, o_xbc, o_dt = 1024, 2048, 3072, 3104, 4128, 5664
    zpad = jnp.zeros((dm, LANES - 32), F32)
    w_in = jnp.concatenate([w[:, :o_gv], w[:, o_gv:o_gr], w[:, o_gr:o_ga], w[:, o_z:o_xbc], w[:, o_xbc:o_dt],
                            w[:, o_ga:o_z], zpad, w[:, o_dt:], w[:, o_dt:], jnp.zeros((dm, LANES - 64), F32)],
                           axis=1).astype(BF16)
    dtb = jnp.concatenate([s_dt_bias[0].reshape(1, 32), s_dt_bias[0].reshape(1, 32),
                           jnp.zeros((1, LANES - 64), F32)], axis=1)
    aneg = jnp.concatenate([jnp.zeros((1, 32), F32), -jnp.exp(s_A_log[0].reshape(1, 32)),
                            jnp.zeros((1, LANES - 64), F32)], axis=1)
    gq, gk, gv, grs, zs, sx, sb, sc, ga, dts, dts_t = _in_cd(x2, modl, modc, g[0:1], w_in, s_conv_w[0],
                                                             s_conv_b[0][None], dtb, aneg, bsz, t, ctx_len)
    aw = jnp.zeros((2, LANES, G_HEADS * G_DK), F32)
    aw = aw.at[0, 0:G_RANK].set(g_alpha_w[0, 0]).at[1, G_RANK:2 * G_RANK].set(g_alpha_w[0, 1])
    awh = aw.astype(BF16)
    awl = (aw - awh.astype(F32)).astype(BF16)
    og_f, og_b = _gla(gq, gk, gv, ga, awh, awl, g_alpha_b[0][:, None], bsz, t, ctx_len)
    ys_f, ys_b = _ssd(sx, sb, sc, dts, dts_t, bsz, t, ctx_len)
    dskip = jnp.repeat(s_D[0], S_P)[None]
    x2 = _out_proj(_out_cd_kernel, "out_proj_cd", [og_f, og_b, grs, ys_f, ys_b, sx, zs],
                   [g_norm_w[0][None], dskip, s_norm_w[0][None]],
                   x2, modl, modc, g[1:2], cd_out_w[0].astype(BF16), bsz, t, ctx_len)

    rw = jnp.concatenate([router_w[0], jnp.zeros((dm, LANES - N_EXPERTS), F32)], axis=1)
    rwh = rw.astype(BF16)
    rw = jnp.concatenate([rwh, (rw - rwh.astype(F32)).astype(BF16)], axis=1)
    rb = jnp.concatenate([router_b[0][None], jnp.zeros((1, LANES - N_EXPERTS), F32)], axis=1)
    h2, top_idx, top_gate, counts = _router(x2, modl, g[2:3], rw, rb, bsz, t, ctx_len)
    slot, block_e, n_used, n_valid, n_blocks = _routing_tables(top_idx[:, 0:TOP_K], top_idx[:, TOP_K:2 * TOP_K],
                                                               counts[0, :N_EXPERTS].astype(I32))
    xs = _dispatch_rows(h2, slot.reshape(-1), n_blocks * MOE_ROWS)
    ys_e = _moe_experts(xs, moe_w1[0], moe_w2[0], block_e, n_used, n_valid, n_blocks)
    yk = _gather_rows(ys_e, jnp.concatenate([slot[:, 0], slot[:, 1]]))
    out = _combine(yk, top_gate, x2, modl, g[3:4], bsz, t, ctx_len)
    return out.reshape(bsz, seq, dm)
```

```python
import functools

import jax
import jax.numpy as jnp
from jax import lax
from jax.experimental import pallas as pl
from jax.experimental.pallas import tpu as pltpu
from jax.experimental.pallas import tpu_sc as plsc

F32, BF16, I32 = jnp.float32, jnp.bfloat16, jnp.int32

D_MODEL = 1024
GRID_W = 64
EPS = 1e-6
LOG2E = 1.4426950408889634
M_HEADS, M_DK, M_DV = 4, 128, 256
LRU_BLOCKS, LRU_BW, LRU_C = 8, 128, 8.0
G_HEADS, G_DK, G_DV, G_RANK, G_TAU = 4, 128, 256, 16, 16.0
S_HEADS, S_P, S_N, S_GROUPS, S_HPG = 16, 64, 128, 2, 8
FF_DENSE = 2816
N_EXPERTS, TOP_K, FF_EXPERT = 8, 2, 3584

LANES, SUBLANES = 128, 8
VMEM_LIMIT = 56 * 1024 * 1024
ROW_TILE = 256
LATENT_TILE = 512
BIG_ROW_TILE_MAX = 640
PROJ_COLS = 256
M_CHUNK = 256
G_CHUNK, G_SUB = 256, 16
G_GROUP = 64
S_CHUNK = 256
LRU_TILE = 256
FF_STEP = 256
MOE_ROWS = 1024
MOE_FF = 512
SC_CORES, SC_SUBCORES = 2, 16
SC_LANES = 16
SC_GATHER_ROWS = 128


def _cparams(*sem):
    return pltpu.CompilerParams(dimension_semantics=sem, vmem_limit_bytes=VMEM_LIMIT)


def _sigmoid(x):
    return 0.5 + 0.5 * jnp.tanh(0.5 * x)


def _silu(x):
    return x * _sigmoid(x)


def _softplus(x):
    return jnp.maximum(x, 0.0) + jnp.log1p(jnp.exp(-jnp.abs(x)))


def _log_sigmoid(x):
    return jnp.minimum(x, 0.0) - jnp.log1p(jnp.exp(-jnp.abs(x)))


def _dot(a, b):
    return jnp.dot(a, b, preferred_element_type=F32)


def _dot_nt(a, b):
    return lax.dot_general(a, b, (((1,), (1,)), ((), ())), preferred_element_type=F32)


def _dot_tn(a, b):
    return lax.dot_general(a, b, (((0,), (0,)), ((), ())), preferred_element_type=F32)


def _split3(f):
    f1 = f.astype(BF16)
    r = f - f1.astype(F32)
    f2 = r.astype(BF16)
    f3 = (r - f2.astype(F32)).astype(BF16)
    return f1, f2, f3


def _sel_cols(mask01, f):
    p1, p2, p3 = _split3(f)
    return _dot(mask01, p1) + _dot(mask01, p2) + _dot(mask01, p3)


def _sel_rows(f, mask01):
    p1, p2, p3 = _split3(f)
    return _dot_nt(p1, mask01) + _dot_nt(p2, mask01) + _dot_nt(p3, mask01)


def _pack_rows(u):
    n = u.shape[1] // 2
    bits = lax.bitcast_convert_type(u.astype(BF16).astype(F32), jnp.uint32)
    return jnp.bitwise_or(bits[:, :n], jnp.right_shift(bits[:, n:], jnp.uint32(16)))


def _unpack_rows(p):
    hi = lax.bitcast_convert_type(jnp.bitwise_and(p, jnp.uint32(0xFFFF0000)), F32)
    lo = lax.bitcast_convert_type(jnp.left_shift(p, jnp.uint32(16)), F32)
    return jnp.concatenate([hi, lo], axis=1)


def _rms(u, g):
    return u * lax.rsqrt(jnp.mean(u * u, axis=-1, keepdims=True) + EPS) * g


def _head_rms(u, g, n_heads):
    w = u.shape[-1] // n_heads
    return jnp.concatenate([_rms(u[:, h * w:(h + 1) * w], g[:, h * w:(h + 1) * w]) for h in range(n_heads)], axis=1)


def _row_ids(tile_idx, tm):
    return tile_idx * tm + lax.broadcasted_iota(I32, (tm, 1), 0)


def _mod(modl_ref, modc_ref, idx, is_ctx):
    if isinstance(is_ctx, bool):
        return modc_ref[idx:idx + 1, :] if is_ctx else modl_ref[0, idx:idx + 1, :]
    return jnp.where(is_ctx, modc_ref[idx:idx + 1, :], modl_ref[0, idx:idx + 1, :])


def _adaln(x, g, modl_ref, modc_ref, shift_idx, is_ctx):
    shift = _mod(modl_ref, modc_ref, shift_idx, is_ctx)
    scale = _mod(modl_ref, modc_ref, shift_idx + 1, is_ctx)
    return _rms(x, g) * (1.0 + scale) + shift


def _neighbour_tiles(r, step, is_lat):
    n_tiles, per_seg = r.shape[0], GRID_W // SUBLANES
    zero = jnp.zeros((1,) + r.shape[1:], r.dtype)

    def across(src):
        return r[src:src + 1] if 0 <= src < n_tiles and not is_lat else zero

    pieces = []
    for s0 in range(0, n_tiles, per_seg):
        if step > 0:
            pieces += [across(s0 - 1), r[s0:s0 + per_seg - 1]]
        else:
            pieces += [r[s0 + 1:s0 + per_seg], across(s0 + per_seg)]
    return jnp.concatenate(pieces, axis=0)


def _dwconv(y, cw, cb, is_lat):
    tm, n = y.shape
    y3 = y.reshape(tm // SUBLANES, SUBLANES, n)
    sub = lax.broadcasted_iota(I32, (1, SUBLANES, 1), 1)
    w = lambda j: cw[j:j + 1, :].reshape(1, 1, n)
    r1, r2, r7 = pltpu.roll(y3, 1, 1), pltpu.roll(y3, 2, 1), pltpu.roll(y3, SUBLANES - 1, 1)
    ym1 = jnp.where(sub >= 1, r1, _neighbour_tiles(r1, 1, is_lat))
    ym2 = jnp.where(sub >= 2, r2, _neighbour_tiles(r2, 1, is_lat))
    yp1 = jnp.where(sub <= SUBLANES - 2, r7, _neighbour_tiles(r7, -1, is_lat))
    out = cb.reshape(1, 1, n) + w(0) * ym2 + w(1) * ym1 + w(2) * y3 + w(3) * yp1
    return out.reshape(tm, n)


def _resident(shape):
    nd = len(shape)
    return pl.BlockSpec(shape, lambda *_: (0,) * nd, pipeline_mode=pl.Buffered(1))


def _big_row_tile(t):
    return max(tm for tm in range(8, BIG_ROW_TILE_MAX + 1, 8) if t % tm == 0)


def _mod_kernel(c_ref, w_ref, b_ref, o_ref):
    a1, a2, _ = _split3(_silu(c_ref[...]))
    w = w_ref[0]
    wh = w.astype(BF16)
    wl = (w - wh.astype(F32)).astype(BF16)
    o_ref[0] = _dot(a1, wh) + _dot(a2, wh) + _dot(a1, wl) + b_ref[0]


def _modulation(c_all, mod_w, mod_b):
    depth, d, n6 = mod_w.shape
    rows = c_all.shape[0]
    tn = 1024
    return pl.pallas_call(
        _mod_kernel,
        grid=(depth, n6 // tn),
        in_specs=[pl.BlockSpec((rows, d), lambda l, j: (0, 0)),
                  pl.BlockSpec((1, d, tn), lambda l, j: (l, 0, j)),
                  pl.BlockSpec((1, 1, tn), lambda l, j: (l, 0, j))],
        out_specs=pl.BlockSpec((1, rows, tn), lambda l, j: (l, 0, j)),
        out_shape=jax.ShapeDtypeStruct((depth, rows, n6), F32),
        compiler_params=_cparams("arbitrary", "arbitrary"),
        name="modulation",
    )(c_all, mod_w, mod_b.reshape(depth, 1, n6))


def _project(h_ref, w_ref, w_col, out_ref, epilogue):
    for c in range(0, out_ref.shape[1], PROJ_COLS):
        y = _dot(h_ref[...], w_ref[:, w_col + c:w_col + c + PROJ_COLS])
        out_ref[:, c:c + PROJ_COLS] = epilogue(y, c).astype(out_ref.dtype)


def _in_ab_kernel(is_lat, n_aliased, xin_ref, modl_ref, modc_ref, g_ref, w_ref, mcw_ref, mcb_ref, lcw_ref,
                  lcb_ref, gb_ref, *refs):
    x_ref, q_ref, k_ref, v_ref, og_ref, lx_ref, glg_ref, gates_ref, gates_t_ref, h_scr = refs[n_aliased:]
    x_ref[...] = xin_ref[...]
    h_scr[...] = _adaln(x_ref[...], g_ref[...], modl_ref, modc_ref, 0, not is_lat).astype(BF16)

    def conv(cw_ref, cb_ref):
        return lambda y, c: _dwconv(y, cw_ref[:, c:c + PROJ_COLS], cb_ref[:, c:c + PROJ_COLS], is_lat)

    qc, kc, lc = conv(mcw_ref, mcb_ref), conv(mcw_ref, mcb_ref), conv(lcw_ref, lcb_ref)
    _project(h_scr, w_ref, 0, q_ref, lambda y, c: _silu(qc(y, c)))
    _project(h_scr, w_ref, 512, k_ref, lambda y, c: _silu(kc(y, 512 + c)) * (M_DK ** -0.5))
    _project(h_scr, w_ref, 1024, v_ref, lambda y, c: y)
    _project(h_scr, w_ref, 2048, og_ref, lambda y, c: _sigmoid(y))
    _project(h_scr, w_ref, 3072, lx_ref, lc)
    _project(h_scr, w_ref, 4096, glg_ref, lambda y, c: jax.nn.gelu(y))
    gt = _dot(h_scr[...], w_ref[:, 5120:5248]) + gb_ref[...]
    lane = lax.broadcasted_iota(I32, gt.shape, 1)
    gates = jnp.where(lane >= 2 * M_HEADS, _log_sigmoid(gt), gt)
    gates_ref[...] = gates
    gates_t_ref[...] = gates.T


def _in_proj_calls(kern, name, x_ctx, x_ctx_spec, x_lat, x_lat_spec, params, outs, bsz, t, ctx_len):
    n = bsz * t
    modl, modc = params[0], params[1]
    shapes = [jax.ShapeDtypeStruct((n, c), dt) for c, dt in outs] + [jax.ShapeDtypeStruct((LANES, n), F32)]
    p_specs = [pl.BlockSpec((1, 8, D_MODEL), lambda b, *_: (b, 0, 0))] + [_resident(a.shape) for a in params[1:]]
    tc, tl = ctx_len, LATENT_TILE
    start = lambda b: b * (t // tc)
    first = pl.pallas_call(
        functools.partial(kern, False, 0),
        grid=(bsz,),
        in_specs=[x_ctx_spec] + p_specs,
        out_specs=[pl.BlockSpec((tc, c), lambda b: (start(b), 0)) for c, _ in outs]
        + [pl.BlockSpec((LANES, tc), lambda b: (0, start(b)))],
        out_shape=shapes,
        scratch_shapes=[pltpu.VMEM((tc, D_MODEL), BF16)],
        compiler_params=_cparams("arbitrary"),
        name=name + "_ctx",
    )(x_ctx, *params)
    row0 = lambda b, i: b * t + ctx_len + i * tl
    n_in = 1 + len(params)
    return pl.pallas_call(
        functools.partial(kern, True, len(first)),
        grid=(bsz, (t - ctx_len) // tl),
        in_specs=[x_lat_spec] + p_specs + [pl.BlockSpec(memory_space=pl.ANY)] * len(first),
        out_specs=[pl.BlockSpec((pl.Element(tl), pl.Element(c)), lambda b, i: (pl.multiple_of(row0(b, i), SUBLANES), 0))
                   for c, _ in outs]
        + [pl.BlockSpec((pl.Element(LANES), pl.Element(tl)), lambda b, i: (0, pl.multiple_of(row0(b, i), LANES)))],
        out_shape=shapes,
        input_output_aliases={n_in + j: j for j in range(len(first))},
        scratch_shapes=[pltpu.VMEM((tl, D_MODEL), BF16)],
        compiler_params=_cparams("arbitrary", "arbitrary"),
        name=name + "_lat",
    )(x_lat, *params, *first)


def _in_ab(xc2, xl2, modl, modc, g, w, mcw, mcb, lcw, lcb, gb, bsz, t, ctx_len):
    outs = [(D_MODEL, F32), (512, BF16), (512, BF16), (1024, BF16), (1024, BF16), (1024, BF16), (1024, BF16),
            (LANES, F32)]
    nlt = (t - ctx_len) // LATENT_TILE
    return _in_proj_calls(
        _in_ab_kernel, "in_proj_ab",
        xc2, pl.BlockSpec((ctx_len, D_MODEL), lambda b: (b, 0)),
        xl2, pl.BlockSpec((LATENT_TILE, D_MODEL), lambda b, i: (b * nlt + i, 0)),
        (modl, modc, g, w, mcw, mcb, lcw, lcb, gb), outs, bsz, t, ctx_len)


def _reverse_order(n_ctx_chunks, n_chunks):
    return lambda c: jnp.where(c < n_ctx_chunks, n_ctx_chunks - 1 - c, n_chunks - 1 - (c - n_ctx_chunks))


def _tri_mask(L, reverse):
    row = lax.broadcasted_iota(I32, (L, L), 0)
    col = lax.broadcasted_iota(I32, (L, L), 1)
    return (col >= row) if reverse else (col <= row)


def _bidir_scan(kern, L, row_ins, lane_ins, res_ins, out_cols, scratch, bsz, t, ctx_len, name):
    nc = t // L
    rev = _reverse_order(ctx_len // L, nc)
    fwd = lambda i: i
    n = bsz * t

    def row(c, order):
        return pl.BlockSpec((L, c), lambda b, i: (b * nc + order(i), 0))

    def lanes(r, order):
        return pl.BlockSpec((r, L), lambda b, i: (0, b * nc + order(i)))

    in_specs, args = [], []
    for order in (fwd, rev):
        in_specs += [row(a.shape[1], order) for a in row_ins] + [lanes(r, order) for _, r in lane_ins]
        args += list(row_ins) + [a for a, _ in lane_ins]
    in_specs += [_resident(a.shape) for a in res_ins]
    args += list(res_ins)
    return pl.pallas_call(
        kern,
        grid=(bsz, nc),
        in_specs=in_specs,
        out_specs=[row(c, fwd) for c in out_cols] + [row(c, rev) for c in out_cols],
        out_shape=[jax.ShapeDtypeStruct((n, c), BF16) for c in out_cols] * 2,
        scratch_shapes=scratch,
        compiler_params=_cparams("arbitrary", "arbitrary"),
        name=name,
    )(*args)


def _zero_at_start(*scratch):
    @pl.when(pl.program_id(1) == 0)
    def _():
        for s in scratch:
            s[...] = jnp.zeros_like(s)


def _mlstm_kernel(L, qf, kf, vf, gcf, grf, qb, kb, vb, gcb, grb, of, ob, c_scr, m_scr):
    _zero_at_start(c_scr, m_scr)
    ones = jnp.ones((L, LANES), BF16)
    wide = lambda c, k: jnp.concatenate([c] * k, axis=1)
    er = lax.broadcasted_iota(I32, (LANES, 2 * M_HEADS * LANES), 0)
    ej = jnp.right_shift(lax.broadcasted_iota(I32, (LANES, 2 * M_HEADS * LANES), 1), LANES.bit_length() - 1)
    chains = []
    for d, (q_ref, k_ref, v_ref, gc_ref, gr_ref, out_ref) in enumerate(((qf, kf, vf, gcf, grf, of),
                                                                         (qb, kb, vb, gcb, grb, ob))):
        reverse = d == 1
        mask = _tri_mask(L, reverse)
        mask01 = mask.astype(BF16)
        gc = gc_ref[...]
        gr = gr_ref[...]
        bcol = _sel_cols(mask01, gc)
        brow = _sel_rows(gr, mask01)
        glane = lax.broadcasted_iota(I32, (L, LANES), 1)
        src = jnp.where(ej < M_HEADS, M_HEADS * d + ej, M_HEADS + M_HEADS * d + ej)
        s1, s2, _ = _split3(jnp.where(glane < 2 * M_HEADS, gc, bcol))
        sel = (er == src).astype(BF16)
        rep = _dot(s1, sel) + _dot(s2, sel)
        last = 0 if reverse else L - 1
        for h in range(M_HEADS):
            fo, io = 2 * M_HEADS + M_HEADS * d + h, M_HEADS * d + h
            ic, bc = rep[:, h * LANES:(h + 1) * LANES], rep[:, (M_HEADS + h) * LANES:(M_HEADS + h + 1) * LANES]
            r1 = gr[io:io + 1, :] - brow[fo:fo + 1, :]
            m = m_scr[d, h:h + 1, :]
            g = bc + m
            mx = jnp.max(jnp.where(mask, r1, -jnp.inf), axis=1, keepdims=True)
            mt = jnp.maximum(g, bc + mx)
            p = jnp.exp(jnp.where(mask, wide(bc - mt, L // LANES) + r1, -jnp.inf))
            bl = bc[last:last + 1, :]
            a_s = bl - bc + ic
            m_new = jnp.maximum(bl + m, jnp.max(a_s, axis=0, keepdims=True))
            chains.append(dict(
                d=d, h=h, out_ref=out_ref, mt=mt, w_inter=jnp.exp(g - mt), p=p,
                m_new=m_new, decay=jnp.exp(bl + m - m_new), ws=jnp.exp(a_s - m_new),
                qh=q_ref[:, h * M_DK:(h + 1) * M_DK], kh=k_ref[:, h * M_DK:(h + 1) * M_DK],
                vx=jnp.concatenate([v_ref[:, h * M_DV:(h + 1) * M_DV], ones], axis=1)))
    for ch in chains:
        ch["sc"] = (_dot_nt(ch["qh"], ch["kh"]) * ch["p"]).astype(BF16)
    for ch in chains:
        cx = c_scr[ch["d"], ch["h"]]
        numx = wide(ch["w_inter"], 1 + M_DV // LANES) * _dot(ch["qh"], cx.astype(BF16)) + _dot(ch["sc"], ch["vx"])
        den = numx[:, M_DV:]
        rinv = 1.0 / jnp.maximum(jnp.abs(den), jnp.exp(-ch["mt"]))
        hh = numx[:, 0:M_DV] * wide(rinv, M_DV // LANES)
        ch["out_ref"][:, ch["h"] * M_DV:(ch["h"] + 1) * M_DV] = hh.astype(BF16)
    for ch in chains:
        d, h = ch["d"], ch["h"]
        kw = (ch["kh"].astype(F32) * ch["ws"]).astype(BF16)
        c_scr[d, h] = wide(ch["decay"], 1 + M_DV // LANES) * c_scr[d, h] + _dot_tn(kw, ch["vx"])
        m_scr[d, h:h + 1, :] = ch["m_new"]


def _mlstm(q, k, v, gcol, grow, bsz, t, ctx_len):
    L = M_CHUNK
    scratch = [pltpu.VMEM((2, M_HEADS, M_DK, M_DV + LANES), F32), pltpu.VMEM((2, 8, LANES), F32)]
    return _bidir_scan(functools.partial(_mlstm_kernel, L), L, [q, k, v, gcol], [(grow, 16)], [], [1024], scratch,
                       bsz, t, ctx_len, "mlstm_scan")


def _lru_kernel(L, uf, ub_, wa_ref, wx_ref, ba_ref, bx_ref, lam_ref, of, ob, a_scr, b_scr, h_scr):
    _zero_at_start(h_scr)
    R, C = L // SUBLANES, LRU_BLOCKS * LRU_BW
    i0, i1 = lax.broadcasted_iota(I32, (L, L), 0), lax.broadcasted_iota(I32, (L, L), 1)
    when = lambda p: jnp.bitwise_and(p, SUBLANES - 1) * R + jnp.right_shift(p, SUBLANES.bit_length() - 1)
    regroup = (i1 == when(i0)).astype(BF16)
    ungroup = (i0 == when(i1)).astype(BF16)
    sub = lax.broadcasted_iota(I32, (SUBLANES, 1), 0)
    dirs = ((uf, of), (ub_, ob))
    for d, (u_ref, _) in enumerate(dirs):
        u = _dot(regroup, u_ref[...])
        ub = u.astype(BF16)
        sp = -LRU_C * _softplus(-lam_ref[d])
        for n in range(LRU_BLOCKS):
            sl = slice(n * LRU_BW, (n + 1) * LRU_BW)
            rg = _sigmoid(_dot(ub[:, sl], wa_ref[d, n]) + ba_ref[d, :, sl])
            ig = _sigmoid(_dot(ub[:, sl], wx_ref[d, n]) + bx_ref[d, :, sl])
            log_a = rg * sp[:, sl]
            a = jnp.exp(log_a)
            a_scr[d, :, sl] = a
            b_scr[d, :, sl] = jnp.sqrt(-jnp.tanh(log_a) * (a * a + 1.0)) * (ig * u[:, sl])
    S = SUBLANES
    hs = [jnp.zeros((S, C), F32), jnp.zeros((S, C), F32)]
    ps = [jnp.ones((S, C), F32), jnp.ones((S, C), F32)]
    for j in range(R):
        for d in range(2):
            rows = slice(S * j, S * j + S) if d == 0 else slice(S * (R - 1 - j), S * (R - j))
            a = a_scr[d, rows, :]
            hs[d] = a * hs[d] + b_scr[d, rows, :]
            ps[d] = a * ps[d]
            b_scr[d, rows, :] = hs[d]
            a_scr[d, rows, :] = ps[d]
    for d, (_, out_ref) in enumerate(dirs):
        c = h_scr[d:d + 1, :]
        cin = jnp.zeros((S, C), F32)
        for s in (range(S) if d == 0 else reversed(range(S))):
            cin = jnp.where(sub == s, c, cin)
            c = ps[d][s:s + 1, :] * c + hs[d][s:s + 1, :]
        h_scr[d:d + 1, :] = c
        h = b_scr[d].reshape(R, S, C) + a_scr[d].reshape(R, S, C) * cin[None]
        out_ref[...] = _dot(ungroup, h.reshape(L, C).astype(BF16)).astype(BF16)


def _lru(u, wa, wx, ba, bx, lam, bsz, t, ctx_len):
    L = LRU_TILE
    scratch = [pltpu.VMEM((2, L, 1024), F32), pltpu.VMEM((2, L, 1024), F32), pltpu.VMEM((8, 1024), F32)]
    return _bidir_scan(functools.partial(_lru_kernel, L), L, [u], [], [wa, wx, ba, bx, lam], [1024], scratch,
                       bsz, t, ctx_len, "rglru_scan")


def _swiglu_rows(x, is_ctx, modl_ref, modc_ref, g2_ref, g3_ref, w1_ref, w2_ref):
    h = _adaln(x, g2_ref[...], modl_ref, modc_ref, 3, is_ctx).astype(BF16)
    ff = w2_ref.shape[0]
    acc = jnp.zeros(x.shape, F32)
    for j in range(ff // FF_STEP):
        lo = j * FF_STEP
        gj = _dot(h, w1_ref[:, lo:lo + FF_STEP])
        uj = _dot(h, w1_ref[:, ff + lo:ff + lo + FF_STEP])
        acc = acc + _dot((_silu(gj) * uj).astype(BF16), w2_ref[lo:lo + FF_STEP, :])
    return x + _mod(modl_ref, modc_ref, 5, is_ctx) * _rms(acc, g3_ref[...])


def _out_tail(tm, ctx_len, a1, a2, x_ref, modl_ref, modc_ref, g_ref, w_ref, xo_ref, ffn_refs=None):
    is_ctx = _row_ids(pl.program_id(1), tm) < ctx_len
    half = w_ref.shape[0] // 2
    y = _dot(a1.astype(BF16), w_ref[0:half, :]) + _dot(a2.astype(BF16), w_ref[half:, :])
    x = x_ref[...] + _mod(modl_ref, modc_ref, 2, is_ctx) * _rms(y, g_ref[...])
    if ffn_refs is not None:
        x = _swiglu_rows(x, is_ctx, modl_ref, modc_ref, *ffn_refs)
    xo_ref[...] = x


def _out_ab_kernel(tm, ctx_len, hf, hb, og, lf, lb, glg, nw, x_ref, modl_ref, modc_ref, g_ref, w_ref,
                   g2_ref, g3_ref, w1_ref, w2_ref, xo_ref):
    hm = _head_rms(hf[...].astype(F32) + hb[...].astype(F32), nw[...], M_HEADS) * og[...].astype(F32)
    hl = (lf[...].astype(F32) + lb[...].astype(F32)) * glg[...].astype(F32)
    _out_tail(tm, ctx_len, hm, hl, x_ref, modl_ref, modc_ref, g_ref, w_ref, xo_ref,
              (g2_ref, g3_ref, w1_ref, w2_ref))


def _out_cd_kernel(tm, ctx_len, gf, gb, grs, yf, yb, sx, zs, gnw, dsk, snw, x_ref, modl_ref, modc_ref, g_ref,
                   w_ref, xo_ref):
    og = _head_rms(gf[...].astype(F32) + gb[...].astype(F32), gnw[...], G_HEADS) * grs[...].astype(F32)
    ys = yf[...].astype(F32) + yb[...].astype(F32) + dsk[...] * sx[...].astype(F32)
    ys = _rms(ys * zs[...].astype(F32), snw[...])
    _out_tail(tm, ctx_len, og, ys, x_ref, modl_ref, modc_ref, g_ref, w_ref, xo_ref)


def _out_proj(kern, name, acts, vecs, x2, modl, modc, g, w, bsz, t, ctx_len, extra=()):
    tm = _big_row_tile(t)
    nt = t // tm
    row = lambda c: pl.BlockSpec((tm, c), lambda b, i: (b * nt + i, 0))
    return pl.pallas_call(
        functools.partial(kern, tm, ctx_len),
        grid=(bsz, nt),
        in_specs=[row(1024)] * len(acts) + [_resident((1, 1024))] * len(vecs)
        + [row(D_MODEL), pl.BlockSpec((1, 8, D_MODEL), lambda b, i: (b, 0, 0)),
           _resident((8, D_MODEL)), _resident((1, D_MODEL)), _resident(w.shape)]
        + [_resident(a.shape) for a in extra],
        out_specs=row(D_MODEL),
        out_shape=jax.ShapeDtypeStruct(x2.shape, F32),
        compiler_params=_cparams("arbitrary", "arbitrary"),
        name=name,
    )(*acts, *vecs, x2, modl, modc, g, w, *extra)


def _in_cd_kernel(is_lat, n_aliased, x_ref, modl_ref, modc_ref, g_ref, w_ref, cw_ref, cb_ref, dtb_ref, aneg_ref,
                  *refs):
    gq_ref, gk_ref, gv_ref, gr_ref, z_ref, sx_ref, sb_ref, sc_ref, ga_ref, dts_ref, dts_t_ref, h_scr = \
        refs[n_aliased:]
    h_scr[...] = _adaln(x_ref[...], g_ref[...], modl_ref, modc_ref, 0, not is_lat).astype(BF16)

    def conv_silu(col0):
        return lambda y, c: _silu(_dwconv(y, cw_ref[:, col0 + c:col0 + c + PROJ_COLS],
                                          cb_ref[:, col0 + c:col0 + c + PROJ_COLS], is_lat))

    _project(h_scr, w_ref, 0, gq_ref, lambda y, c: y * (G_DK ** -0.5))
    _project(h_scr, w_ref, 512, gk_ref, lambda y, c: y)
    _project(h_scr, w_ref, 1024, gv_ref, lambda y, c: y)
    _project(h_scr, w_ref, 2048, gr_ref, lambda y, c: _silu(y))
    _project(h_scr, w_ref, 3072, z_ref, lambda y, c: _silu(y))
    _project(h_scr, w_ref, 4096, sx_ref, conv_silu(0))
    _project(h_scr, w_ref, 5120, sb_ref, conv_silu(1024))
    _project(h_scr, w_ref, 5376, sc_ref, conv_silu(1280))
    ga_ref[...] = _dot(h_scr[...], w_ref[:, 5632:5760])
    dt = _softplus(_dot(h_scr[...], w_ref[:, 5760:5888]) + dtb_ref[...])
    lane = lax.broadcasted_iota(I32, dt.shape, 1)
    dts = jnp.where(lane < 2 * S_HEADS, dt, dt * aneg_ref[...])
    dts_ref[...] = dts
    dts_t_ref[...] = dts.T


def _in_cd(x2, modl, modc, g, w, cw, cb, dtb, aneg, bsz, t, ctx_len):
    outs = [(512, BF16), (512, BF16), (1024, BF16), (1024, BF16), (1024, BF16), (1024, BF16), (256, BF16),
            (256, BF16), (LANES, F32), (LANES, F32)]
    return _in_proj_calls(
        _in_cd_kernel, "in_proj_cd",
        x2, pl.BlockSpec((ctx_len, D_MODEL), lambda b: (b * (t // ctx_len), 0)),
        x2, _latent_rows(LATENT_TILE, t, ctx_len),
        (modl, modc, g, w, cw, cb, dtb, aneg), outs, bsz, t, ctx_len)


def _gla_kernel(L, C, qf, kf, vf, gaf, qb, kb, vb, gab, awh_ref, awl_ref, ab_ref, of, ob, s_scr, att_scr):
    _zero_at_start(s_scr, att_scr)
    nb = L // C
    chains = []
    for d, (q_ref, k_ref, v_ref, ga_ref, out_ref) in enumerate(((qf, kf, vf, gaf, of), (qb, kb, vb, gab, ob))):
        reverse = d == 1
        mask = _tri_mask(L, reverse)
        g1, g2, _ = _split3(ga_ref[...])
        pre = _dot(g1, awh_ref[d]) + _dot(g2, awh_ref[d]) + _dot(g1, awl_ref[d])
        lg = _log_sigmoid(pre + ab_ref[d]) * (1.0 / G_TAU)
        ball = _sel_cols(mask.astype(BF16), lg)
        for h in range(G_HEADS):
            ks = slice(h * G_DK, (h + 1) * G_DK)
            b = ball[:, ks] * LOG2E
            qh = q_ref[:, ks].astype(F32)
            inter = _dot_nt((qh * jnp.exp2(b)).astype(BF16), s_scr[d, h].astype(BF16))
            chains.append(dict(d=d, h=h, reverse=reverse, mask=mask, b=b, qh=qh, kh=k_ref[:, ks].astype(F32),
                               vh=v_ref[:, h * G_DV:(h + 1) * G_DV], inter=inter, out_ref=out_ref, blocks=[]))
    T = min(G_GROUP, L)
    for i in range(nb):
        lo, hi = i * C, (i + 1) * C
        t0 = lo // T * T
        for c, ch in enumerate(chains):
            b = ch["b"]
            if ch["reverse"]:
                bref = b[hi:hi + 1, :] if i < nb - 1 else jnp.zeros((1, G_DK), F32)
            else:
                bref = b[lo - 1:lo, :] if i > 0 else jnp.zeros((1, G_DK), F32)
            qi = (ch["qh"][lo:hi, :] * jnp.exp2(b[lo:hi, :] - bref)).astype(BF16)
            ki = (ch["kh"][t0:t0 + T] * jnp.exp2(bref - b[t0:t0 + T])).astype(BF16)
            att_scr[c, lo:hi, t0:t0 + T] = _dot_nt(qi, ki)
    for g in range(L // T):
        t0, t1 = g * T, (g + 1) * T
        for c, ch in enumerate(chains):
            b = ch["b"]
            e0, e1 = (t1, L) if ch["reverse"] else (0, t0)
            if e1 > e0:
                bref = b[t1:t1 + 1, :] if ch["reverse"] else b[t0 - 1:t0, :]
                qg = (ch["qh"][t0:t1] * jnp.exp2(b[t0:t1] - bref)).astype(BF16)
                kg = (ch["kh"][e0:e1] * jnp.exp2(bref - b[e0:e1])).astype(BF16)
                att_scr[c, t0:t1, e0:e1] = _dot_nt(qg, kg)
    for c, ch in enumerate(chains):
        att = jnp.where(ch["mask"], att_scr[c], 0.0).astype(BF16)
        oh = ch["inter"] + _dot(att, ch["vh"])
        ch["out_ref"][:, ch["h"] * G_DV:(ch["h"] + 1) * G_DV] = oh.astype(BF16)
    for ch in chains:
        d, h, b = ch["d"], ch["h"], ch["b"]
        last = 0 if ch["reverse"] else L - 1
        bl = b[last:last + 1, :]
        kd = (ch["kh"] * jnp.exp2(bl - b)).astype(BF16)
        s_scr[d, h] = s_scr[d, h] * jnp.exp2(bl) + _dot_tn(ch["vh"], kd)


def _gla(q, k, v, ga, awh, awl, ab, bsz, t, ctx_len):
    L, C = G_CHUNK, G_SUB
    scratch = [pltpu.VMEM((2, G_HEADS, G_DV, G_DK), F32), pltpu.VMEM((2 * G_HEADS, L, L), F32)]
    return _bidir_scan(functools.partial(_gla_kernel, L, C), L, [q, k, v, ga], [], [awh, awl, ab], [1024], scratch,
                       bsz, t, ctx_len, "gla_scan")


def _ssd_kernel(L, xf, bf, cf, gcf, grf, xb, bb, cb_, gcb, grb, of, ob, s_scr):
    _zero_at_start(s_scr)
    er = lax.broadcasted_iota(I32, (LANES, S_HEADS * S_P), 0)
    ec = jnp.right_shift(lax.broadcasted_iota(I32, (LANES, S_HEADS * S_P), 1), S_P.bit_length() - 1)
    glane = lax.broadcasted_iota(I32, (L, LANES), 1)
    lane = lax.broadcasted_iota(I32, (L, 2 * S_P), 1)
    dirs, chains = [], []
    for d, (x_ref, b_ref, c_ref, gc_ref, gr_ref, out_ref) in enumerate(((xf, bf, cf, gcf, grf, of),
                                                                         (xb, bb, cb_, gcb, grb, ob))):
        reverse = d == 1
        mask = _tri_mask(L, reverse)
        mask01 = mask.astype(BF16)
        gc = gc_ref[...]
        gr = gr_ref[...]
        bcol = _sel_cols(mask01, gc)
        brow = _sel_rows(gr, mask01)
        last = 0 if reverse else L - 1
        dt_o, la_o = S_HEADS * d, 2 * S_HEADS + S_HEADS * d
        sel_la = (er == ec + la_o).astype(BF16)
        x = x_ref[...]
        bla = jnp.where((glane >= la_o) & (glane < la_o + S_HEADS), bcol, 0.0)
        dt_at_la = pltpu.roll(gc, 2 * S_HEADS, 1)
        bl = bla[last:last + 1, :]
        ebx = _dot(jnp.exp(bla).astype(BF16), sel_la)
        wx = _dot((jnp.exp(bl - bla) * dt_at_la).astype(BF16), sel_la)
        dirs.append(dict(d=d, mask=mask, gr=gr, x=x, ebx=ebx, last=last, dt_o=dt_o, la_o=la_o, out_ref=out_ref,
                         xw=(x.astype(F32) * wx).astype(BF16),
                         bcol2=bcol * LOG2E, brow2=brow * LOG2E))
        for g in range(S_GROUPS):
            ns = slice(g * S_N, (g + 1) * S_N)
            chains.append(dict(dr=dirs[-1], g=g, bg=b_ref[:, ns], cg=c_ref[:, ns],
                               gs=slice(g * S_HPG * S_P, (g + 1) * S_HPG * S_P)))
    for ch in chains:
        dr = ch["dr"]
        ch["cb"] = _dot_nt(ch["cg"], ch["bg"])
        ch["inter"] = _dot(ch["cg"], s_scr[dr["d"], ch["g"]].astype(BF16)) * dr["ebx"][:, ch["gs"]]
    for ch in chains:
        dr, g = ch["dr"], ch["g"]
        parts = []
        for pr in range(S_HPG // 2):
            e0 = g * S_HPG + 2 * pr
            xp = dr["x"][:, e0 * S_P:(e0 + 2) * S_P]
            res = []
            for e in (e0, e0 + 1):
                col = dr["la_o"] + e
                bc, br = dr["bcol2"][:, col:col + 1], dr["brow2"][col:col + 1, :]
                dtr = dr["gr"][dr["dt_o"] + e:dr["dt_o"] + e + 1, :]
                att = (ch["cb"] * dtr * jnp.exp2(jnp.where(dr["mask"], bc - br, -jnp.inf))).astype(BF16)
                res.append(_dot(att, xp))
            parts.append(jnp.where(lane < S_P, res[0], res[1]))
        dr["out_ref"][:, ch["gs"]] = (jnp.concatenate(parts, axis=1) + ch["inter"]).astype(BF16)
    for ch in chains:
        dr, g, gs = ch["dr"], ch["g"], ch["gs"]
        s_scr[dr["d"], g] = (s_scr[dr["d"], g] * dr["ebx"][dr["last"]:dr["last"] + 1, gs]
                             + _dot_tn(ch["bg"], dr["xw"][:, gs]))


def _ssd(x, bm, cm, gcol, grow, bsz, t, ctx_len):
    L = S_CHUNK
    scratch = [pltpu.VMEM((2, S_GROUPS, S_N, S_HPG * S_P), F32)]
    return _bidir_scan(functools.partial(_ssd_kernel, L), L, [x, bm, cm, gcol], [(grow, 4 * S_HEADS)], [], [1024],
                       scratch,
                       bsz, t, ctx_len, "ssd_scan")


def _router_kernel(x_ref, modl_ref, g_ref, rw_ref, rb_ref, h_ref, idx_ref, gate_ref, cnt_ref, base_scr):
    @pl.when((pl.program_id(0) == 0) & (pl.program_id(1) == 0))
    def _():
        base_scr[...] = jnp.zeros_like(base_scr)

    h = _rms(x_ref[...], g_ref[...]) * (1.0 + modl_ref[0, 4:5, :]) + modl_ref[0, 3:4, :]
    h_ref[...] = _pack_rows(h)
    h1, h2, _ = _split3(h)
    both = _dot(h1, rw_ref[...])
    logits = both[:, :LANES] + both[:, LANES:] + _dot(h2, rw_ref[:, :LANES]) + rb_ref[...]
    lane = lax.broadcasted_iota(I32, logits.shape, 1)
    logits = jnp.where(lane < N_EXPERTS, logits, -jnp.inf)
    lanef = lane.astype(F32)
    m1 = jnp.max(logits, axis=1, keepdims=True)
    i1 = jnp.min(jnp.where(logits == m1, lanef, float(LANES)), axis=1, keepdims=True)
    rest = jnp.where(lanef == i1, -jnp.inf, logits)
    m2 = jnp.max(rest, axis=1, keepdims=True)
    i2 = jnp.min(jnp.where(rest == m2, lanef, float(LANES)), axis=1, keepdims=True)
    e = jnp.exp(m2 - m1)
    g1 = 1.0 / (1.0 + e)
    gate_ref[...] = jnp.where(lane == 0, g1, jnp.where(lane == 1, e * g1, 0.0))
    tm = logits.shape[0]
    oh1, oh2 = (lanef == i1).astype(F32), (lanef == i2).astype(F32)
    earlier = (lax.broadcasted_iota(I32, (tm, tm), 1) < lax.broadcasted_iota(I32, (tm, tm), 0)).astype(BF16)
    seen = base_scr[0:1, :] + _dot(earlier, (oh1 + oh2).astype(BF16))
    r1 = jnp.sum(oh1 * seen, axis=1, keepdims=True)
    r2 = jnp.sum(oh2 * seen, axis=1, keepdims=True)
    base_scr[0:1, :] = base_scr[0:1, :] + jnp.sum(oh1 + oh2, axis=0, keepdims=True)
    cnt_ref[...] = base_scr[...]
    idx_ref[...] = jnp.where(lane == 0, i1, jnp.where(lane == 1, i2, jnp.where(lane == 2, r1, jnp.where(
        lane == 3, r2, 0.0)))).astype(I32)


def _latent_rows(tm, t, ctx_len):
    return pl.BlockSpec((pl.Element(tm), pl.Element(D_MODEL)),
                        lambda b, i: (pl.multiple_of(b * t + ctx_len + i * tm, SUBLANES), 0))


def _router(x2, modl, g, rw, rb, bsz, t, ctx_len):
    tm = LATENT_TILE
    nl = bsz * (t - ctx_len)
    nlt = (t - ctx_len) // tm
    orow = lambda c: pl.BlockSpec((tm, c), lambda b, i: (b * nlt + i, 0))
    return pl.pallas_call(
        _router_kernel,
        grid=(bsz, nlt),
        in_specs=[_latent_rows(tm, t, ctx_len),
                  pl.BlockSpec((1, 8, D_MODEL), lambda b, i: (b, 0, 0)),
                  _resident((1, D_MODEL)), _resident(rw.shape), _resident(rb.shape)],
        out_specs=[orow(D_MODEL // 2), orow(LANES), orow(LANES), pl.BlockSpec((8, LANES), lambda b, i: (0, 0))],
        out_shape=[jax.ShapeDtypeStruct((nl, D_MODEL // 2), jnp.uint32), jax.ShapeDtypeStruct((nl, LANES), I32),
                   jax.ShapeDtypeStruct((nl, LANES), F32), jax.ShapeDtypeStruct((8, LANES), F32)],
        scratch_shapes=[pltpu.VMEM((8, LANES), F32)],
        compiler_params=_cparams("arbitrary", "arbitrary"),
        name="moe_router",
    )(x2, modl, g, rw, rb)


def _gather_rows(table, idx):
    n_rows, d = idx.shape[0], table.shape[1]
    n_workers = SC_CORES * SC_SUBCORES
    per_worker = n_rows // n_workers
    chunk = SC_GATHER_ROWS
    assert per_worker * n_workers == n_rows and per_worker % chunk == 0
    mesh = plsc.VectorSubcoreMesh(core_axis_name="c", subcore_axis_name="s")

    @functools.partial(
        pl.kernel, mesh=mesh, out_type=jax.ShapeDtypeStruct((n_rows, d), table.dtype),
        scratch_types=[pltpu.VMEM((chunk,), I32), pltpu.VMEM((chunk, d), table.dtype), pltpu.SemaphoreType.DMA],
        name="sc_gather_rows")
    def gather(table_hbm, idx_hbm, out_hbm, idx_v, rows_v, sem):
        base = (lax.axis_index("s") * SC_CORES + lax.axis_index("c")) * per_worker

        @pl.loop(0, per_worker // chunk)
        def _(j):
            off = pl.multiple_of(base + j * chunk, 8)
            pltpu.sync_copy(idx_hbm.at[pl.ds(off, chunk)], idx_v)
            pltpu.async_copy(table_hbm.at[idx_v], rows_v, sem).wait()
            pltpu.sync_copy(rows_v, out_hbm.at[pl.ds(off, chunk)])

    return gather(table, idx)


def _dispatch_rows(table, slot, n_out):
    assert TOP_K == 2
    n_assign, (n_tok, d) = slot.shape[0], table.shape
    n_workers = SC_CORES * SC_SUBCORES
    per_worker = n_out // n_workers
    chunk, vec = SC_GATHER_ROWS, SC_LANES
    assert per_worker * n_workers == n_out and per_worker % chunk == 0 and n_assign % vec == 0
    assert n_out < 3 * n_tok
    mesh = plsc.VectorSubcoreMesh(core_axis_name="c", subcore_axis_name="s")

    @functools.partial(
        pl.kernel, mesh=mesh, out_type=jax.ShapeDtypeStruct((n_out, d), table.dtype),
        scratch_types=[pltpu.VMEM((n_assign,), I32), pltpu.VMEM((per_worker,), I32),
                       pltpu.VMEM((chunk, d), table.dtype), pltpu.SemaphoreType.DMA],
        compiler_params=pltpu.CompilerParams(needs_layout_passes=False),
        name="sc_dispatch_rows")
    def dispatch(table_hbm, slot_hbm, out_hbm, slots_v, tok_v, rows_v, sem):
        base = (lax.axis_index("s") * SC_CORES + lax.axis_index("c")) * per_worker
        pltpu.sync_copy(slot_hbm, slots_v)
        lane = lax.broadcasted_iota(I32, (vec,), 0)

        @pl.loop(0, per_worker // vec)
        def _(j):
            p = base + j * vec + lane
            p = jnp.where(p >= n_tok, p - n_tok, p)
            tok_v[pl.ds(j * vec, vec)] = jnp.where(p >= n_tok, p - n_tok, p)

        @pl.loop(0, n_assign // vec)
        def _(j):
            s = slots_v[pl.ds(j * vec, vec)] - base
            mine = (s >= 0) & (s < per_worker)
            tok = lax.shift_right_logical(j * vec + lane, 1)
            plsc.store_scatter(tok_v, [jnp.where(mine, s, 0)], tok, mask=mine)

        @pl.loop(0, per_worker // chunk)
        def _(j):
            off = pl.multiple_of(j * chunk, 8)
            pltpu.async_copy(table_hbm.at[tok_v.at[pl.ds(off, chunk)]], rows_v, sem).wait()
            pltpu.sync_copy(rows_v, out_hbm.at[pl.ds(base + off, chunk)])

    return dispatch(table, slot)


def _moe_kernel(be_ref, nu_ref, nv_ref, x_ref, w1g_ref, w1u_ref, w2_ref, y_ref, xb_ref, acc_ref):
    i, j = pl.program_id(0), pl.program_id(1)
    bm = x_ref.shape[0]

    def swiglu_rows(rows):
        x = xb_ref[0:rows, :]
        a = (_silu(_dot(x, w1g_ref[0].astype(BF16))) * _dot(x, w1u_ref[0].astype(BF16))).astype(BF16)
        acc_ref[0:rows, :] += _dot(a, w2_ref[0].astype(BF16))

    @pl.when(i < nu_ref[0])
    def _():
        @pl.when(j == 0)
        def _():
            acc_ref[...] = jnp.zeros_like(acc_ref)
            xb_ref[...] = _unpack_rows(x_ref[...]).astype(BF16)

        @pl.when(nv_ref[i] > bm // 2)
        def _():
            swiglu_rows(bm)

        @pl.when(nv_ref[i] <= bm // 2)
        def _():
            swiglu_rows(bm // 2)

        @pl.when(j == pl.num_programs(1) - 1)
        def _():
            y_ref[...] = _pack_rows(acc_ref[...])


def _moe_experts(xs, w1, w2, block_e, n_used, n_valid, n_blocks):
    bm, fc = MOE_ROWS, MOE_FF
    nff = FF_EXPERT // fc
    used = lambda i, nu: jnp.minimum(i, nu[0] - 1)
    ffi = lambda i, j, nu: jnp.where(i < nu[0], j, nff - 1)
    grid_spec = pltpu.PrefetchScalarGridSpec(
        num_scalar_prefetch=3,
        grid=(n_blocks, nff),
        in_specs=[pl.BlockSpec((bm, D_MODEL // 2), lambda i, j, be, nu, nv: (used(i, nu), 0)),
                  pl.BlockSpec((1, D_MODEL, fc), lambda i, j, be, nu, nv: (be[used(i, nu)], 0, ffi(i, j, nu))),
                  pl.BlockSpec((1, D_MODEL, fc),
                               lambda i, j, be, nu, nv: (be[used(i, nu)], 0, nff + ffi(i, j, nu))),
                  pl.BlockSpec((1, fc, D_MODEL), lambda i, j, be, nu, nv: (be[used(i, nu)], ffi(i, j, nu), 0))],
        out_specs=pl.BlockSpec((bm, D_MODEL // 2), lambda i, j, be, nu, nv: (used(i, nu), 0)),
        scratch_shapes=[pltpu.VMEM((bm, D_MODEL), BF16), pltpu.VMEM((bm, D_MODEL), F32)],
    )
    return pl.pallas_call(
        _moe_kernel,
        grid_spec=grid_spec,
        out_shape=jax.ShapeDtypeStruct((n_blocks * bm, D_MODEL // 2), jnp.uint32),
        compiler_params=_cparams("arbitrary", "arbitrary"),
        name="moe_experts",
    )(block_e, n_used, n_valid, xs, w1, w1, w2)


def _combine_kernel(y0_ref, y1_ref, gate_ref, x_ref, modl_ref, g_ref, o_ref):
    gt = gate_ref[...]
    y = gt[:, 0:1] * _unpack_rows(y0_ref[...]) + gt[:, 1:2] * _unpack_rows(y1_ref[...])
    o_ref[...] = x_ref[...] + modl_ref[0, 5:6, :] * _rms(y, g_ref[...])


def _combine(yk, gates, x2, modl, g, bsz, t, ctx_len):
    tm = LATENT_TILE
    nlt = (t - ctx_len) // tm
    nl = bsz * (t - ctx_len)
    orow = lambda c: pl.BlockSpec((tm, c), lambda b, i: (b * nlt + i, 0))
    second = pl.BlockSpec((tm, D_MODEL // 2), lambda b, i: (nl // tm + b * nlt + i, 0))
    return pl.pallas_call(
        _combine_kernel,
        grid=(bsz, nlt),
        in_specs=[orow(D_MODEL // 2), second, orow(LANES),
                  _latent_rows(tm, t, ctx_len),
                  pl.BlockSpec((1, 8, D_MODEL), lambda b, i: (b, 0, 0)), _resident((1, D_MODEL))],
        out_specs=orow(D_MODEL),
        out_shape=jax.ShapeDtypeStruct((nl, D_MODEL), F32),
        compiler_params=_cparams("arbitrary", "arbitrary"),
        name="moe_combine",
    )(yk, yk, gates, x2, modl, g)


def _routing_tables(experts, ranks, counts):
    bm = MOE_ROWS
    n_assign = experts.size
    padded = (counts + bm - 1) // bm * bm
    ends_p = jnp.cumsum(padded)
    pstarts = ends_p - padded
    onehot = experts[..., None] == jnp.arange(N_EXPERTS, dtype=I32)
    slot = (jnp.sum(jnp.where(onehot, pstarts, 0), axis=-1) + ranks).astype(I32)
    n_blocks = n_assign // bm + N_EXPERTS
    block_e = jnp.minimum(jnp.sum(ends_p[None, :] <= (jnp.arange(n_blocks, dtype=I32) * bm)[:, None], axis=1),
                          N_EXPERTS - 1).astype(I32)
    n_used = (ends_p[-1] // bm).astype(I32).reshape(1)
    sel = block_e[:, None] == jnp.arange(N_EXPERTS, dtype=I32)
    rows_left = jnp.sum(jnp.where(sel, counts + pstarts, 0), axis=1) - jnp.arange(n_blocks, dtype=I32) * bm
    n_valid = jnp.clip(rows_left, 0, bm).astype(I32)
    return slot, block_e, n_used, n_valid, n_blocks


def kernel(x, c, ctx, c_ctx, mod_w, mod_b, norm_g, ab_in_w, m_conv_w, m_conv_b, m_gate_b, m_norm_w, l_conv_w,
           l_conv_b, l_wa, l_ba, l_wx, l_bx, l_lam, ab_out_w, ffn_w1, ffn_w2, cd_in_w, g_alpha_w, g_alpha_b,
           g_norm_w, s_conv_w, s_conv_b, s_dt_bias, s_A_log, s_D, s_norm_w, cd_out_w, router_w, router_b,
           moe_w1, moe_w2):
    bsz, seq, dm = x.shape
    ctx_len = ctx.shape[1]
    t = ctx_len + seq
    n = bsz * t
    assert ctx_len == ROW_TILE and seq % ROW_TILE == 0 and ROW_TILE % GRID_W == 0

    c_all = jnp.concatenate([c, c_ctx[None, :], jnp.zeros((-(bsz + 1) % 8, dm), F32)], axis=0)
    mods = _modulation(c_all, mod_w, mod_b).reshape(mod_w.shape[0], c_all.shape[0], 6, dm)
    pad2 = jnp.zeros((bsz, 2, dm), F32)

    def layer_mods(layer):
        modl = jnp.concatenate([mods[layer, :bsz], pad2], axis=1)
        modc = jnp.concatenate([mods[layer, bsz], pad2[0]], axis=0)
        return modl, modc

    modl, modc = layer_mods(0)
    g = norm_g[0]
    w = ab_in_w[0]
    o_qk, o_v, o_o, o_gt, o_lx, o_lg = 0, 1024, 2048, 3072, 3088, 4112
    w_in = jnp.concatenate([w[:, o_qk:o_v], w[:, o_v:o_o], w[:, o_o:o_gt], w[:, o_lx:o_lg], w[:, o_lg:],
                            w[:, o_gt:o_lx], jnp.zeros((dm, LANES - 16), F32)], axis=1).astype(BF16)
    gb = jnp.concatenate([m_gate_b[0].reshape(1, 16), jnp.zeros((1, LANES - 16), F32)], axis=1)
    x2, q, k, v, og, lx, glg, gates, gates_t = _in_ab(
        ctx.reshape(bsz * ctx_len, dm), x.reshape(bsz * seq, dm), modl, modc, g[0:1], w_in, m_conv_w[0],
        m_conv_b[0][None], l_conv_w[0], l_conv_b[0][None], gb, bsz, t, ctx_len)
    hm_f, hm_b = _mlstm(q, k, v, gates, gates_t, bsz, t, ctx_len)
    hl_f, hl_b = _lru(lx, l_wa[0].astype(BF16), l_wx[0].astype(BF16), l_ba[0][:, None], l_bx[0][:, None],
                      l_lam[0][:, None], bsz, t, ctx_len)
    x2 = _out_proj(_out_ab_kernel, "out_proj_swiglu_ab", [hm_f, hm_b, og, hl_f, hl_b, glg], [m_norm_w[0][None]],
                   x2, modl, modc, g[1:2], ab_out_w[0].astype(BF16), bsz, t, ctx_len,
                   extra=(g[2:3], g[3:4], ffn_w1[0].astype(BF16), ffn_w2[0].astype(BF16)))

    modl, modc = layer_mods(1)
    g = norm_g[1]
    w = cd_in_w[0]
    o_gv, o_gr, o_ga, o_z, o_xbc, o_dt = 1024, 2048, 3072, 3104, 4128, 5664
    zpad = jnp.zeros((dm, LANES - 32), F32)
    w_in = jnp.concatenate([w[:, :o_gv], w[:, o_gv:o_gr], w[:, o_gr:o_ga], w[:, o_z:o_xbc], w[:, o_xbc:o_dt],
                            w[:, o_ga:o_z], zpad, w[:, o_dt:], w[:, o_dt:], jnp.zeros((dm, LANES - 64), F32)],
                           axis=1).astype(BF16)
    dtb = jnp.concatenate([s_dt_bias[0].reshape(1, 32), s_dt_bias[0].reshape(1, 32),
                           jnp.zeros((1, LANES - 64), F32)], axis=1)
    aneg = jnp.concatenate([jnp.zeros((1, 32), F32), -jnp.exp(s_A_log[0].reshape(1, 32)),
                            jnp.zeros((1, LANES - 64), F32)], axis=1)
    gq, gk, gv, grs, zs, sx, sb, sc, ga, dts, dts_t = _in_cd(x2, modl, modc, g[0:1], w_in, s_conv_w[0],
                                                             s_conv_b[0][None], dtb, aneg, bsz, t, ctx_len)
    aw = jnp.zeros((2, LANES, G_HEADS * G_DK), F32)
    aw = aw.at[0, 0:G_RANK].set(g_alpha_w[0, 0]).at[1, G_RANK:2 * G_RANK].set(g_alpha_w[0, 1])
    awh = aw.astype(BF16)
    awl = (aw - awh.astype(F32)).astype(BF16)
    og_f, og_b = _gla(gq, gk, gv, ga, awh, awl, g_alpha_b[0][:, None], bsz, t, ctx_len)
    ys_f, ys_b = _ssd(sx, sb, sc, dts, dts_t, bsz, t, ctx_len)
    dskip = jnp.repeat(s_D[0], S_P)[None]
    x2 = _out_proj(_out_cd_kernel, "out_proj_cd", [og_f, og_b, grs, ys_f, ys_b, sx, zs],
                   [g_norm_w[0][None], dskip, s_norm_w[0][None]],
                   x2, modl, modc, g[1:2], cd_out_w[0].astype(BF16), bsz, t, ctx_len)

    rw = jnp.concatenate([router_w[0], jnp.zeros((dm, LANES - N_EXPERTS), F32)], axis=1)
    rwh = rw.astype(BF16)
    rw = jnp.concatenate([rwh, (rw - rwh.astype(F32)).astype(BF16)], axis=1)
    rb = jnp.concatenate([router_b[0][None], jnp.zeros((1, LANES - N_EXPERTS), F32)], axis=1)
    h2, top_idx, top_gate, counts = _router(x2, modl, g[2:3], rw, rb, bsz, t, ctx_len)
    slot, block_e, n_used, n_valid, n_blocks = _routing_tables(top_idx[:, 0:TOP_K], top_idx[:, TOP_K:2 * TOP_K],
                                                               counts[0, :N_EXPERTS].astype(I32))
    xs = _dispatch_rows(h2, slot.reshape(-1), n_blocks * MOE_ROWS)
    ys_e = _moe_experts(xs, moe_w1[0], moe_w2[0], block_e, n_used, n_valid, n_blocks)
    yk = _gather_rows(ys_e, jnp.concatenate([slot[:, 0], slot[:, 1]]))
    out = _combine(yk, top_gate, x2, modl, g[3:4], bsz, t, ctx_len)
    return out.reshape(bsz, seq, dm)
```

```python
import functools

import jax
import jax.numpy as jnp
from jax import lax
from jax.experimental import pallas as pl
from jax.experimental.pallas import tpu as pltpu
from jax.experimental.pallas import tpu_sc as plsc

F32, BF16, I32 = jnp.float32, jnp.bfloat16, jnp.int32

D_MODEL = 1024
GRID_W = 64
EPS = 1e-6
LOG2E = 1.4426950408889634
M_HEADS, M_DK, M_DV = 4, 128, 256
LRU_BLOCKS, LRU_BW, LRU_C = 8, 128, 8.0
G_HEADS, G_DK, G_DV, G_RANK, G_TAU = 4, 128, 256, 16, 16.0
S_HEADS, S_P, S_N, S_GROUPS, S_HPG = 16, 64, 128, 2, 8
FF_DENSE = 2816
N_EXPERTS, TOP_K, FF_EXPERT = 8, 2, 3584

LANES, SUBLANES = 128, 8
VMEM_LIMIT = 56 * 1024 * 1024
ROW_TILE = 256
IN_PARTS = 2
LATENT_TILE = 512
BIG_ROW_TILE_MAX = 640
PROJ_COLS = 256
M_CHUNK = 256
G_CHUNK, G_SUB = 256, 16
G_GROUP = 64
S_CHUNK = 256
LRU_TILE = 256
FF_STEP = 256
MOE_ROWS = 1024
MOE_FF = 512
SC_CORES, SC_SUBCORES = 2, 16
SC_LANES = 16
SC_GATHER_ROWS = 128


def _cparams(*sem):
    return pltpu.CompilerParams(dimension_semantics=sem, vmem_limit_bytes=VMEM_LIMIT)


def _sigmoid(x):
    return 0.5 + 0.5 * jnp.tanh(0.5 * x)


def _silu(x):
    return x * _sigmoid(x)


def _softplus(x):
    return jnp.maximum(x, 0.0) + jnp.log1p(jnp.exp(-jnp.abs(x)))


def _log_sigmoid(x):
    return jnp.minimum(x, 0.0) - jnp.log1p(jnp.exp(-jnp.abs(x)))


def _dot(a, b):
    return jnp.dot(a, b, preferred_element_type=F32)


def _dot_nt(a, b):
    return lax.dot_general(a, b, (((1,), (1,)), ((), ())), preferred_element_type=F32)


def _dot_tn(a, b):
    return lax.dot_general(a, b, (((0,), (0,)), ((), ())), preferred_element_type=F32)


def _split3(f):
    f1 = f.astype(BF16)
    r = f - f1.astype(F32)
    f2 = r.astype(BF16)
    f3 = (r - f2.astype(F32)).astype(BF16)
    return f1, f2, f3


def _sel_cols(mask01, f):
    p1, p2, p3 = _split3(f)
    return _dot(mask01, p1) + _dot(mask01, p2) + _dot(mask01, p3)


def _sel_rows(f, mask01):
    p1, p2, p3 = _split3(f)
    return _dot_nt(p1, mask01) + _dot_nt(p2, mask01) + _dot_nt(p3, mask01)


def _pack_rows(u):
    n = u.shape[1] // 2
    bits = lax.bitcast_convert_type(u.astype(BF16).astype(F32), jnp.uint32)
    return jnp.bitwise_or(bits[:, :n], jnp.right_shift(bits[:, n:], jnp.uint32(16)))


def _unpack_rows(p):
    hi = lax.bitcast_convert_type(jnp.bitwise_and(p, jnp.uint32(0xFFFF0000)), F32)
    lo = lax.bitcast_convert_type(jnp.left_shift(p, jnp.uint32(16)), F32)
    return jnp.concatenate([hi, lo], axis=1)


def _rms(u, g):
    return u * lax.rsqrt(jnp.mean(u * u, axis=-1, keepdims=True) + EPS) * g


def _head_rms(u, g, n_heads):
    w = u.shape[-1] // n_heads
    return jnp.concatenate([_rms(u[:, h * w:(h + 1) * w], g[:, h * w:(h + 1) * w]) for h in range(n_heads)], axis=1)


def _row_ids(tile_idx, tm):
    return tile_idx * tm + lax.broadcasted_iota(I32, (tm, 1), 0)


def _mod(modl_ref, modc_ref, idx, is_ctx):
    return jnp.where(is_ctx, modc_ref[idx:idx + 1, :], modl_ref[0, idx:idx + 1, :])


def _adaln(x, g, modl_ref, modc_ref, shift_idx, is_ctx):
    shift = _mod(modl_ref, modc_ref, shift_idx, is_ctx)
    scale = _mod(modl_ref, modc_ref, shift_idx + 1, is_ctx)
    return _rms(x, g) * (1.0 + scale) + shift


def _neighbour_tiles(r, step, is_lat):
    n_tiles, per_seg = r.shape[0], GRID_W // SUBLANES
    zero = jnp.zeros((1,) + r.shape[1:], r.dtype)

    def across(src):
        return jnp.where(is_lat, 0.0, r[src:src + 1]) if 0 <= src < n_tiles else zero

    pieces = []
    for s0 in range(0, n_tiles, per_seg):
        if step > 0:
            pieces += [across(s0 - 1), r[s0:s0 + per_seg - 1]]
        else:
            pieces += [r[s0 + 1:s0 + per_seg], across(s0 + per_seg)]
    return jnp.concatenate(pieces, axis=0)


def _dwconv(y, cw, cb, is_lat):
    tm, n = y.shape
    y3 = y.reshape(tm // SUBLANES, SUBLANES, n)
    sub = lax.broadcasted_iota(I32, (1, SUBLANES, 1), 1)
    w = lambda j: cw[j:j + 1, :].reshape(1, 1, n)
    r1, r2, r7 = pltpu.roll(y3, 1, 1), pltpu.roll(y3, 2, 1), pltpu.roll(y3, SUBLANES - 1, 1)
    ym1 = jnp.where(sub >= 1, r1, _neighbour_tiles(r1, 1, is_lat))
    ym2 = jnp.where(sub >= 2, r2, _neighbour_tiles(r2, 1, is_lat))
    yp1 = jnp.where(sub <= SUBLANES - 2, r7, _neighbour_tiles(r7, -1, is_lat))
    out = cb.reshape(1, 1, n) + w(0) * ym2 + w(1) * ym1 + w(2) * y3 + w(3) * yp1
    return out.reshape(tm, n)


def _resident(shape):
    nd = len(shape)
    return pl.BlockSpec(shape, lambda *_: (0,) * nd, pipeline_mode=pl.Buffered(1))


def _big_row_tile(t):
    return max(tm for tm in range(8, BIG_ROW_TILE_MAX + 1, 8) if t % tm == 0)


def _mod_kernel(c_ref, w_ref, b_ref, o_ref):
    a1, a2, _ = _split3(_silu(c_ref[...]))
    w = w_ref[0]
    wh = w.astype(BF16)
    wl = (w - wh.astype(F32)).astype(BF16)
    o_ref[0] = _dot(a1, wh) + _dot(a2, wh) + _dot(a1, wl) + b_ref[0]


def _modulation(c_all, mod_w, mod_b):
    depth, d, n6 = mod_w.shape
    rows = c_all.shape[0]
    tn = 1024
    return pl.pallas_call(
        _mod_kernel,
        grid=(depth, n6 // tn),
        in_specs=[pl.BlockSpec((rows, d), lambda l, j: (0, 0)),
                  pl.BlockSpec((1, d, tn), lambda l, j: (l, 0, j)),
                  pl.BlockSpec((1, 1, tn), lambda l, j: (l, 0, j))],
        out_specs=pl.BlockSpec((1, rows, tn), lambda l, j: (l, 0, j)),
        out_shape=jax.ShapeDtypeStruct((depth, rows, n6), F32),
        compiler_params=_cparams("arbitrary", "arbitrary"),
        name="modulation",
    )(c_all, mod_w, mod_b.reshape(depth, 1, n6))


def _project(h_ref, w_ref, w_col, out_ref, epilogue):
    for c in range(0, out_ref.shape[1], PROJ_COLS):
        y = _dot(h_ref[...], w_ref[:, w_col + c:w_col + c + PROJ_COLS])
        out_ref[:, c:c + PROJ_COLS] = epilogue(y, c).astype(out_ref.dtype)


def _part_is_latent(part, parts_per_batch):
    return lax.rem(pl.program_id(0) * IN_PARTS + part, parts_per_batch) != 0


def _conv_parts(y, cw, cb, lat):
    return jnp.concatenate([_dwconv(y[p * ROW_TILE:(p + 1) * ROW_TILE], cw, cb, lat[p]) for p in range(IN_PARTS)],
                           axis=0)


def _in_ab_kernel(parts_per_batch, *refs):
    xs, refs = refs[:3 * IN_PARTS], refs[3 * IN_PARTS:]
    modc_ref, g_ref, w_ref, mcw_ref, mcb_ref, lcw_ref, lcb_ref, gb_ref = refs[:8]
    x_ref, q_ref, k_ref, v_ref, og_ref, lx_ref, glg_ref, gates_ref, gates_t_ref, h_scr = refs[8:]
    lat = [_part_is_latent(p, parts_per_batch) for p in range(IN_PARTS)]
    for p in range(IN_PARTS):
        xc_ref, xl_ref, modl_ref = xs[3 * p:3 * p + 3]
        rows = slice(p * ROW_TILE, (p + 1) * ROW_TILE)
        xp = jnp.where(lat[p], xl_ref[...], xc_ref[...])
        x_ref[rows, :] = xp
        h_scr[rows, :] = _adaln(xp, g_ref[...], modl_ref, modc_ref, 0, jnp.logical_not(lat[p])).astype(BF16)

    def conv(cw_ref, cb_ref):
        return lambda y, c: _conv_parts(y, cw_ref[:, c:c + PROJ_COLS], cb_ref[:, c:c + PROJ_COLS], lat)

    qc, kc, lc = conv(mcw_ref, mcb_ref), conv(mcw_ref, mcb_ref), conv(lcw_ref, lcb_ref)
    _project(h_scr, w_ref, 0, q_ref, lambda y, c: _silu(qc(y, c)))
    _project(h_scr, w_ref, 512, k_ref, lambda y, c: _silu(kc(y, 512 + c)) * (M_DK ** -0.5))
    _project(h_scr, w_ref, 1024, v_ref, lambda y, c: y)
    _project(h_scr, w_ref, 2048, og_ref, lambda y, c: _sigmoid(y))
    _project(h_scr, w_ref, 3072, lx_ref, lc)
    _project(h_scr, w_ref, 4096, glg_ref, lambda y, c: jax.nn.gelu(y))
    gt = _dot(h_scr[...], w_ref[:, 5120:5248]) + gb_ref[...]
    lane = lax.broadcasted_iota(I32, gt.shape, 1)
    gates = jnp.where(lane >= 2 * M_HEADS, _log_sigmoid(gt), gt)
    gates_ref[...] = gates
    gates_t_ref[...] = gates.T


def _in_ab(xc2, xl2, modl, modc, g, w, mcw, mcb, lcw, lcb, gb, bsz, t, ctx_len):
    n = bsz * t
    tm = IN_PARTS * ROW_TILE
    ppb = t // ROW_TILE
    lpb = ppb - 1
    part = lambda j, p: j * IN_PARTS + p
    batch = lambda j, p: part(j, p) // ppb
    x_specs, x_args = [], []
    for p in range(IN_PARTS):
        x_specs += [pl.BlockSpec((ROW_TILE, D_MODEL), lambda j, p=p: (batch(j, p), 0)),
                    pl.BlockSpec((ROW_TILE, D_MODEL),
                                 lambda j, p=p: (batch(j, p) * lpb + jnp.maximum(part(j, p) % ppb - 1, 0), 0)),
                    pl.BlockSpec((1, 8, D_MODEL), lambda j, p=p: (batch(j, p), 0, 0))]
        x_args += [xc2, xl2, modl]
    row = lambda c: pl.BlockSpec((tm, c), lambda j: (j, 0))
    outs = [(D_MODEL, F32), (512, BF16), (512, BF16), (1024, BF16), (1024, BF16), (1024, BF16), (1024, BF16),
            (LANES, F32)]
    params = (modc, g, w, mcw, mcb, lcw, lcb, gb)
    return pl.pallas_call(
        functools.partial(_in_ab_kernel, ppb),
        grid=(n // tm,),
        in_specs=x_specs + [_resident(a.shape) for a in params],
        out_specs=[row(c) for c, _ in outs] + [pl.BlockSpec((LANES, tm), lambda j: (0, j))],
        out_shape=[jax.ShapeDtypeStruct((n, c), dt) for c, dt in outs] + [jax.ShapeDtypeStruct((LANES, n), F32)],
        scratch_shapes=[pltpu.VMEM((tm, D_MODEL), BF16)],
        compiler_params=_cparams("arbitrary"),
        name="in_proj_ab",
    )(*x_args, *params)


def _reverse_order(n_ctx_chunks, n_chunks):
    return lambda c: jnp.where(c < n_ctx_chunks, n_ctx_chunks - 1 - c, n_chunks - 1 - (c - n_ctx_chunks))


def _tri_mask(L, reverse):
    row = lax.broadcasted_iota(I32, (L, L), 0)
    col = lax.broadcasted_iota(I32, (L, L), 1)
    return (col >= row) if reverse else (col <= row)


def _bidir_scan(kern, L, row_ins, lane_ins, res_ins, out_cols, scratch, bsz, t, ctx_len, name):
    nc = t // L
    rev = _reverse_order(ctx_len // L, nc)
    fwd = lambda i: i
    n = bsz * t

    def row(c, order):
        return pl.BlockSpec((L, c), lambda b, i: (b * nc + order(i), 0))

    def lanes(r, order):
        return pl.BlockSpec((r, L), lambda b, i: (0, b * nc + order(i)))

    in_specs, args = [], []
    for order in (fwd, rev):
        in_specs += [row(a.shape[1], order) for a in row_ins] + [lanes(r, order) for _, r in lane_ins]
        args += list(row_ins) + [a for a, _ in lane_ins]
    in_specs += [_resident(a.shape) for a in res_ins]
    args += list(res_ins)
    return pl.pallas_call(
        kern,
        grid=(bsz, nc),
        in_specs=in_specs,
        out_specs=[row(c, fwd) for c in out_cols] + [row(c, rev) for c in out_cols],
        out_shape=[jax.ShapeDtypeStruct((n, c), BF16) for c in out_cols] * 2,
        scratch_shapes=scratch,
        compiler_params=_cparams("arbitrary", "arbitrary"),
        name=name,
    )(*args)


def _zero_at_start(*scratch):
    @pl.when(pl.program_id(1) == 0)
    def _():
        for s in scratch:
            s[...] = jnp.zeros_like(s)


def _mlstm_kernel(L, qf, kf, vf, gcf, grf, qb, kb, vb, gcb, grb, of, ob, c_scr, m_scr):
    _zero_at_start(c_scr, m_scr)
    ones = jnp.ones((L, LANES), BF16)
    wide = lambda c, k: jnp.concatenate([c] * k, axis=1)
    er = lax.broadcasted_iota(I32, (LANES, 2 * M_HEADS * LANES), 0)
    ej = jnp.right_shift(lax.broadcasted_iota(I32, (LANES, 2 * M_HEADS * LANES), 1), LANES.bit_length() - 1)
    chains = []
    for d, (q_ref, k_ref, v_ref, gc_ref, gr_ref, out_ref) in enumerate(((qf, kf, vf, gcf, grf, of),
                                                                         (qb, kb, vb, gcb, grb, ob))):
        reverse = d == 1
        mask = _tri_mask(L, reverse)
        mask01 = mask.astype(BF16)
        gc = gc_ref[...]
        gr = gr_ref[...]
        bcol = _sel_cols(mask01, gc)
        brow = _sel_rows(gr, mask01)
        glane = lax.broadcasted_iota(I32, (L, LANES), 1)
        src = jnp.where(ej < M_HEADS, M_HEADS * d + ej, M_HEADS + M_HEADS * d + ej)
        s1, s2, _ = _split3(jnp.where(glane < 2 * M_HEADS, gc, bcol))
        sel = (er == src).astype(BF16)
        rep = _dot(s1, sel) + _dot(s2, sel)
        last = 0 if reverse else L - 1
        for h in range(M_HEADS):
            fo, io = 2 * M_HEADS + M_HEADS * d + h, M_HEADS * d + h
            ic, bc = rep[:, h * LANES:(h + 1) * LANES], rep[:, (M_HEADS + h) * LANES:(M_HEADS + h + 1) * LANES]
            r1 = gr[io:io + 1, :] - brow[fo:fo + 1, :]
            m = m_scr[d, h:h + 1, :]
            g = bc + m
            mx = jnp.max(jnp.where(mask, r1, -jnp.inf), axis=1, keepdims=True)
            mt = jnp.maximum(g, bc + mx)
            p = jnp.exp(jnp.where(mask, wide(bc - mt, L // LANES) + r1, -jnp.inf))
            bl = bc[last:last + 1, :]
            a_s = bl - bc + ic
            m_new = jnp.maximum(bl + m, jnp.max(a_s, axis=0, keepdims=True))
            chains.append(dict(
                d=d, h=h, out_ref=out_ref, mt=mt, w_inter=jnp.exp(g - mt), p=p,
                m_new=m_new, decay=jnp.exp(bl + m - m_new), ws=jnp.exp(a_s - m_new),
                qh=q_ref[:, h * M_DK:(h + 1) * M_DK], kh=k_ref[:, h * M_DK:(h + 1) * M_DK],
                vx=jnp.concatenate([v_ref[:, h * M_DV:(h + 1) * M_DV], ones], axis=1)))
    for ch in chains:
        ch["sc"] = (_dot_nt(ch["qh"], ch["kh"]) * ch["p"]).astype(BF16)
    for ch in chains:
        cx = c_scr[ch["d"], ch["h"]]
        numx = wide(ch["w_inter"], 1 + M_DV // LANES) * _dot(ch["qh"], cx.astype(BF16)) + _dot(ch["sc"], ch["vx"])
        den = numx[:, M_DV:]
        rinv = 1.0 / jnp.maximum(jnp.abs(den), jnp.exp(-ch["mt"]))
        hh = numx[:, 0:M_DV] * wide(rinv, M_DV // LANES)
        ch["out_ref"][:, ch["h"] * M_DV:(ch["h"] + 1) * M_DV] = hh.astype(BF16)
    for ch in chains:
        d, h = ch["d"], ch["h"]
        kw = (ch["kh"].astype(F32) * ch["ws"]).astype(BF16)
        c_scr[d, h] = wide(ch["decay"], 1 + M_DV // LANES) * c_scr[d, h] + _dot_tn(kw, ch["vx"])
        m_scr[d, h:h + 1, :] = ch["m_new"]


def _mlstm(q, k, v, gcol, grow, bsz, t, ctx_len):
    L = M_CHUNK
    scratch = [pltpu.VMEM((2, M_HEADS, M_DK, M_DV + LANES), F32), pltpu.VMEM((2, 8, LANES), F32)]
    return _bidir_scan(functools.partial(_mlstm_kernel, L), L, [q, k, v, gcol], [(grow, 16)], [], [1024], scratch,
                       bsz, t, ctx_len, "mlstm_scan")


def _lru_kernel(L, uf, ub_, wa_ref, wx_ref, ba_ref, bx_ref, lam_ref, of, ob, a_scr, b_scr, h_scr):
    _zero_at_start(h_scr)
    R, C = L // SUBLANES, LRU_BLOCKS * LRU_BW
    i0, i1 = lax.broadcasted_iota(I32, (L, L), 0), lax.broadcasted_iota(I32, (L, L), 1)
    when = lambda p: jnp.bitwise_and(p, SUBLANES - 1) * R + jnp.right_shift(p, SUBLANES.bit_length() - 1)
    regroup = (i1 == when(i0)).astype(BF16)
    ungroup = (i0 == when(i1)).astype(BF16)
    sub = lax.broadcasted_iota(I32, (SUBLANES, 1), 0)
    dirs = ((uf, of), (ub_, ob))
    for d, (u_ref, _) in enumerate(dirs):
        u = _dot(regroup, u_ref[...])
        ub = u.astype(BF16)
        sp = -LRU_C * _softplus(-lam_ref[d])
        for n in range(LRU_BLOCKS):
            sl = slice(n * LRU_BW, (n + 1) * LRU_BW)
            rg = _sigmoid(_dot(ub[:, sl], wa_ref[d, n]) + ba_ref[d, :, sl])
            ig = _sigmoid(_dot(ub[:, sl], wx_ref[d, n]) + bx_ref[d, :, sl])
            log_a = rg * sp[:, sl]
            a = jnp.exp(log_a)
            a_scr[d, :, sl] = a
            b_scr[d, :, sl] = jnp.sqrt(-jnp.tanh(log_a) * (a * a + 1.0)) * (ig * u[:, sl])
    S = SUBLANES
    hs = [jnp.zeros((S, C), F32), jnp.zeros((S, C), F32)]
    ps = [jnp.ones((S, C), F32), jnp.ones((S, C), F32)]
    for j in range(R):
        for d in range(2):
            rows = slice(S * j, S * j + S) if d == 0 else slice(S * (R - 1 - j), S * (R - j))
            a = a_scr[d, rows, :]
            hs[d] = a * hs[d] + b_scr[d, rows, :]
            ps[d] = a * ps[d]
            b_scr[d, rows, :] = hs[d]
            a_scr[d, rows, :] = ps[d]
    for d, (_, out_ref) in enumerate(dirs):
        c = h_scr[d:d + 1, :]
        cin = jnp.zeros((S, C), F32)
        for s in (range(S) if d == 0 else reversed(range(S))):
            cin = jnp.where(sub == s, c, cin)
            c = ps[d][s:s + 1, :] * c + hs[d][s:s + 1, :]
        h_scr[d:d + 1, :] = c
        h = b_scr[d].reshape(R, S, C) + a_scr[d].reshape(R, S, C) * cin[None]
        out_ref[...] = _dot(ungroup, h.reshape(L, C).astype(BF16)).astype(BF16)


def _lru(u, wa, wx, ba, bx, lam, bsz, t, ctx_len):
    L = LRU_TILE
    scratch = [pltpu.VMEM((2, L, 1024), F32), pltpu.VMEM((2, L, 1024), F32), pltpu.VMEM((8, 1024), F32)]
    return _bidir_scan(functools.partial(_lru_kernel, L), L, [u], [], [wa, wx, ba, bx, lam], [1024], scratch,
                       bsz, t, ctx_len, "rglru_scan")


def _swiglu_rows(x, is_ctx, modl_ref, modc_ref, g2_ref, g3_ref, w1_ref, w2_ref):
    h = _adaln(x, g2_ref[...], modl_ref, modc_ref, 3, is_ctx).astype(BF16)
    ff = w2_ref.shape[0]
    acc = jnp.zeros(x.shape, F32)
    for j in range(ff // FF_STEP):
        lo = j * FF_STEP
        gj = _dot(h, w1_ref[:, lo:lo + FF_STEP])
        uj = _dot(h, w1_ref[:, ff + lo:ff + lo + FF_STEP])
        acc = acc + _dot((_silu(gj) * uj).astype(BF16), w2_ref[lo:lo + FF_STEP, :])
    return x + _mod(modl_ref, modc_ref, 5, is_ctx) * _rms(acc, g3_ref[...])


def _out_tail(tm, ctx_len, a1, a2, x_ref, modl_ref, modc_ref, g_ref, w_ref, xo_ref, ffn_refs=None):
    is_ctx = _row_ids(pl.program_id(1), tm) < ctx_len
    half = w_ref.shape[0] // 2
    y = _dot(a1.astype(BF16), w_ref[0:half, :]) + _dot(a2.astype(BF16), w_ref[half:, :])
    x = x_ref[...] + _mod(modl_ref, modc_ref, 2, is_ctx) * _rms(y, g_ref[...])
    if ffn_refs is not None:
        x = _swiglu_rows(x, is_ctx, modl_ref, modc_ref, *ffn_refs)
    xo_ref[...] = x


def _out_ab_kernel(tm, ctx_len, hf, hb, og, lf, lb, glg, nw, x_ref, modl_ref, modc_ref, g_ref, w_ref,
                   g2_ref, g3_ref, w1_ref, w2_ref, xo_ref):
    hm = _head_rms(hf[...].astype(F32) + hb[...].astype(F32), nw[...], M_HEADS) * og[...].astype(F32)
    hl = (lf[...].astype(F32) + lb[...].astype(F32)) * glg[...].astype(F32)
    _out_tail(tm, ctx_len, hm, hl, x_ref, modl_ref, modc_ref, g_ref, w_ref, xo_ref,
              (g2_ref, g3_ref, w1_ref, w2_ref))


def _out_cd_kernel(tm, ctx_len, gf, gb, grs, yf, yb, sx, zs, gnw, dsk, snw, x_ref, modl_ref, modc_ref, g_ref,
                   w_ref, xo_ref):
    og = _head_rms(gf[...].astype(F32) + gb[...].astype(F32), gnw[...], G_HEADS) * grs[...].astype(F32)
    ys = yf[...].astype(F32) + yb[...].astype(F32) + dsk[...] * sx[...].astype(F32)
    ys = _rms(ys * zs[...].astype(F32), snw[...])
    _out_tail(tm, ctx_len, og, ys, x_ref, modl_ref, modc_ref, g_ref, w_ref, xo_ref)


def _out_proj(kern, name, acts, vecs, x2, modl, modc, g, w, bsz, t, ctx_len, extra=()):
    tm = _big_row_tile(t)
    nt = t // tm
    row = lambda c: pl.BlockSpec((tm, c), lambda b, i: (b * nt + i, 0))
    return pl.pallas_call(
        functools.partial(kern, tm, ctx_len),
        grid=(bsz, nt),
        in_specs=[row(1024)] * len(acts) + [_resident((1, 1024))] * len(vecs)
        + [row(D_MODEL), pl.BlockSpec((1, 8, D_MODEL), lambda b, i: (b, 0, 0)),
           _resident((8, D_MODEL)), _resident((1, D_MODEL)), _resident(w.shape)]
        + [_resident(a.shape) for a in extra],
        out_specs=row(D_MODEL),
        out_shape=jax.ShapeDtypeStruct(x2.shape, F32),
        compiler_params=_cparams("arbitrary", "arbitrary"),
        name=name,
    )(*acts, *vecs, x2, modl, modc, g, w, *extra)


def _in_cd_kernel(parts_per_batch, x_ref, *refs):
    modls, refs = refs[:IN_PARTS], refs[IN_PARTS:]
    modc_ref, g_ref, w_ref, cw_ref, cb_ref, dtb_ref, aneg_ref = refs[:7]
    gq_ref, gk_ref, gv_ref, gr_ref, z_ref, sx_ref, sb_ref, sc_ref, ga_ref, dts_ref, dts_t_ref, h_scr = refs[7:]
    lat = [_part_is_latent(p, parts_per_batch) for p in range(IN_PARTS)]
    for p in range(IN_PARTS):
        rows = slice(p * ROW_TILE, (p + 1) * ROW_TILE)
        h_scr[rows, :] = _adaln(x_ref[rows, :], g_ref[...], modls[p], modc_ref, 0,
                                jnp.logical_not(lat[p])).astype(BF16)

    def conv_silu(col0):
        return lambda y, c: _silu(_conv_parts(y, cw_ref[:, col0 + c:col0 + c + PROJ_COLS],
                                              cb_ref[:, col0 + c:col0 + c + PROJ_COLS], lat))

    _project(h_scr, w_ref, 0, gq_ref, lambda y, c: y * (G_DK ** -0.5))
    _project(h_scr, w_ref, 512, gk_ref, lambda y, c: y)
    _project(h_scr, w_ref, 1024, gv_ref, lambda y, c: y)
    _project(h_scr, w_ref, 2048, gr_ref, lambda y, c: _silu(y))
    _project(h_scr, w_ref, 3072, z_ref, lambda y, c: _silu(y))
    _project(h_scr, w_ref, 4096, sx_ref, conv_silu(0))
    _project(h_scr, w_ref, 5120, sb_ref, conv_silu(1024))
    _project(h_scr, w_ref, 5376, sc_ref, conv_silu(1280))
    ga_ref[...] = _dot(h_scr[...], w_ref[:, 5632:5760])
    dt = _softplus(_dot(h_scr[...], w_ref[:, 5760:5888]) + dtb_ref[...])
    lane = lax.broadcasted_iota(I32, dt.shape, 1)
    dts = jnp.where(lane < 2 * S_HEADS, dt, dt * aneg_ref[...])
    dts_ref[...] = dts
    dts_t_ref[...] = dts.T


def _in_cd(x2, modl, modc, g, w, cw, cb, dtb, aneg, bsz, t, ctx_len):
    n = bsz * t
    tm = IN_PARTS * ROW_TILE
    ppb = t // ROW_TILE
    row = lambda c: pl.BlockSpec((tm, c), lambda j: (j, 0))
    mod_specs = [pl.BlockSpec((1, 8, D_MODEL), lambda j, p=p: ((j * IN_PARTS + p) // ppb, 0, 0))
                 for p in range(IN_PARTS)]
    outs = [(512, BF16), (512, BF16), (1024, BF16), (1024, BF16), (1024, BF16), (1024, BF16), (256, BF16),
            (256, BF16), (LANES, F32), (LANES, F32)]
    params = (modc, g, w, cw, cb, dtb, aneg)
    return pl.pallas_call(
        functools.partial(_in_cd_kernel, ppb),
        grid=(n // tm,),
        in_specs=[row(D_MODEL)] + mod_specs + [_resident(a.shape) for a in params],
        out_specs=[row(c) for c, _ in outs] + [pl.BlockSpec((LANES, tm), lambda j: (0, j))],
        out_shape=[jax.ShapeDtypeStruct((n, c), dt) for c, dt in outs] + [jax.ShapeDtypeStruct((LANES, n), F32)],
        scratch_shapes=[pltpu.VMEM((tm, D_MODEL), BF16)],
        compiler_params=_cparams("arbitrary"),
        name="in_proj_cd",
    )(x2, *([modl] * IN_PARTS), *params)


def _gla_kernel(L, C, qf, kf, vf, gaf, qb, kb, vb, gab, awh_ref, awl_ref, ab_ref, of, ob, s_scr, att_scr):
    _zero_at_start(s_scr, att_scr)
    nb = L // C
    chains = []
    for d, (q_ref, k_ref, v_ref, ga_ref, out_ref) in enumerate(((qf, kf, vf, gaf, of), (qb, kb, vb, gab, ob))):
        reverse = d == 1
        mask = _tri_mask(L, reverse)
        g1, g2, _ = _split3(ga_ref[...])
        pre = _dot(g1, awh_ref[d]) + _dot(g2, awh_ref[d]) + _dot(g1, awl_ref[d])
        lg = _log_sigmoid(pre + ab_ref[d]) * (1.0 / G_TAU)
        ball = _sel_cols(mask.astype(BF16), lg)
        for h in range(G_HEADS):
            ks = slice(h * G_DK, (h + 1) * G_DK)
            b = ball[:, ks] * LOG2E
            qh = q_ref[:, ks].astype(F32)
            inter = _dot_nt((qh * jnp.exp2(b)).astype(BF16), s_scr[d, h].astype(BF16))
            chains.append(dict(d=d, h=h, reverse=reverse, mask=mask, b=b, qh=qh, kh=k_ref[:, ks].astype(F32),
                               vh=v_ref[:, h * G_DV:(h + 1) * G_DV], inter=inter, out_ref=out_ref, blocks=[]))
    T = min(G_GROUP, L)
    for i in range(nb):
        lo, hi = i * C, (i + 1) * C
        t0 = lo // T * T
        for c, ch in enumerate(chains):
            b = ch["b"]
            if ch["reverse"]:
                bref = b[hi:hi + 1, :] if i < nb - 1 else jnp.zeros((1, G_DK), F32)
            else:
                bref = b[lo - 1:lo, :] if i > 0 else jnp.zeros((1, G_DK), F32)
            qi = (ch["qh"][lo:hi, :] * jnp.exp2(b[lo:hi, :] - bref)).astype(BF16)
            ki = (ch["kh"][t0:t0 + T] * jnp.exp2(bref - b[t0:t0 + T])).astype(BF16)
            att_scr[c, lo:hi, t0:t0 + T] = _dot_nt(qi, ki)
    for g in range(L // T):
        t0, t1 = g * T, (g + 1) * T
        for c, ch in enumerate(chains):
            b = ch["b"]
            e0, e1 = (t1, L) if ch["reverse"] else (0, t0)
            if e1 > e0:
                bref = b[t1:t1 + 1, :] if ch["reverse"] else b[t0 - 1:t0, :]
                qg = (ch["qh"][t0:t1] * jnp.exp2(b[t0:t1] - bref)).astype(BF16)
                kg = (ch["kh"][e0:e1] * jnp.exp2(bref - b[e0:e1])).astype(BF16)
                att_scr[c, t0:t1, e0:e1] = _dot_nt(qg, kg)
    for c, ch in enumerate(chains):
        att = jnp.where(ch["mask"], att_scr[c], 0.0).astype(BF16)
        oh = ch["inter"] + _dot(att, ch["vh"])
        ch["out_ref"][:, ch["h"] * G_DV:(ch["h"] + 1) * G_DV] = oh.astype(BF16)
    for ch in chains:
        d, h, b = ch["d"], ch["h"], ch["b"]
        last = 0 if ch["reverse"] else L - 1
        bl = b[last:last + 1, :]
        kd = (ch["kh"] * jnp.exp2(bl - b)).astype(BF16)
        s_scr[d, h] = s_scr[d, h] * jnp.exp2(bl) + _dot_tn(ch["vh"], kd)


def _gla(q, k, v, ga, awh, awl, ab, bsz, t, ctx_len):
    L, C = G_CHUNK, G_SUB
    scratch = [pltpu.VMEM((2, G_HEADS, G_DV, G_DK), F32), pltpu.VMEM((2 * G_HEADS, L, L), F32)]
    return _bidir_scan(functools.partial(_gla_kernel, L, C), L, [q, k, v, ga], [], [awh, awl, ab], [1024], scratch,
                       bsz, t, ctx_len, "gla_scan")


def _ssd_kernel(L, xf, bf, cf, gcf, grf, xb, bb, cb_, gcb, grb, of, ob, s_scr):
    _zero_at_start(s_scr)
    er = lax.broadcasted_iota(I32, (LANES, S_HEADS * S_P), 0)
    ec = jnp.right_shift(lax.broadcasted_iota(I32, (LANES, S_HEADS * S_P), 1), S_P.bit_length() - 1)
    glane = lax.broadcasted_iota(I32, (L, LANES), 1)
    lane = lax.broadcasted_iota(I32, (L, 2 * S_P), 1)
    dirs, chains = [], []
    for d, (x_ref, b_ref, c_ref, gc_ref, gr_ref, out_ref) in enumerate(((xf, bf, cf, gcf, grf, of),
                                                                         (xb, bb, cb_, gcb, grb, ob))):
        reverse = d == 1
        mask = _tri_mask(L, reverse)
        mask01 = mask.astype(BF16)
        gc = gc_ref[...]
        gr = gr_ref[...]
        bcol = _sel_cols(mask01, gc)
        brow = _sel_rows(gr, mask01)
        last = 0 if reverse else L - 1
        dt_o, la_o = S_HEADS * d, 2 * S_HEADS + S_HEADS * d
        sel_la = (er == ec + la_o).astype(BF16)
        x = x_ref[...]
        bla = jnp.where((glane >= la_o) & (glane < la_o + S_HEADS), bcol, 0.0)
        dt_at_la = pltpu.roll(gc, 2 * S_HEADS, 1)
        bl = bla[last:last + 1, :]
        ebx = _dot(jnp.exp(bla).astype(BF16), sel_la)
        wx = _dot((jnp.exp(bl - bla) * dt_at_la).astype(BF16), sel_la)
        dirs.append(dict(d=d, mask=mask, gr=gr, x=x, ebx=ebx, last=last, dt_o=dt_o, la_o=la_o, out_ref=out_ref,
                         xw=(x.astype(F32) * wx).astype(BF16),
                         bcol2=bcol * LOG2E, brow2=brow * LOG2E))
        for g in range(S_GROUPS):
            ns = slice(g * S_N, (g + 1) * S_N)
            chains.append(dict(dr=dirs[-1], g=g, bg=b_ref[:, ns], cg=c_ref[:, ns],
                               gs=slice(g * S_HPG * S_P, (g + 1) * S_HPG * S_P)))
    for ch in chains:
        dr = ch["dr"]
        ch["cb"] = _dot_nt(ch["cg"], ch["bg"])
        ch["inter"] = _dot(ch["cg"], s_scr[dr["d"], ch["g"]].astype(BF16)) * dr["ebx"][:, ch["gs"]]
    for ch in chains:
        dr, g = ch["dr"], ch["g"]
        parts = []
        for pr in range(S_HPG // 2):
            e0 = g * S_HPG + 2 * pr
            xp = dr["x"][:, e0 * S_P:(e0 + 2) * S_P]
            res = []
            for e in (e0, e0 + 1):
                col = dr["la_o"] + e
                bc, br = dr["bcol2"][:, col:col + 1], dr["brow2"][col:col + 1, :]
                dtr = dr["gr"][dr["dt_o"] + e:dr["dt_o"] + e + 1, :]
                att = (ch["cb"] * dtr * jnp.exp2(jnp.where(dr["mask"], bc - br, -jnp.inf))).astype(BF16)
                res.append(_dot(att, xp))
            parts.append(jnp.where(lane < S_P, res[0], res[1]))
        dr["out_ref"][:, ch["gs"]] = (jnp.concatenate(parts, axis=1) + ch["inter"]).astype(BF16)
    for ch in chains:
        dr, g, gs = ch["dr"], ch["g"], ch["gs"]
        s_scr[dr["d"], g] = (s_scr[dr["d"], g] * dr["ebx"][dr["last"]:dr["last"] + 1, gs]
                             + _dot_tn(ch["bg"], dr["xw"][:, gs]))


def _ssd(x, bm, cm, gcol, grow, bsz, t, ctx_len):
    L = S_CHUNK
    scratch = [pltpu.VMEM((2, S_GROUPS, S_N, S_HPG * S_P), F32)]
    return _bidir_scan(functools.partial(_ssd_kernel, L), L, [x, bm, cm, gcol], [(grow, 4 * S_HEADS)], [], [1024],
                       scratch,
                       bsz, t, ctx_len, "ssd_scan")


def _router_kernel(x_ref, modl_ref, g_ref, rw_ref, rb_ref, h_ref, idx_ref, gate_ref, cnt_ref, base_scr):
    @pl.when((pl.program_id(0) == 0) & (pl.program_id(1) == 0))
    def _():
        base_scr[...] = jnp.zeros_like(base_scr)

    h = _rms(x_ref[...], g_ref[...]) * (1.0 + modl_ref[0, 4:5, :]) + modl_ref[0, 3:4, :]
    h_ref[...] = _pack_rows(h)
    h1, h2, _ = _split3(h)
    both = _dot(h1, rw_ref[...])
    logits = both[:, :LANES] + both[:, LANES:] + _dot(h2, rw_ref[:, :LANES]) + rb_ref[...]
    lane = lax.broadcasted_iota(I32, logits.shape, 1)
    logits = jnp.where(lane < N_EXPERTS, logits, -jnp.inf)
    lanef = lane.astype(F32)
    m1 = jnp.max(logits, axis=1, keepdims=True)
    i1 = jnp.min(jnp.where(logits == m1, lanef, float(LANES)), axis=1, keepdims=True)
    rest = jnp.where(lanef == i1, -jnp.inf, logits)
    m2 = jnp.max(rest, axis=1, keepdims=True)
    i2 = jnp.min(jnp.where(rest == m2, lanef, float(LANES)), axis=1, keepdims=True)
    e = jnp.exp(m2 - m1)
    g1 = 1.0 / (1.0 + e)
    gate_ref[...] = jnp.where(lane == 0, g1, jnp.where(lane == 1, e * g1, 0.0))
    tm = logits.shape[0]
    oh1, oh2 = (lanef == i1).astype(F32), (lanef == i2).astype(F32)
    earlier = (lax.broadcasted_iota(I32, (tm, tm), 1) < lax.broadcasted_iota(I32, (tm, tm), 0)).astype(BF16)
    seen = base_scr[0:1, :] + _dot(earlier, (oh1 + oh2).astype(BF16))
    r1 = jnp.sum(oh1 * seen, axis=1, keepdims=True)
    r2 = jnp.sum(oh2 * seen, axis=1, keepdims=True)
    base_scr[0:1, :] = base_scr[0:1, :] + jnp.sum(oh1 + oh2, axis=0, keepdims=True)
    cnt_ref[...] = base_scr[...]
    idx_ref[...] = jnp.where(lane == 0, i1, jnp.where(lane == 1, i2, jnp.where(lane == 2, r1, jnp.where(
        lane == 3, r2, 0.0)))).astype(I32)


def _latent_rows(tm, t, ctx_len):
    return pl.BlockSpec((pl.Element(tm), pl.Element(D_MODEL)),
                        lambda b, i: (pl.multiple_of(b * t + ctx_len + i * tm, SUBLANES), 0))


def _router(x2, modl, g, rw, rb, bsz, t, ctx_len):
    tm = LATENT_TILE
    nl = bsz * (t - ctx_len)
    nlt = (t - ctx_len) // tm
    orow = lambda c: pl.BlockSpec((tm, c), lambda b, i: (b * nlt + i, 0))
    return pl.pallas_call(
        _router_kernel,
        grid=(bsz, nlt),
        in_specs=[_latent_rows(tm, t, ctx_len),
                  pl.BlockSpec((1, 8, D_MODEL), lambda b, i: (b, 0, 0)),
                  _resident((1, D_MODEL)), _resident(rw.shape), _resident(rb.shape)],
        out_specs=[orow(D_MODEL // 2), orow(LANES), orow(LANES), pl.BlockSpec((8, LANES), lambda b, i: (0, 0))],
        out_shape=[jax.ShapeDtypeStruct((nl, D_MODEL // 2), jnp.uint32), jax.ShapeDtypeStruct((nl, LANES), I32),
                   jax.ShapeDtypeStruct((nl, LANES), F32), jax.ShapeDtypeStruct((8, LANES), F32)],
        scratch_shapes=[pltpu.VMEM((8, LANES), F32)],
        compiler_params=_cparams("arbitrary", "arbitrary"),
        name="moe_router",
    )(x2, modl, g, rw, rb)


def _gather_rows(table, idx):
    n_rows, d = idx.shape[0], table.shape[1]
    n_workers = SC_CORES * SC_SUBCORES
    per_worker = n_rows // n_workers
    chunk = SC_GATHER_ROWS
    assert per_worker * n_workers == n_rows and per_worker % chunk == 0
    mesh = plsc.VectorSubcoreMesh(core_axis_name="c", subcore_axis_name="s")

    @functools.partial(
        pl.kernel, mesh=mesh, out_type=jax.ShapeDtypeStruct((n_rows, d), table.dtype),
        scratch_types=[pltpu.VMEM((chunk,), I32), pltpu.VMEM((chunk, d), table.dtype), pltpu.SemaphoreType.DMA],
        name="sc_gather_rows")
    def gather(table_hbm, idx_hbm, out_hbm, idx_v, rows_v, sem):
        base = (lax.axis_index("s") * SC_CORES + lax.axis_index("c")) * per_worker

        @pl.loop(0, per_worker // chunk)
        def _(j):
            off = pl.multiple_of(base + j * chunk, 8)
            pltpu.sync_copy(idx_hbm.at[pl.ds(off, chunk)], idx_v)
            pltpu.async_copy(table_hbm.at[idx_v], rows_v, sem).wait()
            pltpu.sync_copy(rows_v, out_hbm.at[pl.ds(off, chunk)])

    return gather(table, idx)


def _dispatch_rows(table, slot, n_out):
    assert TOP_K == 2
    n_assign, (n_tok, d) = slot.shape[0], table.shape
    n_workers = SC_CORES * SC_SUBCORES
    per_worker = n_out // n_workers
    chunk, vec = SC_GATHER_ROWS, SC_LANES
    assert per_worker * n_workers == n_out and per_worker % chunk == 0 and n_assign % vec == 0
    assert n_out < 3 * n_tok
    mesh = plsc.VectorSubcoreMesh(core_axis_name="c", subcore_axis_name="s")

    @functools.partial(
        pl.kernel, mesh=mesh, out_type=jax.ShapeDtypeStruct((n_out, d), table.dtype),
        scratch_types=[pltpu.VMEM((n_assign,), I32), pltpu.VMEM((per_worker,), I32),
                       pltpu.VMEM((chunk, d), table.dtype), pltpu.SemaphoreType.DMA],
        compiler_params=pltpu.CompilerParams(needs_layout_passes=False),
        name="sc_dispatch_rows")
    def dispatch(table_hbm, slot_hbm, out_hbm, slots_v, tok_v, rows_v, sem):
        base = (lax.axis_index("s") * SC_CORES + lax.axis_index("c")) * per_worker
        pltpu.sync_copy(slot_hbm, slots_v)
        lane = lax.broadcasted_iota(I32, (vec,), 0)

        @pl.loop(0, per_worker // vec)
        def _(j):
            p = base + j * vec + lane
            p = jnp.where(p >= n_tok, p - n_tok, p)
            tok_v[pl.ds(j * vec, vec)] = jnp.where(p >= n_tok, p - n_tok, p)

        @pl.loop(0, n_assign // vec)
        def _(j):
            s = slots_v[pl.ds(j * vec, vec)] - base
            mine = (s >= 0) & (s < per_worker)
            tok = lax.shift_right_logical(j * vec + lane, 1)
            plsc.store_scatter(tok_v, [jnp.where(mine, s, 0)], tok, mask=mine)

        @pl.loop(0, per_worker // chunk)
        def _(j):
            off = pl.multiple_of(j * chunk, 8)
            pltpu.async_copy(table_hbm.at[tok_v.at[pl.ds(off, chunk)]], rows_v, sem).wait()
            pltpu.sync_copy(rows_v, out_hbm.at[pl.ds(base + off, chunk)])

    return dispatch(table, slot)


def _moe_kernel(be_ref, nu_ref, nv_ref, x_ref, w1g_ref, w1u_ref, w2_ref, y_ref, xb_ref, acc_ref):
    i, j = pl.program_id(0), pl.program_id(1)
    bm = x_ref.shape[0]

    def swiglu_rows(rows):
        x = xb_ref[0:rows, :]
        a = (_silu(_dot(x, w1g_ref[0].astype(BF16))) * _dot(x, w1u_ref[0].astype(BF16))).astype(BF16)
        acc_ref[0:rows, :] += _dot(a, w2_ref[0].astype(BF16))

    @pl.when(i < nu_ref[0])
    def _():
        @pl.when(j == 0)
        def _():
            acc_ref[...] = jnp.zeros_like(acc_ref)
            xb_ref[...] = _unpack_rows(x_ref[...]).astype(BF16)

        @pl.when(nv_ref[i] > bm // 2)
        def _():
            swiglu_rows(bm)

        @pl.when(nv_ref[i] <= bm // 2)
        def _():
            swiglu_rows(bm // 2)

        @pl.when(j == pl.num_programs(1) - 1)
        def _():
            y_ref[...] = _pack_rows(acc_ref[...])


def _moe_experts(xs, w1, w2, block_e, n_used, n_valid, n_blocks):
    bm, fc = MOE_ROWS, MOE_FF
    nff = FF_EXPERT // fc
    used = lambda i, nu: jnp.minimum(i, nu[0] - 1)
    ffi = lambda i, j, nu: jnp.where(i < nu[0], j, nff - 1)
    grid_spec = pltpu.PrefetchScalarGridSpec(
        num_scalar_prefetch=3,
        grid=(n_blocks, nff),
        in_specs=[pl.BlockSpec((bm, D_MODEL // 2), lambda i, j, be, nu, nv: (used(i, nu), 0)),
                  pl.BlockSpec((1, D_MODEL, fc), lambda i, j, be, nu, nv: (be[used(i, nu)], 0, ffi(i, j, nu))),
                  pl.BlockSpec((1, D_MODEL, fc),
                               lambda i, j, be, nu, nv: (be[used(i, nu)], 0, nff + ffi(i, j, nu))),
                  pl.BlockSpec((1, fc, D_MODEL), lambda i, j, be, nu, nv: (be[used(i, nu)], ffi(i, j, nu), 0))],
        out_specs=pl.BlockSpec((bm, D_MODEL // 2), lambda i, j, be, nu, nv: (used(i, nu), 0)),
        scratch_shapes=[pltpu.VMEM((bm, D_MODEL), BF16), pltpu.VMEM((bm, D_MODEL), F32)],
    )
    return pl.pallas_call(
        _moe_kernel,
        grid_spec=grid_spec,
        out_shape=jax.ShapeDtypeStruct((n_blocks * bm, D_MODEL // 2), jnp.uint32),
        compiler_params=_cparams("arbitrary", "arbitrary"),
        name="moe_experts",
    )(block_e, n_used, n_valid, xs, w1, w1, w2)


def _combine_kernel(y0_ref, y1_ref, gate_ref, x_ref, modl_ref, g_ref, o_ref):
    gt = gate_ref[...]
    y = gt[:, 0:1] * _unpack_rows(y0_ref[...]) + gt[:, 1:2] * _unpack_rows(y1_ref[...])
    o_ref[...] = x_ref[...] + modl_ref[0, 5:6, :] * _rms(y, g_ref[...])


def _combine(yk, gates, x2, modl, g, bsz, t, ctx_len):
    tm = LATENT_TILE
    nlt = (t - ctx_len) // tm
    nl = bsz * (t - ctx_len)
    orow = lambda c: pl.BlockSpec((tm, c), lambda b, i: (b * nlt + i, 0))
    second = pl.BlockSpec((tm, D_MODEL // 2), lambda b, i: (nl // tm + b * nlt + i, 0))
    return pl.pallas_call(
        _combine_kernel,
        grid=(bsz, nlt),
        in_specs=[orow(D_MODEL // 2), second, orow(LANES),
                  _latent_rows(tm, t, ctx_len),
                  pl.BlockSpec((1, 8, D_MODEL), lambda b, i: (b, 0, 0)), _resident((1, D_MODEL))],
        out_specs=orow(D_MODEL),
        out_shape=jax.ShapeDtypeStruct((nl, D_MODEL), F32),
        compiler_params=_cparams("arbitrary", "arbitrary"),
        name="moe_combine",
    )(yk, yk, gates, x2, modl, g)


def _routing_tables(experts, ranks, counts):
    bm = MOE_ROWS
    n_assign = experts.size
    padded = (counts + bm - 1) // bm * bm
    ends_p = jnp.cumsum(padded)
    pstarts = ends_p - padded
    onehot = experts[..., None] == jnp.arange(N_EXPERTS, dtype=I32)
    slot = (jnp.sum(jnp.where(onehot, pstarts, 0), axis=-1) + ranks).astype(I32)
    n_blocks = n_assign // bm + N_EXPERTS
    block_e = jnp.minimum(jnp.sum(ends_p[None, :] <= (jnp.arange(n_blocks, dtype=I32) * bm)[:, None], axis=1),
                          N_EXPERTS - 1).astype(I32)
    n_used = (ends_p[-1] // bm).astype(I32).reshape(1)
    sel = block_e[:, None] == jnp.arange(N_EXPERTS, dtype=I32)
    rows_left = jnp.sum(jnp.where(sel, counts + pstarts, 0), axis=1) - jnp.arange(n_blocks, dtype=I32) * bm
    n_valid = jnp.clip(rows_left, 0, bm).astype(I32)
    return slot, block_e, n_used, n_valid, n_blocks


def kernel(x, c, ctx, c_ctx, mod_w, mod_b, norm_g, ab_in_w, m_conv_w, m_conv_b, m_gate_b, m_norm_w, l_conv_w,
           l_conv_b, l_wa, l_ba, l_wx, l_bx, l_lam, ab_out_w, ffn_w1, ffn_w2, cd_in_w, g_alpha_w, g_alpha_b,
           g_norm_w, s_conv_w, s_conv_b, s_dt_bias, s_A_log, s_D, s_norm_w, cd_out_w, router_w, router_b,
           moe_w1, moe_w2):
    bsz, seq, dm = x.shape
    ctx_len = ctx.shape[1]
    t = ctx_len + seq
    n = bsz * t
    assert ctx_len == ROW_TILE and seq % ROW_TILE == 0 and ROW_TILE % GRID_W == 0
    assert n % (IN_PARTS * ROW_TILE) == 0

    c_all = jnp.concatenate([c, c_ctx[None, :], jnp.zeros((-(bsz + 1) % 8, dm), F32)], axis=0)
    mods = _modulation(c_all, mod_w, mod_b).reshape(mod_w.shape[0], c_all.shape[0], 6, dm)
    pad2 = jnp.zeros((bsz, 2, dm), F32)

    def layer_mods(layer):
        modl = jnp.concatenate([mods[layer, :bsz], pad2], axis=1)
        modc = jnp.concatenate([mods[layer, bsz], pad2[0]], axis=0)
        return modl, modc

    modl, modc = layer_mods(0)
    g = norm_g[0]
    w = ab_in_w[0]
    o_qk, o_v, o_o, o_gt, o_lx, o_lg = 0, 1024, 2048, 3072, 3088, 4112
    w_in = jnp.concatenate([w[:, o_qk:o_v], w[:, o_v:o_o], w[:, o_o:o_gt], w[:, o_lx:o_lg], w[:, o_lg:],
                            w[:, o_gt:o_lx], jnp.zeros((dm, LANES - 16), F32)], axis=1).astype(BF16)
    gb = jnp.concatenate([m_gate_b[0].reshape(1, 16), jnp.zeros((1, LANES - 16), F32)], axis=1)
    x2, q, k, v, og, lx, glg, gates, gates_t = _in_ab(
        ctx.reshape(bsz * ctx_len, dm), x.reshape(bsz * seq, dm), modl, modc, g[0:1], w_in, m_conv_w[0],
        m_conv_b[0][None], l_conv_w[0], l_conv_b[0][None], gb, bsz, t, ctx_len)
    hm_f, hm_b = _mlstm(q, k, v, gates, gates_t, bsz, t, ctx_len)
    hl_f, hl_b = _lru(lx, l_wa[0].astype(BF16), l_wx[0].astype(BF16), l_ba[0][:, None], l_bx[0][:, None],
                      l_lam[0][:, None], bsz, t, ctx_len)
    x2 = _out_proj(_out_ab_kernel, "out_proj_swiglu_ab", [hm_f, hm_b, og, hl_f, hl_b, glg], [m_norm_w[0][None]],
                   x2, modl, modc, g[1:2], ab_out_w[0].astype(BF16), bsz, t, ctx_len,
                   extra=(g[2:3], g[3:4], ffn_w1[0].astype(BF16), ffn_w2[0].astype(BF16)))

    modl, modc = layer_mods(1)
    g = norm_g[1]
    w = cd_in_w[0]
    o_gv, o_gr, o_ga, o_z, o_xbc, o_dt = 1024, 2048, 3072, 3104, 4128, 5664
    zpad = jnp.zeros((dm, LANES - 32), F32)
    w_in = jnp.concatenate([w[:, :o_gv], w[:, o_gv:o_gr], w[:, o_gr:o_ga], w[:, o_z:o_xbc], w[:, o_xbc:o_dt],
                            w[:, o_ga:o_z], zpad, w[:, o_dt:], w[:, o_dt:], jnp.zeros((dm, LANES - 64), F32)],
                           axis=1).astype(BF16)
    dtb = jnp.concatenate([s_dt_bias[0].reshape(1, 32), s_dt_bias[0].reshape(1, 32),
                           jnp.zeros((1, LANES - 64), F32)], axis=1)
    aneg = jnp.concatenate([jnp.zeros((1, 32), F32), -jnp.exp(s_A_log[0].reshape(1, 32)),
                            jnp.zeros((1, LANES - 64), F32)], axis=1)
    gq, gk, gv, grs, zs, sx, sb, sc, ga, dts, dts_t = _in_cd(x2, modl, modc, g[0:1], w_in, s_conv_w[0],
                                                             s_conv_b[0][None], dtb, aneg, bsz, t, ctx_len)
    aw = jnp.zeros((2, LANES, G_HEADS * G_DK), F32)
    aw = aw.at[0, 0:G_RANK].set(g_alpha_w[0, 0]).at[1, G_RANK:2 * G_RANK].set(g_alpha_w[0, 1])
    awh = aw.astype(BF16)
    awl = (aw - awh.astype(F32)).astype(BF16)
    og_f, og_b = _gla(gq, gk, gv, ga, awh, awl, g_alpha_b[0][:, None], bsz, t, ctx_len)
    ys_f, ys_b = _ssd(sx, sb, sc, dts, dts_t, bsz, t, ctx_len)
    dskip = jnp.repeat(s_D[0], S_P)[None]
    x2 = _out_proj(_out_cd_kernel, "out_proj_cd", [og_f, og_b, grs, ys_f, ys_b, sx, zs],
                   [g_norm_w[0][None], dskip, s_norm_w[0][None]],
                   x2, modl, modc, g[1:2], cd_out_w[0].astype(BF16), bsz, t, ctx_len)

    rw = jnp.concatenate([router_w[0], jnp.zeros((dm, LANES - N_EXPERTS), F32)], axis=1)
    rwh = rw.astype(BF16)
    rw = jnp.concatenate([rwh, (rw - rwh.astype(F32)).astype(BF16)], axis=1)
    rb = jnp.concatenate([router_b[0][None], jnp.zeros((1, LANES - N_EXPERTS), F32)], axis=1)
    h2, top_idx, top_gate, counts = _router(x2, modl, g[2:3], rw, rb, bsz, t, ctx_len)
    slot, block_e, n_used, n_valid, n_blocks = _routing_tables(top_idx[:, 0:TOP_K], top_idx[:, TOP_K:2 * TOP_K],
                                                               counts[0, :N_EXPERTS].astype(I32))
    xs = _dispatch_rows(h2, slot.reshape(-1), n_blocks * MOE_ROWS)
    ys_e = _moe_experts(xs, moe_w1[0], moe_w2[0], block_e, n_used, n_valid, n_blocks)
    yk = _gather_rows(ys_e, jnp.concatenate([slot[:, 0], slot[:, 1]]))
    out = _combine(yk, top_gate, x2, modl, g[3:4], bsz, t, ctx_len)
    return out.reshape(bsz, seq, dm)
```

```python
import functools

import jax
import jax.numpy as jnp
from jax import lax
from jax.experimental import pallas as pl
from jax.experimental.pallas import tpu as pltpu
from jax.experimental.pallas import tpu_sc as plsc

F32, BF16, I32 = jnp.float32, jnp.bfloat16, jnp.int32

D_MODEL = 1024
GRID_W = 64
EPS = 1e-6
LOG2E = 1.4426950408889634
M_HEADS, M_DK, M_DV = 4, 128, 256
LRU_BLOCKS, LRU_BW, LRU_C = 8, 128, 8.0
G_HEADS, G_DK, G_DV, G_RANK, G_TAU = 4, 128, 256, 16, 16.0
S_HEADS, S_P, S_N, S_GROUPS, S_HPG = 16, 64, 128, 2, 8
N_EXPERTS, TOP_K, FF_EXPERT = 8, 2, 3584

LANES, SUBLANES = 128, 8
VMEM_LIMIT = 56 * 1024 * 1024
ROW_TILE = 256
IN_PARTS = 2
LATENT_TILE = 512
BIG_ROW_TILE_MAX = 640
PROJ_COLS = 256
M_CHUNK = 256
G_CHUNK, G_SUB = 256, 16
G_GROUP = 64
S_CHUNK = 256
LRU_TILE = 256
FF_STEP = 256
MOE_ROWS = 1024
MOE_FF = 512
SC_CORES, SC_SUBCORES = 2, 16
SC_LANES = 16
SC_GATHER_ROWS = 128


def _cparams(*sem):
    return pltpu.CompilerParams(dimension_semantics=sem, vmem_limit_bytes=VMEM_LIMIT)


def _sigmoid(x):
    return 0.5 + 0.5 * jnp.tanh(0.5 * x)


def _silu(x):
    return x * _sigmoid(x)


def _softplus(x):
    return jnp.maximum(x, 0.0) + jnp.log1p(jnp.exp(-jnp.abs(x)))


def _log_sigmoid(x):
    return jnp.minimum(x, 0.0) - jnp.log1p(jnp.exp(-jnp.abs(x)))


def _dot(a, b):
    return jnp.dot(a, b, preferred_element_type=F32)


def _dot_nt(a, b):
    return lax.dot_general(a, b, (((1,), (1,)), ((), ())), preferred_element_type=F32)


def _dot_tn(a, b):
    return lax.dot_general(a, b, (((0,), (0,)), ((), ())), preferred_element_type=F32)


def _split3(f):
    f1 = f.astype(BF16)
    r = f - f1.astype(F32)
    f2 = r.astype(BF16)
    f3 = (r - f2.astype(F32)).astype(BF16)
    return f1, f2, f3


def _sel_cols(mask01, f):
    p1, p2, p3 = _split3(f)
    return _dot(mask01, p1) + _dot(mask01, p2) + _dot(mask01, p3)


def _sel_rows(f, mask01):
    p1, p2, p3 = _split3(f)
    return _dot_nt(p1, mask01) + _dot_nt(p2, mask01) + _dot_nt(p3, mask01)


def _pack_rows(u):
    n = u.shape[1] // 2
    bits = lax.bitcast_convert_type(u.astype(BF16).astype(F32), jnp.uint32)
    return jnp.bitwise_or(bits[:, :n], jnp.right_shift(bits[:, n:], jnp.uint32(16)))


def _unpack_rows(p):
    hi = lax.bitcast_convert_type(jnp.bitwise_and(p, jnp.uint32(0xFFFF0000)), F32)
    lo = lax.bitcast_convert_type(jnp.left_shift(p, jnp.uint32(16)), F32)
    return jnp.concatenate([hi, lo], axis=1)


def _rms(u, g):
    return u * lax.rsqrt(jnp.mean(u * u, axis=-1, keepdims=True) + EPS) * g


def _head_rms(u, g, n_heads):
    w = u.shape[-1] // n_heads
    return jnp.concatenate([_rms(u[:, h * w:(h + 1) * w], g[:, h * w:(h + 1) * w]) for h in range(n_heads)], axis=1)


def _row_ids(tile_idx, tm):
    return tile_idx * tm + lax.broadcasted_iota(I32, (tm, 1), 0)


def _mod(modl_ref, modc_ref, idx, is_ctx):
    return jnp.where(is_ctx, modc_ref[idx:idx + 1, :], modl_ref[0, idx:idx + 1, :])


def _adaln(x, g, modl_ref, modc_ref, shift_idx, is_ctx):
    shift = _mod(modl_ref, modc_ref, shift_idx, is_ctx)
    scale = _mod(modl_ref, modc_ref, shift_idx + 1, is_ctx)
    return _rms(x, g) * (1.0 + scale) + shift


def _neighbour_tiles(r, step, is_lat):
    n_tiles, per_seg = r.shape[0], GRID_W // SUBLANES
    zero = jnp.zeros((1,) + r.shape[1:], r.dtype)

    def across(src):
        return jnp.where(is_lat, 0.0, r[src:src + 1]) if 0 <= src < n_tiles else zero

    pieces = []
    for s0 in range(0, n_tiles, per_seg):
        if step > 0:
            pieces += [across(s0 - 1), r[s0:s0 + per_seg - 1]]
        else:
            pieces += [r[s0 + 1:s0 + per_seg], across(s0 + per_seg)]
    return jnp.concatenate(pieces, axis=0)


def _dwconv(y, cw, cb, is_lat):
    tm, n = y.shape
    y3 = y.reshape(tm // SUBLANES, SUBLANES, n)
    sub = lax.broadcasted_iota(I32, (1, SUBLANES, 1), 1)
    w = lambda j: cw[j:j + 1, :].reshape(1, 1, n)
    r1, r2, r7 = pltpu.roll(y3, 1, 1), pltpu.roll(y3, 2, 1), pltpu.roll(y3, SUBLANES - 1, 1)
    ym1 = jnp.where(sub >= 1, r1, _neighbour_tiles(r1, 1, is_lat))
    ym2 = jnp.where(sub >= 2, r2, _neighbour_tiles(r2, 1, is_lat))
    yp1 = jnp.where(sub <= SUBLANES - 2, r7, _neighbour_tiles(r7, -1, is_lat))
    out = cb.reshape(1, 1, n) + w(0) * ym2 + w(1) * ym1 + w(2) * y3 + w(3) * yp1
    return out.reshape(tm, n)


def _resident(shape):
    nd = len(shape)
    return pl.BlockSpec(shape, lambda *_: (0,) * nd, pipeline_mode=pl.Buffered(1))


def _big_row_tile(t):
    return max(tm for tm in range(8, BIG_ROW_TILE_MAX + 1, 8) if t % tm == 0)


def _mod_kernel(c_ref, w_ref, b_ref, o_ref):
    a1, a2, _ = _split3(_silu(c_ref[...]))
    w = w_ref[0]
    wh = w.astype(BF16)
    wl = (w - wh.astype(F32)).astype(BF16)
    o_ref[0] = _dot(a1, wh) + _dot(a2, wh) + _dot(a1, wl) + b_ref[0]


def _modulation(c_all, mod_w, mod_b):
    depth, d, n6 = mod_w.shape
    rows = c_all.shape[0]
    tn = 1024
    return pl.pallas_call(
        _mod_kernel,
        grid=(depth, n6 // tn),
        in_specs=[pl.BlockSpec((rows, d), lambda l, j: (0, 0)),
                  pl.BlockSpec((1, d, tn), lambda l, j: (l, 0, j)),
                  pl.BlockSpec((1, 1, tn), lambda l, j: (l, 0, j))],
        out_specs=pl.BlockSpec((1, rows, tn), lambda l, j: (l, 0, j)),
        out_shape=jax.ShapeDtypeStruct((depth, rows, n6), F32),
        compiler_params=_cparams("arbitrary", "arbitrary"),
        name="modulation",
    )(c_all, mod_w, mod_b.reshape(depth, 1, n6))


def _project(h_ref, w_ref, w_col, out_ref, epilogue):
    for c in range(0, out_ref.shape[1], PROJ_COLS):
        y = _dot(h_ref[...], w_ref[:, w_col + c:w_col + c + PROJ_COLS])
        out_ref[:, c:c + PROJ_COLS] = epilogue(y, c).astype(out_ref.dtype)


def _part_is_latent(part, parts_per_batch):
    return lax.rem(pl.program_id(0) * IN_PARTS + part, parts_per_batch) != 0


def _conv_parts(y, cw, cb, lat):
    return jnp.concatenate([_dwconv(y[p * ROW_TILE:(p + 1) * ROW_TILE], cw, cb, lat[p]) for p in range(IN_PARTS)],
                           axis=0)


def _in_ab_kernel(parts_per_batch, *refs):
    xs, refs = refs[:3 * IN_PARTS], refs[3 * IN_PARTS:]
    modc_ref, g_ref, w_ref, mcw_ref, mcb_ref, lcw_ref, lcb_ref, gb_ref = refs[:8]
    x_ref, q_ref, k_ref, v_ref, og_ref, lx_ref, glg_ref, gates_ref, gates_t_ref, h_scr = refs[8:]
    lat = [_part_is_latent(p, parts_per_batch) for p in range(IN_PARTS)]
    for p in range(IN_PARTS):
        xc_ref, xl_ref, modl_ref = xs[3 * p:3 * p + 3]
        rows = slice(p * ROW_TILE, (p + 1) * ROW_TILE)
        xp = jnp.where(lat[p], xl_ref[...], xc_ref[...])
        x_ref[rows, :] = xp
        h_scr[rows, :] = _adaln(xp, g_ref[...], modl_ref, modc_ref, 0, jnp.logical_not(lat[p])).astype(BF16)

    def conv(cw_ref, cb_ref):
        return lambda y, c: _conv_parts(y, cw_ref[:, c:c + PROJ_COLS], cb_ref[:, c:c + PROJ_COLS], lat)

    qc, kc, lc = conv(mcw_ref, mcb_ref), conv(mcw_ref, mcb_ref), conv(lcw_ref, lcb_ref)
    _project(h_scr, w_ref, 0, q_ref, lambda y, c: _silu(qc(y, c)))
    _project(h_scr, w_ref, 512, k_ref, lambda y, c: _silu(kc(y, 512 + c)) * (M_DK ** -0.5))
    _project(h_scr, w_ref, 1024, v_ref, lambda y, c: y)
    _project(h_scr, w_ref, 2048, og_ref, lambda y, c: _sigmoid(y))
    _project(h_scr, w_ref, 3072, lx_ref, lc)
    _project(h_scr, w_ref, 4096, glg_ref, lambda y, c: jax.nn.gelu(y))
    gt = _dot(h_scr[...], w_ref[:, 5120:5248]) + gb_ref[...]
    lane = lax.broadcasted_iota(I32, gt.shape, 1)
    gates = jnp.where(lane >= 2 * M_HEADS, _log_sigmoid(gt), gt)
    gates_ref[...] = gates
    gates_t_ref[...] = gates.T


def _in_ab(xc2, xl2, modl, modc, g, w, mcw, mcb, lcw, lcb, gb, bsz, t, ctx_len):
    n = bsz * t
    tm = IN_PARTS * ROW_TILE
    ppb = t // ROW_TILE
    lpb = ppb - 1
    part = lambda j, p: j * IN_PARTS + p
    batch = lambda j, p: part(j, p) // ppb
    x_specs, x_args = [], []
    for p in range(IN_PARTS):
        x_specs += [pl.BlockSpec((ROW_TILE, D_MODEL), lambda j, p=p: (batch(j, p), 0)),
                    pl.BlockSpec((ROW_TILE, D_MODEL),
                                 lambda j, p=p: (batch(j, p) * lpb + jnp.maximum(part(j, p) % ppb - 1, 0), 0)),
                    pl.BlockSpec((1, 8, D_MODEL), lambda j, p=p: (batch(j, p), 0, 0))]
        x_args += [xc2, xl2, modl]
    row = lambda c: pl.BlockSpec((tm, c), lambda j: (j, 0))
    outs = [(D_MODEL, F32), (512, BF16), (512, BF16), (1024, BF16), (1024, BF16), (1024, BF16), (1024, BF16),
            (LANES, F32)]
    params = (modc, g, w, mcw, mcb, lcw, lcb, gb)
    return pl.pallas_call(
        functools.partial(_in_ab_kernel, ppb),
        grid=(n // tm,),
        in_specs=x_specs + [_resident(a.shape) for a in params],
        out_specs=[row(c) for c, _ in outs] + [pl.BlockSpec((LANES, tm), lambda j: (0, j))],
        out_shape=[jax.ShapeDtypeStruct((n, c), dt) for c, dt in outs] + [jax.ShapeDtypeStruct((LANES, n), F32)],
        scratch_shapes=[pltpu.VMEM((tm, D_MODEL), BF16)],
        compiler_params=_cparams("arbitrary"),
        name="in_proj_ab",
    )(*x_args, *params)


def _reverse_order(n_ctx_chunks, n_chunks):
    return lambda c: jnp.where(c < n_ctx_chunks, n_ctx_chunks - 1 - c, n_chunks - 1 - (c - n_ctx_chunks))


def _tri_mask(L, reverse):
    row = lax.broadcasted_iota(I32, (L, L), 0)
    col = lax.broadcasted_iota(I32, (L, L), 1)
    return (col >= row) if reverse else (col <= row)


def _bidir_scan(kern, L, row_ins, lane_ins, res_ins, out_cols, scratch, bsz, t, ctx_len, name):
    nc = t // L
    rev = _reverse_order(ctx_len // L, nc)
    fwd = lambda i: i
    n = bsz * t

    def row(c, order):
        return pl.BlockSpec((L, c), lambda b, i: (b * nc + order(i), 0))

    def lanes(r, order):
        return pl.BlockSpec((r, L), lambda b, i: (0, b * nc + order(i)))

    in_specs, args = [], []
    for order in (fwd, rev):
        in_specs += [row(a.shape[1], order) for a in row_ins] + [lanes(r, order) for _, r in lane_ins]
        args += list(row_ins) + [a for a, _ in lane_ins]
    in_specs += [_resident(a.shape) for a in res_ins]
    args += list(res_ins)
    return pl.pallas_call(
        kern,
        grid=(bsz, nc),
        in_specs=in_specs,
        out_specs=[row(c, fwd) for c in out_cols] + [row(c, rev) for c in out_cols],
        out_shape=[jax.ShapeDtypeStruct((n, c), BF16) for c in out_cols] * 2,
        scratch_shapes=scratch,
        compiler_params=_cparams("arbitrary", "arbitrary"),
        name=name,
    )(*args)


def _zero_at_start(*scratch):
    @pl.when(pl.program_id(1) == 0)
    def _():
        for s in scratch:
            s[...] = jnp.zeros_like(s)


def _mlstm_kernel(L, qf, kf, vf, gcf, grf, qb, kb, vb, gcb, grb, of, ob, c_scr, m_scr):
    _zero_at_start(c_scr, m_scr)
    ones = jnp.ones((L, LANES), BF16)
    wide = lambda c, k: jnp.concatenate([c] * k, axis=1)
    er = lax.broadcasted_iota(I32, (LANES, 2 * M_HEADS * LANES), 0)
    ej = jnp.right_shift(lax.broadcasted_iota(I32, (LANES, 2 * M_HEADS * LANES), 1), LANES.bit_length() - 1)
    chains = []
    for d, (q_ref, k_ref, v_ref, gc_ref, gr_ref, out_ref) in enumerate(((qf, kf, vf, gcf, grf, of),
                                                                         (qb, kb, vb, gcb, grb, ob))):
        reverse = d == 1
        mask = _tri_mask(L, reverse)
        mask01 = mask.astype(BF16)
        gc = gc_ref[...]
        gr = gr_ref[...]
        bcol = _sel_cols(mask01, gc)
        brow = _sel_rows(gr, mask01)
        glane = lax.broadcasted_iota(I32, (L, LANES), 1)
        src = jnp.where(ej < M_HEADS, M_HEADS * d + ej, M_HEADS + M_HEADS * d + ej)
        s1, s2, _ = _split3(jnp.where(glane < 2 * M_HEADS, gc, bcol))
        sel = (er == src).astype(BF16)
        rep = _dot(s1, sel) + _dot(s2, sel)
        last = 0 if reverse else L - 1
        for h in range(M_HEADS):
            fo, io = 2 * M_HEADS + M_HEADS * d + h, M_HEADS * d + h
            ic, bc = rep[:, h * LANES:(h + 1) * LANES], rep[:, (M_HEADS + h) * LANES:(M_HEADS + h + 1) * LANES]
            r1 = gr[io:io + 1, :] - brow[fo:fo + 1, :]
            m = m_scr[d, h:h + 1, :]
            g = bc + m
            mx = jnp.max(jnp.where(mask, r1, -jnp.inf), axis=1, keepdims=True)
            mt = jnp.maximum(g, bc + mx)
            p = jnp.exp(jnp.where(mask, wide(bc - mt, L // LANES) + r1, -jnp.inf))
            bl = bc[last:last + 1, :]
            a_s = bl - bc + ic
            m_new = jnp.maximum(bl + m, jnp.max(a_s, axis=0, keepdims=True))
            chains.append(dict(
                d=d, h=h, out_ref=out_ref, mt=mt, w_inter=jnp.exp(g - mt), p=p,
                m_new=m_new, decay=jnp.exp(bl + m - m_new), ws=jnp.exp(a_s - m_new),
                qh=q_ref[:, h * M_DK:(h + 1) * M_DK], kh=k_ref[:, h * M_DK:(h + 1) * M_DK],
                vx=jnp.concatenate([v_ref[:, h * M_DV:(h + 1) * M_DV], ones], axis=1)))
    for ch in chains:
        ch["sc"] = (_dot_nt(ch["qh"], ch["kh"]) * ch["p"]).astype(BF16)
    for ch in chains:
        cx = c_scr[ch["d"], ch["h"]]
        numx = wide(ch["w_inter"], 1 + M_DV // LANES) * _dot(ch["qh"], cx.astype(BF16)) + _dot(ch["sc"], ch["vx"])
        den = numx[:, M_DV:]
        rinv = 1.0 / jnp.maximum(jnp.abs(den), jnp.exp(-ch["mt"]))
        hh = numx[:, 0:M_DV] * wide(rinv, M_DV // LANES)
        ch["out_ref"][:, ch["h"] * M_DV:(ch["h"] + 1) * M_DV] = hh.astype(BF16)
    for ch in chains:
        d, h = ch["d"], ch["h"]
        kw = (ch["kh"].astype(F32) * ch["ws"]).astype(BF16)
        c_scr[d, h] = wide(ch["decay"], 1 + M_DV // LANES) * c_scr[d, h] + _dot_tn(kw, ch["vx"])
        m_scr[d, h:h + 1, :] = ch["m_new"]


def _mlstm(q, k, v, gcol, grow, bsz, t, ctx_len):
    L = M_CHUNK
    scratch = [pltpu.VMEM((2, M_HEADS, M_DK, M_DV + LANES), F32), pltpu.VMEM((2, 8, LANES), F32)]
    return _bidir_scan(functools.partial(_mlstm_kernel, L), L, [q, k, v, gcol], [(grow, 16)], [], [1024], scratch,
                       bsz, t, ctx_len, "mlstm_scan")


def _lru_kernel(L, uf, ub_, wa_ref, wx_ref, ba_ref, bx_ref, lam_ref, of, ob, a_scr, b_scr, h_scr):
    _zero_at_start(h_scr)
    R, C = L // SUBLANES, LRU_BLOCKS * LRU_BW
    i0, i1 = lax.broadcasted_iota(I32, (L, L), 0), lax.broadcasted_iota(I32, (L, L), 1)
    when = lambda p: jnp.bitwise_and(p, SUBLANES - 1) * R + jnp.right_shift(p, SUBLANES.bit_length() - 1)
    regroup = (i1 == when(i0)).astype(BF16)
    ungroup = (i0 == when(i1)).astype(BF16)
    sub = lax.broadcasted_iota(I32, (SUBLANES, 1), 0)
    dirs = ((uf, of), (ub_, ob))
    for d, (u_ref, _) in enumerate(dirs):
        u = _dot(regroup, u_ref[...])
        ub = u.astype(BF16)
        sp = -LRU_C * _softplus(-lam_ref[d])
        for n in range(LRU_BLOCKS):
            sl = slice(n * LRU_BW, (n + 1) * LRU_BW)
            rg = _sigmoid(_dot(ub[:, sl], wa_ref[d, n]) + ba_ref[d, :, sl])
            ig = _sigmoid(_dot(ub[:, sl], wx_ref[d, n]) + bx_ref[d, :, sl])
            log_a = rg * sp[:, sl]
            a = jnp.exp(log_a)
            a_scr[d, :, sl] = a
            b_scr[d, :, sl] = jnp.sqrt(-jnp.tanh(log_a) * (a * a + 1.0)) * (ig * u[:, sl])
    S = SUBLANES
    hs = [jnp.zeros((S, C), F32), jnp.zeros((S, C), F32)]
    ps = [jnp.ones((S, C), F32), jnp.ones((S, C), F32)]
    for j in range(R):
        for d in range(2):
            rows = slice(S * j, S * j + S) if d == 0 else slice(S * (R - 1 - j), S * (R - j))
            a = a_scr[d, rows, :]
            hs[d] = a * hs[d] + b_scr[d, rows, :]
            ps[d] = a * ps[d]
            b_scr[d, rows, :] = hs[d]
            a_scr[d, rows, :] = ps[d]
    for d, (_, out_ref) in enumerate(dirs):
        c = h_scr[d:d + 1, :]
        cin = jnp.zeros((S, C), F32)
        for s in (range(S) if d == 0 else reversed(range(S))):
            cin = jnp.where(sub == s, c, cin)
            c = ps[d][s:s + 1, :] * c + hs[d][s:s + 1, :]
        h_scr[d:d + 1, :] = c
        h = b_scr[d].reshape(R, S, C) + a_scr[d].reshape(R, S, C) * cin[None]
        out_ref[...] = _dot(ungroup, h.reshape(L, C).astype(BF16)).astype(BF16)


def _lru(u, wa, wx, ba, bx, lam, bsz, t, ctx_len):
    L = LRU_TILE
    scratch = [pltpu.VMEM((2, L, 1024), F32), pltpu.VMEM((2, L, 1024), F32), pltpu.VMEM((8, 1024), F32)]
    return _bidir_scan(functools.partial(_lru_kernel, L), L, [u], [], [wa, wx, ba, bx, lam], [1024], scratch,
                       bsz, t, ctx_len, "rglru_scan")


def _swiglu_rows(x, is_ctx, modl_ref, modc_ref, g2_ref, g3_ref, w1_ref, w2_ref):
    h = _adaln(x, g2_ref[...], modl_ref, modc_ref, 3, is_ctx).astype(BF16)
    ff = w2_ref.shape[0]
    acc = jnp.zeros(x.shape, F32)
    for j in range(ff // FF_STEP):
        lo = j * FF_STEP
        gj = _dot(h, w1_ref[:, lo:lo + FF_STEP])
        uj = _dot(h, w1_ref[:, ff + lo:ff + lo + FF_STEP])
        acc = acc + _dot((_silu(gj) * uj).astype(BF16), w2_ref[lo:lo + FF_STEP, :])
    return x + _mod(modl_ref, modc_ref, 5, is_ctx) * _rms(acc, g3_ref[...])


def _out_tail(tm, ctx_len, a1, a2, x_ref, modl_ref, modc_ref, g_ref, w_ref, xo_ref, ffn_refs=None):
    is_ctx = _row_ids(pl.program_id(1), tm) < ctx_len
    half = w_ref.shape[0] // 2
    y = _dot(a1.astype(BF16), w_ref[0:half, :]) + _dot(a2.astype(BF16), w_ref[half:, :])
    x = x_ref[...] + _mod(modl_ref, modc_ref, 2, is_ctx) * _rms(y, g_ref[...])
    if ffn_refs is not None:
        x = _swiglu_rows(x, is_ctx, modl_ref, modc_ref, *ffn_refs)
    xo_ref[...] = x


def _out_ab_kernel(tm, ctx_len, hf, hb, og, lf, lb, glg, nw, x_ref, modl_ref, modc_ref, g_ref, w_ref,
                   g2_ref, g3_ref, w1_ref, w2_ref, xo_ref):
    hm = _head_rms(hf[...].astype(F32) + hb[...].astype(F32), nw[...], M_HEADS) * og[...].astype(F32)
    hl = (lf[...].astype(F32) + lb[...].astype(F32)) * glg[...].astype(F32)
    _out_tail(tm, ctx_len, hm, hl, x_ref, modl_ref, modc_ref, g_ref, w_ref, xo_ref,
              (g2_ref, g3_ref, w1_ref, w2_ref))


def _out_cd_kernel(tm, ctx_len, gf, gb, grs, yf, yb, sx, zs, gnw, dsk, snw, x_ref, modl_ref, modc_ref, g_ref,
                   w_ref, xo_ref):
    og = _head_rms(gf[...].astype(F32) + gb[...].astype(F32), gnw[...], G_HEADS) * grs[...].astype(F32)
    ys = yf[...].astype(F32) + yb[...].astype(F32) + dsk[...] * sx[...].astype(F32)
    ys = _rms(ys * zs[...].astype(F32), snw[...])
    _out_tail(tm, ctx_len, og, ys, x_ref, modl_ref, modc_ref, g_ref, w_ref, xo_ref)


def _out_proj(kern, name, acts, vecs, x2, modl, modc, g, w, bsz, t, ctx_len, extra=()):
    tm = _big_row_tile(t)
    nt = t // tm
    row = lambda c: pl.BlockSpec((tm, c), lambda b, i: (b * nt + i, 0))
    return pl.pallas_call(
        functools.partial(kern, tm, ctx_len),
        grid=(bsz, nt),
        in_specs=[row(1024)] * len(acts) + [_resident((1, 1024))] * len(vecs)
        + [row(D_MODEL), pl.BlockSpec((1, 8, D_MODEL), lambda b, i: (b, 0, 0)),
           _resident((8, D_MODEL)), _resident((1, D_MODEL)), _resident(w.shape)]
        + [_resident(a.shape) for a in extra],
        out_specs=row(D_MODEL),
        out_shape=jax.ShapeDtypeStruct(x2.shape, F32),
        compiler_params=_cparams("arbitrary", "arbitrary"),
        name=name,
    )(*acts, *vecs, x2, modl, modc, g, w, *extra)


def _in_cd_kernel(parts_per_batch, x_ref, *refs):
    modls, refs = refs[:IN_PARTS], refs[IN_PARTS:]
    modc_ref, g_ref, w_ref, cw_ref, cb_ref, dtb_ref, aneg_ref = refs[:7]
    gq_ref, gk_ref, gv_ref, gr_ref, z_ref, sx_ref, sb_ref, sc_ref, ga_ref, dts_ref, dts_t_ref, h_scr = refs[7:]
    lat = [_part_is_latent(p, parts_per_batch) for p in range(IN_PARTS)]
    for p in range(IN_PARTS):
        rows = slice(p * ROW_TILE, (p + 1) * ROW_TILE)
        h_scr[rows, :] = _adaln(x_ref[rows, :], g_ref[...], modls[p], modc_ref, 0,
                                jnp.logical_not(lat[p])).astype(BF16)

    def conv_silu(col0):
        return lambda y, c: _silu(_conv_parts(y, cw_ref[:, col0 + c:col0 + c + PROJ_COLS],
                                              cb_ref[:, col0 + c:col0 + c + PROJ_COLS], lat))

    _project(h_scr, w_ref, 0, gq_ref, lambda y, c: y * (G_DK ** -0.5))
    _project(h_scr, w_ref, 512, gk_ref, lambda y, c: y)
    _project(h_scr, w_ref, 1024, gv_ref, lambda y, c: y)
    _project(h_scr, w_ref, 2048, gr_ref, lambda y, c: _silu(y))
    _project(h_scr, w_ref, 3072, z_ref, lambda y, c: _silu(y))
    _project(h_scr, w_ref, 4096, sx_ref, conv_silu(0))
    _project(h_scr, w_ref, 5120, sb_ref, conv_silu(1024))
    _project(h_scr, w_ref, 5376, sc_ref, conv_silu(1280))
    ga_ref[...] = _dot(h_scr[...], w_ref[:, 5632:5760])
    dt = _softplus(_dot(h_scr[...], w_ref[:, 5760:5888]) + dtb_ref[...])
    lane = lax.broadcasted_iota(I32, dt.shape, 1)
    dts = jnp.where(lane < 2 * S_HEADS, dt, dt * aneg_ref[...])
    dts_ref[...] = dts
    dts_t_ref[...] = dts.T


def _in_cd(x2, modl, modc, g, w, cw, cb, dtb, aneg, bsz, t, ctx_len):
    n = bsz * t
    tm = IN_PARTS * ROW_TILE
    ppb = t // ROW_TILE
    row = lambda c: pl.BlockSpec((tm, c), lambda j: (j, 0))
    mod_specs = [pl.BlockSpec((1, 8, D_MODEL), lambda j, p=p: ((j * IN_PARTS + p) // ppb, 0, 0))
                 for p in range(IN_PARTS)]
    outs = [(512, BF16), (512, BF16), (1024, BF16), (1024, BF16), (1024, BF16), (1024, BF16), (256, BF16),
            (256, BF16), (LANES, F32), (LANES, F32)]
    params = (modc, g, w, cw, cb, dtb, aneg)
    return pl.pallas_call(
        functools.partial(_in_cd_kernel, ppb),
        grid=(n // tm,),
        in_specs=[row(D_MODEL)] + mod_specs + [_resident(a.shape) for a in params],
        out_specs=[row(c) for c, _ in outs] + [pl.BlockSpec((LANES, tm), lambda j: (0, j))],
        out_shape=[jax.ShapeDtypeStruct((n, c), dt) for c, dt in outs] + [jax.ShapeDtypeStruct((LANES, n), F32)],
        scratch_shapes=[pltpu.VMEM((tm, D_MODEL), BF16)],
        compiler_params=_cparams("arbitrary"),
        name="in_proj_cd",
    )(x2, *([modl] * IN_PARTS), *params)


def _gla_kernel(L, C, qf, kf, vf, gaf, qb, kb, vb, gab, awh_ref, awl_ref, ab_ref, of, ob, s_scr, att_scr):
    _zero_at_start(s_scr, att_scr)
    nb = L // C
    chains = []
    for d, (q_ref, k_ref, v_ref, ga_ref, out_ref) in enumerate(((qf, kf, vf, gaf, of), (qb, kb, vb, gab, ob))):
        reverse = d == 1
        mask = _tri_mask(L, reverse)
        g1, g2, _ = _split3(ga_ref[...])
        pre = _dot(g1, awh_ref[d]) + _dot(g2, awh_ref[d]) + _dot(g1, awl_ref[d])
        lg = _log_sigmoid(pre + ab_ref[d]) * (1.0 / G_TAU)
        ball = _sel_cols(mask.astype(BF16), lg)
        for h in range(G_HEADS):
            ks = slice(h * G_DK, (h + 1) * G_DK)
            b = ball[:, ks] * LOG2E
            qh = q_ref[:, ks].astype(F32)
            inter = _dot_nt((qh * jnp.exp2(b)).astype(BF16), s_scr[d, h].astype(BF16))
            chains.append(dict(d=d, h=h, reverse=reverse, mask=mask, b=b, qh=qh, kh=k_ref[:, ks].astype(F32),
                               vh=v_ref[:, h * G_DV:(h + 1) * G_DV], inter=inter, out_ref=out_ref, blocks=[]))
    T = min(G_GROUP, L)
    for i in range(nb):
        lo, hi = i * C, (i + 1) * C
        t0 = lo // T * T
        for c, ch in enumerate(chains):
            b = ch["b"]
            if ch["reverse"]:
                bref = b[hi:hi + 1, :] if i < nb - 1 else jnp.zeros((1, G_DK), F32)
            else:
                bref = b[lo - 1:lo, :] if i > 0 else jnp.zeros((1, G_DK), F32)
            qi = (ch["qh"][lo:hi, :] * jnp.exp2(b[lo:hi, :] - bref)).astype(BF16)
            ki = (ch["kh"][t0:t0 + T] * jnp.exp2(bref - b[t0:t0 + T])).astype(BF16)
            att_scr[c, lo:hi, t0:t0 + T] = _dot_nt(qi, ki)
    for g in range(L // T):
        t0, t1 = g * T, (g + 1) * T
        for c, ch in enumerate(chains):
            b = ch["b"]
            e0, e1 = (t1, L) if ch["reverse"] else (0, t0)
            if e1 > e0:
                bref = b[t1:t1 + 1, :] if ch["reverse"] else b[t0 - 1:t0, :]
                qg = (ch["qh"][t0:t1] * jnp.exp2(b[t0:t1] - bref)).astype(BF16)
                kg = (ch["kh"][e0:e1] * jnp.exp2(bref - b[e0:e1])).astype(BF16)
                att_scr[c, t0:t1, e0:e1] = _dot_nt(qg, kg)
    for c, ch in enumerate(chains):
        att = jnp.where(ch["mask"], att_scr[c], 0.0).astype(BF16)
        oh = ch["inter"] + _dot(att, ch["vh"])
        ch["out_ref"][:, ch["h"] * G_DV:(ch["h"] + 1) * G_DV] = oh.astype(BF16)
    for ch in chains:
        d, h, b = ch["d"], ch["h"], ch["b"]
        last = 0 if ch["reverse"] else L - 1
        bl = b[last:last + 1, :]
        kd = (ch["kh"] * jnp.exp2(bl - b)).astype(BF16)
        s_scr[d, h] = s_scr[d, h] * jnp.exp2(bl) + _dot_tn(ch["vh"], kd)


def _gla(q, k, v, ga, awh, awl, ab, bsz, t, ctx_len):
    L, C = G_CHUNK, G_SUB
    scratch = [pltpu.VMEM((2, G_HEADS, G_DV, G_DK), F32), pltpu.VMEM((2 * G_HEADS, L, L), F32)]
    return _bidir_scan(functools.partial(_gla_kernel, L, C), L, [q, k, v, ga], [], [awh, awl, ab], [1024], scratch,
                       bsz, t, ctx_len, "gla_scan")


def _ssd_kernel(L, xf, bf, cf, gcf, grf, xb, bb, cb_, gcb, grb, of, ob, s_scr):
    _zero_at_start(s_scr)
    er = lax.broadcasted_iota(I32, (LANES, S_HEADS * S_P), 0)
    ec = jnp.right_shift(lax.broadcasted_iota(I32, (LANES, S_HEADS * S_P), 1), S_P.bit_length() - 1)
    glane = lax.broadcasted_iota(I32, (L, LANES), 1)
    lane = lax.broadcasted_iota(I32, (L, 2 * S_P), 1)
    dirs, chains = [], []
    for d, (x_ref, b_ref, c_ref, gc_ref, gr_ref, out_ref) in enumerate(((xf, bf, cf, gcf, grf, of),
                                                                         (xb, bb, cb_, gcb, grb, ob))):
        reverse = d == 1
        mask = _tri_mask(L, reverse)
        mask01 = mask.astype(BF16)
        gc = gc_ref[...]
        gr = gr_ref[...]
        bcol = _sel_cols(mask01, gc)
        brow = _sel_rows(gr, mask01)
        last = 0 if reverse else L - 1
        dt_o, la_o = S_HEADS * d, 2 * S_HEADS + S_HEADS * d
        sel_la = (er == ec + la_o).astype(BF16)
        x = x_ref[...]
        bla = jnp.where((glane >= la_o) & (glane < la_o + S_HEADS), bcol, 0.0)
        dt_at_la = pltpu.roll(gc, 2 * S_HEADS, 1)
        bl = bla[last:last + 1, :]
        ebx = _dot(jnp.exp(bla).astype(BF16), sel_la)
        wx = _dot((jnp.exp(bl - bla) * dt_at_la).astype(BF16), sel_la)
        dirs.append(dict(d=d, mask=mask, gr=gr, x=x, ebx=ebx, last=last, dt_o=dt_o, la_o=la_o, out_ref=out_ref,
                         xw=(x.astype(F32) * wx).astype(BF16),
                         bcol2=bcol * LOG2E, brow2=brow * LOG2E))
        for g in range(S_GROUPS):
            ns = slice(g * S_N, (g + 1) * S_N)
            chains.append(dict(dr=dirs[-1], g=g, bg=b_ref[:, ns], cg=c_ref[:, ns],
                               gs=slice(g * S_HPG * S_P, (g + 1) * S_HPG * S_P)))
    for ch in chains:
        dr = ch["dr"]
        ch["cb"] = _dot_nt(ch["cg"], ch["bg"])
        ch["inter"] = _dot(ch["cg"], s_scr[dr["d"], ch["g"]].astype(BF16)) * dr["ebx"][:, ch["gs"]]
    for ch in chains:
        dr, g = ch["dr"], ch["g"]
        parts = []
        for pr in range(S_HPG // 2):
            e0 = g * S_HPG + 2 * pr
            xp = dr["x"][:, e0 * S_P:(e0 + 2) * S_P]
            res = []
            for e in (e0, e0 + 1):
                col = dr["la_o"] + e
                bc, br = dr["bcol2"][:, col:col + 1], dr["brow2"][col:col + 1, :]
                dtr = dr["gr"][dr["dt_o"] + e:dr["dt_o"] + e + 1, :]
                att = (ch["cb"] * dtr * jnp.exp2(jnp.where(dr["mask"], bc - br, -jnp.inf))).astype(BF16)
                res.append(_dot(att, xp))
            parts.append(jnp.where(lane < S_P, res[0], res[1]))
        dr["out_ref"][:, ch["gs"]] = (jnp.concatenate(parts, axis=1) + ch["inter"]).astype(BF16)
    for ch in chains:
        dr, g, gs = ch["dr"], ch["g"], ch["gs"]
        s_scr[dr["d"], g] = (s_scr[dr["d"], g] * dr["ebx"][dr["last"]:dr["last"] + 1, gs]
                             + _dot_tn(ch["bg"], dr["xw"][:, gs]))


def _ssd(x, bm, cm, gcol, grow, bsz, t, ctx_len):
    L = S_CHUNK
    scratch = [pltpu.VMEM((2, S_GROUPS, S_N, S_HPG * S_P), F32)]
    return _bidir_scan(functools.partial(_ssd_kernel, L), L, [x, bm, cm, gcol], [(grow, 4 * S_HEADS)], [], [1024],
                       scratch,
                       bsz, t, ctx_len, "ssd_scan")


def _router_kernel(x_ref, modl_ref, g_ref, rw_ref, rb_ref, h_ref, idx_ref, gate_ref, cnt_ref, base_scr):
    @pl.when((pl.program_id(0) == 0) & (pl.program_id(1) == 0))
    def _():
        base_scr[...] = jnp.zeros_like(base_scr)

    h = _rms(x_ref[...], g_ref[...]) * (1.0 + modl_ref[0, 4:5, :]) + modl_ref[0, 3:4, :]
    h_ref[...] = _pack_rows(h)
    h1, h2, _ = _split3(h)
    both = _dot(h1, rw_ref[...])
    logits = both[:, :LANES] + both[:, LANES:] + _dot(h2, rw_ref[:, :LANES]) + rb_ref[...]
    lane = lax.broadcasted_iota(I32, logits.shape, 1)
    logits = jnp.where(lane < N_EXPERTS, logits, -jnp.inf)
    lanef = lane.astype(F32)
    m1 = jnp.max(logits, axis=1, keepdims=True)
    i1 = jnp.min(jnp.where(logits == m1, lanef, float(LANES)), axis=1, keepdims=True)
    rest = jnp.where(lanef == i1, -jnp.inf, logits)
    m2 = jnp.max(rest, axis=1, keepdims=True)
    i2 = jnp.min(jnp.where(rest == m2, lanef, float(LANES)), axis=1, keepdims=True)
    e = jnp.exp(m2 - m1)
    g1 = 1.0 / (1.0 + e)
    gate_ref[...] = jnp.where(lane == 0, g1, jnp.where(lane == 1, e * g1, 0.0))
    tm = logits.shape[0]
    oh1, oh2 = (lanef == i1).astype(F32), (lanef == i2).astype(F32)
    earlier = (lax.broadcasted_iota(I32, (tm, tm), 1) < lax.broadcasted_iota(I32, (tm, tm), 0)).astype(BF16)
    seen = base_scr[0:1, :] + _dot(earlier, (oh1 + oh2).astype(BF16))
    r1 = jnp.sum(oh1 * seen, axis=1, keepdims=True)
    r2 = jnp.sum(oh2 * seen, axis=1, keepdims=True)
    base_scr[0:1, :] = base_scr[0:1, :] + jnp.sum(oh1 + oh2, axis=0, keepdims=True)
    cnt_ref[...] = base_scr[...]
    idx_ref[...] = jnp.where(lane == 0, i1, jnp.where(lane == 1, i2, jnp.where(lane == 2, r1, jnp.where(
        lane == 3, r2, 0.0)))).astype(I32)


def _latent_rows(tm, t, ctx_len):
    return pl.BlockSpec((pl.Element(tm), pl.Element(D_MODEL)),
                        lambda b, i: (pl.multiple_of(b * t + ctx_len + i * tm, SUBLANES), 0))


def _router(x2, modl, g, rw, rb, bsz, t, ctx_len):
    tm = LATENT_TILE
    nl = bsz * (t - ctx_len)
    nlt = (t - ctx_len) // tm
    orow = lambda c: pl.BlockSpec((tm, c), lambda b, i: (b * nlt + i, 0))
    return pl.pallas_call(
        _router_kernel,
        grid=(bsz, nlt),
        in_specs=[_latent_rows(tm, t, ctx_len),
                  pl.BlockSpec((1, 8, D_MODEL), lambda b, i: (b, 0, 0)),
                  _resident((1, D_MODEL)), _resident(rw.shape), _resident(rb.shape)],
        out_specs=[orow(D_MODEL // 2), orow(LANES), orow(LANES), pl.BlockSpec((8, LANES), lambda b, i: (0, 0))],
        out_shape=[jax.ShapeDtypeStruct((nl, D_MODEL // 2), jnp.uint32), jax.ShapeDtypeStruct((nl, LANES), I32),
                   jax.ShapeDtypeStruct((nl, LANES), F32), jax.ShapeDtypeStruct((8, LANES), F32)],
        scratch_shapes=[pltpu.VMEM((8, LANES), F32)],
        compiler_params=_cparams("arbitrary", "arbitrary"),
        name="moe_router",
    )(x2, modl, g, rw, rb)


def _gather_rows(table, idx):
    n_rows, d = idx.shape[0], table.shape[1]
    n_workers = SC_CORES * SC_SUBCORES
    per_worker = n_rows // n_workers
    chunk = SC_GATHER_ROWS
    assert per_worker * n_workers == n_rows and per_worker % chunk == 0
    mesh = plsc.VectorSubcoreMesh(core_axis_name="c", subcore_axis_name="s")

    @functools.partial(
        pl.kernel, mesh=mesh, out_type=jax.ShapeDtypeStruct((n_rows, d), table.dtype),
        scratch_types=[pltpu.VMEM((chunk,), I32), pltpu.VMEM((chunk, d), table.dtype), pltpu.SemaphoreType.DMA],
        name="sc_gather_rows")
    def gather(table_hbm, idx_hbm, out_hbm, idx_v, rows_v, sem):
        base = (lax.axis_index("s") * SC_CORES + lax.axis_index("c")) * per_worker

        @pl.loop(0, per_worker // chunk)
        def _(j):
            off = pl.multiple_of(base + j * chunk, 8)
            pltpu.sync_copy(idx_hbm.at[pl.ds(off, chunk)], idx_v)
            pltpu.async_copy(table_hbm.at[idx_v], rows_v, sem).wait()
            pltpu.sync_copy(rows_v, out_hbm.at[pl.ds(off, chunk)])

    return gather(table, idx)


def _dispatch_rows(table, slot, n_out):
    assert TOP_K == 2
    n_assign, (n_tok, d) = slot.shape[0], table.shape
    n_workers = SC_CORES * SC_SUBCORES
    per_worker = n_out // n_workers
    chunk, vec = SC_GATHER_ROWS, SC_LANES
    assert per_worker * n_workers == n_out and per_worker % chunk == 0 and n_assign % vec == 0
    assert n_out < 3 * n_tok
    mesh = plsc.VectorSubcoreMesh(core_axis_name="c", subcore_axis_name="s")

    @functools.partial(
        pl.kernel, mesh=mesh, out_type=jax.ShapeDtypeStruct((n_out, d), table.dtype),
        scratch_types=[pltpu.VMEM((n_assign,), I32), pltpu.VMEM((per_worker,), I32),
                       pltpu.VMEM((chunk, d), table.dtype), pltpu.SemaphoreType.DMA],
        compiler_params=pltpu.CompilerParams(needs_layout_passes=False),
        name="sc_dispatch_rows")
    def dispatch(table_hbm, slot_hbm, out_hbm, slots_v, tok_v, rows_v, sem):
        base = (lax.axis_index("s") * SC_CORES + lax.axis_index("c")) * per_worker
        pltpu.sync_copy(slot_hbm, slots_v)
        lane = lax.broadcasted_iota(I32, (vec,), 0)

        @pl.loop(0, per_worker // vec)
        def _(j):
            p = base + j * vec + lane
            p = jnp.where(p >= n_tok, p - n_tok, p)
            tok_v[pl.ds(j * vec, vec)] = jnp.where(p >= n_tok, p - n_tok, p)

        @pl.loop(0, n_assign // vec)
        def _(j):
            s = slots_v[pl.ds(j * vec, vec)] - base
            mine = (s >= 0) & (s < per_worker)
            tok = lax.shift_right_logical(j * vec + lane, 1)
            plsc.store_scatter(tok_v, [jnp.where(mine, s, 0)], tok, mask=mine)

        @pl.loop(0, per_worker // chunk)
        def _(j):
            off = pl.multiple_of(j * chunk, 8)
            pltpu.async_copy(table_hbm.at[tok_v.at[pl.ds(off, chunk)]], rows_v, sem).wait()
            pltpu.sync_copy(rows_v, out_hbm.at[pl.ds(base + off, chunk)])

    return dispatch(table, slot)


def _moe_kernel(be_ref, nu_ref, nv_ref, x_ref, w1g_ref, w1u_ref, w2_ref, y_ref, xb_ref, acc_ref):
    i, j = pl.program_id(0), pl.program_id(1)
    bm = x_ref.shape[0]

    def swiglu_rows(rows):
        x = xb_ref[0:rows, :]
        a = (_silu(_dot(x, w1g_ref[0].astype(BF16))) * _dot(x, w1u_ref[0].astype(BF16))).astype(BF16)
        acc_ref[0:rows, :] += _dot(a, w2_ref[0].astype(BF16))

    @pl.when(i < nu_ref[0])
    def _():
        @pl.when(j == 0)
        def _():
            acc_ref[...] = jnp.zeros_like(acc_ref)
            xb_ref[...] = _unpack_rows(x_ref[...]).astype(BF16)

        @pl.when(nv_ref[i] > bm // 2)
        def _():
            swiglu_rows(bm)

        @pl.when(nv_ref[i] <= bm // 2)
        def _():
            swiglu_rows(bm // 2)

        @pl.when(j == pl.num_programs(1) - 1)
        def _():
            y_ref[...] = _pack_rows(acc_ref[...])


def _moe_experts(xs, w1, w2, block_e, n_used, n_valid, n_blocks):
    bm, fc = MOE_ROWS, MOE_FF
    nff = FF_EXPERT // fc
    used = lambda i, nu: jnp.minimum(i, nu[0] - 1)
    ffi = lambda i, j, nu: jnp.where(i < nu[0], j, nff - 1)
    grid_spec = pltpu.PrefetchScalarGridSpec(
        num_scalar_prefetch=3,
        grid=(n_blocks, nff),
        in_specs=[pl.BlockSpec((bm, D_MODEL // 2), lambda i, j, be, nu, nv: (used(i, nu), 0)),
                  pl.BlockSpec((1, D_MODEL, fc), lambda i, j, be, nu, nv: (be[used(i, nu)], 0, ffi(i, j, nu))),
                  pl.BlockSpec((1, D_MODEL, fc),
                               lambda i, j, be, nu, nv: (be[used(i, nu)], 0, nff + ffi(i, j, nu))),
                  pl.BlockSpec((1, fc, D_MODEL), lambda i, j, be, nu, nv: (be[used(i, nu)], ffi(i, j, nu), 0))],
        out_specs=pl.BlockSpec((bm, D_MODEL // 2), lambda i, j, be, nu, nv: (used(i, nu), 0)),
        scratch_shapes=[pltpu.VMEM((bm, D_MODEL), BF16), pltpu.VMEM((bm, D_MODEL), F32)],
    )
    return pl.pallas_call(
        _moe_kernel,
        grid_spec=grid_spec,
        out_shape=jax.ShapeDtypeStruct((n_blocks * bm, D_MODEL // 2), jnp.uint32),
        compiler_params=_cparams("arbitrary", "arbitrary"),
        name="moe_experts",
    )(block_e, n_used, n_valid, xs, w1, w1, w2)


def _combine_kernel(y0_ref, y1_ref, gate_ref, x_ref, modl_ref, g_ref, o_ref):
    gt = gate_ref[...]
    y = gt[:, 0:1] * _unpack_rows(y0_ref[...]) + gt[:, 1:2] * _unpack_rows(y1_ref[...])
    o_ref[...] = x_ref[...] + modl_ref[0, 5:6, :] * _rms(y, g_ref[...])


def _combine(yk, gates, x2, modl, g, bsz, t, ctx_len):
    tm = LATENT_TILE
    nlt = (t - ctx_len) // tm
    nl = bsz * (t - ctx_len)
    orow = lambda c: pl.BlockSpec((tm, c), lambda b, i: (b * nlt + i, 0))
    second = pl.BlockSpec((tm, D_MODEL // 2), lambda b, i: (nl // tm + b * nlt + i, 0))
    return pl.pallas_call(
        _combine_kernel,
        grid=(bsz, nlt),
        in_specs=[orow(D_MODEL // 2), second, orow(LANES),
                  _latent_rows(tm, t, ctx_len),
                  pl.BlockSpec((1, 8, D_MODEL), lambda b, i: (b, 0, 0)), _resident((1, D_MODEL))],
        out_specs=orow(D_MODEL),
        out_shape=jax.ShapeDtypeStruct((nl, D_MODEL), F32),
        compiler_params=_cparams("arbitrary", "arbitrary"),
        name="moe_combine",
    )(yk, yk, gates, x2, modl, g)


def _routing_tables(experts, ranks, counts):
    bm = MOE_ROWS
    n_assign = experts.size
    padded = (counts + bm - 1) // bm * bm
    ends_p = jnp.cumsum(padded)
    pstarts = ends_p - padded
    onehot = experts[..., None] == jnp.arange(N_EXPERTS, dtype=I32)
    slot = (jnp.sum(jnp.where(onehot, pstarts, 0), axis=-1) + ranks).astype(I32)
    n_blocks = n_assign // bm + N_EXPERTS
    block_e = jnp.minimum(jnp.sum(ends_p[None, :] <= (jnp.arange(n_blocks, dtype=I32) * bm)[:, None], axis=1),
                          N_EXPERTS - 1).astype(I32)
    n_used = (ends_p[-1] // bm).astype(I32).reshape(1)
    sel = block_e[:, None] == jnp.arange(N_EXPERTS, dtype=I32)
    rows_left = jnp.sum(jnp.where(sel, counts + pstarts, 0), axis=1) - jnp.arange(n_blocks, dtype=I32) * bm
    n_valid = jnp.clip(rows_left, 0, bm).astype(I32)
    return slot, block_e, n_used, n_valid, n_blocks


def kernel(x, c, ctx, c_ctx, mod_w, mod_b, norm_g, ab_in_w, m_conv_w, m_conv_b, m_gate_b, m_norm_w, l_conv_w,
           l_conv_b, l_wa, l_ba, l_wx, l_bx, l_lam, ab_out_w, ffn_w1, ffn_w2, cd_in_w, g_alpha_w, g_alpha_b,
           g_norm_w, s_conv_w, s_conv_b, s_dt_bias, s_A_log, s_D, s_norm_w, cd_out_w, router_w, router_b,
           moe_w1, moe_w2):
    bsz, seq, dm = x.shape
    ctx_len = ctx.shape[1]
    t = ctx_len + seq
    n = bsz * t
    assert ctx_len == ROW_TILE and seq % ROW_TILE == 0 and ROW_TILE % GRID_W == 0
    assert n % (IN_PARTS * ROW_TILE) == 0

    c_all = jnp.concatenate([c, c_ctx[None, :], jnp.zeros((-(bsz + 1) % 8, dm), F32)], axis=0)
    mods = _modulation(c_all, mod_w, mod_b).reshape(mod_w.shape[0], c_all.shape[0], 6, dm)
    pad2 = jnp.zeros((bsz, 2, dm), F32)

    def layer_mods(layer):
        modl = jnp.concatenate([mods[layer, :bsz], pad2], axis=1)
        modc = jnp.concatenate([mods[layer, bsz], pad2[0]], axis=0)
        return modl, modc

    modl, modc = layer_mods(0)
    g = norm_g[0]
    w = ab_in_w[0]
    o_qk, o_v, o_o, o_gt, o_lx, o_lg = 0, 1024, 2048, 3072, 3088, 4112
    w_in = jnp.concatenate([w[:, o_qk:o_v], w[:, o_v:o_o], w[:, o_o:o_gt], w[:, o_lx:o_lg], w[:, o_lg:],
                            w[:, o_gt:o_lx], jnp.zeros((dm, LANES - 16), F32)], axis=1).astype(BF16)
    gb = jnp.concatenate([m_gate_b[0].reshape(1, 16), jnp.zeros((1, LANES - 16), F32)], axis=1)
    x2, q, k, v, og, lx, glg, gates, gates_t = _in_ab(
        ctx.reshape(bsz * ctx_len, dm), x.reshape(bsz * seq, dm), modl, modc, g[0:1], w_in, m_conv_w[0],
        m_conv_b[0][None], l_conv_w[0], l_conv_b[0][None], gb, bsz, t, ctx_len)
    hm_f, hm_b = _mlstm(q, k, v, gates, gates_t, bsz, t, ctx_len)
    hl_f, hl_b = _lru(lx, l_wa[0].astype(BF16), l_wx[0].astype(BF16), l_ba[0][:, None], l_bx[0][:, None],
                      l_lam[0][:, None], bsz, t, ctx_len)
    x2 = _out_proj(_out_ab_kernel, "out_proj_swiglu_ab", [hm_f, hm_b, og, hl_f, hl_b, glg], [m_norm_w[0][None]],
                   x2, modl, modc, g[1:2], ab_out_w[0].astype(BF16), bsz, t, ctx_len,
                   extra=(g[2:3], g[3:4], ffn_w1[0].astype(BF16), ffn_w2[0].astype(BF16)))

    modl, modc = layer_mods(1)
    g = norm_g[1]
    w = cd_in_w[0]
    o_gv, o_gr, o_ga, o_z, o_xbc, o_dt = 1024, 2048, 3072, 3104, 4128, 5664
    zpad = jnp.zeros((dm, LANES - 32), F32)
    w_in = jnp.concatenate([w[:, :o_gv], w[:, o_gv:o_gr], w[:, o_gr:o_ga], w[:, o_z:o_xbc], w[:, o_xbc:o_dt],
                            w[:, o_ga:o_z], zpad, w[:, o_dt:], w[:, o_dt:], jnp.zeros((dm, LANES - 64), F32)],
                           axis=1).astype(BF16)
    dtb = jnp.concatenate([s_dt_bias[0].reshape(1, 32), s_dt_bias[0].reshape(1, 32),
                           jnp.zeros((1, LANES - 64), F32)], axis=1)
    aneg = jnp.concatenate([jnp.zeros((1, 32), F32), -jnp.exp(s_A_log[0].reshape(1, 32)),
                            jnp.zeros((1, LANES - 64), F32)], axis=1)
    gq, gk, gv, grs, zs, sx, sb, sc, ga, dts, dts_t = _in_cd(x2, modl, modc, g[0:1], w_in, s_conv_w[0],
                                                             s_conv_b[0][None], dtb, aneg, bsz, t, ctx_len)
    aw = jnp.zeros((2, LANES, G_HEADS * G_DK), F32)
    aw = aw.at[0, 0:G_RANK].set(g_alpha_w[0, 0]).at[1, G_RANK:2 * G_RANK].set(g_alpha_w[0, 1])
    awh = aw.astype(BF16)
    awl = (aw - awh.astype(F32)).astype(BF16)
    og_f, og_b = _gla(gq, gk, gv, ga, awh, awl, g_alpha_b[0][:, None], bsz, t, ctx_len)
    ys_f, ys_b = _ssd(sx, sb, sc, dts, dts_t, bsz, t, ctx_len)
    dskip = jnp.repeat(s_D[0], S_P)[None]
    x2 = _out_proj(_out_cd_kernel, "out_proj_cd", [og_f, og_b, grs, ys_f, ys_b, sx, zs],
                   [g_norm_w[0][None], dskip, s_norm_w[0][None]],
                   x2, modl, modc, g[1:2], cd_out_w[0].astype(BF16), bsz, t, ctx_len)

    rw = jnp.concatenate([router_w[0], jnp.zeros((dm, LANES - N_EXPERTS), F32)], axis=1)
    rwh = rw.astype(BF16)
    rw = jnp.concatenate([rwh, (rw - rwh.astype(F32)).astype(BF16)], axis=1)
    rb = jnp.concatenate([router_b[0][None], jnp.zeros((1, LANES - N_EXPERTS), F32)], axis=1)
    h2, top_idx, top_gate, counts = _router(x2, modl, g[2:3], rw, rb, bsz, t, ctx_len)
    slot, block_e, n_used, n_valid, n_blocks = _routing_tables(top_idx[:, 0:TOP_K], top_idx[:, TOP_K:2 * TOP_K],
                                                               counts[0, :N_EXPERTS].astype(I32))
    xs = _dispatch_rows(h2, slot.reshape(-1), n_blocks * MOE_ROWS)
    ys_e = _moe_experts(xs, moe_w1[0], moe_w2[0], block_e, n_used, n_valid, n_blocks)
    yk = _gather_rows(ys_e, jnp.concatenate([slot[:, 0], slot[:, 1]]))
    out = _combine(yk, top_gate, x2, modl, g[3:4], bsz, t, ctx_len)
    return out.reshape(bsz, seq, dm)
```

```python
import functools

import jax
import jax.numpy as jnp
from jax import lax
from jax.experimental import pallas as pl
from jax.experimental.pallas import tpu as pltpu
from jax.experimental.pallas import tpu_sc as plsc

F32, BF16, I32 = jnp.float32, jnp.bfloat16, jnp.int32

D_MODEL = 1024
GRID_W = 64
EPS = 1e-6
LOG2E = 1.4426950408889634
M_HEADS, M_DK, M_DV = 4, 128, 256
LRU_BLOCKS, LRU_BW, LRU_C = 8, 128, 8.0
G_HEADS, G_DK, G_DV, G_RANK, G_TAU = 4, 128, 256, 16, 16.0
S_HEADS, S_P, S_N, S_GROUPS, S_HPG = 16, 64, 128, 2, 8
N_EXPERTS, TOP_K, FF_EXPERT = 8, 2, 3584

LANES, SUBLANES = 128, 8
VMEM_LIMIT = 56 * 1024 * 1024
ROW_TILE = 256
IN_PARTS = 2
LATENT_TILE = 512
BIG_ROW_TILE_MAX = 640
PROJ_COLS = 256
M_CHUNK = 256
G_CHUNK, G_SUB = 256, 16
G_GROUP = 64
S_CHUNK = 256
LRU_TILE = 256
FF_STEP = 256
MOE_ROWS = 1024
MOE_FF = 512
MOE_ROW_STEP = 256
SC_CORES, SC_SUBCORES = 2, 16
SC_LANES = 16
SC_GATHER_ROWS = 128


def _cparams(*sem):
    return pltpu.CompilerParams(dimension_semantics=sem, vmem_limit_bytes=VMEM_LIMIT)


def _sigmoid(x):
    return 0.5 + 0.5 * jnp.tanh(0.5 * x)


def _silu(x):
    return x * _sigmoid(x)


def _softplus(x):
    return jnp.maximum(x, 0.0) + jnp.log1p(jnp.exp(-jnp.abs(x)))


def _log_sigmoid(x):
    return jnp.minimum(x, 0.0) - jnp.log1p(jnp.exp(-jnp.abs(x)))


def _dot(a, b):
    return jnp.dot(a, b, preferred_element_type=F32)


def _dot_nt(a, b):
    return lax.dot_general(a, b, (((1,), (1,)), ((), ())), preferred_element_type=F32)


def _dot_tn(a, b):
    return lax.dot_general(a, b, (((0,), (0,)), ((), ())), preferred_element_type=F32)


def _split3(f):
    f1 = f.astype(BF16)
    r = f - f1.astype(F32)
    f2 = r.astype(BF16)
    f3 = (r - f2.astype(F32)).astype(BF16)
    return f1, f2, f3


def _sel_cols(mask01, f):
    p1, p2, p3 = _split3(f)
    return _dot(mask01, p1) + _dot(mask01, p2) + _dot(mask01, p3)


def _sel_rows(f, mask01):
    p1, p2, p3 = _split3(f)
    return _dot_nt(p1, mask01) + _dot_nt(p2, mask01) + _dot_nt(p3, mask01)


def _pack_rows(u):
    n = u.shape[1] // 2
    bits = lax.bitcast_convert_type(u.astype(BF16).astype(F32), jnp.uint32)
    return jnp.bitwise_or(bits[:, :n], jnp.right_shift(bits[:, n:], jnp.uint32(16)))


def _unpack_rows(p):
    hi = lax.bitcast_convert_type(jnp.bitwise_and(p, jnp.uint32(0xFFFF0000)), F32)
    lo = lax.bitcast_convert_type(jnp.left_shift(p, jnp.uint32(16)), F32)
    return jnp.concatenate([hi, lo], axis=1)


def _rms(u, g):
    return u * lax.rsqrt(jnp.mean(u * u, axis=-1, keepdims=True) + EPS) * g


def _head_rms(u, g, n_heads):
    w = u.shape[-1] // n_heads
    return jnp.concatenate([_rms(u[:, h * w:(h + 1) * w], g[:, h * w:(h + 1) * w]) for h in range(n_heads)], axis=1)


def _row_ids(tile_idx, tm):
    return tile_idx * tm + lax.broadcasted_iota(I32, (tm, 1), 0)


def _mod(modl_ref, modc_ref, idx, is_ctx):
    return jnp.where(is_ctx, modc_ref[idx:idx + 1, :], modl_ref[0, idx:idx + 1, :])


def _adaln(x, g, modl_ref, modc_ref, shift_idx, is_ctx):
    shift = _mod(modl_ref, modc_ref, shift_idx, is_ctx)
    scale = _mod(modl_ref, modc_ref, shift_idx + 1, is_ctx)
    return _rms(x, g) * (1.0 + scale) + shift


def _neighbour_tiles(r, step, is_lat):
    n_tiles, per_seg = r.shape[0], GRID_W // SUBLANES
    zero = jnp.zeros((1,) + r.shape[1:], r.dtype)

    def across(src):
        return jnp.where(is_lat, 0.0, r[src:src + 1]) if 0 <= src < n_tiles else zero

    pieces = []
    for s0 in range(0, n_tiles, per_seg):
        if step > 0:
            pieces += [across(s0 - 1), r[s0:s0 + per_seg - 1]]
        else:
            pieces += [r[s0 + 1:s0 + per_seg], across(s0 + per_seg)]
    return jnp.concatenate(pieces, axis=0)


def _dwconv(y, cw, cb, is_lat):
    tm, n = y.shape
    y3 = y.reshape(tm // SUBLANES, SUBLANES, n)
    sub = lax.broadcasted_iota(I32, (1, SUBLANES, 1), 1)
    w = lambda j: cw[j:j + 1, :].reshape(1, 1, n)
    r1, r2, r7 = pltpu.roll(y3, 1, 1), pltpu.roll(y3, 2, 1), pltpu.roll(y3, SUBLANES - 1, 1)
    ym1 = jnp.where(sub >= 1, r1, _neighbour_tiles(r1, 1, is_lat))
    ym2 = jnp.where(sub >= 2, r2, _neighbour_tiles(r2, 1, is_lat))
    yp1 = jnp.where(sub <= SUBLANES - 2, r7, _neighbour_tiles(r7, -1, is_lat))
    out = cb.reshape(1, 1, n) + w(0) * ym2 + w(1) * ym1 + w(2) * y3 + w(3) * yp1
    return out.reshape(tm, n)


def _resident(shape):
    nd = len(shape)
    return pl.BlockSpec(shape, lambda *_: (0,) * nd, pipeline_mode=pl.Buffered(1))


def _big_row_tile(t):
    return max(tm for tm in range(8, BIG_ROW_TILE_MAX + 1, 8) if t % tm == 0)


def _mod_kernel(c_ref, w_ref, b_ref, o_ref):
    a1, a2, _ = _split3(_silu(c_ref[...]))
    w = w_ref[0]
    wh = w.astype(BF16)
    wl = (w - wh.astype(F32)).astype(BF16)
    o_ref[0] = _dot(a1, wh) + _dot(a2, wh) + _dot(a1, wl) + b_ref[0]


def _modulation(c_all, mod_w, mod_b):
    depth, d, n6 = mod_w.shape
    rows = c_all.shape[0]
    tn = 1024
    return pl.pallas_call(
        _mod_kernel,
        grid=(depth, n6 // tn),
        in_specs=[pl.BlockSpec((rows, d), lambda l, j: (0, 0)),
                  pl.BlockSpec((1, d, tn), lambda l, j: (l, 0, j)),
                  pl.BlockSpec((1, 1, tn), lambda l, j: (l, 0, j))],
        out_specs=pl.BlockSpec((1, rows, tn), lambda l, j: (l, 0, j)),
        out_shape=jax.ShapeDtypeStruct((depth, rows, n6), F32),
        compiler_params=_cparams("arbitrary", "arbitrary"),
        name="modulation",
    )(c_all, mod_w, mod_b.reshape(depth, 1, n6))


def _project(h_ref, w_ref, w_col, out_ref, epilogue):
    for c in range(0, out_ref.shape[1], PROJ_COLS):
        y = _dot(h_ref[...], w_ref[:, w_col + c:w_col + c + PROJ_COLS])
        out_ref[:, c:c + PROJ_COLS] = epilogue(y, c).astype(out_ref.dtype)


def _part_is_latent(part, parts_per_batch):
    return lax.rem(pl.program_id(0) * IN_PARTS + part, parts_per_batch) != 0


def _conv_parts(y, cw, cb, lat):
    return jnp.concatenate([_dwconv(y[p * ROW_TILE:(p + 1) * ROW_TILE], cw, cb, lat[p]) for p in range(IN_PARTS)],
                           axis=0)


def _in_ab_kernel(parts_per_batch, *refs):
    xs, refs = refs[:3 * IN_PARTS], refs[3 * IN_PARTS:]
    modc_ref, g_ref, w_ref, mcw_ref, mcb_ref, lcw_ref, lcb_ref, gb_ref = refs[:8]
    x_ref, q_ref, k_ref, v_ref, og_ref, lx_ref, glg_ref, gates_ref, gates_t_ref, h_scr = refs[8:]
    lat = [_part_is_latent(p, parts_per_batch) for p in range(IN_PARTS)]
    for p in range(IN_PARTS):
        xc_ref, xl_ref, modl_ref = xs[3 * p:3 * p + 3]
        rows = slice(p * ROW_TILE, (p + 1) * ROW_TILE)
        xp = jnp.where(lat[p], xl_ref[...], xc_ref[...])
        x_ref[rows, :] = xp
        h_scr[rows, :] = _adaln(xp, g_ref[...], modl_ref, modc_ref, 0, jnp.logical_not(lat[p])).astype(BF16)

    def conv(cw_ref, cb_ref):
        return lambda y, c: _conv_parts(y, cw_ref[:, c:c + PROJ_COLS], cb_ref[:, c:c + PROJ_COLS], lat)

    qc, kc, lc = conv(mcw_ref, mcb_ref), conv(mcw_ref, mcb_ref), conv(lcw_ref, lcb_ref)
    _project(h_scr, w_ref, 0, q_ref, lambda y, c: _silu(qc(y, c)))
    _project(h_scr, w_ref, 512, k_ref, lambda y, c: _silu(kc(y, 512 + c)) * (M_DK ** -0.5))
    _project(h_scr, w_ref, 1024, v_ref, lambda y, c: y)
    _project(h_scr, w_ref, 2048, og_ref, lambda y, c: _sigmoid(y))
    _project(h_scr, w_ref, 3072, lx_ref, lc)
    _project(h_scr, w_ref, 4096, glg_ref, lambda y, c: jax.nn.gelu(y))
    gt = _dot(h_scr[...], w_ref[:, 5120:5248]) + gb_ref[...]
    lane = lax.broadcasted_iota(I32, gt.shape, 1)
    gates = jnp.where(lane >= 2 * M_HEADS, _log_sigmoid(gt), gt)
    gates_ref[...] = gates
    gates_t_ref[...] = gates.T


def _in_ab(xc2, xl2, modl, modc, g, w, mcw, mcb, lcw, lcb, gb, bsz, t, ctx_len):
    n = bsz * t
    tm = IN_PARTS * ROW_TILE
    ppb = t // ROW_TILE
    lpb = ppb - 1
    part = lambda j, p: j * IN_PARTS + p
    batch = lambda j, p: part(j, p) // ppb
    x_specs, x_args = [], []
    for p in range(IN_PARTS):
        x_specs += [pl.BlockSpec((ROW_TILE, D_MODEL), lambda j, p=p: (batch(j, p), 0)),
                    pl.BlockSpec((ROW_TILE, D_MODEL),
                                 lambda j, p=p: (batch(j, p) * lpb + jnp.maximum(part(j, p) % ppb - 1, 0), 0)),
                    pl.BlockSpec((1, 8, D_MODEL), lambda j, p=p: (batch(j, p), 0, 0))]
        x_args += [xc2, xl2, modl]
    row = lambda c: pl.BlockSpec((tm, c), lambda j: (j, 0))
    outs = [(D_MODEL, F32), (512, BF16), (512, BF16), (1024, BF16), (1024, BF16), (1024, BF16), (1024, BF16),
            (LANES, F32)]
    params = (modc, g, w, mcw, mcb, lcw, lcb, gb)
    return pl.pallas_call(
        functools.partial(_in_ab_kernel, ppb),
        grid=(n // tm,),
        in_specs=x_specs + [_resident(a.shape) for a in params],
        out_specs=[row(c) for c, _ in outs] + [pl.BlockSpec((LANES, tm), lambda j: (0, j))],
        out_shape=[jax.ShapeDtypeStruct((n, c), dt) for c, dt in outs] + [jax.ShapeDtypeStruct((LANES, n), F32)],
        scratch_shapes=[pltpu.VMEM((tm, D_MODEL), BF16)],
        compiler_params=_cparams("arbitrary"),
        name="in_proj_ab",
    )(*x_args, *params)


def _reverse_order(n_ctx_chunks, n_chunks):
    return lambda c: jnp.where(c < n_ctx_chunks, n_ctx_chunks - 1 - c, n_chunks - 1 - (c - n_ctx_chunks))


def _tri_mask(L, reverse):
    row = lax.broadcasted_iota(I32, (L, L), 0)
    col = lax.broadcasted_iota(I32, (L, L), 1)
    return (col >= row) if reverse else (col <= row)


def _bidir_scan(kern, L, row_ins, lane_ins, res_ins, out_cols, scratch, bsz, t, ctx_len, name):
    nc = t // L
    rev = _reverse_order(ctx_len // L, nc)
    fwd = lambda i: i
    n = bsz * t

    def row(c, order):
        return pl.BlockSpec((L, c), lambda b, i: (b * nc + order(i), 0))

    def lanes(r, order):
        return pl.BlockSpec((r, L), lambda b, i: (0, b * nc + order(i)))

    in_specs, args = [], []
    for order in (fwd, rev):
        in_specs += [row(a.shape[1], order) for a in row_ins] + [lanes(r, order) for _, r in lane_ins]
        args += list(row_ins) + [a for a, _ in lane_ins]
    in_specs += [_resident(a.shape) for a in res_ins]
    args += list(res_ins)
    return pl.pallas_call(
        kern,
        grid=(bsz, nc),
        in_specs=in_specs,
        out_specs=[row(c, fwd) for c in out_cols] + [row(c, rev) for c in out_cols],
        out_shape=[jax.ShapeDtypeStruct((n, c), BF16) for c in out_cols] * 2,
        scratch_shapes=scratch,
        compiler_params=_cparams("arbitrary", "arbitrary"),
        name=name,
    )(*args)


def _zero_at_start(*scratch):
    @pl.when(pl.program_id(1) == 0)
    def _():
        for s in scratch:
            s[...] = jnp.zeros_like(s)


def _mlstm_kernel(L, qf, kf, vf, gcf, grf, qb, kb, vb, gcb, grb, of, ob, c_scr, m_scr):
    _zero_at_start(c_scr, m_scr)
    ones = jnp.ones((L, LANES), BF16)
    wide = lambda c, k: jnp.concatenate([c] * k, axis=1)
    er = lax.broadcasted_iota(I32, (LANES, 2 * M_HEADS * LANES), 0)
    ej = jnp.right_shift(lax.broadcasted_iota(I32, (LANES, 2 * M_HEADS * LANES), 1), LANES.bit_length() - 1)
    chains = []
    for d, (q_ref, k_ref, v_ref, gc_ref, gr_ref, out_ref) in enumerate(((qf, kf, vf, gcf, grf, of),
                                                                         (qb, kb, vb, gcb, grb, ob))):
        reverse = d == 1
        mask = _tri_mask(L, reverse)
        mask01 = mask.astype(BF16)
        gc = gc_ref[...]
        gr = gr_ref[...]
        bcol = _sel_cols(mask01, gc)
        brow = _sel_rows(gr, mask01)
        glane = lax.broadcasted_iota(I32, (L, LANES), 1)
        src = jnp.where(ej < M_HEADS, M_HEADS * d + ej, M_HEADS + M_HEADS * d + ej)
        s1, s2, _ = _split3(jnp.where(glane < 2 * M_HEADS, gc, bcol))
        sel = (er == src).astype(BF16)
        rep = _dot(s1, sel) + _dot(s2, sel)
        last = 0 if reverse else L - 1
        for h in range(M_HEADS):
            fo, io = 2 * M_HEADS + M_HEADS * d + h, M_HEADS * d + h
            ic, bc = rep[:, h * LANES:(h + 1) * LANES], rep[:, (M_HEADS + h) * LANES:(M_HEADS + h + 1) * LANES]
            r1 = gr[io:io + 1, :] - brow[fo:fo + 1, :]
            m = m_scr[d, h:h + 1, :]
            g = bc + m
            mx = jnp.max(jnp.where(mask, r1, -jnp.inf), axis=1, keepdims=True)
            mt = jnp.maximum(g, bc + mx)
            p = jnp.exp(jnp.where(mask, wide(bc - mt, L // LANES) + r1, -jnp.inf))
            bl = bc[last:last + 1, :]
            a_s = bl - bc + ic
            m_new = jnp.maximum(bl + m, jnp.max(a_s, axis=0, keepdims=True))
            chains.append(dict(
                d=d, h=h, out_ref=out_ref, mt=mt, w_inter=jnp.exp(g - mt), p=p,
                m_new=m_new, decay=jnp.exp(bl + m - m_new), ws=jnp.exp(a_s - m_new),
                qh=q_ref[:, h * M_DK:(h + 1) * M_DK], kh=k_ref[:, h * M_DK:(h + 1) * M_DK],
                vx=jnp.concatenate([v_ref[:, h * M_DV:(h + 1) * M_DV], ones], axis=1)))
    for ch in chains:
        ch["sc"] = (_dot_nt(ch["qh"], ch["kh"]) * ch["p"]).astype(BF16)
    for ch in chains:
        cx = c_scr[ch["d"], ch["h"]]
        numx = wide(ch["w_inter"], 1 + M_DV // LANES) * _dot(ch["qh"], cx.astype(BF16)) + _dot(ch["sc"], ch["vx"])
        den = numx[:, M_DV:]
        rinv = 1.0 / jnp.maximum(jnp.abs(den), jnp.exp(-ch["mt"]))
        hh = numx[:, 0:M_DV] * wide(rinv, M_DV // LANES)
        ch["out_ref"][:, ch["h"] * M_DV:(ch["h"] + 1) * M_DV] = hh.astype(BF16)
    for ch in chains:
        d, h = ch["d"], ch["h"]
        kw = (ch["kh"].astype(F32) * ch["ws"]).astype(BF16)
        c_scr[d, h] = wide(ch["decay"], 1 + M_DV // LANES) * c_scr[d, h] + _dot_tn(kw, ch["vx"])
        m_scr[d, h:h + 1, :] = ch["m_new"]


def _mlstm(q, k, v, gcol, grow, bsz, t, ctx_len):
    L = M_CHUNK
    scratch = [pltpu.VMEM((2, M_HEADS, M_DK, M_DV + LANES), F32), pltpu.VMEM((2, 8, LANES), F32)]
    return _bidir_scan(functools.partial(_mlstm_kernel, L), L, [q, k, v, gcol], [(grow, 16)], [], [1024], scratch,
                       bsz, t, ctx_len, "mlstm_scan")


def _lru_kernel(L, uf, ub_, wa_ref, wx_ref, ba_ref, bx_ref, lam_ref, of, ob, a_scr, b_scr, h_scr):
    _zero_at_start(h_scr)
    R, C = L // SUBLANES, LRU_BLOCKS * LRU_BW
    i0, i1 = lax.broadcasted_iota(I32, (L, L), 0), lax.broadcasted_iota(I32, (L, L), 1)
    when = lambda p: jnp.bitwise_and(p, SUBLANES - 1) * R + jnp.right_shift(p, SUBLANES.bit_length() - 1)
    regroup = (i1 == when(i0)).astype(BF16)
    ungroup = (i0 == when(i1)).astype(BF16)
    sub = lax.broadcasted_iota(I32, (SUBLANES, 1), 0)
    dirs = ((uf, of), (ub_, ob))
    for d, (u_ref, _) in enumerate(dirs):
        u = _dot(regroup, u_ref[...])
        ub = u.astype(BF16)
        sp = -LRU_C * _softplus(-lam_ref[d])
        for n in range(LRU_BLOCKS):
            sl = slice(n * LRU_BW, (n + 1) * LRU_BW)
            rg = _sigmoid(_dot(ub[:, sl], wa_ref[d, n]) + ba_ref[d, :, sl])
            ig = _sigmoid(_dot(ub[:, sl], wx_ref[d, n]) + bx_ref[d, :, sl])
            log_a = rg * sp[:, sl]
            a = jnp.exp(log_a)
            a_scr[d, :, sl] = a
            b_scr[d, :, sl] = jnp.sqrt(-jnp.tanh(log_a) * (a * a + 1.0)) * (ig * u[:, sl])
    S = SUBLANES
    hs = [jnp.zeros((S, C), F32), jnp.zeros((S, C), F32)]
    ps = [jnp.ones((S, C), F32), jnp.ones((S, C), F32)]
    for j in range(R):
        for d in range(2):
            rows = slice(S * j, S * j + S) if d == 0 else slice(S * (R - 1 - j), S * (R - j))
            a = a_scr[d, rows, :]
            hs[d] = a * hs[d] + b_scr[d, rows, :]
            ps[d] = a * ps[d]
            b_scr[d, rows, :] = hs[d]
            a_scr[d, rows, :] = ps[d]
    for d, (_, out_ref) in enumerate(dirs):
        c = h_scr[d:d + 1, :]
        cin = jnp.zeros((S, C), F32)
        for s in (range(S) if d == 0 else reversed(range(S))):
            cin = jnp.where(sub == s, c, cin)
            c = ps[d][s:s + 1, :] * c + hs[d][s:s + 1, :]
        h_scr[d:d + 1, :] = c
        h = b_scr[d].reshape(R, S, C) + a_scr[d].reshape(R, S, C) * cin[None]
        out_ref[...] = _dot(ungroup, h.reshape(L, C).astype(BF16)).astype(BF16)


def _lru(u, wa, wx, ba, bx, lam, bsz, t, ctx_len):
    L = LRU_TILE
    scratch = [pltpu.VMEM((2, L, 1024), F32), pltpu.VMEM((2, L, 1024), F32), pltpu.VMEM((8, 1024), F32)]
    return _bidir_scan(functools.partial(_lru_kernel, L), L, [u], [], [wa, wx, ba, bx, lam], [1024], scratch,
                       bsz, t, ctx_len, "rglru_scan")


def _swiglu_rows(x, is_ctx, modl_ref, modc_ref, g2_ref, g3_ref, w1_ref, w2_ref):
    h = _adaln(x, g2_ref[...], modl_ref, modc_ref, 3, is_ctx).astype(BF16)
    ff = w2_ref.shape[0]
    acc = jnp.zeros(x.shape, F32)
    for j in range(ff // FF_STEP):
        lo = j * FF_STEP
        gj = _dot(h, w1_ref[:, lo:lo + FF_STEP])
        uj = _dot(h, w1_ref[:, ff + lo:ff + lo + FF_STEP])
        acc = acc + _dot((_silu(gj) * uj).astype(BF16), w2_ref[lo:lo + FF_STEP, :])
    return x + _mod(modl_ref, modc_ref, 5, is_ctx) * _rms(acc, g3_ref[...])


def _out_tail(tm, ctx_len, a1, a2, x_ref, modl_ref, modc_ref, g_ref, w_ref, xo_ref, ffn_refs=None):
    is_ctx = _row_ids(pl.program_id(1), tm) < ctx_len
    half = w_ref.shape[0] // 2
    y = _dot(a1.astype(BF16), w_ref[0:half, :]) + _dot(a2.astype(BF16), w_ref[half:, :])
    x = x_ref[...] + _mod(modl_ref, modc_ref, 2, is_ctx) * _rms(y, g_ref[...])
    if ffn_refs is not None:
        x = _swiglu_rows(x, is_ctx, modl_ref, modc_ref, *ffn_refs)
    xo_ref[...] = x


def _out_ab_kernel(tm, ctx_len, hf, hb, og, lf, lb, glg, nw, x_ref, modl_ref, modc_ref, g_ref, w_ref,
                   g2_ref, g3_ref, w1_ref, w2_ref, xo_ref):
    hm = _head_rms(hf[...].astype(F32) + hb[...].astype(F32), nw[...], M_HEADS) * og[...].astype(F32)
    hl = (lf[...].astype(F32) + lb[...].astype(F32)) * glg[...].astype(F32)
    _out_tail(tm, ctx_len, hm, hl, x_ref, modl_ref, modc_ref, g_ref, w_ref, xo_ref,
              (g2_ref, g3_ref, w1_ref, w2_ref))


def _out_cd_kernel(tm, ctx_len, gf, gb, grs, yf, yb, sx, zs, gnw, dsk, snw, x_ref, modl_ref, modc_ref, g_ref,
                   w_ref, xo_ref):
    og = _head_rms(gf[...].astype(F32) + gb[...].astype(F32), gnw[...], G_HEADS) * grs[...].astype(F32)
    ys = yf[...].astype(F32) + yb[...].astype(F32) + dsk[...] * sx[...].astype(F32)
    ys = _rms(ys * zs[...].astype(F32), snw[...])
    _out_tail(tm, ctx_len, og, ys, x_ref, modl_ref, modc_ref, g_ref, w_ref, xo_ref)


def _out_proj(kern, name, acts, vecs, x2, modl, modc, g, w, bsz, t, ctx_len, extra=()):
    tm = _big_row_tile(t)
    nt = t // tm
    row = lambda c: pl.BlockSpec((tm, c), lambda b, i: (b * nt + i, 0))
    return pl.pallas_call(
        functools.partial(kern, tm, ctx_len),
        grid=(bsz, nt),
        in_specs=[row(1024)] * len(acts) + [_resident((1, 1024))] * len(vecs)
        + [row(D_MODEL), pl.BlockSpec((1, 8, D_MODEL), lambda b, i: (b, 0, 0)),
           _resident((8, D_MODEL)), _resident((1, D_MODEL)), _resident(w.shape)]
        + [_resident(a.shape) for a in extra],
        out_specs=row(D_MODEL),
        out_shape=jax.ShapeDtypeStruct(x2.shape, F32),
        compiler_params=_cparams("arbitrary", "arbitrary"),
        name=name,
    )(*acts, *vecs, x2, modl, modc, g, w, *extra)


def _in_cd_kernel(parts_per_batch, x_ref, *refs):
    modls, refs = refs[:IN_PARTS], refs[IN_PARTS:]
    modc_ref, g_ref, w_ref, cw_ref, cb_ref, dtb_ref, aneg_ref = refs[:7]
    gq_ref, gk_ref, gv_ref, gr_ref, z_ref, sx_ref, sb_ref, sc_ref, ga_ref, dts_ref, dts_t_ref, h_scr = refs[7:]
    lat = [_part_is_latent(p, parts_per_batch) for p in range(IN_PARTS)]
    for p in range(IN_PARTS):
        rows = slice(p * ROW_TILE, (p + 1) * ROW_TILE)
        h_scr[rows, :] = _adaln(x_ref[rows, :], g_ref[...], modls[p], modc_ref, 0,
                                jnp.logical_not(lat[p])).astype(BF16)

    def conv_silu(col0):
        return lambda y, c: _silu(_conv_parts(y, cw_ref[:, col0 + c:col0 + c + PROJ_COLS],
                                              cb_ref[:, col0 + c:col0 + c + PROJ_COLS], lat))

    _project(h_scr, w_ref, 0, gq_ref, lambda y, c: y * (G_DK ** -0.5))
    _project(h_scr, w_ref, 512, gk_ref, lambda y, c: y)
    _project(h_scr, w_ref, 1024, gv_ref, lambda y, c: y)
    _project(h_scr, w_ref, 2048, gr_ref, lambda y, c: _silu(y))
    _project(h_scr, w_ref, 3072, z_ref, lambda y, c: _silu(y))
    _project(h_scr, w_ref, 4096, sx_ref, conv_silu(0))
    _project(h_scr, w_ref, 5120, sb_ref, conv_silu(1024))
    _project(h_scr, w_ref, 5376, sc_ref, conv_silu(1280))
    ga_ref[...] = _dot(h_scr[...], w_ref[:, 5632:5760])
    dt = _softplus(_dot(h_scr[...], w_ref[:, 5760:5888]) + dtb_ref[...])
    lane = lax.broadcasted_iota(I32, dt.shape, 1)
    dts = jnp.where(lane < 2 * S_HEADS, dt, dt * aneg_ref[...])
    dts_ref[...] = dts
    dts_t_ref[...] = dts.T


def _in_cd(x2, modl, modc, g, w, cw, cb, dtb, aneg, bsz, t, ctx_len):
    n = bsz * t
    tm = IN_PARTS * ROW_TILE
    ppb = t // ROW_TILE
    row = lambda c: pl.BlockSpec((tm, c), lambda j: (j, 0))
    mod_specs = [pl.BlockSpec((1, 8, D_MODEL), lambda j, p=p: ((j * IN_PARTS + p) // ppb, 0, 0))
                 for p in range(IN_PARTS)]
    outs = [(512, BF16), (512, BF16), (1024, BF16), (1024, BF16), (1024, BF16), (1024, BF16), (256, BF16),
            (256, BF16), (LANES, F32), (LANES, F32)]
    params = (modc, g, w, cw, cb, dtb, aneg)
    return pl.pallas_call(
        functools.partial(_in_cd_kernel, ppb),
        grid=(n // tm,),
        in_specs=[row(D_MODEL)] + mod_specs + [_resident(a.shape) for a in params],
        out_specs=[row(c) for c, _ in outs] + [pl.BlockSpec((LANES, tm), lambda j: (0, j))],
        out_shape=[jax.ShapeDtypeStruct((n, c), dt) for c, dt in outs] + [jax.ShapeDtypeStruct((LANES, n), F32)],
        scratch_shapes=[pltpu.VMEM((tm, D_MODEL), BF16)],
        compiler_params=_cparams("arbitrary"),
        name="in_proj_cd",
    )(x2, *([modl] * IN_PARTS), *params)


def _gla_kernel(L, C, qf, kf, vf, gaf, qb, kb, vb, gab, awh_ref, awl_ref, ab_ref, of, ob, s_scr, att_scr):
    _zero_at_start(s_scr, att_scr)
    nb = L // C
    chains = []
    for d, (q_ref, k_ref, v_ref, ga_ref, out_ref) in enumerate(((qf, kf, vf, gaf, of), (qb, kb, vb, gab, ob))):
        reverse = d == 1
        mask = _tri_mask(L, reverse)
        g1, g2, _ = _split3(ga_ref[...])
        pre = _dot(g1, awh_ref[d]) + _dot(g2, awh_ref[d]) + _dot(g1, awl_ref[d])
        lg = _log_sigmoid(pre + ab_ref[d]) * (1.0 / G_TAU)
        ball = _sel_cols(mask.astype(BF16), lg)
        for h in range(G_HEADS):
            ks = slice(h * G_DK, (h + 1) * G_DK)
            b = ball[:, ks] * LOG2E
            qh = q_ref[:, ks].astype(F32)
            inter = _dot_nt((qh * jnp.exp2(b)).astype(BF16), s_scr[d, h].astype(BF16))
            chains.append(dict(d=d, h=h, reverse=reverse, mask=mask, b=b, qh=qh, kh=k_ref[:, ks].astype(F32),
                               vh=v_ref[:, h * G_DV:(h + 1) * G_DV], inter=inter, out_ref=out_ref, blocks=[]))
    T = min(G_GROUP, L)
    for i in range(nb):
        lo, hi = i * C, (i + 1) * C
        t0 = lo // T * T
        for c, ch in enumerate(chains):
            b = ch["b"]
            if ch["reverse"]:
                bref = b[hi:hi + 1, :] if i < nb - 1 else jnp.zeros((1, G_DK), F32)
            else:
                bref = b[lo - 1:lo, :] if i > 0 else jnp.zeros((1, G_DK), F32)
            qi = (ch["qh"][lo:hi, :] * jnp.exp2(b[lo:hi, :] - bref)).astype(BF16)
            ki = (ch["kh"][t0:t0 + T] * jnp.exp2(bref - b[t0:t0 + T])).astype(BF16)
            att_scr[c, lo:hi, t0:t0 + T] = _dot_nt(qi, ki)
    for g in range(L // T):
        t0, t1 = g * T, (g + 1) * T
        for c, ch in enumerate(chains):
            b = ch["b"]
            e0, e1 = (t1, L) if ch["reverse"] else (0, t0)
            if e1 > e0:
                bref = b[t1:t1 + 1, :] if ch["reverse"] else b[t0 - 1:t0, :]
                qg = (ch["qh"][t0:t1] * jnp.exp2(b[t0:t1] - bref)).astype(BF16)
                kg = (ch["kh"][e0:e1] * jnp.exp2(bref - b[e0:e1])).astype(BF16)
                att_scr[c, t0:t1, e0:e1] = _dot_nt(qg, kg)
    for c, ch in enumerate(chains):
        att = jnp.where(ch["mask"], att_scr[c], 0.0).astype(BF16)
        oh = ch["inter"] + _dot(att, ch["vh"])
        ch["out_ref"][:, ch["h"] * G_DV:(ch["h"] + 1) * G_DV] = oh.astype(BF16)
    for ch in chains:
        d, h, b = ch["d"], ch["h"], ch["b"]
        last = 0 if ch["reverse"] else L - 1
        bl = b[last:last + 1, :]
        kd = (ch["kh"] * jnp.exp2(bl - b)).astype(BF16)
        s_scr[d, h] = s_scr[d, h] * jnp.exp2(bl) + _dot_tn(ch["vh"], kd)


def _gla(q, k, v, ga, awh, awl, ab, bsz, t, ctx_len):
    L, C = G_CHUNK, G_SUB
    scratch = [pltpu.VMEM((2, G_HEADS, G_DV, G_DK), F32), pltpu.VMEM((2 * G_HEADS, L, L), F32)]
    return _bidir_scan(functools.partial(_gla_kernel, L, C), L, [q, k, v, ga], [], [awh, awl, ab], [1024], scratch,
                       bsz, t, ctx_len, "gla_scan")


def _ssd_kernel(L, xf, bf, cf, gcf, grf, xb, bb, cb_, gcb, grb, of, ob, s_scr):
    _zero_at_start(s_scr)
    er = lax.broadcasted_iota(I32, (LANES, S_HEADS * S_P), 0)
    ec = jnp.right_shift(lax.broadcasted_iota(I32, (LANES, S_HEADS * S_P), 1), S_P.bit_length() - 1)
    glane = lax.broadcasted_iota(I32, (L, LANES), 1)
    lane = lax.broadcasted_iota(I32, (L, 2 * S_P), 1)
    dirs, chains = [], []
    for d, (x_ref, b_ref, c_ref, gc_ref, gr_ref, out_ref) in enumerate(((xf, bf, cf, gcf, grf, of),
                                                                         (xb, bb, cb_, gcb, grb, ob))):
        reverse = d == 1
        mask = _tri_mask(L, reverse)
        mask01 = mask.astype(BF16)
        gc = gc_ref[...]
        gr = gr_ref[...]
        bcol = _sel_cols(mask01, gc)
        brow = _sel_rows(gr, mask01)
        last = 0 if reverse else L - 1
        dt_o, la_o = S_HEADS * d, 2 * S_HEADS + S_HEADS * d
        sel_la = (er == ec + la_o).astype(BF16)
        x = x_ref[...]
        bla = jnp.where((glane >= la_o) & (glane < la_o + S_HEADS), bcol, 0.0)
        dt_at_la = pltpu.roll(gc, 2 * S_HEADS, 1)
        bl = bla[last:last + 1, :]
        ebx = _dot(jnp.exp(bla).astype(BF16), sel_la)
        wx = _dot((jnp.exp(bl - bla) * dt_at_la).astype(BF16), sel_la)
        dirs.append(dict(d=d, mask=mask, gr=gr, x=x, ebx=ebx, last=last, dt_o=dt_o, la_o=la_o, out_ref=out_ref,
                         xw=(x.astype(F32) * wx).astype(BF16),
                         bcol2=bcol * LOG2E, brow2=brow * LOG2E))
        for g in range(S_GROUPS):
            ns = slice(g * S_N, (g + 1) * S_N)
            chains.append(dict(dr=dirs[-1], g=g, bg=b_ref[:, ns], cg=c_ref[:, ns],
                               gs=slice(g * S_HPG * S_P, (g + 1) * S_HPG * S_P)))
    for ch in chains:
        dr = ch["dr"]
        ch["cb"] = _dot_nt(ch["cg"], ch["bg"])
        ch["inter"] = _dot(ch["cg"], s_scr[dr["d"], ch["g"]].astype(BF16)) * dr["ebx"][:, ch["gs"]]
    for ch in chains:
        dr, g = ch["dr"], ch["g"]
        parts = []
        for pr in range(S_HPG // 2):
            e0 = g * S_HPG + 2 * pr
            xp = dr["x"][:, e0 * S_P:(e0 + 2) * S_P]
            res = []
            for e in (e0, e0 + 1):
                col = dr["la_o"] + e
                bc, br = dr["bcol2"][:, col:col + 1], dr["brow2"][col:col + 1, :]
                dtr = dr["gr"][dr["dt_o"] + e:dr["dt_o"] + e + 1, :]
                att = (ch["cb"] * dtr * jnp.exp2(jnp.where(dr["mask"], bc - br, -jnp.inf))).astype(BF16)
                res.append(_dot(att, xp))
            parts.append(jnp.where(lane < S_P, res[0], res[1]))
        dr["out_ref"][:, ch["gs"]] = (jnp.concatenate(parts, axis=1) + ch["inter"]).astype(BF16)
    for ch in chains:
        dr, g, gs = ch["dr"], ch["g"], ch["gs"]
        s_scr[dr["d"], g] = (s_scr[dr["d"], g] * dr["ebx"][dr["last"]:dr["last"] + 1, gs]
                             + _dot_tn(ch["bg"], dr["xw"][:, gs]))


def _ssd(x, bm, cm, gcol, grow, bsz, t, ctx_len):
    L = S_CHUNK
    scratch = [pltpu.VMEM((2, S_GROUPS, S_N, S_HPG * S_P), F32)]
    return _bidir_scan(functools.partial(_ssd_kernel, L), L, [x, bm, cm, gcol], [(grow, 4 * S_HEADS)], [], [1024],
                       scratch,
                       bsz, t, ctx_len, "ssd_scan")


def _router_kernel(x_ref, modl_ref, g_ref, rw_ref, rb_ref, h_ref, idx_ref, gate_ref, cnt_ref, base_scr):
    @pl.when((pl.program_id(0) == 0) & (pl.program_id(1) == 0))
    def _():
        base_scr[...] = jnp.zeros_like(base_scr)

    h = _rms(x_ref[...], g_ref[...]) * (1.0 + modl_ref[0, 4:5, :]) + modl_ref[0, 3:4, :]
    h_ref[...] = _pack_rows(h)
    h1, h2, _ = _split3(h)
    both = _dot(h1, rw_ref[...])
    logits = both[:, :LANES] + both[:, LANES:] + _dot(h2, rw_ref[:, :LANES]) + rb_ref[...]
    lane = lax.broadcasted_iota(I32, logits.shape, 1)
    logits = jnp.where(lane < N_EXPERTS, logits, -jnp.inf)
    lanef = lane.astype(F32)
    m1 = jnp.max(logits, axis=1, keepdims=True)
    i1 = jnp.min(jnp.where(logits == m1, lanef, float(LANES)), axis=1, keepdims=True)
    rest = jnp.where(lanef == i1, -jnp.inf, logits)
    m2 = jnp.max(rest, axis=1, keepdims=True)
    i2 = jnp.min(jnp.where(rest == m2, lanef, float(LANES)), axis=1, keepdims=True)
    e = jnp.exp(m2 - m1)
    g1 = 1.0 / (1.0 + e)
    gate_ref[...] = jnp.where(lane == 0, g1, jnp.where(lane == 1, e * g1, 0.0))
    tm = logits.shape[0]
    oh1, oh2 = (lanef == i1).astype(F32), (lanef == i2).astype(F32)
    earlier = (lax.broadcasted_iota(I32, (tm, tm), 1) < lax.broadcasted_iota(I32, (tm, tm), 0)).astype(BF16)
    seen = base_scr[0:1, :] + _dot(earlier, (oh1 + oh2).astype(BF16))
    r1 = jnp.sum(oh1 * seen, axis=1, keepdims=True)
    r2 = jnp.sum(oh2 * seen, axis=1, keepdims=True)
    base_scr[0:1, :] = base_scr[0:1, :] + jnp.sum(oh1 + oh2, axis=0, keepdims=True)
    cnt_ref[...] = base_scr[...]
    idx_ref[...] = jnp.where(lane == 0, i1, jnp.where(lane == 1, i2, jnp.where(lane == 2, r1, jnp.where(
        lane == 3, r2, 0.0)))).astype(I32)


def _latent_rows(tm, t, ctx_len):
    return pl.BlockSpec((pl.Element(tm), pl.Element(D_MODEL)),
                        lambda b, i: (pl.multiple_of(b * t + ctx_len + i * tm, SUBLANES), 0))


def _router(x2, modl, g, rw, rb, bsz, t, ctx_len):
    tm = LATENT_TILE
    nl = bsz * (t - ctx_len)
    nlt = (t - ctx_len) // tm
    orow = lambda c: pl.BlockSpec((tm, c), lambda b, i: (b * nlt + i, 0))
    return pl.pallas_call(
        _router_kernel,
        grid=(bsz, nlt),
        in_specs=[_latent_rows(tm, t, ctx_len),
                  pl.BlockSpec((1, 8, D_MODEL), lambda b, i: (b, 0, 0)),
                  _resident((1, D_MODEL)), _resident(rw.shape), _resident(rb.shape)],
        out_specs=[orow(D_MODEL // 2), orow(LANES), orow(LANES), pl.BlockSpec((8, LANES), lambda b, i: (0, 0))],
        out_shape=[jax.ShapeDtypeStruct((nl, D_MODEL // 2), jnp.uint32), jax.ShapeDtypeStruct((nl, LANES), I32),
                   jax.ShapeDtypeStruct((nl, LANES), F32), jax.ShapeDtypeStruct((8, LANES), F32)],
        scratch_shapes=[pltpu.VMEM((8, LANES), F32)],
        compiler_params=_cparams("arbitrary", "arbitrary"),
        name="moe_router",
    )(x2, modl, g, rw, rb)


def _gather_rows(table, idx):
    n_rows, d = idx.shape[0], table.shape[1]
    n_workers = SC_CORES * SC_SUBCORES
    per_worker = n_rows // n_workers
    chunk = SC_GATHER_ROWS
    assert per_worker * n_workers == n_rows and per_worker % chunk == 0
    mesh = plsc.VectorSubcoreMesh(core_axis_name="c", subcore_axis_name="s")

    @functools.partial(
        pl.kernel, mesh=mesh, out_type=jax.ShapeDtypeStruct((n_rows, d), table.dtype),
        scratch_types=[pltpu.VMEM((chunk,), I32), pltpu.VMEM((chunk, d), table.dtype), pltpu.SemaphoreType.DMA],
        name="sc_gather_rows")
    def gather(table_hbm, idx_hbm, out_hbm, idx_v, rows_v, sem):
        base = (lax.axis_index("s") * SC_CORES + lax.axis_index("c")) * per_worker

        @pl.loop(0, per_worker // chunk)
        def _(j):
            off = pl.multiple_of(base + j * chunk, 8)
            pltpu.sync_copy(idx_hbm.at[pl.ds(off, chunk)], idx_v)
            pltpu.async_copy(table_hbm.at[idx_v], rows_v, sem).wait()
            pltpu.sync_copy(rows_v, out_hbm.at[pl.ds(off, chunk)])

    return gather(table, idx)


def _dispatch_rows(table, slot, n_out):
    assert TOP_K == 2
    n_assign, (n_tok, d) = slot.shape[0], table.shape
    n_workers = SC_CORES * SC_SUBCORES
    per_worker = n_out // n_workers
    chunk, vec = SC_GATHER_ROWS, SC_LANES
    assert per_worker * n_workers == n_out and per_worker % chunk == 0 and n_assign % vec == 0
    assert n_out < 3 * n_tok
    mesh = plsc.VectorSubcoreMesh(core_axis_name="c", subcore_axis_name="s")

    @functools.partial(
        pl.kernel, mesh=mesh, out_type=jax.ShapeDtypeStruct((n_out, d), table.dtype),
        scratch_types=[pltpu.VMEM((n_assign,), I32), pltpu.VMEM((per_worker,), I32),
                       pltpu.VMEM((chunk, d), table.dtype), pltpu.SemaphoreType.DMA],
        compiler_params=pltpu.CompilerParams(needs_layout_passes=False),
        name="sc_dispatch_rows")
    def dispatch(table_hbm, slot_hbm, out_hbm, slots_v, tok_v, rows_v, sem):
        base = (lax.axis_index("s") * SC_CORES + lax.axis_index("c")) * per_worker
        pltpu.sync_copy(slot_hbm, slots_v)
        lane = lax.broadcasted_iota(I32, (vec,), 0)

        @pl.loop(0, per_worker // vec)
        def _(j):
            p = base + j * vec + lane
            p = jnp.where(p >= n_tok, p - n_tok, p)
            tok_v[pl.ds(j * vec, vec)] = jnp.where(p >= n_tok, p - n_tok, p)

        @pl.loop(0, n_assign // vec)
        def _(j):
            s = slots_v[pl.ds(j * vec, vec)] - base
            mine = (s >= 0) & (s < per_worker)
            tok = lax.shift_right_logical(j * vec + lane, 1)
            plsc.store_scatter(tok_v, [jnp.where(mine, s, 0)], tok, mask=mine)

        @pl.loop(0, per_worker // chunk)
        def _(j):
            off = pl.multiple_of(j * chunk, 8)
            pltpu.async_copy(table_hbm.at[tok_v.at[pl.ds(off, chunk)]], rows_v, sem).wait()
            pltpu.sync_copy(rows_v, out_hbm.at[pl.ds(base + off, chunk)])

    return dispatch(table, slot)


def _moe_kernel(be_ref, nu_ref, nv_ref, x_ref, w1g_ref, w1u_ref, w2_ref, y_ref, xb_ref, acc_ref):
    i, j = pl.program_id(0), pl.program_id(1)
    bm = x_ref.shape[0]

    def swiglu_rows(rows):
        x = xb_ref[0:rows, :]
        a = (_silu(_dot(x, w1g_ref[0].astype(BF16))) * _dot(x, w1u_ref[0].astype(BF16))).astype(BF16)
        acc_ref[0:rows, :] += _dot(a, w2_ref[0].astype(BF16))

    @pl.when(i < nu_ref[0])
    def _():
        @pl.when(j == 0)
        def _():
            acc_ref[...] = jnp.zeros_like(acc_ref)
            xb_ref[...] = _unpack_rows(x_ref[...]).astype(BF16)

        for rows in range(MOE_ROW_STEP, bm + 1, MOE_ROW_STEP):
            @pl.when((nv_ref[i] > rows - MOE_ROW_STEP) & (nv_ref[i] <= rows))
            def _(rows=rows):
                swiglu_rows(rows)

        @pl.when(j == pl.num_programs(1) - 1)
        def _():
            y_ref[...] = _pack_rows(acc_ref[...])


def _moe_experts(xs, w1, w2, block_e, n_used, n_valid, n_blocks):
    bm, fc = MOE_ROWS, MOE_FF
    nff = FF_EXPERT // fc
    used = lambda i, nu: jnp.minimum(i, nu[0] - 1)
    ffi = lambda i, j, nu: jnp.where(i < nu[0], j, nff - 1)
    grid_spec = pltpu.PrefetchScalarGridSpec(
        num_scalar_prefetch=3,
        grid=(n_blocks, nff),
        in_specs=[pl.BlockSpec((bm, D_MODEL // 2), lambda i, j, be, nu, nv: (used(i, nu), 0)),
                  pl.BlockSpec((1, D_MODEL, fc), lambda i, j, be, nu, nv: (be[used(i, nu)], 0, ffi(i, j, nu))),
                  pl.BlockSpec((1, D_MODEL, fc),
                               lambda i, j, be, nu, nv: (be[used(i, nu)], 0, nff + ffi(i, j, nu))),
                  pl.BlockSpec((1, fc, D_MODEL), lambda i, j, be, nu, nv: (be[used(i, nu)], ffi(i, j, nu), 0))],
        out_specs=pl.BlockSpec((bm, D_MODEL // 2), lambda i, j, be, nu, nv: (used(i, nu), 0)),
        scratch_shapes=[pltpu.VMEM((bm, D_MODEL), BF16), pltpu.VMEM((bm, D_MODEL), F32)],
    )
    return pl.pallas_call(
        _moe_kernel,
        grid_spec=grid_spec,
        out_shape=jax.ShapeDtypeStruct((n_blocks * bm, D_MODEL // 2), jnp.uint32),
        compiler_params=_cparams("arbitrary", "arbitrary"),
        name="moe_experts",
    )(block_e, n_used, n_valid, xs, w1, w1, w2)


def _combine_kernel(y0_ref, y1_ref, gate_ref, x_ref, modl_ref, g_ref, o_ref):
    gt = gate_ref[...]
    y = gt[:, 0:1] * _unpack_rows(y0_ref[...]) + gt[:, 1:2] * _unpack_rows(y1_ref[...])
    o_ref[...] = x_ref[...] + modl_ref[0, 5:6, :] * _rms(y, g_ref[...])


def _combine(yk, gates, x2, modl, g, bsz, t, ctx_len):
    tm = LATENT_TILE
    nlt = (t - ctx_len) // tm
    nl = bsz * (t - ctx_len)
    orow = lambda c: pl.BlockSpec((tm, c), lambda b, i: (b * nlt + i, 0))
    second = pl.BlockSpec((tm, D_MODEL // 2), lambda b, i: (nl // tm + b * nlt + i, 0))
    return pl.pallas_call(
        _combine_kernel,
        grid=(bsz, nlt),
        in_specs=[orow(D_MODEL // 2), second, orow(LANES),
                  _latent_rows(tm, t, ctx_len),
                  pl.BlockSpec((1, 8, D_MODEL), lambda b, i: (b, 0, 0)), _resident((1, D_MODEL))],
        out_specs=orow(D_MODEL),
        out_shape=jax.ShapeDtypeStruct((nl, D_MODEL), F32),
        compiler_params=_cparams("arbitrary", "arbitrary"),
        name="moe_combine",
    )(yk, yk, gates, x2, modl, g)


def _routing_tables(experts, ranks, counts):
    bm = MOE_ROWS
    n_assign = experts.size
    padded = (counts + bm - 1) // bm * bm
    ends_p = jnp.cumsum(padded)
    pstarts = ends_p - padded
    onehot = experts[..., None] == jnp.arange(N_EXPERTS, dtype=I32)
    slot = (jnp.sum(jnp.where(onehot, pstarts, 0), axis=-1) + ranks).astype(I32)
    n_blocks = n_assign // bm + N_EXPERTS
    block_e = jnp.minimum(jnp.sum(ends_p[None, :] <= (jnp.arange(n_blocks, dtype=I32) * bm)[:, None], axis=1),
                          N_EXPERTS - 1).astype(I32)
    n_used = (ends_p[-1] // bm).astype(I32).reshape(1)
    sel = block_e[:, None] == jnp.arange(N_EXPERTS, dtype=I32)
    rows_left = jnp.sum(jnp.where(sel, counts + pstarts, 0), axis=1) - jnp.arange(n_blocks, dtype=I32) * bm
    n_valid = jnp.clip(rows_left, 0, bm).astype(I32)
    return slot, block_e, n_used, n_valid, n_blocks


def kernel(x, c, ctx, c_ctx, mod_w, mod_b, norm_g, ab_in_w, m_conv_w, m_conv_b, m_gate_b, m_norm_w, l_conv_w,
           l_conv_b, l_wa, l_ba, l_wx, l_bx, l_lam, ab_out_w, ffn_w1, ffn_w2, cd_in_w, g_alpha_w, g_alpha_b,
           g_norm_w, s_conv_w, s_conv_b, s_dt_bias, s_A_log, s_D, s_norm_w, cd_out_w, router_w, router_b,
           moe_w1, moe_w2):
    bsz, seq, dm = x.shape
    ctx_len = ctx.shape[1]
    t = ctx_len + seq
    n = bsz * t
    assert ctx_len == ROW_TILE and seq % ROW_TILE == 0 and ROW_TILE % GRID_W == 0
    assert n % (IN_PARTS * ROW_TILE) == 0

    c_all = jnp.concatenate([c, c_ctx[None, :], jnp.zeros((-(bsz + 1) % 8, dm), F32)], axis=0)
    mods = _modulation(c_all, mod_w, mod_b).reshape(mod_w.shape[0], c_all.shape[0], 6, dm)
    pad2 = jnp.zeros((bsz, 2, dm), F32)

    def layer_mods(layer):
        modl = jnp.concatenate([mods[layer, :bsz], pad2], axis=1)
        modc = jnp.concatenate([mods[layer, bsz], pad2[0]], axis=0)
        return modl, modc

    modl, modc = layer_mods(0)
    g = norm_g[0]
    w = ab_in_w[0]
    o_qk, o_v, o_o, o_gt, o_lx, o_lg = 0, 1024, 2048, 3072, 3088, 4112
    w_in = jnp.concatenate([w[:, o_qk:o_v], w[:, o_v:o_o], w[:, o_o:o_gt], w[:, o_lx:o_lg], w[:, o_lg:],
                            w[:, o_gt:o_lx], jnp.zeros((dm, LANES - 16), F32)], axis=1).astype(BF16)
    gb = jnp.concatenate([m_gate_b[0].reshape(1, 16), jnp.zeros((1, LANES - 16), F32)], axis=1)
    x2, q, k, v, og, lx, glg, gates, gates_t = _in_ab(
        ctx.reshape(bsz * ctx_len, dm), x.reshape(bsz * seq, dm), modl, modc, g[0:1], w_in, m_conv_w[0],
        m_conv_b[0][None], l_conv_w[0], l_conv_b[0][None], gb, bsz, t, ctx_len)
    hm_f, hm_b = _mlstm(q, k, v, gates, gates_t, bsz, t, ctx_len)
    hl_f, hl_b = _lru(lx, l_wa[0].astype(BF16), l_wx[0].astype(BF16), l_ba[0][:, None], l_bx[0][:, None],
                      l_lam[0][:, None], bsz, t, ctx_len)
    x2 = _out_proj(_out_ab_kernel, "out_proj_swiglu_ab", [hm_f, hm_b, og, hl_f, hl_b, glg], [m_norm_w[0][None]],
                   x2, modl, modc, g[1:2], ab_out_w[0].astype(BF16), bsz, t, ctx_len,
                   extra=(g[2:3], g[3:4], ffn_w1[0].astype(BF16), ffn_w2[0].astype(BF16)))

    modl, modc = layer_mods(1)
    g = norm_g[1]
    w = cd_in_w[0]
    o_gv, o_gr, o_ga, o_z, o_xbc, o_dt = 1024, 2048, 3072, 3104, 4128, 5664
    zpad = jnp.zeros((dm, LANES - 32), F32)
    w_in = jnp.concatenate([w[:, :o_gv], w[:, o_gv:o_gr], w[:, o_gr:o_ga], w[:, o_z:o_xbc], w[:, o_xbc:o_dt],
                            w[:, o_ga:o_z], zpad, w[:, o_dt:], w[:, o_dt:], jnp.zeros((dm, LANES - 64), F32)],
                           axis=1).astype(BF16)
    dtb = jnp.concatenate([s_dt_bias[0].reshape(1, 32), s_dt_bias[0].reshape(1, 32),
                           jnp.zeros((1, LANES - 64), F32)], axis=1)
    aneg = jnp.concatenate([jnp.zeros((1, 32), F32), -jnp.exp(s_A_log[0].reshape(1, 32)),
                            jnp.zeros((1, LANES - 64), F32)], axis=1)
    gq, gk, gv, grs, zs, sx, sb, sc, ga, dts, dts_t = _in_cd(x2, modl, modc, g[0:1], w_in, s_conv_w[0],
                                                             s_conv_b[0][None], dtb, aneg, bsz, t, ctx_len)
    aw = jnp.zeros((2, LANES, G_HEADS * G_DK), F32)
    aw = aw.at[0, 0:G_RANK].set(g_alpha_w[0, 0]).at[1, G_RANK:2 * G_RANK].set(g_alpha_w[0, 1])
    awh = aw.astype(BF16)
    awl = (aw - awh.astype(F32)).astype(BF16)
    og_f, og_b = _gla(gq, gk, gv, ga, awh, awl, g_alpha_b[0][:, None], bsz, t, ctx_len)
    ys_f, ys_b = _ssd(sx, sb, sc, dts, dts_t, bsz, t, ctx_len)
    dskip = jnp.repeat(s_D[0], S_P)[None]
    x2 = _out_proj(_out_cd_kernel, "out_proj_cd", [og_f, og_b, grs, ys_f, ys_b, sx, zs],
                   [g_norm_w[0][None], dskip, s_norm_w[0][None]],
                   x2, modl, modc, g[1:2], cd_out_w[0].astype(BF16), bsz, t, ctx_len)

    rw = jnp.concatenate([router_w[0], jnp.zeros((dm, LANES - N_EXPERTS), F32)], axis=1)
    rwh = rw.astype(BF16)
    rw = jnp.concatenate([rwh, (rw - rwh.astype(F32)).astype(BF16)], axis=1)
    rb = jnp.concatenate([router_b[0][None], jnp.zeros((1, LANES - N_EXPERTS), F32)], axis=1)
    h2, top_idx, top_gate, counts = _router(x2, modl, g[2:3], rw, rb, bsz, t, ctx_len)
    slot, block_e, n_used, n_valid, n_blocks = _routing_tables(top_idx[:, 0:TOP_K], top_idx[:, TOP_K:2 * TOP_K],
                                                               counts[0, :N_EXPERTS].astype(I32))
    xs = _dispatch_rows(h2, slot.reshape(-1), n_blocks * MOE_ROWS)
    ys_e = _moe_experts(xs, moe_w1[0], moe_w2[0], block_e, n_used, n_valid, n_blocks)
    yk = _gather_rows(ys_e, jnp.concatenate([slot[:, 0], slot[:, 1]]))
    out = _combine(yk, top_gate, x2, modl, g[3:4], bsz, t, ctx_len)
    return out.reshape(bsz, seq, dm)
```

```python
import functools

import jax
import jax.numpy as jnp
from jax import lax
from jax.experimental import pallas as pl
from jax.experimental.pallas import tpu as pltpu
from jax.experimental.pallas import tpu_sc as plsc

F32, BF16, I32 = jnp.float32, jnp.bfloat16, jnp.int32

D_MODEL = 1024
GRID_W = 64
EPS = 1e-6
LOG2E = 1.4426950408889634
M_HEADS, M_DK, M_DV = 4, 128, 256
LRU_BLOCKS, LRU_BW, LRU_C = 8, 128, 8.0
G_HEADS, G_DK, G_DV, G_RANK, G_TAU = 4, 128, 256, 16, 16.0
S_HEADS, S_P, S_N, S_GROUPS, S_HPG = 16, 64, 128, 2, 8
N_EXPERTS, TOP_K, FF_EXPERT = 8, 2, 3584

LANES, SUBLANES = 128, 8
VMEM_LIMIT = 56 * 1024 * 1024
ROW_TILE = 256
IN_PARTS = 2
LATENT_TILE = 512
BIG_ROW_TILE_MAX = 640
PROJ_COLS = 256
M_CHUNK = 256
G_CHUNK, G_SUB = 256, 16
G_GROUP = 64
S_CHUNK = 256
LRU_TILE = 256
FF_STEP = 256
MOE_ROWS = 1024
MOE_FF = 512
MOE_ROW_STEP = 256
SC_CORES, SC_SUBCORES = 2, 16
SC_LANES = 16
SC_GATHER_ROWS = 128


def _cparams(*sem):
    return pltpu.CompilerParams(dimension_semantics=sem, vmem_limit_bytes=VMEM_LIMIT)


def _sigmoid(x):
    return 0.5 + 0.5 * jnp.tanh(0.5 * x)


def _silu(x):
    return x * _sigmoid(x)


def _softplus(x):
    return jnp.maximum(x, 0.0) + jnp.log1p(jnp.exp(-jnp.abs(x)))


def _log_sigmoid(x):
    return jnp.minimum(x, 0.0) - jnp.log1p(jnp.exp(-jnp.abs(x)))


def _dot(a, b):
    return jnp.dot(a, b, preferred_element_type=F32)


def _dot_nt(a, b):
    return lax.dot_general(a, b, (((1,), (1,)), ((), ())), preferred_element_type=F32)


def _dot_tn(a, b):
    return lax.dot_general(a, b, (((0,), (0,)), ((), ())), preferred_element_type=F32)


def _split3(f):
    f1 = f.astype(BF16)
    r = f - f1.astype(F32)
    f2 = r.astype(BF16)
    f3 = (r - f2.astype(F32)).astype(BF16)
    return f1, f2, f3


def _sel_cols(mask01, f):
    p1, p2, p3 = _split3(f)
    return _dot(mask01, p1) + _dot(mask01, p2) + _dot(mask01, p3)


def _sel_rows(f, mask01):
    p1, p2, p3 = _split3(f)
    return _dot_nt(p1, mask01) + _dot_nt(p2, mask01) + _dot_nt(p3, mask01)


def _pack_rows(u):
    n = u.shape[1] // 2
    bits = lax.bitcast_convert_type(u.astype(BF16).astype(F32), jnp.uint32)
    return jnp.bitwise_or(bits[:, :n], jnp.right_shift(bits[:, n:], jnp.uint32(16)))


def _unpack_rows(p):
    hi = lax.bitcast_convert_type(jnp.bitwise_and(p, jnp.uint32(0xFFFF0000)), F32)
    lo = lax.bitcast_convert_type(jnp.left_shift(p, jnp.uint32(16)), F32)
    return jnp.concatenate([hi, lo], axis=1)


def _rms(u, g):
    return u * lax.rsqrt(jnp.mean(u * u, axis=-1, keepdims=True) + EPS) * g


def _head_rms(u, g, n_heads):
    w = u.shape[-1] // n_heads
    return jnp.concatenate([_rms(u[:, h * w:(h + 1) * w], g[:, h * w:(h + 1) * w]) for h in range(n_heads)], axis=1)


def _row_ids(tile_idx, tm):
    return tile_idx * tm + lax.broadcasted_iota(I32, (tm, 1), 0)


def _mod(modl_ref, modc_ref, idx, is_ctx):
    return jnp.where(is_ctx, modc_ref[idx:idx + 1, :], modl_ref[0, idx:idx + 1, :])


def _adaln(x, g, modl_ref, modc_ref, shift_idx, is_ctx):
    shift = _mod(modl_ref, modc_ref, shift_idx, is_ctx)
    scale = _mod(modl_ref, modc_ref, shift_idx + 1, is_ctx)
    return _rms(x, g) * (1.0 + scale) + shift


def _neighbour_tiles(r, step, is_lat):
    n_tiles, per_seg = r.shape[0], GRID_W // SUBLANES
    zero = jnp.zeros((1,) + r.shape[1:], r.dtype)

    def across(src):
        return jnp.where(is_lat, 0.0, r[src:src + 1]) if 0 <= src < n_tiles else zero

    pieces = []
    for s0 in range(0, n_tiles, per_seg):
        if step > 0:
            pieces += [across(s0 - 1), r[s0:s0 + per_seg - 1]]
        else:
            pieces += [r[s0 + 1:s0 + per_seg], across(s0 + per_seg)]
    return jnp.concatenate(pieces, axis=0)


def _dwconv(y, cw, cb, is_lat):
    tm, n = y.shape
    y3 = y.reshape(tm // SUBLANES, SUBLANES, n)
    sub = lax.broadcasted_iota(I32, (1, SUBLANES, 1), 1)
    w = lambda j: cw[j:j + 1, :].reshape(1, 1, n)
    r1, r2, r7 = pltpu.roll(y3, 1, 1), pltpu.roll(y3, 2, 1), pltpu.roll(y3, SUBLANES - 1, 1)
    ym1 = jnp.where(sub >= 1, r1, _neighbour_tiles(r1, 1, is_lat))
    ym2 = jnp.where(sub >= 2, r2, _neighbour_tiles(r2, 1, is_lat))
    yp1 = jnp.where(sub <= SUBLANES - 2, r7, _neighbour_tiles(r7, -1, is_lat))
    out = cb.reshape(1, 1, n) + w(0) * ym2 + w(1) * ym1 + w(2) * y3 + w(3) * yp1
    return out.reshape(tm, n)


def _resident(shape):
    nd = len(shape)
    return pl.BlockSpec(shape, lambda *_: (0,) * nd, pipeline_mode=pl.Buffered(1))


def _big_row_tile(t):
    return max(tm for tm in range(8, BIG_ROW_TILE_MAX + 1, 8) if t % tm == 0)


def _mod_kernel(c_ref, w_ref, b_ref, o_ref):
    a1, a2, _ = _split3(_silu(c_ref[...]))
    w = w_ref[0]
    wh = w.astype(BF16)
    wl = (w - wh.astype(F32)).astype(BF16)
    o_ref[0] = _dot(a1, wh) + _dot(a2, wh) + _dot(a1, wl) + b_ref[0]


def _modulation(c_all, mod_w, mod_b):
    depth, d, n6 = mod_w.shape
    rows = c_all.shape[0]
    tn = 1024
    return pl.pallas_call(
        _mod_kernel,
        grid=(depth, n6 // tn),
        in_specs=[pl.BlockSpec((rows, d), lambda l, j: (0, 0)),
                  pl.BlockSpec((1, d, tn), lambda l, j: (l, 0, j)),
                  pl.BlockSpec((1, 1, tn), lambda l, j: (l, 0, j))],
        out_specs=pl.BlockSpec((1, rows, tn), lambda l, j: (l, 0, j)),
        out_shape=jax.ShapeDtypeStruct((depth, rows, n6), F32),
        compiler_params=_cparams("arbitrary", "arbitrary"),
        name="modulation",
    )(c_all, mod_w, mod_b.reshape(depth, 1, n6))


def _project(h_ref, w_ref, w_col, out_ref, epilogue):
    for c in range(0, out_ref.shape[1], PROJ_COLS):
        y = _dot(h_ref[...], w_ref[:, w_col + c:w_col + c + PROJ_COLS])
        out_ref[:, c:c + PROJ_COLS] = epilogue(y, c).astype(out_ref.dtype)


def _part_is_latent(part, parts_per_batch):
    return lax.rem(pl.program_id(0) * IN_PARTS + part, parts_per_batch) != 0


def _conv_parts(y, cw, cb, lat):
    return jnp.concatenate([_dwconv(y[p * ROW_TILE:(p + 1) * ROW_TILE], cw, cb, lat[p]) for p in range(IN_PARTS)],
                           axis=0)


def _in_ab_kernel(parts_per_batch, *refs):
    xs, refs = refs[:3 * IN_PARTS], refs[3 * IN_PARTS:]
    modc_ref, g_ref, w_ref, mcw_ref, mcb_ref, lcw_ref, lcb_ref, gb_ref = refs[:8]
    x_ref, q_ref, k_ref, v_ref, og_ref, lx_ref, glg_ref, gates_ref, gates_t_ref, h_scr = refs[8:]
    lat = [_part_is_latent(p, parts_per_batch) for p in range(IN_PARTS)]
    for p in range(IN_PARTS):
        xc_ref, xl_ref, modl_ref = xs[3 * p:3 * p + 3]
        rows = slice(p * ROW_TILE, (p + 1) * ROW_TILE)
        xp = jnp.where(lat[p], xl_ref[...], xc_ref[...])
        x_ref[rows, :] = xp
        h_scr[rows, :] = _adaln(xp, g_ref[...], modl_ref, modc_ref, 0, jnp.logical_not(lat[p])).astype(BF16)

    def conv(cw_ref, cb_ref):
        return lambda y, c: _conv_parts(y, cw_ref[:, c:c + PROJ_COLS], cb_ref[:, c:c + PROJ_COLS], lat)

    qc, kc, lc = conv(mcw_ref, mcb_ref), conv(mcw_ref, mcb_ref), conv(lcw_ref, lcb_ref)
    _project(h_scr, w_ref, 0, q_ref, lambda y, c: _silu(qc(y, c)))
    _project(h_scr, w_ref, 512, k_ref, lambda y, c: _silu(kc(y, 512 + c)) * (M_DK ** -0.5))
    _project(h_scr, w_ref, 1024, v_ref, lambda y, c: y)
    _project(h_scr, w_ref, 2048, og_ref, lambda y, c: _sigmoid(y))
    _project(h_scr, w_ref, 3072, lx_ref, lc)
    _project(h_scr, w_ref, 4096, glg_ref, lambda y, c: jax.nn.gelu(y))
    gt = _dot(h_scr[...], w_ref[:, 5120:5248]) + gb_ref[...]
    lane = lax.broadcasted_iota(I32, gt.shape, 1)
    gates = jnp.where(lane >= 2 * M_HEADS, _log_sigmoid(gt), gt)
    gates_ref[...] = gates
    gates_t_ref[...] = gates.T


def _in_ab(xc2, xl2, modl, modc, g, w, mcw, mcb, lcw, lcb, gb, bsz, t, ctx_len):
    n = bsz * t
    tm = IN_PARTS * ROW_TILE
    ppb = t // ROW_TILE
    lpb = ppb - 1
    part = lambda j, p: j * IN_PARTS + p
    batch = lambda j, p: part(j, p) // ppb
    x_specs, x_args = [], []
    for p in range(IN_PARTS):
        x_specs += [pl.BlockSpec((ROW_TILE, D_MODEL), lambda j, p=p: (batch(j, p), 0)),
                    pl.BlockSpec((ROW_TILE, D_MODEL),
                                 lambda j, p=p: (batch(j, p) * lpb + jnp.maximum(part(j, p) % ppb - 1, 0), 0)),
                    pl.BlockSpec((1, 8, D_MODEL), lambda j, p=p: (batch(j, p), 0, 0))]
        x_args += [xc2, xl2, modl]
    row = lambda c: pl.BlockSpec((tm, c), lambda j: (j, 0))
    outs = [(D_MODEL, F32), (512, BF16), (512, BF16), (1024, BF16), (1024, BF16), (1024, BF16), (1024, BF16),
            (LANES, F32)]
    params = (modc, g, w, mcw, mcb, lcw, lcb, gb)
    return pl.pallas_call(
        functools.partial(_in_ab_kernel, ppb),
        grid=(n // tm,),
        in_specs=x_specs + [_resident(a.shape) for a in params],
        out_specs=[row(c) for c, _ in outs] + [pl.BlockSpec((LANES, tm), lambda j: (0, j))],
        out_shape=[jax.ShapeDtypeStruct((n, c), dt) for c, dt in outs] + [jax.ShapeDtypeStruct((LANES, n), F32)],
        scratch_shapes=[pltpu.VMEM((tm, D_MODEL), BF16)],
        compiler_params=_cparams("arbitrary"),
        name="in_proj_ab",
    )(*x_args, *params)


def _reverse_order(n_ctx_chunks, n_chunks):
    return lambda c: jnp.where(c < n_ctx_chunks, n_ctx_chunks - 1 - c, n_chunks - 1 - (c - n_ctx_chunks))


def _tri_mask(L, reverse):
    row = lax.broadcasted_iota(I32, (L, L), 0)
    col = lax.broadcasted_iota(I32, (L, L), 1)
    return (col >= row) if reverse else (col <= row)


def _bidir_scan(kern, L, row_ins, lane_ins, res_ins, out_cols, scratch, bsz, t, ctx_len, name):
    nc = t // L
    rev = _reverse_order(ctx_len // L, nc)
    fwd = lambda i: i
    n = bsz * t

    def row(c, order):
        return pl.BlockSpec((L, c), lambda b, i: (b * nc + order(i), 0))

    def lanes(r, order):
        return pl.BlockSpec((r, L), lambda b, i: (0, b * nc + order(i)))

    in_specs, args = [], []
    for order in (fwd, rev):
        in_specs += [row(a.shape[1], order) for a in row_ins] + [lanes(r, order) for _, r in lane_ins]
        args += list(row_ins) + [a for a, _ in lane_ins]
    in_specs += [_resident(a.shape) for a in res_ins]
    args += list(res_ins)
    return pl.pallas_call(
        kern,
        grid=(bsz, nc),
        in_specs=in_specs,
        out_specs=[row(c, fwd) for c in out_cols] + [row(c, rev) for c in out_cols],
        out_shape=[jax.ShapeDtypeStruct((n, c), BF16) for c in out_cols] * 2,
        scratch_shapes=scratch,
        compiler_params=_cparams("arbitrary", "arbitrary"),
        name=name,
    )(*args)


def _zero_at_start(*scratch):
    @pl.when(pl.program_id(1) == 0)
    def _():
        for s in scratch:
            s[...] = jnp.zeros_like(s)


def _mlstm_kernel(L, qf, kf, vf, gcf, grf, qb, kb, vb, gcb, grb, of, ob, c_scr, m_scr):
    _zero_at_start(c_scr, m_scr)
    ones = jnp.ones((L, LANES), BF16)
    wide = lambda c, k: jnp.concatenate([c] * k, axis=1)
    er = lax.broadcasted_iota(I32, (LANES, 2 * M_HEADS * LANES), 0)
    ej = jnp.right_shift(lax.broadcasted_iota(I32, (LANES, 2 * M_HEADS * LANES), 1), LANES.bit_length() - 1)
    chains = []
    for d, (q_ref, k_ref, v_ref, gc_ref, gr_ref, out_ref) in enumerate(((qf, kf, vf, gcf, grf, of),
                                                                         (qb, kb, vb, gcb, grb, ob))):
        reverse = d == 1
        mask = _tri_mask(L, reverse)
        mask01 = mask.astype(BF16)
        gc = gc_ref[...]
        gr = gr_ref[...]
        bcol = _sel_cols(mask01, gc)
        brow = _sel_rows(gr, mask01)
        glane = lax.broadcasted_iota(I32, (L, LANES), 1)
        src = jnp.where(ej < M_HEADS, M_HEADS * d + ej, M_HEADS + M_HEADS * d + ej)
        s1, s2, _ = _split3(jnp.where(glane < 2 * M_HEADS, gc, bcol))
        sel = (er == src).astype(BF16)
        rep = _dot(s1, sel) + _dot(s2, sel)
        last = 0 if reverse else L - 1
        for h in range(M_HEADS):
            fo, io = 2 * M_HEADS + M_HEADS * d + h, M_HEADS * d + h
            ic, bc = rep[:, h * LANES:(h + 1) * LANES], rep[:, (M_HEADS + h) * LANES:(M_HEADS + h + 1) * LANES]
            r1 = gr[io:io + 1, :] - brow[fo:fo + 1, :]
            m = m_scr[d, h:h + 1, :]
            g = bc + m
            mx = jnp.max(jnp.where(mask, r1, -jnp.inf), axis=1, keepdims=True)
            mt = jnp.maximum(g, bc + mx)
            p = jnp.exp(jnp.where(mask, wide(bc - mt, L // LANES) + r1, -jnp.inf))
            bl = bc[last:last + 1, :]
            a_s = bl - bc + ic
            m_new = jnp.maximum(bl + m, jnp.max(a_s, axis=0, keepdims=True))
            chains.append(dict(
                d=d, h=h, out_ref=out_ref, mt=mt, w_inter=jnp.exp(g - mt), p=p,
                m_new=m_new, decay=jnp.exp(bl + m - m_new), ws=jnp.exp(a_s - m_new),
                qh=q_ref[:, h * M_DK:(h + 1) * M_DK], kh=k_ref[:, h * M_DK:(h + 1) * M_DK],
                vx=jnp.concatenate([v_ref[:, h * M_DV:(h + 1) * M_DV], ones], axis=1)))
    for ch in chains:
        ch["sc"] = (_dot_nt(ch["qh"], ch["kh"]) * ch["p"]).astype(BF16)
    for ch in chains:
        cx = c_scr[ch["d"], ch["h"]]
        numx = wide(ch["w_inter"], 1 + M_DV // LANES) * _dot(ch["qh"], cx.astype(BF16)) + _dot(ch["sc"], ch["vx"])
        den = numx[:, M_DV:]
        rinv = 1.0 / jnp.maximum(jnp.abs(den), jnp.exp(-ch["mt"]))
        hh = numx[:, 0:M_DV] * wide(rinv, M_DV // LANES)
        ch["out_ref"][:, ch["h"] * M_DV:(ch["h"] + 1) * M_DV] = hh.astype(BF16)
    for ch in chains:
        d, h = ch["d"], ch["h"]
        kw = (ch["kh"].astype(F32) * ch["ws"]).astype(BF16)
        c_scr[d, h] = wide(ch["decay"], 1 + M_DV // LANES) * c_scr[d, h] + _dot_tn(kw, ch["vx"])
        m_scr[d, h:h + 1, :] = ch["m_new"]


def _mlstm(q, k, v, gcol, grow, bsz, t, ctx_len):
    L = M_CHUNK
    scratch = [pltpu.VMEM((2, M_HEADS, M_DK, M_DV + LANES), F32), pltpu.VMEM((2, 8, LANES), F32)]
    return _bidir_scan(functools.partial(_mlstm_kernel, L), L, [q, k, v, gcol], [(grow, 16)], [], [1024], scratch,
                       bsz, t, ctx_len, "mlstm_scan")


def _lru_kernel(L, uf, ub_, wa_ref, wx_ref, ba_ref, bx_ref, lam_ref, of, ob, a_scr, b_scr, h_scr):
    _zero_at_start(h_scr)
    R, C = L // SUBLANES, LRU_BLOCKS * LRU_BW
    i0, i1 = lax.broadcasted_iota(I32, (L, L), 0), lax.broadcasted_iota(I32, (L, L), 1)
    when = lambda p: jnp.bitwise_and(p, SUBLANES - 1) * R + jnp.right_shift(p, SUBLANES.bit_length() - 1)
    regroup = (i1 == when(i0)).astype(BF16)
    ungroup = (i0 == when(i1)).astype(BF16)
    sub = lax.broadcasted_iota(I32, (SUBLANES, 1), 0)
    dirs = ((uf, of), (ub_, ob))
    for d, (u_ref, _) in enumerate(dirs):
        u = _dot(regroup, u_ref[...])
        ub = u.astype(BF16)
        sp = -LRU_C * _softplus(-lam_ref[d])
        for n in range(LRU_BLOCKS):
            sl = slice(n * LRU_BW, (n + 1) * LRU_BW)
            rg = _sigmoid(_dot(ub[:, sl], wa_ref[d, n]) + ba_ref[d, :, sl])
            ig = _sigmoid(_dot(ub[:, sl], wx_ref[d, n]) + bx_ref[d, :, sl])
            log_a = rg * sp[:, sl]
            a = jnp.exp(log_a)
            a_scr[d, :, sl] = a
            b_scr[d, :, sl] = jnp.sqrt(-jnp.tanh(log_a) * (a * a + 1.0)) * (ig * u[:, sl])
    S = SUBLANES
    hs = [jnp.zeros((S, C), F32), jnp.zeros((S, C), F32)]
    ps = [jnp.ones((S, C), F32), jnp.ones((S, C), F32)]
    for j in range(R):
        for d in range(2):
            rows = slice(S * j, S * j + S) if d == 0 else slice(S * (R - 1 - j), S * (R - j))
            a = a_scr[d, rows, :]
            hs[d] = a * hs[d] + b_scr[d, rows, :]
            ps[d] = a * ps[d]
            b_scr[d, rows, :] = hs[d]
            a_scr[d, rows, :] = ps[d]
    for d, (_, out_ref) in enumerate(dirs):
        c = h_scr[d:d + 1, :]
        cin = jnp.zeros((S, C), F32)
        for s in (range(S) if d == 0 else reversed(range(S))):
            cin = jnp.where(sub == s, c, cin)
            c = ps[d][s:s + 1, :] * c + hs[d][s:s + 1, :]
        h_scr[d:d + 1, :] = c
        h = b_scr[d].reshape(R, S, C) + a_scr[d].reshape(R, S, C) * cin[None]
        out_ref[...] = _dot(ungroup, h.reshape(L, C).astype(BF16)).astype(BF16)


def _lru(u, wa, wx, ba, bx, lam, bsz, t, ctx_len):
    L = LRU_TILE
    scratch = [pltpu.VMEM((2, L, 1024), F32), pltpu.VMEM((2, L, 1024), F32), pltpu.VMEM((8, 1024), F32)]
    return _bidir_scan(functools.partial(_lru_kernel, L), L, [u], [], [wa, wx, ba, bx, lam], [1024], scratch,
                       bsz, t, ctx_len, "rglru_scan")


def _swiglu_rows(x, is_ctx, modl_ref, modc_ref, g2_ref, g3_ref, w1_ref, w2_ref):
    h = _adaln(x, g2_ref[...], modl_ref, modc_ref, 3, is_ctx).astype(BF16)
    ff = w2_ref.shape[0]
    acc = jnp.zeros(x.shape, F32)
    for j in range(ff // FF_STEP):
        lo = j * FF_STEP
        gj = _dot(h, w1_ref[:, lo:lo + FF_STEP])
        uj = _dot(h, w1_ref[:, ff + lo:ff + lo + FF_STEP])
        acc = acc + _dot((_silu(gj) * uj).astype(BF16), w2_ref[lo:lo + FF_STEP, :])
    return x + _mod(modl_ref, modc_ref, 5, is_ctx) * _rms(acc, g3_ref[...])


def _out_tail(tm, ctx_len, a1, a2, x_ref, modl_ref, modc_ref, g_ref, w_ref, xo_ref, ffn_refs=None):
    is_ctx = False if ctx_len is None else _row_ids(pl.program_id(1), tm) < ctx_len
    half = w_ref.shape[0] // 2
    y = _dot(a1.astype(BF16), w_ref[0:half, :]) + _dot(a2.astype(BF16), w_ref[half:, :])
    x = x_ref[...] + _mod(modl_ref, modc_ref, 2, is_ctx) * _rms(y, g_ref[...])
    if ffn_refs is not None:
        x = _swiglu_rows(x, is_ctx, modl_ref, modc_ref, *ffn_refs)
    xo_ref[...] = x


def _out_ab_kernel(tm, ctx_len, hf, hb, og, lf, lb, glg, nw, x_ref, modl_ref, modc_ref, g_ref, w_ref,
                   g2_ref, g3_ref, w1_ref, w2_ref, xo_ref):
    hm = _head_rms(hf[...].astype(F32) + hb[...].astype(F32), nw[...], M_HEADS) * og[...].astype(F32)
    hl = (lf[...].astype(F32) + lb[...].astype(F32)) * glg[...].astype(F32)
    _out_tail(tm, ctx_len, hm, hl, x_ref, modl_ref, modc_ref, g_ref, w_ref, xo_ref,
              (g2_ref, g3_ref, w1_ref, w2_ref))


def _out_cd_kernel(tm, ctx_len, gf, gb, grs, yf, yb, sx, zs, gnw, dsk, snw, x_ref, modl_ref, modc_ref, g_ref,
                   w_ref, xo_ref):
    og = _head_rms(gf[...].astype(F32) + gb[...].astype(F32), gnw[...], G_HEADS) * grs[...].astype(F32)
    ys = yf[...].astype(F32) + yb[...].astype(F32) + dsk[...] * sx[...].astype(F32)
    ys = _rms(ys * zs[...].astype(F32), snw[...])
    _out_tail(tm, ctx_len, og, ys, x_ref, modl_ref, modc_ref, g_ref, w_ref, xo_ref)


def _out_proj(kern, name, acts, vecs, x2, modl, modc, g, w, bsz, t, ctx_len, extra=(), latent_only=False):
    if latent_only:
        tm = LATENT_TILE
        nt = (t - ctx_len) // tm
        row = lambda c: _latent_rows(tm, t, ctx_len, c)
        out_spec = pl.BlockSpec((tm, D_MODEL), lambda b, i: (b * nt + i, 0))
        out_rows = bsz * (t - ctx_len)
    else:
        tm = _big_row_tile(t)
        nt = t // tm
        row = lambda c: pl.BlockSpec((tm, c), lambda b, i: (b * nt + i, 0))
        out_spec, out_rows = row(D_MODEL), bsz * t
    return pl.pallas_call(
        functools.partial(kern, tm, None if latent_only else ctx_len),
        grid=(bsz, nt),
        in_specs=[row(1024)] * len(acts) + [_resident((1, 1024))] * len(vecs)
        + [row(D_MODEL), pl.BlockSpec((1, 8, D_MODEL), lambda b, i: (b, 0, 0)),
           _resident((8, D_MODEL)), _resident((1, D_MODEL)), _resident(w.shape)]
        + [_resident(a.shape) for a in extra],
        out_specs=out_spec,
        out_shape=jax.ShapeDtypeStruct((out_rows, D_MODEL), F32),
        compiler_params=_cparams("arbitrary", "arbitrary"),
        name=name,
    )(*acts, *vecs, x2, modl, modc, g, w, *extra)


def _in_cd_kernel(parts_per_batch, x_ref, *refs):
    modls, refs = refs[:IN_PARTS], refs[IN_PARTS:]
    modc_ref, g_ref, w_ref, cw_ref, cb_ref, dtb_ref, aneg_ref = refs[:7]
    gq_ref, gk_ref, gv_ref, gr_ref, z_ref, sx_ref, sb_ref, sc_ref, ga_ref, dts_ref, dts_t_ref, h_scr = refs[7:]
    lat = [_part_is_latent(p, parts_per_batch) for p in range(IN_PARTS)]
    for p in range(IN_PARTS):
        rows = slice(p * ROW_TILE, (p + 1) * ROW_TILE)
        h_scr[rows, :] = _adaln(x_ref[rows, :], g_ref[...], modls[p], modc_ref, 0,
                                jnp.logical_not(lat[p])).astype(BF16)

    def conv_silu(col0):
        return lambda y, c: _silu(_conv_parts(y, cw_ref[:, col0 + c:col0 + c + PROJ_COLS],
                                              cb_ref[:, col0 + c:col0 + c + PROJ_COLS], lat))

    _project(h_scr, w_ref, 0, gq_ref, lambda y, c: y * (G_DK ** -0.5))
    _project(h_scr, w_ref, 512, gk_ref, lambda y, c: y)
    _project(h_scr, w_ref, 1024, gv_ref, lambda y, c: y)
    _project(h_scr, w_ref, 2048, gr_ref, lambda y, c: _silu(y))
    _project(h_scr, w_ref, 3072, z_ref, lambda y, c: _silu(y))
    _project(h_scr, w_ref, 4096, sx_ref, conv_silu(0))
    _project(h_scr, w_ref, 5120, sb_ref, conv_silu(1024))
    _project(h_scr, w_ref, 5376, sc_ref, conv_silu(1280))
    ga_ref[...] = _dot(h_scr[...], w_ref[:, 5632:5760])
    dt = _softplus(_dot(h_scr[...], w_ref[:, 5760:5888]) + dtb_ref[...])
    lane = lax.broadcasted_iota(I32, dt.shape, 1)
    dts = jnp.where(lane < 2 * S_HEADS, dt, dt * aneg_ref[...])
    dts_ref[...] = dts
    dts_t_ref[...] = dts.T


def _in_cd(x2, modl, modc, g, w, cw, cb, dtb, aneg, bsz, t, ctx_len):
    n = bsz * t
    tm = IN_PARTS * ROW_TILE
    ppb = t // ROW_TILE
    row = lambda c: pl.BlockSpec((tm, c), lambda j: (j, 0))
    mod_specs = [pl.BlockSpec((1, 8, D_MODEL), lambda j, p=p: ((j * IN_PARTS + p) // ppb, 0, 0))
                 for p in range(IN_PARTS)]
    outs = [(512, BF16), (512, BF16), (1024, BF16), (1024, BF16), (1024, BF16), (1024, BF16), (256, BF16),
            (256, BF16), (LANES, F32), (LANES, F32)]
    params = (modc, g, w, cw, cb, dtb, aneg)
    return pl.pallas_call(
        functools.partial(_in_cd_kernel, ppb),
        grid=(n // tm,),
        in_specs=[row(D_MODEL)] + mod_specs + [_resident(a.shape) for a in params],
        out_specs=[row(c) for c, _ in outs] + [pl.BlockSpec((LANES, tm), lambda j: (0, j))],
        out_shape=[jax.ShapeDtypeStruct((n, c), dt) for c, dt in outs] + [jax.ShapeDtypeStruct((LANES, n), F32)],
        scratch_shapes=[pltpu.VMEM((tm, D_MODEL), BF16)],
        compiler_params=_cparams("arbitrary"),
        name="in_proj_cd",
    )(x2, *([modl] * IN_PARTS), *params)


def _gla_kernel(L, C, qf, kf, vf, gaf, qb, kb, vb, gab, awh_ref, awl_ref, ab_ref, of, ob, s_scr, att_scr):
    _zero_at_start(s_scr, att_scr)
    nb = L // C
    chains = []
    for d, (q_ref, k_ref, v_ref, ga_ref, out_ref) in enumerate(((qf, kf, vf, gaf, of), (qb, kb, vb, gab, ob))):
        reverse = d == 1
        mask = _tri_mask(L, reverse)
        g1, g2, _ = _split3(ga_ref[...])
        pre = _dot(g1, awh_ref[d]) + _dot(g2, awh_ref[d]) + _dot(g1, awl_ref[d])
        lg = _log_sigmoid(pre + ab_ref[d]) * (1.0 / G_TAU)
        ball = _sel_cols(mask.astype(BF16), lg)
        for h in range(G_HEADS):
            ks = slice(h * G_DK, (h + 1) * G_DK)
            b = ball[:, ks] * LOG2E
            qh = q_ref[:, ks].astype(F32)
            inter = _dot_nt((qh * jnp.exp2(b)).astype(BF16), s_scr[d, h].astype(BF16))
            chains.append(dict(d=d, h=h, reverse=reverse, mask=mask, b=b, qh=qh, kh=k_ref[:, ks].astype(F32),
                               vh=v_ref[:, h * G_DV:(h + 1) * G_DV], inter=inter, out_ref=out_ref, blocks=[]))
    T = min(G_GROUP, L)
    for i in range(nb):
        lo, hi = i * C, (i + 1) * C
        t0 = lo // T * T
        for c, ch in enumerate(chains):
            b = ch["b"]
            if ch["reverse"]:
                bref = b[hi:hi + 1, :] if i < nb - 1 else jnp.zeros((1, G_DK), F32)
            else:
                bref = b[lo - 1:lo, :] if i > 0 else jnp.zeros((1, G_DK), F32)
            qi = (ch["qh"][lo:hi, :] * jnp.exp2(b[lo:hi, :] - bref)).astype(BF16)
            ki = (ch["kh"][t0:t0 + T] * jnp.exp2(bref - b[t0:t0 + T])).astype(BF16)
            att_scr[c, lo:hi, t0:t0 + T] = _dot_nt(qi, ki)
    for g in range(L // T):
        t0, t1 = g * T, (g + 1) * T
        for c, ch in enumerate(chains):
            b = ch["b"]
            e0, e1 = (t1, L) if ch["reverse"] else (0, t0)
            if e1 > e0:
                bref = b[t1:t1 + 1, :] if ch["reverse"] else b[t0 - 1:t0, :]
                qg = (ch["qh"][t0:t1] * jnp.exp2(b[t0:t1] - bref)).astype(BF16)
                kg = (ch["kh"][e0:e1] * jnp.exp2(bref - b[e0:e1])).astype(BF16)
                att_scr[c, t0:t1, e0:e1] = _dot_nt(qg, kg)
    for c, ch in enumerate(chains):
        att = jnp.where(ch["mask"], att_scr[c], 0.0).astype(BF16)
        oh = ch["inter"] + _dot(att, ch["vh"])
        ch["out_ref"][:, ch["h"] * G_DV:(ch["h"] + 1) * G_DV] = oh.astype(BF16)
    for ch in chains:
        d, h, b = ch["d"], ch["h"], ch["b"]
        last = 0 if ch["reverse"] else L - 1
        bl = b[last:last + 1, :]
        kd = (ch["kh"] * jnp.exp2(bl - b)).astype(BF16)
        s_scr[d, h] = s_scr[d, h] * jnp.exp2(bl) + _dot_tn(ch["vh"], kd)


def _gla(q, k, v, ga, awh, awl, ab, bsz, t, ctx_len):
    L, C = G_CHUNK, G_SUB
    scratch = [pltpu.VMEM((2, G_HEADS, G_DV, G_DK), F32), pltpu.VMEM((2 * G_HEADS, L, L), F32)]
    return _bidir_scan(functools.partial(_gla_kernel, L, C), L, [q, k, v, ga], [], [awh, awl, ab], [1024], scratch,
                       bsz, t, ctx_len, "gla_scan")


def _ssd_kernel(L, xf, bf, cf, gcf, grf, xb, bb, cb_, gcb, grb, of, ob, s_scr):
    _zero_at_start(s_scr)
    er = lax.broadcasted_iota(I32, (LANES, S_HEADS * S_P), 0)
    ec = jnp.right_shift(lax.broadcasted_iota(I32, (LANES, S_HEADS * S_P), 1), S_P.bit_length() - 1)
    glane = lax.broadcasted_iota(I32, (L, LANES), 1)
    lane = lax.broadcasted_iota(I32, (L, 2 * S_P), 1)
    dirs, chains = [], []
    for d, (x_ref, b_ref, c_ref, gc_ref, gr_ref, out_ref) in enumerate(((xf, bf, cf, gcf, grf, of),
                                                                         (xb, bb, cb_, gcb, grb, ob))):
        reverse = d == 1
        mask = _tri_mask(L, reverse)
        mask01 = mask.astype(BF16)
        gc = gc_ref[...]
        gr = gr_ref[...]
        bcol = _sel_cols(mask01, gc)
        brow = _sel_rows(gr, mask01)
        last = 0 if reverse else L - 1
        dt_o, la_o = S_HEADS * d, 2 * S_HEADS + S_HEADS * d
        sel_la = (er == ec + la_o).astype(BF16)
        x = x_ref[...]
        bla = jnp.where((glane >= la_o) & (glane < la_o + S_HEADS), bcol, 0.0)
        dt_at_la = pltpu.roll(gc, 2 * S_HEADS, 1)
        bl = bla[last:last + 1, :]
        ebx = _dot(jnp.exp(bla).astype(BF16), sel_la)
        wx = _dot((jnp.exp(bl - bla) * dt_at_la).astype(BF16), sel_la)
        dirs.append(dict(d=d, mask=mask, gr=gr, x=x, ebx=ebx, last=last, dt_o=dt_o, la_o=la_o, out_ref=out_ref,
                         xw=(x.astype(F32) * wx).astype(BF16),
                         bcol2=bcol * LOG2E, brow2=brow * LOG2E))
        for g in range(S_GROUPS):
            ns = slice(g * S_N, (g + 1) * S_N)
            chains.append(dict(dr=dirs[-1], g=g, bg=b_ref[:, ns], cg=c_ref[:, ns],
                               gs=slice(g * S_HPG * S_P, (g + 1) * S_HPG * S_P)))
    for ch in chains:
        dr = ch["dr"]
        ch["cb"] = _dot_nt(ch["cg"], ch["bg"])
        ch["inter"] = _dot(ch["cg"], s_scr[dr["d"], ch["g"]].astype(BF16)) * dr["ebx"][:, ch["gs"]]
    for ch in chains:
        dr, g = ch["dr"], ch["g"]
        parts = []
        for pr in range(S_HPG // 2):
            e0 = g * S_HPG + 2 * pr
            xp = dr["x"][:, e0 * S_P:(e0 + 2) * S_P]
            res = []
            for e in (e0, e0 + 1):
                col = dr["la_o"] + e
                bc, br = dr["bcol2"][:, col:col + 1], dr["brow2"][col:col + 1, :]
                dtr = dr["gr"][dr["dt_o"] + e:dr["dt_o"] + e + 1, :]
                att = (ch["cb"] * dtr * jnp.exp2(jnp.where(dr["mask"], bc - br, -jnp.inf))).astype(BF16)
                res.append(_dot(att, xp))
            parts.append(jnp.where(lane < S_P, res[0], res[1]))
        dr["out_ref"][:, ch["gs"]] = (jnp.concatenate(parts, axis=1) + ch["inter"]).astype(BF16)
    for ch in chains:
        dr, g, gs = ch["dr"], ch["g"], ch["gs"]
        s_scr[dr["d"], g] = (s_scr[dr["d"], g] * dr["ebx"][dr["last"]:dr["last"] + 1, gs]
                             + _dot_tn(ch["bg"], dr["xw"][:, gs]))


def _ssd(x, bm, cm, gcol, grow, bsz, t, ctx_len):
    L = S_CHUNK
    scratch = [pltpu.VMEM((2, S_GROUPS, S_N, S_HPG * S_P), F32)]
    return _bidir_scan(functools.partial(_ssd_kernel, L), L, [x, bm, cm, gcol], [(grow, 4 * S_HEADS)], [], [1024],
                       scratch,
                       bsz, t, ctx_len, "ssd_scan")


def _router_kernel(x_ref, modl_ref, g_ref, rw_ref, rb_ref, h_ref, idx_ref, gate_ref, cnt_ref, base_scr):
    @pl.when((pl.program_id(0) == 0) & (pl.program_id(1) == 0))
    def _():
        base_scr[...] = jnp.zeros_like(base_scr)

    h = _rms(x_ref[...], g_ref[...]) * (1.0 + modl_ref[0, 4:5, :]) + modl_ref[0, 3:4, :]
    h_ref[...] = _pack_rows(h)
    h1, h2, _ = _split3(h)
    both = _dot(h1, rw_ref[...])
    logits = both[:, :LANES] + both[:, LANES:] + _dot(h2, rw_ref[:, :LANES]) + rb_ref[...]
    lane = lax.broadcasted_iota(I32, logits.shape, 1)
    logits = jnp.where(lane < N_EXPERTS, logits, -jnp.inf)
    lanef = lane.astype(F32)
    m1 = jnp.max(logits, axis=1, keepdims=True)
    i1 = jnp.min(jnp.where(logits == m1, lanef, float(LANES)), axis=1, keepdims=True)
    rest = jnp.where(lanef == i1, -jnp.inf, logits)
    m2 = jnp.max(rest, axis=1, keepdims=True)
    i2 = jnp.min(jnp.where(rest == m2, lanef, float(LANES)), axis=1, keepdims=True)
    e = jnp.exp(m2 - m1)
    g1 = 1.0 / (1.0 + e)
    gate_ref[...] = jnp.where(lane == 0, g1, jnp.where(lane == 1, e * g1, 0.0))
    tm = logits.shape[0]
    oh1, oh2 = (lanef == i1).astype(F32), (lanef == i2).astype(F32)
    earlier = (lax.broadcasted_iota(I32, (tm, tm), 1) < lax.broadcasted_iota(I32, (tm, tm), 0)).astype(BF16)
    seen = base_scr[0:1, :] + _dot(earlier, (oh1 + oh2).astype(BF16))
    r1 = jnp.sum(oh1 * seen, axis=1, keepdims=True)
    r2 = jnp.sum(oh2 * seen, axis=1, keepdims=True)
    base_scr[0:1, :] = base_scr[0:1, :] + jnp.sum(oh1 + oh2, axis=0, keepdims=True)
    cnt_ref[...] = base_scr[...]
    idx_ref[...] = jnp.where(lane == 0, i1, jnp.where(lane == 1, i2, jnp.where(lane == 2, r1, jnp.where(
        lane == 3, r2, 0.0)))).astype(I32)


def _latent_rows(tm, t, ctx_len, cols=D_MODEL):
    return pl.BlockSpec((pl.Element(tm), pl.Element(cols)),
                        lambda b, i: (pl.multiple_of(b * t + ctx_len + i * tm, SUBLANES), 0))


def _router(x2, modl, g, rw, rb, bsz, t, ctx_len):
    tm = LATENT_TILE
    nl = bsz * (t - ctx_len)
    nlt = (t - ctx_len) // tm
    orow = lambda c: pl.BlockSpec((tm, c), lambda b, i: (b * nlt + i, 0))
    return pl.pallas_call(
        _router_kernel,
        grid=(bsz, nlt),
        in_specs=[_latent_rows(tm, t, ctx_len),
                  pl.BlockSpec((1, 8, D_MODEL), lambda b, i: (b, 0, 0)),
                  _resident((1, D_MODEL)), _resident(rw.shape), _resident(rb.shape)],
        out_specs=[orow(D_MODEL // 2), orow(LANES), orow(LANES), pl.BlockSpec((8, LANES), lambda b, i: (0, 0))],
        out_shape=[jax.ShapeDtypeStruct((nl, D_MODEL // 2), jnp.uint32), jax.ShapeDtypeStruct((nl, LANES), I32),
                   jax.ShapeDtypeStruct((nl, LANES), F32), jax.ShapeDtypeStruct((8, LANES), F32)],
        scratch_shapes=[pltpu.VMEM((8, LANES), F32)],
        compiler_params=_cparams("arbitrary", "arbitrary"),
        name="moe_router",
    )(x2, modl, g, rw, rb)


def _gather_rows(table, idx):
    n_rows, d = idx.shape[0], table.shape[1]
    n_workers = SC_CORES * SC_SUBCORES
    per_worker = n_rows // n_workers
    chunk = SC_GATHER_ROWS
    assert per_worker * n_workers == n_rows and per_worker % chunk == 0
    mesh = plsc.VectorSubcoreMesh(core_axis_name="c", subcore_axis_name="s")

    @functools.partial(
        pl.kernel, mesh=mesh, out_type=jax.ShapeDtypeStruct((n_rows, d), table.dtype),
        scratch_types=[pltpu.VMEM((chunk,), I32), pltpu.VMEM((chunk, d), table.dtype), pltpu.SemaphoreType.DMA],
        name="sc_gather_rows")
    def gather(table_hbm, idx_hbm, out_hbm, idx_v, rows_v, sem):
        base = (lax.axis_index("s") * SC_CORES + lax.axis_index("c")) * per_worker

        @pl.loop(0, per_worker // chunk)
        def _(j):
            off = pl.multiple_of(base + j * chunk, 8)
            pltpu.sync_copy(idx_hbm.at[pl.ds(off, chunk)], idx_v)
            pltpu.async_copy(table_hbm.at[idx_v], rows_v, sem).wait()
            pltpu.sync_copy(rows_v, out_hbm.at[pl.ds(off, chunk)])

    return gather(table, idx)


def _dispatch_rows(table, slot, n_out):
    assert TOP_K == 2
    n_assign, (n_tok, d) = slot.shape[0], table.shape
    n_workers = SC_CORES * SC_SUBCORES
    per_worker = n_out // n_workers
    chunk, vec = SC_GATHER_ROWS, SC_LANES
    assert per_worker * n_workers == n_out and per_worker % chunk == 0 and n_assign % vec == 0
    assert n_out < 3 * n_tok
    mesh = plsc.VectorSubcoreMesh(core_axis_name="c", subcore_axis_name="s")

    @functools.partial(
        pl.kernel, mesh=mesh, out_type=jax.ShapeDtypeStruct((n_out, d), table.dtype),
        scratch_types=[pltpu.VMEM((n_assign,), I32), pltpu.VMEM((per_worker,), I32),
                       pltpu.VMEM((chunk, d), table.dtype), pltpu.SemaphoreType.DMA],
        compiler_params=pltpu.CompilerParams(needs_layout_passes=False),
        name="sc_dispatch_rows")
    def dispatch(table_hbm, slot_hbm, out_hbm, slots_v, tok_v, rows_v, sem):
        base = (lax.axis_index("s") * SC_CORES + lax.axis_index("c")) * per_worker
        pltpu.sync_copy(slot_hbm, slots_v)
        lane = lax.broadcasted_iota(I32, (vec,), 0)

        @pl.loop(0, per_worker // vec)
        def _(j):
            p = base + j * vec + lane
            p = jnp.where(p >= n_tok, p - n_tok, p)
            tok_v[pl.ds(j * vec, vec)] = jnp.where(p >= n_tok, p - n_tok, p)

        @pl.loop(0, n_assign // vec)
        def _(j):
            s = slots_v[pl.ds(j * vec, vec)] - base
            mine = (s >= 0) & (s < per_worker)
            tok = lax.shift_right_logical(j * vec + lane, 1)
            plsc.store_scatter(tok_v, [jnp.where(mine, s, 0)], tok, mask=mine)

        @pl.loop(0, per_worker // chunk)
        def _(j):
            off = pl.multiple_of(j * chunk, 8)
            pltpu.async_copy(table_hbm.at[tok_v.at[pl.ds(off, chunk)]], rows_v, sem).wait()
            pltpu.sync_copy(rows_v, out_hbm.at[pl.ds(base + off, chunk)])

    return dispatch(table, slot)


def _moe_kernel(be_ref, nu_ref, nv_ref, x_ref, w1g_ref, w1u_ref, w2_ref, y_ref, xb_ref, acc_ref):
    i, j = pl.program_id(0), pl.program_id(1)
    bm = x_ref.shape[0]

    def swiglu_rows(rows):
        x = xb_ref[0:rows, :]
        a = (_silu(_dot(x, w1g_ref[0].astype(BF16))) * _dot(x, w1u_ref[0].astype(BF16))).astype(BF16)
        acc_ref[0:rows, :] += _dot(a, w2_ref[0].astype(BF16))

    @pl.when(i < nu_ref[0])
    def _():
        @pl.when(j == 0)
        def _():
            acc_ref[...] = jnp.zeros_like(acc_ref)
            xb_ref[...] = _unpack_rows(x_ref[...]).astype(BF16)

        for rows in range(MOE_ROW_STEP, bm + 1, MOE_ROW_STEP):
            @pl.when((nv_ref[i] > rows - MOE_ROW_STEP) & (nv_ref[i] <= rows))
            def _(rows=rows):
                swiglu_rows(rows)

        @pl.when(j == pl.num_programs(1) - 1)
        def _():
            y_ref[...] = _pack_rows(acc_ref[...])


def _moe_experts(xs, w1, w2, block_e, n_used, n_valid, n_blocks):
    bm, fc = MOE_ROWS, MOE_FF
    nff = FF_EXPERT // fc
    used = lambda i, nu: jnp.minimum(i, nu[0] - 1)
    ffi = lambda i, j, nu: jnp.where(i < nu[0], j, nff - 1)
    grid_spec = pltpu.PrefetchScalarGridSpec(
        num_scalar_prefetch=3,
        grid=(n_blocks, nff),
        in_specs=[pl.BlockSpec((bm, D_MODEL // 2), lambda i, j, be, nu, nv: (used(i, nu), 0)),
                  pl.BlockSpec((1, D_MODEL, fc), lambda i, j, be, nu, nv: (be[used(i, nu)], 0, ffi(i, j, nu))),
                  pl.BlockSpec((1, D_MODEL, fc),
                               lambda i, j, be, nu, nv: (be[used(i, nu)], 0, nff + ffi(i, j, nu))),
                  pl.BlockSpec((1, fc, D_MODEL), lambda i, j, be, nu, nv: (be[used(i, nu)], ffi(i, j, nu), 0))],
        out_specs=pl.BlockSpec((bm, D_MODEL // 2), lambda i, j, be, nu, nv: (used(i, nu), 0)),
        scratch_shapes=[pltpu.VMEM((bm, D_MODEL), BF16), pltpu.VMEM((bm, D_MODEL), F32)],
    )
    return pl.pallas_call(
        _moe_kernel,
        grid_spec=grid_spec,
        out_shape=jax.ShapeDtypeStruct((n_blocks * bm, D_MODEL // 2), jnp.uint32),
        compiler_params=_cparams("arbitrary", "arbitrary"),
        name="moe_experts",
    )(block_e, n_used, n_valid, xs, w1, w1, w2)


def _combine_kernel(y0_ref, y1_ref, gate_ref, x_ref, modl_ref, g_ref, o_ref):
    gt = gate_ref[...]
    y = gt[:, 0:1] * _unpack_rows(y0_ref[...]) + gt[:, 1:2] * _unpack_rows(y1_ref[...])
    o_ref[...] = x_ref[...] + modl_ref[0, 5:6, :] * _rms(y, g_ref[...])


def _combine(yk, gates, x2, modl, g, bsz, t, ctx_len):
    tm = LATENT_TILE
    nlt = (t - ctx_len) // tm
    nl = bsz * (t - ctx_len)
    orow = lambda c: pl.BlockSpec((tm, c), lambda b, i: (b * nlt + i, 0))
    second = pl.BlockSpec((tm, D_MODEL // 2), lambda b, i: (nl // tm + b * nlt + i, 0))
    return pl.pallas_call(
        _combine_kernel,
        grid=(bsz, nlt),
        in_specs=[orow(D_MODEL // 2), second, orow(LANES),
                  _latent_rows(tm, t, ctx_len),
                  pl.BlockSpec((1, 8, D_MODEL), lambda b, i: (b, 0, 0)), _resident((1, D_MODEL))],
        out_specs=orow(D_MODEL),
        out_shape=jax.ShapeDtypeStruct((nl, D_MODEL), F32),
        compiler_params=_cparams("arbitrary", "arbitrary"),
        name="moe_combine",
    )(yk, yk, gates, x2, modl, g)


def _routing_tables(experts, ranks, counts):
    bm = MOE_ROWS
    n_assign = experts.size
    padded = (counts + bm - 1) // bm * bm
    ends_p = jnp.cumsum(padded)
    pstarts = ends_p - padded
    onehot = experts[..., None] == jnp.arange(N_EXPERTS, dtype=I32)
    slot = (jnp.sum(jnp.where(onehot, pstarts, 0), axis=-1) + ranks).astype(I32)
    n_blocks = n_assign // bm + N_EXPERTS
    block_e = jnp.minimum(jnp.sum(ends_p[None, :] <= (jnp.arange(n_blocks, dtype=I32) * bm)[:, None], axis=1),
                          N_EXPERTS - 1).astype(I32)
    n_used = (ends_p[-1] // bm).astype(I32).reshape(1)
    sel = block_e[:, None] == jnp.arange(N_EXPERTS, dtype=I32)
    rows_left = jnp.sum(jnp.where(sel, counts + pstarts, 0), axis=1) - jnp.arange(n_blocks, dtype=I32) * bm
    n_valid = jnp.clip(rows_left, 0, bm).astype(I32)
    return slot, block_e, n_used, n_valid, n_blocks


def kernel(x, c, ctx, c_ctx, mod_w, mod_b, norm_g, ab_in_w, m_conv_w, m_conv_b, m_gate_b, m_norm_w, l_conv_w,
           l_conv_b, l_wa, l_ba, l_wx, l_bx, l_lam, ab_out_w, ffn_w1, ffn_w2, cd_in_w, g_alpha_w, g_alpha_b,
           g_norm_w, s_conv_w, s_conv_b, s_dt_bias, s_A_log, s_D, s_norm_w, cd_out_w, router_w, router_b,
           moe_w1, moe_w2):
    bsz, seq, dm = x.shape
    ctx_len = ctx.shape[1]
    t = ctx_len + seq
    n = bsz * t
    assert ctx_len == ROW_TILE and seq % ROW_TILE == 0 and ROW_TILE % GRID_W == 0
    assert n % (IN_PARTS * ROW_TILE) == 0

    c_all = jnp.concatenate([c, c_ctx[None, :], jnp.zeros((-(bsz + 1) % 8, dm), F32)], axis=0)
    mods = _modulation(c_all, mod_w, mod_b).reshape(mod_w.shape[0], c_all.shape[0], 6, dm)
    pad2 = jnp.zeros((bsz, 2, dm), F32)

    def layer_mods(layer):
        modl = jnp.concatenate([mods[layer, :bsz], pad2], axis=1)
        modc = jnp.concatenate([mods[layer, bsz], pad2[0]], axis=0)
        return modl, modc

    modl, modc = layer_mods(0)
    g = norm_g[0]
    w = ab_in_w[0]
    o_qk, o_v, o_o, o_gt, o_lx, o_lg = 0, 1024, 2048, 3072, 3088, 4112
    w_in = jnp.concatenate([w[:, o_qk:o_v], w[:, o_v:o_o], w[:, o_o:o_gt], w[:, o_lx:o_lg], w[:, o_lg:],
                            w[:, o_gt:o_lx], jnp.zeros((dm, LANES - 16), F32)], axis=1).astype(BF16)
    gb = jnp.concatenate([m_gate_b[0].reshape(1, 16), jnp.zeros((1, LANES - 16), F32)], axis=1)
    x2, q, k, v, og, lx, glg, gates, gates_t = _in_ab(
        ctx.reshape(bsz * ctx_len, dm), x.reshape(bsz * seq, dm), modl, modc, g[0:1], w_in, m_conv_w[0],
        m_conv_b[0][None], l_conv_w[0], l_conv_b[0][None], gb, bsz, t, ctx_len)
    hm_f, hm_b = _mlstm(q, k, v, gates, gates_t, bsz, t, ctx_len)
    hl_f, hl_b = _lru(lx, l_wa[0].astype(BF16), l_wx[0].astype(BF16), l_ba[0][:, None], l_bx[0][:, None],
                      l_lam[0][:, None], bsz, t, ctx_len)
    x2 = _out_proj(_out_ab_kernel, "out_proj_swiglu_ab", [hm_f, hm_b, og, hl_f, hl_b, glg], [m_norm_w[0][None]],
                   x2, modl, modc, g[1:2], ab_out_w[0].astype(BF16), bsz, t, ctx_len,
                   extra=(g[2:3], g[3:4], ffn_w1[0].astype(BF16), ffn_w2[0].astype(BF16)))

    modl, modc = layer_mods(1)
    g = norm_g[1]
    w = cd_in_w[0]
    o_gv, o_gr, o_ga, o_z, o_xbc, o_dt = 1024, 2048, 3072, 3104, 4128, 5664
    zpad = jnp.zeros((dm, LANES - 32), F32)
    w_in = jnp.concatenate([w[:, :o_gv], w[:, o_gv:o_gr], w[:, o_gr:o_ga], w[:, o_z:o_xbc], w[:, o_xbc:o_dt],
                            w[:, o_ga:o_z], zpad, w[:, o_dt:], w[:, o_dt:], jnp.zeros((dm, LANES - 64), F32)],
                           axis=1).astype(BF16)
    dtb = jnp.concatenate([s_dt_bias[0].reshape(1, 32), s_dt_bias[0].reshape(1, 32),
                           jnp.zeros((1, LANES - 64), F32)], axis=1)
    aneg = jnp.concatenate([jnp.zeros((1, 32), F32), -jnp.exp(s_A_log[0].reshape(1, 32)),
                            jnp.zeros((1, LANES - 64), F32)], axis=1)
    gq, gk, gv, grs, zs, sx, sb, sc, ga, dts, dts_t = _in_cd(x2, modl, modc, g[0:1], w_in, s_conv_w[0],
                                                             s_conv_b[0][None], dtb, aneg, bsz, t, ctx_len)
    aw = jnp.zeros((2, LANES, G_HEADS * G_DK), F32)
    aw = aw.at[0, 0:G_RANK].set(g_alpha_w[0, 0]).at[1, G_RANK:2 * G_RANK].set(g_alpha_w[0, 1])
    awh = aw.astype(BF16)
    awl = (aw - awh.astype(F32)).astype(BF16)
    og_f, og_b = _gla(gq, gk, gv, ga, awh, awl, g_alpha_b[0][:, None], bsz, t, ctx_len)
    ys_f, ys_b = _ssd(sx, sb, sc, dts, dts_t, bsz, t, ctx_len)
    dskip = jnp.repeat(s_D[0], S_P)[None]
    x2 = _out_proj(_out_cd_kernel, "out_proj_cd", [og_f, og_b, grs, ys_f, ys_b, sx, zs],
                   [g_norm_w[0][None], dskip, s_norm_w[0][None]],
                   x2, modl, modc, g[1:2], cd_out_w[0].astype(BF16), bsz, t, ctx_len, latent_only=True)
    t, ctx_len = seq, 0

    rw = jnp.concatenate([router_w[0], jnp.zeros((dm, LANES - N_EXPERTS), F32)], axis=1)
    rwh = rw.astype(BF16)
    rw = jnp.concatenate([rwh, (rw - rwh.astype(F32)).astype(BF16)], axis=1)
    rb = jnp.concatenate([router_b[0][None], jnp.zeros((1, LANES - N_EXPERTS), F32)], axis=1)
    h2, top_idx, top_gate, counts = _router(x2, modl, g[2:3], rw, rb, bsz, t, ctx_len)
    slot, block_e, n_used, n_valid, n_blocks = _routing_tables(top_idx[:, 0:TOP_K], top_idx[:, TOP_K:2 * TOP_K],
                                                               counts[0, :N_EXPERTS].astype(I32))
    xs = _dispatch_rows(h2, slot.reshape(-1), n_blocks * MOE_ROWS)
    ys_e = _moe_experts(xs, moe_w1[0], moe_w2[0], block_e, n_used, n_valid, n_blocks)
    yk = _gather_rows(ys_e, jnp.concatenate([slot[:, 0], slot[:, 1]]))
    out = _combine(yk, top_gate, x2, modl, g[3:4], bsz, t, ctx_len)
    return out.reshape(bsz, seq, dm)
```

```python
import functools

import jax
import jax.numpy as jnp
from jax import lax
from jax.experimental import pallas as pl
from jax.experimental.pallas import tpu as pltpu
from jax.experimental.pallas import tpu_sc as plsc

F32, BF16, I32 = jnp.float32, jnp.bfloat16, jnp.int32

D_MODEL = 1024
GRID_W = 64
EPS = 1e-6
LOG2E = 1.4426950408889634
M_HEADS, M_DK, M_DV = 4, 128, 256
LRU_BLOCKS, LRU_BW, LRU_C = 8, 128, 8.0
G_HEADS, G_DK, G_DV, G_RANK, G_TAU = 4, 128, 256, 16, 16.0
S_HEADS, S_P, S_N, S_GROUPS, S_HPG = 16, 64, 128, 2, 8
N_EXPERTS, TOP_K, FF_EXPERT = 8, 2, 3584

LANES, SUBLANES = 128, 8
VMEM_LIMIT = 56 * 1024 * 1024
ROW_TILE = 256
IN_PARTS = 2
LATENT_TILE = 512
BIG_ROW_TILE_MAX = 640
PROJ_COLS = 256
M_CHUNK = 256
G_CHUNK, G_SUB = 256, 16
G_GROUP = 64
S_CHUNK = 256
LRU_TILE = 256
FF_STEP = 256
MOE_ROWS = 1024
MOE_FF = 512
MOE_ROW_STEP = 128
SC_CORES, SC_SUBCORES = 2, 16
SC_LANES = 16
SC_GATHER_ROWS = 128


def _cparams(*sem):
    return pltpu.CompilerParams(dimension_semantics=sem, vmem_limit_bytes=VMEM_LIMIT)


def _sigmoid(x):
    return 0.5 + 0.5 * jnp.tanh(0.5 * x)


def _silu(x):
    return x * _sigmoid(x)


def _softplus(x):
    return jnp.maximum(x, 0.0) + jnp.log1p(jnp.exp(-jnp.abs(x)))


def _log_sigmoid(x):
    return jnp.minimum(x, 0.0) - jnp.log1p(jnp.exp(-jnp.abs(x)))


def _dot(a, b):
    return jnp.dot(a, b, preferred_element_type=F32)


def _dot_nt(a, b):
    return lax.dot_general(a, b, (((1,), (1,)), ((), ())), preferred_element_type=F32)


def _dot_tn(a, b):
    return lax.dot_general(a, b, (((0,), (0,)), ((), ())), preferred_element_type=F32)


def _split3(f):
    f1 = f.astype(BF16)
    r = f - f1.astype(F32)
    f2 = r.astype(BF16)
    f3 = (r - f2.astype(F32)).astype(BF16)
    return f1, f2, f3


def _sel_cols(mask01, f):
    p1, p2, p3 = _split3(f)
    return _dot(mask01, p1) + _dot(mask01, p2) + _dot(mask01, p3)


def _sel_rows(f, mask01):
    p1, p2, p3 = _split3(f)
    return _dot_nt(p1, mask01) + _dot_nt(p2, mask01) + _dot_nt(p3, mask01)


def _pack_rows(u):
    n = u.shape[1] // 2
    bits = lax.bitcast_convert_type(u.astype(BF16).astype(F32), jnp.uint32)
    return jnp.bitwise_or(bits[:, :n], jnp.right_shift(bits[:, n:], jnp.uint32(16)))


def _unpack_rows(p):
    hi = lax.bitcast_convert_type(jnp.bitwise_and(p, jnp.uint32(0xFFFF0000)), F32)
    lo = lax.bitcast_convert_type(jnp.left_shift(p, jnp.uint32(16)), F32)
    return jnp.concatenate([hi, lo], axis=1)


def _rms(u, g):
    return u * lax.rsqrt(jnp.mean(u * u, axis=-1, keepdims=True) + EPS) * g


def _head_rms(u, g, n_heads):
    w = u.shape[-1] // n_heads
    return jnp.concatenate([_rms(u[:, h * w:(h + 1) * w], g[:, h * w:(h + 1) * w]) for h in range(n_heads)], axis=1)


def _row_ids(tile_idx, tm):
    return tile_idx * tm + lax.broadcasted_iota(I32, (tm, 1), 0)


def _mod(modl_ref, modc_ref, idx, is_ctx):
    return jnp.where(is_ctx, modc_ref[idx:idx + 1, :], modl_ref[0, idx:idx + 1, :])


def _adaln(x, g, modl_ref, modc_ref, shift_idx, is_ctx):
    shift = _mod(modl_ref, modc_ref, shift_idx, is_ctx)
    scale = _mod(modl_ref, modc_ref, shift_idx + 1, is_ctx)
    return _rms(x, g) * (1.0 + scale) + shift


def _neighbour_tiles(r, step, is_lat):
    n_tiles, per_seg = r.shape[0], GRID_W // SUBLANES
    zero = jnp.zeros((1,) + r.shape[1:], r.dtype)

    def across(src):
        return jnp.where(is_lat, 0.0, r[src:src + 1]) if 0 <= src < n_tiles else zero

    pieces = []
    for s0 in range(0, n_tiles, per_seg):
        if step > 0:
            pieces += [across(s0 - 1), r[s0:s0 + per_seg - 1]]
        else:
            pieces += [r[s0 + 1:s0 + per_seg], across(s0 + per_seg)]
    return jnp.concatenate(pieces, axis=0)


def _dwconv(y, cw, cb, is_lat):
    tm, n = y.shape
    y3 = y.reshape(tm // SUBLANES, SUBLANES, n)
    sub = lax.broadcasted_iota(I32, (1, SUBLANES, 1), 1)
    w = lambda j: cw[j:j + 1, :].reshape(1, 1, n)
    r1, r2, r7 = pltpu.roll(y3, 1, 1), pltpu.roll(y3, 2, 1), pltpu.roll(y3, SUBLANES - 1, 1)
    ym1 = jnp.where(sub >= 1, r1, _neighbour_tiles(r1, 1, is_lat))
    ym2 = jnp.where(sub >= 2, r2, _neighbour_tiles(r2, 1, is_lat))
    yp1 = jnp.where(sub <= SUBLANES - 2, r7, _neighbour_tiles(r7, -1, is_lat))
    out = cb.reshape(1, 1, n) + w(0) * ym2 + w(1) * ym1 + w(2) * y3 + w(3) * yp1
    return out.reshape(tm, n)


def _resident(shape):
    nd = len(shape)
    return pl.BlockSpec(shape, lambda *_: (0,) * nd, pipeline_mode=pl.Buffered(1))


def _big_row_tile(t):
    return max(tm for tm in range(8, BIG_ROW_TILE_MAX + 1, 8) if t % tm == 0)


def _mod_kernel(c_ref, w_ref, b_ref, o_ref):
    a1, a2, _ = _split3(_silu(c_ref[...]))
    w = w_ref[0]
    wh = w.astype(BF16)
    wl = (w - wh.astype(F32)).astype(BF16)
    o_ref[0] = _dot(a1, wh) + _dot(a2, wh) + _dot(a1, wl) + b_ref[0]


def _modulation(c_all, mod_w, mod_b):
    depth, d, n6 = mod_w.shape
    rows = c_all.shape[0]
    tn = 1024
    return pl.pallas_call(
        _mod_kernel,
        grid=(depth, n6 // tn),
        in_specs=[pl.BlockSpec((rows, d), lambda l, j: (0, 0)),
                  pl.BlockSpec((1, d, tn), lambda l, j: (l, 0, j)),
                  pl.BlockSpec((1, 1, tn), lambda l, j: (l, 0, j))],
        out_specs=pl.BlockSpec((1, rows, tn), lambda l, j: (l, 0, j)),
        out_shape=jax.ShapeDtypeStruct((depth, rows, n6), F32),
        compiler_params=_cparams("arbitrary", "arbitrary"),
        name="modulation",
    )(c_all, mod_w, mod_b.reshape(depth, 1, n6))


def _project(h_ref, w_ref, w_col, out_ref, epilogue):
    for c in range(0, out_ref.shape[1], PROJ_COLS):
        y = _dot(h_ref[...], w_ref[:, w_col + c:w_col + c + PROJ_COLS])
        out_ref[:, c:c + PROJ_COLS] = epilogue(y, c).astype(out_ref.dtype)


def _part_is_latent(part, parts_per_batch):
    return lax.rem(pl.program_id(0) * IN_PARTS + part, parts_per_batch) != 0


def _conv_parts(y, cw, cb, lat):
    return jnp.concatenate([_dwconv(y[p * ROW_TILE:(p + 1) * ROW_TILE], cw, cb, lat[p]) for p in range(IN_PARTS)],
                           axis=0)


def _in_ab_kernel(parts_per_batch, *refs):
    xs, refs = refs[:3 * IN_PARTS], refs[3 * IN_PARTS:]
    modc_ref, g_ref, w_ref, mcw_ref, mcb_ref, lcw_ref, lcb_ref, gb_ref = refs[:8]
    x_ref, q_ref, k_ref, v_ref, og_ref, lx_ref, glg_ref, gates_ref, gates_t_ref, h_scr = refs[8:]
    lat = [_part_is_latent(p, parts_per_batch) for p in range(IN_PARTS)]
    for p in range(IN_PARTS):
        xc_ref, xl_ref, modl_ref = xs[3 * p:3 * p + 3]
        rows = slice(p * ROW_TILE, (p + 1) * ROW_TILE)
        xp = jnp.where(lat[p], xl_ref[...], xc_ref[...])
        x_ref[rows, :] = xp
        h_scr[rows, :] = _adaln(xp, g_ref[...], modl_ref, modc_ref, 0, jnp.logical_not(lat[p])).astype(BF16)

    def conv(cw_ref, cb_ref):
        return lambda y, c: _conv_parts(y, cw_ref[:, c:c + PROJ_COLS], cb_ref[:, c:c + PROJ_COLS], lat)

    qc, kc, lc = conv(mcw_ref, mcb_ref), conv(mcw_ref, mcb_ref), conv(lcw_ref, lcb_ref)
    _project(h_scr, w_ref, 0, q_ref, lambda y, c: _silu(qc(y, c)))
    _project(h_scr, w_ref, 512, k_ref, lambda y, c: _silu(kc(y, 512 + c)) * (M_DK ** -0.5))
    _project(h_scr, w_ref, 1024, v_ref, lambda y, c: y)
    _project(h_scr, w_ref, 2048, og_ref, lambda y, c: _sigmoid(y))
    _project(h_scr, w_ref, 3072, lx_ref, lc)
    _project(h_scr, w_ref, 4096, glg_ref, lambda y, c: jax.nn.gelu(y))
    gt = _dot(h_scr[...], w_ref[:, 5120:5248]) + gb_ref[...]
    lane = lax.broadcasted_iota(I32, gt.shape, 1)
    gates = jnp.where(lane >= 2 * M_HEADS, _log_sigmoid(gt), gt)
    gates_ref[...] = gates
    gates_t_ref[...] = gates.T


def _in_ab(xc2, xl2, modl, modc, g, w, mcw, mcb, lcw, lcb, gb, bsz, t, ctx_len):
    n = bsz * t
    tm = IN_PARTS * ROW_TILE
    ppb = t // ROW_TILE
    lpb = ppb - 1
    part = lambda j, p: j * IN_PARTS + p
    batch = lambda j, p: part(j, p) // ppb
    x_specs, x_args = [], []
    for p in range(IN_PARTS):
        x_specs += [pl.BlockSpec((ROW_TILE, D_MODEL), lambda j, p=p: (batch(j, p), 0)),
                    pl.BlockSpec((ROW_TILE, D_MODEL),
                                 lambda j, p=p: (batch(j, p) * lpb + jnp.maximum(part(j, p) % ppb - 1, 0), 0)),
                    pl.BlockSpec((1, 8, D_MODEL), lambda j, p=p: (batch(j, p), 0, 0))]
        x_args += [xc2, xl2, modl]
    row = lambda c: pl.BlockSpec((tm, c), lambda j: (j, 0))
    outs = [(D_MODEL, F32), (512, BF16), (512, BF16), (1024, BF16), (1024, BF16), (1024, BF16), (1024, BF16),
            (LANES, F32)]
    params = (modc, g, w, mcw, mcb, lcw, lcb, gb)
    return pl.pallas_call(
        functools.partial(_in_ab_kernel, ppb),
        grid=(n // tm,),
        in_specs=x_specs + [_resident(a.shape) for a in params],
        out_specs=[row(c) for c, _ in outs] + [pl.BlockSpec((LANES, tm), lambda j: (0, j))],
        out_shape=[jax.ShapeDtypeStruct((n, c), dt) for c, dt in outs] + [jax.ShapeDtypeStruct((LANES, n), F32)],
        scratch_shapes=[pltpu.VMEM((tm, D_MODEL), BF16)],
        compiler_params=_cparams("arbitrary"),
        name="in_proj_ab",
    )(*x_args, *params)


def _reverse_order(n_ctx_chunks, n_chunks):
    return lambda c: jnp.where(c < n_ctx_chunks, n_ctx_chunks - 1 - c, n_chunks - 1 - (c - n_ctx_chunks))


def _tri_mask(L, reverse):
    row = lax.broadcasted_iota(I32, (L, L), 0)
    col = lax.broadcasted_iota(I32, (L, L), 1)
    return (col >= row) if reverse else (col <= row)


def _bidir_scan(kern, L, row_ins, lane_ins, res_ins, out_cols, scratch, bsz, t, ctx_len, name):
    nc = t // L
    rev = _reverse_order(ctx_len // L, nc)
    fwd = lambda i: i
    n = bsz * t

    def row(c, order):
        return pl.BlockSpec((L, c), lambda b, i: (b * nc + order(i), 0))

    def lanes(r, order):
        return pl.BlockSpec((r, L), lambda b, i: (0, b * nc + order(i)))

    in_specs, args = [], []
    for order in (fwd, rev):
        in_specs += [row(a.shape[1], order) for a in row_ins] + [lanes(r, order) for _, r in lane_ins]
        args += list(row_ins) + [a for a, _ in lane_ins]
    in_specs += [_resident(a.shape) for a in res_ins]
    args += list(res_ins)
    return pl.pallas_call(
        kern,
        grid=(bsz, nc),
        in_specs=in_specs,
        out_specs=[row(c, fwd) for c in out_cols] + [row(c, rev) for c in out_cols],
        out_shape=[jax.ShapeDtypeStruct((n, c), BF16) for c in out_cols] * 2,
        scratch_shapes=scratch,
        compiler_params=_cparams("arbitrary", "arbitrary"),
        name=name,
    )(*args)


def _zero_at_start(*scratch):
    @pl.when(pl.program_id(1) == 0)
    def _():
        for s in scratch:
            s[...] = jnp.zeros_like(s)


def _mlstm_kernel(L, qf, kf, vf, gcf, grf, qb, kb, vb, gcb, grb, of, ob, c_scr, m_scr):
    _zero_at_start(c_scr, m_scr)
    ones = jnp.ones((L, LANES), BF16)
    wide = lambda c, k: jnp.concatenate([c] * k, axis=1)
    er = lax.broadcasted_iota(I32, (LANES, 2 * M_HEADS * LANES), 0)
    ej = jnp.right_shift(lax.broadcasted_iota(I32, (LANES, 2 * M_HEADS * LANES), 1), LANES.bit_length() - 1)
    chains = []
    for d, (q_ref, k_ref, v_ref, gc_ref, gr_ref, out_ref) in enumerate(((qf, kf, vf, gcf, grf, of),
                                                                         (qb, kb, vb, gcb, grb, ob))):
        reverse = d == 1
        mask = _tri_mask(L, reverse)
        mask01 = mask.astype(BF16)
        gc = gc_ref[...]
        gr = gr_ref[...]
        bcol = _sel_cols(mask01, gc)
        brow = _sel_rows(gr, mask01)
        glane = lax.broadcasted_iota(I32, (L, LANES), 1)
        src = jnp.where(ej < M_HEADS, M_HEADS * d + ej, M_HEADS + M_HEADS * d + ej)
        s1, s2, _ = _split3(jnp.where(glane < 2 * M_HEADS, gc, bcol))
        sel = (er == src).astype(BF16)
        rep = _dot(s1, sel) + _dot(s2, sel)
        last = 0 if reverse else L - 1
        for h in range(M_HEADS):
            fo, io = 2 * M_HEADS + M_HEADS * d + h, M_HEADS * d + h
            ic, bc = rep[:, h * LANES:(h + 1) * LANES], rep[:, (M_HEADS + h) * LANES:(M_HEADS + h + 1) * LANES]
            r1 = gr[io:io + 1, :] - brow[fo:fo + 1, :]
            m = m_scr[d, h:h + 1, :]
            g = bc + m
            mx = jnp.max(jnp.where(mask, r1, -jnp.inf), axis=1, keepdims=True)
            mt = jnp.maximum(g, bc + mx)
            p = jnp.exp(jnp.where(mask, wide(bc - mt, L // LANES) + r1, -jnp.inf))
            bl = bc[last:last + 1, :]
            a_s = bl - bc + ic
            m_new = jnp.maximum(bl + m, jnp.max(a_s, axis=0, keepdims=True))
            chains.append(dict(
                d=d, h=h, out_ref=out_ref, mt=mt, w_inter=jnp.exp(g - mt), p=p,
                m_new=m_new, decay=jnp.exp(bl + m - m_new), ws=jnp.exp(a_s - m_new),
                qh=q_ref[:, h * M_DK:(h + 1) * M_DK], kh=k_ref[:, h * M_DK:(h + 1) * M_DK],
                vx=jnp.concatenate([v_ref[:, h * M_DV:(h + 1) * M_DV], ones], axis=1)))
    for ch in chains:
        ch["sc"] = (_dot_nt(ch["qh"], ch["kh"]) * ch["p"]).astype(BF16)
    for ch in chains:
        cx = c_scr[ch["d"], ch["h"]]
        numx = wide(ch["w_inter"], 1 + M_DV // LANES) * _dot(ch["qh"], cx.astype(BF16)) + _dot(ch["sc"], ch["vx"])
        den = numx[:, M_DV:]
        rinv = 1.0 / jnp.maximum(jnp.abs(den), jnp.exp(-ch["mt"]))
        hh = numx[:, 0:M_DV] * wide(rinv, M_DV // LANES)
        ch["out_ref"][:, ch["h"] * M_DV:(ch["h"] + 1) * M_DV] = hh.astype(BF16)
    for ch in chains:
        d, h = ch["d"], ch["h"]
        kw = (ch["kh"].astype(F32) * ch["ws"]).astype(BF16)
        c_scr[d, h] = wide(ch["decay"], 1 + M_DV // LANES) * c_scr[d, h] + _dot_tn(kw, ch["vx"])
        m_scr[d, h:h + 1, :] = ch["m_new"]


def _mlstm(q, k, v, gcol, grow, bsz, t, ctx_len):
    L = M_CHUNK
    scratch = [pltpu.VMEM((2, M_HEADS, M_DK, M_DV + LANES), F32), pltpu.VMEM((2, 8, LANES), F32)]
    return _bidir_scan(functools.partial(_mlstm_kernel, L), L, [q, k, v, gcol], [(grow, 16)], [], [1024], scratch,
                       bsz, t, ctx_len, "mlstm_scan")


def _lru_kernel(L, uf, ub_, wa_ref, wx_ref, ba_ref, bx_ref, lam_ref, of, ob, a_scr, b_scr, h_scr):
    _zero_at_start(h_scr)
    R, C = L // SUBLANES, LRU_BLOCKS * LRU_BW
    i0, i1 = lax.broadcasted_iota(I32, (L, L), 0), lax.broadcasted_iota(I32, (L, L), 1)
    when = lambda p: jnp.bitwise_and(p, SUBLANES - 1) * R + jnp.right_shift(p, SUBLANES.bit_length() - 1)
    regroup = (i1 == when(i0)).astype(BF16)
    ungroup = (i0 == when(i1)).astype(BF16)
    sub = lax.broadcasted_iota(I32, (SUBLANES, 1), 0)
    dirs = ((uf, of), (ub_, ob))
    for d, (u_ref, _) in enumerate(dirs):
        u = _dot(regroup, u_ref[...])
        ub = u.astype(BF16)
        sp = -LRU_C * _softplus(-lam_ref[d])
        for n in range(LRU_BLOCKS):
            sl = slice(n * LRU_BW, (n + 1) * LRU_BW)
            rg = _sigmoid(_dot(ub[:, sl], wa_ref[d, n]) + ba_ref[d, :, sl])
            ig = _sigmoid(_dot(ub[:, sl], wx_ref[d, n]) + bx_ref[d, :, sl])
            log_a = rg * sp[:, sl]
            a = jnp.exp(log_a)
            a_scr[d, :, sl] = a
            b_scr[d, :, sl] = jnp.sqrt(-jnp.tanh(log_a) * (a * a + 1.0)) * (ig * u[:, sl])
    S = SUBLANES
    hs = [jnp.zeros((S, C), F32), jnp.zeros((S, C), F32)]
    ps = [jnp.ones((S, C), F32), jnp.ones((S, C), F32)]
    for j in range(R):
        for d in range(2):
            rows = slice(S * j, S * j + S) if d == 0 else slice(S * (R - 1 - j), S * (R - j))
            a = a_scr[d, rows, :]
            hs[d] = a * hs[d] + b_scr[d, rows, :]
            ps[d] = a * ps[d]
            b_scr[d, rows, :] = hs[d]
            a_scr[d, rows, :] = ps[d]
    for d, (_, out_ref) in enumerate(dirs):
        c = h_scr[d:d + 1, :]
        cin = jnp.zeros((S, C), F32)
        for s in (range(S) if d == 0 else reversed(range(S))):
            cin = jnp.where(sub == s, c, cin)
            c = ps[d][s:s + 1, :] * c + hs[d][s:s + 1, :]
        h_scr[d:d + 1, :] = c
        h = b_scr[d].reshape(R, S, C) + a_scr[d].reshape(R, S, C) * cin[None]
        out_ref[...] = _dot(ungroup, h.reshape(L, C).astype(BF16)).astype(BF16)


def _lru(u, wa, wx, ba, bx, lam, bsz, t, ctx_len):
    L = LRU_TILE
    scratch = [pltpu.VMEM((2, L, 1024), F32), pltpu.VMEM((2, L, 1024), F32), pltpu.VMEM((8, 1024), F32)]
    return _bidir_scan(functools.partial(_lru_kernel, L), L, [u], [], [wa, wx, ba, bx, lam], [1024], scratch,
                       bsz, t, ctx_len, "rglru_scan")


def _swiglu_rows(x, is_ctx, modl_ref, modc_ref, g2_ref, g3_ref, w1_ref, w2_ref):
    h = _adaln(x, g2_ref[...], modl_ref, modc_ref, 3, is_ctx).astype(BF16)
    ff = w2_ref.shape[0]
    acc = jnp.zeros(x.shape, F32)
    for j in range(ff // FF_STEP):
        lo = j * FF_STEP
        gj = _dot(h, w1_ref[:, lo:lo + FF_STEP])
        uj = _dot(h, w1_ref[:, ff + lo:ff + lo + FF_STEP])
        acc = acc + _dot((_silu(gj) * uj).astype(BF16), w2_ref[lo:lo + FF_STEP, :])
    return x + _mod(modl_ref, modc_ref, 5, is_ctx) * _rms(acc, g3_ref[...])


def _out_tail(tm, ctx_len, a1, a2, x_ref, modl_ref, modc_ref, g_ref, w_ref, xo_ref, ffn_refs=None):
    is_ctx = False if ctx_len is None else _row_ids(pl.program_id(1), tm) < ctx_len
    half = w_ref.shape[0] // 2
    y = _dot(a1.astype(BF16), w_ref[0:half, :]) + _dot(a2.astype(BF16), w_ref[half:, :])
    x = x_ref[...] + _mod(modl_ref, modc_ref, 2, is_ctx) * _rms(y, g_ref[...])
    if ffn_refs is not None:
        x = _swiglu_rows(x, is_ctx, modl_ref, modc_ref, *ffn_refs)
    xo_ref[...] = x


def _out_ab_kernel(tm, ctx_len, hf, hb, og, lf, lb, glg, nw, x_ref, modl_ref, modc_ref, g_ref, w_ref,
                   g2_ref, g3_ref, w1_ref, w2_ref, xo_ref):
    hm = _head_rms(hf[...].astype(F32) + hb[...].astype(F32), nw[...], M_HEADS) * og[...].astype(F32)
    hl = (lf[...].astype(F32) + lb[...].astype(F32)) * glg[...].astype(F32)
    _out_tail(tm, ctx_len, hm, hl, x_ref, modl_ref, modc_ref, g_ref, w_ref, xo_ref,
              (g2_ref, g3_ref, w1_ref, w2_ref))


def _out_cd_kernel(tm, ctx_len, gf, gb, grs, yf, yb, sx, zs, gnw, dsk, snw, x_ref, modl_ref, modc_ref, g_ref,
                   w_ref, xo_ref):
    og = _head_rms(gf[...].astype(F32) + gb[...].astype(F32), gnw[...], G_HEADS) * grs[...].astype(F32)
    ys = yf[...].astype(F32) + yb[...].astype(F32) + dsk[...] * sx[...].astype(F32)
    ys = _rms(ys * zs[...].astype(F32), snw[...])
    _out_tail(tm, ctx_len, og, ys, x_ref, modl_ref, modc_ref, g_ref, w_ref, xo_ref)


def _out_proj(kern, name, acts, vecs, x2, modl, modc, g, w, bsz, t, ctx_len, extra=(), latent_only=False):
    if latent_only:
        tm = LATENT_TILE
        nt = (t - ctx_len) // tm
        row = lambda c: _latent_rows(tm, t, ctx_len, c)
        out_spec = pl.BlockSpec((tm, D_MODEL), lambda b, i: (b * nt + i, 0))
        out_rows = bsz * (t - ctx_len)
    else:
        tm = _big_row_tile(t)
        nt = t // tm
        row = lambda c: pl.BlockSpec((tm, c), lambda b, i: (b * nt + i, 0))
        out_spec, out_rows = row(D_MODEL), bsz * t
    return pl.pallas_call(
        functools.partial(kern, tm, None if latent_only else ctx_len),
        grid=(bsz, nt),
        in_specs=[row(1024)] * len(acts) + [_resident((1, 1024))] * len(vecs)
        + [row(D_MODEL), pl.BlockSpec((1, 8, D_MODEL), lambda b, i: (b, 0, 0)),
           _resident((8, D_MODEL)), _resident((1, D_MODEL)), _resident(w.shape)]
        + [_resident(a.shape) for a in extra],
        out_specs=out_spec,
        out_shape=jax.ShapeDtypeStruct((out_rows, D_MODEL), F32),
        compiler_params=_cparams("arbitrary", "arbitrary"),
        name=name,
    )(*acts, *vecs, x2, modl, modc, g, w, *extra)


def _in_cd_kernel(parts_per_batch, x_ref, *refs):
    modls, refs = refs[:IN_PARTS], refs[IN_PARTS:]
    modc_ref, g_ref, w_ref, cw_ref, cb_ref, dtb_ref, aneg_ref = refs[:7]
    gq_ref, gk_ref, gv_ref, gr_ref, z_ref, sx_ref, sb_ref, sc_ref, ga_ref, dts_ref, dts_t_ref, h_scr = refs[7:]
    lat = [_part_is_latent(p, parts_per_batch) for p in range(IN_PARTS)]
    for p in range(IN_PARTS):
        rows = slice(p * ROW_TILE, (p + 1) * ROW_TILE)
        h_scr[rows, :] = _adaln(x_ref[rows, :], g_ref[...], modls[p], modc_ref, 0,
                                jnp.logical_not(lat[p])).astype(BF16)

    def conv_silu(col0):
        return lambda y, c: _silu(_conv_parts(y, cw_ref[:, col0 + c:col0 + c + PROJ_COLS],
                                              cb_ref[:, col0 + c:col0 + c + PROJ_COLS], lat))

    _project(h_scr, w_ref, 0, gq_ref, lambda y, c: y * (G_DK ** -0.5))
    _project(h_scr, w_ref, 512, gk_ref, lambda y, c: y)
    _project(h_scr, w_ref, 1024, gv_ref, lambda y, c: y)
    _project(h_scr, w_ref, 2048, gr_ref, lambda y, c: _silu(y))
    _project(h_scr, w_ref, 3072, z_ref, lambda y, c: _silu(y))
    _project(h_scr, w_ref, 4096, sx_ref, conv_silu(0))
    _project(h_scr, w_ref, 5120, sb_ref, conv_silu(1024))
    _project(h_scr, w_ref, 5376, sc_ref, conv_silu(1280))
    ga_ref[...] = _dot(h_scr[...], w_ref[:, 5632:5760])
    dt = _softplus(_dot(h_scr[...], w_ref[:, 5760:5888]) + dtb_ref[...])
    lane = lax.broadcasted_iota(I32, dt.shape, 1)
    dts = jnp.where(lane < 2 * S_HEADS, dt, dt * aneg_ref[...])
    dts_ref[...] = dts
    dts_t_ref[...] = dts.T


def _in_cd(x2, modl, modc, g, w, cw, cb, dtb, aneg, bsz, t, ctx_len):
    n = bsz * t
    tm = IN_PARTS * ROW_TILE
    ppb = t // ROW_TILE
    row = lambda c: pl.BlockSpec((tm, c), lambda j: (j, 0))
    mod_specs = [pl.BlockSpec((1, 8, D_MODEL), lambda j, p=p: ((j * IN_PARTS + p) // ppb, 0, 0))
                 for p in range(IN_PARTS)]
    outs = [(512, BF16), (512, BF16), (1024, BF16), (1024, BF16), (1024, BF16), (1024, BF16), (256, BF16),
            (256, BF16), (LANES, F32), (LANES, F32)]
    params = (modc, g, w, cw, cb, dtb, aneg)
    return pl.pallas_call(
        functools.partial(_in_cd_kernel, ppb),
        grid=(n // tm,),
        in_specs=[row(D_MODEL)] + mod_specs + [_resident(a.shape) for a in params],
        out_specs=[row(c) for c, _ in outs] + [pl.BlockSpec((LANES, tm), lambda j: (0, j))],
        out_shape=[jax.ShapeDtypeStruct((n, c), dt) for c, dt in outs] + [jax.ShapeDtypeStruct((LANES, n), F32)],
        scratch_shapes=[pltpu.VMEM((tm, D_MODEL), BF16)],
        compiler_params=_cparams("arbitrary"),
        name="in_proj_cd",
    )(x2, *([modl] * IN_PARTS), *params)


def _gla_kernel(L, C, qf, kf, vf, gaf, qb, kb, vb, gab, awh_ref, awl_ref, ab_ref, of, ob, s_scr, att_scr):
    _zero_at_start(s_scr, att_scr)
    nb = L // C
    chains = []
    for d, (q_ref, k_ref, v_ref, ga_ref, out_ref) in enumerate(((qf, kf, vf, gaf, of), (qb, kb, vb, gab, ob))):
        reverse = d == 1
        mask = _tri_mask(L, reverse)
        g1, g2, _ = _split3(ga_ref[...])
        pre = _dot(g1, awh_ref[d]) + _dot(g2, awh_ref[d]) + _dot(g1, awl_ref[d])
        lg = _log_sigmoid(pre + ab_ref[d]) * (1.0 / G_TAU)
        ball = _sel_cols(mask.astype(BF16), lg)
        for h in range(G_HEADS):
            ks = slice(h * G_DK, (h + 1) * G_DK)
            b = ball[:, ks] * LOG2E
            qh = q_ref[:, ks].astype(F32)
            inter = _dot_nt((qh * jnp.exp2(b)).astype(BF16), s_scr[d, h].astype(BF16))
            chains.append(dict(d=d, h=h, reverse=reverse, mask=mask, b=b, qh=qh, kh=k_ref[:, ks].astype(F32),
                               vh=v_ref[:, h * G_DV:(h + 1) * G_DV], inter=inter, out_ref=out_ref, blocks=[]))
    T = min(G_GROUP, L)
    for i in range(nb):
        lo, hi = i * C, (i + 1) * C
        t0 = lo // T * T
        for c, ch in enumerate(chains):
            b = ch["b"]
            if ch["reverse"]:
                bref = b[hi:hi + 1, :] if i < nb - 1 else jnp.zeros((1, G_DK), F32)
            else:
                bref = b[lo - 1:lo, :] if i > 0 else jnp.zeros((1, G_DK), F32)
            qi = (ch["qh"][lo:hi, :] * jnp.exp2(b[lo:hi, :] - bref)).astype(BF16)
            ki = (ch["kh"][t0:t0 + T] * jnp.exp2(bref - b[t0:t0 + T])).astype(BF16)
            att_scr[c, lo:hi, t0:t0 + T] = _dot_nt(qi, ki)
    for g in range(L // T):
        t0, t1 = g * T, (g + 1) * T
        for c, ch in enumerate(chains):
            b = ch["b"]
            e0, e1 = (t1, L) if ch["reverse"] else (0, t0)
            if e1 > e0:
                bref = b[t1:t1 + 1, :] if ch["reverse"] else b[t0 - 1:t0, :]
                qg = (ch["qh"][t0:t1] * jnp.exp2(b[t0:t1] - bref)).astype(BF16)
                kg = (ch["kh"][e0:e1] * jnp.exp2(bref - b[e0:e1])).astype(BF16)
                att_scr[c, t0:t1, e0:e1] = _dot_nt(qg, kg)
    for c, ch in enumerate(chains):
        att = jnp.where(ch["mask"], att_scr[c], 0.0).astype(BF16)
        oh = ch["inter"] + _dot(att, ch["vh"])
        ch["out_ref"][:, ch["h"] * G_DV:(ch["h"] + 1) * G_DV] = oh.astype(BF16)
    for ch in chains:
        d, h, b = ch["d"], ch["h"], ch["b"]
        last = 0 if ch["reverse"] else L - 1
        bl = b[last:last + 1, :]
        kd = (ch["kh"] * jnp.exp2(bl - b)).astype(BF16)
        s_scr[d, h] = s_scr[d, h] * jnp.exp2(bl) + _dot_tn(ch["vh"], kd)


def _gla(q, k, v, ga, awh, awl, ab, bsz, t, ctx_len):
    L, C = G_CHUNK, G_SUB
    scratch = [pltpu.VMEM((2, G_HEADS, G_DV, G_DK), F32), pltpu.VMEM((2 * G_HEADS, L, L), F32)]
    return _bidir_scan(functools.partial(_gla_kernel, L, C), L, [q, k, v, ga], [], [awh, awl, ab], [1024], scratch,
                       bsz, t, ctx_len, "gla_scan")


def _ssd_kernel(L, xf, bf, cf, gcf, grf, xb, bb, cb_, gcb, grb, of, ob, s_scr):
    _zero_at_start(s_scr)
    er = lax.broadcasted_iota(I32, (LANES, S_HEADS * S_P), 0)
    ec = jnp.right_shift(lax.broadcasted_iota(I32, (LANES, S_HEADS * S_P), 1), S_P.bit_length() - 1)
    glane = lax.broadcasted_iota(I32, (L, LANES), 1)
    lane = lax.broadcasted_iota(I32, (L, 2 * S_P), 1)
    dirs, chains = [], []
    for d, (x_ref, b_ref, c_ref, gc_ref, gr_ref, out_ref) in enumerate(((xf, bf, cf, gcf, grf, of),
                                                                         (xb, bb, cb_, gcb, grb, ob))):
        reverse = d == 1
        mask = _tri_mask(L, reverse)
        mask01 = mask.astype(BF16)
        gc = gc_ref[...]
        gr = gr_ref[...]
        bcol = _sel_cols(mask01, gc)
        brow = _sel_rows(gr, mask01)
        last = 0 if reverse else L - 1
        dt_o, la_o = S_HEADS * d, 2 * S_HEADS + S_HEADS * d
        sel_la = (er == ec + la_o).astype(BF16)
        x = x_ref[...]
        bla = jnp.where((glane >= la_o) & (glane < la_o + S_HEADS), bcol, 0.0)
        dt_at_la = pltpu.roll(gc, 2 * S_HEADS, 1)
        bl = bla[last:last + 1, :]
        ebx = _dot(jnp.exp(bla).astype(BF16), sel_la)
        wx = _dot((jnp.exp(bl - bla) * dt_at_la).astype(BF16), sel_la)
        dirs.append(dict(d=d, mask=mask, gr=gr, x=x, ebx=ebx, last=last, dt_o=dt_o, la_o=la_o, out_ref=out_ref,
                         xw=(x.astype(F32) * wx).astype(BF16),
                         bcol2=bcol * LOG2E, brow2=brow * LOG2E))
        for g in range(S_GROUPS):
            ns = slice(g * S_N, (g + 1) * S_N)
            chains.append(dict(dr=dirs[-1], g=g, bg=b_ref[:, ns], cg=c_ref[:, ns],
                               gs=slice(g * S_HPG * S_P, (g + 1) * S_HPG * S_P)))
    for ch in chains:
        dr = ch["dr"]
        ch["cb"] = _dot_nt(ch["cg"], ch["bg"])
        ch["inter"] = _dot(ch["cg"], s_scr[dr["d"], ch["g"]].astype(BF16)) * dr["ebx"][:, ch["gs"]]
    for ch in chains:
        dr, g = ch["dr"], ch["g"]
        parts = []
        for pr in range(S_HPG // 2):
            e0 = g * S_HPG + 2 * pr
            xp = dr["x"][:, e0 * S_P:(e0 + 2) * S_P]
            res = []
            for e in (e0, e0 + 1):
                col = dr["la_o"] + e
                bc, br = dr["bcol2"][:, col:col + 1], dr["brow2"][col:col + 1, :]
                dtr = dr["gr"][dr["dt_o"] + e:dr["dt_o"] + e + 1, :]
                att = (ch["cb"] * dtr * jnp.exp2(jnp.where(dr["mask"], bc - br, -jnp.inf))).astype(BF16)
                res.append(_dot(att, xp))
            parts.append(jnp.where(lane < S_P, res[0], res[1]))
        dr["out_ref"][:, ch["gs"]] = (jnp.concatenate(parts, axis=1) + ch["inter"]).astype(BF16)
    for ch in chains:
        dr, g, gs = ch["dr"], ch["g"], ch["gs"]
        s_scr[dr["d"], g] = (s_scr[dr["d"], g] * dr["ebx"][dr["last"]:dr["last"] + 1, gs]
                             + _dot_tn(ch["bg"], dr["xw"][:, gs]))


def _ssd(x, bm, cm, gcol, grow, bsz, t, ctx_len):
    L = S_CHUNK
    scratch = [pltpu.VMEM((2, S_GROUPS, S_N, S_HPG * S_P), F32)]
    return _bidir_scan(functools.partial(_ssd_kernel, L), L, [x, bm, cm, gcol], [(grow, 4 * S_HEADS)], [], [1024],
                       scratch,
                       bsz, t, ctx_len, "ssd_scan")


def _router_kernel(x_ref, modl_ref, g_ref, rw_ref, rb_ref, h_ref, idx_ref, gate_ref, cnt_ref, base_scr):
    @pl.when((pl.program_id(0) == 0) & (pl.program_id(1) == 0))
    def _():
        base_scr[...] = jnp.zeros_like(base_scr)

    h = _rms(x_ref[...], g_ref[...]) * (1.0 + modl_ref[0, 4:5, :]) + modl_ref[0, 3:4, :]
    h_ref[...] = _pack_rows(h)
    h1, h2, _ = _split3(h)
    both = _dot(h1, rw_ref[...])
    logits = both[:, :LANES] + both[:, LANES:] + _dot(h2, rw_ref[:, :LANES]) + rb_ref[...]
    lane = lax.broadcasted_iota(I32, logits.shape, 1)
    logits = jnp.where(lane < N_EXPERTS, logits, -jnp.inf)
    lanef = lane.astype(F32)
    m1 = jnp.max(logits, axis=1, keepdims=True)
    i1 = jnp.min(jnp.where(logits == m1, lanef, float(LANES)), axis=1, keepdims=True)
    rest = jnp.where(lanef == i1, -jnp.inf, logits)
    m2 = jnp.max(rest, axis=1, keepdims=True)
    i2 = jnp.min(jnp.where(rest == m2, lanef, float(LANES)), axis=1, keepdims=True)
    e = jnp.exp(m2 - m1)
    g1 = 1.0 / (1.0 + e)
    gate_ref[...] = jnp.where(lane == 0, g1, jnp.where(lane == 1, e * g1, 0.0))
    tm = logits.shape[0]
    oh1, oh2 = (lanef == i1).astype(F32), (lanef == i2).astype(F32)
    earlier = (lax.broadcasted_iota(I32, (tm, tm), 1) < lax.broadcasted_iota(I32, (tm, tm), 0)).astype(BF16)
    seen = base_scr[0:1, :] + _dot(earlier, (oh1 + oh2).astype(BF16))
    r1 = jnp.sum(oh1 * seen, axis=1, keepdims=True)
    r2 = jnp.sum(oh2 * seen, axis=1, keepdims=True)
    base_scr[0:1, :] = base_scr[0:1, :] + jnp.sum(oh1 + oh2, axis=0, keepdims=True)
    cnt_ref[...] = base_scr[...]
    idx_ref[...] = jnp.where(lane == 0, i1, jnp.where(lane == 1, i2, jnp.where(lane == 2, r1, jnp.where(
        lane == 3, r2, 0.0)))).astype(I32)


def _latent_rows(tm, t, ctx_len, cols=D_MODEL):
    return pl.BlockSpec((pl.Element(tm), pl.Element(cols)),
                        lambda b, i: (pl.multiple_of(b * t + ctx_len + i * tm, SUBLANES), 0))


def _router(x2, modl, g, rw, rb, bsz, t, ctx_len):
    tm = LATENT_TILE
    nl = bsz * (t - ctx_len)
    nlt = (t - ctx_len) // tm
    orow = lambda c: pl.BlockSpec((tm, c), lambda b, i: (b * nlt + i, 0))
    return pl.pallas_call(
        _router_kernel,
        grid=(bsz, nlt),
        in_specs=[_latent_rows(tm, t, ctx_len),
                  pl.BlockSpec((1, 8, D_MODEL), lambda b, i: (b, 0, 0)),
                  _resident((1, D_MODEL)), _resident(rw.shape), _resident(rb.shape)],
        out_specs=[orow(D_MODEL // 2), orow(LANES), orow(LANES), pl.BlockSpec((8, LANES), lambda b, i: (0, 0))],
        out_shape=[jax.ShapeDtypeStruct((nl, D_MODEL // 2), jnp.uint32), jax.ShapeDtypeStruct((nl, LANES), I32),
                   jax.ShapeDtypeStruct((nl, LANES), F32), jax.ShapeDtypeStruct((8, LANES), F32)],
        scratch_shapes=[pltpu.VMEM((8, LANES), F32)],
        compiler_params=_cparams("arbitrary", "arbitrary"),
        name="moe_router",
    )(x2, modl, g, rw, rb)


def _gather_rows(table, idx):
    n_rows, d = idx.shape[0], table.shape[1]
    n_workers = SC_CORES * SC_SUBCORES
    per_worker = n_rows // n_workers
    chunk = SC_GATHER_ROWS
    assert per_worker * n_workers == n_rows and per_worker % chunk == 0
    mesh = plsc.VectorSubcoreMesh(core_axis_name="c", subcore_axis_name="s")

    @functools.partial(
        pl.kernel, mesh=mesh, out_type=jax.ShapeDtypeStruct((n_rows, d), table.dtype),
        scratch_types=[pltpu.VMEM((chunk,), I32), pltpu.VMEM((chunk, d), table.dtype), pltpu.SemaphoreType.DMA],
        name="sc_gather_rows")
    def gather(table_hbm, idx_hbm, out_hbm, idx_v, rows_v, sem):
        base = (lax.axis_index("s") * SC_CORES + lax.axis_index("c")) * per_worker

        @pl.loop(0, per_worker // chunk)
        def _(j):
            off = pl.multiple_of(base + j * chunk, 8)
            pltpu.sync_copy(idx_hbm.at[pl.ds(off, chunk)], idx_v)
            pltpu.async_copy(table_hbm.at[idx_v], rows_v, sem).wait()
            pltpu.sync_copy(rows_v, out_hbm.at[pl.ds(off, chunk)])

    return gather(table, idx)


def _dispatch_rows(table, slot, n_out):
    assert TOP_K == 2
    n_assign, (n_tok, d) = slot.shape[0], table.shape
    n_workers = SC_CORES * SC_SUBCORES
    per_worker = n_out // n_workers
    chunk, vec = SC_GATHER_ROWS, SC_LANES
    assert per_worker * n_workers == n_out and per_worker % chunk == 0 and n_assign % vec == 0
    assert n_out < 3 * n_tok
    mesh = plsc.VectorSubcoreMesh(core_axis_name="c", subcore_axis_name="s")

    @functools.partial(
        pl.kernel, mesh=mesh, out_type=jax.ShapeDtypeStruct((n_out, d), table.dtype),
        scratch_types=[pltpu.VMEM((n_assign,), I32), pltpu.VMEM((per_worker,), I32),
                       pltpu.VMEM((chunk, d), table.dtype), pltpu.SemaphoreType.DMA],
        compiler_params=pltpu.CompilerParams(needs_layout_passes=False),
        name="sc_dispatch_rows")
    def dispatch(table_hbm, slot_hbm, out_hbm, slots_v, tok_v, rows_v, sem):
        base = (lax.axis_index("s") * SC_CORES + lax.axis_index("c")) * per_worker
        pltpu.sync_copy(slot_hbm, slots_v)
        lane = lax.broadcasted_iota(I32, (vec,), 0)

        @pl.loop(0, per_worker // vec)
        def _(j):
            p = base + j * vec + lane
            p = jnp.where(p >= n_tok, p - n_tok, p)
            tok_v[pl.ds(j * vec, vec)] = jnp.where(p >= n_tok, p - n_tok, p)

        @pl.loop(0, n_assign // vec)
        def _(j):
            s = slots_v[pl.ds(j * vec, vec)] - base
            mine = (s >= 0) & (s < per_worker)
            tok = lax.shift_right_logical(j * vec + lane, 1)
            plsc.store_scatter(tok_v, [jnp.where(mine, s, 0)], tok, mask=mine)

        @pl.loop(0, per_worker // chunk)
        def _(j):
            off = pl.multiple_of(j * chunk, 8)
            pltpu.async_copy(table_hbm.at[tok_v.at[pl.ds(off, chunk)]], rows_v, sem).wait()
            pltpu.sync_copy(rows_v, out_hbm.at[pl.ds(base + off, chunk)])

    return dispatch(table, slot)


def _moe_kernel(be_ref, nu_ref, nv_ref, x_ref, w1g_ref, w1u_ref, w2_ref, y_ref, xb_ref, acc_ref):
    i, j = pl.program_id(0), pl.program_id(1)
    bm = x_ref.shape[0]

    def swiglu_rows(rows):
        x = xb_ref[0:rows, :]
        a = (_silu(_dot(x, w1g_ref[0].astype(BF16))) * _dot(x, w1u_ref[0].astype(BF16))).astype(BF16)
        acc_ref[0:rows, :] += _dot(a, w2_ref[0].astype(BF16))

    @pl.when(i < nu_ref[0])
    def _():
        @pl.when(j == 0)
        def _():
            acc_ref[...] = jnp.zeros_like(acc_ref)
            xb_ref[...] = _unpack_rows(x_ref[...]).astype(BF16)

        for rows in range(MOE_ROW_STEP, bm + 1, MOE_ROW_STEP):
            @pl.when((nv_ref[i] > rows - MOE_ROW_STEP) & (nv_ref[i] <= rows))
            def _(rows=rows):
                swiglu_rows(rows)

        @pl.when(j == pl.num_programs(1) - 1)
        def _():
            y_ref[...] = _pack_rows(acc_ref[...])


def _moe_experts(xs, w1, w2, block_e, n_used, n_valid, n_blocks):
    bm, fc = MOE_ROWS, MOE_FF
    nff = FF_EXPERT // fc
    used = lambda i, nu: jnp.minimum(i, nu[0] - 1)
    ffi = lambda i, j, nu: jnp.where(i < nu[0], j, nff - 1)
    grid_spec = pltpu.PrefetchScalarGridSpec(
        num_scalar_prefetch=3,
        grid=(n_blocks, nff),
        in_specs=[pl.BlockSpec((bm, D_MODEL // 2), lambda i, j, be, nu, nv: (used(i, nu), 0)),
                  pl.BlockSpec((1, D_MODEL, fc), lambda i, j, be, nu, nv: (be[used(i, nu)], 0, ffi(i, j, nu))),
                  pl.BlockSpec((1, D_MODEL, fc),
                               lambda i, j, be, nu, nv: (be[used(i, nu)], 0, nff + ffi(i, j, nu))),
                  pl.BlockSpec((1, fc, D_MODEL), lambda i, j, be, nu, nv: (be[used(i, nu)], ffi(i, j, nu), 0))],
        out_specs=pl.BlockSpec((bm, D_MODEL // 2), lambda i, j, be, nu, nv: (used(i, nu), 0)),
        scratch_shapes=[pltpu.VMEM((bm, D_MODEL), BF16), pltpu.VMEM((bm, D_MODEL), F32)],
    )
    return pl.pallas_call(
        _moe_kernel,
        grid_spec=grid_spec,
        out_shape=jax.ShapeDtypeStruct((n_blocks * bm, D_MODEL // 2), jnp.uint32),
        compiler_params=_cparams("arbitrary", "arbitrary"),
        name="moe_experts",
    )(block_e, n_used, n_valid, xs, w1, w1, w2)


def _combine_kernel(y0_ref, y1_ref, gate_ref, x_ref, modl_ref, g_ref, o_ref):
    gt = gate_ref[...]
    y = gt[:, 0:1] * _unpack_rows(y0_ref[...]) + gt[:, 1:2] * _unpack_rows(y1_ref[...])
    o_ref[...] = x_ref[...] + modl_ref[0, 5:6, :] * _rms(y, g_ref[...])


def _combine(yk, gates, x2, modl, g, bsz, t, ctx_len):
    tm = LATENT_TILE
    nlt = (t - ctx_len) // tm
    nl = bsz * (t - ctx_len)
    orow = lambda c: pl.BlockSpec((tm, c), lambda b, i: (b * nlt + i, 0))
    second = pl.BlockSpec((tm, D_MODEL // 2), lambda b, i: (nl // tm + b * nlt + i, 0))
    return pl.pallas_call(
        _combine_kernel,
        grid=(bsz, nlt),
        in_specs=[orow(D_MODEL // 2), second, orow(LANES),
                  _latent_rows(tm, t, ctx_len),
                  pl.BlockSpec((1, 8, D_MODEL), lambda b, i: (b, 0, 0)), _resident((1, D_MODEL))],
        out_specs=orow(D_MODEL),
        out_shape=jax.ShapeDtypeStruct((nl, D_MODEL), F32),
        compiler_params=_cparams("arbitrary", "arbitrary"),
        name="moe_combine",
    )(yk, yk, gates, x2, modl, g)


def _routing_tables(experts, ranks, counts):
    bm = MOE_ROWS
    n_assign = experts.size
    padded = (counts + bm - 1) // bm * bm
    ends_p = jnp.cumsum(padded)
    pstarts = ends_p - padded
    onehot = experts[..., None] == jnp.arange(N_EXPERTS, dtype=I32)
    slot = (jnp.sum(jnp.where(onehot, pstarts, 0), axis=-1) + ranks).astype(I32)
    n_blocks = n_assign // bm + N_EXPERTS
    block_e = jnp.minimum(jnp.sum(ends_p[None, :] <= (jnp.arange(n_blocks, dtype=I32) * bm)[:, None], axis=1),
                          N_EXPERTS - 1).astype(I32)
    n_used = (ends_p[-1] // bm).astype(I32).reshape(1)
    sel = block_e[:, None] == jnp.arange(N_EXPERTS, dtype=I32)
    rows_left = jnp.sum(jnp.where(sel, counts + pstarts, 0), axis=1) - jnp.arange(n_blocks, dtype=I32) * bm
    n_valid = jnp.clip(rows_left, 0, bm).astype(I32)
    return slot, block_e, n_used, n_valid, n_blocks


def kernel(x, c, ctx, c_ctx, mod_w, mod_b, norm_g, ab_in_w, m_conv_w, m_conv_b, m_gate_b, m_norm_w, l_conv_w,
           l_conv_b, l_wa, l_ba, l_wx, l_bx, l_lam, ab_out_w, ffn_w1, ffn_w2, cd_in_w, g_alpha_w, g_alpha_b,
           g_norm_w, s_conv_w, s_conv_b, s_dt_bias, s_A_log, s_D, s_norm_w, cd_out_w, router_w, router_b,
           moe_w1, moe_w2):
    bsz, seq, dm = x.shape
    ctx_len = ctx.shape[1]
    t = ctx_len + seq
    n = bsz * t
    assert ctx_len == ROW_TILE and seq % ROW_TILE == 0 and ROW_TILE % GRID_W == 0
    assert n % (IN_PARTS * ROW_TILE) == 0

    c_all = jnp.concatenate([c, c_ctx[None, :], jnp.zeros((-(bsz + 1) % 8, dm), F32)], axis=0)
    mods = _modulation(c_all, mod_w, mod_b).reshape(mod_w.shape[0], c_all.shape[0], 6, dm)
    pad2 = jnp.zeros((bsz, 2, dm), F32)

    def layer_mods(layer):
        modl = jnp.concatenate([mods[layer, :bsz], pad2], axis=1)
        modc = jnp.concatenate([mods[layer, bsz], pad2[0]], axis=0)
        return modl, modc

    modl, modc = layer_mods(0)
    g = norm_g[0]
    w = ab_in_w[0]
    o_qk, o_v, o_o, o_gt, o_lx, o_lg = 0, 1024, 2048, 3072, 3088, 4112
    w_in = jnp.concatenate([w[:, o_qk:o_v], w[:, o_v:o_o], w[:, o_o:o_gt], w[:, o_lx:o_lg], w[:, o_lg:],
                            w[:, o_gt:o_lx], jnp.zeros((dm, LANES - 16), F32)], axis=1).astype(BF16)
    gb = jnp.concatenate([m_gate_b[0].reshape(1, 16), jnp.zeros((1, LANES - 16), F32)], axis=1)
    x2, q, k, v, og, lx, glg, gates, gates_t = _in_ab(
        ctx.reshape(bsz * ctx_len, dm), x.reshape(bsz * seq, dm), modl, modc, g[0:1], w_in, m_conv_w[0],
        m_conv_b[0][None], l_conv_w[0], l_conv_b[0][None], gb, bsz, t, ctx_len)
    hm_f, hm_b = _mlstm(q, k, v, gates, gates_t, bsz, t, ctx_len)
    hl_f, hl_b = _lru(lx, l_wa[0].astype(BF16), l_wx[0].astype(BF16), l_ba[0][:, None], l_bx[0][:, None],
                      l_lam[0][:, None], bsz, t, ctx_len)
    x2 = _out_proj(_out_ab_kernel, "out_proj_swiglu_ab", [hm_f, hm_b, og, hl_f, hl_b, glg], [m_norm_w[0][None]],
                   x2, modl, modc, g[1:2], ab_out_w[0].astype(BF16), bsz, t, ctx_len,
                   extra=(g[2:3], g[3:4], ffn_w1[0].astype(BF16), ffn_w2[0].astype(BF16)))

    modl, modc = layer_mods(1)
    g = norm_g[1]
    w = cd_in_w[0]
    o_gv, o_gr, o_ga, o_z, o_xbc, o_dt = 1024, 2048, 3072, 3104, 4128, 5664
    zpad = jnp.zeros((dm, LANES - 32), F32)
    w_in = jnp.concatenate([w[:, :o_gv], w[:, o_gv:o_gr], w[:, o_gr:o_ga], w[:, o_z:o_xbc], w[:, o_xbc:o_dt],
                            w[:, o_ga:o_z], zpad, w[:, o_dt:], w[:, o_dt:], jnp.zeros((dm, LANES - 64), F32)],
                           axis=1).astype(BF16)
    dtb = jnp.concatenate([s_dt_bias[0].reshape(1, 32), s_dt_bias[0].reshape(1, 32),
                           jnp.zeros((1, LANES - 64), F32)], axis=1)
    aneg = jnp.concatenate([jnp.zeros((1, 32), F32), -jnp.exp(s_A_log[0].reshape(1, 32)),
                            jnp.zeros((1, LANES - 64), F32)], axis=1)
    gq, gk, gv, grs, zs, sx, sb, sc, ga, dts, dts_t = _in_cd(x2, modl, modc, g[0:1], w_in, s_conv_w[0],
                                                             s_conv_b[0][None], dtb, aneg, bsz, t, ctx_len)
    aw = jnp.zeros((2, LANES, G_HEADS * G_DK), F32)
    aw = aw.at[0, 0:G_RANK].set(g_alpha_w[0, 0]).at[1, G_RANK:2 * G_RANK].set(g_alpha_w[0, 1])
    awh = aw.astype(BF16)
    awl = (aw - awh.astype(F32)).astype(BF16)
    og_f, og_b = _gla(gq, gk, gv, ga, awh, awl, g_alpha_b[0][:, None], bsz, t, ctx_len)
    ys_f, ys_b = _ssd(sx, sb, sc, dts, dts_t, bsz, t, ctx_len)
    dskip = jnp.repeat(s_D[0], S_P)[None]
    x2 = _out_proj(_out_cd_kernel, "out_proj_cd", [og_f, og_b, grs, ys_f, ys_b, sx, zs],
                   [g_norm_w[0][None], dskip, s_norm_w[0][None]],
                   x2, modl, modc, g[1:2], cd_out_w[0].astype(BF16), bsz, t, ctx_len, latent_only=True)
    t, ctx_len = seq, 0

    rw = jnp.concatenate([router_w[0], jnp.zeros((dm, LANES - N_EXPERTS), F32)], axis=1)
    rwh = rw.astype(BF16)
    rw = jnp.concatenate([rwh, (rw - rwh.astype(F32)).astype(BF16)], axis=1)
    rb = jnp.concatenate([router_b[0][None], jnp.zeros((1, LANES - N_EXPERTS), F32)], axis=1)
    h2, top_idx, top_gate, counts = _router(x2, modl, g[2:3], rw, rb, bsz, t, ctx_len)
    slot, block_e, n_used, n_valid, n_blocks = _routing_tables(top_idx[:, 0:TOP_K], top_idx[:, TOP_K:2 * TOP_K],
                                                               counts[0, :N_EXPERTS].astype(I32))
    xs = _dispatch_rows(h2, slot.reshape(-1), n_blocks * MOE_ROWS)
    ys_e = _moe_experts(xs, moe_w1[0], moe_w2[0], block_e, n_used, n_valid, n_blocks)
    yk = _gather_rows(ys_e, jnp.concatenate([slot[:, 0], slot[:, 1]]))
    out = _combine(yk, top_gate, x2, modl, g[3:4], bsz, t, ctx_len)
    return out.reshape(bsz, seq, dm)
```

```python
import functools

import jax
import jax.numpy as jnp
from jax import lax
from jax.experimental import pallas as pl
from jax.experimental.pallas import tpu as pltpu
from jax.experimental.pallas import tpu_sc as plsc

F32, BF16, I32 = jnp.float32, jnp.bfloat16, jnp.int32

D_MODEL = 1024
GRID_W = 64
EPS = 1e-6
LOG2E = 1.4426950408889634
M_HEADS, M_DK, M_DV = 4, 128, 256
LRU_BLOCKS, LRU_BW, LRU_C = 8, 128, 8.0
G_HEADS, G_DK, G_DV, G_RANK, G_TAU = 4, 128, 256, 16, 16.0
S_HEADS, S_P, S_N, S_GROUPS, S_HPG = 16, 64, 128, 2, 8
N_EXPERTS, TOP_K, FF_EXPERT = 8, 2, 3584

LANES, SUBLANES = 128, 8
VMEM_LIMIT = 56 * 1024 * 1024
ROW_TILE = 256
IN_PARTS = 2
LATENT_TILE = 512
BIG_ROW_TILE_MAX = 640
PROJ_COLS = 256
M_CHUNK = 256
G_CHUNK, G_SUB = 256, 16
G_GROUP = 64
S_CHUNK = 256
LRU_TILE = 256
FF_STEP = 256
MOE_ROWS = 1024
MOE_FF = 512
MOE_ROW_STEP = 256
SC_CORES, SC_SUBCORES = 2, 16
SC_LANES = 16
SC_GATHER_ROWS = 128


def _cparams(*sem):
    return pltpu.CompilerParams(dimension_semantics=sem, vmem_limit_bytes=VMEM_LIMIT)


def _sigmoid(x):
    return 0.5 + 0.5 * jnp.tanh(0.5 * x)


def _silu(x):
    return x * _sigmoid(x)


def _softplus(x):
    return jnp.maximum(x, 0.0) + jnp.log1p(jnp.exp(-jnp.abs(x)))


def _log_sigmoid(x):
    return jnp.minimum(x, 0.0) - jnp.log1p(jnp.exp(-jnp.abs(x)))


def _dot(a, b):
    return jnp.dot(a, b, preferred_element_type=F32)


def _dot_nt(a, b):
    return lax.dot_general(a, b, (((1,), (1,)), ((), ())), preferred_element_type=F32)


def _dot_tn(a, b):
    return lax.dot_general(a, b, (((0,), (0,)), ((), ())), preferred_element_type=F32)


def _split3(f):
    f1 = f.astype(BF16)
    r = f - f1.astype(F32)
    f2 = r.astype(BF16)
    f3 = (r - f2.astype(F32)).astype(BF16)
    return f1, f2, f3


def _sel_cols(mask01, f):
    p1, p2, p3 = _split3(f)
    return _dot(mask01, p1) + _dot(mask01, p2) + _dot(mask01, p3)


def _sel_rows(f, mask01):
    p1, p2, p3 = _split3(f)
    return _dot_nt(p1, mask01) + _dot_nt(p2, mask01) + _dot_nt(p3, mask01)


def _pack_rows(u):
    n = u.shape[1] // 2
    bits = lax.bitcast_convert_type(u.astype(BF16).astype(F32), jnp.uint32)
    return jnp.bitwise_or(bits[:, :n], jnp.right_shift(bits[:, n:], jnp.uint32(16)))


def _unpack_rows(p):
    hi = lax.bitcast_convert_type(jnp.bitwise_and(p, jnp.uint32(0xFFFF0000)), F32)
    lo = lax.bitcast_convert_type(jnp.left_shift(p, jnp.uint32(16)), F32)
    return jnp.concatenate([hi, lo], axis=1)


def _rms(u, g):
    return u * lax.rsqrt(jnp.mean(u * u, axis=-1, keepdims=True) + EPS) * g


def _head_rms(u, g, n_heads):
    w = u.shape[-1] // n_heads
    return jnp.concatenate([_rms(u[:, h * w:(h + 1) * w], g[:, h * w:(h + 1) * w]) for h in range(n_heads)], axis=1)


def _row_ids(tile_idx, tm):
    return tile_idx * tm + lax.broadcasted_iota(I32, (tm, 1), 0)


def _mod(modl_ref, modc_ref, idx, is_ctx):
    return jnp.where(is_ctx, modc_ref[idx:idx + 1, :], modl_ref[0, idx:idx + 1, :])


def _adaln(x, g, modl_ref, modc_ref, shift_idx, is_ctx):
    shift = _mod(modl_ref, modc_ref, shift_idx, is_ctx)
    scale = _mod(modl_ref, modc_ref, shift_idx + 1, is_ctx)
    return _rms(x, g) * (1.0 + scale) + shift


def _neighbour_tiles(r, step, is_lat):
    n_tiles, per_seg = r.shape[0], GRID_W // SUBLANES
    zero = jnp.zeros((1,) + r.shape[1:], r.dtype)

    def across(src):
        return jnp.where(is_lat, 0.0, r[src:src + 1]) if 0 <= src < n_tiles else zero

    pieces = []
    for s0 in range(0, n_tiles, per_seg):
        if step > 0:
            pieces += [across(s0 - 1), r[s0:s0 + per_seg - 1]]
        else:
            pieces += [r[s0 + 1:s0 + per_seg], across(s0 + per_seg)]
    return jnp.concatenate(pieces, axis=0)


def _dwconv(y, cw, cb, is_lat):
    tm, n = y.shape
    y3 = y.reshape(tm // SUBLANES, SUBLANES, n)
    sub = lax.broadcasted_iota(I32, (1, SUBLANES, 1), 1)
    w = lambda j: cw[j:j + 1, :].reshape(1, 1, n)
    r1, r2, r7 = pltpu.roll(y3, 1, 1), pltpu.roll(y3, 2, 1), pltpu.roll(y3, SUBLANES - 1, 1)
    ym1 = jnp.where(sub >= 1, r1, _neighbour_tiles(r1, 1, is_lat))
    ym2 = jnp.where(sub >= 2, r2, _neighbour_tiles(r2, 1, is_lat))
    yp1 = jnp.where(sub <= SUBLANES - 2, r7, _neighbour_tiles(r7, -1, is_lat))
    out = cb.reshape(1, 1, n) + w(0) * ym2 + w(1) * ym1 + w(2) * y3 + w(3) * yp1
    return out.reshape(tm, n)


def _resident(shape):
    nd = len(shape)
    return pl.BlockSpec(shape, lambda *_: (0,) * nd, pipeline_mode=pl.Buffered(1))


def _big_row_tile(t):
    return max(tm for tm in range(8, BIG_ROW_TILE_MAX + 1, 8) if t % tm == 0)


def _mod_kernel(c_ref, w_ref, b_ref, o_ref):
    a1, a2, _ = _split3(_silu(c_ref[...]))
    w = w_ref[0]
    wh = w.astype(BF16)
    wl = (w - wh.astype(F32)).astype(BF16)
    o_ref[0] = _dot(a1, wh) + _dot(a2, wh) + _dot(a1, wl) + b_ref[0]


def _modulation(c_all, mod_w, mod_b):
    depth, d, n6 = mod_w.shape
    rows = c_all.shape[0]
    tn = 1024
    return pl.pallas_call(
        _mod_kernel,
        grid=(depth, n6 // tn),
        in_specs=[pl.BlockSpec((rows, d), lambda l, j: (0, 0)),
                  pl.BlockSpec((1, d, tn), lambda l, j: (l, 0, j)),
                  pl.BlockSpec((1, 1, tn), lambda l, j: (l, 0, j))],
        out_specs=pl.BlockSpec((1, rows, tn), lambda l, j: (l, 0, j)),
        out_shape=jax.ShapeDtypeStruct((depth, rows, n6), F32),
        compiler_params=_cparams("arbitrary", "arbitrary"),
        name="modulation",
    )(c_all, mod_w, mod_b.reshape(depth, 1, n6))


def _project(h_ref, w_ref, w_col, out_ref, epilogue):
    for c in range(0, out_ref.shape[1], PROJ_COLS):
        y = _dot(h_ref[...], w_ref[:, w_col + c:w_col + c + PROJ_COLS])
        out_ref[:, c:c + PROJ_COLS] = epilogue(y, c).astype(out_ref.dtype)


def _part_is_latent(part, parts_per_batch):
    return lax.rem(pl.program_id(0) * IN_PARTS + part, parts_per_batch) != 0


def _conv_parts(y, cw, cb, lat):
    return jnp.concatenate([_dwconv(y[p * ROW_TILE:(p + 1) * ROW_TILE], cw, cb, lat[p]) for p in range(IN_PARTS)],
                           axis=0)


def _in_ab_kernel(parts_per_batch, *refs):
    xs, refs = refs[:3 * IN_PARTS], refs[3 * IN_PARTS:]
    modc_ref, g_ref, w_ref, mcw_ref, mcb_ref, lcw_ref, lcb_ref, gb_ref = refs[:8]
    x_ref, q_ref, k_ref, v_ref, og_ref, lx_ref, glg_ref, gates_ref, gates_t_ref, h_scr = refs[8:]
    lat = [_part_is_latent(p, parts_per_batch) for p in range(IN_PARTS)]
    for p in range(IN_PARTS):
        xc_ref, xl_ref, modl_ref = xs[3 * p:3 * p + 3]
        rows = slice(p * ROW_TILE, (p + 1) * ROW_TILE)
        xp = jnp.where(lat[p], xl_ref[...], xc_ref[...])
        x_ref[rows, :] = xp
        h_scr[rows, :] = _adaln(xp, g_ref[...], modl_ref, modc_ref, 0, jnp.logical_not(lat[p])).astype(BF16)

    def conv(cw_ref, cb_ref):
        return lambda y, c: _conv_parts(y, cw_ref[:, c:c + PROJ_COLS], cb_ref[:, c:c + PROJ_COLS], lat)

    qc, kc, lc = conv(mcw_ref, mcb_ref), conv(mcw_ref, mcb_ref), conv(lcw_ref, lcb_ref)
    _project(h_scr, w_ref, 0, q_ref, lambda y, c: _silu(qc(y, c)))
    _project(h_scr, w_ref, 512, k_ref, lambda y, c: _silu(kc(y, 512 + c)) * (M_DK ** -0.5))
    _project(h_scr, w_ref, 1024, v_ref, lambda y, c: y)
    _project(h_scr, w_ref, 2048, og_ref, lambda y, c: _sigmoid(y))
    _project(h_scr, w_ref, 3072, lx_ref, lc)
    _project(h_scr, w_ref, 4096, glg_ref, lambda y, c: jax.nn.gelu(y))
    gt = _dot(h_scr[...], w_ref[:, 5120:5248]) + gb_ref[...]
    lane = lax.broadcasted_iota(I32, gt.shape, 1)
    gates = jnp.where(lane >= 2 * M_HEADS, _log_sigmoid(gt), gt)
    gates_ref[...] = gates
    gates_t_ref[...] = gates.T


def _in_ab(xc2, xl2, modl, modc, g, w, mcw, mcb, lcw, lcb, gb, bsz, t, ctx_len):
    n = bsz * t
    tm = IN_PARTS * ROW_TILE
    ppb = t // ROW_TILE
    lpb = ppb - 1
    part = lambda j, p: j * IN_PARTS + p
    batch = lambda j, p: part(j, p) // ppb
    x_specs, x_args = [], []
    for p in range(IN_PARTS):
        x_specs += [pl.BlockSpec((ROW_TILE, D_MODEL), lambda j, p=p: (batch(j, p), 0)),
                    pl.BlockSpec((ROW_TILE, D_MODEL),
                                 lambda j, p=p: (batch(j, p) * lpb + jnp.maximum(part(j, p) % ppb - 1, 0), 0)),
                    pl.BlockSpec((1, 8, D_MODEL), lambda j, p=p: (batch(j, p), 0, 0))]
        x_args += [xc2, xl2, modl]
    row = lambda c: pl.BlockSpec((tm, c), lambda j: (j, 0))
    outs = [(D_MODEL, F32), (512, BF16), (512, BF16), (1024, BF16), (1024, BF16), (1024, BF16), (1024, BF16),
            (LANES, F32)]
    params = (modc, g, w, mcw, mcb, lcw, lcb, gb)
    return pl.pallas_call(
        functools.partial(_in_ab_kernel, ppb),
        grid=(n // tm,),
        in_specs=x_specs + [_resident(a.shape) for a in params],
        out_specs=[row(c) for c, _ in outs] + [pl.BlockSpec((LANES, tm), lambda j: (0, j))],
        out_shape=[jax.ShapeDtypeStruct((n, c), dt) for c, dt in outs] + [jax.ShapeDtypeStruct((LANES, n), F32)],
        scratch_shapes=[pltpu.VMEM((tm, D_MODEL), BF16)],
        compiler_params=_cparams("arbitrary"),
        name="in_proj_ab",
    )(*x_args, *params)


def _reverse_order(n_ctx_chunks, n_chunks):
    return lambda c: jnp.where(c < n_ctx_chunks, n_ctx_chunks - 1 - c, n_chunks - 1 - (c - n_ctx_chunks))


def _tri_mask(L, reverse):
    row = lax.broadcasted_iota(I32, (L, L), 0)
    col = lax.broadcasted_iota(I32, (L, L), 1)
    return (col >= row) if reverse else (col <= row)


def _bidir_scan(kern, L, row_ins, lane_ins, res_ins, out_cols, scratch, bsz, t, ctx_len, name):
    nc = t // L
    rev = _reverse_order(ctx_len // L, nc)
    fwd = lambda i: i
    n = bsz * t

    def row(c, order):
        return pl.BlockSpec((L, c), lambda b, i: (b * nc + order(i), 0))

    def lanes(r, order):
        return pl.BlockSpec((r, L), lambda b, i: (0, b * nc + order(i)))

    in_specs, args = [], []
    for order in (fwd, rev):
        in_specs += [row(a.shape[1], order) for a in row_ins] + [lanes(r, order) for _, r in lane_ins]
        args += list(row_ins) + [a for a, _ in lane_ins]
    in_specs += [_resident(a.shape) for a in res_ins]
    args += list(res_ins)
    return pl.pallas_call(
        kern,
        grid=(bsz, nc),
        in_specs=in_specs,
        out_specs=[row(c, fwd) for c in out_cols] + [row(c, rev) for c in out_cols],
        out_shape=[jax.ShapeDtypeStruct((n, c), BF16) for c in out_cols] * 2,
        scratch_shapes=scratch,
        compiler_params=_cparams("arbitrary", "arbitrary"),
        name=name,
    )(*args)


def _zero_at_start(*scratch):
    @pl.when(pl.program_id(1) == 0)
    def _():
        for s in scratch:
            s[...] = jnp.zeros_like(s)


def _mlstm_kernel(L, qf, kf, vf, gcf, grf, qb, kb, vb, gcb, grb, of, ob, c_scr, m_scr):
    _zero_at_start(c_scr, m_scr)
    ones = jnp.ones((L, LANES), BF16)
    wide = lambda c, k: jnp.concatenate([c] * k, axis=1)
    er = lax.broadcasted_iota(I32, (LANES, 2 * M_HEADS * LANES), 0)
    ej = jnp.right_shift(lax.broadcasted_iota(I32, (LANES, 2 * M_HEADS * LANES), 1), LANES.bit_length() - 1)
    chains = []
    for d, (q_ref, k_ref, v_ref, gc_ref, gr_ref, out_ref) in enumerate(((qf, kf, vf, gcf, grf, of),
                                                                         (qb, kb, vb, gcb, grb, ob))):
        reverse = d == 1
        mask = _tri_mask(L, reverse)
        mask01 = mask.astype(BF16)
        gc = gc_ref[...]
        gr = gr_ref[...]
        bcol = _sel_cols(mask01, gc)
        brow = _sel_rows(gr, mask01)
        glane = lax.broadcasted_iota(I32, (L, LANES), 1)
        src = jnp.where(ej < M_HEADS, M_HEADS * d + ej, M_HEADS + M_HEADS * d + ej)
        s1, s2, _ = _split3(jnp.where(glane < 2 * M_HEADS, gc, bcol))
        sel = (er == src).astype(BF16)
        rep = _dot(s1, sel) + _dot(s2, sel)
        last = 0 if reverse else L - 1
        for h in range(M_HEADS):
            fo, io = 2 * M_HEADS + M_HEADS * d + h, M_HEADS * d + h
            ic, bc = rep[:, h * LANES:(h + 1) * LANES], rep[:, (M_HEADS + h) * LANES:(M_HEADS + h + 1) * LANES]
            r1 = gr[io:io + 1, :] - brow[fo:fo + 1, :]
            m = m_scr[d, h:h + 1, :]
            g = bc + m
            mx = jnp.max(jnp.where(mask, r1, -jnp.inf), axis=1, keepdims=True)
            mt = jnp.maximum(g, bc + mx)
            p = jnp.exp(jnp.where(mask, wide(bc - mt, L // LANES) + r1, -jnp.inf))
            bl = bc[last:last + 1, :]
            a_s = bl - bc + ic
            m_new = jnp.maximum(bl + m, jnp.max(a_s, axis=0, keepdims=True))
            chains.append(dict(
                d=d, h=h, out_ref=out_ref, mt=mt, w_inter=jnp.exp(g - mt), p=p,
                m_new=m_new, decay=jnp.exp(bl + m - m_new), ws=jnp.exp(a_s - m_new),
                qh=q_ref[:, h * M_DK:(h + 1) * M_DK], kh=k_ref[:, h * M_DK:(h + 1) * M_DK],
                vx=jnp.concatenate([v_ref[:, h * M_DV:(h + 1) * M_DV], ones], axis=1)))
    for ch in chains:
        ch["sc"] = (_dot_nt(ch["qh"], ch["kh"]) * ch["p"]).astype(BF16)
    for ch in chains:
        cx = c_scr[ch["d"], ch["h"]]
        numx = wide(ch["w_inter"], 1 + M_DV // LANES) * _dot(ch["qh"], cx.astype(BF16)) + _dot(ch["sc"], ch["vx"])
        den = numx[:, M_DV:]
        rinv = 1.0 / jnp.maximum(jnp.abs(den), jnp.exp(-ch["mt"]))
        hh = numx[:, 0:M_DV] * wide(rinv, M_DV // LANES)
        ch["out_ref"][:, ch["h"] * M_DV:(ch["h"] + 1) * M_DV] = hh.astype(BF16)
    for ch in chains:
        d, h = ch["d"], ch["h"]
        kw = (ch["kh"].astype(F32) * ch["ws"]).astype(BF16)
        c_scr[d, h] = wide(ch["decay"], 1 + M_DV // LANES) * c_scr[d, h] + _dot_tn(kw, ch["vx"])
        m_scr[d, h:h + 1, :] = ch["m_new"]


def _mlstm(q, k, v, gcol, grow, bsz, t, ctx_len):
    L = M_CHUNK
    scratch = [pltpu.VMEM((2, M_HEADS, M_DK, M_DV + LANES), F32), pltpu.VMEM((2, 8, LANES), F32)]
    return _bidir_scan(functools.partial(_mlstm_kernel, L), L, [q, k, v, gcol], [(grow, 16)], [], [1024], scratch,
                       bsz, t, ctx_len, "mlstm_scan")


def _lru_kernel(L, uf, ub_, wa_ref, wx_ref, ba_ref, bx_ref, lam_ref, of, ob, a_scr, b_scr, h_scr):
    _zero_at_start(h_scr)
    R, C = L // SUBLANES, LRU_BLOCKS * LRU_BW
    i0, i1 = lax.broadcasted_iota(I32, (L, L), 0), lax.broadcasted_iota(I32, (L, L), 1)
    when = lambda p: jnp.bitwise_and(p, SUBLANES - 1) * R + jnp.right_shift(p, SUBLANES.bit_length() - 1)
    regroup = (i1 == when(i0)).astype(BF16)
    ungroup = (i0 == when(i1)).astype(BF16)
    sub = lax.broadcasted_iota(I32, (SUBLANES, 1), 0)
    dirs = ((uf, of), (ub_, ob))
    for d, (u_ref, _) in enumerate(dirs):
        u = _dot(regroup, u_ref[...])
        ub = u.astype(BF16)
        sp = -LRU_C * _softplus(-lam_ref[d])
        for n in range(LRU_BLOCKS):
            sl = slice(n * LRU_BW, (n + 1) * LRU_BW)
            rg = _sigmoid(_dot(ub[:, sl], wa_ref[d, n]) + ba_ref[d, :, sl])
            ig = _sigmoid(_dot(ub[:, sl], wx_ref[d, n]) + bx_ref[d, :, sl])
            log_a = rg * sp[:, sl]
            a = jnp.exp(log_a)
            a_scr[d, :, sl] = a
            b_scr[d, :, sl] = jnp.sqrt(-jnp.tanh(log_a) * (a * a + 1.0)) * (ig * u[:, sl])
    S = SUBLANES
    hs = [jnp.zeros((S, C), F32), jnp.zeros((S, C), F32)]
    ps = [jnp.ones((S, C), F32), jnp.ones((S, C), F32)]
    for j in range(R):
        for d in range(2):
            rows = slice(S * j, S * j + S) if d == 0 else slice(S * (R - 1 - j), S * (R - j))
            a = a_scr[d, rows, :]
            hs[d] = a * hs[d] + b_scr[d, rows, :]
            ps[d] = a * ps[d]
            b_scr[d, rows, :] = hs[d]
            a_scr[d, rows, :] = ps[d]
    for d, (_, out_ref) in enumerate(dirs):
        c = h_scr[d:d + 1, :]
        cin = jnp.zeros((S, C), F32)
        for s in (range(S) if d == 0 else reversed(range(S))):
            cin = jnp.where(sub == s, c, cin)
            c = ps[d][s:s + 1, :] * c + hs[d][s:s + 1, :]
        h_scr[d:d + 1, :] = c
        h = b_scr[d].reshape(R, S, C) + a_scr[d].reshape(R, S, C) * cin[None]
        out_ref[...] = _dot(ungroup, h.reshape(L, C).astype(BF16)).astype(BF16)


def _lru(u, wa, wx, ba, bx, lam, bsz, t, ctx_len):
    L = LRU_TILE
    scratch = [pltpu.VMEM((2, L, 1024), F32), pltpu.VMEM((2, L, 1024), F32), pltpu.VMEM((8, 1024), F32)]
    return _bidir_scan(functools.partial(_lru_kernel, L), L, [u], [], [wa, wx, ba, bx, lam], [1024], scratch,
                       bsz, t, ctx_len, "rglru_scan")


def _swiglu_rows(x, is_ctx, modl_ref, modc_ref, g2_ref, g3_ref, w1_ref, w2_ref):
    h = _adaln(x, g2_ref[...], modl_ref, modc_ref, 3, is_ctx).astype(BF16)
    ff = w2_ref.shape[0]
    acc = jnp.zeros(x.shape, F32)
    for j in range(ff // FF_STEP):
        lo = j * FF_STEP
        gj = _dot(h, w1_ref[:, lo:lo + FF_STEP])
        uj = _dot(h, w1_ref[:, ff + lo:ff + lo + FF_STEP])
        acc = acc + _dot((_silu(gj) * uj).astype(BF16), w2_ref[lo:lo + FF_STEP, :])
    return x + _mod(modl_ref, modc_ref, 5, is_ctx) * _rms(acc, g3_ref[...])


def _out_tail(tm, ctx_len, a1, a2, x_ref, modl_ref, modc_ref, g_ref, w_ref, xo_ref, ffn_refs=None):
    is_ctx = False if ctx_len is None else _row_ids(pl.program_id(1), tm) < ctx_len
    half = w_ref.shape[0] // 2
    y = _dot(a1.astype(BF16), w_ref[0:half, :]) + _dot(a2.astype(BF16), w_ref[half:, :])
    x = x_ref[...] + _mod(modl_ref, modc_ref, 2, is_ctx) * _rms(y, g_ref[...])
    if ffn_refs is not None:
        x = _swiglu_rows(x, is_ctx, modl_ref, modc_ref, *ffn_refs)
    xo_ref[...] = x


def _out_ab_kernel(tm, ctx_len, hf, hb, og, lf, lb, glg, nw, x_ref, modl_ref, modc_ref, g_ref, w_ref,
                   g2_ref, g3_ref, w1_ref, w2_ref, xo_ref):
    hm = _head_rms(hf[...].astype(F32) + hb[...].astype(F32), nw[...], M_HEADS) * og[...].astype(F32)
    hl = (lf[...].astype(F32) + lb[...].astype(F32)) * glg[...].astype(F32)
    _out_tail(tm, ctx_len, hm, hl, x_ref, modl_ref, modc_ref, g_ref, w_ref, xo_ref,
              (g2_ref, g3_ref, w1_ref, w2_ref))


def _out_cd_kernel(tm, ctx_len, gf, gb, grs, yf, yb, sx, zs, gnw, dsk, snw, x_ref, modl_ref, modc_ref, g_ref,
                   w_ref, xo_ref):
    og = _head_rms(gf[...].astype(F32) + gb[...].astype(F32), gnw[...], G_HEADS) * grs[...].astype(F32)
    ys = yf[...].astype(F32) + yb[...].astype(F32) + dsk[...] * sx[...].astype(F32)
    ys = _rms(ys * zs[...].astype(F32), snw[...])
    _out_tail(tm, ctx_len, og, ys, x_ref, modl_ref, modc_ref, g_ref, w_ref, xo_ref)


def _out_proj(kern, name, acts, vecs, x2, modl, modc, g, w, bsz, t, ctx_len, extra=(), latent_only=False):
    if latent_only:
        tm = LATENT_TILE
        nt = (t - ctx_len) // tm
        row = lambda c: _latent_rows(tm, t, ctx_len, c)
        out_spec = pl.BlockSpec((tm, D_MODEL), lambda b, i: (b * nt + i, 0))
        out_rows = bsz * (t - ctx_len)
    else:
        tm = _big_row_tile(t)
        nt = t // tm
        row = lambda c: pl.BlockSpec((tm, c), lambda b, i: (b * nt + i, 0))
        out_spec, out_rows = row(D_MODEL), bsz * t
    return pl.pallas_call(
        functools.partial(kern, tm, None if latent_only else ctx_len),
        grid=(bsz, nt),
        in_specs=[row(1024)] * len(acts) + [_resident((1, 1024))] * len(vecs)
        + [row(D_MODEL), pl.BlockSpec((1, 8, D_MODEL), lambda b, i: (b, 0, 0)),
           _resident((8, D_MODEL)), _resident((1, D_MODEL)), _resident(w.shape)]
        + [_resident(a.shape) for a in extra],
        out_specs=out_spec,
        out_shape=jax.ShapeDtypeStruct((out_rows, D_MODEL), F32),
        compiler_params=_cparams("arbitrary", "arbitrary"),
        name=name,
    )(*acts, *vecs, x2, modl, modc, g, w, *extra)


def _in_cd_kernel(parts_per_batch, x_ref, *refs):
    modls, refs = refs[:IN_PARTS], refs[IN_PARTS:]
    modc_ref, g_ref, w_ref, cw_ref, cb_ref, dtb_ref, aneg_ref = refs[:7]
    gq_ref, gk_ref, gv_ref, gr_ref, z_ref, sx_ref, sb_ref, sc_ref, ga_ref, dts_ref, dts_t_ref, h_scr = refs[7:]
    lat = [_part_is_latent(p, parts_per_batch) for p in range(IN_PARTS)]
    for p in range(IN_PARTS):
        rows = slice(p * ROW_TILE, (p + 1) * ROW_TILE)
        h_scr[rows, :] = _adaln(x_ref[rows, :], g_ref[...], modls[p], modc_ref, 0,
                                jnp.logical_not(lat[p])).astype(BF16)

    def conv_silu(col0):
        return lambda y, c: _silu(_conv_parts(y, cw_ref[:, col0 + c:col0 + c + PROJ_COLS],
                                              cb_ref[:, col0 + c:col0 + c + PROJ_COLS], lat))

    _project(h_scr, w_ref, 0, gq_ref, lambda y, c: y * (G_DK ** -0.5))
    _project(h_scr, w_ref, 512, gk_ref, lambda y, c: y)
    _project(h_scr, w_ref, 1024, gv_ref, lambda y, c: y)
    _project(h_scr, w_ref, 2048, gr_ref, lambda y, c: _silu(y))
    _project(h_scr, w_ref, 3072, z_ref, lambda y, c: _silu(y))
    _project(h_scr, w_ref, 4096, sx_ref, conv_silu(0))
    _project(h_scr, w_ref, 5120, sb_ref, conv_silu(1024))
    _project(h_scr, w_ref, 5376, sc_ref, conv_silu(1280))
    ga_ref[...] = _dot(h_scr[...], w_ref[:, 5632:5760])
    dt = _softplus(_dot(h_scr[...], w_ref[:, 5760:5888]) + dtb_ref[...])
    lane = lax.broadcasted_iota(I32, dt.shape, 1)
    dts = jnp.where(lane < 2 * S_HEADS, dt, dt * aneg_ref[...])
    dts_ref[...] = dts
    dts_t_ref[...] = dts.T


def _in_cd(x2, modl, modc, g, w, cw, cb, dtb, aneg, bsz, t, ctx_len):
    n = bsz * t
    tm = IN_PARTS * ROW_TILE
    ppb = t // ROW_TILE
    row = lambda c: pl.BlockSpec((tm, c), lambda j: (j, 0))
    mod_specs = [pl.BlockSpec((1, 8, D_MODEL), lambda j, p=p: ((j * IN_PARTS + p) // ppb, 0, 0))
                 for p in range(IN_PARTS)]
    outs = [(512, BF16), (512, BF16), (1024, BF16), (1024, BF16), (1024, BF16), (1024, BF16), (256, BF16),
            (256, BF16), (LANES, F32), (LANES, F32)]
    params = (modc, g, w, cw, cb, dtb, aneg)
    return pl.pallas_call(
        functools.partial(_in_cd_kernel, ppb),
        grid=(n // tm,),
        in_specs=[row(D_MODEL)] + mod_specs + [_resident(a.shape) for a in params],
        out_specs=[row(c) for c, _ in outs] + [pl.BlockSpec((LANES, tm), lambda j: (0, j))],
        out_shape=[jax.ShapeDtypeStruct((n, c), dt) for c, dt in outs] + [jax.ShapeDtypeStruct((LANES, n), F32)],
        scratch_shapes=[pltpu.VMEM((tm, D_MODEL), BF16)],
        compiler_params=_cparams("arbitrary"),
        name="in_proj_cd",
    )(x2, *([modl] * IN_PARTS), *params)


def _gla_kernel(L, C, qf, kf, vf, gaf, qb, kb, vb, gab, awh_ref, awl_ref, ab_ref, of, ob, s_scr, att_scr):
    _zero_at_start(s_scr, att_scr)
    nb = L // C
    chains = []
    for d, (q_ref, k_ref, v_ref, ga_ref, out_ref) in enumerate(((qf, kf, vf, gaf, of), (qb, kb, vb, gab, ob))):
        reverse = d == 1
        mask = _tri_mask(L, reverse)
        g1, g2, _ = _split3(ga_ref[...])
        pre = _dot(g1, awh_ref[d]) + _dot(g2, awh_ref[d]) + _dot(g1, awl_ref[d])
        lg = _log_sigmoid(pre + ab_ref[d]) * (1.0 / G_TAU)
        ball = _sel_cols(mask.astype(BF16), lg)
        for h in range(G_HEADS):
            ks = slice(h * G_DK, (h + 1) * G_DK)
            b = ball[:, ks] * LOG2E
            qh = q_ref[:, ks].astype(F32)
            inter = _dot_nt((qh * jnp.exp2(b)).astype(BF16), s_scr[d, h].astype(BF16))
            chains.append(dict(d=d, h=h, reverse=reverse, mask=mask, b=b, qh=qh, kh=k_ref[:, ks].astype(F32),
                               vh=v_ref[:, h * G_DV:(h + 1) * G_DV], inter=inter, out_ref=out_ref, blocks=[]))
    T = min(G_GROUP, L)
    for i in range(nb):
        lo, hi = i * C, (i + 1) * C
        t0 = lo // T * T
        for c, ch in enumerate(chains):
            b = ch["b"]
            if ch["reverse"]:
                bref = b[hi:hi + 1, :] if i < nb - 1 else jnp.zeros((1, G_DK), F32)
            else:
                bref = b[lo - 1:lo, :] if i > 0 else jnp.zeros((1, G_DK), F32)
            qi = (ch["qh"][lo:hi, :] * jnp.exp2(b[lo:hi, :] - bref)).astype(BF16)
            ki = (ch["kh"][t0:t0 + T] * jnp.exp2(bref - b[t0:t0 + T])).astype(BF16)
            att_scr[c, lo:hi, t0:t0 + T] = _dot_nt(qi, ki)
    for g in range(L // T):
        t0, t1 = g * T, (g + 1) * T
        for c, ch in enumerate(chains):
            b = ch["b"]
            e0, e1 = (t1, L) if ch["reverse"] else (0, t0)
            if e1 > e0:
                bref = b[t1:t1 + 1, :] if ch["reverse"] else b[t0 - 1:t0, :]
                qg = (ch["qh"][t0:t1] * jnp.exp2(b[t0:t1] - bref)).astype(BF16)
                kg = (ch["kh"][e0:e1] * jnp.exp2(bref - b[e0:e1])).astype(BF16)
                att_scr[c, t0:t1, e0:e1] = _dot_nt(qg, kg)
    for c, ch in enumerate(chains):
        att = jnp.where(ch["mask"], att_scr[c], 0.0).astype(BF16)
        oh = ch["inter"] + _dot(att, ch["vh"])
        ch["out_ref"][:, ch["h"] * G_DV:(ch["h"] + 1) * G_DV] = oh.astype(BF16)
    for ch in chains:
        d, h, b = ch["d"], ch["h"], ch["b"]
        last = 0 if ch["reverse"] else L - 1
        bl = b[last:last + 1, :]
        kd = (ch["kh"] * jnp.exp2(bl - b)).astype(BF16)
        s_scr[d, h] = s_scr[d, h] * jnp.exp2(bl) + _dot_tn(ch["vh"], kd)


def _gla(q, k, v, ga, awh, awl, ab, bsz, t, ctx_len):
    L, C = G_CHUNK, G_SUB
    scratch = [pltpu.VMEM((2, G_HEADS, G_DV, G_DK), F32), pltpu.VMEM((2 * G_HEADS, L, L), F32)]
    return _bidir_scan(functools.partial(_gla_kernel, L, C), L, [q, k, v, ga], [], [awh, awl, ab], [1024], scratch,
                       bsz, t, ctx_len, "gla_scan")


def _ssd_kernel(L, xf, bf, cf, gcf, grf, xb, bb, cb_, gcb, grb, of, ob, s_scr):
    _zero_at_start(s_scr)
    er = lax.broadcasted_iota(I32, (LANES, S_HEADS * S_P), 0)
    ec = jnp.right_shift(lax.broadcasted_iota(I32, (LANES, S_HEADS * S_P), 1), S_P.bit_length() - 1)
    glane = lax.broadcasted_iota(I32, (L, LANES), 1)
    lane = lax.broadcasted_iota(I32, (L, 2 * S_P), 1)
    dirs, chains = [], []
    for d, (x_ref, b_ref, c_ref, gc_ref, gr_ref, out_ref) in enumerate(((xf, bf, cf, gcf, grf, of),
                                                                         (xb, bb, cb_, gcb, grb, ob))):
        reverse = d == 1
        mask = _tri_mask(L, reverse)
        mask01 = mask.astype(BF16)
        gc = gc_ref[...]
        gr = gr_ref[...]
        bcol = _sel_cols(mask01, gc)
        brow = _sel_rows(gr, mask01)
        last = 0 if reverse else L - 1
        dt_o, la_o = S_HEADS * d, 2 * S_HEADS + S_HEADS * d
        sel_la = (er == ec + la_o).astype(BF16)
        x = x_ref[...]
        bla = jnp.where((glane >= la_o) & (glane < la_o + S_HEADS), bcol, 0.0)
        dt_at_la = pltpu.roll(gc, 2 * S_HEADS, 1)
        bl = bla[last:last + 1, :]
        ebx = _dot(jnp.exp(bla).astype(BF16), sel_la)
        wx = _dot((jnp.exp(bl - bla) * dt_at_la).astype(BF16), sel_la)
        dirs.append(dict(d=d, mask=mask, gr=gr, x=x, ebx=ebx, last=last, dt_o=dt_o, la_o=la_o, out_ref=out_ref,
                         xw=(x.astype(F32) * wx).astype(BF16),
                         bcol2=bcol * LOG2E, brow2=brow * LOG2E))
        for g in range(S_GROUPS):
            ns = slice(g * S_N, (g + 1) * S_N)
            chains.append(dict(dr=dirs[-1], g=g, bg=b_ref[:, ns], cg=c_ref[:, ns],
                               gs=slice(g * S_HPG * S_P, (g + 1) * S_HPG * S_P)))
    for ch in chains:
        dr = ch["dr"]
        ch["cb"] = _dot_nt(ch["cg"], ch["bg"])
        ch["inter"] = _dot(ch["cg"], s_scr[dr["d"], ch["g"]].astype(BF16)) * dr["ebx"][:, ch["gs"]]
    for ch in chains:
        dr, g = ch["dr"], ch["g"]
        parts = []
        for pr in range(S_HPG // 2):
            e0 = g * S_HPG + 2 * pr
            xp = dr["x"][:, e0 * S_P:(e0 + 2) * S_P]
            res = []
            for e in (e0, e0 + 1):
                col = dr["la_o"] + e
                bc, br = dr["bcol2"][:, col:col + 1], dr["brow2"][col:col + 1, :]
                dtr = dr["gr"][dr["dt_o"] + e:dr["dt_o"] + e + 1, :]
                att = (ch["cb"] * dtr * jnp.exp2(jnp.where(dr["mask"], bc - br, -jnp.inf))).astype(BF16)
                res.append(_dot(att, xp))
            parts.append(jnp.where(lane < S_P, res[0], res[1]))
        dr["out_ref"][:, ch["gs"]] = (jnp.concatenate(parts, axis=1) + ch["inter"]).astype(BF16)
    for ch in chains:
        dr, g, gs = ch["dr"], ch["g"], ch["gs"]
        s_scr[dr["d"], g] = (s_scr[dr["d"], g] * dr["ebx"][dr["last"]:dr["last"] + 1, gs]
                             + _dot_tn(ch["bg"], dr["xw"][:, gs]))


def _ssd(x, bm, cm, gcol, grow, bsz, t, ctx_len):
    L = S_CHUNK
    scratch = [pltpu.VMEM((2, S_GROUPS, S_N, S_HPG * S_P), F32)]
    return _bidir_scan(functools.partial(_ssd_kernel, L), L, [x, bm, cm, gcol], [(grow, 4 * S_HEADS)], [], [1024],
                       scratch,
                       bsz, t, ctx_len, "ssd_scan")


def _router_kernel(x_ref, modl_ref, g_ref, rw_ref, rb_ref, h_ref, idx_ref, gate_ref, cnt_ref, base_scr):
    @pl.when((pl.program_id(0) == 0) & (pl.program_id(1) == 0))
    def _():
        base_scr[...] = jnp.zeros_like(base_scr)

    h = _rms(x_ref[...], g_ref[...]) * (1.0 + modl_ref[0, 4:5, :]) + modl_ref[0, 3:4, :]
    h_ref[...] = _pack_rows(h)
    h1, h2, _ = _split3(h)
    both = _dot(h1, rw_ref[...])
    logits = both[:, :LANES] + both[:, LANES:] + _dot(h2, rw_ref[:, :LANES]) + rb_ref[...]
    lane = lax.broadcasted_iota(I32, logits.shape, 1)
    logits = jnp.where(lane < N_EXPERTS, logits, -jnp.inf)
    lanef = lane.astype(F32)
    m1 = jnp.max(logits, axis=1, keepdims=True)
    i1 = jnp.min(jnp.where(logits == m1, lanef, float(LANES)), axis=1, keepdims=True)
    rest = jnp.where(lanef == i1, -jnp.inf, logits)
    m2 = jnp.max(rest, axis=1, keepdims=True)
    i2 = jnp.min(jnp.where(rest == m2, lanef, float(LANES)), axis=1, keepdims=True)
    e = jnp.exp(m2 - m1)
    g1 = 1.0 / (1.0 + e)
    gate_ref[...] = jnp.where(lane == 0, g1, jnp.where(lane == 1, e * g1, 0.0))
    tm = logits.shape[0]
    oh1, oh2 = (lanef == i1).astype(F32), (lanef == i2).astype(F32)
    earlier = (lax.broadcasted_iota(I32, (tm, tm), 1) < lax.broadcasted_iota(I32, (tm, tm), 0)).astype(BF16)
    seen = base_scr[0:1, :] + _dot(earlier, (oh1 + oh2).astype(BF16))
    r1 = jnp.sum(oh1 * seen, axis=1, keepdims=True)
    r2 = jnp.sum(oh2 * seen, axis=1, keepdims=True)
    base_scr[0:1, :] = base_scr[0:1, :] + jnp.sum(oh1 + oh2, axis=0, keepdims=True)
    cnt_ref[...] = base_scr[...]
    idx_ref[...] = jnp.where(lane == 0, i1, jnp.where(lane == 1, i2, jnp.where(lane == 2, r1, jnp.where(
        lane == 3, r2, 0.0)))).astype(I32)


def _latent_rows(tm, t, ctx_len, cols=D_MODEL):
    return pl.BlockSpec((pl.Element(tm), pl.Element(cols)),
                        lambda b, i: (pl.multiple_of(b * t + ctx_len + i * tm, SUBLANES), 0))


def _router(x2, modl, g, rw, rb, bsz, t, ctx_len):
    tm = LATENT_TILE
    nl = bsz * (t - ctx_len)
    nlt = (t - ctx_len) // tm
    orow = lambda c: pl.BlockSpec((tm, c), lambda b, i: (b * nlt + i, 0))
    return pl.pallas_call(
        _router_kernel,
        grid=(bsz, nlt),
        in_specs=[_latent_rows(tm, t, ctx_len),
                  pl.BlockSpec((1, 8, D_MODEL), lambda b, i: (b, 0, 0)),
                  _resident((1, D_MODEL)), _resident(rw.shape), _resident(rb.shape)],
        out_specs=[orow(D_MODEL // 2), orow(LANES), orow(LANES), pl.BlockSpec((8, LANES), lambda b, i: (0, 0))],
        out_shape=[jax.ShapeDtypeStruct((nl, D_MODEL // 2), jnp.uint32), jax.ShapeDtypeStruct((nl, LANES), I32),
                   jax.ShapeDtypeStruct((nl, LANES), F32), jax.ShapeDtypeStruct((8, LANES), F32)],
        scratch_shapes=[pltpu.VMEM((8, LANES), F32)],
        compiler_params=_cparams("arbitrary", "arbitrary"),
        name="moe_router",
    )(x2, modl, g, rw, rb)


def _gather_rows(table, idx):
    n_rows, d = idx.shape[0], table.shape[1]
    n_workers = SC_CORES * SC_SUBCORES
    per_worker = n_rows // n_workers
    chunk = SC_GATHER_ROWS
    assert per_worker * n_workers == n_rows and per_worker % chunk == 0
    mesh = plsc.VectorSubcoreMesh(core_axis_name="c", subcore_axis_name="s")

    @functools.partial(
        pl.kernel, mesh=mesh, out_type=jax.ShapeDtypeStruct((n_rows, d), table.dtype),
        scratch_types=[pltpu.VMEM((chunk,), I32), pltpu.VMEM((chunk, d), table.dtype), pltpu.SemaphoreType.DMA],
        name="sc_gather_rows")
    def gather(table_hbm, idx_hbm, out_hbm, idx_v, rows_v, sem):
        base = (lax.axis_index("s") * SC_CORES + lax.axis_index("c")) * per_worker

        @pl.loop(0, per_worker // chunk)
        def _(j):
            off = pl.multiple_of(base + j * chunk, 8)
            pltpu.sync_copy(idx_hbm.at[pl.ds(off, chunk)], idx_v)
            pltpu.async_copy(table_hbm.at[idx_v], rows_v, sem).wait()
            pltpu.sync_copy(rows_v, out_hbm.at[pl.ds(off, chunk)])

    return gather(table, idx)


def _dispatch_rows(table, slot, n_out):
    assert TOP_K == 2
    n_assign, (n_tok, d) = slot.shape[0], table.shape
    n_workers = SC_CORES * SC_SUBCORES
    per_worker = n_out // n_workers
    chunk, vec = SC_GATHER_ROWS, SC_LANES
    assert per_worker * n_workers == n_out and per_worker % chunk == 0 and n_assign % vec == 0
    assert n_out < 3 * n_tok
    mesh = plsc.VectorSubcoreMesh(core_axis_name="c", subcore_axis_name="s")

    @functools.partial(
        pl.kernel, mesh=mesh, out_type=jax.ShapeDtypeStruct((n_out, d), table.dtype),
        scratch_types=[pltpu.VMEM((n_assign,), I32), pltpu.VMEM((per_worker,), I32),
                       pltpu.VMEM((chunk, d), table.dtype), pltpu.SemaphoreType.DMA],
        compiler_params=pltpu.CompilerParams(needs_layout_passes=False),
        name="sc_dispatch_rows")
    def dispatch(table_hbm, slot_hbm, out_hbm, slots_v, tok_v, rows_v, sem):
        base = (lax.axis_index("s") * SC_CORES + lax.axis_index("c")) * per_worker
        pltpu.sync_copy(slot_hbm, slots_v)
        lane = lax.broadcasted_iota(I32, (vec,), 0)

        @pl.loop(0, per_worker // vec)
        def _(j):
            p = base + j * vec + lane
            p = jnp.where(p >= n_tok, p - n_tok, p)
            tok_v[pl.ds(j * vec, vec)] = jnp.where(p >= n_tok, p - n_tok, p)

        @pl.loop(0, n_assign // vec)
        def _(j):
            s = slots_v[pl.ds(j * vec, vec)] - base
            mine = (s >= 0) & (s < per_worker)
            tok = lax.shift_right_logical(j * vec + lane, 1)
            plsc.store_scatter(tok_v, [jnp.where(mine, s, 0)], tok, mask=mine)

        @pl.loop(0, per_worker // chunk)
        def _(j):
            off = pl.multiple_of(j * chunk, 8)
            pltpu.async_copy(table_hbm.at[tok_v.at[pl.ds(off, chunk)]], rows_v, sem).wait()
            pltpu.sync_copy(rows_v, out_hbm.at[pl.ds(base + off, chunk)])

    return dispatch(table, slot)


def _moe_kernel(be_ref, nu_ref, nv_ref, x_ref, w1g_ref, w1u_ref, w2_ref, y_ref, xb_ref, acc_ref):
    i, j = pl.program_id(0), pl.program_id(1)
    bm = x_ref.shape[0]

    def swiglu_rows(rows):
        x = xb_ref[0:rows, :]
        a = (_silu(_dot(x, w1g_ref[0].astype(BF16))) * _dot(x, w1u_ref[0].astype(BF16))).astype(BF16)
        acc_ref[0:rows, :] += _dot(a, w2_ref[0].astype(BF16))

    @pl.when(i < nu_ref[0])
    def _():
        @pl.when(j == 0)
        def _():
            acc_ref[...] = jnp.zeros_like(acc_ref)
            xb_ref[...] = _unpack_rows(x_ref[...]).astype(BF16)

        for rows in range(MOE_ROW_STEP, bm + 1, MOE_ROW_STEP):
            @pl.when((nv_ref[i] > rows - MOE_ROW_STEP) & (nv_ref[i] <= rows))
            def _(rows=rows):
                swiglu_rows(rows)

        @pl.when(j == pl.num_programs(1) - 1)
        def _():
            y_ref[...] = _pack_rows(acc_ref[...])


def _moe_experts(xs, w1, w2, block_e, n_used, n_valid, n_blocks):
    bm, fc = MOE_ROWS, MOE_FF
    nff = FF_EXPERT // fc
    used = lambda i, nu: jnp.minimum(i, nu[0] - 1)
    ffi = lambda i, j, nu: jnp.where(i < nu[0], j, nff - 1)
    grid_spec = pltpu.PrefetchScalarGridSpec(
        num_scalar_prefetch=3,
        grid=(n_blocks, nff),
        in_specs=[pl.BlockSpec((bm, D_MODEL // 2), lambda i, j, be, nu, nv: (used(i, nu), 0)),
                  pl.BlockSpec((1, D_MODEL, fc), lambda i, j, be, nu, nv: (be[used(i, nu)], 0, ffi(i, j, nu))),
                  pl.BlockSpec((1, D_MODEL, fc),
                               lambda i, j, be, nu, nv: (be[used(i, nu)], 0, nff + ffi(i, j, nu))),
                  pl.BlockSpec((1, fc, D_MODEL), lambda i, j, be, nu, nv: (be[used(i, nu)], ffi(i, j, nu), 0))],
        out_specs=pl.BlockSpec((bm, D_MODEL // 2), lambda i, j, be, nu, nv: (used(i, nu), 0)),
        scratch_shapes=[pltpu.VMEM((bm, D_MODEL), BF16), pltpu.VMEM((bm, D_MODEL), F32)],
    )
    return pl.pallas_call(
        _moe_kernel,
        grid_spec=grid_spec,
        out_shape=jax.ShapeDtypeStruct((n_blocks * bm, D_MODEL // 2), jnp.uint32),
        compiler_params=_cparams("arbitrary", "arbitrary"),
        name="moe_experts",
    )(block_e, n_used, n_valid, xs, w1, w1, w2)


def _combine_kernel(yk_ref, gate_ref, x_ref, modl_ref, g_ref, o_ref):
    gt = gate_ref[...]
    half = yk_ref.shape[1] // TOP_K
    y = gt[:, 0:1] * _unpack_rows(yk_ref[:, :half]) + gt[:, 1:2] * _unpack_rows(yk_ref[:, half:])
    o_ref[...] = x_ref[...] + modl_ref[0, 5:6, :] * _rms(y, g_ref[...])


def _combine(yk, gates, x2, modl, g, bsz, t, ctx_len):
    tm = LATENT_TILE
    nlt = (t - ctx_len) // tm
    nl = bsz * (t - ctx_len)
    orow = lambda c: pl.BlockSpec((tm, c), lambda b, i: (b * nlt + i, 0))
    return pl.pallas_call(
        _combine_kernel,
        grid=(bsz, nlt),
        in_specs=[orow(yk.shape[1]), orow(LANES),
                  _latent_rows(tm, t, ctx_len),
                  pl.BlockSpec((1, 8, D_MODEL), lambda b, i: (b, 0, 0)), _resident((1, D_MODEL))],
        out_specs=orow(D_MODEL),
        out_shape=jax.ShapeDtypeStruct((nl, D_MODEL), F32),
        compiler_params=_cparams("arbitrary", "arbitrary"),
        name="moe_combine",
    )(yk, gates, x2, modl, g)


def _routing_tables(experts, ranks, counts):
    bm = MOE_ROWS
    n_assign = experts.size
    padded = (counts + bm - 1) // bm * bm
    ends_p = jnp.cumsum(padded)
    pstarts = ends_p - padded
    onehot = experts[..., None] == jnp.arange(N_EXPERTS, dtype=I32)
    slot = (jnp.sum(jnp.where(onehot, pstarts, 0), axis=-1) + ranks).astype(I32)
    n_blocks = n_assign // bm + N_EXPERTS
    block_e = jnp.minimum(jnp.sum(ends_p[None, :] <= (jnp.arange(n_blocks, dtype=I32) * bm)[:, None], axis=1),
                          N_EXPERTS - 1).astype(I32)
    n_used = (ends_p[-1] // bm).astype(I32).reshape(1)
    sel = block_e[:, None] == jnp.arange(N_EXPERTS, dtype=I32)
    rows_left = jnp.sum(jnp.where(sel, counts + pstarts, 0), axis=1) - jnp.arange(n_blocks, dtype=I32) * bm
    n_valid = jnp.clip(rows_left, 0, bm).astype(I32)
    return slot, block_e, n_used, n_valid, n_blocks


def kernel(x, c, ctx, c_ctx, mod_w, mod_b, norm_g, ab_in_w, m_conv_w, m_conv_b, m_gate_b, m_norm_w, l_conv_w,
           l_conv_b, l_wa, l_ba, l_wx, l_bx, l_lam, ab_out_w, ffn_w1, ffn_w2, cd_in_w, g_alpha_w, g_alpha_b,
           g_norm_w, s_conv_w, s_conv_b, s_dt_bias, s_A_log, s_D, s_norm_w, cd_out_w, router_w, router_b,
           moe_w1, moe_w2):
    bsz, seq, dm = x.shape
    ctx_len = ctx.shape[1]
    t = ctx_len + seq
    n = bsz * t
    assert ctx_len == ROW_TILE and seq % ROW_TILE == 0 and ROW_TILE % GRID_W == 0
    assert n % (IN_PARTS * ROW_TILE) == 0

    c_all = jnp.concatenate([c, c_ctx[None, :], jnp.zeros((-(bsz + 1) % 8, dm), F32)], axis=0)
    mods = _modulation(c_all, mod_w, mod_b).reshape(mod_w.shape[0], c_all.shape[0], 6, dm)
    pad2 = jnp.zeros((bsz, 2, dm), F32)

    def layer_mods(layer):
        modl = jnp.concatenate([mods[layer, :bsz], pad2], axis=1)
        modc = jnp.concatenate([mods[layer, bsz], pad2[0]], axis=0)
        return modl, modc

    modl, modc = layer_mods(0)
    g = norm_g[0]
    w = ab_in_w[0]
    o_qk, o_v, o_o, o_gt, o_lx, o_lg = 0, 1024, 2048, 3072, 3088, 4112
    w_in = jnp.concatenate([w[:, o_qk:o_v], w[:, o_v:o_o], w[:, o_o:o_gt], w[:, o_lx:o_lg], w[:, o_lg:],
                            w[:, o_gt:o_lx], jnp.zeros((dm, LANES - 16), F32)], axis=1).astype(BF16)
    gb = jnp.concatenate([m_gate_b[0].reshape(1, 16), jnp.zeros((1, LANES - 16), F32)], axis=1)
    x2, q, k, v, og, lx, glg, gates, gates_t = _in_ab(
        ctx.reshape(bsz * ctx_len, dm), x.reshape(bsz * seq, dm), modl, modc, g[0:1], w_in, m_conv_w[0],
        m_conv_b[0][None], l_conv_w[0], l_conv_b[0][None], gb, bsz, t, ctx_len)
    hm_f, hm_b = _mlstm(q, k, v, gates, gates_t, bsz, t, ctx_len)
    hl_f, hl_b = _lru(lx, l_wa[0].astype(BF16), l_wx[0].astype(BF16), l_ba[0][:, None], l_bx[0][:, None],
                      l_lam[0][:, None], bsz, t, ctx_len)
    x2 = _out_proj(_out_ab_kernel, "out_proj_swiglu_ab", [hm_f, hm_b, og, hl_f, hl_b, glg], [m_norm_w[0][None]],
                   x2, modl, modc, g[1:2], ab_out_w[0].astype(BF16), bsz, t, ctx_len,
                   extra=(g[2:3], g[3:4], ffn_w1[0].astype(BF16), ffn_w2[0].astype(BF16)))

    modl, modc = layer_mods(1)
    g = norm_g[1]
    w = cd_in_w[0]
    o_gv, o_gr, o_ga, o_z, o_xbc, o_dt = 1024, 2048, 3072, 3104, 4128, 5664
    zpad = jnp.zeros((dm, LANES - 32), F32)
    w_in = jnp.concatenate([w[:, :o_gv], w[:, o_gv:o_gr], w[:, o_gr:o_ga], w[:, o_z:o_xbc], w[:, o_xbc:o_dt],
                            w[:, o_ga:o_z], zpad, w[:, o_dt:], w[:, o_dt:], jnp.zeros((dm, LANES - 64), F32)],
                           axis=1).astype(BF16)
    dtb = jnp.concatenate([s_dt_bias[0].reshape(1, 32), s_dt_bias[0].reshape(1, 32),
                           jnp.zeros((1, LANES - 64), F32)], axis=1)
    aneg = jnp.concatenate([jnp.zeros((1, 32), F32), -jnp.exp(s_A_log[0].reshape(1, 32)),
                            jnp.zeros((1, LANES - 64), F32)], axis=1)
    gq, gk, gv, grs, zs, sx, sb, sc, ga, dts, dts_t = _in_cd(x2, modl, modc, g[0:1], w_in, s_conv_w[0],
                                                             s_conv_b[0][None], dtb, aneg, bsz, t, ctx_len)
    aw = jnp.zeros((2, LANES, G_HEADS * G_DK), F32)
    aw = aw.at[0, 0:G_RANK].set(g_alpha_w[0, 0]).at[1, G_RANK:2 * G_RANK].set(g_alpha_w[0, 1])
    awh = aw.astype(BF16)
    awl = (aw - awh.astype(F32)).astype(BF16)
    og_f, og_b = _gla(gq, gk, gv, ga, awh, awl, g_alpha_b[0][:, None], bsz, t, ctx_len)
    ys_f, ys_b = _ssd(sx, sb, sc, dts, dts_t, bsz, t, ctx_len)
    dskip = jnp.repeat(s_D[0], S_P)[None]
    x2 = _out_proj(_out_cd_kernel, "out_proj_cd", [og_f, og_b, grs, ys_f, ys_b, sx, zs],
                   [g_norm_w[0][None], dskip, s_norm_w[0][None]],
                   x2, modl, modc, g[1:2], cd_out_w[0].astype(BF16), bsz, t, ctx_len, latent_only=True)
    t, ctx_len = seq, 0

    rw = jnp.concatenate([router_w[0], jnp.zeros((dm, LANES - N_EXPERTS), F32)], axis=1)
    rwh = rw.astype(BF16)
    rw = jnp.concatenate([rwh, (rw - rwh.astype(F32)).astype(BF16)], axis=1)
    rb = jnp.concatenate([router_b[0][None], jnp.zeros((1, LANES - N_EXPERTS), F32)], axis=1)
    h2, top_idx, top_gate, counts = _router(x2, modl, g[2:3], rw, rb, bsz, t, ctx_len)
    slot, block_e, n_used, n_valid, n_blocks = _routing_tables(top_idx[:, 0:TOP_K], top_idx[:, TOP_K:2 * TOP_K],
                                                               counts[0, :N_EXPERTS].astype(I32))
    xs = _dispatch_rows(h2, slot.reshape(-1), n_blocks * MOE_ROWS)
    ys_e = _moe_experts(xs, moe_w1[0], moe_w2[0], block_e, n_used, n_valid, n_blocks)
    yk = _gather_rows(ys_e, slot.reshape(-1)).reshape(slot.shape[0], TOP_K * (dm // 2))
    out = _combine(yk, top_gate, x2, modl, g[3:4], bsz, t, ctx_len)
    return out.reshape(bsz, seq, dm)
```

```python
import functools

import jax
import jax.numpy as jnp
from jax import lax
from jax.experimental import pallas as pl
from jax.experimental.pallas import tpu as pltpu
from jax.experimental.pallas import tpu_sc as plsc

F32, BF16, I32 = jnp.float32, jnp.bfloat16, jnp.int32

D_MODEL = 1024
GRID_W = 64
EPS = 1e-6
LOG2E = 1.4426950408889634
M_HEADS, M_DK, M_DV = 4, 128, 256
LRU_BLOCKS, LRU_BW, LRU_C = 8, 128, 8.0
G_HEADS, G_DK, G_DV, G_RANK, G_TAU = 4, 128, 256, 16, 16.0
S_HEADS, S_P, S_N, S_GROUPS, S_HPG = 16, 64, 128, 2, 8
N_EXPERTS, TOP_K, FF_EXPERT = 8, 2, 3584

LANES, SUBLANES = 128, 8
VMEM_LIMIT = 56 * 1024 * 1024
ROW_TILE = 256
IN_PARTS = 2
LATENT_TILE = 512
BIG_ROW_TILE_MAX = 640
PROJ_COLS = 256
M_CHUNK = 256
G_CHUNK, G_SUB = 256, 16
G_GROUP = 64
S_CHUNK = 256
LRU_TILE = 256
FF_STEP = 256
MOE_ROWS = 1024
MOE_FF = 512
MOE_ROW_STEP = 256
SC_CORES, SC_SUBCORES = 2, 16
SC_LANES = 16
SC_GATHER_ROWS = 128


def _cparams(*sem):
    return pltpu.CompilerParams(dimension_semantics=sem, vmem_limit_bytes=VMEM_LIMIT)


def _sigmoid(x):
    return 0.5 + 0.5 * jnp.tanh(0.5 * x)


def _silu(x):
    return x * _sigmoid(x)


def _softplus(x):
    return jnp.maximum(x, 0.0) + jnp.log1p(jnp.exp(-jnp.abs(x)))


def _log_sigmoid(x):
    return jnp.minimum(x, 0.0) - jnp.log1p(jnp.exp(-jnp.abs(x)))


def _dot(a, b):
    return jnp.dot(a, b, preferred_element_type=F32)


def _dot_nt(a, b):
    return lax.dot_general(a, b, (((1,), (1,)), ((), ())), preferred_element_type=F32)


def _dot_tn(a, b):
    return lax.dot_general(a, b, (((0,), (0,)), ((), ())), preferred_element_type=F32)


def _split3(f):
    f1 = f.astype(BF16)
    r = f - f1.astype(F32)
    f2 = r.astype(BF16)
    f3 = (r - f2.astype(F32)).astype(BF16)
    return f1, f2, f3


def _sel_cols(mask01, f):
    p1, p2, p3 = _split3(f)
    return _dot(mask01, p1) + _dot(mask01, p2) + _dot(mask01, p3)


def _sel_rows(f, mask01):
    p1, p2, p3 = _split3(f)
    return _dot_nt(p1, mask01) + _dot_nt(p2, mask01) + _dot_nt(p3, mask01)


def _pack_rows(u):
    n = u.shape[1] // 2
    bits = lax.bitcast_convert_type(u.astype(BF16).astype(F32), jnp.uint32)
    return jnp.bitwise_or(bits[:, :n], jnp.right_shift(bits[:, n:], jnp.uint32(16)))


def _unpack_rows(p):
    hi = lax.bitcast_convert_type(jnp.bitwise_and(p, jnp.uint32(0xFFFF0000)), F32)
    lo = lax.bitcast_convert_type(jnp.left_shift(p, jnp.uint32(16)), F32)
    return jnp.concatenate([hi, lo], axis=1)


def _rms(u, g):
    return u * lax.rsqrt(jnp.mean(u * u, axis=-1, keepdims=True) + EPS) * g


def _head_rms(u, g, n_heads):
    w = u.shape[-1] // n_heads
    return jnp.concatenate([_rms(u[:, h * w:(h + 1) * w], g[:, h * w:(h + 1) * w]) for h in range(n_heads)], axis=1)


def _row_ids(tile_idx, tm):
    return tile_idx * tm + lax.broadcasted_iota(I32, (tm, 1), 0)


def _mod(modl_ref, modc_ref, idx, is_ctx):
    return jnp.where(is_ctx, modc_ref[idx:idx + 1, :], modl_ref[0, idx:idx + 1, :])


def _adaln(x, g, modl_ref, modc_ref, shift_idx, is_ctx):
    shift = _mod(modl_ref, modc_ref, shift_idx, is_ctx)
    scale = _mod(modl_ref, modc_ref, shift_idx + 1, is_ctx)
    return _rms(x, g) * (1.0 + scale) + shift


def _neighbour_tiles(r, step, is_lat):
    n_tiles, per_seg = r.shape[0], GRID_W // SUBLANES
    zero = jnp.zeros((1,) + r.shape[1:], r.dtype)

    def across(src):
        return jnp.where(is_lat, 0.0, r[src:src + 1]) if 0 <= src < n_tiles else zero

    pieces = []
    for s0 in range(0, n_tiles, per_seg):
        if step > 0:
            pieces += [across(s0 - 1), r[s0:s0 + per_seg - 1]]
        else:
            pieces += [r[s0 + 1:s0 + per_seg], across(s0 + per_seg)]
    return jnp.concatenate(pieces, axis=0)


def _dwconv(y, cw, cb, is_lat):
    tm, n = y.shape
    y3 = y.reshape(tm // SUBLANES, SUBLANES, n)
    sub = lax.broadcasted_iota(I32, (1, SUBLANES, 1), 1)
    w = lambda j: cw[j:j + 1, :].reshape(1, 1, n)
    r1, r2, r7 = pltpu.roll(y3, 1, 1), pltpu.roll(y3, 2, 1), pltpu.roll(y3, SUBLANES - 1, 1)
    ym1 = jnp.where(sub >= 1, r1, _neighbour_tiles(r1, 1, is_lat))
    ym2 = jnp.where(sub >= 2, r2, _neighbour_tiles(r2, 1, is_lat))
    yp1 = jnp.where(sub <= SUBLANES - 2, r7, _neighbour_tiles(r7, -1, is_lat))
    out = cb.reshape(1, 1, n) + w(0) * ym2 + w(1) * ym1 + w(2) * y3 + w(3) * yp1
    return out.reshape(tm, n)


def _resident(shape):
    nd = len(shape)
    return pl.BlockSpec(shape, lambda *_: (0,) * nd, pipeline_mode=pl.Buffered(1))


def _big_row_tile(t):
    return max(tm for tm in range(8, BIG_ROW_TILE_MAX + 1, 8) if t % tm == 0)


def _mod_kernel(c_ref, w_ref, b_ref, o_ref):
    a1, a2, _ = _split3(_silu(c_ref[...]))
    w = w_ref[0]
    wh = w.astype(BF16)
    wl = (w - wh.astype(F32)).astype(BF16)
    o_ref[0] = _dot(a1, wh) + _dot(a2, wh) + _dot(a1, wl) + b_ref[0]


def _modulation(c_all, mod_w, mod_b):
    depth, d, n6 = mod_w.shape
    rows = c_all.shape[0]
    tn = 1024
    return pl.pallas_call(
        _mod_kernel,
        grid=(depth, n6 // tn),
        in_specs=[pl.BlockSpec((rows, d), lambda l, j: (0, 0)),
                  pl.BlockSpec((1, d, tn), lambda l, j: (l, 0, j)),
                  pl.BlockSpec((1, 1, tn), lambda l, j: (l, 0, j))],
        out_specs=pl.BlockSpec((1, rows, tn), lambda l, j: (l, 0, j)),
        out_shape=jax.ShapeDtypeStruct((depth, rows, n6), F32),
        compiler_params=_cparams("arbitrary", "arbitrary"),
        name="modulation",
    )(c_all, mod_w, mod_b.reshape(depth, 1, n6))


def _project(h_ref, w_ref, w_col, out_ref, epilogue):
    for c in range(0, out_ref.shape[1], PROJ_COLS):
        y = _dot(h_ref[...], w_ref[:, w_col + c:w_col + c + PROJ_COLS])
        out_ref[:, c:c + PROJ_COLS] = epilogue(y, c).astype(out_ref.dtype)


def _part_is_latent(part, parts_per_batch):
    return lax.rem(pl.program_id(0) * IN_PARTS + part, parts_per_batch) != 0


def _conv_parts(y, cw, cb, lat):
    return jnp.concatenate([_dwconv(y[p * ROW_TILE:(p + 1) * ROW_TILE], cw, cb, lat[p]) for p in range(IN_PARTS)],
                           axis=0)


def _in_ab_kernel(parts_per_batch, *refs):
    xs, refs = refs[:3 * IN_PARTS], refs[3 * IN_PARTS:]
    modc_ref, g_ref, w_ref, mcw_ref, mcb_ref, lcw_ref, lcb_ref, gb_ref = refs[:8]
    x_ref, q_ref, k_ref, v_ref, og_ref, lx_ref, glg_ref, gates_ref, gates_t_ref, h_scr = refs[8:]
    lat = [_part_is_latent(p, parts_per_batch) for p in range(IN_PARTS)]
    for p in range(IN_PARTS):
        xc_ref, xl_ref, modl_ref = xs[3 * p:3 * p + 3]
        rows = slice(p * ROW_TILE, (p + 1) * ROW_TILE)
        xp = jnp.where(lat[p], xl_ref[...], xc_ref[...])
        x_ref[rows, :] = xp
        h_scr[rows, :] = _adaln(xp, g_ref[...], modl_ref, modc_ref, 0, jnp.logical_not(lat[p])).astype(BF16)

    def conv(cw_ref, cb_ref):
        return lambda y, c: _conv_parts(y, cw_ref[:, c:c + PROJ_COLS], cb_ref[:, c:c + PROJ_COLS], lat)

    qc, kc, lc = conv(mcw_ref, mcb_ref), conv(mcw_ref, mcb_ref), conv(lcw_ref, lcb_ref)
    _project(h_scr, w_ref, 0, q_ref, lambda y, c: _silu(qc(y, c)))
    _project(h_scr, w_ref, 512, k_ref, lambda y, c: _silu(kc(y, 512 + c)) * (M_DK ** -0.5))
    _project(h_scr, w_ref, 1024, v_ref, lambda y, c: y)
    _project(h_scr, w_ref, 2048, og_ref, lambda y, c: _sigmoid(y))
    _project(h_scr, w_ref, 3072, lx_ref, lc)
    _project(h_scr, w_ref, 4096, glg_ref, lambda y, c: jax.nn.gelu(y))
    gt = _dot(h_scr[...], w_ref[:, 5120:5248]) + gb_ref[...]
    lane = lax.broadcasted_iota(I32, gt.shape, 1)
    gates = jnp.where(lane >= 2 * M_HEADS, _log_sigmoid(gt), gt)
    gates_ref[...] = gates
    gates_t_ref[...] = gates.T


def _in_ab(xc2, xl2, modl, modc, g, w, mcw, mcb, lcw, lcb, gb, bsz, t, ctx_len):
    n = bsz * t
    tm = IN_PARTS * ROW_TILE
    ppb = t // ROW_TILE
    lpb = ppb - 1
    part = lambda j, p: j * IN_PARTS + p
    batch = lambda j, p: part(j, p) // ppb
    x_specs, x_args = [], []
    for p in range(IN_PARTS):
        x_specs += [pl.BlockSpec((ROW_TILE, D_MODEL), lambda j, p=p: (batch(j, p), 0)),
                    pl.BlockSpec((ROW_TILE, D_MODEL),
                                 lambda j, p=p: (batch(j, p) * lpb + jnp.maximum(part(j, p) % ppb - 1, 0), 0)),
                    pl.BlockSpec((1, 8, D_MODEL), lambda j, p=p: (batch(j, p), 0, 0))]
        x_args += [xc2, xl2, modl]
    row = lambda c: pl.BlockSpec((tm, c), lambda j: (j, 0))
    outs = [(D_MODEL, F32), (512, BF16), (512, BF16), (1024, BF16), (1024, BF16), (1024, BF16), (1024, BF16),
            (LANES, F32)]
    params = (modc, g, w, mcw, mcb, lcw, lcb, gb)
    return pl.pallas_call(
        functools.partial(_in_ab_kernel, ppb),
        grid=(n // tm,),
        in_specs=x_specs + [_resident(a.shape) for a in params],
        out_specs=[row(c) for c, _ in outs] + [pl.BlockSpec((LANES, tm), lambda j: (0, j))],
        out_shape=[jax.ShapeDtypeStruct((n, c), dt) for c, dt in outs] + [jax.ShapeDtypeStruct((LANES, n), F32)],
        scratch_shapes=[pltpu.VMEM((tm, D_MODEL), BF16)],
        compiler_params=_cparams("arbitrary"),
        name="in_proj_ab",
    )(*x_args, *params)


def _reverse_order(n_ctx_chunks, n_chunks):
    return lambda c: jnp.where(c < n_ctx_chunks, n_ctx_chunks - 1 - c, n_chunks - 1 - (c - n_ctx_chunks))


def _tri_mask(L, reverse):
    row = lax.broadcasted_iota(I32, (L, L), 0)
    col = lax.broadcasted_iota(I32, (L, L), 1)
    return (col >= row) if reverse else (col <= row)


def _bidir_scan(kern, L, row_ins, lane_ins, res_ins, out_cols, scratch, bsz, t, ctx_len, name):
    nc = t // L
    rev = _reverse_order(ctx_len // L, nc)
    fwd = lambda i: i
    n = bsz * t

    def row(c, order):
        return pl.BlockSpec((L, c), lambda b, i: (b * nc + order(i), 0))

    def lanes(r, order):
        return pl.BlockSpec((r, L), lambda b, i: (0, b * nc + order(i)))

    in_specs, args = [], []
    for order in (fwd, rev):
        in_specs += [row(a.shape[1], order) for a in row_ins] + [lanes(r, order) for _, r in lane_ins]
        args += list(row_ins) + [a for a, _ in lane_ins]
    in_specs += [_resident(a.shape) for a in res_ins]
    args += list(res_ins)
    return pl.pallas_call(
        kern,
        grid=(bsz, nc),
        in_specs=in_specs,
        out_specs=[row(c, fwd) for c in out_cols] + [row(c, rev) for c in out_cols],
        out_shape=[jax.ShapeDtypeStruct((n, c), BF16) for c in out_cols] * 2,
        scratch_shapes=scratch,
        compiler_params=_cparams("arbitrary", "arbitrary"),
        name=name,
    )(*args)


def _zero_at_start(*scratch):
    @pl.when(pl.program_id(1) == 0)
    def _():
        for s in scratch:
            s[...] = jnp.zeros_like(s)


def _mlstm_kernel(L, qf, kf, vf, gcf, grf, qb, kb, vb, gcb, grb, of, ob, c_scr, m_scr):
    _zero_at_start(c_scr, m_scr)
    ones = jnp.ones((L, LANES), BF16)
    wide = lambda c, k: jnp.concatenate([c] * k, axis=1)
    er = lax.broadcasted_iota(I32, (LANES, 2 * M_HEADS * LANES), 0)
    ej = jnp.right_shift(lax.broadcasted_iota(I32, (LANES, 2 * M_HEADS * LANES), 1), LANES.bit_length() - 1)
    chains = []
    for d, (q_ref, k_ref, v_ref, gc_ref, gr_ref, out_ref) in enumerate(((qf, kf, vf, gcf, grf, of),
                                                                         (qb, kb, vb, gcb, grb, ob))):
        reverse = d == 1
        mask = _tri_mask(L, reverse)
        mask01 = mask.astype(BF16)
        gc = gc_ref[...]
        gr = gr_ref[...]
        bcol = _sel_cols(mask01, gc)
        brow = _sel_rows(gr, mask01)
        glane = lax.broadcasted_iota(I32, (L, LANES), 1)
        src = jnp.where(ej < M_HEADS, M_HEADS * d + ej, M_HEADS + M_HEADS * d + ej)
        s1, s2, _ = _split3(jnp.where(glane < 2 * M_HEADS, gc, bcol))
        sel = (er == src).astype(BF16)
        rep = _dot(s1, sel) + _dot(s2, sel)
        last = 0 if reverse else L - 1
        for h in range(M_HEADS):
            fo, io = 2 * M_HEADS + M_HEADS * d + h, M_HEADS * d + h
            ic, bc = rep[:, h * LANES:(h + 1) * LANES], rep[:, (M_HEADS + h) * LANES:(M_HEADS + h + 1) * LANES]
            r1 = gr[io:io + 1, :] - brow[fo:fo + 1, :]
            m = m_scr[d, h:h + 1, :]
            g = bc + m
            mx = jnp.max(jnp.where(mask, r1, -jnp.inf), axis=1, keepdims=True)
            mt = jnp.maximum(g, bc + mx)
            p = jnp.exp(jnp.where(mask, wide(bc - mt, L // LANES) + r1, -jnp.inf))
            bl = bc[last:last + 1, :]
            a_s = bl - bc + ic
            m_new = jnp.maximum(bl + m, jnp.max(a_s, axis=0, keepdims=True))
            chains.append(dict(
                d=d, h=h, out_ref=out_ref, mt=mt, w_inter=jnp.exp(g - mt), p=p,
                m_new=m_new, decay=jnp.exp(bl + m - m_new), ws=jnp.exp(a_s - m_new),
                qh=q_ref[:, h * M_DK:(h + 1) * M_DK], kh=k_ref[:, h * M_DK:(h + 1) * M_DK],
                vx=jnp.concatenate([v_ref[:, h * M_DV:(h + 1) * M_DV], ones], axis=1)))
    for ch in chains:
        ch["sc"] = (_dot_nt(ch["qh"], ch["kh"]) * ch["p"]).astype(BF16)
    for ch in chains:
        cx = c_scr[ch["d"], ch["h"]]
        numx = wide(ch["w_inter"], 1 + M_DV // LANES) * _dot(ch["qh"], cx.astype(BF16)) + _dot(ch["sc"], ch["vx"])
        den = numx[:, M_DV:]
        rinv = 1.0 / jnp.maximum(jnp.abs(den), jnp.exp(-ch["mt"]))
        hh = numx[:, 0:M_DV] * wide(rinv, M_DV // LANES)
        ch["out_ref"][:, ch["h"] * M_DV:(ch["h"] + 1) * M_DV] = hh.astype(BF16)
    for ch in chains:
        d, h = ch["d"], ch["h"]
        kw = (ch["kh"].astype(F32) * ch["ws"]).astype(BF16)
        c_scr[d, h] = wide(ch["decay"], 1 + M_DV // LANES) * c_scr[d, h] + _dot_tn(kw, ch["vx"])
        m_scr[d, h:h + 1, :] = ch["m_new"]


def _mlstm(q, k, v, gcol, grow, bsz, t, ctx_len):
    L = M_CHUNK
    scratch = [pltpu.VMEM((2, M_HEADS, M_DK, M_DV + LANES), F32), pltpu.VMEM((2, 8, LANES), F32)]
    return _bidir_scan(functools.partial(_mlstm_kernel, L), L, [q, k, v, gcol], [(grow, 16)], [], [1024], scratch,
                       bsz, t, ctx_len, "mlstm_scan")


def _lru_kernel(L, uf, ub_, wa_ref, wx_ref, ba_ref, bx_ref, lam_ref, of, ob, a_scr, b_scr, h_scr):
    _zero_at_start(h_scr)
    R, C = L // SUBLANES, LRU_BLOCKS * LRU_BW
    i0, i1 = lax.broadcasted_iota(I32, (L, L), 0), lax.broadcasted_iota(I32, (L, L), 1)
    when = lambda p: jnp.bitwise_and(p, SUBLANES - 1) * R + jnp.right_shift(p, SUBLANES.bit_length() - 1)
    regroup = (i1 == when(i0)).astype(BF16)
    ungroup = (i0 == when(i1)).astype(BF16)
    sub = lax.broadcasted_iota(I32, (SUBLANES, 1), 0)
    dirs = ((uf, of), (ub_, ob))
    for d, (u_ref, _) in enumerate(dirs):
        u = _dot(regroup, u_ref[...])
        ub = u.astype(BF16)
        sp = -LRU_C * _softplus(-lam_ref[d])
        for n in range(LRU_BLOCKS):
            sl = slice(n * LRU_BW, (n + 1) * LRU_BW)
            rg = _sigmoid(_dot(ub[:, sl], wa_ref[d, n]) + ba_ref[d, :, sl])
            ig = _sigmoid(_dot(ub[:, sl], wx_ref[d, n]) + bx_ref[d, :, sl])
            log_a = rg * sp[:, sl]
            a = jnp.exp(log_a)
            a_scr[d, :, sl] = a
            b_scr[d, :, sl] = jnp.sqrt(-jnp.tanh(log_a) * (a * a + 1.0)) * (ig * u[:, sl])
    S = SUBLANES
    hs = [jnp.zeros((S, C), F32), jnp.zeros((S, C), F32)]
    ps = [jnp.ones((S, C), F32), jnp.ones((S, C), F32)]
    for j in range(R):
        for d in range(2):
            rows = slice(S * j, S * j + S) if d == 0 else slice(S * (R - 1 - j), S * (R - j))
            a = a_scr[d, rows, :]
            hs[d] = a * hs[d] + b_scr[d, rows, :]
            ps[d] = a * ps[d]
            b_scr[d, rows, :] = hs[d]
            a_scr[d, rows, :] = ps[d]
    for d, (_, out_ref) in enumerate(dirs):
        c = h_scr[d:d + 1, :]
        cin = jnp.zeros((S, C), F32)
        for s in (range(S) if d == 0 else reversed(range(S))):
            cin = jnp.where(sub == s, c, cin)
            c = ps[d][s:s + 1, :] * c + hs[d][s:s + 1, :]
        h_scr[d:d + 1, :] = c
        h = b_scr[d].reshape(R, S, C) + a_scr[d].reshape(R, S, C) * cin[None]
        out_ref[...] = _dot(ungroup, h.reshape(L, C).astype(BF16)).astype(BF16)


def _lru(u, wa, wx, ba, bx, lam, bsz, t, ctx_len):
    L = LRU_TILE
    scratch = [pltpu.VMEM((2, L, 1024), F32), pltpu.VMEM((2, L, 1024), F32), pltpu.VMEM((8, 1024), F32)]
    return _bidir_scan(functools.partial(_lru_kernel, L), L, [u], [], [wa, wx, ba, bx, lam], [1024], scratch,
                       bsz, t, ctx_len, "rglru_scan")


def _swiglu_rows(x, is_ctx, modl_ref, modc_ref, g2_ref, g3_ref, w1_ref, w2_ref):
    h = _adaln(x, g2_ref[...], modl_ref, modc_ref, 3, is_ctx).astype(BF16)
    ff = w2_ref.shape[0]
    acc = jnp.zeros(x.shape, F32)
    for j in range(ff // FF_STEP):
        lo = j * FF_STEP
        gj = _dot(h, w1_ref[:, lo:lo + FF_STEP])
        uj = _dot(h, w1_ref[:, ff + lo:ff + lo + FF_STEP])
        acc = acc + _dot((_silu(gj) * uj).astype(BF16), w2_ref[lo:lo + FF_STEP, :])
    return x + _mod(modl_ref, modc_ref, 5, is_ctx) * _rms(acc, g3_ref[...])


def _out_tail(tm, ctx_len, a1, a2, x_ref, modl_ref, modc_ref, g_ref, w_ref, xo_ref, ffn_refs=None):
    is_ctx = False if ctx_len is None else _row_ids(pl.program_id(1), tm) < ctx_len
    half = w_ref.shape[0] // 2
    y = _dot(a1.astype(BF16), w_ref[0:half, :]) + _dot(a2.astype(BF16), w_ref[half:, :])
    x = x_ref[...] + _mod(modl_ref, modc_ref, 2, is_ctx) * _rms(y, g_ref[...])
    if ffn_refs is not None:
        x = _swiglu_rows(x, is_ctx, modl_ref, modc_ref, *ffn_refs)
    xo_ref[...] = x
    return x


def _out_ab_kernel(tm, ctx_len, hf, hb, og, lf, lb, glg, nw, x_ref, modl_ref, modc_ref, g_ref, w_ref,
                   g2_ref, g3_ref, w1_ref, w2_ref, xo_ref):
    hm = _head_rms(hf[...].astype(F32) + hb[...].astype(F32), nw[...], M_HEADS) * og[...].astype(F32)
    hl = (lf[...].astype(F32) + lb[...].astype(F32)) * glg[...].astype(F32)
    _out_tail(tm, ctx_len, hm, hl, x_ref, modl_ref, modc_ref, g_ref, w_ref, xo_ref,
              (g2_ref, g3_ref, w1_ref, w2_ref))


def _out_cd_kernel(tm, ctx_len, gf, gb, grs, yf, yb, sx, zs, gnw, dsk, snw, x_ref, modl_ref, modc_ref, g_ref,
                   w_ref, gr_ref, rw_ref, rb_ref, xo_ref, h_ref, idx_ref, gate_ref, cnt_ref, base_scr):
    og = _head_rms(gf[...].astype(F32) + gb[...].astype(F32), gnw[...], G_HEADS) * grs[...].astype(F32)
    ys = yf[...].astype(F32) + yb[...].astype(F32) + dsk[...] * sx[...].astype(F32)
    ys = _rms(ys * zs[...].astype(F32), snw[...])
    x = _out_tail(tm, ctx_len, og, ys, x_ref, modl_ref, modc_ref, g_ref, w_ref, xo_ref)
    _route_rows(x, modl_ref, gr_ref, rw_ref, rb_ref, h_ref, idx_ref, gate_ref, cnt_ref, base_scr)


def _out_proj(kern, name, acts, vecs, x2, modl, modc, g, w, bsz, t, ctx_len, extra=(), route_latent=False):
    if route_latent:
        tm = LATENT_TILE
        nt = (t - ctx_len) // tm
        row = lambda c: _latent_rows(tm, t, ctx_len, c)
        orow = lambda c: pl.BlockSpec((tm, c), lambda b, i: (b * nt + i, 0))
        nl = bsz * (t - ctx_len)
        out_specs = [orow(D_MODEL), orow(D_MODEL // 2), orow(LANES), orow(LANES),
                     pl.BlockSpec((8, LANES), lambda b, i: (0, 0))]
        out_shape = [jax.ShapeDtypeStruct((nl, D_MODEL), F32), jax.ShapeDtypeStruct((nl, D_MODEL // 2), jnp.uint32),
                     jax.ShapeDtypeStruct((nl, LANES), I32), jax.ShapeDtypeStruct((nl, LANES), F32),
                     jax.ShapeDtypeStruct((8, LANES), F32)]
        scratch = [pltpu.VMEM((8, LANES), F32)]
    else:
        tm = _big_row_tile(t)
        nt = t // tm
        row = lambda c: pl.BlockSpec((tm, c), lambda b, i: (b * nt + i, 0))
        out_specs, out_shape, scratch = row(D_MODEL), jax.ShapeDtypeStruct((bsz * t, D_MODEL), F32), []
    return pl.pallas_call(
        functools.partial(kern, tm, None if route_latent else ctx_len),
        grid=(bsz, nt),
        in_specs=[row(1024)] * len(acts) + [_resident((1, 1024))] * len(vecs)
        + [row(D_MODEL), pl.BlockSpec((1, 8, D_MODEL), lambda b, i: (b, 0, 0)),
           _resident((8, D_MODEL)), _resident((1, D_MODEL)), _resident(w.shape)]
        + [_resident(a.shape) for a in extra],
        out_specs=out_specs,
        out_shape=out_shape,
        scratch_shapes=scratch,
        compiler_params=_cparams("arbitrary", "arbitrary"),
        name=name,
    )(*acts, *vecs, x2, modl, modc, g, w, *extra)


def _in_cd_kernel(parts_per_batch, x_ref, *refs):
    modls, refs = refs[:IN_PARTS], refs[IN_PARTS:]
    modc_ref, g_ref, w_ref, cw_ref, cb_ref, dtb_ref, aneg_ref = refs[:7]
    gq_ref, gk_ref, gv_ref, gr_ref, z_ref, sx_ref, sb_ref, sc_ref, ga_ref, dts_ref, dts_t_ref, h_scr = refs[7:]
    lat = [_part_is_latent(p, parts_per_batch) for p in range(IN_PARTS)]
    for p in range(IN_PARTS):
        rows = slice(p * ROW_TILE, (p + 1) * ROW_TILE)
        h_scr[rows, :] = _adaln(x_ref[rows, :], g_ref[...], modls[p], modc_ref, 0,
                                jnp.logical_not(lat[p])).astype(BF16)

    def conv_silu(col0):
        return lambda y, c: _silu(_conv_parts(y, cw_ref[:, col0 + c:col0 + c + PROJ_COLS],
                                              cb_ref[:, col0 + c:col0 + c + PROJ_COLS], lat))

    _project(h_scr, w_ref, 0, gq_ref, lambda y, c: y * (G_DK ** -0.5))
    _project(h_scr, w_ref, 512, gk_ref, lambda y, c: y)
    _project(h_scr, w_ref, 1024, gv_ref, lambda y, c: y)
    _project(h_scr, w_ref, 2048, gr_ref, lambda y, c: _silu(y))
    _project(h_scr, w_ref, 3072, z_ref, lambda y, c: _silu(y))
    _project(h_scr, w_ref, 4096, sx_ref, conv_silu(0))
    _project(h_scr, w_ref, 5120, sb_ref, conv_silu(1024))
    _project(h_scr, w_ref, 5376, sc_ref, conv_silu(1280))
    ga_ref[...] = _dot(h_scr[...], w_ref[:, 5632:5760])
    dt = _softplus(_dot(h_scr[...], w_ref[:, 5760:5888]) + dtb_ref[...])
    lane = lax.broadcasted_iota(I32, dt.shape, 1)
    dts = jnp.where(lane < 2 * S_HEADS, dt, dt * aneg_ref[...])
    dts_ref[...] = dts
    dts_t_ref[...] = dts.T


def _in_cd(x2, modl, modc, g, w, cw, cb, dtb, aneg, bsz, t, ctx_len):
    n = bsz * t
    tm = IN_PARTS * ROW_TILE
    ppb = t // ROW_TILE
    row = lambda c: pl.BlockSpec((tm, c), lambda j: (j, 0))
    mod_specs = [pl.BlockSpec((1, 8, D_MODEL), lambda j, p=p: ((j * IN_PARTS + p) // ppb, 0, 0))
                 for p in range(IN_PARTS)]
    outs = [(512, BF16), (512, BF16), (1024, BF16), (1024, BF16), (1024, BF16), (1024, BF16), (256, BF16),
            (256, BF16), (LANES, F32), (LANES, F32)]
    params = (modc, g, w, cw, cb, dtb, aneg)
    return pl.pallas_call(
        functools.partial(_in_cd_kernel, ppb),
        grid=(n // tm,),
        in_specs=[row(D_MODEL)] + mod_specs + [_resident(a.shape) for a in params],
        out_specs=[row(c) for c, _ in outs] + [pl.BlockSpec((LANES, tm), lambda j: (0, j))],
        out_shape=[jax.ShapeDtypeStruct((n, c), dt) for c, dt in outs] + [jax.ShapeDtypeStruct((LANES, n), F32)],
        scratch_shapes=[pltpu.VMEM((tm, D_MODEL), BF16)],
        compiler_params=_cparams("arbitrary"),
        name="in_proj_cd",
    )(x2, *([modl] * IN_PARTS), *params)


def _gla_kernel(L, C, qf, kf, vf, gaf, qb, kb, vb, gab, awh_ref, awl_ref, ab_ref, of, ob, s_scr, att_scr):
    _zero_at_start(s_scr, att_scr)
    nb = L // C
    chains = []
    for d, (q_ref, k_ref, v_ref, ga_ref, out_ref) in enumerate(((qf, kf, vf, gaf, of), (qb, kb, vb, gab, ob))):
        reverse = d == 1
        mask = _tri_mask(L, reverse)
        g1, g2, _ = _split3(ga_ref[...])
        pre = _dot(g1, awh_ref[d]) + _dot(g2, awh_ref[d]) + _dot(g1, awl_ref[d])
        lg = _log_sigmoid(pre + ab_ref[d]) * (1.0 / G_TAU)
        ball = _sel_cols(mask.astype(BF16), lg)
        for h in range(G_HEADS):
            ks = slice(h * G_DK, (h + 1) * G_DK)
            b = ball[:, ks] * LOG2E
            qh = q_ref[:, ks].astype(F32)
            inter = _dot_nt((qh * jnp.exp2(b)).astype(BF16), s_scr[d, h].astype(BF16))
            chains.append(dict(d=d, h=h, reverse=reverse, mask=mask, b=b, qh=qh, kh=k_ref[:, ks].astype(F32),
                               vh=v_ref[:, h * G_DV:(h + 1) * G_DV], inter=inter, out_ref=out_ref, blocks=[]))
    T = min(G_GROUP, L)
    for i in range(nb):
        lo, hi = i * C, (i + 1) * C
        t0 = lo // T * T
        for c, ch in enumerate(chains):
            b = ch["b"]
            if ch["reverse"]:
                bref = b[hi:hi + 1, :] if i < nb - 1 else jnp.zeros((1, G_DK), F32)
            else:
                bref = b[lo - 1:lo, :] if i > 0 else jnp.zeros((1, G_DK), F32)
            qi = (ch["qh"][lo:hi, :] * jnp.exp2(b[lo:hi, :] - bref)).astype(BF16)
            ki = (ch["kh"][t0:t0 + T] * jnp.exp2(bref - b[t0:t0 + T])).astype(BF16)
            att_scr[c, lo:hi, t0:t0 + T] = _dot_nt(qi, ki)
    for g in range(L // T):
        t0, t1 = g * T, (g + 1) * T
        for c, ch in enumerate(chains):
            b = ch["b"]
            e0, e1 = (t1, L) if ch["reverse"] else (0, t0)
            if e1 > e0:
                bref = b[t1:t1 + 1, :] if ch["reverse"] else b[t0 - 1:t0, :]
                qg = (ch["qh"][t0:t1] * jnp.exp2(b[t0:t1] - bref)).astype(BF16)
                kg = (ch["kh"][e0:e1] * jnp.exp2(bref - b[e0:e1])).astype(BF16)
                att_scr[c, t0:t1, e0:e1] = _dot_nt(qg, kg)
    for c, ch in enumerate(chains):
        att = jnp.where(ch["mask"], att_scr[c], 0.0).astype(BF16)
        oh = ch["inter"] + _dot(att, ch["vh"])
        ch["out_ref"][:, ch["h"] * G_DV:(ch["h"] + 1) * G_DV] = oh.astype(BF16)
    for ch in chains:
        d, h, b = ch["d"], ch["h"], ch["b"]
        last = 0 if ch["reverse"] else L - 1
        bl = b[last:last + 1, :]
        kd = (ch["kh"] * jnp.exp2(bl - b)).astype(BF16)
        s_scr[d, h] = s_scr[d, h] * jnp.exp2(bl) + _dot_tn(ch["vh"], kd)


def _gla(q, k, v, ga, awh, awl, ab, bsz, t, ctx_len):
    L, C = G_CHUNK, G_SUB
    scratch = [pltpu.VMEM((2, G_HEADS, G_DV, G_DK), F32), pltpu.VMEM((2 * G_HEADS, L, L), F32)]
    return _bidir_scan(functools.partial(_gla_kernel, L, C), L, [q, k, v, ga], [], [awh, awl, ab], [1024], scratch,
                       bsz, t, ctx_len, "gla_scan")


def _ssd_kernel(L, xf, bf, cf, gcf, grf, xb, bb, cb_, gcb, grb, of, ob, s_scr):
    _zero_at_start(s_scr)
    er = lax.broadcasted_iota(I32, (LANES, S_HEADS * S_P), 0)
    ec = jnp.right_shift(lax.broadcasted_iota(I32, (LANES, S_HEADS * S_P), 1), S_P.bit_length() - 1)
    glane = lax.broadcasted_iota(I32, (L, LANES), 1)
    lane = lax.broadcasted_iota(I32, (L, 2 * S_P), 1)
    dirs, chains = [], []
    for d, (x_ref, b_ref, c_ref, gc_ref, gr_ref, out_ref) in enumerate(((xf, bf, cf, gcf, grf, of),
                                                                         (xb, bb, cb_, gcb, grb, ob))):
        reverse = d == 1
        mask = _tri_mask(L, reverse)
        mask01 = mask.astype(BF16)
        gc = gc_ref[...]
        gr = gr_ref[...]
        bcol = _sel_cols(mask01, gc)
        brow = _sel_rows(gr, mask01)
        last = 0 if reverse else L - 1
        dt_o, la_o = S_HEADS * d, 2 * S_HEADS + S_HEADS * d
        sel_la = (er == ec + la_o).astype(BF16)
        x = x_ref[...]
        bla = jnp.where((glane >= la_o) & (glane < la_o + S_HEADS), bcol, 0.0)
        dt_at_la = pltpu.roll(gc, 2 * S_HEADS, 1)
        bl = bla[last:last + 1, :]
        ebx = _dot(jnp.exp(bla).astype(BF16), sel_la)
        wx = _dot((jnp.exp(bl - bla) * dt_at_la).astype(BF16), sel_la)
        dirs.append(dict(d=d, mask=mask, gr=gr, x=x, ebx=ebx, last=last, dt_o=dt_o, la_o=la_o, out_ref=out_ref,
                         xw=(x.astype(F32) * wx).astype(BF16),
                         bcol2=bcol * LOG2E, brow2=brow * LOG2E))
        for g in range(S_GROUPS):
            ns = slice(g * S_N, (g + 1) * S_N)
            chains.append(dict(dr=dirs[-1], g=g, bg=b_ref[:, ns], cg=c_ref[:, ns],
                               gs=slice(g * S_HPG * S_P, (g + 1) * S_HPG * S_P)))
    for ch in chains:
        dr = ch["dr"]
        ch["cb"] = _dot_nt(ch["cg"], ch["bg"])
        ch["inter"] = _dot(ch["cg"], s_scr[dr["d"], ch["g"]].astype(BF16)) * dr["ebx"][:, ch["gs"]]
    for ch in chains:
        dr, g = ch["dr"], ch["g"]
        parts = []
        for pr in range(S_HPG // 2):
            e0 = g * S_HPG + 2 * pr
            xp = dr["x"][:, e0 * S_P:(e0 + 2) * S_P]
            res = []
            for e in (e0, e0 + 1):
                col = dr["la_o"] + e
                bc, br = dr["bcol2"][:, col:col + 1], dr["brow2"][col:col + 1, :]
                dtr = dr["gr"][dr["dt_o"] + e:dr["dt_o"] + e + 1, :]
                att = (ch["cb"] * dtr * jnp.exp2(jnp.where(dr["mask"], bc - br, -jnp.inf))).astype(BF16)
                res.append(_dot(att, xp))
            parts.append(jnp.where(lane < S_P, res[0], res[1]))
        dr["out_ref"][:, ch["gs"]] = (jnp.concatenate(parts, axis=1) + ch["inter"]).astype(BF16)
    for ch in chains:
        dr, g, gs = ch["dr"], ch["g"], ch["gs"]
        s_scr[dr["d"], g] = (s_scr[dr["d"], g] * dr["ebx"][dr["last"]:dr["last"] + 1, gs]
                             + _dot_tn(ch["bg"], dr["xw"][:, gs]))


def _ssd(x, bm, cm, gcol, grow, bsz, t, ctx_len):
    L = S_CHUNK
    scratch = [pltpu.VMEM((2, S_GROUPS, S_N, S_HPG * S_P), F32)]
    return _bidir_scan(functools.partial(_ssd_kernel, L), L, [x, bm, cm, gcol], [(grow, 4 * S_HEADS)], [], [1024],
                       scratch,
                       bsz, t, ctx_len, "ssd_scan")


def _route_rows(x, modl_ref, g_ref, rw_ref, rb_ref, h_ref, idx_ref, gate_ref, cnt_ref, base_scr):
    @pl.when((pl.program_id(0) == 0) & (pl.program_id(1) == 0))
    def _():
        base_scr[...] = jnp.zeros_like(base_scr)

    h = _rms(x, g_ref[...]) * (1.0 + modl_ref[0, 4:5, :]) + modl_ref[0, 3:4, :]
    h_ref[...] = _pack_rows(h)
    h1, h2, _ = _split3(h)
    both = _dot(h1, rw_ref[...])
    logits = both[:, :LANES] + both[:, LANES:] + _dot(h2, rw_ref[:, :LANES]) + rb_ref[...]
    lane = lax.broadcasted_iota(I32, logits.shape, 1)
    logits = jnp.where(lane < N_EXPERTS, logits, -jnp.inf)
    lanef = lane.astype(F32)
    m1 = jnp.max(logits, axis=1, keepdims=True)
    i1 = jnp.min(jnp.where(logits == m1, lanef, float(LANES)), axis=1, keepdims=True)
    rest = jnp.where(lanef == i1, -jnp.inf, logits)
    m2 = jnp.max(rest, axis=1, keepdims=True)
    i2 = jnp.min(jnp.where(rest == m2, lanef, float(LANES)), axis=1, keepdims=True)
    e = jnp.exp(m2 - m1)
    g1 = 1.0 / (1.0 + e)
    gate_ref[...] = jnp.where(lane == 0, g1, jnp.where(lane == 1, e * g1, 0.0))
    tm = logits.shape[0]
    oh1, oh2 = (lanef == i1).astype(F32), (lanef == i2).astype(F32)
    earlier = (lax.broadcasted_iota(I32, (tm, tm), 1) < lax.broadcasted_iota(I32, (tm, tm), 0)).astype(BF16)
    seen = base_scr[0:1, :] + _dot(earlier, (oh1 + oh2).astype(BF16))
    r1 = jnp.sum(oh1 * seen, axis=1, keepdims=True)
    r2 = jnp.sum(oh2 * seen, axis=1, keepdims=True)
    base_scr[0:1, :] = base_scr[0:1, :] + jnp.sum(oh1 + oh2, axis=0, keepdims=True)
    cnt_ref[...] = base_scr[...]
    idx_ref[...] = jnp.where(lane == 0, i1, jnp.where(lane == 1, i2, jnp.where(lane == 2, r1, jnp.where(
        lane == 3, r2, 0.0)))).astype(I32)


def _latent_rows(tm, t, ctx_len, cols=D_MODEL):
    return pl.BlockSpec((pl.Element(tm), pl.Element(cols)),
                        lambda b, i: (pl.multiple_of(b * t + ctx_len + i * tm, SUBLANES), 0))


def _gather_rows(table, idx):
    n_rows, d = idx.shape[0], table.shape[1]
    n_workers = SC_CORES * SC_SUBCORES
    per_worker = n_rows // n_workers
    chunk = SC_GATHER_ROWS
    assert per_worker * n_workers == n_rows and per_worker % chunk == 0
    mesh = plsc.VectorSubcoreMesh(core_axis_name="c", subcore_axis_name="s")

    @functools.partial(
        pl.kernel, mesh=mesh, out_type=jax.ShapeDtypeStruct((n_rows, d), table.dtype),
        scratch_types=[pltpu.VMEM((chunk,), I32), pltpu.VMEM((chunk, d), table.dtype), pltpu.SemaphoreType.DMA],
        name="sc_gather_rows")
    def gather(table_hbm, idx_hbm, out_hbm, idx_v, rows_v, sem):
        base = (lax.axis_index("s") * SC_CORES + lax.axis_index("c")) * per_worker

        @pl.loop(0, per_worker // chunk)
        def _(j):
            off = pl.multiple_of(base + j * chunk, 8)
            pltpu.sync_copy(idx_hbm.at[pl.ds(off, chunk)], idx_v)
            pltpu.async_copy(table_hbm.at[idx_v], rows_v, sem).wait()
            pltpu.sync_copy(rows_v, out_hbm.at[pl.ds(off, chunk)])

    return gather(table, idx)


def _dispatch_rows(table, slot, n_out):
    assert TOP_K == 2
    n_assign, (n_tok, d) = slot.shape[0], table.shape
    n_workers = SC_CORES * SC_SUBCORES
    per_worker = n_out // n_workers
    chunk, vec = SC_GATHER_ROWS, SC_LANES
    assert per_worker * n_workers == n_out and per_worker % chunk == 0 and n_assign % vec == 0
    assert n_out < 3 * n_tok
    mesh = plsc.VectorSubcoreMesh(core_axis_name="c", subcore_axis_name="s")

    @functools.partial(
        pl.kernel, mesh=mesh, out_type=jax.ShapeDtypeStruct((n_out, d), table.dtype),
        scratch_types=[pltpu.VMEM((n_assign,), I32), pltpu.VMEM((per_worker,), I32),
                       pltpu.VMEM((chunk, d), table.dtype), pltpu.SemaphoreType.DMA],
        compiler_params=pltpu.CompilerParams(needs_layout_passes=False),
        name="sc_dispatch_rows")
    def dispatch(table_hbm, slot_hbm, out_hbm, slots_v, tok_v, rows_v, sem):
        base = (lax.axis_index("s") * SC_CORES + lax.axis_index("c")) * per_worker
        pltpu.sync_copy(slot_hbm, slots_v)
        lane = lax.broadcasted_iota(I32, (vec,), 0)

        @pl.loop(0, per_worker // vec)
        def _(j):
            p = base + j * vec + lane
            p = jnp.where(p >= n_tok, p - n_tok, p)
            tok_v[pl.ds(j * vec, vec)] = jnp.where(p >= n_tok, p - n_tok, p)

        @pl.loop(0, n_assign // vec)
        def _(j):
            s = slots_v[pl.ds(j * vec, vec)] - base
            mine = (s >= 0) & (s < per_worker)
            tok = lax.shift_right_logical(j * vec + lane, 1)
            plsc.store_scatter(tok_v, [jnp.where(mine, s, 0)], tok, mask=mine)

        @pl.loop(0, per_worker // chunk)
        def _(j):
            off = pl.multiple_of(j * chunk, 8)
            pltpu.async_copy(table_hbm.at[tok_v.at[pl.ds(off, chunk)]], rows_v, sem).wait()
            pltpu.sync_copy(rows_v, out_hbm.at[pl.ds(base + off, chunk)])

    return dispatch(table, slot)


def _moe_kernel(be_ref, nu_ref, nv_ref, x_ref, w1g_ref, w1u_ref, w2_ref, y_ref, xb_ref, acc_ref):
    i, j = pl.program_id(0), pl.program_id(1)
    bm = x_ref.shape[0]

    def swiglu_rows(rows):
        x = xb_ref[0:rows, :]
        a = (_silu(_dot(x, w1g_ref[0].astype(BF16))) * _dot(x, w1u_ref[0].astype(BF16))).astype(BF16)
        acc_ref[0:rows, :] += _dot(a, w2_ref[0].astype(BF16))

    @pl.when(i < nu_ref[0])
    def _():
        @pl.when(j == 0)
        def _():
            acc_ref[...] = jnp.zeros_like(acc_ref)
            xb_ref[...] = _unpack_rows(x_ref[...]).astype(BF16)

        for rows in range(MOE_ROW_STEP, bm + 1, MOE_ROW_STEP):
            @pl.when((nv_ref[i] > rows - MOE_ROW_STEP) & (nv_ref[i] <= rows))
            def _(rows=rows):
                swiglu_rows(rows)

        @pl.when(j == pl.num_programs(1) - 1)
        def _():
            y_ref[...] = _pack_rows(acc_ref[...])


def _moe_experts(xs, w1, w2, block_e, n_used, n_valid, n_blocks):
    bm, fc = MOE_ROWS, MOE_FF
    nff = FF_EXPERT // fc
    used = lambda i, nu: jnp.minimum(i, nu[0] - 1)
    ffi = lambda i, j, nu: jnp.where(i < nu[0], j, nff - 1)
    grid_spec = pltpu.PrefetchScalarGridSpec(
        num_scalar_prefetch=3,
        grid=(n_blocks, nff),
        in_specs=[pl.BlockSpec((bm, D_MODEL // 2), lambda i, j, be, nu, nv: (used(i, nu), 0)),
                  pl.BlockSpec((1, D_MODEL, fc), lambda i, j, be, nu, nv: (be[used(i, nu)], 0, ffi(i, j, nu))),
                  pl.BlockSpec((1, D_MODEL, fc),
                               lambda i, j, be, nu, nv: (be[used(i, nu)], 0, nff + ffi(i, j, nu))),
                  pl.BlockSpec((1, fc, D_MODEL), lambda i, j, be, nu, nv: (be[used(i, nu)], ffi(i, j, nu), 0))],
        out_specs=pl.BlockSpec((bm, D_MODEL // 2), lambda i, j, be, nu, nv: (used(i, nu), 0)),
        scratch_shapes=[pltpu.VMEM((bm, D_MODEL), BF16), pltpu.VMEM((bm, D_MODEL), F32)],
    )
    return pl.pallas_call(
        _moe_kernel,
        grid_spec=grid_spec,
        out_shape=jax.ShapeDtypeStruct((n_blocks * bm, D_MODEL // 2), jnp.uint32),
        compiler_params=_cparams("arbitrary", "arbitrary"),
        name="moe_experts",
    )(block_e, n_used, n_valid, xs, w1, w1, w2)


def _combine_kernel(y0_ref, y1_ref, gate_ref, x_ref, modl_ref, g_ref, o_ref):
    gt = gate_ref[...]
    y = gt[:, 0:1] * _unpack_rows(y0_ref[...]) + gt[:, 1:2] * _unpack_rows(y1_ref[...])
    o_ref[...] = x_ref[...] + modl_ref[0, 5:6, :] * _rms(y, g_ref[...])


def _combine(yk, gates, x2, modl, g, bsz, t, ctx_len):
    tm = LATENT_TILE
    nlt = (t - ctx_len) // tm
    nl = bsz * (t - ctx_len)
    orow = lambda c: pl.BlockSpec((tm, c), lambda b, i: (b * nlt + i, 0))
    second = pl.BlockSpec((tm, D_MODEL // 2), lambda b, i: (nl // tm + b * nlt + i, 0))
    return pl.pallas_call(
        _combine_kernel,
        grid=(bsz, nlt),
        in_specs=[orow(D_MODEL // 2), second, orow(LANES),
                  _latent_rows(tm, t, ctx_len),
                  pl.BlockSpec((1, 8, D_MODEL), lambda b, i: (b, 0, 0)), _resident((1, D_MODEL))],
        out_specs=orow(D_MODEL),
        out_shape=jax.ShapeDtypeStruct((nl, D_MODEL), F32),
        compiler_params=_cparams("arbitrary", "arbitrary"),
        name="moe_combine",
    )(yk, yk, gates, x2, modl, g)


def _routing_tables(experts, ranks, counts):
    bm = MOE_ROWS
    n_assign = experts.size
    padded = (counts + bm - 1) // bm * bm
    ends_p = jnp.cumsum(padded)
    pstarts = ends_p - padded
    onehot = experts[..., None] == jnp.arange(N_EXPERTS, dtype=I32)
    slot = (jnp.sum(jnp.where(onehot, pstarts, 0), axis=-1) + ranks).astype(I32)
    n_blocks = n_assign // bm + N_EXPERTS
    block_e = jnp.minimum(jnp.sum(ends_p[None, :] <= (jnp.arange(n_blocks, dtype=I32) * bm)[:, None], axis=1),
                          N_EXPERTS - 1).astype(I32)
    n_used = (ends_p[-1] // bm).astype(I32).reshape(1)
    sel = block_e[:, None] == jnp.arange(N_EXPERTS, dtype=I32)
    rows_left = jnp.sum(jnp.where(sel, counts + pstarts, 0), axis=1) - jnp.arange(n_blocks, dtype=I32) * bm
    n_valid = jnp.clip(rows_left, 0, bm).astype(I32)
    return slot, block_e, n_used, n_valid, n_blocks


def kernel(x, c, ctx, c_ctx, mod_w, mod_b, norm_g, ab_in_w, m_conv_w, m_conv_b, m_gate_b, m_norm_w, l_conv_w,
           l_conv_b, l_wa, l_ba, l_wx, l_bx, l_lam, ab_out_w, ffn_w1, ffn_w2, cd_in_w, g_alpha_w, g_alpha_b,
           g_norm_w, s_conv_w, s_conv_b, s_dt_bias, s_A_log, s_D, s_norm_w, cd_out_w, router_w, router_b,
           moe_w1, moe_w2):
    bsz, seq, dm = x.shape
    ctx_len = ctx.shape[1]
    t = ctx_len + seq
    n = bsz * t
    assert ctx_len == ROW_TILE and seq % ROW_TILE == 0 and ROW_TILE % GRID_W == 0
    assert n % (IN_PARTS * ROW_TILE) == 0

    c_all = jnp.concatenate([c, c_ctx[None, :], jnp.zeros((-(bsz + 1) % 8, dm), F32)], axis=0)
    mods = _modulation(c_all, mod_w, mod_b).reshape(mod_w.shape[0], c_all.shape[0], 6, dm)
    pad2 = jnp.zeros((bsz, 2, dm), F32)

    def layer_mods(layer):
        modl = jnp.concatenate([mods[layer, :bsz], pad2], axis=1)
        modc = jnp.concatenate([mods[layer, bsz], pad2[0]], axis=0)
        return modl, modc

    modl, modc = layer_mods(0)
    g = norm_g[0]
    w = ab_in_w[0]
    o_qk, o_v, o_o, o_gt, o_lx, o_lg = 0, 1024, 2048, 3072, 3088, 4112
    w_in = jnp.concatenate([w[:, o_qk:o_v], w[:, o_v:o_o], w[:, o_o:o_gt], w[:, o_lx:o_lg], w[:, o_lg:],
                            w[:, o_gt:o_lx], jnp.zeros((dm, LANES - 16), F32)], axis=1).astype(BF16)
    gb = jnp.concatenate([m_gate_b[0].reshape(1, 16), jnp.zeros((1, LANES - 16), F32)], axis=1)
    x2, q, k, v, og, lx, glg, gates, gates_t = _in_ab(
        ctx.reshape(bsz * ctx_len, dm), x.reshape(bsz * seq, dm), modl, modc, g[0:1], w_in, m_conv_w[0],
        m_conv_b[0][None], l_conv_w[0], l_conv_b[0][None], gb, bsz, t, ctx_len)
    hm_f, hm_b = _mlstm(q, k, v, gates, gates_t, bsz, t, ctx_len)
    hl_f, hl_b = _lru(lx, l_wa[0].astype(BF16), l_wx[0].astype(BF16), l_ba[0][:, None], l_bx[0][:, None],
                      l_lam[0][:, None], bsz, t, ctx_len)
    x2 = _out_proj(_out_ab_kernel, "out_proj_swiglu_ab", [hm_f, hm_b, og, hl_f, hl_b, glg], [m_norm_w[0][None]],
                   x2, modl, modc, g[1:2], ab_out_w[0].astype(BF16), bsz, t, ctx_len,
                   extra=(g[2:3], g[3:4], ffn_w1[0].astype(BF16), ffn_w2[0].astype(BF16)))

    modl, modc = layer_mods(1)
    g = norm_g[1]
    w = cd_in_w[0]
    o_gv, o_gr, o_ga, o_z, o_xbc, o_dt = 1024, 2048, 3072, 3104, 4128, 5664
    zpad = jnp.zeros((dm, LANES - 32), F32)
    w_in = jnp.concatenate([w[:, :o_gv], w[:, o_gv:o_gr], w[:, o_gr:o_ga], w[:, o_z:o_xbc], w[:, o_xbc:o_dt],
                            w[:, o_ga:o_z], zpad, w[:, o_dt:], w[:, o_dt:], jnp.zeros((dm, LANES - 64), F32)],
                           axis=1).astype(BF16)
    dtb = jnp.concatenate([s_dt_bias[0].reshape(1, 32), s_dt_bias[0].reshape(1, 32),
                           jnp.zeros((1, LANES - 64), F32)], axis=1)
    aneg = jnp.concatenate([jnp.zeros((1, 32), F32), -jnp.exp(s_A_log[0].reshape(1, 32)),
                            jnp.zeros((1, LANES - 64), F32)], axis=1)
    gq, gk, gv, grs, zs, sx, sb, sc, ga, dts, dts_t = _in_cd(x2, modl, modc, g[0:1], w_in, s_conv_w[0],
                                                             s_conv_b[0][None], dtb, aneg, bsz, t, ctx_len)
    aw = jnp.zeros((2, LANES, G_HEADS * G_DK), F32)
    aw = aw.at[0, 0:G_RANK].set(g_alpha_w[0, 0]).at[1, G_RANK:2 * G_RANK].set(g_alpha_w[0, 1])
    awh = aw.astype(BF16)
    awl = (aw - awh.astype(F32)).astype(BF16)
    og_f, og_b = _gla(gq, gk, gv, ga, awh, awl, g_alpha_b[0][:, None], bsz, t, ctx_len)
    ys_f, ys_b = _ssd(sx, sb, sc, dts, dts_t, bsz, t, ctx_len)
    dskip = jnp.repeat(s_D[0], S_P)[None]
    rw = jnp.concatenate([router_w[0], jnp.zeros((dm, LANES - N_EXPERTS), F32)], axis=1)
    rwh = rw.astype(BF16)
    rw = jnp.concatenate([rwh, (rw - rwh.astype(F32)).astype(BF16)], axis=1)
    rb = jnp.concatenate([router_b[0][None], jnp.zeros((1, LANES - N_EXPERTS), F32)], axis=1)
    x2, h2, top_idx, top_gate, counts = _out_proj(
        _out_cd_kernel, "out_proj_router_cd", [og_f, og_b, grs, ys_f, ys_b, sx, zs],
        [g_norm_w[0][None], dskip, s_norm_w[0][None]],
        x2, modl, modc, g[1:2], cd_out_w[0].astype(BF16), bsz, t, ctx_len, extra=(g[2:3], rw, rb), route_latent=True)
    t, ctx_len = seq, 0
    slot, block_e, n_used, n_valid, n_blocks = _routing_tables(top_idx[:, 0:TOP_K], top_idx[:, TOP_K:2 * TOP_K],
                                                               counts[0, :N_EXPERTS].astype(I32))
    xs = _dispatch_rows(h2, slot.reshape(-1), n_blocks * MOE_ROWS)
    ys_e = _moe_experts(xs, moe_w1[0], moe_w2[0], block_e, n_used, n_valid, n_blocks)
    yk = _gather_rows(ys_e, jnp.concatenate([slot[:, 0], slot[:, 1]]))
    out = _combine(yk, top_gate, x2, modl, g[3:4], bsz, t, ctx_len)
    return out.reshape(bsz, seq, dm)
```
